```python
import jax, jax.numpy as jnp
from jax import lax
import numpy as np


D_MODEL = 1024
BATCH = 8
SEQ = 4096
DEPTH = 2

CHUNK = 64
N_BRANCH = 3
BRANCH_WIDTH = D_MODEL // 2
POOL_WINDOWS = (2, 4, 8, 16)
N_POOL_GROUPS = len(POOL_WINDOWS)
POOL_GROUP = BRANCH_WIDTH // N_POOL_GROUPS
CONV_K = 3
SB_HEAD_DIM = 64
SB_HEADS = BRANCH_WIDTH // SB_HEAD_DIM
Q_BLOCK = 128
RMS_EPS = 1e-6
IN_SIZES = (BRANCH_WIDTH,) * 10 + (N_BRANCH * D_MODEL,)
N_IN = sum(IN_SIZES)

kernel_name = "hybrid_pool_conv_stickbreak_block"


def _split_points():
    return [int(p) for p in np.cumsum(IN_SIZES)[:-1]]


def rms_norm(x, g):
    xf = x.astype(jnp.float32)
    y = xf * lax.rsqrt(jnp.mean(xf * xf, axis=-1, keepdims=True) + RMS_EPS)
    return (y * g.astype(jnp.float32)).astype(x.dtype)


def pool_mixer(v, w_group, scale):
    b, s, _ = v.shape
    vg = v.astype(jnp.float32).reshape(b, s, N_POOL_GROUPS, POOL_GROUP)
    csum = jnp.cumsum(vg, axis=1)
    pos = jnp.arange(s)
    outs = []
    for gi, w in enumerate(POOL_WINDOWS):
        c = csum[:, :, gi]
        lag = jnp.pad(c[:, :s - w], ((0, 0), (w, 0), (0, 0)))
        cnt = jnp.minimum(pos + 1, w).astype(jnp.float32)[None, :, None]
        outs.append((c - lag) / cnt - vg[:, :, gi])
    pooled = jnp.stack(outs, axis=2).astype(v.dtype)
    mixed = jnp.einsum('bsgc,gcd->bsgd', pooled, w_group)
    return mixed.reshape(b, s, BRANCH_WIDTH) * scale


def conv_mixer(xc, gate_b, gate_c, w, bias):
    z = gate_c * xc
    y = lax.conv_general_dilated(
        z, w[:, None, :].astype(z.dtype), window_strides=(1,), padding=[(CONV_K - 1, 0)],
        dimension_numbers=('NWC', 'WIO', 'NWC'), feature_group_count=BRANCH_WIDTH)
    return gate_b * (y + bias)


def stick_breaking_attention(q, k, v):
    b, s, _ = q.shape
    nblk = s // Q_BLOCK
    qb = q.reshape(b, nblk, Q_BLOCK, SB_HEADS, SB_HEAD_DIM).transpose(1, 0, 2, 3, 4)
    kf = k.reshape(b, s, SB_HEADS, SB_HEAD_DIM).astype(jnp.float32)
    vf = v.reshape(b, s, SB_HEADS, SB_HEAD_DIM).astype(jnp.float32)
    key_pos = jnp.arange(s)
    scale = SB_HEAD_DIM ** -0.5

    def block(args):
        qi, i = args
        logits = jnp.einsum('bqhd,bkhd->bhqk', qi.astype(jnp.float32), kf) * scale
        q_pos = i * Q_BLOCK + jnp.arange(Q_BLOCK)
        mask = key_pos[None, :] < q_pos[:, None]
        log_keep = jnp.where(mask, jax.nn.log_sigmoid(-logits), 0.0)
        later = lax.cumsum(log_keep, axis=3, reverse=True) - log_keep
        weights = jnp.where(mask, jnp.exp(jax.nn.log_sigmoid(logits) + later), 0.0)
        return jnp.einsum('bhqk,bkhd->bqhd', weights, vf)

    out = lax.map(block, (qb, jnp.arange(nblk)))
    return out.transpose(1, 0, 2, 3, 4).reshape(b, s, BRANCH_WIDTH).astype(q.dtype)


def hybrid_layer(x, g_pre, w_in, pool_w, pool_scale, conv_w, conv_b, w_branch, w_out, g_post):
    b, s, _ = x.shape
    h = rms_norm(x, g_pre)
    u = jnp.einsum('bsd,dn->bsn', h, w_in)
    (pool_v, pool_g, conv_x, conv_gb, conv_gc, conv_g,
     sb_q, sb_k, sb_v, sb_g, merge) = jnp.split(u, _split_points(), axis=-1)
    y_pool = pool_mixer(pool_v, pool_w, pool_scale) * jax.nn.silu(pool_g)
    y_conv = conv_mixer(conv_x, conv_gb, conv_gc, conv_w, conv_b) * jax.nn.silu(conv_g)
    y_sb = stick_breaking_attention(sb_q, sb_k, sb_v) * jax.nn.silu(sb_g)
    branches = jnp.stack([y_pool, y_conv, y_sb], axis=2)
    proj = jnp.einsum('bsnw,nwd->bsnd', branches, w_branch)
    gates = jax.nn.sigmoid(merge.reshape(b, s, N_BRANCH, D_MODEL))
    merged = jnp.sum(gates * proj, axis=2)
    out = jnp.einsum('bsd,de->bse', merged, w_out)
    return x + rms_norm(out, g_post)


def _fwd_setup_inputs(seed: int = 0) -> dict:
    key = jax.random.key(seed)
    ks = jax.random.split(key, 10)
    f32 = jnp.float32
    x = jax.random.normal(ks[0], (BATCH, SEQ, D_MODEL), f32)
    pre_norm_g = 1.0 + 0.05 * jax.random.normal(ks[1], (DEPTH, D_MODEL), f32)
    w_in = jax.random.normal(ks[2], (DEPTH, D_MODEL, N_IN), f32) * D_MODEL ** -0.5
    pool_w = jax.random.normal(ks[3], (DEPTH, N_POOL_GROUPS, POOL_GROUP, POOL_GROUP), f32) * POOL_GROUP ** -0.5
    pool_scale = 1.0 + 0.1 * jax.random.normal(ks[4], (DEPTH, BRANCH_WIDTH), f32)
    conv_w = jax.random.normal(ks[5], (DEPTH, CONV_K, BRANCH_WIDTH), f32) * CONV_K ** -0.5
    conv_b = 0.01 * jax.random.normal(ks[6], (DEPTH, BRANCH_WIDTH), f32)
    w_branch = jax.random.normal(ks[7], (DEPTH, N_BRANCH, BRANCH_WIDTH, D_MODEL), f32) * BRANCH_WIDTH ** -0.5
    w_out = jax.random.normal(ks[8], (DEPTH, D_MODEL, D_MODEL), f32) * D_MODEL ** -0.5
    post_norm_g = 1.0 + 0.05 * jax.random.normal(ks[9], (DEPTH, D_MODEL), f32)
    return {"x": x, "pre_norm_g": pre_norm_g, "w_in": w_in, "pool_w": pool_w,
            "pool_scale": pool_scale, "conv_w": conv_w, "conv_b": conv_b,
            "w_branch": w_branch, "w_out": w_out, "post_norm_g": post_norm_g}


def _fwd_reference(x, pre_norm_g, w_in, pool_w, pool_scale, conv_w, conv_b, w_branch, w_out, post_norm_g):
    for l in range(DEPTH):
        x = hybrid_layer(x, pre_norm_g[l], w_in[l], pool_w[l], pool_scale[l], conv_w[l],
                         conv_b[l], w_branch[l], w_out[l], post_norm_g[l])
    return x


import jax as _jax
import jax.numpy as _jnp

TWIN_FORMAT = 'train_step'
FWD_PARAMS = ['x', 'pre_norm_g', 'w_in', 'pool_w', 'pool_scale', 'conv_w', 'conv_b', 'w_branch', 'w_out', 'post_norm_g']
TWIN_WEIGHTS = ['pre_norm_g', 'w_in', 'pool_w', 'pool_scale', 'conv_w', 'conv_b', 'w_branch', 'w_out', 'post_norm_g']
TWIN_DIFF_INPUT = 'x'
TWIN_INPUTS = ['x', 'pre_norm_g', 'w_in', 'pool_w', 'pool_scale', 'conv_w', 'conv_b', 'w_branch', 'w_out', 'post_norm_g', 'loss_target', 'm_pre_norm_g', 'm_w_in', 'm_pool_w', 'm_pool_scale', 'm_conv_w', 'm_conv_b', 'm_w_branch', 'm_w_out', 'm_post_norm_g', 'v_pre_norm_g', 'v_w_in', 'v_pool_w', 'v_pool_scale', 'v_conv_w', 'v_conv_b', 'v_w_branch', 'v_w_out', 'v_post_norm_g']
TWIN_OUTPUTS = ['loss', 'grad_x', 'grad_pre_norm_g', 'grad_w_in', 'grad_pool_w', 'grad_pool_scale', 'grad_conv_w', 'grad_conv_b', 'grad_w_branch', 'grad_w_out', 'grad_post_norm_g', 'delta_pre_norm_g', 'delta_w_in', 'delta_pool_w', 'delta_pool_scale', 'delta_conv_w', 'delta_conv_b', 'delta_w_branch', 'delta_w_out', 'delta_post_norm_g', 'new_m_pre_norm_g', 'new_m_w_in', 'new_m_pool_w', 'new_m_pool_scale', 'new_m_conv_w', 'new_m_conv_b', 'new_m_w_branch', 'new_m_w_out', 'new_m_post_norm_g', 'new_v_pre_norm_g', 'new_v_w_in', 'new_v_pool_w', 'new_v_pool_scale', 'new_v_conv_w', 'new_v_conv_b', 'new_v_w_branch', 'new_v_w_out', 'new_v_post_norm_g']
TWIN_LEAF_KINDS = {'loss': 'loss', 'grad_x': 'grad_x', 'grad_pre_norm_g': 'grad_w', 'grad_w_in': 'grad_w', 'grad_pool_w': 'grad_w', 'grad_pool_scale': 'grad_w', 'grad_conv_w': 'grad_w', 'grad_conv_b': 'grad_w', 'grad_w_branch': 'grad_w', 'grad_w_out': 'grad_w', 'grad_post_norm_g': 'grad_w', 'delta_pre_norm_g': 'delta_w', 'delta_w_in': 'delta_w', 'delta_pool_w': 'delta_w', 'delta_pool_scale': 'delta_w', 'delta_conv_w': 'delta_w', 'delta_conv_b': 'delta_w', 'delta_w_branch': 'delta_w', 'delta_w_out': 'delta_w', 'delta_post_norm_g': 'delta_w', 'new_m_pre_norm_g': 'new_m', 'new_m_w_in': 'new_m', 'new_m_pool_w': 'new_m', 'new_m_pool_scale': 'new_m', 'new_m_conv_w': 'new_m', 'new_m_conv_b': 'new_m', 'new_m_w_branch': 'new_m', 'new_m_w_out': 'new_m', 'new_m_post_norm_g': 'new_m', 'new_v_pre_norm_g': 'new_v', 'new_v_w_in': 'new_v', 'new_v_pool_w': 'new_v', 'new_v_pool_scale': 'new_v', 'new_v_conv_w': 'new_v', 'new_v_conv_b': 'new_v', 'new_v_w_branch': 'new_v', 'new_v_w_out': 'new_v', 'new_v_post_norm_g': 'new_v'}


def _forward(args):
    return _fwd_reference(*[args[k] for k in FWD_PARAMS])


def _output_shape():
    def fwd():
        inp = _fwd_setup_inputs(0)
        return _fwd_reference(*[inp[k] for k in FWD_PARAMS])
    out = _jax.eval_shape(fwd)
    return out.shape, out.dtype

N_MICROBATCH = 1
ADAM_LR = 0.001
ADAM_B1 = 0.9
ADAM_B2 = 0.999
ADAM_EPS = 1e-08
ADAM_WD = 0.01
ADAM_STEP = 10
PER_EXAMPLE_BATCH_AXIS = {'x': 0, 'loss_target': 0}
SHARED_INPUTS = []
_WEIGHT_DTYPES = {'pre_norm_g': _jnp.float32, 'w_in': _jnp.float32, 'pool_w': _jnp.float32, 'pool_scale': _jnp.float32, 'conv_w': _jnp.float32, 'conv_b': _jnp.float32, 'w_branch': _jnp.float32, 'w_out': _jnp.float32, 'post_norm_g': _jnp.float32}
MOMENT_SCALE = {'pre_norm_g': 7.589086e-01, 'w_in': 2.703112e-01, 'pool_w': 4.308479e-01, 'pool_scale': 4.097419e-01, 'conv_w': 4.061429e-01, 'conv_b': 4.957912e-01, 'w_branch': 2.627232e-01, 'w_out': 4.682758e-01, 'post_norm_g': 3.202976e+01}


def _to_microbatches(a, axis):
    t = _jnp.moveaxis(a, axis, 0)
    t = t.reshape((N_MICROBATCH, t.shape[0] // N_MICROBATCH) + t.shape[1:])
    return _jnp.moveaxis(t, 1, axis + 1)


def setup_inputs(seed: int = 0) -> dict:
    inp = _fwd_setup_inputs(seed)
    key = _jax.random.fold_in(_jax.random.key(seed), 7919)
    shape, _ = _output_shape()
    out = dict(inp)
    out["loss_target"] = _jax.random.normal(_jax.random.fold_in(key, 0), shape, _jnp.float32)
    for i, name in enumerate(TWIN_WEIGHTS):
        w = inp[name].astype(_jnp.float32)
        if MOMENT_SCALE is None:
            s = _jnp.sqrt(_jnp.mean(_jnp.square(w)) + 1e-30)
        else:
            s = MOMENT_SCALE[name]
        km, kv = _jax.random.split(_jax.random.fold_in(key, i + 1))
        out[name] = w
        out["m_" + name] = s * _jax.random.normal(km, w.shape, _jnp.float32)
        out["v_" + name] = (s * s) * _jax.random.uniform(kv, w.shape, _jnp.float32, 0.5, 1.5)
    if N_MICROBATCH > 1:
        for name, axis in PER_EXAMPLE_BATCH_AXIS.items():
            out[name] = _to_microbatches(out[name], axis)
    return {'x': out['x'], 'pre_norm_g': out['pre_norm_g'], 'w_in': out['w_in'], 'pool_w': out['pool_w'], 'pool_scale': out['pool_scale'], 'conv_w': out['conv_w'], 'conv_b': out['conv_b'], 'w_branch': out['w_branch'], 'w_out': out['w_out'], 'post_norm_g': out['post_norm_g'], 'loss_target': out['loss_target'], 'm_pre_norm_g': out['m_pre_norm_g'], 'm_w_in': out['m_w_in'], 'm_pool_w': out['m_pool_w'], 'm_pool_scale': out['m_pool_scale'], 'm_conv_w': out['m_conv_w'], 'm_conv_b': out['m_conv_b'], 'm_w_branch': out['m_w_branch'], 'm_w_out': out['m_w_out'], 'm_post_norm_g': out['m_post_norm_g'], 'v_pre_norm_g': out['v_pre_norm_g'], 'v_w_in': out['v_w_in'], 'v_pool_w': out['v_pool_w'], 'v_pool_scale': out['v_pool_scale'], 'v_conv_w': out['v_conv_w'], 'v_conv_b': out['v_conv_b'], 'v_w_branch': out['v_w_branch'], 'v_w_out': out['v_w_out'], 'v_post_norm_g': out['v_post_norm_g']}


def _loss(weights, diff, rest, loss_target):
    with _jax.named_scope("forward"):
        args = {**rest, TWIN_DIFF_INPUT: diff, **{k: w.astype(_WEIGHT_DTYPES[k]) for k, w in weights.items()}}
        y = _forward(args)
    with _jax.named_scope("loss_head"):
        err = _jnp.square(y.astype(_jnp.float32) - loss_target)
        return 0.5 * _jnp.sum(_jnp.mean(err, axis=-1)) if err.ndim else 0.5 * err


def _adamw(w, g, m, v):
    m = ADAM_B1 * m + (1.0 - ADAM_B1) * g
    v = ADAM_B2 * v + (1.0 - ADAM_B2) * _jnp.square(g)
    m_hat = m / (1.0 - ADAM_B1 ** ADAM_STEP)
    v_hat = v / (1.0 - ADAM_B2 ** ADAM_STEP)
    delta = -ADAM_LR * (m_hat / (_jnp.sqrt(v_hat) + ADAM_EPS) + ADAM_WD * w)
    return delta, m, v


def reference(x, pre_norm_g, w_in, pool_w, pool_scale, conv_w, conv_b, w_branch, w_out, post_norm_g, loss_target, m_pre_norm_g, m_w_in, m_pool_w, m_pool_scale, m_conv_w, m_conv_b, m_w_branch, m_w_out, m_post_norm_g, v_pre_norm_g, v_w_in, v_pool_w, v_pool_scale, v_conv_w, v_conv_b, v_w_branch, v_w_out, v_post_norm_g):
    given = dict(x=x, pre_norm_g=pre_norm_g, w_in=w_in, pool_w=pool_w, pool_scale=pool_scale, conv_w=conv_w, conv_b=conv_b, w_branch=w_branch, w_out=w_out, post_norm_g=post_norm_g, loss_target=loss_target, m_pre_norm_g=m_pre_norm_g, m_w_in=m_w_in, m_pool_w=m_pool_w, m_pool_scale=m_pool_scale, m_conv_w=m_conv_w, m_conv_b=m_conv_b, m_w_branch=m_w_branch, m_w_out=m_w_out, m_post_norm_g=m_post_norm_g, v_pre_norm_g=v_pre_norm_g, v_w_in=v_w_in, v_pool_w=v_pool_w, v_pool_scale=v_pool_scale, v_conv_w=v_conv_w, v_conv_b=v_conv_b, v_w_branch=v_w_branch, v_w_out=v_w_out, v_post_norm_g=v_post_norm_g)
    weights = {n: given[n] for n in TWIN_WEIGHTS}
    shared = {n: given[n] for n in SHARED_INPUTS}
    per_example = {n: given[n] for n in ['x']}
    grad_fn = _jax.value_and_grad(_loss, argnums=(0, 1))

    def one_microbatch(ex, loss_target):
        ex = dict(ex)
        diff = ex.pop(TWIN_DIFF_INPUT)
        return grad_fn(weights, diff, {**shared, **ex}, loss_target)

    if N_MICROBATCH == 1:
        loss, (grad_w, grad_x) = one_microbatch(per_example, given["loss_target"])
    else:
        def body(carry, xs):
            loss_sum, grad_sum = carry
            l_k, (gw_k, gx_k) = one_microbatch(xs[0], xs[1])
            with _jax.named_scope("update"):
                return (loss_sum + l_k, _jax.tree.map(_jnp.add, grad_sum, gw_k)), gx_k

        init = (_jnp.zeros((), _jnp.float32), _jax.tree.map(_jnp.zeros_like, weights))
        (loss, grad_w), grad_x = _jax.lax.scan(body, init, (per_example, given["loss_target"]))
    with _jax.named_scope("update"):
        delta_w, new_m, new_v = {}, {}, {}
        for n in TWIN_WEIGHTS:
            delta_w[n], new_m[n], new_v[n] = _adamw(weights[n], grad_w[n], given["m_" + n], given["v_" + n])
    return (loss, grad_x, *[grad_w[n] for n in TWIN_WEIGHTS], *[delta_w[n] for n in TWIN_WEIGHTS],
            *[new_m[n] for n in TWIN_WEIGHTS], *[new_v[n] for n in TWIN_WEIGHTS])
```

```python
import functools

import jax
import jax.numpy as jnp
from jax import lax
from jax.experimental import pallas as pl
from jax.experimental.pallas import tpu as pltpu

F32 = jnp.float32
BF16 = jnp.bfloat16
MESH = pl.DeviceIdType.MESH
ANY = pl.BlockSpec(memory_space=pl.ANY)

DEPTH = 2
D_MODEL = 1024
WIDTH = 512
N_IN = 8192
N_CHIPS = 4
HEAD_DIM = 64
RMS_EPS = 1e-6
POOL_HALO = 16
CONV_HALO = 8
LANE = 128
COL_POOL_V, COL_POOL_G = 0, 4
COL_CONV_X, COL_CONV_GB, COL_CONV_GC, COL_CONV_G = 8, 12, 16, 20
COL_SB_Q, COL_SB_K, COL_SB_V, COL_SB_G = 24, 28, 32, 36
COL_MERGE_1024 = 5

ADAM_LR, ADAM_B1, ADAM_B2, ADAM_EPS, ADAM_WD, ADAM_STEP = 0.001, 0.9, 0.999, 1e-08, 0.01, 10

NN = (((1,), (0,)), ((), ()))
NT = (((1,), (1,)), ((), ()))
TN = (((0,), (0,)), ((), ()))


def _sigmoid(x):
    return 1.0 / (1.0 + jnp.exp(-x))


def _silu_and_grad(x):
    s = _sigmoid(x)
    return x * s, s * (1.0 + x * (1.0 - s))


def _dot(a, b, dims):
    return lax.dot_general(a, b, dims, preferred_element_type=F32)


def _matmul(a, b, mode, out_dtype, name, tm=1024, tn=1024, tk=1024):
    if mode == "nn":
        (m, k), (k2, n) = a.shape, b.shape
    elif mode == "nt":
        (m, k), (n, k2) = a.shape, b.shape
    else:
        (k, m), (k2, n) = a.shape, b.shape
    assert k == k2 and a.dtype == BF16 and b.dtype == BF16
    tm, tn, tk = min(tm, m), min(tn, n), min(tk, k)
    assert m % tm == 0 and n % tn == 0 and k % tk == 0
    nk = k // tk
    dims = {"nn": NN, "nt": NT, "tn": TN}[mode]

    def body(a_ref, b_ref, o_ref, *scratch):
        p = _dot(a_ref[...], b_ref[...], dims)
        if nk == 1:
            o_ref[...] = p.astype(o_ref.dtype)
        else:
            acc = scratch[0]
            kk = pl.program_id(2)

            @pl.when(kk == 0)
            def _():
                acc[...] = p

            @pl.when(kk > 0)
            def _():
                acc[...] += p

            @pl.when(kk == nk - 1)
            def _():
                o_ref[...] = acc[...].astype(o_ref.dtype)

    if mode == "tn":
        a_spec = pl.BlockSpec((tk, tm), lambda i, j, kk: (kk, i))
    else:
        a_spec = pl.BlockSpec((tm, tk), lambda i, j, kk: (i, kk))
    if mode == "nt":
        b_spec = pl.BlockSpec((tn, tk), lambda i, j, kk: (j, kk))
    else:
        b_spec = pl.BlockSpec((tk, tn), lambda i, j, kk: (kk, j))
    return pl.pallas_call(
        body, name=name,
        grid=(m // tm, n // tn, nk),
        in_specs=[a_spec, b_spec],
        out_specs=pl.BlockSpec((tm, tn), lambda i, j, kk: (i, j)),
        out_shape=jax.ShapeDtypeStruct((m, n), out_dtype),
        scratch_shapes=[pltpu.VMEM((tm, tn), F32)] if nk > 1 else [],
        compiler_params=pltpu.CompilerParams(dimension_semantics=("parallel", "parallel", "arbitrary")),
    )(a, b)


def _rms_fwd(x, g, name, ts=512):
    s, d = x.shape

    def body(x_ref, g_ref, h_ref):
        xv = x_ref[...]
        r = lax.rsqrt(jnp.mean(xv * xv, axis=-1, keepdims=True) + RMS_EPS)
        h_ref[...] = (xv * r * g_ref[...]).astype(BF16)

    return pl.pallas_call(
        body, name=name, grid=(s // ts,),
        in_specs=[pl.BlockSpec((ts, d), lambda i: (i, 0)), pl.BlockSpec((1, d), lambda i: (0, 0))],
        out_specs=pl.BlockSpec((ts, d), lambda i: (i, 0)),
        out_shape=jax.ShapeDtypeStruct((s, d), BF16),
    )(x, g)


def _rms_bwd(xin, g, dh, resid, out_dtype, name, ts=512):
    s, d = xin.shape
    has_resid = resid is not None

    def body(*refs):
        if has_resid:
            x_ref, g_ref, dh_ref, res_ref, dx_ref, dg_ref = refs
        else:
            x_ref, g_ref, dh_ref, dx_ref, dg_ref = refs
        xv = x_ref[...]
        dhv = dh_ref[...].astype(F32)
        r = lax.rsqrt(jnp.mean(xv * xv, axis=-1, keepdims=True) + RMS_EPS)
        nrm = xv * r
        dn = dhv * g_ref[...]
        dx = r * (dn - nrm * jnp.mean(dn * nrm, axis=-1, keepdims=True))
        if has_resid:
            dx = dx + res_ref[...]
        dx_ref[...] = dx.astype(dx_ref.dtype)
        part = jnp.sum(dhv * nrm, axis=0, keepdims=True)

        @pl.when(pl.program_id(0) == 0)
        def _():
            dg_ref[...] = part

        @pl.when(pl.program_id(0) > 0)
        def _():
            dg_ref[...] += part

    row = pl.BlockSpec((ts, d), lambda i: (i, 0))
    vec = pl.BlockSpec((1, d), lambda i: (0, 0))
    ins = [xin, g, dh] + ([resid] if has_resid else [])
    return pl.pallas_call(
        body, name=name, grid=(s // ts,),
        in_specs=[row, vec, row] + ([row] if has_resid else []),
        out_specs=[row, vec],
        out_shape=[jax.ShapeDtypeStruct((s, d), out_dtype), jax.ShapeDtypeStruct((1, d), F32)],
        compiler_params=pltpu.CompilerParams(dimension_semantics=("arbitrary",)),
    )(*ins)


def _resid_out(x, out, g, target, name, ts=512):
    s, d = x.shape
    has_loss = target is not None

    def body(*refs):
        if has_loss:
            x_ref, o_ref, g_ref, t_ref, dy_ref, loss_ref = refs
        else:
            x_ref, o_ref, g_ref, y_ref = refs
        ov = o_ref[...]
        r = lax.rsqrt(jnp.mean(ov * ov, axis=-1, keepdims=True) + RMS_EPS)
        yv = x_ref[...] + ov * r * g_ref[...]
        if not has_loss:
            y_ref[...] = yv
            return
        err = yv - t_ref[...]
        dy_ref[...] = err * (1.0 / d)
        part = jnp.sum(jnp.sum(err * err, axis=-1, keepdims=True), axis=0, keepdims=True) * (0.5 / d)
        part = jnp.broadcast_to(part, (1, LANE))

        @pl.when(pl.program_id(0) == 0)
        def _():
            loss_ref[...] = part

        @pl.when(pl.program_id(0) > 0)
        def _():
            loss_ref[...] += part

    row = pl.BlockSpec((ts, d), lambda i: (i, 0))
    vec = pl.BlockSpec((1, d), lambda i: (0, 0))
    if has_loss:
        return pl.pallas_call(
            body, name=name, grid=(s // ts,),
            in_specs=[row, row, vec, row],
            out_specs=[row, pl.BlockSpec((1, LANE), lambda i: (0, 0))],
            out_shape=[jax.ShapeDtypeStruct((s, d), F32), jax.ShapeDtypeStruct((1, LANE), F32)],
            compiler_params=pltpu.CompilerParams(dimension_semantics=("arbitrary",)),
        )(x, out, g, target)
    return pl.pallas_call(
        body, name=name, grid=(s // ts,),
        in_specs=[row, row, vec], out_specs=row,
        out_shape=jax.ShapeDtypeStruct((s, d), F32),
    )(x, out, g)


def _rows_before(ref, start, n, halo):
    if start == 0:
        return jnp.concatenate([jnp.zeros((halo, ref.shape[1]), F32), ref[0:n, :]], axis=0)
    return ref[start - halo:start + n, :]


def _rows_after(ref, start, n, halo):
    if start + n == ref.shape[0]:
        return jnp.concatenate([ref[start:start + n, :].astype(F32), jnp.zeros((halo, ref.shape[1]), F32)], axis=0)
    return ref[start:start + n + halo, :].astype(F32)


def _pick_window(group, s2, s4, s8, s16):
    return jnp.where(group == 0, s2, jnp.where(group == 1, s4, jnp.where(group == 2, s8, s16)))


def _trailing_sums(ext, group):
    s2 = ext + pltpu.roll(ext, 1, 0)
    s4 = s2 + pltpu.roll(s2, 2, 0)
    s8 = s4 + pltpu.roll(s4, 4, 0)
    s16 = s8 + pltpu.roll(s8, 8, 0)
    return _pick_window(group, s2, s4, s8, s16)


def _leading_sums(ext, group):
    n = ext.shape[0]
    s2 = ext + pltpu.roll(ext, n - 1, 0)
    s4 = s2 + pltpu.roll(s2, n - 2, 0)
    s8 = s4 + pltpu.roll(s4, n - 4, 0)
    s16 = s8 + pltpu.roll(s8, n - 8, 0)
    return _pick_window(group, s2, s4, s8, s16)


def _window_count(start, n, group):
    pos = start + lax.broadcasted_iota(jnp.int32, (n, LANE), 0)
    return jnp.minimum(pos + 1, 2 << group).astype(F32)


def _pooled(v_ref, start, n, group):
    ext = _rows_before(v_ref, start, n, POOL_HALO)
    sums = _trailing_sums(ext, group)[POOL_HALO:, :]
    return sums / _window_count(start, n, group) - ext[POOL_HALO:, :]


def _pool_fwd(u, pool_w, pool_scale, name, ts=512):
    s = u.shape[0]

    def body(v_ref, gate_ref, w_ref, sc_ref, y_ref):
        group = pl.program_id(0)
        for c in range(s // ts):
            a = c * ts
            pooled = _pooled(v_ref, a, ts, group)
            mixed = _dot(pooled.astype(BF16), w_ref[...], NN)
            gate = gate_ref[a:a + ts, :]
            y_ref[a:a + ts, :] = (mixed * sc_ref[...] * (gate * _sigmoid(gate))).astype(BF16)

    col = lambda base: pl.BlockSpec((s, LANE), lambda g: (0, base + g))
    return pl.pallas_call(
        body, name=name, grid=(4,),
        in_specs=[col(COL_POOL_V), col(COL_POOL_G),
                  pl.BlockSpec((None, LANE, LANE), lambda g: (g, 0, 0)),
                  pl.BlockSpec((1, LANE), lambda g: (0, g))],
        out_specs=pl.BlockSpec((s, LANE), lambda g: (0, g)),
        out_shape=jax.ShapeDtypeStruct((s, WIDTH), BF16),
    )(u, u, pool_w, pool_scale)


def _pool_bwd(u, dy, pool_w, pool_scale, name, ts=512):
    s = u.shape[0]

    def body(v_ref, gate_ref, dy_ref, w_ref, sc_ref, dv_ref, dgate_ref, dw_ref, dsc_ref):
        group = pl.program_id(0)
        w = w_ref[...]
        scale = sc_ref[...]
        dw = jnp.zeros((LANE, LANE), F32)
        dsc = jnp.zeros((1, LANE), F32)
        for c in range(s // ts):
            a = c * ts
            n_ext = ts + POOL_HALO
            gate_e = _rows_after(gate_ref, a, ts, POOL_HALO)
            dy_e = _rows_after(dy_ref, a, ts, POOL_HALO)
            silu_e, dsilu_e = _silu_and_grad(gate_e)
            dms_e = dy_e * silu_e
            dm_e = (dms_e * scale).astype(BF16)
            dpool_e = _dot(dm_e, w, NT)
            spread = _leading_sums(dpool_e / _window_count(a, n_ext, group), group)
            dv_ref[a:a + ts, :] = (spread[0:ts, :] - dpool_e[0:ts, :]).astype(BF16)
            pooled = _pooled(v_ref, a, ts, group).astype(BF16)
            mixed = _dot(pooled, w, NN)
            dgate_ref[a:a + ts, :] = (dy_e[0:ts, :] * mixed * scale * dsilu_e[0:ts, :]).astype(BF16)
            dsc = dsc + jnp.sum(dms_e[0:ts, :] * mixed, axis=0, keepdims=True)
            dw = dw + _dot(pooled, dm_e[0:ts, :], TN)
        dw_ref[...] = dw
        dsc_ref[...] = dsc

    col = lambda base: pl.BlockSpec((s, LANE), lambda g: (0, base + g))
    out_col = pl.BlockSpec((s, LANE), lambda g: (0, g))
    return pl.pallas_call(
        body, name=name, grid=(4,),
        in_specs=[col(COL_POOL_V), col(COL_POOL_G), out_col,
                  pl.BlockSpec((None, LANE, LANE), lambda g: (g, 0, 0)),
                  pl.BlockSpec((1, LANE), lambda g: (0, g))],
        out_specs=[out_col, out_col,
                   pl.BlockSpec((None, LANE, LANE), lambda g: (g, 0, 0)),
                   pl.BlockSpec((1, LANE), lambda g: (0, g))],
        out_shape=[jax.ShapeDtypeStruct((s, WIDTH), BF16), jax.ShapeDtypeStruct((s, WIDTH), BF16),
                   jax.ShapeDtypeStruct((4, LANE, LANE), F32), jax.ShapeDtypeStruct((1, WIDTH), F32)],
    )(u, u, dy, pool_w, pool_scale)


def _conv_taps(x_ref, gc_ref, start, n):
    z_ext = _rows_before(gc_ref, start, n, CONV_HALO) * _rows_before(x_ref, start, n, CONV_HALO)
    z0 = z_ext[CONV_HALO:, :]
    z1 = pltpu.roll(z_ext, 1, 0)[CONV_HALO:, :]
    z2 = pltpu.roll(z_ext, 2, 0)[CONV_HALO:, :]
    return z0, z1, z2


def _conv_fwd(u, conv_w, conv_b, name, ts=512):
    s = u.shape[0]

    def body(x_ref, gb_ref, gc_ref, g_ref, w_ref, b_ref, y_ref):
        w0, w1, w2 = w_ref[0:1, :], w_ref[1:2, :], w_ref[2:3, :]
        for c in range(s // ts):
            a = c * ts
            z0, z1, z2 = _conv_taps(x_ref, gc_ref, a, ts)
            y = w2 * z0 + w1 * z1 + w0 * z2 + b_ref[...]
            gate = g_ref[a:a + ts, :]
            y_ref[a:a + ts, :] = (gb_ref[a:a + ts, :] * y * (gate * _sigmoid(gate))).astype(BF16)

    col = lambda base: pl.BlockSpec((s, LANE), lambda j: (0, base + j))
    return pl.pallas_call(
        body, name=name, grid=(4,),
        in_specs=[col(COL_CONV_X), col(COL_CONV_GB), col(COL_CONV_GC), col(COL_CONV_G),
                  pl.BlockSpec((3, LANE), lambda j: (0, j)), pl.BlockSpec((1, LANE), lambda j: (0, j))],
        out_specs=pl.BlockSpec((s, LANE), lambda j: (0, j)),
        out_shape=jax.ShapeDtypeStruct((s, WIDTH), BF16),
    )(u, u, u, u, conv_w, conv_b)


def _conv_bwd(u, dy, conv_w, conv_b, name, ts=512):
    s = u.shape[0]

    def body(x_ref, gb_ref, gc_ref, g_ref, dy_ref, w_ref, b_ref,
             dx_ref, dgb_ref, dgc_ref, dg_ref, dw_ref, db_ref):
        w0, w1, w2 = w_ref[0:1, :], w_ref[1:2, :], w_ref[2:3, :]
        acc = [jnp.zeros((1, LANE), F32) for _ in range(4)]
        for c in range(s // ts):
            a = c * ts
            n_ext = ts + CONV_HALO
            gate_e = _rows_after(g_ref, a, ts, CONV_HALO)
            silu_e, dsilu_e = _silu_and_grad(gate_e)
            dy_e = _rows_after(dy_ref, a, ts, CONV_HALO)
            gb_e = _rows_after(gb_ref, a, ts, CONV_HALO)
            dyy_e = dy_e * silu_e * gb_e
            dz = (w2 * dyy_e + w1 * pltpu.roll(dyy_e, n_ext - 1, 0) + w0 * pltpu.roll(dyy_e, n_ext - 2, 0))[0:ts, :]
            z0, z1, z2 = _conv_taps(x_ref, gc_ref, a, ts)
            yb = w2 * z0 + w1 * z1 + w0 * z2 + b_ref[...]
            dyv = dy_e[0:ts, :]
            dyy = dyy_e[0:ts, :]
            dg_ref[a:a + ts, :] = (dyv * gb_e[0:ts, :] * yb * dsilu_e[0:ts, :]).astype(BF16)
            dgb_ref[a:a + ts, :] = (dyv * silu_e[0:ts, :] * yb).astype(BF16)
            dx_ref[a:a + ts, :] = (dz * gc_ref[a:a + ts, :]).astype(BF16)
            dgc_ref[a:a + ts, :] = (dz * x_ref[a:a + ts, :]).astype(BF16)
            for i, term in enumerate((dyy * z2, dyy * z1, dyy * z0, dyy)):
                acc[i] = acc[i] + jnp.sum(term, axis=0, keepdims=True)
        dw_ref[0:1, :] = acc[0]
        dw_ref[1:2, :] = acc[1]
        dw_ref[2:3, :] = acc[2]
        db_ref[...] = acc[3]

    col = lambda base: pl.BlockSpec((s, LANE), lambda j: (0, base + j))
    out_col = pl.BlockSpec((s, LANE), lambda j: (0, j))
    big = jax.ShapeDtypeStruct((s, WIDTH), BF16)
    return pl.pallas_call(
        body, name=name, grid=(4,),
        in_specs=[col(COL_CONV_X), col(COL_CONV_GB), col(COL_CONV_GC), col(COL_CONV_G), out_col,
                  pl.BlockSpec((3, LANE), lambda j: (0, j)), pl.BlockSpec((1, LANE), lambda j: (0, j))],
        out_specs=[out_col, out_col, out_col, out_col,
                   pl.BlockSpec((3, LANE), lambda j: (0, j)), pl.BlockSpec((1, LANE), lambda j: (0, j))],
        out_shape=[big, big, big, big,
                   jax.ShapeDtypeStruct((3, WIDTH), F32), jax.ShapeDtypeStruct((1, WIDTH), F32)],
    )(u, u, u, u, dy, conv_w, conv_b)


def _split_bf16(x):
    hi = x.astype(BF16)
    return hi, (x - hi.astype(F32)).astype(BF16)


def _sb_scores(q_h, k_blk, on_diagonal, later_mat, carry):
    t = q_h.shape[0]
    z = _dot(q_h, k_blk, NT)
    soft = jnp.log(1.0 + jnp.exp(-jnp.abs(z)))
    row = lax.broadcasted_iota(jnp.int32, (t, t), 0)
    colm = lax.broadcasted_iota(jnp.int32, (t, t), 1)
    valid = jnp.logical_or(colm < row, jnp.logical_not(on_diagonal))
    log_keep = jnp.where(valid, -(jnp.maximum(z, 0.0) + soft), 0.0)
    beta = jnp.exp(jnp.minimum(z, 0.0) - soft)
    hi, lo = _split_bf16(log_keep)
    later = _dot(hi, later_mat, NN) + _dot(lo, later_mat, NN) + carry
    return log_keep, beta, jnp.exp(later), valid


def _head_masks(t):
    lane = lax.broadcasted_iota(jnp.int32, (t, LANE), 1)
    return lane < HEAD_DIM


def _sb_fwd(u, name, t=128):
    s = u.shape[0]
    assert s // t <= LANE
    scale = HEAD_DIM ** -0.5

    def body(q_ref, k_ref, v_ref, g_ref, o_ref, y_ref, after_ref, kb_ref, vb_ref, acc_ref, carry_ref):
        i = pl.program_id(1)

        @pl.when(i == 0)
        def _():
            kb_ref[...] = k_ref[...].astype(BF16)
            vb_ref[...] = v_ref[...].astype(BF16)

        first = _head_masks(t)
        lane = lax.broadcasted_iota(jnp.int32, (t, LANE), 1)
        after_ref[...] = jnp.zeros_like(after_ref)
        qv = q_ref[...] * scale
        q_heads = [jnp.where(first, qv, 0.0).astype(BF16), jnp.where(first, 0.0, qv).astype(BF16)]
        r = lax.broadcasted_iota(jnp.int32, (t, t), 0)
        cidx = lax.broadcasted_iota(jnp.int32, (t, t), 1)
        later_mat = (r > cidx).astype(BF16)
        acc_ref[...] = jnp.zeros_like(acc_ref)
        carry_ref[...] = jnp.zeros_like(carry_ref)

        def step(j, _):
            kb = i - j
            rows = pl.ds(pl.multiple_of(kb * t, t), t)
            k_blk = kb_ref[rows, :]
            v_blk = vb_ref[rows, :]
            for h in range(2):
                carry = carry_ref[h]
                cols = slice(h * LANE, (h + 1) * LANE)
                after_ref[:, cols] = jnp.where(lane == kb, carry, after_ref[:, cols])
                log_keep, beta, e_later, valid = _sb_scores(q_heads[h], k_blk, j == 0, later_mat, carry)
                a = jnp.where(valid, beta * e_later, 0.0)
                acc_ref[h] += _dot(a.astype(BF16), v_blk, NN)
                carry_ref[h] = carry + jnp.sum(log_keep, axis=1, keepdims=True)
            return 0

        lax.fori_loop(0, i + 1, step, 0)
        o = jnp.where(first, acc_ref[0], acc_ref[1])
        o_ref[...] = o
        gate = g_ref[...]
        y_ref[...] = (o * gate * _sigmoid(gate)).astype(BF16)

    blk = lambda base: pl.BlockSpec((t, LANE), lambda h, i: (i, base + h))
    full = lambda base: pl.BlockSpec((s, LANE), lambda h, i: (0, base + h))
    out_blk = pl.BlockSpec((t, LANE), lambda h, i: (i, h))
    return pl.pallas_call(
        body, name=name, grid=(4, s // t),
        in_specs=[blk(COL_SB_Q), full(COL_SB_K), full(COL_SB_V), blk(COL_SB_G)],
        out_specs=[out_blk, out_blk, pl.BlockSpec((t, 2 * LANE), lambda h, i: (i, h))],
        out_shape=[jax.ShapeDtypeStruct((s, WIDTH), F32), jax.ShapeDtypeStruct((s, WIDTH), BF16),
                   jax.ShapeDtypeStruct((s, 8 * LANE), F32)],
        scratch_shapes=[pltpu.VMEM((s, LANE), BF16), pltpu.VMEM((s, LANE), BF16),
                        pltpu.VMEM((2, t, LANE), F32), pltpu.VMEM((2, t, 1), F32)],
        compiler_params=pltpu.CompilerParams(dimension_semantics=("arbitrary", "arbitrary")),
    )(u, u, u, u)


def _sb_bwd(u, o, after, dy, name, t=128):
    s = u.shape[0]
    nq = s // t
    scale = HEAD_DIM ** -0.5

    def body(q_ref, k_ref, v_ref, g_ref, o_ref, after_ref, dy_ref, dq_ref, dk_ref, dv_ref, dg_ref,
             kb_ref, vb_ref, dk_acc, dv_acc, dq_acc, carry_ref):
        i = pl.program_id(1)

        @pl.when(i == 0)
        def _():
            kb_ref[...] = k_ref[...].astype(BF16)
            vb_ref[...] = v_ref[...].astype(BF16)
            dk_acc[...] = jnp.zeros_like(dk_acc)
            dv_acc[...] = jnp.zeros_like(dv_acc)

        first = _head_masks(t)
        lane = lax.broadcasted_iota(jnp.int32, (t, LANE), 1)
        gate = g_ref[...]
        silu, dsilu = _silu_and_grad(gate)
        dyv = dy_ref[...]
        do = dyv * silu
        dg_ref[...] = (dyv * o_ref[...] * dsilu).astype(BF16)
        do_heads = [jnp.where(first, do, 0.0).astype(BF16), jnp.where(first, 0.0, do).astype(BF16)]
        qv = q_ref[...] * scale
        q_heads = [jnp.where(first, qv, 0.0).astype(BF16), jnp.where(first, 0.0, qv).astype(BF16)]
        r = lax.broadcasted_iota(jnp.int32, (t, t), 0)
        cidx = lax.broadcasted_iota(jnp.int32, (t, t), 1)
        later_mat = (r > cidx).astype(BF16)
        before_mat = (r < cidx).astype(BF16)
        dq_acc[...] = jnp.zeros_like(dq_acc)
        carry_ref[...] = jnp.zeros_like(carry_ref)

        def step(kb, _):
            rows = pl.ds(pl.multiple_of(kb * t, t), t)
            k_blk = kb_ref[rows, :]
            v_blk = vb_ref[rows, :]
            for h in range(2):
                after = jnp.sum(jnp.where(lane == kb, after_ref[:, h * LANE:(h + 1) * LANE], 0.0), axis=1, keepdims=True)
                _, beta, e_later, valid = _sb_scores(q_heads[h], k_blk, kb == i, later_mat, after)
                a = jnp.where(valid, beta * e_later, 0.0)
                da = _dot(do_heads[h], v_blk, NT)
                gterm = a * da
                g_hi, g_lo = _split_bf16(gterm)
                before = _dot(g_hi, before_mat, NN) + _dot(g_lo, before_mat, NN) + carry_ref[h]
                dz = jnp.where(valid, gterm * (1.0 - beta) - beta * before, 0.0)
                dz_b = dz.astype(BF16)
                dq_acc[h] += _dot(dz_b, k_blk, NN)
                dk_acc[rows, :] += _dot(dz_b, q_heads[h], TN)
                dv_acc[rows, :] += _dot(a.astype(BF16), do_heads[h], TN)
                carry_ref[h] += jnp.sum(gterm, axis=1, keepdims=True)
            return 0

        lax.fori_loop(0, i + 1, step, 0)
        dq_ref[...] = (jnp.where(first, dq_acc[0], dq_acc[1]) * scale).astype(BF16)

        @pl.when(i == nq - 1)
        def _():
            dk_ref[...] = dk_acc[...].astype(BF16)
            dv_ref[...] = dv_acc[...].astype(BF16)

    blk = lambda base: pl.BlockSpec((t, LANE), lambda h, i: (i, base + h))
    full = lambda base: pl.BlockSpec((s, LANE), lambda h, i: (0, base + h))
    out_blk = pl.BlockSpec((t, LANE), lambda h, i: (i, h))
    out_full = pl.BlockSpec((s, LANE), lambda h, i: (0, h))
    big = jax.ShapeDtypeStruct((s, WIDTH), BF16)
    return pl.pallas_call(
        body, name=name, grid=(4, nq),
        in_specs=[blk(COL_SB_Q), full(COL_SB_K), full(COL_SB_V), blk(COL_SB_G), out_blk,
                  pl.BlockSpec((t, 2 * LANE), lambda h, i: (i, h)), out_blk],
        out_specs=[out_blk, out_full, out_full, out_blk],
        out_shape=[big, big, big, big],
        scratch_shapes=[pltpu.VMEM((s, LANE), BF16), pltpu.VMEM((s, LANE), BF16),
                        pltpu.VMEM((s, LANE), F32), pltpu.VMEM((s, LANE), F32),
                        pltpu.VMEM((2, t, LANE), F32), pltpu.VMEM((2, t, 1), F32)],
        compiler_params=pltpu.CompilerParams(dimension_semantics=("arbitrary", "arbitrary")),
    )(u, u, u, u, o, after, dy)


def _gate_fwd(u, projs, name, ts=256):
    s = u.shape[0]

    def body(m0, m1, m2, p0, p1, p2, out_ref):
        tot = None
        for m_ref, p_ref in ((m0, p0), (m1, p1), (m2, p2)):
            term = _sigmoid(m_ref[...]) * p_ref[...].astype(F32)
            tot = term if tot is None else tot + term
        out_ref[...] = tot.astype(BF16)

    mspec = lambda n: pl.BlockSpec((ts, D_MODEL), lambda i: (i, COL_MERGE_1024 + n))
    row = pl.BlockSpec((ts, D_MODEL), lambda i: (i, 0))
    return pl.pallas_call(
        body, name=name, grid=(s // ts,),
        in_specs=[mspec(0), mspec(1), mspec(2), row, row, row],
        out_specs=row, out_shape=jax.ShapeDtypeStruct((s, D_MODEL), BF16),
    )(u, u, u, *projs)


def _gate_bwd(u, projs, dmerged, name, ts=256):
    s = u.shape[0]

    def body(m0, m1, m2, p0, p1, p2, dm_ref, dp0, dp1, dp2, dl0, dl1, dl2):
        dm = dm_ref[...].astype(F32)
        for m_ref, p_ref, dp_ref, dl_ref in ((m0, p0, dp0, dl0), (m1, p1, dp1, dl1), (m2, p2, dp2, dl2)):
            gate = _sigmoid(m_ref[...])
            dp_ref[...] = (dm * gate).astype(BF16)
            dl_ref[...] = (dm * p_ref[...].astype(F32) * gate * (1.0 - gate)).astype(BF16)

    mspec = lambda n: pl.BlockSpec((ts, D_MODEL), lambda i: (i, COL_MERGE_1024 + n))
    row = pl.BlockSpec((ts, D_MODEL), lambda i: (i, 0))
    big = jax.ShapeDtypeStruct((s, D_MODEL), BF16)
    outs = pl.pallas_call(
        body, name=name, grid=(s // ts,),
        in_specs=[mspec(0), mspec(1), mspec(2), row, row, row, row],
        out_specs=[row] * 6, out_shape=[big] * 6,
    )(u, u, u, *projs, dmerged)
    return outs[:3], outs[3:]


def _as_rows(a):
    return a.reshape(-1, a.shape[-1])


def _row_tile(rows, cols, bytes_per_row_elem=4, cap=1 << 20):
    tr = rows
    while tr * cols * bytes_per_row_elem > cap and tr % 2 == 0 and (tr // 2) % 16 == 0:
        tr //= 2
    return tr


def _cast_bf16(a, name):
    a2 = _as_rows(a)
    rows, cols = a2.shape
    tr = _row_tile(rows, cols)

    def body(a_ref, o_ref):
        o_ref[...] = a_ref[...].astype(BF16)

    spec = pl.BlockSpec((tr, cols), lambda i: (i, 0))
    out = pl.pallas_call(body, name=name, grid=(rows // tr,), in_specs=[spec], out_specs=spec,
                         out_shape=jax.ShapeDtypeStruct((rows, cols), BF16))(a2)
    return out.reshape(a.shape)


def _adamw(w, g, m, v, name):
    shape = w.shape
    w2, g2, m2, v2 = (_as_rows(a) for a in (w, g, m, v))
    rows, cols = w2.shape
    tr = _row_tile(rows, cols)
    c1 = 1.0 - ADAM_B1 ** ADAM_STEP
    c2 = 1.0 - ADAM_B2 ** ADAM_STEP

    def body(w_ref, g_ref, m_ref, v_ref, d_ref, nm_ref, nv_ref):
        gv = g_ref[...]
        nm = ADAM_B1 * m_ref[...] + (1.0 - ADAM_B1) * gv
        nv = ADAM_B2 * v_ref[...] + (1.0 - ADAM_B2) * (gv * gv)
        nm_ref[...] = nm
        nv_ref[...] = nv
        d_ref[...] = -ADAM_LR * ((nm / c1) / (jnp.sqrt(nv / c2) + ADAM_EPS) + ADAM_WD * w_ref[...])

    spec = pl.BlockSpec((tr, cols), lambda i: (i, 0))
    sds = jax.ShapeDtypeStruct((rows, cols), F32)
    outs = pl.pallas_call(body, name=name, grid=(rows // tr,), in_specs=[spec] * 4, out_specs=[spec] * 3,
                          out_shape=[sds] * 3)(w2, g2, m2, v2)
    return tuple(o.reshape(shape) for o in outs)


def _sum_slots(a, out_dtype, name):
    n = a.shape[0]
    a3 = a.reshape(n, -1, a.shape[-1])
    _, rows, cols = a3.shape
    tr = _row_tile(rows, cols * n)

    def body(a_ref, o_ref):
        tot = a_ref[0].astype(F32)
        for k in range(1, n):
            tot = tot + a_ref[k].astype(F32)
        o_ref[...] = tot.astype(out_dtype)

    out = pl.pallas_call(
        body, name=name, grid=(rows // tr,),
        in_specs=[pl.BlockSpec((n, tr, cols), lambda i: (0, i, 0))],
        out_specs=pl.BlockSpec((tr, cols), lambda i: (i, 0)),
        out_shape=jax.ShapeDtypeStruct((rows, cols), out_dtype))(a3)
    return out.reshape(a.shape[1:])


def _chip_sum(own, recv, axis, core, name):
    half = recv.shape
    nd = len(half)
    last = nd - 1
    if axis == last:
        tl, nt = half[last], 1
    else:
        tl = min(half[last], 2048)
        nt = half[last] // tl
    block = half[:last] + (tl,)

    def own_index(i, core_ref):
        idx = [0] * nd
        idx[last] = i
        if axis == last:
            idx[last] = core_ref[0]
        else:
            idx[axis] = core_ref[0]
        return tuple(idx)

    def recv_index(i, core_ref):
        idx = [0] * nd
        idx[last] = i
        return tuple(idx)

    def body(core_ref, own_ref, recv_ref, o_ref):
        o_ref[...] = (own_ref[...] + recv_ref[...]).astype(BF16)

    return pl.pallas_call(
        body, name=name,
        grid_spec=pltpu.PrefetchScalarGridSpec(
            num_scalar_prefetch=1, grid=(nt,),
            in_specs=[pl.BlockSpec(block, own_index), pl.BlockSpec(block, recv_index)],
            out_specs=pl.BlockSpec(block, recv_index)),
        out_shape=jax.ShapeDtypeStruct(half, BF16),
    )(core, own, recv)


def _mesh_position():
    return lax.axis_index("x"), lax.axis_index("y"), lax.axis_index("c")


def _other_chips(x, y):
    return [(1 - x, y), (x, 1 - y), (1 - x, 1 - y)]


ALL_FLIPS = [(0, 0, 1), (1, 0, 0), (0, 1, 0), (1, 1, 0), (1, 0, 1), (0, 1, 1), (1, 1, 1)]


def _half(ref, axis, which, size):
    idx = [slice(None)] * len(ref.shape)
    idx[axis] = pl.ds(which * size, size)
    return ref.at[tuple(idx)]


def _sub(ref, picks):
    idx = [slice(None)] * len(ref.shape)
    for axis, start, size in picks:
        idx[axis] = pl.ds(start, size)
    return ref.at[tuple(idx)]


def _remote(src, dst, sems_send, sems_recv, k, to):
    return pltpu.make_async_remote_copy(src_ref=src, dst_ref=dst, send_sem=sems_send.at[k], recv_sem=sems_recv.at[k],
                                        device_id=to, device_id_type=MESH)


def _gather_weights(w_in, w_branch, w_out, conv_w):
    layout = [(2, w_in.shape[2], 1, w_in.shape[1] // 2),
              (3, w_branch.shape[3], 2, w_branch.shape[2] // 2),
              (1, w_out.shape[1], 2, w_out.shape[2] // 2)]
    n_big = len(layout)

    def body(win, wbr, wout, cw, win_f, wbr_f, wout_f, cw_f, send_sems, recv_sems, local_sems):
        x, y, c = _mesh_position()
        chips = _other_chips(x, y)
        sibling = (x, y, 1 - c)
        mine = 2 * x + y
        shards = [win, wbr, wout]
        fulls = [win_f, wbr_f, wout_f]

        def place(t, chip, core):
            sh_axis, sh_size, half_axis, half_size = layout[t]
            return _sub(fulls[t], [(sh_axis, chip * sh_size, sh_size), (half_axis, core * half_size, half_size)])

        local = []
        for t in range(n_big):
            sh_axis, sh_size, _, _ = layout[t]
            local.append(pltpu.make_async_copy(shards[t], _sub(fulls[t], [(sh_axis, mine * sh_size, sh_size)]),
                                               local_sems.at[t]))
        local.append(pltpu.make_async_copy(cw, cw_f.at[mine], local_sems.at[n_big]))
        for cp in local:
            cp.start()

        sends = []
        for t in range(n_big):
            _, _, half_axis, half_size = layout[t]
            for k, chip in enumerate(chips):
                sends.append(_remote(_half(shards[t], half_axis, c, half_size), place(t, mine, c),
                                     send_sems, recv_sems, 6 * t + k, (*chip, c)))
        small_base = 6 * n_big
        for k, chip in enumerate(chips):
            sends.append(_remote(cw, cw_f.at[mine], send_sems, recv_sems, small_base + k, (*chip, c)))
        for cp in sends:
            cp.start()

        passed = []
        for t in range(n_big):
            for k, (px, py) in enumerate(chips):
                theirs = 2 * px + py
                _remote(place(t, theirs, c), place(t, theirs, c), send_sems, recv_sems, 6 * t + k, (px, py, c)).wait_recv()
                fwd = _remote(place(t, theirs, c), place(t, theirs, c), send_sems, recv_sems, 6 * t + 3 + k, sibling)
                fwd.start()
                passed.append(fwd)
        for t in range(n_big):
            for k, (px, py) in enumerate(chips):
                theirs = 2 * px + py
                _remote(place(t, theirs, 1 - c), place(t, theirs, 1 - c), send_sems, recv_sems, 6 * t + 3 + k,
                        sibling).wait_recv()
        for k, (px, py) in enumerate(chips):
            _remote(cw, cw_f.at[2 * px + py], send_sems, recv_sems, small_base + k, sibling).wait_recv()
        for cp in sends + passed:
            cp.wait_send()
        for cp in local:
            cp.wait()

    n_sems = 6 * n_big + 3
    out_shape = [jax.ShapeDtypeStruct((DEPTH, w_in.shape[1], N_CHIPS * w_in.shape[2]), BF16),
                 jax.ShapeDtypeStruct(w_branch.shape[:3] + (N_CHIPS * w_branch.shape[3],), BF16),
                 jax.ShapeDtypeStruct((DEPTH, N_CHIPS * w_out.shape[1], w_out.shape[2]), BF16),
                 jax.ShapeDtypeStruct((N_CHIPS,) + conv_w.shape, F32)]
    return pl.pallas_call(
        body, name="gather_weights",
        in_specs=[ANY] * 4, out_specs=[ANY] * 4, out_shape=out_shape,
        scratch_shapes=[pltpu.SemaphoreType.DMA((n_sems,)), pltpu.SemaphoreType.DMA((n_sems,)),
                        pltpu.SemaphoreType.DMA((n_big + 1,))],
    )(w_in, w_branch, w_out, conv_w)


def _swap_halves(items, name):
    n = len(items)
    halves = []
    for a, axis in items:
        shp = list(a.shape)
        shp[axis] //= 2
        halves.append(tuple(shp))

    def body(*refs):
        srcs, dsts, (send_sems, recv_sems) = refs[:n], refs[n:2 * n], refs[2 * n:]
        x, y, c = _mesh_position()
        copies = []
        for k in range(n):
            axis = items[k][1]
            copies.append(_remote(_half(srcs[k], axis, 1 - c, halves[k][axis]), dsts[k], send_sems, recv_sems, k,
                                  (x, y, 1 - c)))
        for cp in copies:
            cp.start()
        for cp in copies:
            cp.wait()

    return pl.pallas_call(
        body, name=name, in_specs=[ANY] * n, out_specs=[ANY] * n,
        out_shape=[jax.ShapeDtypeStruct(h, a.dtype) for h, (a, _) in zip(halves, items)],
        scratch_shapes=[pltpu.SemaphoreType.DMA((n,)), pltpu.SemaphoreType.DMA((n,))],
    )(*[a for a, _ in items])


def _exchange_grads(items, small):
    n = len(items)
    slices = []
    for a, axis in items:
        shp = list(a.shape)
        shp[axis] //= N_CHIPS
        slices.append(tuple(shp))

    def body(*refs):
        srcs, small_ref = refs[:n], refs[n]
        dsts, small_all = refs[n + 1:2 * n + 1], refs[2 * n + 1]
        send_sems, recv_sems, local_sems = refs[2 * n + 2:]
        x, y, c = _mesh_position()
        chips = _other_chips(x, y)
        mine = 2 * x + y
        me = 4 * x + 2 * y + c

        def piece(k, chip):
            axis = items[k][1]
            return _half(srcs[k], axis, chip, slices[k][axis])

        local = [pltpu.make_async_copy(piece(k, mine), dsts[k].at[3], local_sems.at[k]) for k in range(n)]
        local.append(pltpu.make_async_copy(small_ref, small_all.at[me], local_sems.at[n]))
        for cp in local:
            cp.start()
        copies = []
        for k in range(n):
            for r, (px, py) in enumerate(chips):
                copies.append(_remote(piece(k, 2 * px + py), dsts[k].at[r], send_sems, recv_sems, 3 * k + r, (px, py, c)))
        for r, (fx, fy, fc) in enumerate(ALL_FLIPS):
            copies.append(_remote(small_ref, small_all.at[me], send_sems, recv_sems, 3 * n + r,
                                  (x ^ fx, y ^ fy, c ^ fc)))
        for cp in copies:
            cp.start()
        for cp in copies:
            cp.wait()
        for cp in local:
            cp.wait()

    n_sems = 3 * n + len(ALL_FLIPS)
    out_shape = [jax.ShapeDtypeStruct((N_CHIPS,) + s, BF16) for s in slices]
    out_shape.append(jax.ShapeDtypeStruct((2 * N_CHIPS,) + small.shape, F32))
    outs = pl.pallas_call(
        body, name="exchange_grads", in_specs=[ANY] * (n + 1), out_specs=[ANY] * (n + 1), out_shape=out_shape,
        scratch_shapes=[pltpu.SemaphoreType.DMA((n_sems,)), pltpu.SemaphoreType.DMA((n_sems,)),
                        pltpu.SemaphoreType.DMA((n + 1,))],
    )(*[a for a, _ in items], small)
    return outs[:n], outs[n]


def _join_halves(groups, name):
    flat = [(h, g, l) for g, (hs, _) in enumerate(groups) for l, h in enumerate(hs)]
    n = len(flat)
    fulls = []
    for hs, axis in groups:
        shp = list(hs[0].shape)
        shp[axis] *= 2
        fulls.append((len(hs),) + tuple(shp))

    def body(*refs):
        srcs, outs = refs[:n], refs[n:n + len(groups)]
        send_sems, recv_sems, local_sems = refs[n + len(groups):]
        x, y, c = _mesh_position()
        local, copies = [], []
        for k, (h, g, l) in enumerate(flat):
            axis = groups[g][1]
            dst = _half(outs[g].at[l], axis, c, h.shape[axis])
            local.append(pltpu.make_async_copy(srcs[k], dst, local_sems.at[k]))
            copies.append(_remote(srcs[k], dst, send_sems, recv_sems, k, (x, y, 1 - c)))
        for cp in local + copies:
            cp.start()
        for cp in copies:
            cp.wait()
        for cp in local:
            cp.wait()

    return pl.pallas_call(
        body, name=name, in_specs=[ANY] * n, out_specs=[ANY] * len(groups),
        out_shape=[jax.ShapeDtypeStruct(f, F32) for f in fulls],
        scratch_shapes=[pltpu.SemaphoreType.DMA((n,)), pltpu.SemaphoreType.DMA((n,)), pltpu.SemaphoreType.DMA((n,))],
    )(*[h for h, _, _ in flat])


def _layer_fwd(x, p, l):
    tag = f"l{l}_"
    h = _rms_fwd(x, p["pre_g"], tag + "pre_norm")
    u = _matmul(h, p["w_in"], "nn", F32, tag + "in_proj")
    y_pool = _pool_fwd(u, p["pool_w"], p["pool_scale"], tag + "pool")
    y_conv = _conv_fwd(u, p["conv_w"], p["conv_b"], tag + "conv")
    o_sb, y_sb, sb_after = _sb_fwd(u, tag + "stickbreak")
    ys = [y_pool, y_conv, y_sb]
    projs = [_matmul(ys[n], p["w_branch"][n], "nn", BF16, tag + f"branch_proj{n}") for n in range(3)]
    merged = _gate_fwd(u, projs, tag + "merge")
    out = _matmul(merged, p["w_out"], "nn", F32, tag + "out_proj")
    return out, dict(x=x, h=h, u=u, ys=ys, o_sb=o_sb, sb_after=sb_after, projs=projs, merged=merged, out=out)


def _layer_bwd(dy, p, saved, l):
    tag = f"l{l}_bwd_"
    u = saved["u"]
    d_out, g_post = _rms_bwd(saved["out"], p["post_g"], dy, None, BF16, tag + "post_norm")
    d_merged = _matmul(d_out, p["w_out"], "nt", BF16, tag + "out_proj_dx")
    g_w_out = _matmul(saved["merged"], d_out, "tn", F32, tag + "out_proj_dw")
    d_projs, d_logits = _gate_bwd(u, saved["projs"], d_merged, tag + "merge")
    g_w_branch = [_matmul(saved["ys"][n], d_projs[n], "tn", F32, tag + f"branch_dw{n}") for n in range(3)]
    d_ys = [_matmul(d_projs[n], p["w_branch"][n], "nt", F32, tag + f"branch_dx{n}") for n in range(3)]
    d_pv, d_pg, g_pool_w, g_pool_scale = _pool_bwd(u, d_ys[0], p["pool_w"], p["pool_scale"], tag + "pool")
    d_cx, d_cgb, d_cgc, d_cg, g_conv_w, g_conv_b = _conv_bwd(u, d_ys[1], p["conv_w"], p["conv_b"], tag + "conv")
    d_q, d_k, d_v, d_sg = _sb_bwd(u, saved["o_sb"], saved["sb_after"], d_ys[2], tag + "stickbreak")
    du = jnp.concatenate([d_pv, d_pg, d_cx, d_cgb, d_cgc, d_cg, d_q, d_k, d_v, d_sg] + list(d_logits), axis=1)
    g_w_in = _matmul(saved["h"], du, "tn", F32, tag + "in_proj_dw", tk=512)
    dh = _matmul(du, p["w_in"], "nt", BF16, tag + "in_proj_dx")
    dx, g_pre = _rms_bwd(saved["x"], p["pre_g"], dh, dy, F32, tag + "pre_norm")
    grads = dict(w_in=g_w_in, w_branch=jnp.stack(g_w_branch), w_out=g_w_out, pre_g=g_pre, post_g=g_post,
                 pool_w=g_pool_w, pool_scale=g_pool_scale, conv_w=g_conv_w, conv_b=g_conv_b)
    return dx, grads


SMALL_ORDER = ["pre_g", "pool_w", "pool_scale", "conv_w", "conv_b", "post_g"]


def _pack_small(per_layer):
    parts, spans, at = [], {}, 0
    for name in SMALL_ORDER:
        a = jnp.stack([per_layer[l][name] for l in range(DEPTH)]).reshape(-1, LANE)
        parts.append(a)
        spans[name] = (at, a.shape[0])
        at += a.shape[0]
    return jnp.concatenate(parts, axis=0), spans


def kernel(x, pre_norm_g, w_in, pool_w, pool_scale, conv_w, conv_b, w_branch, w_out, post_norm_g, loss_target, m_pre_norm_g, m_w_in, m_pool_w, m_pool_scale, m_conv_w, m_conv_b, m_w_branch, m_w_out, m_post_norm_g, v_pre_norm_g, v_w_in, v_pool_w, v_pool_scale, v_conv_w, v_conv_b, v_w_branch, v_w_out, v_post_norm_g):
    mx, my, mc = _mesh_position()
    chip = 2 * mx + my
    core = mc.astype(jnp.int32).reshape(1)

    w_in_f, w_branch_f, w_out_f, conv_w_by_chip = _gather_weights(
        _cast_bf16(w_in, "cast_w_in"), _cast_bf16(w_branch, "cast_w_branch"), _cast_bf16(w_out, "cast_w_out"), conv_w)
    conv_w_f = conv_w_by_chip.transpose(1, 2, 0, 3).reshape(DEPTH, 3, WIDTH)
    pool_w_b = _cast_bf16(pool_w, "cast_pool_w")
    params = []
    for l in range(DEPTH):
        params.append(dict(
            pre_g=pre_norm_g[l:l + 1], post_g=post_norm_g[l:l + 1], w_in=w_in_f[l], w_branch=w_branch_f[l],
            w_out=w_out_f[l], pool_w=pool_w_b[l], pool_scale=pool_scale[l:l + 1], conv_w=conv_w_f[l],
            conv_b=conv_b[l:l + 1]))

    act = x[0]
    saved = []
    for l in range(DEPTH):
        out, sv = _layer_fwd(act, params[l], l)
        saved.append(sv)
        if l < DEPTH - 1:
            act = _resid_out(act, out, params[l]["post_g"], None, f"l{l}_resid")
    dy, loss_part = _resid_out(act, saved[-1]["out"], params[-1]["post_g"], loss_target[0], "loss_head")
    loss = lax.psum(loss_part[0, 0], ("x", "y", "c"))

    grads = [None] * DEPTH
    for l in reversed(range(DEPTH)):
        dy, grads[l] = _layer_bwd(dy, params[l], saved[l], l)
    grad_x = dy[None]

    split_axis = dict(w_in=0, w_branch=1, w_out=1)
    shard_axis = dict(w_in=1, w_branch=2, w_out=0)
    names = ["w_in", "w_branch", "w_out"]
    items = [(grads[l][n], split_axis[n]) for l in range(DEPTH) for n in names]
    from_sibling = _swap_halves(items, "swap_grad_halves")
    chip_sums = [(_chip_sum(a, r, axis, core, f"chip_sum{k}"), shard_axis[names[k % 3]])
                 for k, ((a, axis), r) in enumerate(zip(items, from_sibling))]
    small_part, spans = _pack_small(grads)
    by_chip, small_all = _exchange_grads(chip_sums, small_part)
    half_grads = [_sum_slots(a, F32, f"sum_chips{k}") for k, a in enumerate(by_chip)]
    groups = [([half_grads[l * 3 + i] for l in range(DEPTH)], split_axis[n]) for i, n in enumerate(names)]
    g_w_in, g_w_branch, g_w_out = _join_halves(groups, "join_grad_halves")

    small_sum = _sum_slots(small_all, F32, "sum_small")
    small = {}
    for name, like in (("pre_g", pre_norm_g), ("pool_w", pool_w), ("pool_scale", pool_scale), ("conv_b", conv_b),
                       ("post_g", post_norm_g)):
        at, n = spans[name]
        small[name] = small_sum[at:at + n].reshape(like.shape)
    at, n = spans["conv_w"]
    g_conv_w_full = small_sum[at:at + n].reshape(DEPTH, 3, WIDTH)
    g_conv_w = lax.dynamic_slice_in_dim(g_conv_w_full, chip * conv_w.shape[2], conv_w.shape[2], axis=2)

    g = dict(pre_norm_g=small["pre_g"], w_in=g_w_in, pool_w=small["pool_w"], pool_scale=small["pool_scale"],
             conv_w=g_conv_w, conv_b=small["conv_b"], w_branch=g_w_branch, w_out=g_w_out, post_norm_g=small["post_g"])
    w = dict(pre_norm_g=pre_norm_g, w_in=w_in, pool_w=pool_w, pool_scale=pool_scale, conv_w=conv_w, conv_b=conv_b,
             w_branch=w_branch, w_out=w_out, post_norm_g=post_norm_g)
    m = dict(pre_norm_g=m_pre_norm_g, w_in=m_w_in, pool_w=m_pool_w, pool_scale=m_pool_scale, conv_w=m_conv_w,
             conv_b=m_conv_b, w_branch=m_w_branch, w_out=m_w_out, post_norm_g=m_post_norm_g)
    v = dict(pre_norm_g=v_pre_norm_g, w_in=v_w_in, pool_w=v_pool_w, pool_scale=v_pool_scale, conv_w=v_conv_w,
             conv_b=v_conv_b, w_branch=v_w_branch, w_out=v_w_out, post_norm_g=v_post_norm_g)
    order = ["pre_norm_g", "w_in", "pool_w", "pool_scale", "conv_w", "conv_b", "w_branch", "w_out", "post_norm_g"]
    upd = {n: _adamw(w[n], g[n], m[n], v[n], "adamw_" + n) for n in order}
    return (loss, grad_x, *[g[n] for n in order], *[upd[n][0] for n in order], *[upd[n][1] for n in order],
            *[upd[n][2] for n in order])
```

```python
import functools

import jax
import jax.numpy as jnp
from jax import lax
from jax.experimental import pallas as pl
from jax.experimental.pallas import tpu as pltpu

F32 = jnp.float32
BF16 = jnp.bfloat16
MESH = pl.DeviceIdType.MESH
ANY = pl.BlockSpec(memory_space=pl.ANY)

DEPTH = 2
D_MODEL = 1024
WIDTH = 512
N_IN = 8192
N_CHIPS = 4
HEAD_DIM = 64
RMS_EPS = 1e-6
POOL_HALO = 16
CONV_HALO = 8
LANE = 128
COL_POOL_V, COL_POOL_G = 0, 4
COL_CONV_X, COL_CONV_GB, COL_CONV_GC, COL_CONV_G = 8, 12, 16, 20
COL_SB_Q, COL_SB_K, COL_SB_V, COL_SB_G = 24, 28, 32, 36
COL_MERGE_1024 = 5

ADAM_LR, ADAM_B1, ADAM_B2, ADAM_EPS, ADAM_WD, ADAM_STEP = 0.001, 0.9, 0.999, 1e-08, 0.01, 10

NN = (((1,), (0,)), ((), ()))
NT = (((1,), (1,)), ((), ()))
TN = (((0,), (0,)), ((), ()))


def _sigmoid(x):
    return 1.0 / (1.0 + jnp.exp(-x))


def _silu_and_grad(x):
    s = _sigmoid(x)
    return x * s, s * (1.0 + x * (1.0 - s))


def _dot(a, b, dims):
    return lax.dot_general(a, b, dims, preferred_element_type=F32)


def _matmul(a, b, mode, out_dtype, name, tm=1024, tn=1024, tk=1024):
    if mode == "nn":
        (m, k), (k2, n) = a.shape, b.shape
    elif mode == "nt":
        (m, k), (n, k2) = a.shape, b.shape
    else:
        (k, m), (k2, n) = a.shape, b.shape
    assert k == k2 and a.dtype == BF16 and b.dtype == BF16
    tm, tn, tk = min(tm, m), min(tn, n), min(tk, k)
    assert m % tm == 0 and n % tn == 0 and k % tk == 0
    nk = k // tk
    dims = {"nn": NN, "nt": NT, "tn": TN}[mode]

    def body(a_ref, b_ref, o_ref, *scratch):
        p = _dot(a_ref[...], b_ref[...], dims)
        if nk == 1:
            o_ref[...] = p.astype(o_ref.dtype)
        else:
            acc = scratch[0]
            kk = pl.program_id(2)

            @pl.when(kk == 0)
            def _():
                acc[...] = p

            @pl.when(kk > 0)
            def _():
                acc[...] += p

            @pl.when(kk == nk - 1)
            def _():
                o_ref[...] = acc[...].astype(o_ref.dtype)

    if mode == "tn":
        a_spec = pl.BlockSpec((tk, tm), lambda i, j, kk: (kk, i))
    else:
        a_spec = pl.BlockSpec((tm, tk), lambda i, j, kk: (i, kk))
    if mode == "nt":
        b_spec = pl.BlockSpec((tn, tk), lambda i, j, kk: (j, kk))
    else:
        b_spec = pl.BlockSpec((tk, tn), lambda i, j, kk: (kk, j))
    return pl.pallas_call(
        body, name=name,
        grid=(m // tm, n // tn, nk),
        in_specs=[a_spec, b_spec],
        out_specs=pl.BlockSpec((tm, tn), lambda i, j, kk: (i, j)),
        out_shape=jax.ShapeDtypeStruct((m, n), out_dtype),
        scratch_shapes=[pltpu.VMEM((tm, tn), F32)] if nk > 1 else [],
        compiler_params=pltpu.CompilerParams(dimension_semantics=("parallel", "parallel", "arbitrary")),
    )(a, b)


def _rms_fwd(x, g, name, ts=512):
    s, d = x.shape

    def body(x_ref, g_ref, h_ref):
        xv = x_ref[...]
        r = lax.rsqrt(jnp.mean(xv * xv, axis=-1, keepdims=True) + RMS_EPS)
        h_ref[...] = (xv * r * g_ref[...]).astype(BF16)

    return pl.pallas_call(
        body, name=name, grid=(s // ts,),
        in_specs=[pl.BlockSpec((ts, d), lambda i: (i, 0)), pl.BlockSpec((1, d), lambda i: (0, 0))],
        out_specs=pl.BlockSpec((ts, d), lambda i: (i, 0)),
        out_shape=jax.ShapeDtypeStruct((s, d), BF16),
    )(x, g)


def _rms_bwd(xin, g, dh, resid, out_dtype, name, ts=512):
    s, d = xin.shape
    has_resid = resid is not None

    def body(*refs):
        if has_resid:
            x_ref, g_ref, dh_ref, res_ref, dx_ref, dg_ref = refs
        else:
            x_ref, g_ref, dh_ref, dx_ref, dg_ref = refs
        xv = x_ref[...]
        dhv = dh_ref[...].astype(F32)
        r = lax.rsqrt(jnp.mean(xv * xv, axis=-1, keepdims=True) + RMS_EPS)
        nrm = xv * r
        dn = dhv * g_ref[...]
        dx = r * (dn - nrm * jnp.mean(dn * nrm, axis=-1, keepdims=True))
        if has_resid:
            dx = dx + res_ref[...]
        dx_ref[...] = dx.astype(dx_ref.dtype)
        part = jnp.sum(dhv * nrm, axis=0, keepdims=True)

        @pl.when(pl.program_id(0) == 0)
        def _():
            dg_ref[...] = part

        @pl.when(pl.program_id(0) > 0)
        def _():
            dg_ref[...] += part

    row = pl.BlockSpec((ts, d), lambda i: (i, 0))
    vec = pl.BlockSpec((1, d), lambda i: (0, 0))
    ins = [xin, g, dh] + ([resid] if has_resid else [])
    return pl.pallas_call(
        body, name=name, grid=(s // ts,),
        in_specs=[row, vec, row] + ([row] if has_resid else []),
        out_specs=[row, vec],
        out_shape=[jax.ShapeDtypeStruct((s, d), out_dtype), jax.ShapeDtypeStruct((1, d), F32)],
        compiler_params=pltpu.CompilerParams(dimension_semantics=("arbitrary",)),
    )(*ins)


def _resid_out(x, out, g, target, name, ts=512):
    s, d = x.shape
    has_loss = target is not None

    def body(*refs):
        if has_loss:
            x_ref, o_ref, g_ref, t_ref, dy_ref, loss_ref = refs
        else:
            x_ref, o_ref, g_ref, y_ref = refs
        ov = o_ref[...]
        r = lax.rsqrt(jnp.mean(ov * ov, axis=-1, keepdims=True) + RMS_EPS)
        yv = x_ref[...] + ov * r * g_ref[...]
        if not has_loss:
            y_ref[...] = yv
            return
        err = yv - t_ref[...]
        dy_ref[...] = err * (1.0 / d)
        part = jnp.sum(jnp.sum(err * err, axis=-1, keepdims=True), axis=0, keepdims=True) * (0.5 / d)
        part = jnp.broadcast_to(part, (1, LANE))

        @pl.when(pl.program_id(0) == 0)
        def _():
            loss_ref[...] = part

        @pl.when(pl.program_id(0) > 0)
        def _():
            loss_ref[...] += part

    row = pl.BlockSpec((ts, d), lambda i: (i, 0))
    vec = pl.BlockSpec((1, d), lambda i: (0, 0))
    if has_loss:
        return pl.pallas_call(
            body, name=name, grid=(s // ts,),
            in_specs=[row, row, vec, row],
            out_specs=[row, pl.BlockSpec((1, LANE), lambda i: (0, 0))],
            out_shape=[jax.ShapeDtypeStruct((s, d), F32), jax.ShapeDtypeStruct((1, LANE), F32)],
            compiler_params=pltpu.CompilerParams(dimension_semantics=("arbitrary",)),
        )(x, out, g, target)
    return pl.pallas_call(
        body, name=name, grid=(s // ts,),
        in_specs=[row, row, vec], out_specs=row,
        out_shape=jax.ShapeDtypeStruct((s, d), F32),
    )(x, out, g)


def _rows_before(ref, start, n, halo):
    if start == 0:
        return jnp.concatenate([jnp.zeros((halo, ref.shape[1]), F32), ref[0:n, :]], axis=0)
    return ref[start - halo:start + n, :]


def _rows_after(ref, start, n, halo):
    if start + n == ref.shape[0]:
        return jnp.concatenate([ref[start:start + n, :].astype(F32), jnp.zeros((halo, ref.shape[1]), F32)], axis=0)
    return ref[start:start + n + halo, :].astype(F32)


def _pick_window(group, s2, s4, s8, s16):
    return jnp.where(group == 0, s2, jnp.where(group == 1, s4, jnp.where(group == 2, s8, s16)))


def _trailing_sums(ext, group):
    s2 = ext + pltpu.roll(ext, 1, 0)
    s4 = s2 + pltpu.roll(s2, 2, 0)
    s8 = s4 + pltpu.roll(s4, 4, 0)
    s16 = s8 + pltpu.roll(s8, 8, 0)
    return _pick_window(group, s2, s4, s8, s16)


def _leading_sums(ext, group):
    n = ext.shape[0]
    s2 = ext + pltpu.roll(ext, n - 1, 0)
    s4 = s2 + pltpu.roll(s2, n - 2, 0)
    s8 = s4 + pltpu.roll(s4, n - 4, 0)
    s16 = s8 + pltpu.roll(s8, n - 8, 0)
    return _pick_window(group, s2, s4, s8, s16)


def _window_count(start, n, group):
    pos = start + lax.broadcasted_iota(jnp.int32, (n, LANE), 0)
    return jnp.minimum(pos + 1, 2 << group).astype(F32)


def _pooled(v_ref, start, n, group):
    ext = _rows_before(v_ref, start, n, POOL_HALO)
    sums = _trailing_sums(ext, group)[POOL_HALO:, :]
    return sums / _window_count(start, n, group) - ext[POOL_HALO:, :]


def _pool_fwd(u, pool_w, pool_scale, name, ts=512):
    s = u.shape[0]

    def body(v_ref, gate_ref, w_ref, sc_ref, y_ref):
        group = pl.program_id(0)
        for c in range(s // ts):
            a = c * ts
            pooled = _pooled(v_ref, a, ts, group)
            mixed = _dot(pooled.astype(BF16), w_ref[...], NN)
            gate = gate_ref[a:a + ts, :]
            y_ref[a:a + ts, :] = (mixed * sc_ref[...] * (gate * _sigmoid(gate))).astype(BF16)

    col = lambda base: pl.BlockSpec((s, LANE), lambda g: (0, base + g))
    return pl.pallas_call(
        body, name=name, grid=(4,),
        in_specs=[col(COL_POOL_V), col(COL_POOL_G),
                  pl.BlockSpec((None, LANE, LANE), lambda g: (g, 0, 0)),
                  pl.BlockSpec((1, LANE), lambda g: (0, g))],
        out_specs=pl.BlockSpec((s, LANE), lambda g: (0, g)),
        out_shape=jax.ShapeDtypeStruct((s, WIDTH), BF16),
    )(u, u, pool_w, pool_scale)


def _pool_bwd(u, dy, pool_w, pool_scale, name, ts=512):
    s = u.shape[0]

    def body(v_ref, gate_ref, dy_ref, w_ref, sc_ref, dv_ref, dgate_ref, dw_ref, dsc_ref):
        group = pl.program_id(0)
        w = w_ref[...]
        scale = sc_ref[...]
        dw = jnp.zeros((LANE, LANE), F32)
        dsc = jnp.zeros((1, LANE), F32)
        for c in range(s // ts):
            a = c * ts
            n_ext = ts + POOL_HALO
            gate_e = _rows_after(gate_ref, a, ts, POOL_HALO)
            dy_e = _rows_after(dy_ref, a, ts, POOL_HALO)
            silu_e, dsilu_e = _silu_and_grad(gate_e)
            dms_e = dy_e * silu_e
            dm_e = (dms_e * scale).astype(BF16)
            dpool_e = _dot(dm_e, w, NT)
            spread = _leading_sums(dpool_e / _window_count(a, n_ext, group), group)
            dv_ref[a:a + ts, :] = (spread[0:ts, :] - dpool_e[0:ts, :]).astype(BF16)
            pooled = _pooled(v_ref, a, ts, group).astype(BF16)
            mixed = _dot(pooled, w, NN)
            dgate_ref[a:a + ts, :] = (dy_e[0:ts, :] * mixed * scale * dsilu_e[0:ts, :]).astype(BF16)
            dsc = dsc + jnp.sum(dms_e[0:ts, :] * mixed, axis=0, keepdims=True)
            dw = dw + _dot(pooled, dm_e[0:ts, :], TN)
        dw_ref[...] = dw
        dsc_ref[...] = dsc

    col = lambda base: pl.BlockSpec((s, LANE), lambda g: (0, base + g))
    out_col = pl.BlockSpec((s, LANE), lambda g: (0, g))
    return pl.pallas_call(
        body, name=name, grid=(4,),
        in_specs=[col(COL_POOL_V), col(COL_POOL_G), out_col,
                  pl.BlockSpec((None, LANE, LANE), lambda g: (g, 0, 0)),
                  pl.BlockSpec((1, LANE), lambda g: (0, g))],
        out_specs=[out_col, out_col,
                   pl.BlockSpec((None, LANE, LANE), lambda g: (g, 0, 0)),
                   pl.BlockSpec((1, LANE), lambda g: (0, g))],
        out_shape=[jax.ShapeDtypeStruct((s, WIDTH), BF16), jax.ShapeDtypeStruct((s, WIDTH), BF16),
                   jax.ShapeDtypeStruct((4, LANE, LANE), F32), jax.ShapeDtypeStruct((1, WIDTH), F32)],
    )(u, u, dy, pool_w, pool_scale)


def _conv_taps(x_ref, gc_ref, start, n):
    z_ext = _rows_before(gc_ref, start, n, CONV_HALO) * _rows_before(x_ref, start, n, CONV_HALO)
    z0 = z_ext[CONV_HALO:, :]
    z1 = pltpu.roll(z_ext, 1, 0)[CONV_HALO:, :]
    z2 = pltpu.roll(z_ext, 2, 0)[CONV_HALO:, :]
    return z0, z1, z2


def _conv_fwd(u, conv_w, conv_b, name, ts=512):
    s = u.shape[0]

    def body(x_ref, gb_ref, gc_ref, g_ref, w_ref, b_ref, y_ref):
        w0, w1, w2 = w_ref[0:1, :], w_ref[1:2, :], w_ref[2:3, :]
        for c in range(s // ts):
            a = c * ts
            z0, z1, z2 = _conv_taps(x_ref, gc_ref, a, ts)
            y = w2 * z0 + w1 * z1 + w0 * z2 + b_ref[...]
            gate = g_ref[a:a + ts, :]
            y_ref[a:a + ts, :] = (gb_ref[a:a + ts, :] * y * (gate * _sigmoid(gate))).astype(BF16)

    col = lambda base: pl.BlockSpec((s, LANE), lambda j: (0, base + j))
    return pl.pallas_call(
        body, name=name, grid=(4,),
        in_specs=[col(COL_CONV_X), col(COL_CONV_GB), col(COL_CONV_GC), col(COL_CONV_G),
                  pl.BlockSpec((3, LANE), lambda j: (0, j)), pl.BlockSpec((1, LANE), lambda j: (0, j))],
        out_specs=pl.BlockSpec((s, LANE), lambda j: (0, j)),
        out_shape=jax.ShapeDtypeStruct((s, WIDTH), BF16),
    )(u, u, u, u, conv_w, conv_b)


def _conv_bwd(u, dy, conv_w, conv_b, name, ts=512):
    s = u.shape[0]

    def body(x_ref, gb_ref, gc_ref, g_ref, dy_ref, w_ref, b_ref,
             dx_ref, dgb_ref, dgc_ref, dg_ref, dw_ref, db_ref):
        w0, w1, w2 = w_ref[0:1, :], w_ref[1:2, :], w_ref[2:3, :]
        acc = [jnp.zeros((1, LANE), F32) for _ in range(4)]
        for c in range(s // ts):
            a = c * ts
            n_ext = ts + CONV_HALO
            gate_e = _rows_after(g_ref, a, ts, CONV_HALO)
            silu_e, dsilu_e = _silu_and_grad(gate_e)
            dy_e = _rows_after(dy_ref, a, ts, CONV_HALO)
            gb_e = _rows_after(gb_ref, a, ts, CONV_HALO)
            dyy_e = dy_e * silu_e * gb_e
            dz = (w2 * dyy_e + w1 * pltpu.roll(dyy_e, n_ext - 1, 0) + w0 * pltpu.roll(dyy_e, n_ext - 2, 0))[0:ts, :]
            z0, z1, z2 = _conv_taps(x_ref, gc_ref, a, ts)
            yb = w2 * z0 + w1 * z1 + w0 * z2 + b_ref[...]
            dyv = dy_e[0:ts, :]
            dyy = dyy_e[0:ts, :]
            dg_ref[a:a + ts, :] = (dyv * gb_e[0:ts, :] * yb * dsilu_e[0:ts, :]).astype(BF16)
            dgb_ref[a:a + ts, :] = (dyv * silu_e[0:ts, :] * yb).astype(BF16)
            dx_ref[a:a + ts, :] = (dz * gc_ref[a:a + ts, :]).astype(BF16)
            dgc_ref[a:a + ts, :] = (dz * x_ref[a:a + ts, :]).astype(BF16)
            for i, term in enumerate((dyy * z2, dyy * z1, dyy * z0, dyy)):
                acc[i] = acc[i] + jnp.sum(term, axis=0, keepdims=True)
        dw_ref[0:1, :] = acc[0]
        dw_ref[1:2, :] = acc[1]
        dw_ref[2:3, :] = acc[2]
        db_ref[...] = acc[3]

    col = lambda base: pl.BlockSpec((s, LANE), lambda j: (0, base + j))
    out_col = pl.BlockSpec((s, LANE), lambda j: (0, j))
    big = jax.ShapeDtypeStruct((s, WIDTH), BF16)
    return pl.pallas_call(
        body, name=name, grid=(4,),
        in_specs=[col(COL_CONV_X), col(COL_CONV_GB), col(COL_CONV_GC), col(COL_CONV_G), out_col,
                  pl.BlockSpec((3, LANE), lambda j: (0, j)), pl.BlockSpec((1, LANE), lambda j: (0, j))],
        out_specs=[out_col, out_col, out_col, out_col,
                   pl.BlockSpec((3, LANE), lambda j: (0, j)), pl.BlockSpec((1, LANE), lambda j: (0, j))],
        out_shape=[big, big, big, big,
                   jax.ShapeDtypeStruct((3, WIDTH), F32), jax.ShapeDtypeStruct((1, WIDTH), F32)],
    )(u, u, u, u, dy, conv_w, conv_b)


def _split_bf16(x):
    hi = x.astype(BF16)
    return hi, (x - hi.astype(F32)).astype(BF16)


def _sb_scores(q_h, k_blk, valid, later_mat, carry):
    z = _dot(q_h, k_blk, NT)
    neg_z = -z
    soft = jnp.log(1.0 + jnp.exp(jnp.minimum(z, neg_z)))
    log_keep = jnp.minimum(neg_z, 0.0) - soft
    if valid is not None:
        log_keep = jnp.where(valid, log_keep, 0.0)
    log_beta = jnp.minimum(z, 0.0) - soft
    hi, lo = _split_bf16(log_keep)
    later = _dot(hi, later_mat, NN) + _dot(lo, later_mat, NN) + carry
    return log_keep, log_beta, later


def _masked(valid, x):
    return x if valid is None else jnp.where(valid, x, 0.0)


def _head_masks(t):
    lane = lax.broadcasted_iota(jnp.int32, (t, LANE), 1)
    return lane < HEAD_DIM


def _sb_fwd(u, name, t=256, pairs=2):
    s = u.shape[0]
    assert s // t <= LANE and 4 % pairs == 0
    scale = HEAD_DIM ** -0.5
    nh = 2 * pairs
    wide = pairs * LANE

    def body(q_ref, k_ref, v_ref, g_ref, o_ref, y_ref, after_ref, kb_ref, vb_ref, acc_ref, carry_ref):
        i = pl.program_id(1)

        @pl.when(i == 0)
        def _():
            kb_ref[...] = k_ref[...].astype(BF16)
            vb_ref[...] = v_ref[...].astype(BF16)

        lane = lax.broadcasted_iota(jnp.int32, (t, LANE), 1)
        first = lane < HEAD_DIM
        after_ref[...] = jnp.zeros_like(after_ref)
        qv = q_ref[...] * scale
        q_heads = []
        for p in range(pairs):
            qp = qv[:, p * LANE:(p + 1) * LANE]
            q_heads += [jnp.where(first, qp, 0.0).astype(BF16), jnp.where(first, 0.0, qp).astype(BF16)]
        r = lax.broadcasted_iota(jnp.int32, (t, t), 0)
        cidx = lax.broadcasted_iota(jnp.int32, (t, t), 1)
        later_mat = (r > cidx).astype(BF16)
        acc_ref[...] = jnp.zeros_like(acc_ref)
        carry_ref[...] = jnp.zeros_like(carry_ref)

        def block(kb, valid):
            rows = pl.ds(pl.multiple_of(kb * t, t), t)
            k_blk = kb_ref[rows, :]
            v_blk = vb_ref[rows, :]
            carries = [carry_ref[h] for h in range(nh)]
            afters = [after_ref[:, h * LANE:(h + 1) * LANE] for h in range(nh)]
            accs = [acc_ref[h] for h in range(nh)]
            outs = []
            for h in range(nh):
                cols = slice((h // 2) * LANE, (h // 2 + 1) * LANE)
                log_keep, log_beta, later = _sb_scores(q_heads[h], k_blk[:, cols], valid, later_mat, carries[h])
                a = _masked(valid, jnp.exp(log_beta + later))
                outs.append((accs[h] + _dot(a.astype(BF16), v_blk[:, cols], NN),
                             carries[h] + jnp.sum(log_keep, axis=1, keepdims=True),
                             jnp.where(lane == kb, carries[h], afters[h])))
            for h in range(nh):
                acc_ref[h] = outs[h][0]
                carry_ref[h] = outs[h][1]
                after_ref[:, h * LANE:(h + 1) * LANE] = outs[h][2]

        def step(j, _):
            block(i - 1 - j, None)
            return 0

        block(i, cidx < r)
        lax.fori_loop(0, i, step, 0)
        for p in range(pairs):
            cols = slice(p * LANE, (p + 1) * LANE)
            o = jnp.where(first, acc_ref[2 * p], acc_ref[2 * p + 1])
            o_ref[:, cols] = o
            gate = g_ref[:, cols]
            y_ref[:, cols] = (o * gate * _sigmoid(gate)).astype(BF16)

    blk = lambda base: pl.BlockSpec((t, wide), lambda g, i: (i, base // pairs + g))
    full = lambda base: pl.BlockSpec((s, wide), lambda g, i: (0, base // pairs + g))
    out_blk = pl.BlockSpec((t, wide), lambda g, i: (i, g))
    return pl.pallas_call(
        body, name=name, grid=(4 // pairs, s // t),
        in_specs=[blk(COL_SB_Q), full(COL_SB_K), full(COL_SB_V), blk(COL_SB_G)],
        out_specs=[out_blk, out_blk, pl.BlockSpec((t, nh * LANE), lambda g, i: (i, g))],
        out_shape=[jax.ShapeDtypeStruct((s, WIDTH), F32), jax.ShapeDtypeStruct((s, WIDTH), BF16),
                   jax.ShapeDtypeStruct((s, 8 * LANE), F32)],
        scratch_shapes=[pltpu.VMEM((s, wide), BF16), pltpu.VMEM((s, wide), BF16),
                        pltpu.VMEM((nh, t, LANE), F32), pltpu.VMEM((nh, t, 1), F32)],
        compiler_params=pltpu.CompilerParams(dimension_semantics=("arbitrary", "arbitrary")),
    )(u, u, u, u)


def _sb_bwd(u, o, after, dy, name, t=256, pairs=2):
    s = u.shape[0]
    nq = s // t
    scale = HEAD_DIM ** -0.5
    nh = 2 * pairs
    wide = pairs * LANE

    def body(q_ref, k_ref, v_ref, g_ref, o_ref, after_ref, dy_ref, dq_ref, dk_ref, dv_ref, dg_ref,
             kb_ref, vb_ref, dk_acc, dv_acc, dq_acc, carry_ref):
        i = pl.program_id(1)

        @pl.when(i == 0)
        def _():
            kb_ref[...] = k_ref[...].astype(BF16)
            vb_ref[...] = v_ref[...].astype(BF16)
            dk_acc[...] = jnp.zeros_like(dk_acc)
            dv_acc[...] = jnp.zeros_like(dv_acc)

        lane = lax.broadcasted_iota(jnp.int32, (t, LANE), 1)
        first = lane < HEAD_DIM
        gate = g_ref[...]
        silu, dsilu = _silu_and_grad(gate)
        dyv = dy_ref[...]
        do = dyv * silu
        dg_ref[...] = (dyv * o_ref[...] * dsilu).astype(BF16)
        qv = q_ref[...] * scale
        do_heads, q_heads = [], []
        for p in range(pairs):
            cols = slice(p * LANE, (p + 1) * LANE)
            do_heads += [jnp.where(first, do[:, cols], 0.0).astype(BF16), jnp.where(first, 0.0, do[:, cols]).astype(BF16)]
            q_heads += [jnp.where(first, qv[:, cols], 0.0).astype(BF16), jnp.where(first, 0.0, qv[:, cols]).astype(BF16)]
        r = lax.broadcasted_iota(jnp.int32, (t, t), 0)
        cidx = lax.broadcasted_iota(jnp.int32, (t, t), 1)
        later_mat = (r > cidx).astype(BF16)
        before_mat = (r < cidx).astype(BF16)
        dq_acc[...] = jnp.zeros_like(dq_acc)
        carry_ref[...] = jnp.zeros_like(carry_ref)

        def block(kb, valid):
            rows = pl.ds(pl.multiple_of(kb * t, t), t)
            k_blk = kb_ref[rows, :]
            v_blk = vb_ref[rows, :]
            carries = [carry_ref[h] for h in range(nh)]
            dq_old = [dq_acc[h] for h in range(nh)]
            dk_old = dk_acc[rows, :]
            dv_old = dv_acc[rows, :]
            outs = []
            for h in range(nh):
                cols = slice((h // 2) * LANE, (h // 2 + 1) * LANE)
                after = jnp.sum(jnp.where(lane == kb, after_ref[:, h * LANE:(h + 1) * LANE], 0.0), axis=1, keepdims=True)
                _, log_beta, later = _sb_scores(q_heads[h], k_blk[:, cols], valid, later_mat, after)
                beta = jnp.exp(log_beta)
                a = _masked(valid, jnp.exp(log_beta + later))
                da = _dot(do_heads[h], v_blk[:, cols], NT)
                gterm = a * da
                g_hi, g_lo = _split_bf16(gterm)
                before = _dot(g_hi, before_mat, NN) + _dot(g_lo, before_mat, NN) + carries[h]
                dz_b = _masked(valid, gterm * (1.0 - beta) - beta * before).astype(BF16)
                outs.append((dq_old[h] + _dot(dz_b, k_blk[:, cols], NN), _dot(dz_b, q_heads[h], TN),
                             _dot(a.astype(BF16), do_heads[h], TN),
                             carries[h] + jnp.sum(gterm, axis=1, keepdims=True)))
            for h in range(nh):
                dq_acc[h] = outs[h][0]
                carry_ref[h] = outs[h][3]
            dk_new = [outs[2 * p][1] + outs[2 * p + 1][1] for p in range(pairs)]
            dv_new = [outs[2 * p][2] + outs[2 * p + 1][2] for p in range(pairs)]
            dk_acc[rows, :] = dk_old + (dk_new[0] if pairs == 1 else jnp.concatenate(dk_new, axis=1))
            dv_acc[rows, :] = dv_old + (dv_new[0] if pairs == 1 else jnp.concatenate(dv_new, axis=1))

        def step(kb, _):
            block(kb, None)
            return 0

        lax.fori_loop(0, i, step, 0)
        block(i, cidx < r)
        for p in range(pairs):
            dq_ref[:, p * LANE:(p + 1) * LANE] = (jnp.where(first, dq_acc[2 * p], dq_acc[2 * p + 1]) * scale).astype(BF16)

        @pl.when(i == nq - 1)
        def _():
            dk_ref[...] = dk_acc[...].astype(BF16)
            dv_ref[...] = dv_acc[...].astype(BF16)

    blk = lambda base: pl.BlockSpec((t, wide), lambda g, i: (i, base // pairs + g))
    full = lambda base: pl.BlockSpec((s, wide), lambda g, i: (0, base // pairs + g))
    out_blk = pl.BlockSpec((t, wide), lambda g, i: (i, g))
    out_full = pl.BlockSpec((s, wide), lambda g, i: (0, g))
    big = jax.ShapeDtypeStruct((s, WIDTH), BF16)
    return pl.pallas_call(
        body, name=name, grid=(4 // pairs, nq),
        in_specs=[blk(COL_SB_Q), full(COL_SB_K), full(COL_SB_V), blk(COL_SB_G), out_blk,
                  pl.BlockSpec((t, nh * LANE), lambda g, i: (i, g)), out_blk],
        out_specs=[out_blk, out_full, out_full, out_blk],
        out_shape=[big, big, big, big],
        scratch_shapes=[pltpu.VMEM((s, wide), BF16), pltpu.VMEM((s, wide), BF16),
                        pltpu.VMEM((s, wide), F32), pltpu.VMEM((s, wide), F32),
                        pltpu.VMEM((nh, t, LANE), F32), pltpu.VMEM((nh, t, 1), F32)],
        compiler_params=pltpu.CompilerParams(dimension_semantics=("arbitrary", "arbitrary")),
    )(u, u, u, u, o, after, dy)


def _gate_fwd(u, projs, name, ts=256):
    s = u.shape[0]

    def body(m0, m1, m2, p0, p1, p2, out_ref):
        tot = None
        for m_ref, p_ref in ((m0, p0), (m1, p1), (m2, p2)):
            term = _sigmoid(m_ref[...]) * p_ref[...].astype(F32)
            tot = term if tot is None else tot + term
        out_ref[...] = tot.astype(BF16)

    mspec = lambda n: pl.BlockSpec((ts, D_MODEL), lambda i: (i, COL_MERGE_1024 + n))
    row = pl.BlockSpec((ts, D_MODEL), lambda i: (i, 0))
    return pl.pallas_call(
        body, name=name, grid=(s // ts,),
        in_specs=[mspec(0), mspec(1), mspec(2), row, row, row],
        out_specs=row, out_shape=jax.ShapeDtypeStruct((s, D_MODEL), BF16),
    )(u, u, u, *projs)


def _gate_bwd(u, projs, dmerged, name, ts=256):
    s = u.shape[0]

    def body(m0, m1, m2, p0, p1, p2, dm_ref, dp0, dp1, dp2, dl0, dl1, dl2):
        dm = dm_ref[...].astype(F32)
        for m_ref, p_ref, dp_ref, dl_ref in ((m0, p0, dp0, dl0), (m1, p1, dp1, dl1), (m2, p2, dp2, dl2)):
            gate = _sigmoid(m_ref[...])
            dp_ref[...] = (dm * gate).astype(BF16)
            dl_ref[...] = (dm * p_ref[...].astype(F32) * gate * (1.0 - gate)).astype(BF16)

    mspec = lambda n: pl.BlockSpec((ts, D_MODEL), lambda i: (i, COL_MERGE_1024 + n))
    row = pl.BlockSpec((ts, D_MODEL), lambda i: (i, 0))
    big = jax.ShapeDtypeStruct((s, D_MODEL), BF16)
    outs = pl.pallas_call(
        body, name=name, grid=(s // ts,),
        in_specs=[mspec(0), mspec(1), mspec(2), row, row, row, row],
        out_specs=[row] * 6, out_shape=[big] * 6,
    )(u, u, u, *projs, dmerged)
    return outs[:3], outs[3:]


def _as_rows(a):
    return a.reshape(-1, a.shape[-1])


def _row_tile(rows, cols, bytes_per_row_elem=4, cap=1 << 20):
    tr = rows
    while tr * cols * bytes_per_row_elem > cap and tr % 2 == 0 and (tr // 2) % 16 == 0:
        tr //= 2
    return tr


def _cast_bf16(a, name):
    a2 = _as_rows(a)
    rows, cols = a2.shape
    tr = _row_tile(rows, cols)

    def body(a_ref, o_ref):
        o_ref[...] = a_ref[...].astype(BF16)

    spec = pl.BlockSpec((tr, cols), lambda i: (i, 0))
    out = pl.pallas_call(body, name=name, grid=(rows // tr,), in_specs=[spec], out_specs=spec,
                         out_shape=jax.ShapeDtypeStruct((rows, cols), BF16))(a2)
    return out.reshape(a.shape)


def _adamw(w, g, m, v, name):
    shape = w.shape
    w2, g2, m2, v2 = (_as_rows(a) for a in (w, g, m, v))
    rows, cols = w2.shape
    tr = _row_tile(rows, cols)
    c1 = 1.0 - ADAM_B1 ** ADAM_STEP
    c2 = 1.0 - ADAM_B2 ** ADAM_STEP

    def body(w_ref, g_ref, m_ref, v_ref, d_ref, nm_ref, nv_ref):
        gv = g_ref[...]
        nm = ADAM_B1 * m_ref[...] + (1.0 - ADAM_B1) * gv
        nv = ADAM_B2 * v_ref[...] + (1.0 - ADAM_B2) * (gv * gv)
        nm_ref[...] = nm
        nv_ref[...] = nv
        d_ref[...] = -ADAM_LR * ((nm / c1) / (jnp.sqrt(nv / c2) + ADAM_EPS) + ADAM_WD * w_ref[...])

    spec = pl.BlockSpec((tr, cols), lambda i: (i, 0))
    sds = jax.ShapeDtypeStruct((rows, cols), F32)
    outs = pl.pallas_call(body, name=name, grid=(rows // tr,), in_specs=[spec] * 4, out_specs=[spec] * 3,
                          out_shape=[sds] * 3)(w2, g2, m2, v2)
    return tuple(o.reshape(shape) for o in outs)


def _sum_slots(a, out_dtype, name):
    n = a.shape[0]
    a3 = a.reshape(n, -1, a.shape[-1])
    _, rows, cols = a3.shape
    tr = _row_tile(rows, cols * n)

    def body(a_ref, o_ref):
        tot = a_ref[0].astype(F32)
        for k in range(1, n):
            tot = tot + a_ref[k].astype(F32)
        o_ref[...] = tot.astype(out_dtype)

    out = pl.pallas_call(
        body, name=name, grid=(rows // tr,),
        in_specs=[pl.BlockSpec((n, tr, cols), lambda i: (0, i, 0))],
        out_specs=pl.BlockSpec((tr, cols), lambda i: (i, 0)),
        out_shape=jax.ShapeDtypeStruct((rows, cols), out_dtype))(a3)
    return out.reshape(a.shape[1:])


def _chip_sum(own, recv, axis, core, name):
    half = recv.shape
    nd = len(half)
    last = nd - 1
    if axis == last:
        tl, nt = half[last], 1
    else:
        tl = min(half[last], 2048)
        nt = half[last] // tl
    block = half[:last] + (tl,)

    def own_index(i, core_ref):
        idx = [0] * nd
        idx[last] = i
        if axis == last:
            idx[last] = core_ref[0]
        else:
            idx[axis] = core_ref[0]
        return tuple(idx)

    def recv_index(i, core_ref):
        idx = [0] * nd
        idx[last] = i
        return tuple(idx)

    def body(core_ref, own_ref, recv_ref, o_ref):
        o_ref[...] = (own_ref[...] + recv_ref[...]).astype(BF16)

    return pl.pallas_call(
        body, name=name,
        grid_spec=pltpu.PrefetchScalarGridSpec(
            num_scalar_prefetch=1, grid=(nt,),
            in_specs=[pl.BlockSpec(block, own_index), pl.BlockSpec(block, recv_index)],
            out_specs=pl.BlockSpec(block, recv_index)),
        out_shape=jax.ShapeDtypeStruct(half, BF16),
    )(core, own, recv)


def _mesh_position():
    return lax.axis_index("x"), lax.axis_index("y"), lax.axis_index("c")


def _other_chips(x, y):
    return [(1 - x, y), (x, 1 - y), (1 - x, 1 - y)]


ALL_FLIPS = [(0, 0, 1), (1, 0, 0), (0, 1, 0), (1, 1, 0), (1, 0, 1), (0, 1, 1), (1, 1, 1)]


def _half(ref, axis, which, size):
    idx = [slice(None)] * len(ref.shape)
    idx[axis] = pl.ds(which * size, size)
    return ref.at[tuple(idx)]


def _sub(ref, picks):
    idx = [slice(None)] * len(ref.shape)
    for axis, start, size in picks:
        idx[axis] = pl.ds(start, size)
    return ref.at[tuple(idx)]


def _remote(src, dst, sems_send, sems_recv, k, to):
    return pltpu.make_async_remote_copy(src_ref=src, dst_ref=dst, send_sem=sems_send.at[k], recv_sem=sems_recv.at[k],
                                        device_id=to, device_id_type=MESH)


def _gather_weights(w_in, w_branch, w_out, conv_w):
    layout = [(2, w_in.shape[2], 1, w_in.shape[1] // 2),
              (3, w_branch.shape[3], 2, w_branch.shape[2] // 2),
              (1, w_out.shape[1], 2, w_out.shape[2] // 2)]
    n_big = len(layout)

    def body(win, wbr, wout, cw, win_f, wbr_f, wout_f, cw_f, send_sems, recv_sems, local_sems):
        x, y, c = _mesh_position()
        chips = _other_chips(x, y)
        sibling = (x, y, 1 - c)
        mine = 2 * x + y
        shards = [win, wbr, wout]
        fulls = [win_f, wbr_f, wout_f]

        def place(t, chip, core):
            sh_axis, sh_size, half_axis, half_size = layout[t]
            return _sub(fulls[t], [(sh_axis, chip * sh_size, sh_size), (half_axis, core * half_size, half_size)])

        local = []
        for t in range(n_big):
            sh_axis, sh_size, _, _ = layout[t]
            local.append(pltpu.make_async_copy(shards[t], _sub(fulls[t], [(sh_axis, mine * sh_size, sh_size)]),
                                               local_sems.at[t]))
        local.append(pltpu.make_async_copy(cw, cw_f.at[mine], local_sems.at[n_big]))
        for cp in local:
            cp.start()

        sends = []
        for t in range(n_big):
            _, _, half_axis, half_size = layout[t]
            for k, chip in enumerate(chips):
                sends.append(_remote(_half(shards[t], half_axis, c, half_size), place(t, mine, c),
                                     send_sems, recv_sems, 6 * t + k, (*chip, c)))
        small_base = 6 * n_big
        for k, chip in enumerate(chips):
            sends.append(_remote(cw, cw_f.at[mine], send_sems, recv_sems, small_base + k, (*chip, c)))
        for cp in sends:
            cp.start()

        passed = []
        for t in range(n_big):
            for k, (px, py) in enumerate(chips):
                theirs = 2 * px + py
                _remote(place(t, theirs, c), place(t, theirs, c), send_sems, recv_sems, 6 * t + k, (px, py, c)).wait_recv()
                fwd = _remote(place(t, theirs, c), place(t, theirs, c), send_sems, recv_sems, 6 * t + 3 + k, sibling)
                fwd.start()
                passed.append(fwd)
        for t in range(n_big):
            for k, (px, py) in enumerate(chips):
                theirs = 2 * px + py
                _remote(place(t, theirs, 1 - c), place(t, theirs, 1 - c), send_sems, recv_sems, 6 * t + 3 + k,
                        sibling).wait_recv()
        for k, (px, py) in enumerate(chips):
            _remote(cw, cw_f.at[2 * px + py], send_sems, recv_sems, small_base + k, sibling).wait_recv()
        for cp in sends + passed:
            cp.wait_send()
        for cp in local:
            cp.wait()

    n_sems = 6 * n_big + 3
    out_shape = [jax.ShapeDtypeStruct((DEPTH, w_in.shape[1], N_CHIPS * w_in.shape[2]), BF16),
                 jax.ShapeDtypeStruct(w_branch.shape[:3] + (N_CHIPS * w_branch.shape[3],), BF16),
                 jax.ShapeDtypeStruct((DEPTH, N_CHIPS * w_out.shape[1], w_out.shape[2]), BF16),
                 jax.ShapeDtypeStruct((N_CHIPS,) + conv_w.shape, F32)]
    return pl.pallas_call(
        body, name="gather_weights",
        in_specs=[ANY] * 4, out_specs=[ANY] * 4, out_shape=out_shape,
        scratch_shapes=[pltpu.SemaphoreType.DMA((n_sems,)), pltpu.SemaphoreType.DMA((n_sems,)),
                        pltpu.SemaphoreType.DMA((n_big + 1,))],
    )(w_in, w_branch, w_out, conv_w)


def _swap_halves(items, name):
    n = len(items)
    halves = []
    for a, axis in items:
        shp = list(a.shape)
        shp[axis] //= 2
        halves.append(tuple(shp))

    def body(*refs):
        srcs, dsts, (send_sems, recv_sems) = refs[:n], refs[n:2 * n], refs[2 * n:]
        x, y, c = _mesh_position()
        copies = []
        for k in range(n):
            axis = items[k][1]
            copies.append(_remote(_half(srcs[k], axis, 1 - c, halves[k][axis]), dsts[k], send_sems, recv_sems, k,
                                  (x, y, 1 - c)))
        for cp in copies:
            cp.start()
        for cp in copies:
            cp.wait()

    return pl.pallas_call(
        body, name=name, in_specs=[ANY] * n, out_specs=[ANY] * n,
        out_shape=[jax.ShapeDtypeStruct(h, a.dtype) for h, (a, _) in zip(halves, items)],
        scratch_shapes=[pltpu.SemaphoreType.DMA((n,)), pltpu.SemaphoreType.DMA((n,))],
    )(*[a for a, _ in items])


def _exchange_grads(items, small):
    n = len(items)
    slices = []
    for a, axis in items:
        shp = list(a.shape)
        shp[axis] //= N_CHIPS
        slices.append(tuple(shp))

    def body(*refs):
        srcs, small_ref = refs[:n], refs[n]
        dsts, small_all = refs[n + 1:2 * n + 1], refs[2 * n + 1]
        send_sems, recv_sems, local_sems = refs[2 * n + 2:]
        x, y, c = _mesh_position()
        chips = _other_chips(x, y)
        mine = 2 * x + y
        me = 4 * x + 2 * y + c

        def piece(k, chip):
            axis = items[k][1]
            return _half(srcs[k], axis, chip, slices[k][axis])

        local = [pltpu.make_async_copy(piece(k, mine), dsts[k].at[3], local_sems.at[k]) for k in range(n)]
        local.append(pltpu.make_async_copy(small_ref, small_all.at[me], local_sems.at[n]))
        for cp in local:
            cp.start()
        copies = []
        for k in range(n):
            for r, (px, py) in enumerate(chips):
                copies.append(_remote(piece(k, 2 * px + py), dsts[k].at[r], send_sems, recv_sems, 3 * k + r, (px, py, c)))
        for r, (fx, fy, fc) in enumerate(ALL_FLIPS):
            copies.append(_remote(small_ref, small_all.at[me], send_sems, recv_sems, 3 * n + r,
                                  (x ^ fx, y ^ fy, c ^ fc)))
        for cp in copies:
            cp.start()
        for cp in copies:
            cp.wait()
        for cp in local:
            cp.wait()

    n_sems = 3 * n + len(ALL_FLIPS)
    out_shape = [jax.ShapeDtypeStruct((N_CHIPS,) + s, BF16) for s in slices]
    out_shape.append(jax.ShapeDtypeStruct((2 * N_CHIPS,) + small.shape, F32))
    outs = pl.pallas_call(
        body, name="exchange_grads", in_specs=[ANY] * (n + 1), out_specs=[ANY] * (n + 1), out_shape=out_shape,
        scratch_shapes=[pltpu.SemaphoreType.DMA((n_sems,)), pltpu.SemaphoreType.DMA((n_sems,)),
                        pltpu.SemaphoreType.DMA((n + 1,))],
    )(*[a for a, _ in items], small)
    return outs[:n], outs[n]


def _join_halves(groups, name):
    flat = [(h, g, l) for g, (hs, _) in enumerate(groups) for l, h in enumerate(hs)]
    n = len(flat)
    fulls = []
    for hs, axis in groups:
        shp = list(hs[0].shape)
        shp[axis] *= 2
        fulls.append((len(hs),) + tuple(shp))

    def body(*refs):
        srcs, outs = refs[:n], refs[n:n + len(groups)]
        send_sems, recv_sems, local_sems = refs[n + len(groups):]
        x, y, c = _mesh_position()
        local, copies = [], []
        for k, (h, g, l) in enumerate(flat):
            axis = groups[g][1]
            dst = _half(outs[g].at[l], axis, c, h.shape[axis])
            local.append(pltpu.make_async_copy(srcs[k], dst, local_sems.at[k]))
            copies.append(_remote(srcs[k], dst, send_sems, recv_sems, k, (x, y, 1 - c)))
        for cp in local + copies:
            cp.start()
        for cp in copies:
            cp.wait()
        for cp in local:
            cp.wait()

    return pl.pallas_call(
        body, name=name, in_specs=[ANY] * n, out_specs=[ANY] * len(groups),
        out_shape=[jax.ShapeDtypeStruct(f, F32) for f in fulls],
        scratch_shapes=[pltpu.SemaphoreType.DMA((n,)), pltpu.SemaphoreType.DMA((n,)), pltpu.SemaphoreType.DMA((n,))],
    )(*[h for h, _, _ in flat])


def _layer_fwd(x, p, l):
    tag = f"l{l}_"
    h = _rms_fwd(x, p["pre_g"], tag + "pre_norm")
    u = _matmul(h, p["w_in"], "nn", F32, tag + "in_proj")
    y_pool = _pool_fwd(u, p["pool_w"], p["pool_scale"], tag + "pool")
    y_conv = _conv_fwd(u, p["conv_w"], p["conv_b"], tag + "conv")
    o_sb, y_sb, sb_after = _sb_fwd(u, tag + "stickbreak")
    ys = [y_pool, y_conv, y_sb]
    projs = [_matmul(ys[n], p["w_branch"][n], "nn", BF16, tag + f"branch_proj{n}") for n in range(3)]
    merged = _gate_fwd(u, projs, tag + "merge")
    out = _matmul(merged, p["w_out"], "nn", F32, tag + "out_proj")
    return out, dict(x=x, h=h, u=u, ys=ys, o_sb=o_sb, sb_after=sb_after, projs=projs, merged=merged, out=out)


def _layer_bwd(dy, p, saved, l):
    tag = f"l{l}_bwd_"
    u = saved["u"]
    d_out, g_post = _rms_bwd(saved["out"], p["post_g"], dy, None, BF16, tag + "post_norm")
    d_merged = _matmul(d_out, p["w_out"], "nt", BF16, tag + "out_proj_dx")
    g_w_out = _matmul(saved["merged"], d_out, "tn", F32, tag + "out_proj_dw")
    d_projs, d_logits = _gate_bwd(u, saved["projs"], d_merged, tag + "merge")
    g_w_branch = [_matmul(saved["ys"][n], d_projs[n], "tn", F32, tag + f"branch_dw{n}") for n in range(3)]
    d_ys = [_matmul(d_projs[n], p["w_branch"][n], "nt", F32, tag + f"branch_dx{n}") for n in range(3)]
    d_pv, d_pg, g_pool_w, g_pool_scale = _pool_bwd(u, d_ys[0], p["pool_w"], p["pool_scale"], tag + "pool")
    d_cx, d_cgb, d_cgc, d_cg, g_conv_w, g_conv_b = _conv_bwd(u, d_ys[1], p["conv_w"], p["conv_b"], tag + "conv")
    d_q, d_k, d_v, d_sg = _sb_bwd(u, saved["o_sb"], saved["sb_after"], d_ys[2], tag + "stickbreak")
    du = jnp.concatenate([d_pv, d_pg, d_cx, d_cgb, d_cgc, d_cg, d_q, d_k, d_v, d_sg] + list(d_logits), axis=1)
    g_w_in = _matmul(saved["h"], du, "tn", F32, tag + "in_proj_dw", tk=512)
    dh = _matmul(du, p["w_in"], "nt", BF16, tag + "in_proj_dx")
    dx, g_pre = _rms_bwd(saved["x"], p["pre_g"], dh, dy, F32, tag + "pre_norm")
    grads = dict(w_in=g_w_in, w_branch=jnp.stack(g_w_branch), w_out=g_w_out, pre_g=g_pre, post_g=g_post,
                 pool_w=g_pool_w, pool_scale=g_pool_scale, conv_w=g_conv_w, conv_b=g_conv_b)
    return dx, grads


SMALL_ORDER = ["pre_g", "pool_w", "pool_scale", "conv_w", "conv_b", "post_g"]


def _pack_small(per_layer):
    parts, spans, at = [], {}, 0
    for name in SMALL_ORDER:
        a = jnp.stack([per_layer[l][name] for l in range(DEPTH)]).reshape(-1, LANE)
        parts.append(a)
        spans[name] = (at, a.shape[0])
        at += a.shape[0]
    return jnp.concatenate(parts, axis=0), spans


def kernel(x, pre_norm_g, w_in, pool_w, pool_scale, conv_w, conv_b, w_branch, w_out, post_norm_g, loss_target, m_pre_norm_g, m_w_in, m_pool_w, m_pool_scale, m_conv_w, m_conv_b, m_w_branch, m_w_out, m_post_norm_g, v_pre_norm_g, v_w_in, v_pool_w, v_pool_scale, v_conv_w, v_conv_b, v_w_branch, v_w_out, v_post_norm_g):
    mx, my, mc = _mesh_position()
    chip = 2 * mx + my
    core = mc.astype(jnp.int32).reshape(1)

    w_in_f, w_branch_f, w_out_f, conv_w_by_chip = _gather_weights(
        _cast_bf16(w_in, "cast_w_in"), _cast_bf16(w_branch, "cast_w_branch"), _cast_bf16(w_out, "cast_w_out"), conv_w)
    conv_w_f = conv_w_by_chip.transpose(1, 2, 0, 3).reshape(DEPTH, 3, WIDTH)
    pool_w_b = _cast_bf16(pool_w, "cast_pool_w")
    params = []
    for l in range(DEPTH):
        params.append(dict(
            pre_g=pre_norm_g[l:l + 1], post_g=post_norm_g[l:l + 1], w_in=w_in_f[l], w_branch=w_branch_f[l],
            w_out=w_out_f[l], pool_w=pool_w_b[l], pool_scale=pool_scale[l:l + 1], conv_w=conv_w_f[l],
            conv_b=conv_b[l:l + 1]))

    act = x[0]
    saved = []
    for l in range(DEPTH):
        out, sv = _layer_fwd(act, params[l], l)
        saved.append(sv)
        if l < DEPTH - 1:
            act = _resid_out(act, out, params[l]["post_g"], None, f"l{l}_resid")
    dy, loss_part = _resid_out(act, saved[-1]["out"], params[-1]["post_g"], loss_target[0], "loss_head")
    loss = lax.psum(loss_part[0, 0], ("x", "y", "c"))

    grads = [None] * DEPTH
    for l in reversed(range(DEPTH)):
        dy, grads[l] = _layer_bwd(dy, params[l], saved[l], l)
    grad_x = dy[None]

    split_axis = dict(w_in=0, w_branch=1, w_out=1)
    shard_axis = dict(w_in=1, w_branch=2, w_out=0)
    names = ["w_in", "w_branch", "w_out"]
    items = [(grads[l][n], split_axis[n]) for l in range(DEPTH) for n in names]
    from_sibling = _swap_halves(items, "swap_grad_halves")
    chip_sums = [(_chip_sum(a, r, axis, core, f"chip_sum{k}"), shard_axis[names[k % 3]])
                 for k, ((a, axis), r) in enumerate(zip(items, from_sibling))]
    small_part, spans = _pack_small(grads)
    by_chip, small_all = _exchange_grads(chip_sums, small_part)
    half_grads = [_sum_slots(a, F32, f"sum_chips{k}") for k, a in enumerate(by_chip)]
    groups = [([half_grads[l * 3 + i] for l in range(DEPTH)], split_axis[n]) for i, n in enumerate(names)]
    g_w_in, g_w_branch, g_w_out = _join_halves(groups, "join_grad_halves")

    small_sum = _sum_slots(small_all, F32, "sum_small")
    small = {}
    for name, like in (("pre_g", pre_norm_g), ("pool_w", pool_w), ("pool_scale", pool_scale), ("conv_b", conv_b),
                       ("post_g", post_norm_g)):
        at, n = spans[name]
        small[name] = small_sum[at:at + n].reshape(like.shape)
    at, n = spans["conv_w"]
    g_conv_w_full = small_sum[at:at + n].reshape(DEPTH, 3, WIDTH)
    g_conv_w = lax.dynamic_slice_in_dim(g_conv_w_full, chip * conv_w.shape[2], conv_w.shape[2], axis=2)

    g = dict(pre_norm_g=small["pre_g"], w_in=g_w_in, pool_w=small["pool_w"], pool_scale=small["pool_scale"],
             conv_w=g_conv_w, conv_b=small["conv_b"], w_branch=g_w_branch, w_out=g_w_out, post_norm_g=small["post_g"])
    w = dict(pre_norm_g=pre_norm_g, w_in=w_in, pool_w=pool_w, pool_scale=pool_scale, conv_w=conv_w, conv_b=conv_b,
             w_branch=w_branch, w_out=w_out, post_norm_g=post_norm_g)
    m = dict(pre_norm_g=m_pre_norm_g, w_in=m_w_in, pool_w=m_pool_w, pool_scale=m_pool_scale, conv_w=m_conv_w,
             conv_b=m_conv_b, w_branch=m_w_branch, w_out=m_w_out, post_norm_g=m_post_norm_g)
    v = dict(pre_norm_g=v_pre_norm_g, w_in=v_w_in, pool_w=v_pool_w, pool_scale=v_pool_scale, conv_w=v_conv_w,
             conv_b=v_conv_b, w_branch=v_w_branch, w_out=v_w_out, post_norm_g=v_post_norm_g)
    order = ["pre_norm_g", "w_in", "pool_w", "pool_scale", "conv_w", "conv_b", "w_branch", "w_out", "post_norm_g"]
    upd = {n: _adamw(w[n], g[n], m[n], v[n], "adamw_" + n) for n in order}
    return (loss, grad_x, *[g[n] for n in order], *[upd[n][0] for n in order], *[upd[n][1] for n in order],
            *[upd[n][2] for n in order])
```

```python
import functools

import jax
import jax.numpy as jnp
from jax import lax
from jax.experimental import pallas as pl
from jax.experimental.pallas import tpu as pltpu

F32 = jnp.float32
BF16 = jnp.bfloat16
MESH = pl.DeviceIdType.MESH
ANY = pl.BlockSpec(memory_space=pl.ANY)

DEPTH = 2
D_MODEL = 1024
WIDTH = 512
N_IN = 8192
N_CHIPS = 4
HEAD_DIM = 64
RMS_EPS = 1e-6
POOL_HALO = 16
CONV_HALO = 8
LANE = 128
COL_POOL_V, COL_POOL_G = 0, 4
COL_CONV_X, COL_CONV_GB, COL_CONV_GC, COL_CONV_G = 8, 12, 16, 20
COL_SB_Q, COL_SB_K, COL_SB_V, COL_SB_G = 24, 28, 32, 36
COL_MERGE_1024 = 5

ADAM_LR, ADAM_B1, ADAM_B2, ADAM_EPS, ADAM_WD, ADAM_STEP = 0.001, 0.9, 0.999, 1e-08, 0.01, 10

NN = (((1,), (0,)), ((), ()))
NT = (((1,), (1,)), ((), ()))
TN = (((0,), (0,)), ((), ()))


def _sigmoid(x):
    return 1.0 / (1.0 + jnp.exp(-x))


def _silu_and_grad(x):
    s = _sigmoid(x)
    return x * s, s * (1.0 + x * (1.0 - s))


def _dot(a, b, dims):
    return lax.dot_general(a, b, dims, preferred_element_type=F32)


def _matmul(a, b, mode, out_dtype, name, tm=1024, tn=1024, tk=1024):
    if mode == "nn":
        (m, k), (k2, n) = a.shape, b.shape
    elif mode == "nt":
        (m, k), (n, k2) = a.shape, b.shape
    else:
        (k, m), (k2, n) = a.shape, b.shape
    assert k == k2 and a.dtype == BF16 and b.dtype == BF16
    tm, tn, tk = min(tm, m), min(tn, n), min(tk, k)
    assert m % tm == 0 and n % tn == 0 and k % tk == 0
    nk = k // tk
    dims = {"nn": NN, "nt": NT, "tn": TN}[mode]

    def body(a_ref, b_ref, o_ref, *scratch):
        p = _dot(a_ref[...], b_ref[...], dims)
        if nk == 1:
            o_ref[...] = p.astype(o_ref.dtype)
        else:
            acc = scratch[0]
            kk = pl.program_id(2)

            @pl.when(kk == 0)
            def _():
                acc[...] = p

            @pl.when(kk > 0)
            def _():
                acc[...] += p

            @pl.when(kk == nk - 1)
            def _():
                o_ref[...] = acc[...].astype(o_ref.dtype)

    if mode == "tn":
        a_spec = pl.BlockSpec((tk, tm), lambda i, j, kk: (kk, i))
    else:
        a_spec = pl.BlockSpec((tm, tk), lambda i, j, kk: (i, kk))
    if mode == "nt":
        b_spec = pl.BlockSpec((tn, tk), lambda i, j, kk: (j, kk))
    else:
        b_spec = pl.BlockSpec((tk, tn), lambda i, j, kk: (kk, j))
    return pl.pallas_call(
        body, name=name,
        grid=(m // tm, n // tn, nk),
        in_specs=[a_spec, b_spec],
        out_specs=pl.BlockSpec((tm, tn), lambda i, j, kk: (i, j)),
        out_shape=jax.ShapeDtypeStruct((m, n), out_dtype),
        scratch_shapes=[pltpu.VMEM((tm, tn), F32)] if nk > 1 else [],
        compiler_params=pltpu.CompilerParams(dimension_semantics=("parallel", "parallel", "arbitrary")),
    )(a, b)


def _rms_fwd(x, g, name, ts=512):
    s, d = x.shape

    def body(x_ref, g_ref, h_ref):
        xv = x_ref[...]
        r = lax.rsqrt(jnp.mean(xv * xv, axis=-1, keepdims=True) + RMS_EPS)
        h_ref[...] = (xv * r * g_ref[...]).astype(BF16)

    return pl.pallas_call(
        body, name=name, grid=(s // ts,),
        in_specs=[pl.BlockSpec((ts, d), lambda i: (i, 0)), pl.BlockSpec((1, d), lambda i: (0, 0))],
        out_specs=pl.BlockSpec((ts, d), lambda i: (i, 0)),
        out_shape=jax.ShapeDtypeStruct((s, d), BF16),
    )(x, g)


def _rms_bwd(xin, g, dh, resid, out_dtype, name, ts=512):
    s, d = xin.shape
    has_resid = resid is not None

    def body(*refs):
        if has_resid:
            x_ref, g_ref, dh_ref, res_ref, dx_ref, dg_ref = refs
        else:
            x_ref, g_ref, dh_ref, dx_ref, dg_ref = refs
        xv = x_ref[...]
        dhv = dh_ref[...].astype(F32)
        r = lax.rsqrt(jnp.mean(xv * xv, axis=-1, keepdims=True) + RMS_EPS)
        nrm = xv * r
        dn = dhv * g_ref[...]
        dx = r * (dn - nrm * jnp.mean(dn * nrm, axis=-1, keepdims=True))
        if has_resid:
            dx = dx + res_ref[...]
        dx_ref[...] = dx.astype(dx_ref.dtype)
        part = jnp.sum(dhv * nrm, axis=0, keepdims=True)

        @pl.when(pl.program_id(0) == 0)
        def _():
            dg_ref[...] = part

        @pl.when(pl.program_id(0) > 0)
        def _():
            dg_ref[...] += part

    row = pl.BlockSpec((ts, d), lambda i: (i, 0))
    vec = pl.BlockSpec((1, d), lambda i: (0, 0))
    ins = [xin, g, dh] + ([resid] if has_resid else [])
    return pl.pallas_call(
        body, name=name, grid=(s // ts,),
        in_specs=[row, vec, row] + ([row] if has_resid else []),
        out_specs=[row, vec],
        out_shape=[jax.ShapeDtypeStruct((s, d), out_dtype), jax.ShapeDtypeStruct((1, d), F32)],
        compiler_params=pltpu.CompilerParams(dimension_semantics=("arbitrary",)),
    )(*ins)


def _resid_out(x, out, g, target, name, ts=512):
    s, d = x.shape
    has_loss = target is not None

    def body(*refs):
        if has_loss:
            x_ref, o_ref, g_ref, t_ref, dy_ref, loss_ref = refs
        else:
            x_ref, o_ref, g_ref, y_ref = refs
        ov = o_ref[...]
        r = lax.rsqrt(jnp.mean(ov * ov, axis=-1, keepdims=True) + RMS_EPS)
        yv = x_ref[...] + ov * r * g_ref[...]
        if not has_loss:
            y_ref[...] = yv
            return
        err = yv - t_ref[...]
        dy_ref[...] = err * (1.0 / d)
        part = jnp.sum(jnp.sum(err * err, axis=-1, keepdims=True), axis=0, keepdims=True) * (0.5 / d)
        part = jnp.broadcast_to(part, (1, LANE))

        @pl.when(pl.program_id(0) == 0)
        def _():
            loss_ref[...] = part

        @pl.when(pl.program_id(0) > 0)
        def _():
            loss_ref[...] += part

    row = pl.BlockSpec((ts, d), lambda i: (i, 0))
    vec = pl.BlockSpec((1, d), lambda i: (0, 0))
    if has_loss:
        return pl.pallas_call(
            body, name=name, grid=(s // ts,),
            in_specs=[row, row, vec, row],
            out_specs=[row, pl.BlockSpec((1, LANE), lambda i: (0, 0))],
            out_shape=[jax.ShapeDtypeStruct((s, d), F32), jax.ShapeDtypeStruct((1, LANE), F32)],
            compiler_params=pltpu.CompilerParams(dimension_semantics=("arbitrary",)),
        )(x, out, g, target)
    return pl.pallas_call(
        body, name=name, grid=(s // ts,),
        in_specs=[row, row, vec], out_specs=row,
        out_shape=jax.ShapeDtypeStruct((s, d), F32),
    )(x, out, g)


def _rows_before(ref, start, n, halo):
    if start == 0:
        return jnp.concatenate([jnp.zeros((halo, ref.shape[1]), F32), ref[0:n, :]], axis=0)
    return ref[start - halo:start + n, :]


def _rows_after(ref, start, n, halo):
    if start + n == ref.shape[0]:
        return jnp.concatenate([ref[start:start + n, :].astype(F32), jnp.zeros((halo, ref.shape[1]), F32)], axis=0)
    return ref[start:start + n + halo, :].astype(F32)


def _pick_window(group, s2, s4, s8, s16):
    return jnp.where(group == 0, s2, jnp.where(group == 1, s4, jnp.where(group == 2, s8, s16)))


def _trailing_sums(ext, group):
    s2 = ext + pltpu.roll(ext, 1, 0)
    s4 = s2 + pltpu.roll(s2, 2, 0)
    s8 = s4 + pltpu.roll(s4, 4, 0)
    s16 = s8 + pltpu.roll(s8, 8, 0)
    return _pick_window(group, s2, s4, s8, s16)


def _leading_sums(ext, group):
    n = ext.shape[0]
    s2 = ext + pltpu.roll(ext, n - 1, 0)
    s4 = s2 + pltpu.roll(s2, n - 2, 0)
    s8 = s4 + pltpu.roll(s4, n - 4, 0)
    s16 = s8 + pltpu.roll(s8, n - 8, 0)
    return _pick_window(group, s2, s4, s8, s16)


def _window_count(start, n, group):
    pos = start + lax.broadcasted_iota(jnp.int32, (n, LANE), 0)
    return jnp.minimum(pos + 1, 2 << group).astype(F32)


def _pooled(v_ref, start, n, group):
    ext = _rows_before(v_ref, start, n, POOL_HALO)
    sums = _trailing_sums(ext, group)[POOL_HALO:, :]
    return sums / _window_count(start, n, group) - ext[POOL_HALO:, :]


def _pool_fwd(u, pool_w, pool_scale, name, ts=512):
    s = u.shape[0]

    def body(v_ref, gate_ref, w_ref, sc_ref, y_ref):
        group = pl.program_id(0)
        for c in range(s // ts):
            a = c * ts
            pooled = _pooled(v_ref, a, ts, group)
            mixed = _dot(pooled.astype(BF16), w_ref[...], NN)
            gate = gate_ref[a:a + ts, :]
            y_ref[a:a + ts, :] = (mixed * sc_ref[...] * (gate * _sigmoid(gate))).astype(BF16)

    col = lambda base: pl.BlockSpec((s, LANE), lambda g: (0, base + g))
    return pl.pallas_call(
        body, name=name, grid=(4,),
        in_specs=[col(COL_POOL_V), col(COL_POOL_G),
                  pl.BlockSpec((None, LANE, LANE), lambda g: (g, 0, 0)),
                  pl.BlockSpec((1, LANE), lambda g: (0, g))],
        out_specs=pl.BlockSpec((s, LANE), lambda g: (0, g)),
        out_shape=jax.ShapeDtypeStruct((s, WIDTH), BF16),
    )(u, u, pool_w, pool_scale)


def _pool_bwd(u, dy, pool_w, pool_scale, name, ts=512):
    s = u.shape[0]

    def body(v_ref, gate_ref, dy_ref, w_ref, sc_ref, dv_ref, dgate_ref, dw_ref, dsc_ref):
        group = pl.program_id(0)
        w = w_ref[...]
        scale = sc_ref[...]
        dw = jnp.zeros((LANE, LANE), F32)
        dsc = jnp.zeros((1, LANE), F32)
        for c in range(s // ts):
            a = c * ts
            n_ext = ts + POOL_HALO
            gate_e = _rows_after(gate_ref, a, ts, POOL_HALO)
            dy_e = _rows_after(dy_ref, a, ts, POOL_HALO)
            silu_e, dsilu_e = _silu_and_grad(gate_e)
            dms_e = dy_e * silu_e
            dm_e = (dms_e * scale).astype(BF16)
            dpool_e = _dot(dm_e, w, NT)
            spread = _leading_sums(dpool_e / _window_count(a, n_ext, group), group)
            dv_ref[a:a + ts, :] = (spread[0:ts, :] - dpool_e[0:ts, :]).astype(BF16)
            pooled = _pooled(v_ref, a, ts, group).astype(BF16)
            mixed = _dot(pooled, w, NN)
            dgate_ref[a:a + ts, :] = (dy_e[0:ts, :] * mixed * scale * dsilu_e[0:ts, :]).astype(BF16)
            dsc = dsc + jnp.sum(dms_e[0:ts, :] * mixed, axis=0, keepdims=True)
            dw = dw + _dot(pooled, dm_e[0:ts, :], TN)
        dw_ref[...] = dw
        dsc_ref[...] = dsc

    col = lambda base: pl.BlockSpec((s, LANE), lambda g: (0, base + g))
    out_col = pl.BlockSpec((s, LANE), lambda g: (0, g))
    return pl.pallas_call(
        body, name=name, grid=(4,),
        in_specs=[col(COL_POOL_V), col(COL_POOL_G), out_col,
                  pl.BlockSpec((None, LANE, LANE), lambda g: (g, 0, 0)),
                  pl.BlockSpec((1, LANE), lambda g: (0, g))],
        out_specs=[out_col, out_col,
                   pl.BlockSpec((None, LANE, LANE), lambda g: (g, 0, 0)),
                   pl.BlockSpec((1, LANE), lambda g: (0, g))],
        out_shape=[jax.ShapeDtypeStruct((s, WIDTH), BF16), jax.ShapeDtypeStruct((s, WIDTH), BF16),
                   jax.ShapeDtypeStruct((4, LANE, LANE), F32), jax.ShapeDtypeStruct((1, WIDTH), F32)],
    )(u, u, dy, pool_w, pool_scale)


def _conv_taps(x_ref, gc_ref, start, n):
    z_ext = _rows_before(gc_ref, start, n, CONV_HALO) * _rows_before(x_ref, start, n, CONV_HALO)
    z0 = z_ext[CONV_HALO:, :]
    z1 = pltpu.roll(z_ext, 1, 0)[CONV_HALO:, :]
    z2 = pltpu.roll(z_ext, 2, 0)[CONV_HALO:, :]
    return z0, z1, z2


def _conv_fwd(u, conv_w, conv_b, name, ts=512):
    s = u.shape[0]

    def body(x_ref, gb_ref, gc_ref, g_ref, w_ref, b_ref, y_ref):
        w0, w1, w2 = w_ref[0:1, :], w_ref[1:2, :], w_ref[2:3, :]
        for c in range(s // ts):
            a = c * ts
            z0, z1, z2 = _conv_taps(x_ref, gc_ref, a, ts)
            y = w2 * z0 + w1 * z1 + w0 * z2 + b_ref[...]
            gate = g_ref[a:a + ts, :]
            y_ref[a:a + ts, :] = (gb_ref[a:a + ts, :] * y * (gate * _sigmoid(gate))).astype(BF16)

    col = lambda base: pl.BlockSpec((s, LANE), lambda j: (0, base + j))
    return pl.pallas_call(
        body, name=name, grid=(4,),
        in_specs=[col(COL_CONV_X), col(COL_CONV_GB), col(COL_CONV_GC), col(COL_CONV_G),
                  pl.BlockSpec((3, LANE), lambda j: (0, j)), pl.BlockSpec((1, LANE), lambda j: (0, j))],
        out_specs=pl.BlockSpec((s, LANE), lambda j: (0, j)),
        out_shape=jax.ShapeDtypeStruct((s, WIDTH), BF16),
    )(u, u, u, u, conv_w, conv_b)


def _conv_bwd(u, dy, conv_w, conv_b, name, ts=512):
    s = u.shape[0]

    def body(x_ref, gb_ref, gc_ref, g_ref, dy_ref, w_ref, b_ref,
             dx_ref, dgb_ref, dgc_ref, dg_ref, dw_ref, db_ref):
        w0, w1, w2 = w_ref[0:1, :], w_ref[1:2, :], w_ref[2:3, :]
        acc = [jnp.zeros((1, LANE), F32) for _ in range(4)]
        for c in range(s // ts):
            a = c * ts
            n_ext = ts + CONV_HALO
            gate_e = _rows_after(g_ref, a, ts, CONV_HALO)
            silu_e, dsilu_e = _silu_and_grad(gate_e)
            dy_e = _rows_after(dy_ref, a, ts, CONV_HALO)
            gb_e = _rows_after(gb_ref, a, ts, CONV_HALO)
            dyy_e = dy_e * silu_e * gb_e
            dz = (w2 * dyy_e + w1 * pltpu.roll(dyy_e, n_ext - 1, 0) + w0 * pltpu.roll(dyy_e, n_ext - 2, 0))[0:ts, :]
            z0, z1, z2 = _conv_taps(x_ref, gc_ref, a, ts)
            yb = w2 * z0 + w1 * z1 + w0 * z2 + b_ref[...]
            dyv = dy_e[0:ts, :]
            dyy = dyy_e[0:ts, :]
            dg_ref[a:a + ts, :] = (dyv * gb_e[0:ts, :] * yb * dsilu_e[0:ts, :]).astype(BF16)
            dgb_ref[a:a + ts, :] = (dyv * silu_e[0:ts, :] * yb).astype(BF16)
            dx_ref[a:a + ts, :] = (dz * gc_ref[a:a + ts, :]).astype(BF16)
            dgc_ref[a:a + ts, :] = (dz * x_ref[a:a + ts, :]).astype(BF16)
            for i, term in enumerate((dyy * z2, dyy * z1, dyy * z0, dyy)):
                acc[i] = acc[i] + jnp.sum(term, axis=0, keepdims=True)
        dw_ref[0:1, :] = acc[0]
        dw_ref[1:2, :] = acc[1]
        dw_ref[2:3, :] = acc[2]
        db_ref[...] = acc[3]

    col = lambda base: pl.BlockSpec((s, LANE), lambda j: (0, base + j))
    out_col = pl.BlockSpec((s, LANE), lambda j: (0, j))
    big = jax.ShapeDtypeStruct((s, WIDTH), BF16)
    return pl.pallas_call(
        body, name=name, grid=(4,),
        in_specs=[col(COL_CONV_X), col(COL_CONV_GB), col(COL_CONV_GC), col(COL_CONV_G), out_col,
                  pl.BlockSpec((3, LANE), lambda j: (0, j)), pl.BlockSpec((1, LANE), lambda j: (0, j))],
        out_specs=[out_col, out_col, out_col, out_col,
                   pl.BlockSpec((3, LANE), lambda j: (0, j)), pl.BlockSpec((1, LANE), lambda j: (0, j))],
        out_shape=[big, big, big, big,
                   jax.ShapeDtypeStruct((3, WIDTH), F32), jax.ShapeDtypeStruct((1, WIDTH), F32)],
    )(u, u, u, u, dy, conv_w, conv_b)


def _split_bf16(x):
    hi = x.astype(BF16)
    return hi, (x - hi.astype(F32)).astype(BF16)


def _sb_scores(q_h, k_blk, valid, later_mat, carry):
    z = _dot(q_h, k_blk, NT)
    neg_z = -z
    soft = jnp.log(1.0 + jnp.exp(jnp.minimum(z, neg_z)))
    log_keep = jnp.minimum(neg_z, 0.0) - soft
    if valid is not None:
        log_keep = jnp.where(valid, log_keep, 0.0)
    log_beta = jnp.minimum(z, 0.0) - soft
    hi, lo = _split_bf16(log_keep)
    later = _dot(hi, later_mat, NN) + _dot(lo, later_mat, NN) + carry
    return log_keep, log_beta, later


def _masked(valid, x):
    return x if valid is None else jnp.where(valid, x, 0.0)


def _diagonal_masks(tq, tk):
    r = lax.broadcasted_iota(jnp.int32, (tq, tk), 0)
    cidx = lax.broadcasted_iota(jnp.int32, (tq, tk), 1)
    return [cidx + d * tk < r for d in range(tq // tk)]


def _triangle(tk, op):
    r = lax.broadcasted_iota(jnp.int32, (tk, tk), 0)
    cidx = lax.broadcasted_iota(jnp.int32, (tk, tk), 1)
    return op(r, cidx).astype(BF16)


def _sb_fwd(u, name, t=512, tk=256, pairs=2):
    s = u.shape[0]
    assert s // tk <= LANE and 4 % pairs == 0 and t % tk == 0
    scale = HEAD_DIM ** -0.5
    nh = 2 * pairs
    wide = pairs * LANE
    ratio = t // tk

    def body(q_ref, k_ref, v_ref, g_ref, o_ref, y_ref, after_ref, kb_ref, vb_ref, acc_ref, carry_ref):
        i = pl.program_id(1)

        @pl.when(i == 0)
        def _():
            kb_ref[...] = k_ref[...].astype(BF16)
            vb_ref[...] = v_ref[...].astype(BF16)

        lane = lax.broadcasted_iota(jnp.int32, (t, LANE), 1)
        first = lane < HEAD_DIM
        after_ref[...] = jnp.zeros_like(after_ref)
        qv = q_ref[...] * scale
        q_heads = []
        for p in range(pairs):
            qp = qv[:, p * LANE:(p + 1) * LANE]
            q_heads += [jnp.where(first, qp, 0.0).astype(BF16), jnp.where(first, 0.0, qp).astype(BF16)]
        later_mat = _triangle(tk, lambda r, cidx: r > cidx)
        acc_ref[...] = jnp.zeros_like(acc_ref)
        carry_ref[...] = jnp.zeros_like(carry_ref)

        def block(kb, valid):
            rows = pl.ds(pl.multiple_of(kb * tk, tk), tk)
            k_blk = kb_ref[rows, :]
            v_blk = vb_ref[rows, :]
            carries = [carry_ref[h] for h in range(nh)]
            afters = [after_ref[:, h * LANE:(h + 1) * LANE] for h in range(nh)]
            accs = [acc_ref[h] for h in range(nh)]
            outs = []
            for h in range(nh):
                cols = slice((h // 2) * LANE, (h // 2 + 1) * LANE)
                log_keep, log_beta, later = _sb_scores(q_heads[h], k_blk[:, cols], valid, later_mat, carries[h])
                a = _masked(valid, jnp.exp(log_beta + later))
                outs.append((accs[h] + _dot(a.astype(BF16), v_blk[:, cols], NN),
                             carries[h] + jnp.sum(log_keep, axis=1, keepdims=True),
                             jnp.where(lane == kb, carries[h], afters[h])))
            for h in range(nh):
                acc_ref[h] = outs[h][0]
                carry_ref[h] = outs[h][1]
                after_ref[:, h * LANE:(h + 1) * LANE] = outs[h][2]

        def step(j, _):
            block(ratio * i - 1 - j, None)
            return 0

        masks = _diagonal_masks(t, tk)
        for d in reversed(range(ratio)):
            block(ratio * i + d, masks[d])
        lax.fori_loop(0, ratio * i, step, 0)
        for p in range(pairs):
            cols = slice(p * LANE, (p + 1) * LANE)
            o = jnp.where(first, acc_ref[2 * p], acc_ref[2 * p + 1])
            o_ref[:, cols] = o
            gate = g_ref[:, cols]
            y_ref[:, cols] = (o * gate * _sigmoid(gate)).astype(BF16)

    blk = lambda base: pl.BlockSpec((t, wide), lambda g, i: (i, base // pairs + g))
    full = lambda base: pl.BlockSpec((s, wide), lambda g, i: (0, base // pairs + g))
    out_blk = pl.BlockSpec((t, wide), lambda g, i: (i, g))
    return pl.pallas_call(
        body, name=name, grid=(4 // pairs, s // t),
        in_specs=[blk(COL_SB_Q), full(COL_SB_K), full(COL_SB_V), blk(COL_SB_G)],
        out_specs=[out_blk, out_blk, pl.BlockSpec((t, nh * LANE), lambda g, i: (i, g))],
        out_shape=[jax.ShapeDtypeStruct((s, WIDTH), F32), jax.ShapeDtypeStruct((s, WIDTH), BF16),
                   jax.ShapeDtypeStruct((s, 8 * LANE), F32)],
        scratch_shapes=[pltpu.VMEM((s, wide), BF16), pltpu.VMEM((s, wide), BF16),
                        pltpu.VMEM((nh, t, LANE), F32), pltpu.VMEM((nh, t, 1), F32)],
        compiler_params=pltpu.CompilerParams(dimension_semantics=("arbitrary", "arbitrary")),
    )(u, u, u, u)


def _sb_bwd(u, o, after, dy, name, t=512, tk=256, pairs=2):
    s = u.shape[0]
    nq = s // t
    scale = HEAD_DIM ** -0.5
    nh = 2 * pairs
    wide = pairs * LANE
    ratio = t // tk

    def body(q_ref, k_ref, v_ref, g_ref, o_ref, after_ref, dy_ref, dq_ref, dk_ref, dv_ref, dg_ref,
             kb_ref, vb_ref, dk_acc, dv_acc, dq_acc, carry_ref):
        i = pl.program_id(1)

        @pl.when(i == 0)
        def _():
            kb_ref[...] = k_ref[...].astype(BF16)
            vb_ref[...] = v_ref[...].astype(BF16)
            dk_acc[...] = jnp.zeros_like(dk_acc)
            dv_acc[...] = jnp.zeros_like(dv_acc)

        lane = lax.broadcasted_iota(jnp.int32, (t, LANE), 1)
        first = lane < HEAD_DIM
        gate = g_ref[...]
        silu, dsilu = _silu_and_grad(gate)
        dyv = dy_ref[...]
        do = dyv * silu
        dg_ref[...] = (dyv * o_ref[...] * dsilu).astype(BF16)
        qv = q_ref[...] * scale
        do_heads, q_heads = [], []
        for p in range(pairs):
            cols = slice(p * LANE, (p + 1) * LANE)
            do_heads += [jnp.where(first, do[:, cols], 0.0).astype(BF16), jnp.where(first, 0.0, do[:, cols]).astype(BF16)]
            q_heads += [jnp.where(first, qv[:, cols], 0.0).astype(BF16), jnp.where(first, 0.0, qv[:, cols]).astype(BF16)]
        later_mat = _triangle(tk, lambda r, cidx: r > cidx)
        before_mat = _triangle(tk, lambda r, cidx: r < cidx)
        dq_acc[...] = jnp.zeros_like(dq_acc)
        carry_ref[...] = jnp.zeros_like(carry_ref)

        def block(kb, valid):
            rows = pl.ds(pl.multiple_of(kb * tk, tk), tk)
            k_blk = kb_ref[rows, :]
            v_blk = vb_ref[rows, :]
            carries = [carry_ref[h] for h in range(nh)]
            dq_old = [dq_acc[h] for h in range(nh)]
            dk_old = dk_acc[rows, :]
            dv_old = dv_acc[rows, :]
            outs = []
            for h in range(nh):
                cols = slice((h // 2) * LANE, (h // 2 + 1) * LANE)
                after = jnp.sum(jnp.where(lane == kb, after_ref[:, h * LANE:(h + 1) * LANE], 0.0), axis=1, keepdims=True)
                _, log_beta, later = _sb_scores(q_heads[h], k_blk[:, cols], valid, later_mat, after)
                beta = jnp.exp(log_beta)
                a = _masked(valid, jnp.exp(log_beta + later))
                da = _dot(do_heads[h], v_blk[:, cols], NT)
                gterm = a * da
                g_hi, g_lo = _split_bf16(gterm)
                before = _dot(g_hi, before_mat, NN) + _dot(g_lo, before_mat, NN) + carries[h]
                dz_b = _masked(valid, gterm * (1.0 - beta) - beta * before).astype(BF16)
                outs.append((dq_old[h] + _dot(dz_b, k_blk[:, cols], NN), _dot(dz_b, q_heads[h], TN),
                             _dot(a.astype(BF16), do_heads[h], TN),
                             carries[h] + jnp.sum(gterm, axis=1, keepdims=True)))
            for h in range(nh):
                dq_acc[h] = outs[h][0]
                carry_ref[h] = outs[h][3]
            dk_new = [outs[2 * p][1] + outs[2 * p + 1][1] for p in range(pairs)]
            dv_new = [outs[2 * p][2] + outs[2 * p + 1][2] for p in range(pairs)]
            dk_acc[rows, :] = dk_old + (dk_new[0] if pairs == 1 else jnp.concatenate(dk_new, axis=1))
            dv_acc[rows, :] = dv_old + (dv_new[0] if pairs == 1 else jnp.concatenate(dv_new, axis=1))

        def step(kb, _):
            block(kb, None)
            return 0

        lax.fori_loop(0, ratio * i, step, 0)
        masks = _diagonal_masks(t, tk)
        for d in range(ratio):
            block(ratio * i + d, masks[d])
        for p in range(pairs):
            dq_ref[:, p * LANE:(p + 1) * LANE] = (jnp.where(first, dq_acc[2 * p], dq_acc[2 * p + 1]) * scale).astype(BF16)

        @pl.when(i == nq - 1)
        def _():
            dk_ref[...] = dk_acc[...].astype(BF16)
            dv_ref[...] = dv_acc[...].astype(BF16)

    blk = lambda base: pl.BlockSpec((t, wide), lambda g, i: (i, base // pairs + g))
    full = lambda base: pl.BlockSpec((s, wide), lambda g, i: (0, base // pairs + g))
    out_blk = pl.BlockSpec((t, wide), lambda g, i: (i, g))
    out_full = pl.BlockSpec((s, wide), lambda g, i: (0, g))
    big = jax.ShapeDtypeStruct((s, WIDTH), BF16)
    return pl.pallas_call(
        body, name=name, grid=(4 // pairs, nq),
        in_specs=[blk(COL_SB_Q), full(COL_SB_K), full(COL_SB_V), blk(COL_SB_G), out_blk,
                  pl.BlockSpec((t, nh * LANE), lambda g, i: (i, g)), out_blk],
        out_specs=[out_blk, out_full, out_full, out_blk],
        out_shape=[big, big, big, big],
        scratch_shapes=[pltpu.VMEM((s, wide), BF16), pltpu.VMEM((s, wide), BF16),
                        pltpu.VMEM((s, wide), F32), pltpu.VMEM((s, wide), F32),
                        pltpu.VMEM((nh, t, LANE), F32), pltpu.VMEM((nh, t, 1), F32)],
        compiler_params=pltpu.CompilerParams(dimension_semantics=("arbitrary", "arbitrary")),
    )(u, u, u, u, o, after, dy)


def _gate_fwd(u, projs, name, ts=256):
    s = u.shape[0]

    def body(m0, m1, m2, p0, p1, p2, out_ref):
        tot = None
        for m_ref, p_ref in ((m0, p0), (m1, p1), (m2, p2)):
            term = _sigmoid(m_ref[...]) * p_ref[...].astype(F32)
            tot = term if tot is None else tot + term
        out_ref[...] = tot.astype(BF16)

    mspec = lambda n: pl.BlockSpec((ts, D_MODEL), lambda i: (i, COL_MERGE_1024 + n))
    row = pl.BlockSpec((ts, D_MODEL), lambda i: (i, 0))
    return pl.pallas_call(
        body, name=name, grid=(s // ts,),
        in_specs=[mspec(0), mspec(1), mspec(2), row, row, row],
        out_specs=row, out_shape=jax.ShapeDtypeStruct((s, D_MODEL), BF16),
    )(u, u, u, *projs)


def _gate_bwd(u, projs, dmerged, name, ts=256):
    s = u.shape[0]

    def body(m0, m1, m2, p0, p1, p2, dm_ref, dp0, dp1, dp2, dl0, dl1, dl2):
        dm = dm_ref[...].astype(F32)
        for m_ref, p_ref, dp_ref, dl_ref in ((m0, p0, dp0, dl0), (m1, p1, dp1, dl1), (m2, p2, dp2, dl2)):
            gate = _sigmoid(m_ref[...])
            dp_ref[...] = (dm * gate).astype(BF16)
            dl_ref[...] = (dm * p_ref[...].astype(F32) * gate * (1.0 - gate)).astype(BF16)

    mspec = lambda n: pl.BlockSpec((ts, D_MODEL), lambda i: (i, COL_MERGE_1024 + n))
    row = pl.BlockSpec((ts, D_MODEL), lambda i: (i, 0))
    big = jax.ShapeDtypeStruct((s, D_MODEL), BF16)
    outs = pl.pallas_call(
        body, name=name, grid=(s // ts,),
        in_specs=[mspec(0), mspec(1), mspec(2), row, row, row, row],
        out_specs=[row] * 6, out_shape=[big] * 6,
    )(u, u, u, *projs, dmerged)
    return outs[:3], outs[3:]


def _as_rows(a):
    return a.reshape(-1, a.shape[-1])


def _row_tile(rows, cols, bytes_per_row_elem=4, cap=1 << 20):
    tr = rows
    while tr * cols * bytes_per_row_elem > cap and tr % 2 == 0 and (tr // 2) % 16 == 0:
        tr //= 2
    return tr


def _cast_bf16(a, name):
    a2 = _as_rows(a)
    rows, cols = a2.shape
    tr = _row_tile(rows, cols)

    def body(a_ref, o_ref):
        o_ref[...] = a_ref[...].astype(BF16)

    spec = pl.BlockSpec((tr, cols), lambda i: (i, 0))
    out = pl.pallas_call(body, name=name, grid=(rows // tr,), in_specs=[spec], out_specs=spec,
                         out_shape=jax.ShapeDtypeStruct((rows, cols), BF16))(a2)
    return out.reshape(a.shape)


def _adamw(w, g, m, v, name):
    shape = w.shape
    w2, g2, m2, v2 = (_as_rows(a) for a in (w, g, m, v))
    rows, cols = w2.shape
    tr = _row_tile(rows, cols)
    c1 = 1.0 - ADAM_B1 ** ADAM_STEP
    c2 = 1.0 - ADAM_B2 ** ADAM_STEP

    def body(w_ref, g_ref, m_ref, v_ref, d_ref, nm_ref, nv_ref):
        gv = g_ref[...]
        nm = ADAM_B1 * m_ref[...] + (1.0 - ADAM_B1) * gv
        nv = ADAM_B2 * v_ref[...] + (1.0 - ADAM_B2) * (gv * gv)
        nm_ref[...] = nm
        nv_ref[...] = nv
        d_ref[...] = -ADAM_LR * ((nm / c1) / (jnp.sqrt(nv / c2) + ADAM_EPS) + ADAM_WD * w_ref[...])

    spec = pl.BlockSpec((tr, cols), lambda i: (i, 0))
    sds = jax.ShapeDtypeStruct((rows, cols), F32)
    outs = pl.pallas_call(body, name=name, grid=(rows // tr,), in_specs=[spec] * 4, out_specs=[spec] * 3,
                          out_shape=[sds] * 3)(w2, g2, m2, v2)
    return tuple(o.reshape(shape) for o in outs)


def _sum_slots(a, out_dtype, name):
    n = a.shape[0]
    a3 = a.reshape(n, -1, a.shape[-1])
    _, rows, cols = a3.shape
    tr = _row_tile(rows, cols * n)

    def body(a_ref, o_ref):
        tot = a_ref[0].astype(F32)
        for k in range(1, n):
            tot = tot + a_ref[k].astype(F32)
        o_ref[...] = tot.astype(out_dtype)

    out = pl.pallas_call(
        body, name=name, grid=(rows // tr,),
        in_specs=[pl.BlockSpec((n, tr, cols), lambda i: (0, i, 0))],
        out_specs=pl.BlockSpec((tr, cols), lambda i: (i, 0)),
        out_shape=jax.ShapeDtypeStruct((rows, cols), out_dtype))(a3)
    return out.reshape(a.shape[1:])


def _chip_sum(own, recv, axis, core, name):
    half = recv.shape
    nd = len(half)
    last = nd - 1
    if axis == last:
        tl, nt = half[last], 1
    else:
        tl = min(half[last], 2048)
        nt = half[last] // tl
    block = half[:last] + (tl,)

    def own_index(i, core_ref):
        idx = [0] * nd
        idx[last] = i
        if axis == last:
            idx[last] = core_ref[0]
        else:
            idx[axis] = core_ref[0]
        return tuple(idx)

    def recv_index(i, core_ref):
        idx = [0] * nd
        idx[last] = i
        return tuple(idx)

    def body(core_ref, own_ref, recv_ref, o_ref):
        o_ref[...] = (own_ref[...] + recv_ref[...]).astype(BF16)

    return pl.pallas_call(
        body, name=name,
        grid_spec=pltpu.PrefetchScalarGridSpec(
            num_scalar_prefetch=1, grid=(nt,),
            in_specs=[pl.BlockSpec(block, own_index), pl.BlockSpec(block, recv_index)],
            out_specs=pl.BlockSpec(block, recv_index)),
        out_shape=jax.ShapeDtypeStruct(half, BF16),
    )(core, own, recv)


def _mesh_position():
    return lax.axis_index("x"), lax.axis_index("y"), lax.axis_index("c")


def _other_chips(x, y):
    return [(1 - x, y), (x, 1 - y), (1 - x, 1 - y)]


ALL_FLIPS = [(0, 0, 1), (1, 0, 0), (0, 1, 0), (1, 1, 0), (1, 0, 1), (0, 1, 1), (1, 1, 1)]


def _half(ref, axis, which, size):
    idx = [slice(None)] * len(ref.shape)
    idx[axis] = pl.ds(which * size, size)
    return ref.at[tuple(idx)]


def _sub(ref, picks):
    idx = [slice(None)] * len(ref.shape)
    for axis, start, size in picks:
        idx[axis] = pl.ds(start, size)
    return ref.at[tuple(idx)]


def _remote(src, dst, sems_send, sems_recv, k, to):
    return pltpu.make_async_remote_copy(src_ref=src, dst_ref=dst, send_sem=sems_send.at[k], recv_sem=sems_recv.at[k],
                                        device_id=to, device_id_type=MESH)


def _cast_shard(w, shard_axis, pos, name, tr=512):
    shape = w.shape
    nd = len(shape)
    assert shard_axis in (nd - 1, nd - 2)
    rows, cols = shape[-2:]
    tr = min(tr, rows)
    nt = rows // tr
    lead = shape[:-2]
    full = list(shape)
    full[shard_axis] *= N_CHIPS
    block = (1,) * len(lead) + (tr, cols)

    def in_index(*args):
        return (*args[:-1], 0)

    def out_index(*args):
        *g, pos_ref = args
        if shard_axis == nd - 1:
            return (*g, pos_ref[1])
        return (*g[:-1], pos_ref[1] * nt + g[-1], 0)

    def body(pos_ref, a_ref, o_ref):
        o_ref[...] = a_ref[...].astype(BF16)

    return pl.pallas_call(
        body, name=name,
        grid_spec=pltpu.PrefetchScalarGridSpec(
            num_scalar_prefetch=1, grid=lead + (nt,),
            in_specs=[pl.BlockSpec(block, in_index)], out_specs=pl.BlockSpec(block, out_index)),
        out_shape=jax.ShapeDtypeStruct(tuple(full), BF16),
    )(pos, w)


def _gather_weights(w_in, w_branch, w_out, conv_w):
    layout = [(2, w_in.shape[2] // N_CHIPS, 1, w_in.shape[1] // 2),
              (3, w_branch.shape[3] // N_CHIPS, 2, w_branch.shape[2] // 2),
              (1, w_out.shape[1] // N_CHIPS, 2, w_out.shape[2] // 2)]
    n_big = len(layout)

    def body(win_in, wbr_in, wout_in, cw, win_f, wbr_f, wout_f, cw_f, send_sems, recv_sems, local_sems):
        x, y, c = _mesh_position()
        chips = _other_chips(x, y)
        sibling = (x, y, 1 - c)
        mine = 2 * x + y
        fulls = [win_f, wbr_f, wout_f]

        def place(t, chip, core):
            sh_axis, sh_size, half_axis, half_size = layout[t]
            return _sub(fulls[t], [(sh_axis, chip * sh_size, sh_size), (half_axis, core * half_size, half_size)])

        local = [pltpu.make_async_copy(cw, cw_f.at[mine], local_sems.at[0])]
        for cp in local:
            cp.start()

        sends = []
        for t in range(n_big):
            for k, chip in enumerate(chips):
                sends.append(_remote(place(t, mine, c), place(t, mine, c),
                                     send_sems, recv_sems, 6 * t + k, (*chip, c)))
        small_base = 6 * n_big
        for k, chip in enumerate(chips):
            sends.append(_remote(cw, cw_f.at[mine], send_sems, recv_sems, small_base + k, (*chip, c)))
        for cp in sends:
            cp.start()

        passed = []
        for t in range(n_big):
            for k, (px, py) in enumerate(chips):
                theirs = 2 * px + py
                _remote(place(t, theirs, c), place(t, theirs, c), send_sems, recv_sems, 6 * t + k, (px, py, c)).wait_recv()
                fwd = _remote(place(t, theirs, c), place(t, theirs, c), send_sems, recv_sems, 6 * t + 3 + k, sibling)
                fwd.start()
                passed.append(fwd)
        for t in range(n_big):
            for k, (px, py) in enumerate(chips):
                theirs = 2 * px + py
                _remote(place(t, theirs, 1 - c), place(t, theirs, 1 - c), send_sems, recv_sems, 6 * t + 3 + k,
                        sibling).wait_recv()
        for k, (px, py) in enumerate(chips):
            _remote(cw, cw_f.at[2 * px + py], send_sems, recv_sems, small_base + k, sibling).wait_recv()
        for cp in sends + passed:
            cp.wait_send()
        for cp in local:
            cp.wait()

    n_sems = 6 * n_big + 3
    out_shape = [jax.ShapeDtypeStruct(a.shape, BF16) for a in (w_in, w_branch, w_out)]
    out_shape.append(jax.ShapeDtypeStruct((N_CHIPS,) + conv_w.shape, F32))
    return pl.pallas_call(
        body, name="gather_weights",
        in_specs=[ANY] * 4, out_specs=[ANY] * 4, out_shape=out_shape,
        input_output_aliases={0: 0, 1: 1, 2: 2},
        scratch_shapes=[pltpu.SemaphoreType.DMA((n_sems,)), pltpu.SemaphoreType.DMA((n_sems,)),
                        pltpu.SemaphoreType.DMA((1,))],
    )(w_in, w_branch, w_out, conv_w)


def _swap_halves(items, name):
    n = len(items)
    halves = []
    for a, axis in items:
        shp = list(a.shape)
        shp[axis] //= 2
        halves.append(tuple(shp))

    def body(*refs):
        srcs, dsts, (send_sems, recv_sems) = refs[:n], refs[n:2 * n], refs[2 * n:]
        x, y, c = _mesh_position()
        copies = []
        for k in range(n):
            axis = items[k][1]
            copies.append(_remote(_half(srcs[k], axis, 1 - c, halves[k][axis]), dsts[k], send_sems, recv_sems, k,
                                  (x, y, 1 - c)))
        for cp in copies:
            cp.start()
        for cp in copies:
            cp.wait()

    return pl.pallas_call(
        body, name=name, in_specs=[ANY] * n, out_specs=[ANY] * n,
        out_shape=[jax.ShapeDtypeStruct(h, a.dtype) for h, (a, _) in zip(halves, items)],
        scratch_shapes=[pltpu.SemaphoreType.DMA((n,)), pltpu.SemaphoreType.DMA((n,))],
    )(*[a for a, _ in items])


def _exchange_grads(items, small):
    n = len(items)
    slices = []
    for a, axis in items:
        shp = list(a.shape)
        shp[axis] //= N_CHIPS
        slices.append(tuple(shp))

    def body(*refs):
        srcs, small_ref = refs[:n], refs[n]
        dsts, small_all = refs[n + 1:2 * n + 1], refs[2 * n + 1]
        send_sems, recv_sems, local_sems = refs[2 * n + 2:]
        x, y, c = _mesh_position()
        chips = _other_chips(x, y)
        mine = 2 * x + y
        me = 4 * x + 2 * y + c

        def piece(k, chip):
            axis = items[k][1]
            return _half(srcs[k], axis, chip, slices[k][axis])

        local = [pltpu.make_async_copy(small_ref, small_all.at[me], local_sems.at[0])]
        for cp in local:
            cp.start()
        copies = []
        for k in range(n):
            for r, (px, py) in enumerate(chips):
                copies.append(_remote(piece(k, 2 * px + py), dsts[k].at[r], send_sems, recv_sems, 3 * k + r, (px, py, c)))
        for r, (fx, fy, fc) in enumerate(ALL_FLIPS):
            copies.append(_remote(small_ref, small_all.at[me], send_sems, recv_sems, 3 * n + r,
                                  (x ^ fx, y ^ fy, c ^ fc)))
        for cp in copies:
            cp.start()
        for cp in copies:
            cp.wait()
        for cp in local:
            cp.wait()

    n_sems = 3 * n + len(ALL_FLIPS)
    out_shape = [jax.ShapeDtypeStruct((N_CHIPS - 1,) + s, BF16) for s in slices]
    out_shape.append(jax.ShapeDtypeStruct((2 * N_CHIPS,) + small.shape, F32))
    outs = pl.pallas_call(
        body, name="exchange_grads", in_specs=[ANY] * (n + 1), out_specs=[ANY] * (n + 1), out_shape=out_shape,
        scratch_shapes=[pltpu.SemaphoreType.DMA((n_sems,)), pltpu.SemaphoreType.DMA((n_sems,)),
                        pltpu.SemaphoreType.DMA((1,))],
    )(*[a for a, _ in items], small)
    return outs[:n], outs[n]


def _sum_chips(recv, own, shard_axis, split_axis, pos, dest, layer, name, tr=128):
    sl = recv.shape[1:]
    nd = len(sl)
    tiled = nd == 2 and sl[0] > tr
    nt = sl[0] // tr if tiled else 1
    block = ((tr,) + sl[1:]) if tiled else sl
    shard = list(sl)
    shard[split_axis] *= 2

    def recv_index(i, pos_ref):
        return (0, i) + (0,) * (nd - 1) if tiled else (0,) * (nd + 1)

    def own_index(i, pos_ref):
        idx = [0] * nd
        idx[shard_axis] = pos_ref[1]
        if tiled:
            idx[0] = pos_ref[1] * nt + i if shard_axis == 0 else i
        return tuple(idx)

    def out_index(i, pos_ref):
        idx = [0] * nd
        idx[split_axis] = pos_ref[0]
        if tiled:
            idx[0] = pos_ref[0] * nt + i if split_axis == 0 else i
        return (layer, *idx)

    def body(pos_ref, recv_ref, own_ref, *rest):
        o_ref = rest[-1]
        tot = own_ref[...].astype(F32)
        for k in range(N_CHIPS - 1):
            tot = tot + recv_ref[k].astype(F32)
        o_ref[0] = tot

    in_specs = [pl.BlockSpec((N_CHIPS - 1,) + block, recv_index), pl.BlockSpec(block, own_index)]
    args = [pos, recv, own]
    aliases = {}
    if dest is not None:
        in_specs.append(ANY)
        args.append(dest)
        aliases = {3: 0}
    return pl.pallas_call(
        body, name=name,
        grid_spec=pltpu.PrefetchScalarGridSpec(
            num_scalar_prefetch=1, grid=(nt,), in_specs=in_specs,
            out_specs=pl.BlockSpec((1,) + block, out_index)),
        out_shape=jax.ShapeDtypeStruct((DEPTH,) + tuple(shard), F32),
        input_output_aliases=aliases,
    )(*args)


def _share_halves(bufs, name):
    n = len(bufs)

    def body(*refs):
        outs, (send_sems, recv_sems) = refs[n:2 * n], refs[2 * n:]
        x, y, c = _mesh_position()
        copies = []
        for k, (a, axis) in enumerate(bufs):
            size = a.shape[1 + axis] // 2
            mine = _half(outs[k], 1 + axis, c, size)
            copies.append(_remote(mine, mine, send_sems, recv_sems, k, (x, y, 1 - c)))
        for cp in copies:
            cp.start()
        for cp in copies:
            cp.wait()

    return pl.pallas_call(
        body, name=name, in_specs=[ANY] * n, out_specs=[ANY] * n,
        out_shape=[jax.ShapeDtypeStruct(a.shape, F32) for a, _ in bufs],
        input_output_aliases={k: k for k in range(n)},
        scratch_shapes=[pltpu.SemaphoreType.DMA((n,)), pltpu.SemaphoreType.DMA((n,))],
    )(*[a for a, _ in bufs])


def _layer_fwd(x, p, l):
    tag = f"l{l}_"
    h = _rms_fwd(x, p["pre_g"], tag + "pre_norm")
    u = _matmul(h, p["w_in"], "nn", F32, tag + "in_proj")
    y_pool = _pool_fwd(u, p["pool_w"], p["pool_scale"], tag + "pool")
    y_conv = _conv_fwd(u, p["conv_w"], p["conv_b"], tag + "conv")
    o_sb, y_sb, sb_after = _sb_fwd(u, tag + "stickbreak")
    ys = [y_pool, y_conv, y_sb]
    projs = [_matmul(ys[n], p["w_branch"][n], "nn", BF16, tag + f"branch_proj{n}") for n in range(3)]
    merged = _gate_fwd(u, projs, tag + "merge")
    out = _matmul(merged, p["w_out"], "nn", F32, tag + "out_proj")
    return out, dict(x=x, h=h, u=u, ys=ys, o_sb=o_sb, sb_after=sb_after, projs=projs, merged=merged, out=out)


def _layer_bwd(dy, p, saved, l):
    tag = f"l{l}_bwd_"
    u = saved["u"]
    d_out, g_post = _rms_bwd(saved["out"], p["post_g"], dy, None, BF16, tag + "post_norm")
    d_merged = _matmul(d_out, p["w_out"], "nt", BF16, tag + "out_proj_dx")
    g_w_out = _matmul(saved["merged"], d_out, "tn", F32, tag + "out_proj_dw")
    d_projs, d_logits = _gate_bwd(u, saved["projs"], d_merged, tag + "merge")
    g_w_branch = [_matmul(saved["ys"][n], d_projs[n], "tn", F32, tag + f"branch_dw{n}") for n in range(3)]
    d_ys = [_matmul(d_projs[n], p["w_branch"][n], "nt", F32, tag + f"branch_dx{n}") for n in range(3)]
    d_pv, d_pg, g_pool_w, g_pool_scale = _pool_bwd(u, d_ys[0], p["pool_w"], p["pool_scale"], tag + "pool")
    d_cx, d_cgb, d_cgc, d_cg, g_conv_w, g_conv_b = _conv_bwd(u, d_ys[1], p["conv_w"], p["conv_b"], tag + "conv")
    d_q, d_k, d_v, d_sg = _sb_bwd(u, saved["o_sb"], saved["sb_after"], d_ys[2], tag + "stickbreak")
    du = jnp.concatenate([d_pv, d_pg, d_cx, d_cgb, d_cgc, d_cg, d_q, d_k, d_v, d_sg] + list(d_logits), axis=1)
    g_w_in = _matmul(saved["h"], du, "tn", F32, tag + "in_proj_dw", tk=512)
    dh = _matmul(du, p["w_in"], "nt", BF16, tag + "in_proj_dx")
    dx, g_pre = _rms_bwd(saved["x"], p["pre_g"], dh, dy, F32, tag + "pre_norm")
    grads = dict(w_in=g_w_in, w_branch=jnp.stack(g_w_branch), w_out=g_w_out, pre_g=g_pre, post_g=g_post,
                 pool_w=g_pool_w, pool_scale=g_pool_scale, conv_w=g_conv_w, conv_b=g_conv_b)
    return dx, grads


SMALL_ORDER = ["pre_g", "pool_w", "pool_scale", "conv_w", "conv_b", "post_g"]


def _pack_small(per_layer):
    parts, spans, at = [], {}, 0
    for name in SMALL_ORDER:
        a = jnp.stack([per_layer[l][name] for l in range(DEPTH)]).reshape(-1, LANE)
        parts.append(a)
        spans[name] = (at, a.shape[0])
        at += a.shape[0]
    return jnp.concatenate(parts, axis=0), spans


def kernel(x, pre_norm_g, w_in, pool_w, pool_scale, conv_w, conv_b, w_branch, w_out, post_norm_g, loss_target, m_pre_norm_g, m_w_in, m_pool_w, m_pool_scale, m_conv_w, m_conv_b, m_w_branch, m_w_out, m_post_norm_g, v_pre_norm_g, v_w_in, v_pool_w, v_pool_scale, v_conv_w, v_conv_b, v_w_branch, v_w_out, v_post_norm_g):
    mx, my, mc = _mesh_position()
    chip = 2 * mx + my
    core = mc.astype(jnp.int32).reshape(1)
    pos = jnp.stack([mc, chip]).astype(jnp.int32)

    w_in_f, w_branch_f, w_out_f, conv_w_by_chip = _gather_weights(
        _cast_shard(w_in, 2, pos, "cast_w_in"), _cast_shard(w_branch, 3, pos, "cast_w_branch"),
        _cast_shard(w_out, 1, pos, "cast_w_out"), conv_w)
    conv_w_f = conv_w_by_chip.transpose(1, 2, 0, 3).reshape(DEPTH, 3, WIDTH)
    pool_w_b = _cast_bf16(pool_w, "cast_pool_w")
    params = []
    for l in range(DEPTH):
        params.append(dict(
            pre_g=pre_norm_g[l:l + 1], post_g=post_norm_g[l:l + 1], w_in=w_in_f[l], w_branch=w_branch_f[l],
            w_out=w_out_f[l], pool_w=pool_w_b[l], pool_scale=pool_scale[l:l + 1], conv_w=conv_w_f[l],
            conv_b=conv_b[l:l + 1]))

    act = x[0]
    saved = []
    for l in range(DEPTH):
        out, sv = _layer_fwd(act, params[l], l)
        saved.append(sv)
        if l < DEPTH - 1:
            act = _resid_out(act, out, params[l]["post_g"], None, f"l{l}_resid")
    dy, loss_part = _resid_out(act, saved[-1]["out"], params[-1]["post_g"], loss_target[0], "loss_head")
    loss = lax.psum(loss_part[0, 0], ("x", "y", "c"))

    grads = [None] * DEPTH
    for l in reversed(range(DEPTH)):
        dy, grads[l] = _layer_bwd(dy, params[l], saved[l], l)
    grad_x = dy[None]

    split_axis = dict(w_in=0, w_branch=1, w_out=1)
    shard_axis = dict(w_in=1, w_branch=2, w_out=0)
    names = ["w_in", "w_branch", "w_out"]
    items = [(grads[l][n], split_axis[n]) for l in range(DEPTH) for n in names]
    from_sibling = _swap_halves(items, "swap_grad_halves")
    chip_sums = [(_chip_sum(a, r, axis, core, f"chip_sum{k}"), shard_axis[names[k % 3]])
                 for k, ((a, axis), r) in enumerate(zip(items, from_sibling))]
    small_part, spans = _pack_small(grads)
    by_chip, small_all = _exchange_grads(chip_sums, small_part)
    bufs = []
    for i, n in enumerate(names):
        dest = None
        for l in range(DEPTH):
            k = l * 3 + i
            dest = _sum_chips(by_chip[k], chip_sums[k][0], shard_axis[n], split_axis[n], pos, dest, l, f"sum_chips{k}")
        bufs.append((dest, split_axis[n]))
    g_w_in, g_w_branch, g_w_out = _share_halves(bufs, "share_grad_halves")

    small_sum = _sum_slots(small_all, F32, "sum_small")
    small = {}
    for name, like in (("pre_g", pre_norm_g), ("pool_w", pool_w), ("pool_scale", pool_scale), ("conv_b", conv_b),
                       ("post_g", post_norm_g)):
        at, n = spans[name]
        small[name] = small_sum[at:at + n].reshape(like.shape)
    at, n = spans["conv_w"]
    g_conv_w_full = small_sum[at:at + n].reshape(DEPTH, 3, WIDTH)
    g_conv_w = lax.dynamic_slice_in_dim(g_conv_w_full, chip * conv_w.shape[2], conv_w.shape[2], axis=2)

    g = dict(pre_norm_g=small["pre_g"], w_in=g_w_in, pool_w=small["pool_w"], pool_scale=small["pool_scale"],
             conv_w=g_conv_w, conv_b=small["conv_b"], w_branch=g_w_branch, w_out=g_w_out, post_norm_g=small["post_g"])
    w = dict(pre_norm_g=pre_norm_g, w_in=w_in, pool_w=pool_w, pool_scale=pool_scale, conv_w=conv_w, conv_b=conv_b,
             w_branch=w_branch, w_out=w_out, post_norm_g=post_norm_g)
    m = dict(pre_norm_g=m_pre_norm_g, w_in=m_w_in, pool_w=m_pool_w, pool_scale=m_pool_scale, conv_w=m_conv_w,
             conv_b=m_conv_b, w_branch=m_w_branch, w_out=m_w_out, post_norm_g=m_post_norm_g)
    v = dict(pre_norm_g=v_pre_norm_g, w_in=v_w_in, pool_w=v_pool_w, pool_scale=v_pool_scale, conv_w=v_conv_w,
             conv_b=v_conv_b, w_branch=v_w_branch, w_out=v_w_out, post_norm_g=v_post_norm_g)
    order = ["pre_norm_g", "w_in", "pool_w", "pool_scale", "conv_w", "conv_b", "w_branch", "w_out", "post_norm_g"]
    upd = {n: _adamw(w[n], g[n], m[n], v[n], "adamw_" + n) for n in order}
    return (loss, grad_x, *[g[n] for n in order], *[upd[n][0] for n in order], *[upd[n][1] for n in order],
            *[upd[n][2] for n in order])
```

```python
import functools

import jax
import jax.numpy as jnp
from jax import lax
from jax.experimental import pallas as pl
from jax.experimental.pallas import tpu as pltpu

F32 = jnp.float32
BF16 = jnp.bfloat16
MESH = pl.DeviceIdType.MESH
ANY = pl.BlockSpec(memory_space=pl.ANY)

DEPTH = 2
D_MODEL = 1024
WIDTH = 512
N_IN = 8192
N_CHIPS = 4
HEAD_DIM = 64
RMS_EPS = 1e-6
POOL_HALO = 16
CONV_HALO = 8
LANE = 128
COL_POOL_V, COL_POOL_G = 0, 4
COL_CONV_X, COL_CONV_GB, COL_CONV_GC, COL_CONV_G = 8, 12, 16, 20
COL_SB_Q, COL_SB_K, COL_SB_V, COL_SB_G = 24, 28, 32, 36
COL_MERGE_1024 = 5

ADAM_LR, ADAM_B1, ADAM_B2, ADAM_EPS, ADAM_WD, ADAM_STEP = 0.001, 0.9, 0.999, 1e-08, 0.01, 10

NN = (((1,), (0,)), ((), ()))
NT = (((1,), (1,)), ((), ()))
TN = (((0,), (0,)), ((), ()))


def _sigmoid(x):
    return 1.0 / (1.0 + jnp.exp(-x))


def _silu_and_grad(x):
    s = _sigmoid(x)
    return x * s, s * (1.0 + x * (1.0 - s))


def _dot(a, b, dims):
    return lax.dot_general(a, b, dims, preferred_element_type=F32)


def _matmul(a, b, mode, out_dtype, name, tm=1024, tn=1024, tk=1024):
    if mode == "nn":
        (m, k), (k2, n) = a.shape, b.shape
    elif mode == "nt":
        (m, k), (n, k2) = a.shape, b.shape
    else:
        (k, m), (k2, n) = a.shape, b.shape
    assert k == k2 and a.dtype == BF16 and b.dtype == BF16
    tm, tn, tk = min(tm, m), min(tn, n), min(tk, k)
    assert m % tm == 0 and n % tn == 0 and k % tk == 0
    nk = k // tk
    dims = {"nn": NN, "nt": NT, "tn": TN}[mode]

    def body(a_ref, b_ref, o_ref, *scratch):
        p = _dot(a_ref[...], b_ref[...], dims)
        if nk == 1:
            o_ref[...] = p.astype(o_ref.dtype)
        else:
            acc = scratch[0]
            kk = pl.program_id(2)

            @pl.when(kk == 0)
            def _():
                acc[...] = p

            @pl.when(kk > 0)
            def _():
                acc[...] += p

            @pl.when(kk == nk - 1)
            def _():
                o_ref[...] = acc[...].astype(o_ref.dtype)

    if mode == "tn":
        a_spec = pl.BlockSpec((tk, tm), lambda i, j, kk: (kk, i))
    else:
        a_spec = pl.BlockSpec((tm, tk), lambda i, j, kk: (i, kk))
    if mode == "nt":
        b_spec = pl.BlockSpec((tn, tk), lambda i, j, kk: (j, kk))
    else:
        b_spec = pl.BlockSpec((tk, tn), lambda i, j, kk: (kk, j))
    return pl.pallas_call(
        body, name=name,
        grid=(m // tm, n // tn, nk),
        in_specs=[a_spec, b_spec],
        out_specs=pl.BlockSpec((tm, tn), lambda i, j, kk: (i, j)),
        out_shape=jax.ShapeDtypeStruct((m, n), out_dtype),
        scratch_shapes=[pltpu.VMEM((tm, tn), F32)] if nk > 1 else [],
        compiler_params=pltpu.CompilerParams(dimension_semantics=("parallel", "parallel", "arbitrary")),
    )(a, b)


def _rms_fwd(x, g, name, ts=512):
    s, d = x.shape

    def body(x_ref, g_ref, h_ref):
        xv = x_ref[...]
        r = lax.rsqrt(jnp.mean(xv * xv, axis=-1, keepdims=True) + RMS_EPS)
        h_ref[...] = (xv * r * g_ref[...]).astype(BF16)

    return pl.pallas_call(
        body, name=name, grid=(s // ts,),
        in_specs=[pl.BlockSpec((ts, d), lambda i: (i, 0)), pl.BlockSpec((1, d), lambda i: (0, 0))],
        out_specs=pl.BlockSpec((ts, d), lambda i: (i, 0)),
        out_shape=jax.ShapeDtypeStruct((s, d), BF16),
    )(x, g)


def _rms_bwd(xin, g, dh, resid, out_dtype, name, ts=512):
    s, d = xin.shape
    has_resid = resid is not None

    def body(*refs):
        if has_resid:
            x_ref, g_ref, dh_ref, res_ref, dx_ref, dg_ref = refs
        else:
            x_ref, g_ref, dh_ref, dx_ref, dg_ref = refs
        xv = x_ref[...]
        dhv = dh_ref[...].astype(F32)
        r = lax.rsqrt(jnp.mean(xv * xv, axis=-1, keepdims=True) + RMS_EPS)
        nrm = xv * r
        dn = dhv * g_ref[...]
        dx = r * (dn - nrm * jnp.mean(dn * nrm, axis=-1, keepdims=True))
        if has_resid:
            dx = dx + res_ref[...]
        dx_ref[...] = dx.astype(dx_ref.dtype)
        part = jnp.sum(dhv * nrm, axis=0, keepdims=True)

        @pl.when(pl.program_id(0) == 0)
        def _():
            dg_ref[...] = part

        @pl.when(pl.program_id(0) > 0)
        def _():
            dg_ref[...] += part

    row = pl.BlockSpec((ts, d), lambda i: (i, 0))
    vec = pl.BlockSpec((1, d), lambda i: (0, 0))
    ins = [xin, g, dh] + ([resid] if has_resid else [])
    return pl.pallas_call(
        body, name=name, grid=(s // ts,),
        in_specs=[row, vec, row] + ([row] if has_resid else []),
        out_specs=[row, vec],
        out_shape=[jax.ShapeDtypeStruct((s, d), out_dtype), jax.ShapeDtypeStruct((1, d), F32)],
        compiler_params=pltpu.CompilerParams(dimension_semantics=("arbitrary",)),
    )(*ins)


def _resid_out(x, out, g, target, name, ts=512):
    s, d = x.shape
    has_loss = target is not None

    def body(*refs):
        if has_loss:
            x_ref, o_ref, g_ref, t_ref, dy_ref, loss_ref = refs
        else:
            x_ref, o_ref, g_ref, y_ref = refs
        ov = o_ref[...]
        r = lax.rsqrt(jnp.mean(ov * ov, axis=-1, keepdims=True) + RMS_EPS)
        yv = x_ref[...] + ov * r * g_ref[...]
        if not has_loss:
            y_ref[...] = yv
            return
        err = yv - t_ref[...]
        dy_ref[...] = err * (1.0 / d)
        part = jnp.sum(jnp.sum(err * err, axis=-1, keepdims=True), axis=0, keepdims=True) * (0.5 / d)
        part = jnp.broadcast_to(part, (1, LANE))

        @pl.when(pl.program_id(0) == 0)
        def _():
            loss_ref[...] = part

        @pl.when(pl.program_id(0) > 0)
        def _():
            loss_ref[...] += part

    row = pl.BlockSpec((ts, d), lambda i: (i, 0))
    vec = pl.BlockSpec((1, d), lambda i: (0, 0))
    if has_loss:
        return pl.pallas_call(
            body, name=name, grid=(s // ts,),
            in_specs=[row, row, vec, row],
            out_specs=[row, pl.BlockSpec((1, LANE), lambda i: (0, 0))],
            out_shape=[jax.ShapeDtypeStruct((s, d), F32), jax.ShapeDtypeStruct((1, LANE), F32)],
            compiler_params=pltpu.CompilerParams(dimension_semantics=("arbitrary",)),
        )(x, out, g, target)
    return pl.pallas_call(
        body, name=name, grid=(s // ts,),
        in_specs=[row, row, vec], out_specs=row,
        out_shape=jax.ShapeDtypeStruct((s, d), F32),
    )(x, out, g)


def _rows_before(ref, start, n, halo):
    if start == 0:
        return jnp.concatenate([jnp.zeros((halo, ref.shape[1]), F32), ref[0:n, :]], axis=0)
    return ref[start - halo:start + n, :]


def _rows_after(ref, start, n, halo):
    if start + n == ref.shape[0]:
        return jnp.concatenate([ref[start:start + n, :].astype(F32), jnp.zeros((halo, ref.shape[1]), F32)], axis=0)
    return ref[start:start + n + halo, :].astype(F32)


def _pick_window(group, s2, s4, s8, s16):
    return jnp.where(group == 0, s2, jnp.where(group == 1, s4, jnp.where(group == 2, s8, s16)))


def _trailing_sums(ext, group):
    s2 = ext + pltpu.roll(ext, 1, 0)
    s4 = s2 + pltpu.roll(s2, 2, 0)
    s8 = s4 + pltpu.roll(s4, 4, 0)
    s16 = s8 + pltpu.roll(s8, 8, 0)
    return _pick_window(group, s2, s4, s8, s16)


def _leading_sums(ext, group):
    n = ext.shape[0]
    s2 = ext + pltpu.roll(ext, n - 1, 0)
    s4 = s2 + pltpu.roll(s2, n - 2, 0)
    s8 = s4 + pltpu.roll(s4, n - 4, 0)
    s16 = s8 + pltpu.roll(s8, n - 8, 0)
    return _pick_window(group, s2, s4, s8, s16)


def _window_count(start, n, group):
    pos = start + lax.broadcasted_iota(jnp.int32, (n, LANE), 0)
    return jnp.minimum(pos + 1, 2 << group).astype(F32)


def _pooled(v_ref, start, n, group):
    ext = _rows_before(v_ref, start, n, POOL_HALO)
    sums = _trailing_sums(ext, group)[POOL_HALO:, :]
    return sums / _window_count(start, n, group) - ext[POOL_HALO:, :]


def _pool_fwd(u, pool_w, pool_scale, name, ts=512):
    s = u.shape[0]

    def body(v_ref, gate_ref, w_ref, sc_ref, y_ref):
        group = pl.program_id(0)
        for c in range(s // ts):
            a = c * ts
            pooled = _pooled(v_ref, a, ts, group)
            mixed = _dot(pooled.astype(BF16), w_ref[...], NN)
            gate = gate_ref[a:a + ts, :]
            y_ref[a:a + ts, :] = (mixed * sc_ref[...] * (gate * _sigmoid(gate))).astype(BF16)

    col = lambda base: pl.BlockSpec((s, LANE), lambda g: (0, base + g))
    return pl.pallas_call(
        body, name=name, grid=(4,),
        in_specs=[col(COL_POOL_V), col(COL_POOL_G),
                  pl.BlockSpec((None, LANE, LANE), lambda g: (g, 0, 0)),
                  pl.BlockSpec((1, LANE), lambda g: (0, g))],
        out_specs=pl.BlockSpec((s, LANE), lambda g: (0, g)),
        out_shape=jax.ShapeDtypeStruct((s, WIDTH), BF16),
    )(u, u, pool_w, pool_scale)


def _pool_bwd(u, dy, pool_w, pool_scale, name, ts=512):
    s = u.shape[0]

    def body(v_ref, gate_ref, dy_ref, w_ref, sc_ref, dv_ref, dgate_ref, dw_ref, dsc_ref):
        group = pl.program_id(0)
        w = w_ref[...]
        scale = sc_ref[...]
        dw = jnp.zeros((LANE, LANE), F32)
        dsc = jnp.zeros((1, LANE), F32)
        for c in range(s // ts):
            a = c * ts
            n_ext = ts + POOL_HALO
            gate_e = _rows_after(gate_ref, a, ts, POOL_HALO)
            dy_e = _rows_after(dy_ref, a, ts, POOL_HALO)
            silu_e, dsilu_e = _silu_and_grad(gate_e)
            dms_e = dy_e * silu_e
            dm_e = (dms_e * scale).astype(BF16)
            dpool_e = _dot(dm_e, w, NT)
            spread = _leading_sums(dpool_e / _window_count(a, n_ext, group), group)
            dv_ref[a:a + ts, :] = (spread[0:ts, :] - dpool_e[0:ts, :]).astype(BF16)
            pooled = _pooled(v_ref, a, ts, group).astype(BF16)
            mixed = _dot(pooled, w, NN)
            dgate_ref[a:a + ts, :] = (dy_e[0:ts, :] * mixed * scale * dsilu_e[0:ts, :]).astype(BF16)
            dsc = dsc + jnp.sum(dms_e[0:ts, :] * mixed, axis=0, keepdims=True)
            dw = dw + _dot(pooled, dm_e[0:ts, :], TN)
        dw_ref[...] = dw
        dsc_ref[...] = dsc

    col = lambda base: pl.BlockSpec((s, LANE), lambda g: (0, base + g))
    out_col = pl.BlockSpec((s, LANE), lambda g: (0, g))
    return pl.pallas_call(
        body, name=name, grid=(4,),
        in_specs=[col(COL_POOL_V), col(COL_POOL_G), out_col,
                  pl.BlockSpec((None, LANE, LANE), lambda g: (g, 0, 0)),
                  pl.BlockSpec((1, LANE), lambda g: (0, g))],
        out_specs=[out_col, out_col,
                   pl.BlockSpec((None, LANE, LANE), lambda g: (g, 0, 0)),
                   pl.BlockSpec((1, LANE), lambda g: (0, g))],
        out_shape=[jax.ShapeDtypeStruct((s, WIDTH), BF16), jax.ShapeDtypeStruct((s, WIDTH), BF16),
                   jax.ShapeDtypeStruct((4, LANE, LANE), F32), jax.ShapeDtypeStruct((1, WIDTH), F32)],
    )(u, u, dy, pool_w, pool_scale)


def _conv_taps(x_ref, gc_ref, start, n):
    z_ext = _rows_before(gc_ref, start, n, CONV_HALO) * _rows_before(x_ref, start, n, CONV_HALO)
    z0 = z_ext[CONV_HALO:, :]
    z1 = pltpu.roll(z_ext, 1, 0)[CONV_HALO:, :]
    z2 = pltpu.roll(z_ext, 2, 0)[CONV_HALO:, :]
    return z0, z1, z2


def _conv_fwd(u, conv_w, conv_b, name, ts=512):
    s = u.shape[0]

    def body(x_ref, gb_ref, gc_ref, g_ref, w_ref, b_ref, y_ref):
        w0, w1, w2 = w_ref[0:1, :], w_ref[1:2, :], w_ref[2:3, :]
        for c in range(s // ts):
            a = c * ts
            z0, z1, z2 = _conv_taps(x_ref, gc_ref, a, ts)
            y = w2 * z0 + w1 * z1 + w0 * z2 + b_ref[...]
            gate = g_ref[a:a + ts, :]
            y_ref[a:a + ts, :] = (gb_ref[a:a + ts, :] * y * (gate * _sigmoid(gate))).astype(BF16)

    col = lambda base: pl.BlockSpec((s, LANE), lambda j: (0, base + j))
    return pl.pallas_call(
        body, name=name, grid=(4,),
        in_specs=[col(COL_CONV_X), col(COL_CONV_GB), col(COL_CONV_GC), col(COL_CONV_G),
                  pl.BlockSpec((3, LANE), lambda j: (0, j)), pl.BlockSpec((1, LANE), lambda j: (0, j))],
        out_specs=pl.BlockSpec((s, LANE), lambda j: (0, j)),
        out_shape=jax.ShapeDtypeStruct((s, WIDTH), BF16),
    )(u, u, u, u, conv_w, conv_b)


def _conv_bwd(u, dy, conv_w, conv_b, name, ts=512):
    s = u.shape[0]

    def body(x_ref, gb_ref, gc_ref, g_ref, dy_ref, w_ref, b_ref,
             dx_ref, dgb_ref, dgc_ref, dg_ref, dw_ref, db_ref):
        w0, w1, w2 = w_ref[0:1, :], w_ref[1:2, :], w_ref[2:3, :]
        acc = [jnp.zeros((1, LANE), F32) for _ in range(4)]
        for c in range(s // ts):
            a = c * ts
            n_ext = ts + CONV_HALO
            gate_e = _rows_after(g_ref, a, ts, CONV_HALO)
            silu_e, dsilu_e = _silu_and_grad(gate_e)
            dy_e = _rows_after(dy_ref, a, ts, CONV_HALO)
            gb_e = _rows_after(gb_ref, a, ts, CONV_HALO)
            dyy_e = dy_e * silu_e * gb_e
            dz = (w2 * dyy_e + w1 * pltpu.roll(dyy_e, n_ext - 1, 0) + w0 * pltpu.roll(dyy_e, n_ext - 2, 0))[0:ts, :]
            z0, z1, z2 = _conv_taps(x_ref, gc_ref, a, ts)
            yb = w2 * z0 + w1 * z1 + w0 * z2 + b_ref[...]
            dyv = dy_e[0:ts, :]
            dyy = dyy_e[0:ts, :]
            dg_ref[a:a + ts, :] = (dyv * gb_e[0:ts, :] * yb * dsilu_e[0:ts, :]).astype(BF16)
            dgb_ref[a:a + ts, :] = (dyv * silu_e[0:ts, :] * yb).astype(BF16)
            dx_ref[a:a + ts, :] = (dz * gc_ref[a:a + ts, :]).astype(BF16)
            dgc_ref[a:a + ts, :] = (dz * x_ref[a:a + ts, :]).astype(BF16)
            for i, term in enumerate((dyy * z2, dyy * z1, dyy * z0, dyy)):
                acc[i] = acc[i] + jnp.sum(term, axis=0, keepdims=True)
        dw_ref[0:1, :] = acc[0]
        dw_ref[1:2, :] = acc[1]
        dw_ref[2:3, :] = acc[2]
        db_ref[...] = acc[3]

    col = lambda base: pl.BlockSpec((s, LANE), lambda j: (0, base + j))
    out_col = pl.BlockSpec((s, LANE), lambda j: (0, j))
    big = jax.ShapeDtypeStruct((s, WIDTH), BF16)
    return pl.pallas_call(
        body, name=name, grid=(4,),
        in_specs=[col(COL_CONV_X), col(COL_CONV_GB), col(COL_CONV_GC), col(COL_CONV_G), out_col,
                  pl.BlockSpec((3, LANE), lambda j: (0, j)), pl.BlockSpec((1, LANE), lambda j: (0, j))],
        out_specs=[out_col, out_col, out_col, out_col,
                   pl.BlockSpec((3, LANE), lambda j: (0, j)), pl.BlockSpec((1, LANE), lambda j: (0, j))],
        out_shape=[big, big, big, big,
                   jax.ShapeDtypeStruct((3, WIDTH), F32), jax.ShapeDtypeStruct((1, WIDTH), F32)],
    )(u, u, u, u, dy, conv_w, conv_b)


LOG2_E = 1.4426950408889634
LN_2 = 0.6931471805599453


def _sb_scores(q_h, k_blk, valid, later_mat, carry):
    z = _dot(q_h, k_blk, NT)
    neg_z = -z
    soft = jnp.log(1.0 + jnp.exp2(jnp.minimum(z, neg_z))) * LOG2_E
    log_keep = jnp.minimum(neg_z, 0.0) - soft
    log_beta = log_keep + z
    if valid is not None:
        log_keep = jnp.where(valid, log_keep, 0.0)
    later = _dot(log_keep.astype(BF16), later_mat, NN) + carry
    return log_keep, log_beta, later


def _masked(valid, x):
    return x if valid is None else jnp.where(valid, x, 0.0)


def _diagonal_masks(tq, tk):
    r = lax.broadcasted_iota(jnp.int32, (tq, tk), 0)
    cidx = lax.broadcasted_iota(jnp.int32, (tq, tk), 1)
    return [cidx + d * tk < r for d in range(tq // tk)]


def _triangle(tk, op):
    r = lax.broadcasted_iota(jnp.int32, (tk, tk), 0)
    cidx = lax.broadcasted_iota(jnp.int32, (tk, tk), 1)
    return op(r, cidx).astype(BF16)


def _sb_fwd(u, name, t=512, tk=256, pairs=2):
    s = u.shape[0]
    assert s // tk <= LANE and 4 % pairs == 0 and t % tk == 0
    scale = HEAD_DIM ** -0.5
    nh = 2 * pairs
    wide = pairs * LANE
    ratio = t // tk

    def body(q_ref, k_ref, v_ref, g_ref, o_ref, y_ref, after_ref, kb_ref, vb_ref, acc_ref, carry_ref):
        i = pl.program_id(1)

        @pl.when(i == 0)
        def _():
            kb_ref[...] = k_ref[...].astype(BF16)
            vb_ref[...] = v_ref[...].astype(BF16)

        lane = lax.broadcasted_iota(jnp.int32, (t, LANE), 1)
        first = lane < HEAD_DIM
        after_ref[...] = jnp.zeros_like(after_ref)
        qv = q_ref[...] * (scale * LOG2_E)
        q_heads = []
        for p in range(pairs):
            qp = qv[:, p * LANE:(p + 1) * LANE]
            q_heads += [jnp.where(first, qp, 0.0).astype(BF16), jnp.where(first, 0.0, qp).astype(BF16)]
        later_mat = _triangle(tk, lambda r, cidx: r > cidx)
        acc_ref[...] = jnp.zeros_like(acc_ref)
        carry_ref[...] = jnp.zeros_like(carry_ref)

        def block(kb, valid):
            rows = pl.ds(pl.multiple_of(kb * tk, tk), tk)
            k_blk = kb_ref[rows, :]
            v_blk = vb_ref[rows, :]
            carries = [carry_ref[h] for h in range(nh)]
            afters = [after_ref[:, h * LANE:(h + 1) * LANE] for h in range(nh)]
            accs = [acc_ref[h] for h in range(nh)]
            outs = []
            for h in range(nh):
                cols = slice((h // 2) * LANE, (h // 2 + 1) * LANE)
                log_keep, log_beta, later = _sb_scores(q_heads[h], k_blk[:, cols], valid, later_mat, carries[h])
                a = _masked(valid, jnp.exp2(log_beta + later))
                outs.append((accs[h] + _dot(a.astype(BF16), v_blk[:, cols], NN),
                             carries[h] + jnp.sum(log_keep, axis=1, keepdims=True),
                             jnp.where(lane == kb, carries[h], afters[h])))
            for h in range(nh):
                acc_ref[h] = outs[h][0]
                carry_ref[h] = outs[h][1]
                after_ref[:, h * LANE:(h + 1) * LANE] = outs[h][2]

        def step(j, _):
            block(ratio * i - 1 - j, None)
            return 0

        masks = _diagonal_masks(t, tk)
        for d in reversed(range(ratio)):
            block(ratio * i + d, masks[d])
        lax.fori_loop(0, ratio * i, step, 0)
        for p in range(pairs):
            cols = slice(p * LANE, (p + 1) * LANE)
            o = jnp.where(first, acc_ref[2 * p], acc_ref[2 * p + 1])
            o_ref[:, cols] = o
            gate = g_ref[:, cols]
            y_ref[:, cols] = (o * gate * _sigmoid(gate)).astype(BF16)

    blk = lambda base: pl.BlockSpec((t, wide), lambda g, i: (i, base // pairs + g))
    full = lambda base: pl.BlockSpec((s, wide), lambda g, i: (0, base // pairs + g))
    out_blk = pl.BlockSpec((t, wide), lambda g, i: (i, g))
    return pl.pallas_call(
        body, name=name, grid=(4 // pairs, s // t),
        in_specs=[blk(COL_SB_Q), full(COL_SB_K), full(COL_SB_V), blk(COL_SB_G)],
        out_specs=[out_blk, out_blk, pl.BlockSpec((t, nh * LANE), lambda g, i: (i, g))],
        out_shape=[jax.ShapeDtypeStruct((s, WIDTH), F32), jax.ShapeDtypeStruct((s, WIDTH), BF16),
                   jax.ShapeDtypeStruct((s, 8 * LANE), F32)],
        scratch_shapes=[pltpu.VMEM((s, wide), BF16), pltpu.VMEM((s, wide), BF16),
                        pltpu.VMEM((nh, t, LANE), F32), pltpu.VMEM((nh, t, 1), F32)],
        compiler_params=pltpu.CompilerParams(dimension_semantics=("arbitrary", "arbitrary")),
    )(u, u, u, u)


def _sb_bwd(u, o, after, dy, name, t=512, tk=256, pairs=2):
    s = u.shape[0]
    nq = s // t
    scale = HEAD_DIM ** -0.5
    nh = 2 * pairs
    wide = pairs * LANE
    ratio = t // tk

    def body(q_ref, k_ref, v_ref, g_ref, o_ref, after_ref, dy_ref, dq_ref, dk_ref, dv_ref, dg_ref,
             kb_ref, vb_ref, dk_acc, dv_acc, dq_acc, carry_ref):
        i = pl.program_id(1)

        @pl.when(i == 0)
        def _():
            kb_ref[...] = k_ref[...].astype(BF16)
            vb_ref[...] = v_ref[...].astype(BF16)
            dk_acc[...] = jnp.zeros_like(dk_acc)
            dv_acc[...] = jnp.zeros_like(dv_acc)

        lane = lax.broadcasted_iota(jnp.int32, (t, LANE), 1)
        first = lane < HEAD_DIM
        gate = g_ref[...]
        silu, dsilu = _silu_and_grad(gate)
        dyv = dy_ref[...]
        do = dyv * silu
        dg_ref[...] = (dyv * o_ref[...] * dsilu).astype(BF16)
        qv = q_ref[...] * (scale * LOG2_E)
        do_heads, q_heads = [], []
        for p in range(pairs):
            cols = slice(p * LANE, (p + 1) * LANE)
            do_heads += [jnp.where(first, do[:, cols], 0.0).astype(BF16), jnp.where(first, 0.0, do[:, cols]).astype(BF16)]
            q_heads += [jnp.where(first, qv[:, cols], 0.0).astype(BF16), jnp.where(first, 0.0, qv[:, cols]).astype(BF16)]
        later_mat = _triangle(tk, lambda r, cidx: r > cidx)
        before_mat = _triangle(tk, lambda r, cidx: r < cidx)
        dq_acc[...] = jnp.zeros_like(dq_acc)
        carry_ref[...] = jnp.zeros_like(carry_ref)

        def block(kb, valid):
            rows = pl.ds(pl.multiple_of(kb * tk, tk), tk)
            k_blk = kb_ref[rows, :]
            v_blk = vb_ref[rows, :]
            carries = [carry_ref[h] for h in range(nh)]
            dq_old = [dq_acc[h] for h in range(nh)]
            dk_old = dk_acc[rows, :]
            dv_old = dv_acc[rows, :]
            outs = []
            for h in range(nh):
                cols = slice((h // 2) * LANE, (h // 2 + 1) * LANE)
                after = jnp.sum(jnp.where(lane == kb, after_ref[:, h * LANE:(h + 1) * LANE], 0.0), axis=1, keepdims=True)
                _, log_beta, later = _sb_scores(q_heads[h], k_blk[:, cols], valid, later_mat, after)
                beta = jnp.exp2(log_beta)
                a = _masked(valid, jnp.exp2(log_beta + later))
                da = _dot(do_heads[h], v_blk[:, cols], NT)
                gterm = a * da
                before = _dot(gterm.astype(BF16), before_mat, NN) + carries[h]
                dz_b = _masked(valid, gterm * (1.0 - beta) - beta * before).astype(BF16)
                outs.append((dq_old[h] + _dot(dz_b, k_blk[:, cols], NN), _dot(dz_b, q_heads[h], TN),
                             _dot(a.astype(BF16), do_heads[h], TN),
                             carries[h] + jnp.sum(gterm, axis=1, keepdims=True)))
            for h in range(nh):
                dq_acc[h] = outs[h][0]
                carry_ref[h] = outs[h][3]
            dk_new = [outs[2 * p][1] + outs[2 * p + 1][1] for p in range(pairs)]
            dv_new = [outs[2 * p][2] + outs[2 * p + 1][2] for p in range(pairs)]
            dk_acc[rows, :] = dk_old + (dk_new[0] if pairs == 1 else jnp.concatenate(dk_new, axis=1))
            dv_acc[rows, :] = dv_old + (dv_new[0] if pairs == 1 else jnp.concatenate(dv_new, axis=1))

        def step(kb, _):
            block(kb, None)
            return 0

        lax.fori_loop(0, ratio * i, step, 0)
        masks = _diagonal_masks(t, tk)
        for d in range(ratio):
            block(ratio * i + d, masks[d])
        for p in range(pairs):
            dq_ref[:, p * LANE:(p + 1) * LANE] = (jnp.where(first, dq_acc[2 * p], dq_acc[2 * p + 1]) * scale).astype(BF16)

        @pl.when(i == nq - 1)
        def _():
            dk_ref[...] = (dk_acc[...] * LN_2).astype(BF16)
            dv_ref[...] = dv_acc[...].astype(BF16)

    blk = lambda base: pl.BlockSpec((t, wide), lambda g, i: (i, base // pairs + g))
    full = lambda base: pl.BlockSpec((s, wide), lambda g, i: (0, base // pairs + g))
    out_blk = pl.BlockSpec((t, wide), lambda g, i: (i, g))
    out_full = pl.BlockSpec((s, wide), lambda g, i: (0, g))
    big = jax.ShapeDtypeStruct((s, WIDTH), BF16)
    return pl.pallas_call(
        body, name=name, grid=(4 // pairs, nq),
        in_specs=[blk(COL_SB_Q), full(COL_SB_K), full(COL_SB_V), blk(COL_SB_G), out_blk,
                  pl.BlockSpec((t, nh * LANE), lambda g, i: (i, g)), out_blk],
        out_specs=[out_blk, out_full, out_full, out_blk],
        out_shape=[big, big, big, big],
        scratch_shapes=[pltpu.VMEM((s, wide), BF16), pltpu.VMEM((s, wide), BF16),
                        pltpu.VMEM((s, wide), F32), pltpu.VMEM((s, wide), F32),
                        pltpu.VMEM((nh, t, LANE), F32), pltpu.VMEM((nh, t, 1), F32)],
        compiler_params=pltpu.CompilerParams(dimension_semantics=("arbitrary", "arbitrary")),
    )(u, u, u, u, o, after, dy)


def _gate_fwd(u, projs, name, ts=256):
    s = u.shape[0]

    def body(m0, m1, m2, p0, p1, p2, out_ref):
        tot = None
        for m_ref, p_ref in ((m0, p0), (m1, p1), (m2, p2)):
            term = _sigmoid(m_ref[...]) * p_ref[...].astype(F32)
            tot = term if tot is None else tot + term
        out_ref[...] = tot.astype(BF16)

    mspec = lambda n: pl.BlockSpec((ts, D_MODEL), lambda i: (i, COL_MERGE_1024 + n))
    row = pl.BlockSpec((ts, D_MODEL), lambda i: (i, 0))
    return pl.pallas_call(
        body, name=name, grid=(s // ts,),
        in_specs=[mspec(0), mspec(1), mspec(2), row, row, row],
        out_specs=row, out_shape=jax.ShapeDtypeStruct((s, D_MODEL), BF16),
    )(u, u, u, *projs)


def _gate_bwd(u, projs, dmerged, name, ts=256):
    s = u.shape[0]

    def body(m0, m1, m2, p0, p1, p2, dm_ref, dp0, dp1, dp2, dl0, dl1, dl2):
        dm = dm_ref[...].astype(F32)
        for m_ref, p_ref, dp_ref, dl_ref in ((m0, p0, dp0, dl0), (m1, p1, dp1, dl1), (m2, p2, dp2, dl2)):
            gate = _sigmoid(m_ref[...])
            dp_ref[...] = (dm * gate).astype(BF16)
            dl_ref[...] = (dm * p_ref[...].astype(F32) * gate * (1.0 - gate)).astype(BF16)

    mspec = lambda n: pl.BlockSpec((ts, D_MODEL), lambda i: (i, COL_MERGE_1024 + n))
    row = pl.BlockSpec((ts, D_MODEL), lambda i: (i, 0))
    big = jax.ShapeDtypeStruct((s, D_MODEL), BF16)
    outs = pl.pallas_call(
        body, name=name, grid=(s // ts,),
        in_specs=[mspec(0), mspec(1), mspec(2), row, row, row, row],
        out_specs=[row] * 6, out_shape=[big] * 6,
    )(u, u, u, *projs, dmerged)
    return outs[:3], outs[3:]


def _as_rows(a):
    return a.reshape(-1, a.shape[-1])


def _row_tile(rows, cols, bytes_per_row_elem=4, cap=1 << 20):
    tr = rows
    while tr * cols * bytes_per_row_elem > cap and tr % 2 == 0 and (tr // 2) % 16 == 0:
        tr //= 2
    return tr


def _cast_bf16(a, name):
    a2 = _as_rows(a)
    rows, cols = a2.shape
    tr = _row_tile(rows, cols)

    def body(a_ref, o_ref):
        o_ref[...] = a_ref[...].astype(BF16)

    spec = pl.BlockSpec((tr, cols), lambda i: (i, 0))
    out = pl.pallas_call(body, name=name, grid=(rows // tr,), in_specs=[spec], out_specs=spec,
                         out_shape=jax.ShapeDtypeStruct((rows, cols), BF16))(a2)
    return out.reshape(a.shape)


def _adamw(w, g, m, v, name):
    shape = w.shape
    w2, g2, m2, v2 = (_as_rows(a) for a in (w, g, m, v))
    rows, cols = w2.shape
    tr = _row_tile(rows, cols)
    c1 = 1.0 - ADAM_B1 ** ADAM_STEP
    c2 = 1.0 - ADAM_B2 ** ADAM_STEP

    def body(w_ref, g_ref, m_ref, v_ref, d_ref, nm_ref, nv_ref):
        gv = g_ref[...]
        nm = ADAM_B1 * m_ref[...] + (1.0 - ADAM_B1) * gv
        nv = ADAM_B2 * v_ref[...] + (1.0 - ADAM_B2) * (gv * gv)
        nm_ref[...] = nm
        nv_ref[...] = nv
        d_ref[...] = -ADAM_LR * ((nm / c1) / (jnp.sqrt(nv / c2) + ADAM_EPS) + ADAM_WD * w_ref[...])

    spec = pl.BlockSpec((tr, cols), lambda i: (i, 0))
    sds = jax.ShapeDtypeStruct((rows, cols), F32)
    outs = pl.pallas_call(body, name=name, grid=(rows // tr,), in_specs=[spec] * 4, out_specs=[spec] * 3,
                          out_shape=[sds] * 3)(w2, g2, m2, v2)
    return tuple(o.reshape(shape) for o in outs)


def _sum_slots(a, out_dtype, name):
    n = a.shape[0]
    a3 = a.reshape(n, -1, a.shape[-1])
    _, rows, cols = a3.shape
    tr = _row_tile(rows, cols * n)

    def body(a_ref, o_ref):
        tot = a_ref[0].astype(F32)
        for k in range(1, n):
            tot = tot + a_ref[k].astype(F32)
        o_ref[...] = tot.astype(out_dtype)

    out = pl.pallas_call(
        body, name=name, grid=(rows // tr,),
        in_specs=[pl.BlockSpec((n, tr, cols), lambda i: (0, i, 0))],
        out_specs=pl.BlockSpec((tr, cols), lambda i: (i, 0)),
        out_shape=jax.ShapeDtypeStruct((rows, cols), out_dtype))(a3)
    return out.reshape(a.shape[1:])


def _chip_sum(own, recv, axis, core, name):
    half = recv.shape
    nd = len(half)
    last = nd - 1
    if axis == last:
        tl, nt = half[last], 1
    else:
        tl = min(half[last], 2048)
        nt = half[last] // tl
    block = half[:last] + (tl,)

    def own_index(i, core_ref):
        idx = [0] * nd
        idx[last] = i
        if axis == last:
            idx[last] = core_ref[0]
        else:
            idx[axis] = core_ref[0]
        return tuple(idx)

    def recv_index(i, core_ref):
        idx = [0] * nd
        idx[last] = i
        return tuple(idx)

    def body(core_ref, own_ref, recv_ref, o_ref):
        o_ref[...] = (own_ref[...] + recv_ref[...]).astype(BF16)

    return pl.pallas_call(
        body, name=name,
        grid_spec=pltpu.PrefetchScalarGridSpec(
            num_scalar_prefetch=1, grid=(nt,),
            in_specs=[pl.BlockSpec(block, own_index), pl.BlockSpec(block, recv_index)],
            out_specs=pl.BlockSpec(block, recv_index)),
        out_shape=jax.ShapeDtypeStruct(half, BF16),
    )(core, own, recv)


def _mesh_position():
    return lax.axis_index("x"), lax.axis_index("y"), lax.axis_index("c")


def _other_chips(x, y):
    return [(1 - x, y), (x, 1 - y), (1 - x, 1 - y)]


ALL_FLIPS = [(0, 0, 1), (1, 0, 0), (0, 1, 0), (1, 1, 0), (1, 0, 1), (0, 1, 1), (1, 1, 1)]


def _half(ref, axis, which, size):
    idx = [slice(None)] * len(ref.shape)
    idx[axis] = pl.ds(which * size, size)
    return ref.at[tuple(idx)]


def _sub(ref, picks):
    idx = [slice(None)] * len(ref.shape)
    for axis, start, size in picks:
        idx[axis] = pl.ds(start, size)
    return ref.at[tuple(idx)]


def _remote(src, dst, sems_send, sems_recv, k, to):
    return pltpu.make_async_remote_copy(src_ref=src, dst_ref=dst, send_sem=sems_send.at[k], recv_sem=sems_recv.at[k],
                                        device_id=to, device_id_type=MESH)


def _cast_shard(w, shard_axis, pos, name, tr=512):
    shape = w.shape
    nd = len(shape)
    assert shard_axis in (nd - 1, nd - 2)
    rows, cols = shape[-2:]
    tr = min(tr, rows)
    nt = rows // tr
    lead = shape[:-2]
    full = list(shape)
    full[shard_axis] *= N_CHIPS
    block = (1,) * len(lead) + (tr, cols)

    def in_index(*args):
        return (*args[:-1], 0)

    def out_index(*args):
        *g, pos_ref = args
        if shard_axis == nd - 1:
            return (*g, pos_ref[1])
        return (*g[:-1], pos_ref[1] * nt + g[-1], 0)

    def body(pos_ref, a_ref, o_ref):
        o_ref[...] = a_ref[...].astype(BF16)

    return pl.pallas_call(
        body, name=name,
        grid_spec=pltpu.PrefetchScalarGridSpec(
            num_scalar_prefetch=1, grid=lead + (nt,),
            in_specs=[pl.BlockSpec(block, in_index)], out_specs=pl.BlockSpec(block, out_index)),
        out_shape=jax.ShapeDtypeStruct(tuple(full), BF16),
    )(pos, w)


def _gather_weights(w_in, w_branch, w_out, conv_w):
    layout = [(2, w_in.shape[2] // N_CHIPS, 1, w_in.shape[1] // 2),
              (3, w_branch.shape[3] // N_CHIPS, 2, w_branch.shape[2] // 2),
              (1, w_out.shape[1] // N_CHIPS, 2, w_out.shape[2] // 2)]
    n_big = len(layout)

    def body(win_in, wbr_in, wout_in, cw, win_f, wbr_f, wout_f, cw_f, send_sems, recv_sems, local_sems):
        x, y, c = _mesh_position()
        chips = _other_chips(x, y)
        sibling = (x, y, 1 - c)
        mine = 2 * x + y
        fulls = [win_f, wbr_f, wout_f]

        def place(t, chip, core):
            sh_axis, sh_size, half_axis, half_size = layout[t]
            return _sub(fulls[t], [(sh_axis, chip * sh_size, sh_size), (half_axis, core * half_size, half_size)])

        local = [pltpu.make_async_copy(cw, cw_f.at[mine], local_sems.at[0])]
        for cp in local:
            cp.start()

        sends = []
        for t in range(n_big):
            for k, chip in enumerate(chips):
                sends.append(_remote(place(t, mine, c), place(t, mine, c),
                                     send_sems, recv_sems, 6 * t + k, (*chip, c)))
        small_base = 6 * n_big
        for k, chip in enumerate(chips):
            sends.append(_remote(cw, cw_f.at[mine], send_sems, recv_sems, small_base + k, (*chip, c)))
        for cp in sends:
            cp.start()

        passed = []
        for t in range(n_big):
            for k, (px, py) in enumerate(chips):
                theirs = 2 * px + py
                _remote(place(t, theirs, c), place(t, theirs, c), send_sems, recv_sems, 6 * t + k, (px, py, c)).wait_recv()
                fwd = _remote(place(t, theirs, c), place(t, theirs, c), send_sems, recv_sems, 6 * t + 3 + k, sibling)
                fwd.start()
                passed.append(fwd)
        for t in range(n_big):
            for k, (px, py) in enumerate(chips):
                theirs = 2 * px + py
                _remote(place(t, theirs, 1 - c), place(t, theirs, 1 - c), send_sems, recv_sems, 6 * t + 3 + k,
                        sibling).wait_recv()
        for k, (px, py) in enumerate(chips):
            _remote(cw, cw_f.at[2 * px + py], send_sems, recv_sems, small_base + k, sibling).wait_recv()
        for cp in sends + passed:
            cp.wait_send()
        for cp in local:
            cp.wait()

    n_sems = 6 * n_big + 3
    out_shape = [jax.ShapeDtypeStruct(a.shape, BF16) for a in (w_in, w_branch, w_out)]
    out_shape.append(jax.ShapeDtypeStruct((N_CHIPS,) + conv_w.shape, F32))
    return pl.pallas_call(
        body, name="gather_weights",
        in_specs=[ANY] * 4, out_specs=[ANY] * 4, out_shape=out_shape,
        input_output_aliases={0: 0, 1: 1, 2: 2},
        scratch_shapes=[pltpu.SemaphoreType.DMA((n_sems,)), pltpu.SemaphoreType.DMA((n_sems,)),
                        pltpu.SemaphoreType.DMA((1,))],
    )(w_in, w_branch, w_out, conv_w)


def _swap_halves(items, name):
    n = len(items)
    halves = []
    for a, axis in items:
        shp = list(a.shape)
        shp[axis] //= 2
        halves.append(tuple(shp))

    def body(*refs):
        srcs, dsts, (send_sems, recv_sems) = refs[:n], refs[n:2 * n], refs[2 * n:]
        x, y, c = _mesh_position()
        copies = []
        for k in range(n):
            axis = items[k][1]
            copies.append(_remote(_half(srcs[k], axis, 1 - c, halves[k][axis]), dsts[k], send_sems, recv_sems, k,
                                  (x, y, 1 - c)))
        for cp in copies:
            cp.start()
        for cp in copies:
            cp.wait()

    return pl.pallas_call(
        body, name=name, in_specs=[ANY] * n, out_specs=[ANY] * n,
        out_shape=[jax.ShapeDtypeStruct(h, a.dtype) for h, (a, _) in zip(halves, items)],
        scratch_shapes=[pltpu.SemaphoreType.DMA((n,)), pltpu.SemaphoreType.DMA((n,))],
    )(*[a for a, _ in items])


def _exchange_grads(items, small):
    n = len(items)
    slices = []
    for a, axis in items:
        shp = list(a.shape)
        shp[axis] //= N_CHIPS
        slices.append(tuple(shp))

    def body(*refs):
        srcs, small_ref = refs[:n], refs[n]
        dsts, small_all = refs[n + 1:2 * n + 1], refs[2 * n + 1]
        send_sems, recv_sems, local_sems = refs[2 * n + 2:]
        x, y, c = _mesh_position()
        chips = _other_chips(x, y)
        mine = 2 * x + y
        me = 4 * x + 2 * y + c

        def piece(k, chip):
            axis = items[k][1]
            return _half(srcs[k], axis, chip, slices[k][axis])

        local = [pltpu.make_async_copy(small_ref, small_all.at[me], local_sems.at[0])]
        for cp in local:
            cp.start()
        copies = []
        for k in range(n):
            for r, (px, py) in enumerate(chips):
                copies.append(_remote(piece(k, 2 * px + py), dsts[k].at[r], send_sems, recv_sems, 3 * k + r, (px, py, c)))
        for r, (fx, fy, fc) in enumerate(ALL_FLIPS):
            copies.append(_remote(small_ref, small_all.at[me], send_sems, recv_sems, 3 * n + r,
                                  (x ^ fx, y ^ fy, c ^ fc)))
        for cp in copies:
            cp.start()
        for cp in copies:
            cp.wait()
        for cp in local:
            cp.wait()

    n_sems = 3 * n + len(ALL_FLIPS)
    out_shape = [jax.ShapeDtypeStruct((N_CHIPS - 1,) + s, BF16) for s in slices]
    out_shape.append(jax.ShapeDtypeStruct((2 * N_CHIPS,) + small.shape, F32))
    outs = pl.pallas_call(
        body, name="exchange_grads", in_specs=[ANY] * (n + 1), out_specs=[ANY] * (n + 1), out_shape=out_shape,
        scratch_shapes=[pltpu.SemaphoreType.DMA((n_sems,)), pltpu.SemaphoreType.DMA((n_sems,)),
                        pltpu.SemaphoreType.DMA((1,))],
    )(*[a for a, _ in items], small)
    return outs[:n], outs[n]


def _sum_chips(recv, own, shard_axis, split_axis, pos, dest, layer, name, tr=128):
    sl = recv.shape[1:]
    nd = len(sl)
    tiled = nd == 2 and sl[0] > tr
    nt = sl[0] // tr if tiled else 1
    block = ((tr,) + sl[1:]) if tiled else sl
    shard = list(sl)
    shard[split_axis] *= 2

    def recv_index(i, pos_ref):
        return (0, i) + (0,) * (nd - 1) if tiled else (0,) * (nd + 1)

    def own_index(i, pos_ref):
        idx = [0] * nd
        idx[shard_axis] = pos_ref[1]
        if tiled:
            idx[0] = pos_ref[1] * nt + i if shard_axis == 0 else i
        return tuple(idx)

    def out_index(i, pos_ref):
        idx = [0] * nd
        idx[split_axis] = pos_ref[0]
        if tiled:
            idx[0] = pos_ref[0] * nt + i if split_axis == 0 else i
        return (layer, *idx)

    def body(pos_ref, recv_ref, own_ref, *rest):
        o_ref = rest[-1]
        tot = own_ref[...].astype(F32)
        for k in range(N_CHIPS - 1):
            tot = tot + recv_ref[k].astype(F32)
        o_ref[0] = tot

    in_specs = [pl.BlockSpec((N_CHIPS - 1,) + block, recv_index), pl.BlockSpec(block, own_index)]
    args = [pos, recv, own]
    aliases = {}
    if dest is not None:
        in_specs.append(ANY)
        args.append(dest)
        aliases = {3: 0}
    return pl.pallas_call(
        body, name=name,
        grid_spec=pltpu.PrefetchScalarGridSpec(
            num_scalar_prefetch=1, grid=(nt,), in_specs=in_specs,
            out_specs=pl.BlockSpec((1,) + block, out_index)),
        out_shape=jax.ShapeDtypeStruct((DEPTH,) + tuple(shard), F32),
        input_output_aliases=aliases,
    )(*args)


def _share_halves(bufs, name):
    n = len(bufs)

    def body(*refs):
        outs, (send_sems, recv_sems) = refs[n:2 * n], refs[2 * n:]
        x, y, c = _mesh_position()
        copies = []
        for k, (a, axis) in enumerate(bufs):
            size = a.shape[1 + axis] // 2
            mine = _half(outs[k], 1 + axis, c, size)
            copies.append(_remote(mine, mine, send_sems, recv_sems, k, (x, y, 1 - c)))
        for cp in copies:
            cp.start()
        for cp in copies:
            cp.wait()

    return pl.pallas_call(
        body, name=name, in_specs=[ANY] * n, out_specs=[ANY] * n,
        out_shape=[jax.ShapeDtypeStruct(a.shape, F32) for a, _ in bufs],
        input_output_aliases={k: k for k in range(n)},
        scratch_shapes=[pltpu.SemaphoreType.DMA((n,)), pltpu.SemaphoreType.DMA((n,))],
    )(*[a for a, _ in bufs])


def _layer_fwd(x, p, l):
    tag = f"l{l}_"
    h = _rms_fwd(x, p["pre_g"], tag + "pre_norm")
    u = _matmul(h, p["w_in"], "nn", F32, tag + "in_proj")
    y_pool = _pool_fwd(u, p["pool_w"], p["pool_scale"], tag + "pool")
    y_conv = _conv_fwd(u, p["conv_w"], p["conv_b"], tag + "conv")
    o_sb, y_sb, sb_after = _sb_fwd(u, tag + "stickbreak")
    ys = [y_pool, y_conv, y_sb]
    projs = [_matmul(ys[n], p["w_branch"][n], "nn", BF16, tag + f"branch_proj{n}") for n in range(3)]
    merged = _gate_fwd(u, projs, tag + "merge")
    out = _matmul(merged, p["w_out"], "nn", F32, tag + "out_proj")
    return out, dict(x=x, h=h, u=u, ys=ys, o_sb=o_sb, sb_after=sb_after, projs=projs, merged=merged, out=out)


def _layer_bwd(dy, p, saved, l):
    tag = f"l{l}_bwd_"
    u = saved["u"]
    d_out, g_post = _rms_bwd(saved["out"], p["post_g"], dy, None, BF16, tag + "post_norm")
    d_merged = _matmul(d_out, p["w_out"], "nt", BF16, tag + "out_proj_dx")
    g_w_out = _matmul(saved["merged"], d_out, "tn", F32, tag + "out_proj_dw")
    d_projs, d_logits = _gate_bwd(u, saved["projs"], d_merged, tag + "merge")
    g_w_branch = [_matmul(saved["ys"][n], d_projs[n], "tn", F32, tag + f"branch_dw{n}") for n in range(3)]
    d_ys = [_matmul(d_projs[n], p["w_branch"][n], "nt", F32, tag + f"branch_dx{n}") for n in range(3)]
    d_pv, d_pg, g_pool_w, g_pool_scale = _pool_bwd(u, d_ys[0], p["pool_w"], p["pool_scale"], tag + "pool")
    d_cx, d_cgb, d_cgc, d_cg, g_conv_w, g_conv_b = _conv_bwd(u, d_ys[1], p["conv_w"], p["conv_b"], tag + "conv")
    d_q, d_k, d_v, d_sg = _sb_bwd(u, saved["o_sb"], saved["sb_after"], d_ys[2], tag + "stickbreak")
    du = jnp.concatenate([d_pv, d_pg, d_cx, d_cgb, d_cgc, d_cg, d_q, d_k, d_v, d_sg] + list(d_logits), axis=1)
    g_w_in = _matmul(saved["h"], du, "tn", F32, tag + "in_proj_dw", tk=512)
    dh = _matmul(du, p["w_in"], "nt", BF16, tag + "in_proj_dx")
    dx, g_pre = _rms_bwd(saved["x"], p["pre_g"], dh, dy, F32, tag + "pre_norm")
    grads = dict(w_in=g_w_in, w_branch=jnp.stack(g_w_branch), w_out=g_w_out, pre_g=g_pre, post_g=g_post,
                 pool_w=g_pool_w, pool_scale=g_pool_scale, conv_w=g_conv_w, conv_b=g_conv_b)
    return dx, grads


SMALL_ORDER = ["pre_g", "pool_w", "pool_scale", "conv_w", "conv_b", "post_g"]


def _pack_small(per_layer):
    parts, spans, at = [], {}, 0
    for name in SMALL_ORDER:
        a = jnp.stack([per_layer[l][name] for l in range(DEPTH)]).reshape(-1, LANE)
        parts.append(a)
        spans[name] = (at, a.shape[0])
        at += a.shape[0]
    return jnp.concatenate(parts, axis=0), spans


def kernel(x, pre_norm_g, w_in, pool_w, pool_scale, conv_w, conv_b, w_branch, w_out, post_norm_g, loss_target, m_pre_norm_g, m_w_in, m_pool_w, m_pool_scale, m_conv_w, m_conv_b, m_w_branch, m_w_out, m_post_norm_g, v_pre_norm_g, v_w_in, v_pool_w, v_pool_scale, v_conv_w, v_conv_b, v_w_branch, v_w_out, v_post_norm_g):
    mx, my, mc = _mesh_position()
    chip = 2 * mx + my
    core = mc.astype(jnp.int32).reshape(1)
    pos = jnp.stack([mc, chip]).astype(jnp.int32)

    w_in_f, w_branch_f, w_out_f, conv_w_by_chip = _gather_weights(
        _cast_shard(w_in, 2, pos, "cast_w_in"), _cast_shard(w_branch, 3, pos, "cast_w_branch"),
        _cast_shard(w_out, 1, pos, "cast_w_out"), conv_w)
    conv_w_f = conv_w_by_chip.transpose(1, 2, 0, 3).reshape(DEPTH, 3, WIDTH)
    pool_w_b = _cast_bf16(pool_w, "cast_pool_w")
    params = []
    for l in range(DEPTH):
        params.append(dict(
            pre_g=pre_norm_g[l:l + 1], post_g=post_norm_g[l:l + 1], w_in=w_in_f[l], w_branch=w_branch_f[l],
            w_out=w_out_f[l], pool_w=pool_w_b[l], pool_scale=pool_scale[l:l + 1], conv_w=conv_w_f[l],
            conv_b=conv_b[l:l + 1]))

    act = x[0]
    saved = []
    for l in range(DEPTH):
        out, sv = _layer_fwd(act, params[l], l)
        saved.append(sv)
        if l < DEPTH - 1:
            act = _resid_out(act, out, params[l]["post_g"], None, f"l{l}_resid")
    dy, loss_part = _resid_out(act, saved[-1]["out"], params[-1]["post_g"], loss_target[0], "loss_head")
    loss = lax.psum(loss_part[0, 0], ("x", "y", "c"))

    grads = [None] * DEPTH
    for l in reversed(range(DEPTH)):
        dy, grads[l] = _layer_bwd(dy, params[l], saved[l], l)
    grad_x = dy[None]

    split_axis = dict(w_in=0, w_branch=1, w_out=1)
    shard_axis = dict(w_in=1, w_branch=2, w_out=0)
    names = ["w_in", "w_branch", "w_out"]
    items = [(grads[l][n], split_axis[n]) for l in range(DEPTH) for n in names]
    from_sibling = _swap_halves(items, "swap_grad_halves")
    chip_sums = [(_chip_sum(a, r, axis, core, f"chip_sum{k}"), shard_axis[names[k % 3]])
                 for k, ((a, axis), r) in enumerate(zip(items, from_sibling))]
    small_part, spans = _pack_small(grads)
    by_chip, small_all = _exchange_grads(chip_sums, small_part)
    bufs = []
    for i, n in enumerate(names):
        dest = None
        for l in range(DEPTH):
            k = l * 3 + i
            dest = _sum_chips(by_chip[k], chip_sums[k][0], shard_axis[n], split_axis[n], pos, dest, l, f"sum_chips{k}")
        bufs.append((dest, split_axis[n]))
    g_w_in, g_w_branch, g_w_out = _share_halves(bufs, "share_grad_halves")

    small_sum = _sum_slots(small_all, F32, "sum_small")
    small = {}
    for name, like in (("pre_g", pre_norm_g), ("pool_w", pool_w), ("pool_scale", pool_scale), ("conv_b", conv_b),
                       ("post_g", post_norm_g)):
        at, n = spans[name]
        small[name] = small_sum[at:at + n].reshape(like.shape)
    at, n = spans["conv_w"]
    g_conv_w_full = small_sum[at:at + n].reshape(DEPTH, 3, WIDTH)
    g_conv_w = lax.dynamic_slice_in_dim(g_conv_w_full, chip * conv_w.shape[2], conv_w.shape[2], axis=2)

    g = dict(pre_norm_g=small["pre_g"], w_in=g_w_in, pool_w=small["pool_w"], pool_scale=small["pool_scale"],
             conv_w=g_conv_w, conv_b=small["conv_b"], w_branch=g_w_branch, w_out=g_w_out, post_norm_g=small["post_g"])
    w = dict(pre_norm_g=pre_norm_g, w_in=w_in, pool_w=pool_w, pool_scale=pool_scale, conv_w=conv_w, conv_b=conv_b,
             w_branch=w_branch, w_out=w_out, post_norm_g=post_norm_g)
    m = dict(pre_norm_g=m_pre_norm_g, w_in=m_w_in, pool_w=m_pool_w, pool_scale=m_pool_scale, conv_w=m_conv_w,
             conv_b=m_conv_b, w_branch=m_w_branch, w_out=m_w_out, post_norm_g=m_post_norm_g)
    v = dict(pre_norm_g=v_pre_norm_g, w_in=v_w_in, pool_w=v_pool_w, pool_scale=v_pool_scale, conv_w=v_conv_w,
             conv_b=v_conv_b, w_branch=v_w_branch, w_out=v_w_out, post_norm_g=v_post_norm_g)
    order = ["pre_norm_g", "w_in", "pool_w", "pool_scale", "conv_w", "conv_b", "w_branch", "w_out", "post_norm_g"]
    upd = {n: _adamw(w[n], g[n], m[n], v[n], "adamw_" + n) for n in order}
    return (loss, grad_x, *[g[n] for n in order], *[upd[n][0] for n in order], *[upd[n][1] for n in order],
            *[upd[n][2] for n in order])
```

```python
import functools

import jax
import jax.numpy as jnp
from jax import lax
from jax.experimental import pallas as pl
from jax.experimental.pallas import tpu as pltpu

F32 = jnp.float32
BF16 = jnp.bfloat16
MESH = pl.DeviceIdType.MESH
ANY = pl.BlockSpec(memory_space=pl.ANY)

DEPTH = 2
D_MODEL = 1024
WIDTH = 512
N_IN = 8192
N_CHIPS = 4
HEAD_DIM = 64
RMS_EPS = 1e-6
POOL_HALO = 16
CONV_HALO = 8
LANE = 128
COL_POOL_V, COL_POOL_G = 0, 4
COL_CONV_X, COL_CONV_GB, COL_CONV_GC, COL_CONV_G = 8, 12, 16, 20
COL_SB_Q, COL_SB_K, COL_SB_V, COL_SB_G = 24, 28, 32, 36
COL_MERGE_1024 = 5

ADAM_LR, ADAM_B1, ADAM_B2, ADAM_EPS, ADAM_WD, ADAM_STEP = 0.001, 0.9, 0.999, 1e-08, 0.01, 10

NN = (((1,), (0,)), ((), ()))
NT = (((1,), (1,)), ((), ()))
TN = (((0,), (0,)), ((), ()))


def _sigmoid(x):
    return 1.0 / (1.0 + jnp.exp(-x))


def _silu_and_grad(x):
    s = _sigmoid(x)
    return x * s, s * (1.0 + x * (1.0 - s))


def _dot(a, b, dims):
    return lax.dot_general(a, b, dims, preferred_element_type=F32)


def _matmul(a, b, mode, out_dtype, name, tm=1024, tn=1024, tk=1024, b_lead=()):
    b_shape = b.shape[len(b_lead):]
    if mode == "nn":
        (m, k), (k2, n) = a.shape, b_shape
    elif mode == "nt":
        (m, k), (n, k2) = a.shape, b_shape
    else:
        (k, m), (k2, n) = a.shape, b_shape
    assert k == k2 and a.dtype == BF16 and b.dtype == BF16
    tm, tn, tk = min(tm, m), min(tn, n), min(tk, k)
    assert m % tm == 0 and n % tn == 0 and k % tk == 0
    nk = k // tk
    dims = {"nn": NN, "nt": NT, "tn": TN}[mode]

    def body(a_ref, b_ref, o_ref, *scratch):
        p = _dot(a_ref[...], b_ref[...], dims)
        if nk == 1:
            o_ref[...] = p.astype(o_ref.dtype)
        else:
            acc = scratch[0]
            kk = pl.program_id(2)

            @pl.when(kk == 0)
            def _():
                acc[...] = p

            @pl.when(jnp.logical_and(kk > 0, kk < nk - 1))
            def _():
                acc[...] += p

            @pl.when(kk == nk - 1)
            def _():
                o_ref[...] = (acc[...] + p).astype(o_ref.dtype)

    if mode == "tn":
        a_spec = pl.BlockSpec((tk, tm), lambda i, j, kk: (kk, i))
    else:
        a_spec = pl.BlockSpec((tm, tk), lambda i, j, kk: (i, kk))
    squeezed = (None,) * len(b_lead)
    if mode == "nt":
        b_spec = pl.BlockSpec(squeezed + (tn, tk), lambda i, j, kk: (*b_lead, j, kk))
    else:
        b_spec = pl.BlockSpec(squeezed + (tk, tn), lambda i, j, kk: (*b_lead, kk, j))
    return pl.pallas_call(
        body, name=name,
        grid=(m // tm, n // tn, nk),
        in_specs=[a_spec, b_spec],
        out_specs=pl.BlockSpec((tm, tn), lambda i, j, kk: (i, j)),
        out_shape=jax.ShapeDtypeStruct((m, n), out_dtype),
        scratch_shapes=[pltpu.VMEM((tm, tn), F32)] if nk > 1 else [],
        compiler_params=pltpu.CompilerParams(dimension_semantics=("parallel", "parallel", "arbitrary")),
    )(a, b)


def _rms_fwd(x, g, name, ts=512):
    s, d = x.shape

    def body(x_ref, g_ref, h_ref):
        xv = x_ref[...]
        r = lax.rsqrt(jnp.mean(xv * xv, axis=-1, keepdims=True) + RMS_EPS)
        h_ref[...] = (xv * r * g_ref[...]).astype(BF16)

    return pl.pallas_call(
        body, name=name, grid=(s // ts,),
        in_specs=[pl.BlockSpec((ts, d), lambda i: (i, 0)), pl.BlockSpec((1, d), lambda i: (0, 0))],
        out_specs=pl.BlockSpec((ts, d), lambda i: (i, 0)),
        out_shape=jax.ShapeDtypeStruct((s, d), BF16),
    )(x, g)


def _rms_bwd(xin, g, dh, resid, out_dtype, name, ts=512):
    s, d = xin.shape
    has_resid = resid is not None

    def body(*refs):
        if has_resid:
            x_ref, g_ref, dh_ref, res_ref, dx_ref, dg_ref = refs
        else:
            x_ref, g_ref, dh_ref, dx_ref, dg_ref = refs
        xv = x_ref[...]
        dhv = dh_ref[...].astype(F32)
        r = lax.rsqrt(jnp.mean(xv * xv, axis=-1, keepdims=True) + RMS_EPS)
        nrm = xv * r
        dn = dhv * g_ref[...]
        dx = r * (dn - nrm * jnp.mean(dn * nrm, axis=-1, keepdims=True))
        if has_resid:
            dx = dx + res_ref[...]
        dx_ref[...] = dx.astype(dx_ref.dtype)
        part = jnp.sum(dhv * nrm, axis=0, keepdims=True)

        @pl.when(pl.program_id(0) == 0)
        def _():
            dg_ref[...] = part

        @pl.when(pl.program_id(0) > 0)
        def _():
            dg_ref[...] += part

    row = pl.BlockSpec((ts, d), lambda i: (i, 0))
    vec = pl.BlockSpec((1, d), lambda i: (0, 0))
    ins = [xin, g, dh] + ([resid] if has_resid else [])
    return pl.pallas_call(
        body, name=name, grid=(s // ts,),
        in_specs=[row, vec, row] + ([row] if has_resid else []),
        out_specs=[row, vec],
        out_shape=[jax.ShapeDtypeStruct((s, d), out_dtype), jax.ShapeDtypeStruct((1, d), F32)],
        compiler_params=pltpu.CompilerParams(dimension_semantics=("arbitrary",)),
    )(*ins)


def _resid_out(x, out, g, target, name, ts=512):
    s, d = x.shape
    has_loss = target is not None

    def body(*refs):
        if has_loss:
            x_ref, o_ref, g_ref, t_ref, dy_ref, loss_ref = refs
        else:
            x_ref, o_ref, g_ref, y_ref = refs
        ov = o_ref[...]
        r = lax.rsqrt(jnp.mean(ov * ov, axis=-1, keepdims=True) + RMS_EPS)
        yv = x_ref[...] + ov * r * g_ref[...]
        if not has_loss:
            y_ref[...] = yv
            return
        err = yv - t_ref[...]
        dy_ref[...] = err * (1.0 / d)
        part = jnp.sum(jnp.sum(err * err, axis=-1, keepdims=True), axis=0, keepdims=True) * (0.5 / d)
        part = jnp.broadcast_to(part, (1, LANE))

        @pl.when(pl.program_id(0) == 0)
        def _():
            loss_ref[...] = part

        @pl.when(pl.program_id(0) > 0)
        def _():
            loss_ref[...] += part

    row = pl.BlockSpec((ts, d), lambda i: (i, 0))
    vec = pl.BlockSpec((1, d), lambda i: (0, 0))
    if has_loss:
        return pl.pallas_call(
            body, name=name, grid=(s // ts,),
            in_specs=[row, row, vec, row],
            out_specs=[row, pl.BlockSpec((1, LANE), lambda i: (0, 0))],
            out_shape=[jax.ShapeDtypeStruct((s, d), F32), jax.ShapeDtypeStruct((1, LANE), F32)],
            compiler_params=pltpu.CompilerParams(dimension_semantics=("arbitrary",)),
        )(x, out, g, target)
    return pl.pallas_call(
        body, name=name, grid=(s // ts,),
        in_specs=[row, row, vec], out_specs=row,
        out_shape=jax.ShapeDtypeStruct((s, d), F32),
    )(x, out, g)


def _rows_before(ref, start, n, halo):
    if start == 0:
        return jnp.concatenate([jnp.zeros((halo, ref.shape[1]), F32), ref[0:n, :]], axis=0)
    return ref[start - halo:start + n, :]


def _rows_after(ref, start, n, halo):
    if start + n == ref.shape[0]:
        return jnp.concatenate([ref[start:start + n, :].astype(F32), jnp.zeros((halo, ref.shape[1]), F32)], axis=0)
    return ref[start:start + n + halo, :].astype(F32)


def _pick_window(group, s2, s4, s8, s16):
    return jnp.where(group == 0, s2, jnp.where(group == 1, s4, jnp.where(group == 2, s8, s16)))


def _trailing_sums(ext, group):
    s2 = ext + pltpu.roll(ext, 1, 0)
    s4 = s2 + pltpu.roll(s2, 2, 0)
    s8 = s4 + pltpu.roll(s4, 4, 0)
    s16 = s8 + pltpu.roll(s8, 8, 0)
    return _pick_window(group, s2, s4, s8, s16)


def _leading_sums(ext, group):
    n = ext.shape[0]
    s2 = ext + pltpu.roll(ext, n - 1, 0)
    s4 = s2 + pltpu.roll(s2, n - 2, 0)
    s8 = s4 + pltpu.roll(s4, n - 4, 0)
    s16 = s8 + pltpu.roll(s8, n - 8, 0)
    return _pick_window(group, s2, s4, s8, s16)


def _window_count(start, n, group):
    pos = start + lax.broadcasted_iota(jnp.int32, (n, LANE), 0)
    return jnp.minimum(pos + 1, 2 << group).astype(F32)


def _pooled(v_ref, start, n, group):
    ext = _rows_before(v_ref, start, n, POOL_HALO)
    sums = _trailing_sums(ext, group)[POOL_HALO:, :]
    return sums / _window_count(start, n, group) - ext[POOL_HALO:, :]


def _pool_fwd(u, pool_w, pool_scale, name, ts=512):
    s = u.shape[0]

    def body(v_ref, gate_ref, w_ref, sc_ref, y_ref):
        group = pl.program_id(0)
        for c in range(s // ts):
            a = c * ts
            pooled = _pooled(v_ref, a, ts, group)
            mixed = _dot(pooled.astype(BF16), w_ref[...], NN)
            gate = gate_ref[a:a + ts, :]
            y_ref[a:a + ts, :] = (mixed * sc_ref[...] * (gate * _sigmoid(gate))).astype(BF16)

    col = lambda base: pl.BlockSpec((s, LANE), lambda g: (0, base + g))
    return pl.pallas_call(
        body, name=name, grid=(4,),
        in_specs=[col(COL_POOL_V), col(COL_POOL_G),
                  pl.BlockSpec((None, LANE, LANE), lambda g: (g, 0, 0)),
                  pl.BlockSpec((1, LANE), lambda g: (0, g))],
        out_specs=pl.BlockSpec((s, LANE), lambda g: (0, g)),
        out_shape=jax.ShapeDtypeStruct((s, WIDTH), BF16),
    )(u, u, pool_w, pool_scale)


def _pool_bwd(u, dy, pool_w, pool_scale, name, ts=512):
    s = u.shape[0]

    def body(v_ref, gate_ref, dy_ref, w_ref, sc_ref, dv_ref, dgate_ref, dw_ref, dsc_ref):
        group = pl.program_id(0)
        w = w_ref[...]
        scale = sc_ref[...]
        dw = jnp.zeros((LANE, LANE), F32)
        dsc = jnp.zeros((1, LANE), F32)
        for c in range(s // ts):
            a = c * ts
            n_ext = ts + POOL_HALO
            gate_e = _rows_after(gate_ref, a, ts, POOL_HALO)
            dy_e = _rows_after(dy_ref, a, ts, POOL_HALO)
            silu_e, dsilu_e = _silu_and_grad(gate_e)
            dms_e = dy_e * silu_e
            dm_e = (dms_e * scale).astype(BF16)
            dpool_e = _dot(dm_e, w, NT)
            spread = _leading_sums(dpool_e / _window_count(a, n_ext, group), group)
            dv_ref[a:a + ts, :] = (spread[0:ts, :] - dpool_e[0:ts, :]).astype(BF16)
            pooled = _pooled(v_ref, a, ts, group).astype(BF16)
            mixed = _dot(pooled, w, NN)
            dgate_ref[a:a + ts, :] = (dy_e[0:ts, :] * mixed * scale * dsilu_e[0:ts, :]).astype(BF16)
            dsc = dsc + jnp.sum(dms_e[0:ts, :] * mixed, axis=0, keepdims=True)
            dw = dw + _dot(pooled, dm_e[0:ts, :], TN)
        dw_ref[...] = dw
        dsc_ref[...] = dsc

    col = lambda base: pl.BlockSpec((s, LANE), lambda g: (0, base + g))
    out_col = pl.BlockSpec((s, LANE), lambda g: (0, g))
    return pl.pallas_call(
        body, name=name, grid=(4,),
        in_specs=[col(COL_POOL_V), col(COL_POOL_G), out_col,
                  pl.BlockSpec((None, LANE, LANE), lambda g: (g, 0, 0)),
                  pl.BlockSpec((1, LANE), lambda g: (0, g))],
        out_specs=[out_col, out_col,
                   pl.BlockSpec((None, LANE, LANE), lambda g: (g, 0, 0)),
                   pl.BlockSpec((1, LANE), lambda g: (0, g))],
        out_shape=[jax.ShapeDtypeStruct((s, WIDTH), BF16), jax.ShapeDtypeStruct((s, WIDTH), BF16),
                   jax.ShapeDtypeStruct((4, LANE, LANE), F32), jax.ShapeDtypeStruct((1, WIDTH), F32)],
    )(u, u, dy, pool_w, pool_scale)


def _conv_taps(x_ref, gc_ref, start, n):
    z_ext = _rows_before(gc_ref, start, n, CONV_HALO) * _rows_before(x_ref, start, n, CONV_HALO)
    z0 = z_ext[CONV_HALO:, :]
    z1 = pltpu.roll(z_ext, 1, 0)[CONV_HALO:, :]
    z2 = pltpu.roll(z_ext, 2, 0)[CONV_HALO:, :]
    return z0, z1, z2


def _conv_fwd(u, conv_w, conv_b, name, ts=512):
    s = u.shape[0]

    def body(x_ref, gb_ref, gc_ref, g_ref, w_ref, b_ref, y_ref):
        w0, w1, w2 = w_ref[0:1, :], w_ref[1:2, :], w_ref[2:3, :]
        for c in range(s // ts):
            a = c * ts
            z0, z1, z2 = _conv_taps(x_ref, gc_ref, a, ts)
            y = w2 * z0 + w1 * z1 + w0 * z2 + b_ref[...]
            gate = g_ref[a:a + ts, :]
            y_ref[a:a + ts, :] = (gb_ref[a:a + ts, :] * y * (gate * _sigmoid(gate))).astype(BF16)

    col = lambda base: pl.BlockSpec((s, LANE), lambda j: (0, base + j))
    return pl.pallas_call(
        body, name=name, grid=(4,),
        in_specs=[col(COL_CONV_X), col(COL_CONV_GB), col(COL_CONV_GC), col(COL_CONV_G),
                  pl.BlockSpec((3, LANE), lambda j: (0, j)), pl.BlockSpec((1, LANE), lambda j: (0, j))],
        out_specs=pl.BlockSpec((s, LANE), lambda j: (0, j)),
        out_shape=jax.ShapeDtypeStruct((s, WIDTH), BF16),
    )(u, u, u, u, conv_w, conv_b)


def _conv_bwd(u, dy, conv_w, conv_b, name, ts=512):
    s = u.shape[0]

    def body(x_ref, gb_ref, gc_ref, g_ref, dy_ref, w_ref, b_ref,
             dx_ref, dgb_ref, dgc_ref, dg_ref, dw_ref, db_ref):
        w0, w1, w2 = w_ref[0:1, :], w_ref[1:2, :], w_ref[2:3, :]
        acc = [jnp.zeros((1, LANE), F32) for _ in range(4)]
        for c in range(s // ts):
            a = c * ts
            n_ext = ts + CONV_HALO
            gate_e = _rows_after(g_ref, a, ts, CONV_HALO)
            silu_e, dsilu_e = _silu_and_grad(gate_e)
            dy_e = _rows_after(dy_ref, a, ts, CONV_HALO)
            gb_e = _rows_after(gb_ref, a, ts, CONV_HALO)
            dyy_e = dy_e * silu_e * gb_e
            dz = (w2 * dyy_e + w1 * pltpu.roll(dyy_e, n_ext - 1, 0) + w0 * pltpu.roll(dyy_e, n_ext - 2, 0))[0:ts, :]
            z0, z1, z2 = _conv_taps(x_ref, gc_ref, a, ts)
            yb = w2 * z0 + w1 * z1 + w0 * z2 + b_ref[...]
            dyv = dy_e[0:ts, :]
            dyy = dyy_e[0:ts, :]
            dg_ref[a:a + ts, :] = (dyv * gb_e[0:ts, :] * yb * dsilu_e[0:ts, :]).astype(BF16)
            dgb_ref[a:a + ts, :] = (dyv * silu_e[0:ts, :] * yb).astype(BF16)
            dx_ref[a:a + ts, :] = (dz * gc_ref[a:a + ts, :]).astype(BF16)
            dgc_ref[a:a + ts, :] = (dz * x_ref[a:a + ts, :]).astype(BF16)
            for i, term in enumerate((dyy * z2, dyy * z1, dyy * z0, dyy)):
                acc[i] = acc[i] + jnp.sum(term, axis=0, keepdims=True)
        dw_ref[0:1, :] = acc[0]
        dw_ref[1:2, :] = acc[1]
        dw_ref[2:3, :] = acc[2]
        db_ref[...] = acc[3]

    col = lambda base: pl.BlockSpec((s, LANE), lambda j: (0, base + j))
    out_col = pl.BlockSpec((s, LANE), lambda j: (0, j))
    big = jax.ShapeDtypeStruct((s, WIDTH), BF16)
    return pl.pallas_call(
        body, name=name, grid=(4,),
        in_specs=[col(COL_CONV_X), col(COL_CONV_GB), col(COL_CONV_GC), col(COL_CONV_G), out_col,
                  pl.BlockSpec((3, LANE), lambda j: (0, j)), pl.BlockSpec((1, LANE), lambda j: (0, j))],
        out_specs=[out_col, out_col, out_col, out_col,
                   pl.BlockSpec((3, LANE), lambda j: (0, j)), pl.BlockSpec((1, LANE), lambda j: (0, j))],
        out_shape=[big, big, big, big,
                   jax.ShapeDtypeStruct((3, WIDTH), F32), jax.ShapeDtypeStruct((1, WIDTH), F32)],
    )(u, u, u, u, dy, conv_w, conv_b)


LOG2_E = 1.4426950408889634
LN_2 = 0.6931471805599453


def _sb_scores(q_h, k_blk, valid, later_mat, carry):
    z = _dot(q_h, k_blk, NT)
    neg_z = -z
    soft = jnp.log(1.0 + jnp.exp2(jnp.minimum(z, neg_z))) * LOG2_E
    log_keep = jnp.minimum(neg_z, 0.0) - soft
    log_beta = log_keep + z
    if valid is not None:
        log_keep = jnp.where(valid, log_keep, 0.0)
    later = _dot(log_keep.astype(BF16), later_mat, NN) + carry
    return log_keep, log_beta, later


def _masked(valid, x):
    return x if valid is None else jnp.where(valid, x, 0.0)


def _diagonal_masks(tq, tk):
    r = lax.broadcasted_iota(jnp.int32, (tq, tk), 0)
    cidx = lax.broadcasted_iota(jnp.int32, (tq, tk), 1)
    return [cidx + d * tk < r for d in range(tq // tk)]


def _triangle(tk, op):
    r = lax.broadcasted_iota(jnp.int32, (tk, tk), 0)
    cidx = lax.broadcasted_iota(jnp.int32, (tk, tk), 1)
    return op(r, cidx).astype(BF16)


def _sb_fwd(u, name, t=512, tk=256, pairs=4):
    s = u.shape[0]
    assert s // tk <= LANE and 4 % pairs == 0 and t % tk == 0
    scale = HEAD_DIM ** -0.5
    nh = 2 * pairs
    wide = pairs * LANE
    ratio = t // tk

    def body(q_ref, k_ref, v_ref, g_ref, o_ref, y_ref, after_ref, kb_ref, vb_ref, acc_ref, carry_ref):
        i = pl.program_id(1)

        @pl.when(i == 0)
        def _():
            kb_ref[...] = k_ref[...].astype(BF16)
            vb_ref[...] = v_ref[...].astype(BF16)

        lane = lax.broadcasted_iota(jnp.int32, (t, LANE), 1)
        first = lane < HEAD_DIM
        after_ref[...] = jnp.zeros_like(after_ref)
        qv = q_ref[...] * (scale * LOG2_E)
        q_heads = []
        for p in range(pairs):
            qp = qv[:, p * LANE:(p + 1) * LANE]
            q_heads += [jnp.where(first, qp, 0.0).astype(BF16), jnp.where(first, 0.0, qp).astype(BF16)]
        later_mat = _triangle(tk, lambda r, cidx: r > cidx)
        acc_ref[...] = jnp.zeros_like(acc_ref)
        carry_ref[...] = jnp.zeros_like(carry_ref)

        def block(kb, valid):
            rows = pl.ds(pl.multiple_of(kb * tk, tk), tk)
            k_blk = kb_ref[rows, :]
            v_blk = vb_ref[rows, :]
            carries = [carry_ref[h] for h in range(nh)]
            afters = [after_ref[:, h * LANE:(h + 1) * LANE] for h in range(nh)]
            accs = [acc_ref[h] for h in range(nh)]
            outs = []
            for h in range(nh):
                cols = slice((h // 2) * LANE, (h // 2 + 1) * LANE)
                log_keep, log_beta, later = _sb_scores(q_heads[h], k_blk[:, cols], valid, later_mat, carries[h])
                a = _masked(valid, jnp.exp2(log_beta + later))
                outs.append((accs[h] + _dot(a.astype(BF16), v_blk[:, cols], NN),
                             carries[h] + jnp.sum(log_keep, axis=1, keepdims=True),
                             jnp.where(lane == kb, carries[h], afters[h])))
            for h in range(nh):
                acc_ref[h] = outs[h][0]
                carry_ref[h] = outs[h][1]
                after_ref[:, h * LANE:(h + 1) * LANE] = outs[h][2]

        def step(j, _):
            block(ratio * i - 1 - j, None)
            return 0

        masks = _diagonal_masks(t, tk)
        for d in reversed(range(ratio)):
            block(ratio * i + d, masks[d])
        lax.fori_loop(0, ratio * i, step, 0)
        for p in range(pairs):
            cols = slice(p * LANE, (p + 1) * LANE)
            o = jnp.where(first, acc_ref[2 * p], acc_ref[2 * p + 1])
            o_ref[:, cols] = o
            gate = g_ref[:, cols]
            y_ref[:, cols] = (o * gate * _sigmoid(gate)).astype(BF16)

    blk = lambda base: pl.BlockSpec((t, wide), lambda g, i: (i, base // pairs + g))
    full = lambda base: pl.BlockSpec((s, wide), lambda g, i: (0, base // pairs + g))
    out_blk = pl.BlockSpec((t, wide), lambda g, i: (i, g))
    return pl.pallas_call(
        body, name=name, grid=(4 // pairs, s // t),
        in_specs=[blk(COL_SB_Q), full(COL_SB_K), full(COL_SB_V), blk(COL_SB_G)],
        out_specs=[out_blk, out_blk, pl.BlockSpec((t, nh * LANE), lambda g, i: (i, g))],
        out_shape=[jax.ShapeDtypeStruct((s, WIDTH), F32), jax.ShapeDtypeStruct((s, WIDTH), BF16),
                   jax.ShapeDtypeStruct((s, 8 * LANE), F32)],
        scratch_shapes=[pltpu.VMEM((s, wide), BF16), pltpu.VMEM((s, wide), BF16),
                        pltpu.VMEM((nh, t, LANE), F32), pltpu.VMEM((nh, t, 1), F32)],
        compiler_params=pltpu.CompilerParams(dimension_semantics=("arbitrary", "arbitrary")),
    )(u, u, u, u)


def _sb_bwd(u, o, after, dy, name, t=512, tk=256, pairs=2):
    s = u.shape[0]
    nq = s // t
    scale = HEAD_DIM ** -0.5
    nh = 2 * pairs
    wide = pairs * LANE
    ratio = t // tk

    def body(q_ref, k_ref, v_ref, g_ref, o_ref, after_ref, dy_ref, dq_ref, dk_ref, dv_ref, dg_ref,
             kb_ref, vb_ref, dk_acc, dv_acc, dq_acc, carry_ref):
        i = pl.program_id(1)

        @pl.when(i == 0)
        def _():
            kb_ref[...] = k_ref[...].astype(BF16)
            vb_ref[...] = v_ref[...].astype(BF16)
            dk_acc[...] = jnp.zeros_like(dk_acc)
            dv_acc[...] = jnp.zeros_like(dv_acc)

        lane = lax.broadcasted_iota(jnp.int32, (t, LANE), 1)
        first = lane < HEAD_DIM
        gate = g_ref[...]
        silu, dsilu = _silu_and_grad(gate)
        dyv = dy_ref[...]
        do = dyv * silu
        dg_ref[...] = (dyv * o_ref[...] * dsilu).astype(BF16)
        qv = q_ref[...] * (scale * LOG2_E)
        do_heads, q_heads = [], []
        for p in range(pairs):
            cols = slice(p * LANE, (p + 1) * LANE)
            do_heads += [jnp.where(first, do[:, cols], 0.0).astype(BF16), jnp.where(first, 0.0, do[:, cols]).astype(BF16)]
            q_heads += [jnp.where(first, qv[:, cols], 0.0).astype(BF16), jnp.where(first, 0.0, qv[:, cols]).astype(BF16)]
        later_mat = _triangle(tk, lambda r, cidx: r > cidx)
        before_mat = _triangle(tk, lambda r, cidx: r < cidx)
        dq_acc[...] = jnp.zeros_like(dq_acc)
        carry_ref[...] = jnp.zeros_like(carry_ref)

        def block(kb, valid):
            rows = pl.ds(pl.multiple_of(kb * tk, tk), tk)
            k_blk = kb_ref[rows, :]
            v_blk = vb_ref[rows, :]
            carries = [carry_ref[h] for h in range(nh)]
            dq_old = [dq_acc[h] for h in range(nh)]
            dk_old = dk_acc[rows, :]
            dv_old = dv_acc[rows, :]
            outs = []
            for h in range(nh):
                cols = slice((h // 2) * LANE, (h // 2 + 1) * LANE)
                after = jnp.sum(jnp.where(lane == kb, after_ref[:, h * LANE:(h + 1) * LANE], 0.0), axis=1, keepdims=True)
                _, log_beta, later = _sb_scores(q_heads[h], k_blk[:, cols], valid, later_mat, after)
                beta = jnp.exp2(log_beta)
                a = _masked(valid, jnp.exp2(log_beta + later))
                da = _dot(do_heads[h], v_blk[:, cols], NT)
                gterm = a * da
                before = _dot(gterm.astype(BF16), before_mat, NN) + carries[h]
                dz_b = _masked(valid, gterm * (1.0 - beta) - beta * before).astype(BF16)
                outs.append((dq_old[h] + _dot(dz_b, k_blk[:, cols], NN), _dot(dz_b, q_heads[h], TN),
                             _dot(a.astype(BF16), do_heads[h], TN),
                             carries[h] + jnp.sum(gterm, axis=1, keepdims=True)))
            for h in range(nh):
                dq_acc[h] = outs[h][0]
                carry_ref[h] = outs[h][3]
            dk_new = [outs[2 * p][1] + outs[2 * p + 1][1] for p in range(pairs)]
            dv_new = [outs[2 * p][2] + outs[2 * p + 1][2] for p in range(pairs)]
            dk_acc[rows, :] = dk_old + (dk_new[0] if pairs == 1 else jnp.concatenate(dk_new, axis=1))
            dv_acc[rows, :] = dv_old + (dv_new[0] if pairs == 1 else jnp.concatenate(dv_new, axis=1))

        def step(kb, _):
            block(kb, None)
            return 0

        lax.fori_loop(0, ratio * i, step, 0)
        masks = _diagonal_masks(t, tk)
        for d in range(ratio):
            block(ratio * i + d, masks[d])
        for p in range(pairs):
            dq_ref[:, p * LANE:(p + 1) * LANE] = (jnp.where(first, dq_acc[2 * p], dq_acc[2 * p + 1]) * scale).astype(BF16)

        @pl.when(i == nq - 1)
        def _():
            dk_ref[...] = (dk_acc[...] * LN_2).astype(BF16)
            dv_ref[...] = dv_acc[...].astype(BF16)

    blk = lambda base: pl.BlockSpec((t, wide), lambda g, i: (i, base // pairs + g))
    full = lambda base: pl.BlockSpec((s, wide), lambda g, i: (0, base // pairs + g))
    out_blk = pl.BlockSpec((t, wide), lambda g, i: (i, g))
    out_full = pl.BlockSpec((s, wide), lambda g, i: (0, g))
    big = jax.ShapeDtypeStruct((s, WIDTH), BF16)
    return pl.pallas_call(
        body, name=name, grid=(4 // pairs, nq),
        in_specs=[blk(COL_SB_Q), full(COL_SB_K), full(COL_SB_V), blk(COL_SB_G), out_blk,
                  pl.BlockSpec((t, nh * LANE), lambda g, i: (i, g)), out_blk],
        out_specs=[out_blk, out_full, out_full, out_blk],
        out_shape=[big, big, big, big],
        scratch_shapes=[pltpu.VMEM((s, wide), BF16), pltpu.VMEM((s, wide), BF16),
                        pltpu.VMEM((s, wide), F32), pltpu.VMEM((s, wide), F32),
                        pltpu.VMEM((nh, t, LANE), F32), pltpu.VMEM((nh, t, 1), F32)],
        compiler_params=pltpu.CompilerParams(dimension_semantics=("arbitrary", "arbitrary")),
    )(u, u, u, u, o, after, dy)


def _gate_fwd(u, projs, name, ts=256):
    s = u.shape[0]

    def body(m0, m1, m2, p0, p1, p2, out_ref):
        tot = None
        for m_ref, p_ref in ((m0, p0), (m1, p1), (m2, p2)):
            term = _sigmoid(m_ref[...]) * p_ref[...].astype(F32)
            tot = term if tot is None else tot + term
        out_ref[...] = tot.astype(BF16)

    mspec = lambda n: pl.BlockSpec((ts, D_MODEL), lambda i: (i, COL_MERGE_1024 + n))
    row = pl.BlockSpec((ts, D_MODEL), lambda i: (i, 0))
    return pl.pallas_call(
        body, name=name, grid=(s // ts,),
        in_specs=[mspec(0), mspec(1), mspec(2), row, row, row],
        out_specs=row, out_shape=jax.ShapeDtypeStruct((s, D_MODEL), BF16),
    )(u, u, u, *projs)


def _gate_bwd(u, projs, dmerged, name, ts=256):
    s = u.shape[0]

    def body(m0, m1, m2, p0, p1, p2, dm_ref, dp0, dp1, dp2, dl0, dl1, dl2):
        dm = dm_ref[...].astype(F32)
        for m_ref, p_ref, dp_ref, dl_ref in ((m0, p0, dp0, dl0), (m1, p1, dp1, dl1), (m2, p2, dp2, dl2)):
            gate = _sigmoid(m_ref[...])
            dp_ref[...] = (dm * gate).astype(BF16)
            dl_ref[...] = (dm * p_ref[...].astype(F32) * gate * (1.0 - gate)).astype(BF16)

    mspec = lambda n: pl.BlockSpec((ts, D_MODEL), lambda i: (i, COL_MERGE_1024 + n))
    row = pl.BlockSpec((ts, D_MODEL), lambda i: (i, 0))
    big = jax.ShapeDtypeStruct((s, D_MODEL), BF16)
    outs = pl.pallas_call(
        body, name=name, grid=(s // ts,),
        in_specs=[mspec(0), mspec(1), mspec(2), row, row, row, row],
        out_specs=[row] * 6, out_shape=[big] * 6,
    )(u, u, u, *projs, dmerged)
    return outs[:3], outs[3:]


def _as_rows(a):
    return a.reshape(-1, a.shape[-1])


def _row_tile(rows, cols, bytes_per_row_elem=4, cap=1 << 20):
    tr = rows
    while tr * cols * bytes_per_row_elem > cap and tr % 2 == 0 and (tr // 2) % 16 == 0:
        tr //= 2
    return tr


def _cast_bf16(a, name):
    a2 = _as_rows(a)
    rows, cols = a2.shape
    tr = _row_tile(rows, cols)

    def body(a_ref, o_ref):
        o_ref[...] = a_ref[...].astype(BF16)

    spec = pl.BlockSpec((tr, cols), lambda i: (i, 0))
    out = pl.pallas_call(body, name=name, grid=(rows // tr,), in_specs=[spec], out_specs=spec,
                         out_shape=jax.ShapeDtypeStruct((rows, cols), BF16))(a2)
    return out.reshape(a.shape)


def _adamw(w, g, m, v, name):
    shape = w.shape
    w2, g2, m2, v2 = (_as_rows(a) for a in (w, g, m, v))
    rows, cols = w2.shape
    tr = _row_tile(rows, cols)
    c1 = 1.0 - ADAM_B1 ** ADAM_STEP
    c2 = 1.0 - ADAM_B2 ** ADAM_STEP

    def body(w_ref, g_ref, m_ref, v_ref, d_ref, nm_ref, nv_ref):
        gv = g_ref[...]
        nm = ADAM_B1 * m_ref[...] + (1.0 - ADAM_B1) * gv
        nv = ADAM_B2 * v_ref[...] + (1.0 - ADAM_B2) * (gv * gv)
        nm_ref[...] = nm
        nv_ref[...] = nv
        d_ref[...] = -ADAM_LR * ((nm / c1) / (jnp.sqrt(nv / c2) + ADAM_EPS) + ADAM_WD * w_ref[...])

    spec = pl.BlockSpec((tr, cols), lambda i: (i, 0))
    sds = jax.ShapeDtypeStruct((rows, cols), F32)
    outs = pl.pallas_call(body, name=name, grid=(rows // tr,), in_specs=[spec] * 4, out_specs=[spec] * 3,
                          out_shape=[sds] * 3)(w2, g2, m2, v2)
    return tuple(o.reshape(shape) for o in outs)


def _sum_slots(a, out_dtype, name):
    n = a.shape[0]
    a3 = a.reshape(n, -1, a.shape[-1])
    _, rows, cols = a3.shape
    tr = _row_tile(rows, cols * n)

    def body(a_ref, o_ref):
        tot = a_ref[0].astype(F32)
        for k in range(1, n):
            tot = tot + a_ref[k].astype(F32)
        o_ref[...] = tot.astype(out_dtype)

    out = pl.pallas_call(
        body, name=name, grid=(rows // tr,),
        in_specs=[pl.BlockSpec((n, tr, cols), lambda i: (0, i, 0))],
        out_specs=pl.BlockSpec((tr, cols), lambda i: (i, 0)),
        out_shape=jax.ShapeDtypeStruct((rows, cols), out_dtype))(a3)
    return out.reshape(a.shape[1:])


def _chip_sum(own, recv, axis, core, name):
    half = recv.shape
    nd = len(half)
    last = nd - 1
    if axis == last:
        tl, nt = half[last], 1
    else:
        tl = min(half[last], 2048)
        nt = half[last] // tl
    block = half[:last] + (tl,)

    def own_index(i, core_ref):
        idx = [0] * nd
        idx[last] = i
        if axis == last:
            idx[last] = core_ref[0]
        else:
            idx[axis] = core_ref[0]
        return tuple(idx)

    def recv_index(i, core_ref):
        idx = [0] * nd
        idx[last] = i
        return tuple(idx)

    def body(core_ref, own_ref, recv_ref, o_ref):
        o_ref[...] = (own_ref[...] + recv_ref[...]).astype(BF16)

    return pl.pallas_call(
        body, name=name,
        grid_spec=pltpu.PrefetchScalarGridSpec(
            num_scalar_prefetch=1, grid=(nt,),
            in_specs=[pl.BlockSpec(block, own_index), pl.BlockSpec(block, recv_index)],
            out_specs=pl.BlockSpec(block, recv_index)),
        out_shape=jax.ShapeDtypeStruct(half, BF16),
    )(core, own, recv)


def _mesh_position():
    return lax.axis_index("x"), lax.axis_index("y"), lax.axis_index("c")


def _other_chips(x, y):
    return [(1 - x, y), (x, 1 - y), (1 - x, 1 - y)]


ALL_FLIPS = [(0, 0, 1), (1, 0, 0), (0, 1, 0), (1, 1, 0), (1, 0, 1), (0, 1, 1), (1, 1, 1)]


def _half(ref, axis, which, size):
    idx = [slice(None)] * len(ref.shape)
    idx[axis] = pl.ds(which * size, size)
    return ref.at[tuple(idx)]


def _sub(ref, picks):
    idx = [slice(None)] * len(ref.shape)
    for axis, start, size in picks:
        idx[axis] = pl.ds(start, size)
    return ref.at[tuple(idx)]


def _remote(src, dst, sems_send, sems_recv, k, to):
    return pltpu.make_async_remote_copy(src_ref=src, dst_ref=dst, send_sem=sems_send.at[k], recv_sem=sems_recv.at[k],
                                        device_id=to, device_id_type=MESH)


def _cast_shard(w, shard_axis, pos, name, tr=512):
    shape = w.shape
    nd = len(shape)
    assert shard_axis in (nd - 1, nd - 2)
    rows, cols = shape[-2:]
    tr = min(tr, rows)
    nt = rows // tr
    lead = shape[:-2]
    full = list(shape)
    full[shard_axis] *= N_CHIPS
    block = (1,) * len(lead) + (tr, cols)

    def in_index(*args):
        return (*args[:-1], 0)

    def out_index(*args):
        *g, pos_ref = args
        if shard_axis == nd - 1:
            return (*g, pos_ref[1])
        return (*g[:-1], pos_ref[1] * nt + g[-1], 0)

    def body(pos_ref, a_ref, o_ref):
        o_ref[...] = a_ref[...].astype(BF16)

    return pl.pallas_call(
        body, name=name,
        grid_spec=pltpu.PrefetchScalarGridSpec(
            num_scalar_prefetch=1, grid=lead + (nt,),
            in_specs=[pl.BlockSpec(block, in_index)], out_specs=pl.BlockSpec(block, out_index)),
        out_shape=jax.ShapeDtypeStruct(tuple(full), BF16),
    )(pos, w)


def _gather_weights(w_in, w_branch, w_out, conv_w):
    layout = [(2, w_in.shape[2] // N_CHIPS, 1, w_in.shape[1] // 2),
              (3, w_branch.shape[3] // N_CHIPS, 2, w_branch.shape[2] // 2),
              (1, w_out.shape[1] // N_CHIPS, 2, w_out.shape[2] // 2)]
    n_big = len(layout)

    def body(win_in, wbr_in, wout_in, cw, win_f, wbr_f, wout_f, cw_f, send_sems, recv_sems, local_sems):
        x, y, c = _mesh_position()
        chips = _other_chips(x, y)
        sibling = (x, y, 1 - c)
        mine = 2 * x + y
        fulls = [win_f, wbr_f, wout_f]

        def place(t, chip, core):
            sh_axis, sh_size, half_axis, half_size = layout[t]
            return _sub(fulls[t], [(sh_axis, chip * sh_size, sh_size), (half_axis, core * half_size, half_size)])

        local = [pltpu.make_async_copy(cw, cw_f.at[mine], local_sems.at[0])]
        for cp in local:
            cp.start()

        sends = []
        for t in range(n_big):
            for k, chip in enumerate(chips):
                sends.append(_remote(place(t, mine, c), place(t, mine, c),
                                     send_sems, recv_sems, 6 * t + k, (*chip, c)))
        small_base = 6 * n_big
        for k, chip in enumerate(chips):
            sends.append(_remote(cw, cw_f.at[mine], send_sems, recv_sems, small_base + k, (*chip, c)))
        for cp in sends:
            cp.start()

        passed = []
        for t in range(n_big):
            for k, (px, py) in enumerate(chips):
                theirs = 2 * px + py
                _remote(place(t, theirs, c), place(t, theirs, c), send_sems, recv_sems, 6 * t + k, (px, py, c)).wait_recv()
                fwd = _remote(place(t, theirs, c), place(t, theirs, c), send_sems, recv_sems, 6 * t + 3 + k, sibling)
                fwd.start()
                passed.append(fwd)
        for t in range(n_big):
            for k, (px, py) in enumerate(chips):
                theirs = 2 * px + py
                _remote(place(t, theirs, 1 - c), place(t, theirs, 1 - c), send_sems, recv_sems, 6 * t + 3 + k,
                        sibling).wait_recv()
        for k, (px, py) in enumerate(chips):
            _remote(cw, cw_f.at[2 * px + py], send_sems, recv_sems, small_base + k, sibling).wait_recv()
        for cp in sends + passed:
            cp.wait_send()
        for cp in local:
            cp.wait()

    n_sems = 6 * n_big + 3
    out_shape = [jax.ShapeDtypeStruct(a.shape, BF16) for a in (w_in, w_branch, w_out)]
    out_shape.append(jax.ShapeDtypeStruct((N_CHIPS,) + conv_w.shape, F32))
    return pl.pallas_call(
        body, name="gather_weights",
        in_specs=[ANY] * 4, out_specs=[ANY] * 4, out_shape=out_shape,
        input_output_aliases={0: 0, 1: 1, 2: 2},
        scratch_shapes=[pltpu.SemaphoreType.DMA((n_sems,)), pltpu.SemaphoreType.DMA((n_sems,)),
                        pltpu.SemaphoreType.DMA((1,))],
    )(w_in, w_branch, w_out, conv_w)


def _swap_halves(items, name):
    n = len(items)
    halves = []
    for a, axis in items:
        shp = list(a.shape)
        shp[axis] //= 2
        halves.append(tuple(shp))

    def body(*refs):
        srcs, dsts, (send_sems, recv_sems) = refs[:n], refs[n:2 * n], refs[2 * n:]
        x, y, c = _mesh_position()
        copies = []
        for k in range(n):
            axis = items[k][1]
            copies.append(_remote(_half(srcs[k], axis, 1 - c, halves[k][axis]), dsts[k], send_sems, recv_sems, k,
                                  (x, y, 1 - c)))
        for cp in copies:
            cp.start()
        for cp in copies:
            cp.wait()

    return pl.pallas_call(
        body, name=name, in_specs=[ANY] * n, out_specs=[ANY] * n,
        out_shape=[jax.ShapeDtypeStruct(h, a.dtype) for h, (a, _) in zip(halves, items)],
        scratch_shapes=[pltpu.SemaphoreType.DMA((n,)), pltpu.SemaphoreType.DMA((n,))],
    )(*[a for a, _ in items])


def _exchange_grads(items, small):
    n = len(items)
    slices = []
    for a, axis in items:
        shp = list(a.shape)
        shp[axis] //= N_CHIPS
        slices.append(tuple(shp))

    def body(*refs):
        srcs, small_ref = refs[:n], refs[n]
        dsts, small_all = refs[n + 1:2 * n + 1], refs[2 * n + 1]
        send_sems, recv_sems, local_sems = refs[2 * n + 2:]
        x, y, c = _mesh_position()
        chips = _other_chips(x, y)
        mine = 2 * x + y
        me = 4 * x + 2 * y + c

        def piece(k, chip):
            axis = items[k][1]
            return _half(srcs[k], axis, chip, slices[k][axis])

        local = [pltpu.make_async_copy(small_ref, small_all.at[me], local_sems.at[0])]
        for cp in local:
            cp.start()
        copies = []
        for k in range(n):
            for r, (px, py) in enumerate(chips):
                copies.append(_remote(piece(k, 2 * px + py), dsts[k].at[r], send_sems, recv_sems, 3 * k + r, (px, py, c)))
        for r, (fx, fy, fc) in enumerate(ALL_FLIPS):
            copies.append(_remote(small_ref, small_all.at[me], send_sems, recv_sems, 3 * n + r,
                                  (x ^ fx, y ^ fy, c ^ fc)))
        for cp in copies:
            cp.start()
        for cp in copies:
            cp.wait()
        for cp in local:
            cp.wait()

    n_sems = 3 * n + len(ALL_FLIPS)
    out_shape = [jax.ShapeDtypeStruct((N_CHIPS - 1,) + s, BF16) for s in slices]
    out_shape.append(jax.ShapeDtypeStruct((2 * N_CHIPS,) + small.shape, F32))
    outs = pl.pallas_call(
        body, name="exchange_grads", in_specs=[ANY] * (n + 1), out_specs=[ANY] * (n + 1), out_shape=out_shape,
        scratch_shapes=[pltpu.SemaphoreType.DMA((n_sems,)), pltpu.SemaphoreType.DMA((n_sems,)),
                        pltpu.SemaphoreType.DMA((1,))],
    )(*[a for a, _ in items], small)
    return outs[:n], outs[n]


def _sum_chips(recv, own, shard_axis, split_axis, pos, dest, layer, name, tr=128):
    sl = recv.shape[1:]
    nd = len(sl)
    tiled = nd == 2 and sl[0] > tr
    nt = sl[0] // tr if tiled else 1
    block = ((tr,) + sl[1:]) if tiled else sl
    shard = list(sl)
    shard[split_axis] *= 2

    def recv_index(i, pos_ref):
        return (0, i) + (0,) * (nd - 1) if tiled else (0,) * (nd + 1)

    def own_index(i, pos_ref):
        idx = [0] * nd
        idx[shard_axis] = pos_ref[1]
        if tiled:
            idx[0] = pos_ref[1] * nt + i if shard_axis == 0 else i
        return tuple(idx)

    def out_index(i, pos_ref):
        idx = [0] * nd
        idx[split_axis] = pos_ref[0]
        if tiled:
            idx[0] = pos_ref[0] * nt + i if split_axis == 0 else i
        return (layer, *idx)

    def body(pos_ref, recv_ref, own_ref, *rest):
        o_ref = rest[-1]
        tot = own_ref[...].astype(F32)
        for k in range(N_CHIPS - 1):
            tot = tot + recv_ref[k].astype(F32)
        o_ref[0] = tot

    in_specs = [pl.BlockSpec((N_CHIPS - 1,) + block, recv_index), pl.BlockSpec(block, own_index)]
    args = [pos, recv, own]
    aliases = {}
    if dest is not None:
        in_specs.append(ANY)
        args.append(dest)
        aliases = {3: 0}
    return pl.pallas_call(
        body, name=name,
        grid_spec=pltpu.PrefetchScalarGridSpec(
            num_scalar_prefetch=1, grid=(nt,), in_specs=in_specs,
            out_specs=pl.BlockSpec((1,) + block, out_index)),
        out_shape=jax.ShapeDtypeStruct((DEPTH,) + tuple(shard), F32),
        input_output_aliases=aliases,
    )(*args)


def _share_halves(bufs, name):
    n = len(bufs)

    def body(*refs):
        outs, (send_sems, recv_sems) = refs[n:2 * n], refs[2 * n:]
        x, y, c = _mesh_position()
        copies = []
        for k, (a, axis) in enumerate(bufs):
            size = a.shape[1 + axis] // 2
            mine = _half(outs[k], 1 + axis, c, size)
            copies.append(_remote(mine, mine, send_sems, recv_sems, k, (x, y, 1 - c)))
        for cp in copies:
            cp.start()
        for cp in copies:
            cp.wait()

    return pl.pallas_call(
        body, name=name, in_specs=[ANY] * n, out_specs=[ANY] * n,
        out_shape=[jax.ShapeDtypeStruct(a.shape, F32) for a, _ in bufs],
        input_output_aliases={k: k for k in range(n)},
        scratch_shapes=[pltpu.SemaphoreType.DMA((n,)), pltpu.SemaphoreType.DMA((n,))],
    )(*[a for a, _ in bufs])


def _layer_fwd(x, p, l):
    tag = f"l{l}_"
    h = _rms_fwd(x, p["pre_g"], tag + "pre_norm")
    u = _matmul(h, p["w_in"], "nn", F32, tag + "in_proj", b_lead=(l,))
    y_pool = _pool_fwd(u, p["pool_w"], p["pool_scale"], tag + "pool")
    y_conv = _conv_fwd(u, p["conv_w"], p["conv_b"], tag + "conv")
    o_sb, y_sb, sb_after = _sb_fwd(u, tag + "stickbreak")
    ys = [y_pool, y_conv, y_sb]
    projs = [_matmul(ys[n], p["w_branch"], "nn", BF16, tag + f"branch_proj{n}", b_lead=(l, n)) for n in range(3)]
    merged = _gate_fwd(u, projs, tag + "merge")
    out = _matmul(merged, p["w_out"], "nn", F32, tag + "out_proj", b_lead=(l,))
    return out, dict(x=x, h=h, u=u, ys=ys, o_sb=o_sb, sb_after=sb_after, projs=projs, merged=merged, out=out)


def _layer_bwd(dy, p, saved, l):
    tag = f"l{l}_bwd_"
    u = saved["u"]
    d_out, g_post = _rms_bwd(saved["out"], p["post_g"], dy, None, BF16, tag + "post_norm")
    d_merged = _matmul(d_out, p["w_out"], "nt", BF16, tag + "out_proj_dx", b_lead=(l,))
    g_w_out = _matmul(saved["merged"], d_out, "tn", F32, tag + "out_proj_dw", tk=2048)
    d_projs, d_logits = _gate_bwd(u, saved["projs"], d_merged, tag + "merge")
    g_w_branch = [_matmul(saved["ys"][n], d_projs[n], "tn", F32, tag + f"branch_dw{n}", tk=2048) for n in range(3)]
    d_ys = [_matmul(d_projs[n], p["w_branch"], "nt", F32, tag + f"branch_dx{n}", b_lead=(l, n)) for n in range(3)]
    d_pv, d_pg, g_pool_w, g_pool_scale = _pool_bwd(u, d_ys[0], p["pool_w"], p["pool_scale"], tag + "pool")
    d_cx, d_cgb, d_cgc, d_cg, g_conv_w, g_conv_b = _conv_bwd(u, d_ys[1], p["conv_w"], p["conv_b"], tag + "conv")
    d_q, d_k, d_v, d_sg = _sb_bwd(u, saved["o_sb"], saved["sb_after"], d_ys[2], tag + "stickbreak")
    du = jnp.concatenate([d_pv, d_pg, d_cx, d_cgb, d_cgc, d_cg, d_q, d_k, d_v, d_sg] + list(d_logits), axis=1)
    g_w_in = _matmul(saved["h"], du, "tn", F32, tag + "in_proj_dw", tk=2048)
    dh = _matmul(du, p["w_in"], "nt", BF16, tag + "in_proj_dx", tk=2048, b_lead=(l,))
    dx, g_pre = _rms_bwd(saved["x"], p["pre_g"], dh, dy, F32, tag + "pre_norm")
    grads = dict(w_in=g_w_in, w_branch=jnp.stack(g_w_branch), w_out=g_w_out, pre_g=g_pre, post_g=g_post,
                 pool_w=g_pool_w, pool_scale=g_pool_scale, conv_w=g_conv_w, conv_b=g_conv_b)
    return dx, grads


SMALL_ORDER = ["pre_g", "pool_w", "pool_scale", "conv_w", "conv_b", "post_g"]


def _pack_small(per_layer):
    parts, spans, at = [], {}, 0
    for name in SMALL_ORDER:
        a = jnp.stack([per_layer[l][name] for l in range(DEPTH)]).reshape(-1, LANE)
        parts.append(a)
        spans[name] = (at, a.shape[0])
        at += a.shape[0]
    return jnp.concatenate(parts, axis=0), spans


def kernel(x, pre_norm_g, w_in, pool_w, pool_scale, conv_w, conv_b, w_branch, w_out, post_norm_g, loss_target, m_pre_norm_g, m_w_in, m_pool_w, m_pool_scale, m_conv_w, m_conv_b, m_w_branch, m_w_out, m_post_norm_g, v_pre_norm_g, v_w_in, v_pool_w, v_pool_scale, v_conv_w, v_conv_b, v_w_branch, v_w_out, v_post_norm_g):
    mx, my, mc = _mesh_position()
    chip = 2 * mx + my
    core = mc.astype(jnp.int32).reshape(1)
    pos = jnp.stack([mc, chip]).astype(jnp.int32)

    w_in_f, w_branch_f, w_out_f, conv_w_by_chip = _gather_weights(
        _cast_shard(w_in, 2, pos, "cast_w_in"), _cast_shard(w_branch, 3, pos, "cast_w_branch"),
        _cast_shard(w_out, 1, pos, "cast_w_out"), conv_w)
    conv_w_f = conv_w_by_chip.transpose(1, 2, 0, 3).reshape(DEPTH, 3, WIDTH)
    pool_w_b = _cast_bf16(pool_w, "cast_pool_w")
    params = []
    for l in range(DEPTH):
        params.append(dict(
            pre_g=pre_norm_g[l:l + 1], post_g=post_norm_g[l:l + 1], w_in=w_in_f, w_branch=w_branch_f,
            w_out=w_out_f, pool_w=pool_w_b[l], pool_scale=pool_scale[l:l + 1], conv_w=conv_w_f[l],
            conv_b=conv_b[l:l + 1]))

    act = x[0]
    saved = []
    for l in range(DEPTH):
        out, sv = _layer_fwd(act, params[l], l)
        saved.append(sv)
        if l < DEPTH - 1:
            act = _resid_out(act, out, params[l]["post_g"], None, f"l{l}_resid")
    dy, loss_part = _resid_out(act, saved[-1]["out"], params[-1]["post_g"], loss_target[0], "loss_head")
    loss = lax.psum(loss_part[0, 0], ("x", "y", "c"))

    grads = [None] * DEPTH
    for l in reversed(range(DEPTH)):
        dy, grads[l] = _layer_bwd(dy, params[l], saved[l], l)
    grad_x = dy[None]

    split_axis = dict(w_in=0, w_branch=1, w_out=1)
    shard_axis = dict(w_in=1, w_branch=2, w_out=0)
    names = ["w_in", "w_branch", "w_out"]
    items = [(grads[l][n], split_axis[n]) for l in range(DEPTH) for n in names]
    from_sibling = _swap_halves(items, "swap_grad_halves")
    chip_sums = [(_chip_sum(a, r, axis, core, f"chip_sum{k}"), shard_axis[names[k % 3]])
                 for k, ((a, axis), r) in enumerate(zip(items, from_sibling))]
    small_part, spans = _pack_small(grads)
    by_chip, small_all = _exchange_grads(chip_sums, small_part)
    bufs = []
    for i, n in enumerate(names):
        dest = None
        for l in range(DEPTH):
            k = l * 3 + i
            dest = _sum_chips(by_chip[k], chip_sums[k][0], shard_axis[n], split_axis[n], pos, dest, l, f"sum_chips{k}")
        bufs.append((dest, split_axis[n]))
    g_w_in, g_w_branch, g_w_out = _share_halves(bufs, "share_grad_halves")

    small_sum = _sum_slots(small_all, F32, "sum_small")
    small = {}
    for name, like in (("pre_g", pre_norm_g), ("pool_w", pool_w), ("pool_scale", pool_scale), ("conv_b", conv_b),
                       ("post_g", post_norm_g)):
        at, n = spans[name]
        small[name] = small_sum[at:at + n].reshape(like.shape)
    at, n = spans["conv_w"]
    g_conv_w_full = small_sum[at:at + n].reshape(DEPTH, 3, WIDTH)
    g_conv_w = lax.dynamic_slice_in_dim(g_conv_w_full, chip * conv_w.shape[2], conv_w.shape[2], axis=2)

    g = dict(pre_norm_g=small["pre_g"], w_in=g_w_in, pool_w=small["pool_w"], pool_scale=small["pool_scale"],
             conv_w=g_conv_w, conv_b=small["conv_b"], w_branch=g_w_branch, w_out=g_w_out, post_norm_g=small["post_g"])
    w = dict(pre_norm_g=pre_norm_g, w_in=w_in, pool_w=pool_w, pool_scale=pool_scale, conv_w=conv_w, conv_b=conv_b,
             w_branch=w_branch, w_out=w_out, post_norm_g=post_norm_g)
    m = dict(pre_norm_g=m_pre_norm_g, w_in=m_w_in, pool_w=m_pool_w, pool_scale=m_pool_scale, conv_w=m_conv_w,
             conv_b=m_conv_b, w_branch=m_w_branch, w_out=m_w_out, post_norm_g=m_post_norm_g)
    v = dict(pre_norm_g=v_pre_norm_g, w_in=v_w_in, pool_w=v_pool_w, pool_scale=v_pool_scale, conv_w=v_conv_w,
             conv_b=v_conv_b, w_branch=v_w_branch, w_out=v_w_out, post_norm_g=v_post_norm_g)
    order = ["pre_norm_g", "w_in", "pool_w", "pool_scale", "conv_w", "conv_b", "w_branch", "w_out", "post_norm_g"]
    upd = {n: _adamw(w[n], g[n], m[n], v[n], "adamw_" + n) for n in order}
    return (loss, grad_x, *[g[n] for n in order], *[upd[n][0] for n in order], *[upd[n][1] for n in order],
            *[upd[n][2] for n in order])
```

```python
import functools

import jax
import jax.numpy as jnp
from jax import lax
from jax.experimental import pallas as pl
from jax.experimental.pallas import tpu as pltpu

F32 = jnp.float32
BF16 = jnp.bfloat16
MESH = pl.DeviceIdType.MESH
ANY = pl.BlockSpec(memory_space=pl.ANY)

DEPTH = 2
D_MODEL = 1024
WIDTH = 512
N_IN = 8192
N_CHIPS = 4
HEAD_DIM = 64
RMS_EPS = 1e-6
POOL_HALO = 16
CONV_HALO = 8
LANE = 128
COL_POOL_V, COL_POOL_G = 0, 4
COL_CONV_X, COL_CONV_GB, COL_CONV_GC, COL_CONV_G = 8, 12, 16, 20
COL_SB_Q, COL_SB_K, COL_SB_V, COL_SB_G = 24, 28, 32, 36
COL_MERGE_1024 = 5

ADAM_LR, ADAM_B1, ADAM_B2, ADAM_EPS, ADAM_WD, ADAM_STEP = 0.001, 0.9, 0.999, 1e-08, 0.01, 10

NN = (((1,), (0,)), ((), ()))
NT = (((1,), (1,)), ((), ()))
TN = (((0,), (0,)), ((), ()))


def _sigmoid(x):
    return 1.0 / (1.0 + jnp.exp(-x))


def _silu_and_grad(x):
    s = _sigmoid(x)
    return x * s, s * (1.0 + x * (1.0 - s))


def _dot(a, b, dims):
    return lax.dot_general(a, b, dims, preferred_element_type=F32)


def _matmul(a, b, mode, out_dtype, name, tm=1024, tn=1024, tk=1024, b_lead=()):
    b_shape = b.shape[len(b_lead):]
    if mode == "nn":
        (m, k), (k2, n) = a.shape, b_shape
    elif mode == "nt":
        (m, k), (n, k2) = a.shape, b_shape
    else:
        (k, m), (k2, n) = a.shape, b_shape
    assert k == k2 and a.dtype == BF16 and b.dtype == BF16
    tm, tn, tk = min(tm, m), min(tn, n), min(tk, k)
    assert m % tm == 0 and n % tn == 0 and k % tk == 0
    nk = k // tk
    dims = {"nn": NN, "nt": NT, "tn": TN}[mode]

    def body(a_ref, b_ref, o_ref, *scratch):
        p = _dot(a_ref[...], b_ref[...], dims)
        if nk == 1:
            o_ref[...] = p.astype(o_ref.dtype)
        else:
            acc = scratch[0]
            kk = pl.program_id(2)

            @pl.when(kk == 0)
            def _():
                acc[...] = p

            @pl.when(jnp.logical_and(kk > 0, kk < nk - 1))
            def _():
                acc[...] += p

            @pl.when(kk == nk - 1)
            def _():
                o_ref[...] = (acc[...] + p).astype(o_ref.dtype)

    if mode == "tn":
        a_spec = pl.BlockSpec((tk, tm), lambda i, j, kk: (kk, i))
    else:
        a_spec = pl.BlockSpec((tm, tk), lambda i, j, kk: (i, kk))
    squeezed = (None,) * len(b_lead)
    if mode == "nt":
        b_spec = pl.BlockSpec(squeezed + (tn, tk), lambda i, j, kk: (*b_lead, j, kk))
    else:
        b_spec = pl.BlockSpec(squeezed + (tk, tn), lambda i, j, kk: (*b_lead, kk, j))
    return pl.pallas_call(
        body, name=name,
        grid=(m // tm, n // tn, nk),
        in_specs=[a_spec, b_spec],
        out_specs=pl.BlockSpec((tm, tn), lambda i, j, kk: (i, j)),
        out_shape=jax.ShapeDtypeStruct((m, n), out_dtype),
        scratch_shapes=[pltpu.VMEM((tm, tn), F32)] if nk > 1 else [],
        compiler_params=pltpu.CompilerParams(dimension_semantics=("parallel", "parallel", "arbitrary")),
    )(a, b)


def _rms_fwd(x, g, name, ts=512):
    s, d = x.shape

    def body(x_ref, g_ref, h_ref):
        xv = x_ref[...]
        r = lax.rsqrt(jnp.mean(xv * xv, axis=-1, keepdims=True) + RMS_EPS)
        h_ref[...] = (xv * r * g_ref[...]).astype(BF16)

    return pl.pallas_call(
        body, name=name, grid=(s // ts,),
        in_specs=[pl.BlockSpec((ts, d), lambda i: (i, 0)), pl.BlockSpec((1, d), lambda i: (0, 0))],
        out_specs=pl.BlockSpec((ts, d), lambda i: (i, 0)),
        out_shape=jax.ShapeDtypeStruct((s, d), BF16),
    )(x, g)


def _rms_bwd(xin, g, dh, resid, out_dtype, name, ts=512):
    s, d = xin.shape
    has_resid = resid is not None

    def body(*refs):
        if has_resid:
            x_ref, g_ref, dh_ref, res_ref, dx_ref, dg_ref = refs
        else:
            x_ref, g_ref, dh_ref, dx_ref, dg_ref = refs
        xv = x_ref[...]
        dhv = dh_ref[...].astype(F32)
        r = lax.rsqrt(jnp.mean(xv * xv, axis=-1, keepdims=True) + RMS_EPS)
        nrm = xv * r
        dn = dhv * g_ref[...]
        dx = r * (dn - nrm * jnp.mean(dn * nrm, axis=-1, keepdims=True))
        if has_resid:
            dx = dx + res_ref[...]
        dx_ref[...] = dx.astype(dx_ref.dtype)
        part = jnp.sum(dhv * nrm, axis=0, keepdims=True)

        @pl.when(pl.program_id(0) == 0)
        def _():
            dg_ref[...] = part

        @pl.when(pl.program_id(0) > 0)
        def _():
            dg_ref[...] += part

    row = pl.BlockSpec((ts, d), lambda i: (i, 0))
    vec = pl.BlockSpec((1, d), lambda i: (0, 0))
    ins = [xin, g, dh] + ([resid] if has_resid else [])
    return pl.pallas_call(
        body, name=name, grid=(s // ts,),
        in_specs=[row, vec, row] + ([row] if has_resid else []),
        out_specs=[row, vec],
        out_shape=[jax.ShapeDtypeStruct((s, d), out_dtype), jax.ShapeDtypeStruct((1, d), F32)],
        compiler_params=pltpu.CompilerParams(dimension_semantics=("arbitrary",)),
    )(*ins)


def _resid_out(x, out, g, target, name, ts=512):
    s, d = x.shape
    has_loss = target is not None

    def body(*refs):
        if has_loss:
            x_ref, o_ref, g_ref, t_ref, dy_ref, loss_ref = refs
        else:
            x_ref, o_ref, g_ref, y_ref = refs
        ov = o_ref[...]
        r = lax.rsqrt(jnp.mean(ov * ov, axis=-1, keepdims=True) + RMS_EPS)
        yv = x_ref[...] + ov * r * g_ref[...]
        if not has_loss:
            y_ref[...] = yv
            return
        err = yv - t_ref[...]
        dy_ref[...] = err * (1.0 / d)
        part = jnp.sum(jnp.sum(err * err, axis=-1, keepdims=True), axis=0, keepdims=True) * (0.5 / d)
        part = jnp.broadcast_to(part, (1, LANE))

        @pl.when(pl.program_id(0) == 0)
        def _():
            loss_ref[...] = part

        @pl.when(pl.program_id(0) > 0)
        def _():
            loss_ref[...] += part

    row = pl.BlockSpec((ts, d), lambda i: (i, 0))
    vec = pl.BlockSpec((1, d), lambda i: (0, 0))
    if has_loss:
        return pl.pallas_call(
            body, name=name, grid=(s // ts,),
            in_specs=[row, row, vec, row],
            out_specs=[row, pl.BlockSpec((1, LANE), lambda i: (0, 0))],
            out_shape=[jax.ShapeDtypeStruct((s, d), F32), jax.ShapeDtypeStruct((1, LANE), F32)],
            compiler_params=pltpu.CompilerParams(dimension_semantics=("arbitrary",)),
        )(x, out, g, target)
    return pl.pallas_call(
        body, name=name, grid=(s // ts,),
        in_specs=[row, row, vec], out_specs=row,
        out_shape=jax.ShapeDtypeStruct((s, d), F32),
    )(x, out, g)


def _rows_before(ref, start, n, halo):
    if start == 0:
        return jnp.concatenate([jnp.zeros((halo, ref.shape[1]), F32), ref[0:n, :]], axis=0)
    return ref[start - halo:start + n, :]


def _rows_after(ref, start, n, halo):
    if start + n == ref.shape[0]:
        return jnp.concatenate([ref[start:start + n, :].astype(F32), jnp.zeros((halo, ref.shape[1]), F32)], axis=0)
    return ref[start:start + n + halo, :].astype(F32)


def _pick_window(group, s2, s4, s8, s16):
    return jnp.where(group == 0, s2, jnp.where(group == 1, s4, jnp.where(group == 2, s8, s16)))


def _trailing_sums(ext, group):
    s2 = ext + pltpu.roll(ext, 1, 0)
    s4 = s2 + pltpu.roll(s2, 2, 0)
    s8 = s4 + pltpu.roll(s4, 4, 0)
    s16 = s8 + pltpu.roll(s8, 8, 0)
    return _pick_window(group, s2, s4, s8, s16)


def _leading_sums(ext, group):
    n = ext.shape[0]
    s2 = ext + pltpu.roll(ext, n - 1, 0)
    s4 = s2 + pltpu.roll(s2, n - 2, 0)
    s8 = s4 + pltpu.roll(s4, n - 4, 0)
    s16 = s8 + pltpu.roll(s8, n - 8, 0)
    return _pick_window(group, s2, s4, s8, s16)


def _window_count(start, n, group):
    pos = start + lax.broadcasted_iota(jnp.int32, (n, LANE), 0)
    return jnp.minimum(pos + 1, 2 << group).astype(F32)


def _pooled(v_ref, start, n, group):
    ext = _rows_before(v_ref, start, n, POOL_HALO)
    sums = _trailing_sums(ext, group)[POOL_HALO:, :]
    return sums / _window_count(start, n, group) - ext[POOL_HALO:, :]


def _pool_fwd(u, pool_w, pool_scale, name, ts=512):
    s = u.shape[0]

    def body(v_ref, gate_ref, w_ref, sc_ref, y_ref):
        group = pl.program_id(0)
        for c in range(s // ts):
            a = c * ts
            pooled = _pooled(v_ref, a, ts, group)
            mixed = _dot(pooled.astype(BF16), w_ref[...], NN)
            gate = gate_ref[a:a + ts, :]
            y_ref[a:a + ts, :] = (mixed * sc_ref[...] * (gate * _sigmoid(gate))).astype(BF16)

    col = lambda base: pl.BlockSpec((s, LANE), lambda g: (0, base + g))
    return pl.pallas_call(
        body, name=name, grid=(4,),
        in_specs=[col(COL_POOL_V), col(COL_POOL_G),
                  pl.BlockSpec((None, LANE, LANE), lambda g: (g, 0, 0)),
                  pl.BlockSpec((1, LANE), lambda g: (0, g))],
        out_specs=pl.BlockSpec((s, LANE), lambda g: (0, g)),
        out_shape=jax.ShapeDtypeStruct((s, WIDTH), BF16),
    )(u, u, pool_w, pool_scale)


def _pool_bwd(u, dy, pool_w, pool_scale, name, ts=512):
    s = u.shape[0]

    def body(v_ref, gate_ref, dy_ref, w_ref, sc_ref, dv_ref, dgate_ref, dw_ref, dsc_ref):
        group = pl.program_id(0)
        w = w_ref[...]
        scale = sc_ref[...]
        dw = jnp.zeros((LANE, LANE), F32)
        dsc = jnp.zeros((1, LANE), F32)
        for c in range(s // ts):
            a = c * ts
            n_ext = ts + POOL_HALO
            gate_e = _rows_after(gate_ref, a, ts, POOL_HALO)
            dy_e = _rows_after(dy_ref, a, ts, POOL_HALO)
            silu_e, dsilu_e = _silu_and_grad(gate_e)
            dms_e = dy_e * silu_e
            dm_e = (dms_e * scale).astype(BF16)
            dpool_e = _dot(dm_e, w, NT)
            spread = _leading_sums(dpool_e / _window_count(a, n_ext, group), group)
            dv_ref[a:a + ts, :] = (spread[0:ts, :] - dpool_e[0:ts, :]).astype(BF16)
            pooled = _pooled(v_ref, a, ts, group).astype(BF16)
            mixed = _dot(pooled, w, NN)
            dgate_ref[a:a + ts, :] = (dy_e[0:ts, :] * mixed * scale * dsilu_e[0:ts, :]).astype(BF16)
            dsc = dsc + jnp.sum(dms_e[0:ts, :] * mixed, axis=0, keepdims=True)
            dw = dw + _dot(pooled, dm_e[0:ts, :], TN)
        dw_ref[...] = dw
        dsc_ref[...] = dsc

    col = lambda base: pl.BlockSpec((s, LANE), lambda g: (0, base + g))
    out_col = pl.BlockSpec((s, LANE), lambda g: (0, g))
    return pl.pallas_call(
        body, name=name, grid=(4,),
        in_specs=[col(COL_POOL_V), col(COL_POOL_G), out_col,
                  pl.BlockSpec((None, LANE, LANE), lambda g: (g, 0, 0)),
                  pl.BlockSpec((1, LANE), lambda g: (0, g))],
        out_specs=[out_col, out_col,
                   pl.BlockSpec((None, LANE, LANE), lambda g: (g, 0, 0)),
                   pl.BlockSpec((1, LANE), lambda g: (0, g))],
        out_shape=[jax.ShapeDtypeStruct((s, WIDTH), BF16), jax.ShapeDtypeStruct((s, WIDTH), BF16),
                   jax.ShapeDtypeStruct((4, LANE, LANE), F32), jax.ShapeDtypeStruct((1, WIDTH), F32)],
    )(u, u, dy, pool_w, pool_scale)


def _conv_taps(x_ref, gc_ref, start, n):
    z_ext = _rows_before(gc_ref, start, n, CONV_HALO) * _rows_before(x_ref, start, n, CONV_HALO)
    z0 = z_ext[CONV_HALO:, :]
    z1 = pltpu.roll(z_ext, 1, 0)[CONV_HALO:, :]
    z2 = pltpu.roll(z_ext, 2, 0)[CONV_HALO:, :]
    return z0, z1, z2


def _conv_fwd(u, conv_w, conv_b, name, ts=512):
    s = u.shape[0]

    def body(x_ref, gb_ref, gc_ref, g_ref, w_ref, b_ref, y_ref):
        w0, w1, w2 = w_ref[0:1, :], w_ref[1:2, :], w_ref[2:3, :]
        for c in range(s // ts):
            a = c * ts
            z0, z1, z2 = _conv_taps(x_ref, gc_ref, a, ts)
            y = w2 * z0 + w1 * z1 + w0 * z2 + b_ref[...]
            gate = g_ref[a:a + ts, :]
            y_ref[a:a + ts, :] = (gb_ref[a:a + ts, :] * y * (gate * _sigmoid(gate))).astype(BF16)

    col = lambda base: pl.BlockSpec((s, LANE), lambda j: (0, base + j))
    return pl.pallas_call(
        body, name=name, grid=(4,),
        in_specs=[col(COL_CONV_X), col(COL_CONV_GB), col(COL_CONV_GC), col(COL_CONV_G),
                  pl.BlockSpec((3, LANE), lambda j: (0, j)), pl.BlockSpec((1, LANE), lambda j: (0, j))],
        out_specs=pl.BlockSpec((s, LANE), lambda j: (0, j)),
        out_shape=jax.ShapeDtypeStruct((s, WIDTH), BF16),
    )(u, u, u, u, conv_w, conv_b)


def _conv_bwd(u, dy, conv_w, conv_b, name, ts=512):
    s = u.shape[0]

    def body(x_ref, gb_ref, gc_ref, g_ref, dy_ref, w_ref, b_ref,
             dx_ref, dgb_ref, dgc_ref, dg_ref, dw_ref, db_ref):
        w0, w1, w2 = w_ref[0:1, :], w_ref[1:2, :], w_ref[2:3, :]
        acc = [jnp.zeros((1, LANE), F32) for _ in range(4)]
        for c in range(s // ts):
            a = c * ts
            n_ext = ts + CONV_HALO
            gate_e = _rows_after(g_ref, a, ts, CONV_HALO)
            silu_e, dsilu_e = _silu_and_grad(gate_e)
            dy_e = _rows_after(dy_ref, a, ts, CONV_HALO)
            gb_e = _rows_after(gb_ref, a, ts, CONV_HALO)
            dyy_e = dy_e * silu_e * gb_e
            dz = (w2 * dyy_e + w1 * pltpu.roll(dyy_e, n_ext - 1, 0) + w0 * pltpu.roll(dyy_e, n_ext - 2, 0))[0:ts, :]
            z0, z1, z2 = _conv_taps(x_ref, gc_ref, a, ts)
            yb = w2 * z0 + w1 * z1 + w0 * z2 + b_ref[...]
            dyv = dy_e[0:ts, :]
            dyy = dyy_e[0:ts, :]
            dg_ref[a:a + ts, :] = (dyv * gb_e[0:ts, :] * yb * dsilu_e[0:ts, :]).astype(BF16)
            dgb_ref[a:a + ts, :] = (dyv * silu_e[0:ts, :] * yb).astype(BF16)
            dx_ref[a:a + ts, :] = (dz * gc_ref[a:a + ts, :]).astype(BF16)
            dgc_ref[a:a + ts, :] = (dz * x_ref[a:a + ts, :]).astype(BF16)
            for i, term in enumerate((dyy * z2, dyy * z1, dyy * z0, dyy)):
                acc[i] = acc[i] + jnp.sum(term, axis=0, keepdims=True)
        dw_ref[0:1, :] = acc[0]
        dw_ref[1:2, :] = acc[1]
        dw_ref[2:3, :] = acc[2]
        db_ref[...] = acc[3]

    col = lambda base: pl.BlockSpec((s, LANE), lambda j: (0, base + j))
    out_col = pl.BlockSpec((s, LANE), lambda j: (0, j))
    big = jax.ShapeDtypeStruct((s, WIDTH), BF16)
    return pl.pallas_call(
        body, name=name, grid=(4,),
        in_specs=[col(COL_CONV_X), col(COL_CONV_GB), col(COL_CONV_GC), col(COL_CONV_G), out_col,
                  pl.BlockSpec((3, LANE), lambda j: (0, j)), pl.BlockSpec((1, LANE), lambda j: (0, j))],
        out_specs=[out_col, out_col, out_col, out_col,
                   pl.BlockSpec((3, LANE), lambda j: (0, j)), pl.BlockSpec((1, LANE), lambda j: (0, j))],
        out_shape=[big, big, big, big,
                   jax.ShapeDtypeStruct((3, WIDTH), F32), jax.ShapeDtypeStruct((1, WIDTH), F32)],
    )(u, u, u, u, dy, conv_w, conv_b)


LOG2_E = 1.4426950408889634
LN_2 = 0.6931471805599453


def _sb_scores(q_h, k_blk, valid, later_mat, carry):
    z = _dot(q_h, k_blk, NT)
    neg_z = -z
    soft = jnp.log(1.0 + jnp.exp2(jnp.minimum(z, neg_z))) * LOG2_E
    log_keep = jnp.minimum(neg_z, 0.0) - soft
    log_beta = log_keep + z
    if valid is not None:
        log_keep = jnp.where(valid, log_keep, 0.0)
    later = _dot(log_keep.astype(BF16), later_mat, NN) + carry
    return log_keep, log_beta, later


def _masked(valid, x):
    return x if valid is None else jnp.where(valid, x, 0.0)


def _diagonal_masks(tq, tk):
    r = lax.broadcasted_iota(jnp.int32, (tq, tk), 0)
    cidx = lax.broadcasted_iota(jnp.int32, (tq, tk), 1)
    return [cidx + d * tk < r for d in range(tq // tk)]


def _triangle(tk, op):
    r = lax.broadcasted_iota(jnp.int32, (tk, tk), 0)
    cidx = lax.broadcasted_iota(jnp.int32, (tk, tk), 1)
    return op(r, cidx).astype(BF16)


def _split_refs(refs, n_in, n_out, n_scratch, rider):
    r_in = len(rider.inputs) if rider else 0
    r_out = len(rider.out_shape) if rider else 0
    a, b = n_in + r_in, n_in + r_in + n_out + r_out
    own = refs[:n_in] + refs[a:a + n_out] + refs[b:b + n_scratch]
    return own, (refs[n_in:a], refs[a + n_out:b], refs[b + n_scratch:])


def _rider_call_args(rider, n_in, n_out):
    if rider is None:
        return dict(in_specs=[], out_specs=[], out_shape=[], aliases={}, scratch=[], inputs=[])
    return dict(in_specs=[ANY] * len(rider.inputs), out_specs=[ANY] * len(rider.out_shape),
                out_shape=list(rider.out_shape), scratch=list(rider.scratch), inputs=list(rider.inputs),
                aliases={n_in + a: n_out + b for a, b in rider.aliases.items()})


def _ride(rider, phase, when, parts):
    fn = getattr(rider, phase) if rider else None
    if fn is not None:
        pl.when(when)(lambda: fn(*parts))


def _sb_fwd(u, name, t=512, tk=256, pairs=4, rider=None):
    s = u.shape[0]
    assert s // tk <= LANE and 4 % pairs == 0 and t % tk == 0
    scale = HEAD_DIM ** -0.5
    nh = 2 * pairs
    wide = pairs * LANE
    ratio = t // tk
    groups, nq = 4 // pairs, s // t

    def body(*refs):
        own, riding = _split_refs(refs, 4, 3, 4, rider)
        q_ref, k_ref, v_ref, g_ref, o_ref, y_ref, after_ref, kb_ref, vb_ref, acc_ref, carry_ref = own
        grp = pl.program_id(0)
        i = pl.program_id(1)
        _ride(rider, "start", jnp.logical_and(grp == 0, i == 0), riding)
        _ride(rider, "middle", jnp.logical_and(grp == groups - 1, i == (3 * nq) // 4), riding)

        @pl.when(i == 0)
        def _():
            kb_ref[...] = k_ref[...].astype(BF16)
            vb_ref[...] = v_ref[...].astype(BF16)

        lane = lax.broadcasted_iota(jnp.int32, (t, LANE), 1)
        first = lane < HEAD_DIM
        after_ref[...] = jnp.zeros_like(after_ref)
        qv = q_ref[...] * (scale * LOG2_E)
        q_heads = []
        for p in range(pairs):
            qp = qv[:, p * LANE:(p + 1) * LANE]
            q_heads += [jnp.where(first, qp, 0.0).astype(BF16), jnp.where(first, 0.0, qp).astype(BF16)]
        later_mat = _triangle(tk, lambda r, cidx: r > cidx)
        acc_ref[...] = jnp.zeros_like(acc_ref)
        carry_ref[...] = jnp.zeros_like(carry_ref)

        def block(kb, valid):
            rows = pl.ds(pl.multiple_of(kb * tk, tk), tk)
            k_blk = kb_ref[rows, :]
            v_blk = vb_ref[rows, :]
            carries = [carry_ref[h] for h in range(nh)]
            afters = [after_ref[:, h * LANE:(h + 1) * LANE] for h in range(nh)]
            accs = [acc_ref[h] for h in range(nh)]
            outs = []
            for h in range(nh):
                cols = slice((h // 2) * LANE, (h // 2 + 1) * LANE)
                log_keep, log_beta, later = _sb_scores(q_heads[h], k_blk[:, cols], valid, later_mat, carries[h])
                a = _masked(valid, jnp.exp2(log_beta + later))
                outs.append((accs[h] + _dot(a.astype(BF16), v_blk[:, cols], NN),
                             carries[h] + jnp.sum(log_keep, axis=1, keepdims=True),
                             jnp.where(lane == kb, carries[h], afters[h])))
            for h in range(nh):
                acc_ref[h] = outs[h][0]
                carry_ref[h] = outs[h][1]
                after_ref[:, h * LANE:(h + 1) * LANE] = outs[h][2]

        def step(j, _):
            block(ratio * i - 1 - j, None)
            return 0

        masks = _diagonal_masks(t, tk)
        for d in reversed(range(ratio)):
            block(ratio * i + d, masks[d])
        lax.fori_loop(0, ratio * i, step, 0)
        for p in range(pairs):
            cols = slice(p * LANE, (p + 1) * LANE)
            o = jnp.where(first, acc_ref[2 * p], acc_ref[2 * p + 1])
            o_ref[:, cols] = o
            gate = g_ref[:, cols]
            y_ref[:, cols] = (o * gate * _sigmoid(gate)).astype(BF16)
        _ride(rider, "finish", jnp.logical_and(grp == groups - 1, i == nq - 1), riding)

    blk = lambda base: pl.BlockSpec((t, wide), lambda g, i: (i, base // pairs + g))
    full = lambda base: pl.BlockSpec((s, wide), lambda g, i: (0, base // pairs + g))
    out_blk = pl.BlockSpec((t, wide), lambda g, i: (i, g))
    extra = _rider_call_args(rider, 4, 3)
    outs = pl.pallas_call(
        body, name=name, grid=(groups, nq),
        in_specs=[blk(COL_SB_Q), full(COL_SB_K), full(COL_SB_V), blk(COL_SB_G)] + extra["in_specs"],
        out_specs=[out_blk, out_blk, pl.BlockSpec((t, nh * LANE), lambda g, i: (i, g))] + extra["out_specs"],
        out_shape=[jax.ShapeDtypeStruct((s, WIDTH), F32), jax.ShapeDtypeStruct((s, WIDTH), BF16),
                   jax.ShapeDtypeStruct((s, 8 * LANE), F32)] + extra["out_shape"],
        input_output_aliases=extra["aliases"],
        scratch_shapes=[pltpu.VMEM((s, wide), BF16), pltpu.VMEM((s, wide), BF16),
                        pltpu.VMEM((nh, t, LANE), F32), pltpu.VMEM((nh, t, 1), F32)] + extra["scratch"],
        compiler_params=pltpu.CompilerParams(dimension_semantics=("arbitrary", "arbitrary")),
    )(u, u, u, u, *extra["inputs"])
    return outs[:3], outs[3:]


def _sb_bwd(u, o, after, dy, name, t=512, tk=256, pairs=2, rider=None):
    s = u.shape[0]
    nq = s // t
    scale = HEAD_DIM ** -0.5
    nh = 2 * pairs
    wide = pairs * LANE
    ratio = t // tk
    groups = 4 // pairs

    def body(*refs):
        own, riding = _split_refs(refs, 7, 4, 6, rider)
        (q_ref, k_ref, v_ref, g_ref, o_ref, after_ref, dy_ref, dq_ref, dk_ref, dv_ref, dg_ref,
         kb_ref, vb_ref, dk_acc, dv_acc, dq_acc, carry_ref) = own
        grp = pl.program_id(0)
        i = pl.program_id(1)
        _ride(rider, "start", jnp.logical_and(grp == 0, i == 0), riding)

        @pl.when(i == 0)
        def _():
            kb_ref[...] = k_ref[...].astype(BF16)
            vb_ref[...] = v_ref[...].astype(BF16)
            dk_acc[...] = jnp.zeros_like(dk_acc)
            dv_acc[...] = jnp.zeros_like(dv_acc)

        lane = lax.broadcasted_iota(jnp.int32, (t, LANE), 1)
        first = lane < HEAD_DIM
        gate = g_ref[...]
        silu, dsilu = _silu_and_grad(gate)
        dyv = dy_ref[...]
        do = dyv * silu
        dg_ref[...] = (dyv * o_ref[...] * dsilu).astype(BF16)
        qv = q_ref[...] * (scale * LOG2_E)
        do_heads, q_heads = [], []
        for p in range(pairs):
            cols = slice(p * LANE, (p + 1) * LANE)
            do_heads += [jnp.where(first, do[:, cols], 0.0).astype(BF16), jnp.where(first, 0.0, do[:, cols]).astype(BF16)]
            q_heads += [jnp.where(first, qv[:, cols], 0.0).astype(BF16), jnp.where(first, 0.0, qv[:, cols]).astype(BF16)]
        later_mat = _triangle(tk, lambda r, cidx: r > cidx)
        before_mat = _triangle(tk, lambda r, cidx: r < cidx)
        dq_acc[...] = jnp.zeros_like(dq_acc)
        carry_ref[...] = jnp.zeros_like(carry_ref)

        def block(kb, valid):
            rows = pl.ds(pl.multiple_of(kb * tk, tk), tk)
            k_blk = kb_ref[rows, :]
            v_blk = vb_ref[rows, :]
            carries = [carry_ref[h] for h in range(nh)]
            dq_old = [dq_acc[h] for h in range(nh)]
            dk_old = dk_acc[rows, :]
            dv_old = dv_acc[rows, :]
            outs = []
            for h in range(nh):
                cols = slice((h // 2) * LANE, (h // 2 + 1) * LANE)
                after = jnp.sum(jnp.where(lane == kb, after_ref[:, h * LANE:(h + 1) * LANE], 0.0), axis=1, keepdims=True)
                _, log_beta, later = _sb_scores(q_heads[h], k_blk[:, cols], valid, later_mat, after)
                beta = jnp.exp2(log_beta)
                a = _masked(valid, jnp.exp2(log_beta + later))
                da = _dot(do_heads[h], v_blk[:, cols], NT)
                gterm = a * da
                before = _dot(gterm.astype(BF16), before_mat, NN) + carries[h]
                dz_b = _masked(valid, gterm * (1.0 - beta) - beta * before).astype(BF16)
                outs.append((dq_old[h] + _dot(dz_b, k_blk[:, cols], NN), _dot(dz_b, q_heads[h], TN),
                             _dot(a.astype(BF16), do_heads[h], TN),
                             carries[h] + jnp.sum(gterm, axis=1, keepdims=True)))
            for h in range(nh):
                dq_acc[h] = outs[h][0]
                carry_ref[h] = outs[h][3]
            dk_new = [outs[2 * p][1] + outs[2 * p + 1][1] for p in range(pairs)]
            dv_new = [outs[2 * p][2] + outs[2 * p + 1][2] for p in range(pairs)]
            dk_acc[rows, :] = dk_old + (dk_new[0] if pairs == 1 else jnp.concatenate(dk_new, axis=1))
            dv_acc[rows, :] = dv_old + (dv_new[0] if pairs == 1 else jnp.concatenate(dv_new, axis=1))

        def step(kb, _):
            block(kb, None)
            return 0

        lax.fori_loop(0, ratio * i, step, 0)
        masks = _diagonal_masks(t, tk)
        for d in range(ratio):
            block(ratio * i + d, masks[d])
        for p in range(pairs):
            dq_ref[:, p * LANE:(p + 1) * LANE] = (jnp.where(first, dq_acc[2 * p], dq_acc[2 * p + 1]) * scale).astype(BF16)

        @pl.when(i == nq - 1)
        def _():
            dk_ref[...] = (dk_acc[...] * LN_2).astype(BF16)
            dv_ref[...] = dv_acc[...].astype(BF16)

        _ride(rider, "finish", jnp.logical_and(grp == groups - 1, i == nq - 1), riding)

    blk = lambda base: pl.BlockSpec((t, wide), lambda g, i: (i, base // pairs + g))
    full = lambda base: pl.BlockSpec((s, wide), lambda g, i: (0, base // pairs + g))
    out_blk = pl.BlockSpec((t, wide), lambda g, i: (i, g))
    out_full = pl.BlockSpec((s, wide), lambda g, i: (0, g))
    big = jax.ShapeDtypeStruct((s, WIDTH), BF16)
    extra = _rider_call_args(rider, 7, 4)
    outs = pl.pallas_call(
        body, name=name, grid=(groups, nq),
        in_specs=[blk(COL_SB_Q), full(COL_SB_K), full(COL_SB_V), blk(COL_SB_G), out_blk,
                  pl.BlockSpec((t, nh * LANE), lambda g, i: (i, g)), out_blk] + extra["in_specs"],
        out_specs=[out_blk, out_full, out_full, out_blk] + extra["out_specs"],
        out_shape=[big, big, big, big] + extra["out_shape"],
        input_output_aliases=extra["aliases"],
        scratch_shapes=[pltpu.VMEM((s, wide), BF16), pltpu.VMEM((s, wide), BF16),
                        pltpu.VMEM((s, wide), F32), pltpu.VMEM((s, wide), F32),
                        pltpu.VMEM((nh, t, LANE), F32), pltpu.VMEM((nh, t, 1), F32)] + extra["scratch"],
        compiler_params=pltpu.CompilerParams(dimension_semantics=("arbitrary", "arbitrary")),
    )(u, u, u, u, o, after, dy, *extra["inputs"])
    return outs[:4], outs[4:]


def _gate_fwd(u, projs, name, ts=256):
    s = u.shape[0]

    def body(m0, m1, m2, p0, p1, p2, out_ref):
        tot = None
        for m_ref, p_ref in ((m0, p0), (m1, p1), (m2, p2)):
            term = _sigmoid(m_ref[...]) * p_ref[...].astype(F32)
            tot = term if tot is None else tot + term
        out_ref[...] = tot.astype(BF16)

    mspec = lambda n: pl.BlockSpec((ts, D_MODEL), lambda i: (i, COL_MERGE_1024 + n))
    row = pl.BlockSpec((ts, D_MODEL), lambda i: (i, 0))
    return pl.pallas_call(
        body, name=name, grid=(s // ts,),
        in_specs=[mspec(0), mspec(1), mspec(2), row, row, row],
        out_specs=row, out_shape=jax.ShapeDtypeStruct((s, D_MODEL), BF16),
    )(u, u, u, *projs)


def _gate_bwd(u, projs, dmerged, name, ts=256):
    s = u.shape[0]

    def body(m0, m1, m2, p0, p1, p2, dm_ref, dp0, dp1, dp2, dl0, dl1, dl2):
        dm = dm_ref[...].astype(F32)
        for m_ref, p_ref, dp_ref, dl_ref in ((m0, p0, dp0, dl0), (m1, p1, dp1, dl1), (m2, p2, dp2, dl2)):
            gate = _sigmoid(m_ref[...])
            dp_ref[...] = (dm * gate).astype(BF16)
            dl_ref[...] = (dm * p_ref[...].astype(F32) * gate * (1.0 - gate)).astype(BF16)

    mspec = lambda n: pl.BlockSpec((ts, D_MODEL), lambda i: (i, COL_MERGE_1024 + n))
    row = pl.BlockSpec((ts, D_MODEL), lambda i: (i, 0))
    big = jax.ShapeDtypeStruct((s, D_MODEL), BF16)
    outs = pl.pallas_call(
        body, name=name, grid=(s // ts,),
        in_specs=[mspec(0), mspec(1), mspec(2), row, row, row, row],
        out_specs=[row] * 6, out_shape=[big] * 6,
    )(u, u, u, *projs, dmerged)
    return outs[:3], outs[3:]


def _as_rows(a):
    return a.reshape(-1, a.shape[-1])


def _row_tile(rows, cols, bytes_per_row_elem=4, cap=1 << 20):
    tr = rows
    while tr * cols * bytes_per_row_elem > cap and tr % 2 == 0 and (tr // 2) % 16 == 0:
        tr //= 2
    return tr


def _cast_bf16(a, name):
    a2 = _as_rows(a)
    rows, cols = a2.shape
    tr = _row_tile(rows, cols)

    def body(a_ref, o_ref):
        o_ref[...] = a_ref[...].astype(BF16)

    spec = pl.BlockSpec((tr, cols), lambda i: (i, 0))
    out = pl.pallas_call(body, name=name, grid=(rows // tr,), in_specs=[spec], out_specs=spec,
                         out_shape=jax.ShapeDtypeStruct((rows, cols), BF16))(a2)
    return out.reshape(a.shape)


def _adamw(w, g, m, v, name):
    shape = w.shape
    w2, g2, m2, v2 = (_as_rows(a) for a in (w, g, m, v))
    rows, cols = w2.shape
    tr = _row_tile(rows, cols)
    c1 = 1.0 - ADAM_B1 ** ADAM_STEP
    c2 = 1.0 - ADAM_B2 ** ADAM_STEP

    def body(w_ref, g_ref, m_ref, v_ref, d_ref, nm_ref, nv_ref):
        gv = g_ref[...]
        nm = ADAM_B1 * m_ref[...] + (1.0 - ADAM_B1) * gv
        nv = ADAM_B2 * v_ref[...] + (1.0 - ADAM_B2) * (gv * gv)
        nm_ref[...] = nm
        nv_ref[...] = nv
        d_ref[...] = -ADAM_LR * ((nm / c1) / (jnp.sqrt(nv / c2) + ADAM_EPS) + ADAM_WD * w_ref[...])

    spec = pl.BlockSpec((tr, cols), lambda i: (i, 0))
    sds = jax.ShapeDtypeStruct((rows, cols), F32)
    outs = pl.pallas_call(body, name=name, grid=(rows // tr,), in_specs=[spec] * 4, out_specs=[spec] * 3,
                          out_shape=[sds] * 3)(w2, g2, m2, v2)
    return tuple(o.reshape(shape) for o in outs)


def _sum_slots(a, out_dtype, name):
    n = a.shape[0]
    a3 = a.reshape(n, -1, a.shape[-1])
    _, rows, cols = a3.shape
    tr = _row_tile(rows, cols * n)

    def body(a_ref, o_ref):
        tot = a_ref[0].astype(F32)
        for k in range(1, n):
            tot = tot + a_ref[k].astype(F32)
        o_ref[...] = tot.astype(out_dtype)

    out = pl.pallas_call(
        body, name=name, grid=(rows // tr,),
        in_specs=[pl.BlockSpec((n, tr, cols), lambda i: (0, i, 0))],
        out_specs=pl.BlockSpec((tr, cols), lambda i: (i, 0)),
        out_shape=jax.ShapeDtypeStruct((rows, cols), out_dtype))(a3)
    return out.reshape(a.shape[1:])


def _chip_sum(own, recv, axis, core, name):
    half = recv.shape
    nd = len(half)
    last = nd - 1
    if axis == last:
        tl, nt = half[last], 1
    else:
        tl = min(half[last], 2048)
        nt = half[last] // tl
    block = half[:last] + (tl,)

    def own_index(i, core_ref):
        idx = [0] * nd
        idx[last] = i
        if axis == last:
            idx[last] = core_ref[0]
        else:
            idx[axis] = core_ref[0]
        return tuple(idx)

    def recv_index(i, core_ref):
        idx = [0] * nd
        idx[last] = i
        return tuple(idx)

    def body(core_ref, own_ref, recv_ref, o_ref):
        o_ref[...] = (own_ref[...] + recv_ref[...]).astype(BF16)

    return pl.pallas_call(
        body, name=name,
        grid_spec=pltpu.PrefetchScalarGridSpec(
            num_scalar_prefetch=1, grid=(nt,),
            in_specs=[pl.BlockSpec(block, own_index), pl.BlockSpec(block, recv_index)],
            out_specs=pl.BlockSpec(block, recv_index)),
        out_shape=jax.ShapeDtypeStruct(half, BF16),
    )(core, own, recv)


def _mesh_position():
    return lax.axis_index("x"), lax.axis_index("y"), lax.axis_index("c")


def _other_chips(x, y):
    return [(1 - x, y), (x, 1 - y), (1 - x, 1 - y)]


ALL_FLIPS = [(0, 0, 1), (1, 0, 0), (0, 1, 0), (1, 1, 0), (1, 0, 1), (0, 1, 1), (1, 1, 1)]


def _half(ref, axis, which, size):
    idx = [slice(None)] * len(ref.shape)
    idx[axis] = pl.ds(which * size, size)
    return ref.at[tuple(idx)]


def _sub(ref, picks):
    idx = [slice(None)] * len(ref.shape)
    for axis, start, size in picks:
        idx[axis] = pl.ds(start, size)
    return ref.at[tuple(idx)]


def _remote(src, dst, sems_send, sems_recv, k, to):
    return pltpu.make_async_remote_copy(src_ref=src, dst_ref=dst, send_sem=sems_send.at[k], recv_sem=sems_recv.at[k],
                                        device_id=to, device_id_type=MESH)


def _cast_shard(w, layer, shard_axis, pos, name, tr=512):
    shape = w.shape[1:]
    nd = len(shape)
    assert shard_axis in (nd - 1, nd - 2)
    rows, cols = shape[-2:]
    tr = min(tr, rows)
    nt = rows // tr
    lead = shape[:-2]
    full = list(shape)
    full[shard_axis] *= N_CHIPS
    block = (1,) * len(lead) + (tr, cols)

    def in_index(*args):
        return (layer, *args[:-1], 0)

    def out_index(*args):
        *g, pos_ref = args
        if shard_axis == nd - 1:
            return (*g, pos_ref[1])
        return (*g[:-1], pos_ref[1] * nt + g[-1], 0)

    def body(pos_ref, a_ref, o_ref):
        o_ref[...] = a_ref[...].astype(BF16)

    return pl.pallas_call(
        body, name=name,
        grid_spec=pltpu.PrefetchScalarGridSpec(
            num_scalar_prefetch=1, grid=lead + (nt,),
            in_specs=[pl.BlockSpec((None,) + block, in_index)], out_specs=pl.BlockSpec(block, out_index)),
        out_shape=jax.ShapeDtypeStruct(tuple(full), BF16),
    )(pos, w)


class _Rider:
    def __init__(self, inputs, out_shape, aliases, scratch, start, middle, finish):
        self.inputs, self.out_shape, self.aliases, self.scratch = inputs, out_shape, aliases, scratch
        self.start, self.middle, self.finish = start, middle, finish


def _weight_gather_rider(fulls, layout):
    n = len(fulls)

    def copies(outs, sems):
        send_sems, recv_sems = sems
        x, y, c = _mesh_position()
        chips = _other_chips(x, y)
        sibling = (x, y, 1 - c)
        mine = 2 * x + y

        def place(t, chip, core):
            sh_axis, sh_size, half_axis, half_size = layout[t]
            return _sub(outs[t], [(sh_axis, chip * sh_size, sh_size), (half_axis, core * half_size, half_size)])

        direct, arrive, forward, arrive_fwd = [], [], [], []
        for t in range(n):
            for k, (px, py) in enumerate(chips):
                theirs = 2 * px + py
                direct.append(_remote(place(t, mine, c), place(t, mine, c), send_sems, recv_sems, 6 * t + k, (px, py, c)))
                arrive.append(_remote(place(t, theirs, c), place(t, theirs, c), send_sems, recv_sems, 6 * t + k, (px, py, c)))
                forward.append(_remote(place(t, theirs, c), place(t, theirs, c), send_sems, recv_sems, 6 * t + 3 + k, sibling))
                arrive_fwd.append(_remote(place(t, theirs, 1 - c), place(t, theirs, 1 - c), send_sems, recv_sems,
                                          6 * t + 3 + k, sibling))
        return direct, arrive, forward, arrive_fwd

    def start(ins, outs, sems):
        for cp in copies(outs, sems)[0]:
            cp.start()

    def middle(ins, outs, sems):
        _, arrive, forward, _ = copies(outs, sems)
        for a, f in zip(arrive, forward):
            a.wait_recv()
            f.start()

    def finish(ins, outs, sems):
        direct, _, forward, arrive_fwd = copies(outs, sems)
        for cp in arrive_fwd:
            cp.wait_recv()
        for cp in direct + forward:
            cp.wait_send()

    return _Rider(list(fulls), [jax.ShapeDtypeStruct(a.shape, a.dtype) for a in fulls], {k: k for k in range(n)},
                  [pltpu.SemaphoreType.DMA((6 * n,)), pltpu.SemaphoreType.DMA((6 * n,))], start, middle, finish)


WEIGHT_LAYOUT = [(1, 2048, 0, 512), (2, 256, 1, 256), (0, 256, 1, 512)]


def _gather_weights(fulls, conv_w):
    rider = _weight_gather_rider(fulls, WEIGHT_LAYOUT)
    n = len(fulls)

    def body(*refs):
        cw, outs, cw_f = refs[n], refs[n + 1:2 * n + 1], refs[2 * n + 1]
        sems, (cw_send, cw_recv, local_sem) = refs[2 * n + 2:2 * n + 4], refs[2 * n + 4:]
        x, y, c = _mesh_position()
        chips = _other_chips(x, y)
        mine = 2 * x + y
        local = pltpu.make_async_copy(cw, cw_f.at[mine], local_sem.at[0])
        local.start()
        rider.start(None, outs, sems)
        small = [_remote(cw, cw_f.at[mine], cw_send, cw_recv, k, (*chip, c)) for k, chip in enumerate(chips)]
        for cp in small:
            cp.start()
        rider.middle(None, outs, sems)
        rider.finish(None, outs, sems)
        for k, (px, py) in enumerate(chips):
            _remote(cw, cw_f.at[2 * px + py], cw_send, cw_recv, k, (px, py, c)).wait_recv()
        for cp in small:
            cp.wait_send()
        local.wait()

    outs = pl.pallas_call(
        body, name="gather_weights",
        in_specs=[ANY] * (n + 1), out_specs=[ANY] * (n + 1),
        out_shape=rider.out_shape + [jax.ShapeDtypeStruct((N_CHIPS,) + conv_w.shape, F32)],
        input_output_aliases=rider.aliases,
        scratch_shapes=rider.scratch + [pltpu.SemaphoreType.DMA((3,)), pltpu.SemaphoreType.DMA((3,)),
                                        pltpu.SemaphoreType.DMA((1,))],
    )(*fulls, conv_w)
    return outs[:n], outs[n]


def _swap_halves(items, name):
    n = len(items)
    halves = []
    for a, axis in items:
        shp = list(a.shape)
        shp[axis] //= 2
        halves.append(tuple(shp))

    def body(*refs):
        srcs, dsts, (send_sems, recv_sems) = refs[:n], refs[n:2 * n], refs[2 * n:]
        x, y, c = _mesh_position()
        copies = []
        for k in range(n):
            axis = items[k][1]
            copies.append(_remote(_half(srcs[k], axis, 1 - c, halves[k][axis]), dsts[k], send_sems, recv_sems, k,
                                  (x, y, 1 - c)))
        for cp in copies:
            cp.start()
        for cp in copies:
            cp.wait()

    return pl.pallas_call(
        body, name=name, in_specs=[ANY] * n, out_specs=[ANY] * n,
        out_shape=[jax.ShapeDtypeStruct(h, a.dtype) for h, (a, _) in zip(halves, items)],
        scratch_shapes=[pltpu.SemaphoreType.DMA((n,)), pltpu.SemaphoreType.DMA((n,))],
    )(*[a for a, _ in items])


def _grad_exchange_rider(items):
    n = len(items)
    slices = []
    for a, axis in items:
        shp = list(a.shape)
        shp[axis] //= N_CHIPS
        slices.append(tuple(shp))

    def copies(ins, outs, sems):
        send_sems, recv_sems = sems
        x, y, c = _mesh_position()
        made = []
        for k in range(n):
            axis = items[k][1]
            for r, (px, py) in enumerate(_other_chips(x, y)):
                made.append(_remote(_half(ins[k], axis, 2 * px + py, slices[k][axis]), outs[k].at[r],
                                    send_sems, recv_sems, 3 * k + r, (px, py, c)))
        return made

    def start(ins, outs, sems):
        for cp in copies(ins, outs, sems):
            cp.start()

    def finish(ins, outs, sems):
        for cp in copies(ins, outs, sems):
            cp.wait()

    return _Rider([a for a, _ in items], [jax.ShapeDtypeStruct((N_CHIPS - 1,) + s, BF16) for s in slices], {},
                  [pltpu.SemaphoreType.DMA((3 * n,)), pltpu.SemaphoreType.DMA((3 * n,))], start, None, finish)


def _exchange_grads(items, small):
    rider = _grad_exchange_rider(items)
    n = len(items)

    def body(*refs):
        srcs, small_ref = refs[:n], refs[n]
        dsts, small_all = refs[n + 1:2 * n + 1], refs[2 * n + 1]
        sems, (small_send, small_recv, local_sem) = refs[2 * n + 2:2 * n + 4], refs[2 * n + 4:]
        x, y, c = _mesh_position()
        me = 4 * x + 2 * y + c
        local = pltpu.make_async_copy(small_ref, small_all.at[me], local_sem.at[0])
        local.start()
        rider.start(srcs, dsts, sems)
        small_copies = [_remote(small_ref, small_all.at[me], small_send, small_recv, r, (x ^ fx, y ^ fy, c ^ fc))
                        for r, (fx, fy, fc) in enumerate(ALL_FLIPS)]
        for cp in small_copies:
            cp.start()
        rider.finish(srcs, dsts, sems)
        for cp in small_copies:
            cp.wait()
        local.wait()

    outs = pl.pallas_call(
        body, name="exchange_grads", in_specs=[ANY] * (n + 1), out_specs=[ANY] * (n + 1),
        out_shape=rider.out_shape + [jax.ShapeDtypeStruct((2 * N_CHIPS,) + small.shape, F32)],
        scratch_shapes=rider.scratch + [pltpu.SemaphoreType.DMA((len(ALL_FLIPS),)),
                                        pltpu.SemaphoreType.DMA((len(ALL_FLIPS),)), pltpu.SemaphoreType.DMA((1,))],
    )(*[a for a, _ in items], small)
    return outs[:n], outs[n]


def _sum_chips(recv, own, shard_axis, split_axis, pos, dest, layer, name, tr=128):
    sl = recv.shape[1:]
    nd = len(sl)
    tiled = nd == 2 and sl[0] > tr
    nt = sl[0] // tr if tiled else 1
    block = ((tr,) + sl[1:]) if tiled else sl
    shard = list(sl)
    shard[split_axis] *= 2

    def recv_index(i, pos_ref):
        return (0, i) + (0,) * (nd - 1) if tiled else (0,) * (nd + 1)

    def own_index(i, pos_ref):
        idx = [0] * nd
        idx[shard_axis] = pos_ref[1]
        if tiled:
            idx[0] = pos_ref[1] * nt + i if shard_axis == 0 else i
        return tuple(idx)

    def out_index(i, pos_ref):
        idx = [0] * nd
        idx[split_axis] = pos_ref[0]
        if tiled:
            idx[0] = pos_ref[0] * nt + i if split_axis == 0 else i
        return (layer, *idx)

    def body(pos_ref, recv_ref, own_ref, *rest):
        o_ref = rest[-1]
        tot = own_ref[...].astype(F32)
        for k in range(N_CHIPS - 1):
            tot = tot + recv_ref[k].astype(F32)
        o_ref[0] = tot

    in_specs = [pl.BlockSpec((N_CHIPS - 1,) + block, recv_index), pl.BlockSpec(block, own_index)]
    args = [pos, recv, own]
    aliases = {}
    if dest is not None:
        in_specs.append(ANY)
        args.append(dest)
        aliases = {3: 0}
    return pl.pallas_call(
        body, name=name,
        grid_spec=pltpu.PrefetchScalarGridSpec(
            num_scalar_prefetch=1, grid=(nt,), in_specs=in_specs,
            out_specs=pl.BlockSpec((1,) + block, out_index)),
        out_shape=jax.ShapeDtypeStruct((DEPTH,) + tuple(shard), F32),
        input_output_aliases=aliases,
    )(*args)


def _share_halves(bufs, name):
    n = len(bufs)

    def body(*refs):
        outs, (send_sems, recv_sems) = refs[n:2 * n], refs[2 * n:]
        x, y, c = _mesh_position()
        copies = []
        for k, (a, axis) in enumerate(bufs):
            size = a.shape[1 + axis] // 2
            mine = _half(outs[k], 1 + axis, c, size)
            copies.append(_remote(mine, mine, send_sems, recv_sems, k, (x, y, 1 - c)))
        for cp in copies:
            cp.start()
        for cp in copies:
            cp.wait()

    return pl.pallas_call(
        body, name=name, in_specs=[ANY] * n, out_specs=[ANY] * n,
        out_shape=[jax.ShapeDtypeStruct(a.shape, F32) for a, _ in bufs],
        input_output_aliases={k: k for k in range(n)},
        scratch_shapes=[pltpu.SemaphoreType.DMA((n,)), pltpu.SemaphoreType.DMA((n,))],
    )(*[a for a, _ in bufs])


def _layer_fwd(x, p, l, rider=None):
    tag = f"l{l}_"
    h = _rms_fwd(x, p["pre_g"], tag + "pre_norm")
    u = _matmul(h, p["w_in"], "nn", F32, tag + "in_proj")
    y_pool = _pool_fwd(u, p["pool_w"], p["pool_scale"], tag + "pool")
    y_conv = _conv_fwd(u, p["conv_w"], p["conv_b"], tag + "conv")
    (o_sb, y_sb, sb_after), carried = _sb_fwd(u, tag + "stickbreak", rider=rider)
    ys = [y_pool, y_conv, y_sb]
    projs = [_matmul(ys[n], p["w_branch"], "nn", BF16, tag + f"branch_proj{n}", b_lead=(n,)) for n in range(3)]
    merged = _gate_fwd(u, projs, tag + "merge")
    out = _matmul(merged, p["w_out"], "nn", F32, tag + "out_proj")
    saved = dict(x=x, h=h, u=u, ys=ys, o_sb=o_sb, sb_after=sb_after, projs=projs, merged=merged, out=out)
    return out, saved, carried


def _layer_bwd(dy, p, saved, l, rider=None):
    tag = f"l{l}_bwd_"
    u = saved["u"]
    d_out, g_post = _rms_bwd(saved["out"], p["post_g"], dy, None, BF16, tag + "post_norm")
    d_merged = _matmul(d_out, p["w_out"], "nt", BF16, tag + "out_proj_dx")
    g_w_out = _matmul(saved["merged"], d_out, "tn", F32, tag + "out_proj_dw", tk=2048)
    d_projs, d_logits = _gate_bwd(u, saved["projs"], d_merged, tag + "merge")
    g_w_branch = [_matmul(saved["ys"][n], d_projs[n], "tn", F32, tag + f"branch_dw{n}", tk=2048) for n in range(3)]
    d_ys = [_matmul(d_projs[n], p["w_branch"], "nt", F32, tag + f"branch_dx{n}", b_lead=(n,)) for n in range(3)]
    d_pv, d_pg, g_pool_w, g_pool_scale = _pool_bwd(u, d_ys[0], p["pool_w"], p["pool_scale"], tag + "pool")
    d_cx, d_cgb, d_cgc, d_cg, g_conv_w, g_conv_b = _conv_bwd(u, d_ys[1], p["conv_w"], p["conv_b"], tag + "conv")
    (d_q, d_k, d_v, d_sg), carried = _sb_bwd(u, saved["o_sb"], saved["sb_after"], d_ys[2], tag + "stickbreak",
                                             rider=rider)
    du = jnp.concatenate([d_pv, d_pg, d_cx, d_cgb, d_cgc, d_cg, d_q, d_k, d_v, d_sg] + list(d_logits), axis=1)
    g_w_in = _matmul(saved["h"], du, "tn", F32, tag + "in_proj_dw", tk=2048)
    dh = _matmul(du, p["w_in"], "nt", BF16, tag + "in_proj_dx", tk=2048)
    dx, g_pre = _rms_bwd(saved["x"], p["pre_g"], dh, dy, F32, tag + "pre_norm")
    grads = dict(w_in=g_w_in, w_branch=jnp.stack(g_w_branch), w_out=g_w_out, pre_g=g_pre, post_g=g_post,
                 pool_w=g_pool_w, pool_scale=g_pool_scale, conv_w=g_conv_w, conv_b=g_conv_b)
    return dx, grads, carried


SMALL_ORDER = ["pre_g", "pool_w", "pool_scale", "conv_w", "conv_b", "post_g"]


def _pack_small(per_layer):
    parts, spans, at = [], {}, 0
    for name in SMALL_ORDER:
        a = jnp.stack([per_layer[l][name] for l in range(DEPTH)]).reshape(-1, LANE)
        parts.append(a)
        spans[name] = (at, a.shape[0])
        at += a.shape[0]
    return jnp.concatenate(parts, axis=0), spans


def kernel(x, pre_norm_g, w_in, pool_w, pool_scale, conv_w, conv_b, w_branch, w_out, post_norm_g, loss_target, m_pre_norm_g, m_w_in, m_pool_w, m_pool_scale, m_conv_w, m_conv_b, m_w_branch, m_w_out, m_post_norm_g, v_pre_norm_g, v_w_in, v_pool_w, v_pool_scale, v_conv_w, v_conv_b, v_w_branch, v_w_out, v_post_norm_g):
    mx, my, mc = _mesh_position()
    chip = 2 * mx + my
    core = mc.astype(jnp.int32).reshape(1)
    pos = jnp.stack([mc, chip]).astype(jnp.int32)

    names = ["w_in", "w_branch", "w_out"]
    given = dict(w_in=w_in, w_branch=w_branch, w_out=w_out)
    in_place = [[_cast_shard(given[n], l, WEIGHT_LAYOUT[i][0], pos, f"cast_{n}{l}") for i, n in enumerate(names)]
                for l in range(DEPTH)]
    gathered, conv_w_by_chip = _gather_weights(in_place[0], conv_w)
    conv_w_f = conv_w_by_chip.transpose(1, 2, 0, 3).reshape(DEPTH, 3, WIDTH)
    pool_w_b = _cast_bf16(pool_w, "cast_pool_w")

    def layer_params(l, big):
        return dict(pre_g=pre_norm_g[l:l + 1], post_g=post_norm_g[l:l + 1], w_in=big[0], w_branch=big[1],
                    w_out=big[2], pool_w=pool_w_b[l], pool_scale=pool_scale[l:l + 1], conv_w=conv_w_f[l],
                    conv_b=conv_b[l:l + 1])

    act = x[0]
    params, saved = [], []
    for l in range(DEPTH):
        params.append(layer_params(l, gathered))
        rider = _weight_gather_rider(in_place[l + 1], WEIGHT_LAYOUT) if l + 1 < DEPTH else None
        out, sv, gathered = _layer_fwd(act, params[l], l, rider)
        saved.append(sv)
        if l < DEPTH - 1:
            act = _resid_out(act, out, params[l]["post_g"], None, f"l{l}_resid")
    dy, loss_part = _resid_out(act, saved[-1]["out"], params[-1]["post_g"], loss_target[0], "loss_head")
    loss = lax.psum(loss_part[0, 0], ("x", "y", "c"))

    split_axis = dict(w_in=0, w_branch=1, w_out=1)
    shard_axis = dict(w_in=1, w_branch=2, w_out=0)
    grads, chip_sums, by_chip = [None] * DEPTH, [None] * DEPTH, [None] * DEPTH
    rider = None
    for l in reversed(range(DEPTH)):
        dy, grads[l], carried = _layer_bwd(dy, params[l], saved[l], l, rider)
        if rider is not None:
            by_chip[l + 1] = carried
        items = [(grads[l][n], split_axis[n]) for n in names]
        from_sibling = _swap_halves(items, f"swap_grad_halves{l}")
        chip_sums[l] = [(_chip_sum(a, r, axis, core, f"chip_sum{l}_{n}"), shard_axis[n])
                        for n, (a, axis), r in zip(names, items, from_sibling)]
        rider = _grad_exchange_rider(chip_sums[l]) if l > 0 else None
    grad_x = dy[None]
    small_part, spans = _pack_small(grads)
    by_chip[0], small_all = _exchange_grads(chip_sums[0], small_part)
    bufs = []
    for i, n in enumerate(names):
        dest = None
        for l in range(DEPTH):
            dest = _sum_chips(by_chip[l][i], chip_sums[l][i][0], shard_axis[n], split_axis[n], pos, dest, l,
                              f"sum_chips{l}_{n}")
        bufs.append((dest, split_axis[n]))
    g_w_in, g_w_branch, g_w_out = _share_halves(bufs, "share_grad_halves")

    small_sum = _sum_slots(small_all, F32, "sum_small")
    small = {}
    for name, like in (("pre_g", pre_norm_g), ("pool_w", pool_w), ("pool_scale", pool_scale), ("conv_b", conv_b),
                       ("post_g", post_norm_g)):
        at, n = spans[name]
        small[name] = small_sum[at:at + n].reshape(like.shape)
    at, n = spans["conv_w"]
    g_conv_w_full = small_sum[at:at + n].reshape(DEPTH, 3, WIDTH)
    g_conv_w = lax.dynamic_slice_in_dim(g_conv_w_full, chip * conv_w.shape[2], conv_w.shape[2], axis=2)

    g = dict(pre_norm_g=small["pre_g"], w_in=g_w_in, pool_w=small["pool_w"], pool_scale=small["pool_scale"],
             conv_w=g_conv_w, conv_b=small["conv_b"], w_branch=g_w_branch, w_out=g_w_out, post_norm_g=small["post_g"])
    w = dict(pre_norm_g=pre_norm_g, w_in=w_in, pool_w=pool_w, pool_scale=pool_scale, conv_w=conv_w, conv_b=conv_b,
             w_branch=w_branch, w_out=w_out, post_norm_g=post_norm_g)
    m = dict(pre_norm_g=m_pre_norm_g, w_in=m_w_in, pool_w=m_pool_w, pool_scale=m_pool_scale, conv_w=m_conv_w,
             conv_b=m_conv_b, w_branch=m_w_branch, w_out=m_w_out, post_norm_g=m_post_norm_g)
    v = dict(pre_norm_g=v_pre_norm_g, w_in=v_w_in, pool_w=v_pool_w, pool_scale=v_pool_scale, conv_w=v_conv_w,
             conv_b=v_conv_b, w_branch=v_w_branch, w_out=v_w_out, post_norm_g=v_post_norm_g)
    order = ["pre_norm_g", "w_in", "pool_w", "pool_scale", "conv_w", "conv_b", "w_branch", "w_out", "post_norm_g"]
    upd = {n: _adamw(w[n], g[n], m[n], v[n], "adamw_" + n) for n in order}
    return (loss, grad_x, *[g[n] for n in order], *[upd[n][0] for n in order], *[upd[n][1] for n in order],
            *[upd[n][2] for n in order])
```

```python
import functools

import jax
import jax.numpy as jnp
from jax import lax
from jax.experimental import pallas as pl
from jax.experimental.pallas import tpu as pltpu

F32 = jnp.float32
BF16 = jnp.bfloat16
MESH = pl.DeviceIdType.MESH
ANY = pl.BlockSpec(memory_space=pl.ANY)

DEPTH = 2
D_MODEL = 1024
WIDTH = 512
N_IN = 8192
N_CHIPS = 4
HEAD_DIM = 64
RMS_EPS = 1e-6
POOL_HALO = 16
CONV_HALO = 8
LANE = 128
COL_POOL_V, COL_POOL_G = 0, 4
COL_CONV_X, COL_CONV_GB, COL_CONV_GC, COL_CONV_G = 8, 12, 16, 20
COL_SB_Q, COL_SB_K, COL_SB_V, COL_SB_G = 24, 28, 32, 36
COL_MERGE_1024 = 5

ADAM_LR, ADAM_B1, ADAM_B2, ADAM_EPS, ADAM_WD, ADAM_STEP = 0.001, 0.9, 0.999, 1e-08, 0.01, 10

NN = (((1,), (0,)), ((), ()))
NT = (((1,), (1,)), ((), ()))
TN = (((0,), (0,)), ((), ()))


def _sigmoid(x):
    return 1.0 / (1.0 + jnp.exp(-x))


def _silu_and_grad(x):
    s = _sigmoid(x)
    return x * s, s * (1.0 + x * (1.0 - s))


def _dot(a, b, dims):
    return lax.dot_general(a, b, dims, preferred_element_type=F32)


def _matmul(a, b, mode, out_dtype, name, tm=1024, tn=1024, tk=1024, b_lead=(), rider=None):
    b_shape = b.shape[len(b_lead):]
    if mode == "nn":
        (m, k), (k2, n) = a.shape, b_shape
    elif mode == "nt":
        (m, k), (n, k2) = a.shape, b_shape
    else:
        (k, m), (k2, n) = a.shape, b_shape
    assert k == k2 and a.dtype == BF16 and b.dtype == BF16
    tm, tn, tk = min(tm, m), min(tn, n), min(tk, k)
    assert m % tm == 0 and n % tn == 0 and k % tk == 0
    nk = k // tk
    dims = {"nn": NN, "nt": NT, "tn": TN}[mode]

    grid = (m // tm, n // tn, nk)

    def at_step(which):
        return functools.reduce(jnp.logical_and, [pl.program_id(d) == (g - 1 if which else 0) for d, g in enumerate(grid)])

    def body(*refs):
        (a_ref, b_ref, o_ref, *scratch), riding = _split_refs(refs, 2, 1, 1 if nk > 1 else 0, rider)
        _ride(rider, "start", at_step(0), riding)
        compute(a_ref, b_ref, o_ref, scratch)
        _ride(rider, "finish", at_step(1), riding)

    def compute(a_ref, b_ref, o_ref, scratch):
        p = _dot(a_ref[...], b_ref[...], dims)
        if nk == 1:
            o_ref[...] = p.astype(o_ref.dtype)
        else:
            acc = scratch[0]
            kk = pl.program_id(2)

            @pl.when(kk == 0)
            def _():
                acc[...] = p

            @pl.when(jnp.logical_and(kk > 0, kk < nk - 1))
            def _():
                acc[...] += p

            @pl.when(kk == nk - 1)
            def _():
                o_ref[...] = (acc[...] + p).astype(o_ref.dtype)

    if mode == "tn":
        a_spec = pl.BlockSpec((tk, tm), lambda i, j, kk: (kk, i))
    else:
        a_spec = pl.BlockSpec((tm, tk), lambda i, j, kk: (i, kk))
    squeezed = (None,) * len(b_lead)
    if mode == "nt":
        b_spec = pl.BlockSpec(squeezed + (tn, tk), lambda i, j, kk: (*b_lead, j, kk))
    else:
        b_spec = pl.BlockSpec(squeezed + (tk, tn), lambda i, j, kk: (*b_lead, kk, j))
    extra = _rider_call_args(rider, 2, 1)
    outs = pl.pallas_call(
        body, name=name, grid=grid,
        in_specs=[a_spec, b_spec] + extra["in_specs"],
        out_specs=[pl.BlockSpec((tm, tn), lambda i, j, kk: (i, j))] + extra["out_specs"],
        out_shape=[jax.ShapeDtypeStruct((m, n), out_dtype)] + extra["out_shape"],
        input_output_aliases=extra["aliases"],
        scratch_shapes=([pltpu.VMEM((tm, tn), F32)] if nk > 1 else []) + extra["scratch"],
        compiler_params=pltpu.CompilerParams(dimension_semantics=("arbitrary",) * 3 if rider else
                                             ("parallel", "parallel", "arbitrary")),
    )(a, b, *extra["inputs"])
    return (outs[0], outs[1:]) if rider else outs[0]


def _rms_fwd(x, g, name, ts=512):
    s, d = x.shape

    def body(x_ref, g_ref, h_ref):
        xv = x_ref[...]
        r = lax.rsqrt(jnp.mean(xv * xv, axis=-1, keepdims=True) + RMS_EPS)
        h_ref[...] = (xv * r * g_ref[...]).astype(BF16)

    return pl.pallas_call(
        body, name=name, grid=(s // ts,),
        in_specs=[pl.BlockSpec((ts, d), lambda i: (i, 0)), pl.BlockSpec((1, d), lambda i: (0, 0))],
        out_specs=pl.BlockSpec((ts, d), lambda i: (i, 0)),
        out_shape=jax.ShapeDtypeStruct((s, d), BF16),
    )(x, g)


def _rms_bwd(xin, g, dh, resid, out_dtype, name, ts=512):
    s, d = xin.shape
    has_resid = resid is not None

    def body(*refs):
        if has_resid:
            x_ref, g_ref, dh_ref, res_ref, dx_ref, dg_ref = refs
        else:
            x_ref, g_ref, dh_ref, dx_ref, dg_ref = refs
        xv = x_ref[...]
        dhv = dh_ref[...].astype(F32)
        r = lax.rsqrt(jnp.mean(xv * xv, axis=-1, keepdims=True) + RMS_EPS)
        nrm = xv * r
        dn = dhv * g_ref[...]
        dx = r * (dn - nrm * jnp.mean(dn * nrm, axis=-1, keepdims=True))
        if has_resid:
            dx = dx + res_ref[...]
        dx_ref[...] = dx.astype(dx_ref.dtype)
        part = jnp.sum(dhv * nrm, axis=0, keepdims=True)

        @pl.when(pl.program_id(0) == 0)
        def _():
            dg_ref[...] = part

        @pl.when(pl.program_id(0) > 0)
        def _():
            dg_ref[...] += part

    row = pl.BlockSpec((ts, d), lambda i: (i, 0))
    vec = pl.BlockSpec((1, d), lambda i: (0, 0))
    ins = [xin, g, dh] + ([resid] if has_resid else [])
    return pl.pallas_call(
        body, name=name, grid=(s // ts,),
        in_specs=[row, vec, row] + ([row] if has_resid else []),
        out_specs=[row, vec],
        out_shape=[jax.ShapeDtypeStruct((s, d), out_dtype), jax.ShapeDtypeStruct((1, d), F32)],
        compiler_params=pltpu.CompilerParams(dimension_semantics=("arbitrary",)),
    )(*ins)


def _resid_out(x, out, g, target, name, ts=512):
    s, d = x.shape
    has_loss = target is not None

    def body(*refs):
        if has_loss:
            x_ref, o_ref, g_ref, t_ref, dy_ref, loss_ref = refs
        else:
            x_ref, o_ref, g_ref, y_ref = refs
        ov = o_ref[...]
        r = lax.rsqrt(jnp.mean(ov * ov, axis=-1, keepdims=True) + RMS_EPS)
        yv = x_ref[...] + ov * r * g_ref[...]
        if not has_loss:
            y_ref[...] = yv
            return
        err = yv - t_ref[...]
        dy_ref[...] = err * (1.0 / d)
        part = jnp.sum(jnp.sum(err * err, axis=-1, keepdims=True), axis=0, keepdims=True) * (0.5 / d)
        part = jnp.broadcast_to(part, (1, LANE))

        @pl.when(pl.program_id(0) == 0)
        def _():
            loss_ref[...] = part

        @pl.when(pl.program_id(0) > 0)
        def _():
            loss_ref[...] += part

    row = pl.BlockSpec((ts, d), lambda i: (i, 0))
    vec = pl.BlockSpec((1, d), lambda i: (0, 0))
    if has_loss:
        return pl.pallas_call(
            body, name=name, grid=(s // ts,),
            in_specs=[row, row, vec, row],
            out_specs=[row, pl.BlockSpec((1, LANE), lambda i: (0, 0))],
            out_shape=[jax.ShapeDtypeStruct((s, d), F32), jax.ShapeDtypeStruct((1, LANE), F32)],
            compiler_params=pltpu.CompilerParams(dimension_semantics=("arbitrary",)),
        )(x, out, g, target)
    return pl.pallas_call(
        body, name=name, grid=(s // ts,),
        in_specs=[row, row, vec], out_specs=row,
        out_shape=jax.ShapeDtypeStruct((s, d), F32),
    )(x, out, g)


def _rows_before(ref, start, n, halo):
    if start == 0:
        return jnp.concatenate([jnp.zeros((halo, ref.shape[1]), F32), ref[0:n, :]], axis=0)
    return ref[start - halo:start + n, :]


def _rows_after(ref, start, n, halo):
    if start + n == ref.shape[0]:
        return jnp.concatenate([ref[start:start + n, :].astype(F32), jnp.zeros((halo, ref.shape[1]), F32)], axis=0)
    return ref[start:start + n + halo, :].astype(F32)


def _pick_window(group, s2, s4, s8, s16):
    return jnp.where(group == 0, s2, jnp.where(group == 1, s4, jnp.where(group == 2, s8, s16)))


def _trailing_sums(ext, group):
    s2 = ext + pltpu.roll(ext, 1, 0)
    s4 = s2 + pltpu.roll(s2, 2, 0)
    s8 = s4 + pltpu.roll(s4, 4, 0)
    s16 = s8 + pltpu.roll(s8, 8, 0)
    return _pick_window(group, s2, s4, s8, s16)


def _leading_sums(ext, group):
    n = ext.shape[0]
    s2 = ext + pltpu.roll(ext, n - 1, 0)
    s4 = s2 + pltpu.roll(s2, n - 2, 0)
    s8 = s4 + pltpu.roll(s4, n - 4, 0)
    s16 = s8 + pltpu.roll(s8, n - 8, 0)
    return _pick_window(group, s2, s4, s8, s16)


def _window_count(start, n, group):
    pos = start + lax.broadcasted_iota(jnp.int32, (n, LANE), 0)
    return jnp.minimum(pos + 1, 2 << group).astype(F32)


def _pooled(v_ref, start, n, group):
    ext = _rows_before(v_ref, start, n, POOL_HALO)
    sums = _trailing_sums(ext, group)[POOL_HALO:, :]
    return sums / _window_count(start, n, group) - ext[POOL_HALO:, :]


def _pool_fwd(u, pool_w, pool_scale, name, ts=512):
    s = u.shape[0]

    def body(v_ref, gate_ref, w_ref, sc_ref, y_ref):
        group = pl.program_id(0)
        for c in range(s // ts):
            a = c * ts
            pooled = _pooled(v_ref, a, ts, group)
            mixed = _dot(pooled.astype(BF16), w_ref[...], NN)
            gate = gate_ref[a:a + ts, :]
            y_ref[a:a + ts, :] = (mixed * sc_ref[...] * (gate * _sigmoid(gate))).astype(BF16)

    col = lambda base: pl.BlockSpec((s, LANE), lambda g: (0, base + g))
    return pl.pallas_call(
        body, name=name, grid=(4,),
        in_specs=[col(COL_POOL_V), col(COL_POOL_G),
                  pl.BlockSpec((None, LANE, LANE), lambda g: (g, 0, 0)),
                  pl.BlockSpec((1, LANE), lambda g: (0, g))],
        out_specs=pl.BlockSpec((s, LANE), lambda g: (0, g)),
        out_shape=jax.ShapeDtypeStruct((s, WIDTH), BF16),
    )(u, u, pool_w, pool_scale)


def _pool_bwd(u, dy, pool_w, pool_scale, name, ts=512):
    s = u.shape[0]

    def body(v_ref, gate_ref, dy_ref, w_ref, sc_ref, dv_ref, dgate_ref, dw_ref, dsc_ref):
        group = pl.program_id(0)
        w = w_ref[...]
        scale = sc_ref[...]
        dw = jnp.zeros((LANE, LANE), F32)
        dsc = jnp.zeros((1, LANE), F32)
        for c in range(s // ts):
            a = c * ts
            n_ext = ts + POOL_HALO
            gate_e = _rows_after(gate_ref, a, ts, POOL_HALO)
            dy_e = _rows_after(dy_ref, a, ts, POOL_HALO)
            silu_e, dsilu_e = _silu_and_grad(gate_e)
            dms_e = dy_e * silu_e
            dm_e = (dms_e * scale).astype(BF16)
            dpool_e = _dot(dm_e, w, NT)
            spread = _leading_sums(dpool_e / _window_count(a, n_ext, group), group)
            dv_ref[a:a + ts, :] = (spread[0:ts, :] - dpool_e[0:ts, :]).astype(BF16)
            pooled = _pooled(v_ref, a, ts, group).astype(BF16)
            mixed = _dot(pooled, w, NN)
            dgate_ref[a:a + ts, :] = (dy_e[0:ts, :] * mixed * scale * dsilu_e[0:ts, :]).astype(BF16)
            dsc = dsc + jnp.sum(dms_e[0:ts, :] * mixed, axis=0, keepdims=True)
            dw = dw + _dot(pooled, dm_e[0:ts, :], TN)
        dw_ref[...] = dw
        dsc_ref[...] = dsc

    col = lambda base: pl.BlockSpec((s, LANE), lambda g: (0, base + g))
    out_col = pl.BlockSpec((s, LANE), lambda g: (0, g))
    return pl.pallas_call(
        body, name=name, grid=(4,),
        in_specs=[col(COL_POOL_V), col(COL_POOL_G), out_col,
                  pl.BlockSpec((None, LANE, LANE), lambda g: (g, 0, 0)),
                  pl.BlockSpec((1, LANE), lambda g: (0, g))],
        out_specs=[out_col, out_col,
                   pl.BlockSpec((None, LANE, LANE), lambda g: (g, 0, 0)),
                   pl.BlockSpec((1, LANE), lambda g: (0, g))],
        out_shape=[jax.ShapeDtypeStruct((s, WIDTH), BF16), jax.ShapeDtypeStruct((s, WIDTH), BF16),
                   jax.ShapeDtypeStruct((4, LANE, LANE), F32), jax.ShapeDtypeStruct((1, WIDTH), F32)],
    )(u, u, dy, pool_w, pool_scale)


def _conv_taps(x_ref, gc_ref, start, n):
    z_ext = _rows_before(gc_ref, start, n, CONV_HALO) * _rows_before(x_ref, start, n, CONV_HALO)
    z0 = z_ext[CONV_HALO:, :]
    z1 = pltpu.roll(z_ext, 1, 0)[CONV_HALO:, :]
    z2 = pltpu.roll(z_ext, 2, 0)[CONV_HALO:, :]
    return z0, z1, z2


def _conv_fwd(u, conv_w, conv_b, name, ts=512):
    s = u.shape[0]

    def body(x_ref, gb_ref, gc_ref, g_ref, w_ref, b_ref, y_ref):
        w0, w1, w2 = w_ref[0:1, :], w_ref[1:2, :], w_ref[2:3, :]
        for c in range(s // ts):
            a = c * ts
            z0, z1, z2 = _conv_taps(x_ref, gc_ref, a, ts)
            y = w2 * z0 + w1 * z1 + w0 * z2 + b_ref[...]
            gate = g_ref[a:a + ts, :]
            y_ref[a:a + ts, :] = (gb_ref[a:a + ts, :] * y * (gate * _sigmoid(gate))).astype(BF16)

    col = lambda base: pl.BlockSpec((s, LANE), lambda j: (0, base + j))
    return pl.pallas_call(
        body, name=name, grid=(4,),
        in_specs=[col(COL_CONV_X), col(COL_CONV_GB), col(COL_CONV_GC), col(COL_CONV_G),
                  pl.BlockSpec((3, LANE), lambda j: (0, j)), pl.BlockSpec((1, LANE), lambda j: (0, j))],
        out_specs=pl.BlockSpec((s, LANE), lambda j: (0, j)),
        out_shape=jax.ShapeDtypeStruct((s, WIDTH), BF16),
    )(u, u, u, u, conv_w, conv_b)


def _conv_bwd(u, dy, conv_w, conv_b, name, ts=512):
    s = u.shape[0]

    def body(x_ref, gb_ref, gc_ref, g_ref, dy_ref, w_ref, b_ref,
             dx_ref, dgb_ref, dgc_ref, dg_ref, dw_ref, db_ref):
        w0, w1, w2 = w_ref[0:1, :], w_ref[1:2, :], w_ref[2:3, :]
        acc = [jnp.zeros((1, LANE), F32) for _ in range(4)]
        for c in range(s // ts):
            a = c * ts
            n_ext = ts + CONV_HALO
            gate_e = _rows_after(g_ref, a, ts, CONV_HALO)
            silu_e, dsilu_e = _silu_and_grad(gate_e)
            dy_e = _rows_after(dy_ref, a, ts, CONV_HALO)
            gb_e = _rows_after(gb_ref, a, ts, CONV_HALO)
            dyy_e = dy_e * silu_e * gb_e
            dz = (w2 * dyy_e + w1 * pltpu.roll(dyy_e, n_ext - 1, 0) + w0 * pltpu.roll(dyy_e, n_ext - 2, 0))[0:ts, :]
            z0, z1, z2 = _conv_taps(x_ref, gc_ref, a, ts)
            yb = w2 * z0 + w1 * z1 + w0 * z2 + b_ref[...]
            dyv = dy_e[0:ts, :]
            dyy = dyy_e[0:ts, :]
            dg_ref[a:a + ts, :] = (dyv * gb_e[0:ts, :] * yb * dsilu_e[0:ts, :]).astype(BF16)
            dgb_ref[a:a + ts, :] = (dyv * silu_e[0:ts, :] * yb).astype(BF16)
            dx_ref[a:a + ts, :] = (dz * gc_ref[a:a + ts, :]).astype(BF16)
            dgc_ref[a:a + ts, :] = (dz * x_ref[a:a + ts, :]).astype(BF16)
            for i, term in enumerate((dyy * z2, dyy * z1, dyy * z0, dyy)):
                acc[i] = acc[i] + jnp.sum(term, axis=0, keepdims=True)
        dw_ref[0:1, :] = acc[0]
        dw_ref[1:2, :] = acc[1]
        dw_ref[2:3, :] = acc[2]
        db_ref[...] = acc[3]

    col = lambda base: pl.BlockSpec((s, LANE), lambda j: (0, base + j))
    out_col = pl.BlockSpec((s, LANE), lambda j: (0, j))
    big = jax.ShapeDtypeStruct((s, WIDTH), BF16)
    return pl.pallas_call(
        body, name=name, grid=(4,),
        in_specs=[col(COL_CONV_X), col(COL_CONV_GB), col(COL_CONV_GC), col(COL_CONV_G), out_col,
                  pl.BlockSpec((3, LANE), lambda j: (0, j)), pl.BlockSpec((1, LANE), lambda j: (0, j))],
        out_specs=[out_col, out_col, out_col, out_col,
                   pl.BlockSpec((3, LANE), lambda j: (0, j)), pl.BlockSpec((1, LANE), lambda j: (0, j))],
        out_shape=[big, big, big, big,
                   jax.ShapeDtypeStruct((3, WIDTH), F32), jax.ShapeDtypeStruct((1, WIDTH), F32)],
    )(u, u, u, u, dy, conv_w, conv_b)


LOG2_E = 1.4426950408889634
LN_2 = 0.6931471805599453


def _sb_scores(q_h, k_blk, valid, later_mat, carry):
    z = _dot(q_h, k_blk, NT)
    neg_z = -z
    soft = jnp.log(1.0 + jnp.exp2(jnp.minimum(z, neg_z))) * LOG2_E
    log_keep = jnp.minimum(neg_z, 0.0) - soft
    log_beta = log_keep + z
    if valid is not None:
        log_keep = jnp.where(valid, log_keep, 0.0)
    later = _dot(log_keep.astype(BF16), later_mat, NN) + carry
    return log_keep, log_beta, later


def _masked(valid, x):
    return x if valid is None else jnp.where(valid, x, 0.0)


def _diagonal_masks(tq, tk):
    r = lax.broadcasted_iota(jnp.int32, (tq, tk), 0)
    cidx = lax.broadcasted_iota(jnp.int32, (tq, tk), 1)
    return [cidx + d * tk < r for d in range(tq // tk)]


def _triangle(tk, op):
    r = lax.broadcasted_iota(jnp.int32, (tk, tk), 0)
    cidx = lax.broadcasted_iota(jnp.int32, (tk, tk), 1)
    return op(r, cidx).astype(BF16)


def _split_refs(refs, n_in, n_out, n_scratch, rider):
    r_in = len(rider.inputs) if rider else 0
    r_out = len(rider.out_shape) if rider else 0
    a, b = n_in + r_in, n_in + r_in + n_out + r_out
    own = refs[:n_in] + refs[a:a + n_out] + refs[b:b + n_scratch]
    return own, (refs[n_in:a], refs[a + n_out:b], refs[b + n_scratch:])


def _rider_call_args(rider, n_in, n_out):
    if rider is None:
        return dict(in_specs=[], out_specs=[], out_shape=[], aliases={}, scratch=[], inputs=[])
    return dict(in_specs=[ANY] * len(rider.inputs), out_specs=[ANY] * len(rider.out_shape),
                out_shape=list(rider.out_shape), scratch=list(rider.scratch), inputs=list(rider.inputs),
                aliases={n_in + a: n_out + b for a, b in rider.aliases.items()})


def _ride(rider, phase, when, parts):
    fn = getattr(rider, phase) if rider else None
    if fn is not None:
        pl.when(when)(lambda: fn(*parts))


def _sb_fwd(u, name, t=512, tk=256, pairs=4, rider=None):
    s = u.shape[0]
    assert s // tk <= LANE and 4 % pairs == 0 and t % tk == 0
    scale = HEAD_DIM ** -0.5
    nh = 2 * pairs
    wide = pairs * LANE
    ratio = t // tk
    groups, nq = 4 // pairs, s // t

    def body(*refs):
        own, riding = _split_refs(refs, 4, 3, 4, rider)
        q_ref, k_ref, v_ref, g_ref, o_ref, y_ref, after_ref, kb_ref, vb_ref, acc_ref, carry_ref = own
        grp = pl.program_id(0)
        i = pl.program_id(1)
        _ride(rider, "start", jnp.logical_and(grp == 0, i == 0), riding)
        _ride(rider, "middle", jnp.logical_and(grp == groups - 1, i == (3 * nq) // 4), riding)

        @pl.when(i == 0)
        def _():
            kb_ref[...] = k_ref[...].astype(BF16)
            vb_ref[...] = v_ref[...].astype(BF16)

        lane = lax.broadcasted_iota(jnp.int32, (t, LANE), 1)
        first = lane < HEAD_DIM
        after_ref[...] = jnp.zeros_like(after_ref)
        qv = q_ref[...] * (scale * LOG2_E)
        q_heads = []
        for p in range(pairs):
            qp = qv[:, p * LANE:(p + 1) * LANE]
            q_heads += [jnp.where(first, qp, 0.0).astype(BF16), jnp.where(first, 0.0, qp).astype(BF16)]
        later_mat = _triangle(tk, lambda r, cidx: r > cidx)
        acc_ref[...] = jnp.zeros_like(acc_ref)
        carry_ref[...] = jnp.zeros_like(carry_ref)

        def block(kb, valid):
            rows = pl.ds(pl.multiple_of(kb * tk, tk), tk)
            k_blk = kb_ref[rows, :]
            v_blk = vb_ref[rows, :]
            carries = [carry_ref[h] for h in range(nh)]
            afters = [after_ref[:, h * LANE:(h + 1) * LANE] for h in range(nh)]
            accs = [acc_ref[h] for h in range(nh)]
            outs = []
            for h in range(nh):
                cols = slice((h // 2) * LANE, (h // 2 + 1) * LANE)
                log_keep, log_beta, later = _sb_scores(q_heads[h], k_blk[:, cols], valid, later_mat, carries[h])
                a = _masked(valid, jnp.exp2(log_beta + later))
                outs.append((accs[h] + _dot(a.astype(BF16), v_blk[:, cols], NN),
                             carries[h] + jnp.sum(log_keep, axis=1, keepdims=True),
                             jnp.where(lane == kb, carries[h], afters[h])))
            for h in range(nh):
                acc_ref[h] = outs[h][0]
                carry_ref[h] = outs[h][1]
                after_ref[:, h * LANE:(h + 1) * LANE] = outs[h][2]

        def step(j, _):
            block(ratio * i - 1 - j, None)
            return 0

        masks = _diagonal_masks(t, tk)
        for d in reversed(range(ratio)):
            block(ratio * i + d, masks[d])
        lax.fori_loop(0, ratio * i, step, 0)
        for p in range(pairs):
            cols = slice(p * LANE, (p + 1) * LANE)
            o = jnp.where(first, acc_ref[2 * p], acc_ref[2 * p + 1])
            o_ref[:, cols] = o
            gate = g_ref[:, cols]
            y_ref[:, cols] = (o * gate * _sigmoid(gate)).astype(BF16)
        _ride(rider, "finish", jnp.logical_and(grp == groups - 1, i == nq - 1), riding)

    blk = lambda base: pl.BlockSpec((t, wide), lambda g, i: (i, base // pairs + g))
    full = lambda base: pl.BlockSpec((s, wide), lambda g, i: (0, base // pairs + g))
    out_blk = pl.BlockSpec((t, wide), lambda g, i: (i, g))
    extra = _rider_call_args(rider, 4, 3)
    outs = pl.pallas_call(
        body, name=name, grid=(groups, nq),
        in_specs=[blk(COL_SB_Q), full(COL_SB_K), full(COL_SB_V), blk(COL_SB_G)] + extra["in_specs"],
        out_specs=[out_blk, out_blk, pl.BlockSpec((t, nh * LANE), lambda g, i: (i, g))] + extra["out_specs"],
        out_shape=[jax.ShapeDtypeStruct((s, WIDTH), F32), jax.ShapeDtypeStruct((s, WIDTH), BF16),
                   jax.ShapeDtypeStruct((s, 8 * LANE), F32)] + extra["out_shape"],
        input_output_aliases=extra["aliases"],
        scratch_shapes=[pltpu.VMEM((s, wide), BF16), pltpu.VMEM((s, wide), BF16),
                        pltpu.VMEM((nh, t, LANE), F32), pltpu.VMEM((nh, t, 1), F32)] + extra["scratch"],
        compiler_params=pltpu.CompilerParams(dimension_semantics=("arbitrary", "arbitrary")),
    )(u, u, u, u, *extra["inputs"])
    return outs[:3], outs[3:]


def _sb_bwd(u, o, after, dy, name, t=512, tk=256, pairs=2, rider=None):
    s = u.shape[0]
    nq = s // t
    scale = HEAD_DIM ** -0.5
    nh = 2 * pairs
    wide = pairs * LANE
    ratio = t // tk
    groups = 4 // pairs

    def body(*refs):
        own, riding = _split_refs(refs, 7, 4, 6, rider)
        (q_ref, k_ref, v_ref, g_ref, o_ref, after_ref, dy_ref, dq_ref, dk_ref, dv_ref, dg_ref,
         kb_ref, vb_ref, dk_acc, dv_acc, dq_acc, carry_ref) = own
        grp = pl.program_id(0)
        i = pl.program_id(1)
        _ride(rider, "start", jnp.logical_and(grp == 0, i == 0), riding)

        @pl.when(i == 0)
        def _():
            kb_ref[...] = k_ref[...].astype(BF16)
            vb_ref[...] = v_ref[...].astype(BF16)
            dk_acc[...] = jnp.zeros_like(dk_acc)
            dv_acc[...] = jnp.zeros_like(dv_acc)

        lane = lax.broadcasted_iota(jnp.int32, (t, LANE), 1)
        first = lane < HEAD_DIM
        gate = g_ref[...]
        silu, dsilu = _silu_and_grad(gate)
        dyv = dy_ref[...]
        do = dyv * silu
        dg_ref[...] = (dyv * o_ref[...] * dsilu).astype(BF16)
        qv = q_ref[...] * (scale * LOG2_E)
        do_heads, q_heads = [], []
        for p in range(pairs):
            cols = slice(p * LANE, (p + 1) * LANE)
            do_heads += [jnp.where(first, do[:, cols], 0.0).astype(BF16), jnp.where(first, 0.0, do[:, cols]).astype(BF16)]
            q_heads += [jnp.where(first, qv[:, cols], 0.0).astype(BF16), jnp.where(first, 0.0, qv[:, cols]).astype(BF16)]
        later_mat = _triangle(tk, lambda r, cidx: r > cidx)
        before_mat = _triangle(tk, lambda r, cidx: r < cidx)
        dq_acc[...] = jnp.zeros_like(dq_acc)
        carry_ref[...] = jnp.zeros_like(carry_ref)

        def block(kb, valid):
            rows = pl.ds(pl.multiple_of(kb * tk, tk), tk)
            k_blk = kb_ref[rows, :]
            v_blk = vb_ref[rows, :]
            carries = [carry_ref[h] for h in range(nh)]
            dq_old = [dq_acc[h] for h in range(nh)]
            dk_old = dk_acc[rows, :]
            dv_old = dv_acc[rows, :]
            outs = []
            for h in range(nh):
                cols = slice((h // 2) * LANE, (h // 2 + 1) * LANE)
                after = jnp.sum(jnp.where(lane == kb, after_ref[:, h * LANE:(h + 1) * LANE], 0.0), axis=1, keepdims=True)
                _, log_beta, later = _sb_scores(q_heads[h], k_blk[:, cols], valid, later_mat, after)
                beta = jnp.exp2(log_beta)
                a = _masked(valid, jnp.exp2(log_beta + later))
                da = _dot(do_heads[h], v_blk[:, cols], NT)
                gterm = a * da
                before = _dot(gterm.astype(BF16), before_mat, NN) + carries[h]
                dz_b = _masked(valid, gterm * (1.0 - beta) - beta * before).astype(BF16)
                outs.append((dq_old[h] + _dot(dz_b, k_blk[:, cols], NN), _dot(dz_b, q_heads[h], TN),
                             _dot(a.astype(BF16), do_heads[h], TN),
                             carries[h] + jnp.sum(gterm, axis=1, keepdims=True)))
            for h in range(nh):
                dq_acc[h] = outs[h][0]
                carry_ref[h] = outs[h][3]
            dk_new = [outs[2 * p][1] + outs[2 * p + 1][1] for p in range(pairs)]
            dv_new = [outs[2 * p][2] + outs[2 * p + 1][2] for p in range(pairs)]
            dk_acc[rows, :] = dk_old + (dk_new[0] if pairs == 1 else jnp.concatenate(dk_new, axis=1))
            dv_acc[rows, :] = dv_old + (dv_new[0] if pairs == 1 else jnp.concatenate(dv_new, axis=1))

        def step(kb, _):
            block(kb, None)
            return 0

        lax.fori_loop(0, ratio * i, step, 0)
        masks = _diagonal_masks(t, tk)
        for d in range(ratio):
            block(ratio * i + d, masks[d])
        for p in range(pairs):
            dq_ref[:, p * LANE:(p + 1) * LANE] = (jnp.where(first, dq_acc[2 * p], dq_acc[2 * p + 1]) * scale).astype(BF16)

        @pl.when(i == nq - 1)
        def _():
            dk_ref[...] = (dk_acc[...] * LN_2).astype(BF16)
            dv_ref[...] = dv_acc[...].astype(BF16)

        _ride(rider, "finish", jnp.logical_and(grp == groups - 1, i == nq - 1), riding)

    blk = lambda base: pl.BlockSpec((t, wide), lambda g, i: (i, base // pairs + g))
    full = lambda base: pl.BlockSpec((s, wide), lambda g, i: (0, base // pairs + g))
    out_blk = pl.BlockSpec((t, wide), lambda g, i: (i, g))
    out_full = pl.BlockSpec((s, wide), lambda g, i: (0, g))
    big = jax.ShapeDtypeStruct((s, WIDTH), BF16)
    extra = _rider_call_args(rider, 7, 4)
    outs = pl.pallas_call(
        body, name=name, grid=(groups, nq),
        in_specs=[blk(COL_SB_Q), full(COL_SB_K), full(COL_SB_V), blk(COL_SB_G), out_blk,
                  pl.BlockSpec((t, nh * LANE), lambda g, i: (i, g)), out_blk] + extra["in_specs"],
        out_specs=[out_blk, out_full, out_full, out_blk] + extra["out_specs"],
        out_shape=[big, big, big, big] + extra["out_shape"],
        input_output_aliases=extra["aliases"],
        scratch_shapes=[pltpu.VMEM((s, wide), BF16), pltpu.VMEM((s, wide), BF16),
                        pltpu.VMEM((s, wide), F32), pltpu.VMEM((s, wide), F32),
                        pltpu.VMEM((nh, t, LANE), F32), pltpu.VMEM((nh, t, 1), F32)] + extra["scratch"],
        compiler_params=pltpu.CompilerParams(dimension_semantics=("arbitrary", "arbitrary")),
    )(u, u, u, u, o, after, dy, *extra["inputs"])
    return outs[:4], outs[4:]


def _gate_fwd(u, projs, name, ts=256):
    s = u.shape[0]

    def body(m0, m1, m2, p0, p1, p2, out_ref):
        tot = None
        for m_ref, p_ref in ((m0, p0), (m1, p1), (m2, p2)):
            term = _sigmoid(m_ref[...]) * p_ref[...].astype(F32)
            tot = term if tot is None else tot + term
        out_ref[...] = tot.astype(BF16)

    mspec = lambda n: pl.BlockSpec((ts, D_MODEL), lambda i: (i, COL_MERGE_1024 + n))
    row = pl.BlockSpec((ts, D_MODEL), lambda i: (i, 0))
    return pl.pallas_call(
        body, name=name, grid=(s // ts,),
        in_specs=[mspec(0), mspec(1), mspec(2), row, row, row],
        out_specs=row, out_shape=jax.ShapeDtypeStruct((s, D_MODEL), BF16),
    )(u, u, u, *projs)


def _gate_bwd(u, projs, dmerged, name, ts=256):
    s = u.shape[0]

    def body(m0, m1, m2, p0, p1, p2, dm_ref, dp0, dp1, dp2, dl0, dl1, dl2):
        dm = dm_ref[...].astype(F32)
        for m_ref, p_ref, dp_ref, dl_ref in ((m0, p0, dp0, dl0), (m1, p1, dp1, dl1), (m2, p2, dp2, dl2)):
            gate = _sigmoid(m_ref[...])
            dp_ref[...] = (dm * gate).astype(BF16)
            dl_ref[...] = (dm * p_ref[...].astype(F32) * gate * (1.0 - gate)).astype(BF16)

    mspec = lambda n: pl.BlockSpec((ts, D_MODEL), lambda i: (i, COL_MERGE_1024 + n))
    row = pl.BlockSpec((ts, D_MODEL), lambda i: (i, 0))
    big = jax.ShapeDtypeStruct((s, D_MODEL), BF16)
    outs = pl.pallas_call(
        body, name=name, grid=(s // ts,),
        in_specs=[mspec(0), mspec(1), mspec(2), row, row, row, row],
        out_specs=[row] * 6, out_shape=[big] * 6,
    )(u, u, u, *projs, dmerged)
    return outs[:3], outs[3:]


def _as_rows(a):
    return a.reshape(-1, a.shape[-1])


def _row_tile(rows, cols, bytes_per_row_elem=4, cap=1 << 20):
    tr = rows
    while tr * cols * bytes_per_row_elem > cap and tr % 2 == 0 and (tr // 2) % 16 == 0:
        tr //= 2
    return tr


def _cast_bf16(a, name):
    a2 = _as_rows(a)
    rows, cols = a2.shape
    tr = _row_tile(rows, cols)

    def body(a_ref, o_ref):
        o_ref[...] = a_ref[...].astype(BF16)

    spec = pl.BlockSpec((tr, cols), lambda i: (i, 0))
    out = pl.pallas_call(body, name=name, grid=(rows // tr,), in_specs=[spec], out_specs=spec,
                         out_shape=jax.ShapeDtypeStruct((rows, cols), BF16))(a2)
    return out.reshape(a.shape)


def _adamw(w, g, m, v, name):
    shape = w.shape
    w2, g2, m2, v2 = (_as_rows(a) for a in (w, g, m, v))
    rows, cols = w2.shape
    tr = _row_tile(rows, cols)
    c1 = 1.0 - ADAM_B1 ** ADAM_STEP
    c2 = 1.0 - ADAM_B2 ** ADAM_STEP

    def body(w_ref, g_ref, m_ref, v_ref, d_ref, nm_ref, nv_ref):
        gv = g_ref[...]
        nm = ADAM_B1 * m_ref[...] + (1.0 - ADAM_B1) * gv
        nv = ADAM_B2 * v_ref[...] + (1.0 - ADAM_B2) * (gv * gv)
        nm_ref[...] = nm
        nv_ref[...] = nv
        d_ref[...] = -ADAM_LR * ((nm / c1) / (jnp.sqrt(nv / c2) + ADAM_EPS) + ADAM_WD * w_ref[...])

    spec = pl.BlockSpec((tr, cols), lambda i: (i, 0))
    sds = jax.ShapeDtypeStruct((rows, cols), F32)
    outs = pl.pallas_call(body, name=name, grid=(rows // tr,), in_specs=[spec] * 4, out_specs=[spec] * 3,
                          out_shape=[sds] * 3)(w2, g2, m2, v2)
    return tuple(o.reshape(shape) for o in outs)


def _sum_slots(a, out_dtype, name):
    n = a.shape[0]
    a3 = a.reshape(n, -1, a.shape[-1])
    _, rows, cols = a3.shape
    tr = _row_tile(rows, cols * n)

    def body(a_ref, o_ref):
        tot = a_ref[0].astype(F32)
        for k in range(1, n):
            tot = tot + a_ref[k].astype(F32)
        o_ref[...] = tot.astype(out_dtype)

    out = pl.pallas_call(
        body, name=name, grid=(rows // tr,),
        in_specs=[pl.BlockSpec((n, tr, cols), lambda i: (0, i, 0))],
        out_specs=pl.BlockSpec((tr, cols), lambda i: (i, 0)),
        out_shape=jax.ShapeDtypeStruct((rows, cols), out_dtype))(a3)
    return out.reshape(a.shape[1:])


def _chip_sum(own, recv, axis, core, name):
    half = recv.shape
    nd = len(half)
    last = nd - 1
    if axis == last:
        tl, nt = half[last], 1
    else:
        tl = min(half[last], 2048)
        nt = half[last] // tl
    block = half[:last] + (tl,)

    def own_index(i, core_ref):
        idx = [0] * nd
        idx[last] = i
        if axis == last:
            idx[last] = core_ref[0]
        else:
            idx[axis] = core_ref[0]
        return tuple(idx)

    def recv_index(i, core_ref):
        idx = [0] * nd
        idx[last] = i
        return tuple(idx)

    def body(core_ref, own_ref, recv_ref, o_ref):
        o_ref[...] = (own_ref[...] + recv_ref[...]).astype(BF16)

    return pl.pallas_call(
        body, name=name,
        grid_spec=pltpu.PrefetchScalarGridSpec(
            num_scalar_prefetch=1, grid=(nt,),
            in_specs=[pl.BlockSpec(block, own_index), pl.BlockSpec(block, recv_index)],
            out_specs=pl.BlockSpec(block, recv_index)),
        out_shape=jax.ShapeDtypeStruct(half, BF16),
    )(core, own, recv)


def _mesh_position():
    return lax.axis_index("x"), lax.axis_index("y"), lax.axis_index("c")


def _other_chips(x, y):
    return [(1 - x, y), (x, 1 - y), (1 - x, 1 - y)]


ALL_FLIPS = [(0, 0, 1), (1, 0, 0), (0, 1, 0), (1, 1, 0), (1, 0, 1), (0, 1, 1), (1, 1, 1)]


def _half(ref, axis, which, size):
    idx = [slice(None)] * len(ref.shape)
    idx[axis] = pl.ds(which * size, size)
    return ref.at[tuple(idx)]


def _sub(ref, picks):
    idx = [slice(None)] * len(ref.shape)
    for axis, start, size in picks:
        idx[axis] = pl.ds(start, size)
    return ref.at[tuple(idx)]


def _remote(src, dst, sems_send, sems_recv, k, to):
    return pltpu.make_async_remote_copy(src_ref=src, dst_ref=dst, send_sem=sems_send.at[k], recv_sem=sems_recv.at[k],
                                        device_id=to, device_id_type=MESH)


def _cast_shard(w, layer, shard_axis, pos, name, tr=512):
    shape = w.shape[1:]
    nd = len(shape)
    assert shard_axis in (nd - 1, nd - 2)
    rows, cols = shape[-2:]
    tr = min(tr, rows)
    nt = rows // tr
    lead = shape[:-2]
    full = list(shape)
    full[shard_axis] *= N_CHIPS
    block = (1,) * len(lead) + (tr, cols)

    def in_index(*args):
        return (layer, *args[:-1], 0)

    def out_index(*args):
        *g, pos_ref = args
        if shard_axis == nd - 1:
            return (*g, pos_ref[1])
        return (*g[:-1], pos_ref[1] * nt + g[-1], 0)

    def body(pos_ref, a_ref, o_ref):
        o_ref[...] = a_ref[...].astype(BF16)

    return pl.pallas_call(
        body, name=name,
        grid_spec=pltpu.PrefetchScalarGridSpec(
            num_scalar_prefetch=1, grid=lead + (nt,),
            in_specs=[pl.BlockSpec((None,) + block, in_index)], out_specs=pl.BlockSpec(block, out_index)),
        out_shape=jax.ShapeDtypeStruct(tuple(full), BF16),
    )(pos, w)


class _Rider:
    def __init__(self, inputs, out_shape, aliases, scratch, start, middle, finish):
        self.inputs, self.out_shape, self.aliases, self.scratch = inputs, out_shape, aliases, scratch
        self.start, self.middle, self.finish = start, middle, finish


def _weight_gather_rider(fulls, layout):
    n = len(fulls)

    def copies(outs, sems):
        send_sems, recv_sems = sems
        x, y, c = _mesh_position()
        chips = _other_chips(x, y)
        sibling = (x, y, 1 - c)
        mine = 2 * x + y

        def place(t, chip, core):
            sh_axis, sh_size, half_axis, half_size = layout[t]
            return _sub(outs[t], [(sh_axis, chip * sh_size, sh_size), (half_axis, core * half_size, half_size)])

        direct, arrive, forward, arrive_fwd = [], [], [], []
        for t in range(n):
            for k, (px, py) in enumerate(chips):
                theirs = 2 * px + py
                direct.append(_remote(place(t, mine, c), place(t, mine, c), send_sems, recv_sems, 6 * t + k, (px, py, c)))
                arrive.append(_remote(place(t, theirs, c), place(t, theirs, c), send_sems, recv_sems, 6 * t + k, (px, py, c)))
                forward.append(_remote(place(t, theirs, c), place(t, theirs, c), send_sems, recv_sems, 6 * t + 3 + k, sibling))
                arrive_fwd.append(_remote(place(t, theirs, 1 - c), place(t, theirs, 1 - c), send_sems, recv_sems,
                                          6 * t + 3 + k, sibling))
        return direct, arrive, forward, arrive_fwd

    def start(ins, outs, sems):
        for cp in copies(outs, sems)[0]:
            cp.start()

    def middle(ins, outs, sems):
        _, arrive, forward, _ = copies(outs, sems)
        for a, f in zip(arrive, forward):
            a.wait_recv()
            f.start()

    def finish(ins, outs, sems):
        direct, _, forward, arrive_fwd = copies(outs, sems)
        for cp in arrive_fwd:
            cp.wait_recv()
        for cp in direct + forward:
            cp.wait_send()

    return _Rider(list(fulls), [jax.ShapeDtypeStruct(a.shape, a.dtype) for a in fulls], {k: k for k in range(n)},
                  [pltpu.SemaphoreType.DMA((6 * n,)), pltpu.SemaphoreType.DMA((6 * n,))], start, middle, finish)


WEIGHT_LAYOUT = [(1, 2048, 0, 512), (2, 256, 1, 256), (0, 256, 1, 512)]


def _gather_weights(fulls, conv_w):
    rider = _weight_gather_rider(fulls, WEIGHT_LAYOUT)
    n = len(fulls)

    def body(*refs):
        cw, outs, cw_f = refs[n], refs[n + 1:2 * n + 1], refs[2 * n + 1]
        sems, (cw_send, cw_recv, local_sem) = refs[2 * n + 2:2 * n + 4], refs[2 * n + 4:]
        x, y, c = _mesh_position()
        chips = _other_chips(x, y)
        mine = 2 * x + y
        local = pltpu.make_async_copy(cw, cw_f.at[mine], local_sem.at[0])
        local.start()
        rider.start(None, outs, sems)
        small = [_remote(cw, cw_f.at[mine], cw_send, cw_recv, k, (*chip, c)) for k, chip in enumerate(chips)]
        for cp in small:
            cp.start()
        rider.middle(None, outs, sems)
        rider.finish(None, outs, sems)
        for k, (px, py) in enumerate(chips):
            _remote(cw, cw_f.at[2 * px + py], cw_send, cw_recv, k, (px, py, c)).wait_recv()
        for cp in small:
            cp.wait_send()
        local.wait()

    outs = pl.pallas_call(
        body, name="gather_weights",
        in_specs=[ANY] * (n + 1), out_specs=[ANY] * (n + 1),
        out_shape=rider.out_shape + [jax.ShapeDtypeStruct((N_CHIPS,) + conv_w.shape, F32)],
        input_output_aliases=rider.aliases,
        scratch_shapes=rider.scratch + [pltpu.SemaphoreType.DMA((3,)), pltpu.SemaphoreType.DMA((3,)),
                                        pltpu.SemaphoreType.DMA((1,))],
    )(*fulls, conv_w)
    return outs[:n], outs[n]


def _swap_halves(items, name):
    n = len(items)
    halves = []
    for a, axis in items:
        shp = list(a.shape)
        shp[axis] //= 2
        halves.append(tuple(shp))

    def body(*refs):
        srcs, dsts, (send_sems, recv_sems) = refs[:n], refs[n:2 * n], refs[2 * n:]
        x, y, c = _mesh_position()
        copies = []
        for k in range(n):
            axis = items[k][1]
            copies.append(_remote(_half(srcs[k], axis, 1 - c, halves[k][axis]), dsts[k], send_sems, recv_sems, k,
                                  (x, y, 1 - c)))
        for cp in copies:
            cp.start()
        for cp in copies:
            cp.wait()

    return pl.pallas_call(
        body, name=name, in_specs=[ANY] * n, out_specs=[ANY] * n,
        out_shape=[jax.ShapeDtypeStruct(h, a.dtype) for h, (a, _) in zip(halves, items)],
        scratch_shapes=[pltpu.SemaphoreType.DMA((n,)), pltpu.SemaphoreType.DMA((n,))],
    )(*[a for a, _ in items])


def _grad_exchange_rider(items):
    n = len(items)
    slices = []
    for a, axis in items:
        shp = list(a.shape)
        shp[axis] //= N_CHIPS
        slices.append(tuple(shp))

    def copies(ins, outs, sems):
        send_sems, recv_sems = sems
        x, y, c = _mesh_position()
        made = []
        for k in range(n):
            axis = items[k][1]
            for r, (px, py) in enumerate(_other_chips(x, y)):
                made.append(_remote(_half(ins[k], axis, 2 * px + py, slices[k][axis]), outs[k].at[r],
                                    send_sems, recv_sems, 3 * k + r, (px, py, c)))
        return made

    def start(ins, outs, sems):
        for cp in copies(ins, outs, sems):
            cp.start()

    def finish(ins, outs, sems):
        for cp in copies(ins, outs, sems):
            cp.wait()

    return _Rider([a for a, _ in items], [jax.ShapeDtypeStruct((N_CHIPS - 1,) + s, BF16) for s in slices], {},
                  [pltpu.SemaphoreType.DMA((3 * n,)), pltpu.SemaphoreType.DMA((3 * n,))], start, None, finish)


def _gather_small(small):
    def body(small_ref, small_all, send_sems, recv_sems, local_sem):
        x, y, c = _mesh_position()
        me = 4 * x + 2 * y + c
        local = pltpu.make_async_copy(small_ref, small_all.at[me], local_sem.at[0])
        local.start()
        copies = [_remote(small_ref, small_all.at[me], send_sems, recv_sems, r, (x ^ fx, y ^ fy, c ^ fc))
                  for r, (fx, fy, fc) in enumerate(ALL_FLIPS)]
        for cp in copies:
            cp.start()
        for cp in copies:
            cp.wait()
        local.wait()

    return pl.pallas_call(
        body, name="gather_small_grads", in_specs=[ANY], out_specs=ANY,
        out_shape=jax.ShapeDtypeStruct((2 * N_CHIPS,) + small.shape, F32),
        scratch_shapes=[pltpu.SemaphoreType.DMA((len(ALL_FLIPS),)), pltpu.SemaphoreType.DMA((len(ALL_FLIPS),)),
                        pltpu.SemaphoreType.DMA((1,))],
    )(small)


def _sum_chips(recv, own, shard_axis, split_axis, pos, dest, layer, name, tr=128):
    sl = recv.shape[1:]
    nd = len(sl)
    tiled = nd == 2 and sl[0] > tr
    nt = sl[0] // tr if tiled else 1
    block = ((tr,) + sl[1:]) if tiled else sl
    shard = list(sl)
    shard[split_axis] *= 2

    def recv_index(i, pos_ref):
        return (0, i) + (0,) * (nd - 1) if tiled else (0,) * (nd + 1)

    def own_index(i, pos_ref):
        idx = [0] * nd
        idx[shard_axis] = pos_ref[1]
        if tiled:
            idx[0] = pos_ref[1] * nt + i if shard_axis == 0 else i
        return tuple(idx)

    def out_index(i, pos_ref):
        idx = [0] * nd
        idx[split_axis] = pos_ref[0]
        if tiled:
            idx[0] = pos_ref[0] * nt + i if split_axis == 0 else i
        return (layer, *idx)

    def body(pos_ref, recv_ref, own_ref, *rest):
        o_ref = rest[-1]
        tot = own_ref[...].astype(F32)
        for k in range(N_CHIPS - 1):
            tot = tot + recv_ref[k].astype(F32)
        o_ref[0] = tot

    in_specs = [pl.BlockSpec((N_CHIPS - 1,) + block, recv_index), pl.BlockSpec(block, own_index)]
    args = [pos, recv, own]
    aliases = {}
    if dest is not None:
        in_specs.append(ANY)
        args.append(dest)
        aliases = {3: 0}
    return pl.pallas_call(
        body, name=name,
        grid_spec=pltpu.PrefetchScalarGridSpec(
            num_scalar_prefetch=1, grid=(nt,), in_specs=in_specs,
            out_specs=pl.BlockSpec((1,) + block, out_index)),
        out_shape=jax.ShapeDtypeStruct((DEPTH,) + tuple(shard), F32),
        input_output_aliases=aliases,
    )(*args)


def _share_halves(bufs, name):
    n = len(bufs)

    def body(*refs):
        outs, (send_sems, recv_sems) = refs[n:2 * n], refs[2 * n:]
        x, y, c = _mesh_position()
        copies = []
        for k, (a, axis) in enumerate(bufs):
            size = a.shape[1 + axis] // 2
            mine = _half(outs[k], 1 + axis, c, size)
            copies.append(_remote(mine, mine, send_sems, recv_sems, k, (x, y, 1 - c)))
        for cp in copies:
            cp.start()
        for cp in copies:
            cp.wait()

    return pl.pallas_call(
        body, name=name, in_specs=[ANY] * n, out_specs=[ANY] * n,
        out_shape=[jax.ShapeDtypeStruct(a.shape, F32) for a, _ in bufs],
        input_output_aliases={k: k for k in range(n)},
        scratch_shapes=[pltpu.SemaphoreType.DMA((n,)), pltpu.SemaphoreType.DMA((n,))],
    )(*[a for a, _ in bufs])


def _layer_fwd(x, p, l, rider=None):
    tag = f"l{l}_"
    h = _rms_fwd(x, p["pre_g"], tag + "pre_norm")
    u = _matmul(h, p["w_in"], "nn", F32, tag + "in_proj")
    y_pool = _pool_fwd(u, p["pool_w"], p["pool_scale"], tag + "pool")
    y_conv = _conv_fwd(u, p["conv_w"], p["conv_b"], tag + "conv")
    (o_sb, y_sb, sb_after), carried = _sb_fwd(u, tag + "stickbreak", rider=rider)
    ys = [y_pool, y_conv, y_sb]
    projs = [_matmul(ys[n], p["w_branch"], "nn", BF16, tag + f"branch_proj{n}", b_lead=(n,)) for n in range(3)]
    merged = _gate_fwd(u, projs, tag + "merge")
    out = _matmul(merged, p["w_out"], "nn", F32, tag + "out_proj")
    saved = dict(x=x, h=h, u=u, ys=ys, o_sb=o_sb, sb_after=sb_after, projs=projs, merged=merged, out=out)
    return out, saved, carried


def _layer_bwd(dy, p, saved, l, early=None, late=None):
    tag = f"l{l}_bwd_"
    u = saved["u"]
    d_out, g_post = _rms_bwd(saved["out"], p["post_g"], dy, None, BF16, tag + "post_norm")
    d_merged = _matmul(d_out, p["w_out"], "nt", BF16, tag + "out_proj_dx")
    g_w_out = _matmul(saved["merged"], d_out, "tn", F32, tag + "out_proj_dw", tk=2048)
    d_projs, d_logits = _gate_bwd(u, saved["projs"], d_merged, tag + "merge")
    g_w_branch = jnp.stack([_matmul(saved["ys"][n], d_projs[n], "tn", F32, tag + f"branch_dw{n}", tk=2048)
                            for n in range(3)])
    rider = early(g_w_branch, g_w_out) if early else None
    d_ys = [_matmul(d_projs[n], p["w_branch"], "nt", F32, tag + f"branch_dx{n}", b_lead=(n,)) for n in range(3)]
    d_pv, d_pg, g_pool_w, g_pool_scale = _pool_bwd(u, d_ys[0], p["pool_w"], p["pool_scale"], tag + "pool")
    d_cx, d_cgb, d_cgc, d_cg, g_conv_w, g_conv_b = _conv_bwd(u, d_ys[1], p["conv_w"], p["conv_b"], tag + "conv")
    (d_q, d_k, d_v, d_sg), carried_attn = _sb_bwd(u, saved["o_sb"], saved["sb_after"], d_ys[2], tag + "stickbreak",
                                                  rider=rider)
    du = jnp.concatenate([d_pv, d_pg, d_cx, d_cgb, d_cgc, d_cg, d_q, d_k, d_v, d_sg] + list(d_logits), axis=1)
    g_w_in = _matmul(saved["h"], du, "tn", F32, tag + "in_proj_dw", tk=2048)
    rider = late(g_w_in) if late else None
    dh = _matmul(du, p["w_in"], "nt", BF16, tag + "in_proj_dx", tk=2048, rider=rider)
    dh, carried_dx = dh if rider else (dh, [])
    dx, g_pre = _rms_bwd(saved["x"], p["pre_g"], dh, dy, F32, tag + "pre_norm")
    grads = dict(w_in=g_w_in, w_branch=g_w_branch, w_out=g_w_out, pre_g=g_pre, post_g=g_post,
                 pool_w=g_pool_w, pool_scale=g_pool_scale, conv_w=g_conv_w, conv_b=g_conv_b)
    return dx, grads, carried_attn, carried_dx


SMALL_ORDER = ["pre_g", "pool_w", "pool_scale", "conv_w", "conv_b", "post_g"]


def _pack_small(per_layer):
    parts, spans, at = [], {}, 0
    for name in SMALL_ORDER:
        a = jnp.stack([per_layer[l][name] for l in range(DEPTH)]).reshape(-1, LANE)
        parts.append(a)
        spans[name] = (at, a.shape[0])
        at += a.shape[0]
    return jnp.concatenate(parts, axis=0), spans


def kernel(x, pre_norm_g, w_in, pool_w, pool_scale, conv_w, conv_b, w_branch, w_out, post_norm_g, loss_target, m_pre_norm_g, m_w_in, m_pool_w, m_pool_scale, m_conv_w, m_conv_b, m_w_branch, m_w_out, m_post_norm_g, v_pre_norm_g, v_w_in, v_pool_w, v_pool_scale, v_conv_w, v_conv_b, v_w_branch, v_w_out, v_post_norm_g):
    mx, my, mc = _mesh_position()
    chip = 2 * mx + my
    core = mc.astype(jnp.int32).reshape(1)
    pos = jnp.stack([mc, chip]).astype(jnp.int32)

    names = ["w_in", "w_branch", "w_out"]
    given = dict(w_in=w_in, w_branch=w_branch, w_out=w_out)
    in_place = [[_cast_shard(given[n], l, WEIGHT_LAYOUT[i][0], pos, f"cast_{n}{l}") for i, n in enumerate(names)]
                for l in range(DEPTH)]
    gathered, conv_w_by_chip = _gather_weights(in_place[0], conv_w)
    conv_w_f = conv_w_by_chip.transpose(1, 2, 0, 3).reshape(DEPTH, 3, WIDTH)
    pool_w_b = _cast_bf16(pool_w, "cast_pool_w")

    def layer_params(l, big):
        return dict(pre_g=pre_norm_g[l:l + 1], post_g=post_norm_g[l:l + 1], w_in=big[0], w_branch=big[1],
                    w_out=big[2], pool_w=pool_w_b[l], pool_scale=pool_scale[l:l + 1], conv_w=conv_w_f[l],
                    conv_b=conv_b[l:l + 1])

    act = x[0]
    params, saved = [], []
    for l in range(DEPTH):
        params.append(layer_params(l, gathered))
        rider = _weight_gather_rider(in_place[l + 1], WEIGHT_LAYOUT) if l + 1 < DEPTH else None
        out, sv, gathered = _layer_fwd(act, params[l], l, rider)
        saved.append(sv)
        if l < DEPTH - 1:
            act = _resid_out(act, out, params[l]["post_g"], None, f"l{l}_resid")
    dy, loss_part = _resid_out(act, saved[-1]["out"], params[-1]["post_g"], loss_target[0], "loss_head")
    loss = lax.psum(loss_part[0, 0], ("x", "y", "c"))

    split_axis = dict(w_in=0, w_branch=1, w_out=1)
    shard_axis = dict(w_in=1, w_branch=2, w_out=0)
    grads = [None] * DEPTH
    chip_sums = [dict() for _ in range(DEPTH)]
    by_chip = [dict() for _ in range(DEPTH)]

    def reduce_in_chip(l, which, g):
        items = [(g[n], split_axis[n]) for n in which]
        from_sibling = _swap_halves(items, f"swap_grad_halves{l}_{which[0]}")
        for n, (a, axis), r in zip(which, items, from_sibling):
            chip_sums[l][n] = _chip_sum(a, r, axis, core, f"chip_sum{l}_{n}")

    def exchange_rider(keys):
        return _grad_exchange_rider([(chip_sums[l][n], shard_axis[n]) for l, n in keys])

    waiting = []
    for l in reversed(range(DEPTH)):
        sent_early, sent_late = list(waiting) + [(l, "w_branch"), (l, "w_out")], [(l, "w_in")]

        def early(g_w_branch, g_w_out, l=l, keys=sent_early):
            reduce_in_chip(l, ["w_branch", "w_out"], dict(w_branch=g_w_branch, w_out=g_w_out))
            return exchange_rider(keys)

        def late(g_w_in, l=l, keys=sent_late):
            reduce_in_chip(l, ["w_in"], dict(w_in=g_w_in))
            return exchange_rider(keys)

        if l == DEPTH - 1:
            dy, grads[l], _, _ = _layer_bwd(dy, params[l], saved[l], l)
            reduce_in_chip(l, names, grads[l])
            waiting = [(l, n) for n in names]
        else:
            dy, grads[l], got_early, got_late = _layer_bwd(dy, params[l], saved[l], l, early, late)
            for (ll, n), r in zip(sent_early + sent_late, list(got_early) + list(got_late)):
                by_chip[ll][n] = r
            waiting = []
    grad_x = dy[None]
    small_part, spans = _pack_small(grads)
    small_all = _gather_small(small_part)
    bufs = []
    for n in names:
        dest = None
        for l in range(DEPTH):
            dest = _sum_chips(by_chip[l][n], chip_sums[l][n], shard_axis[n], split_axis[n], pos, dest, l,
                              f"sum_chips{l}_{n}")
        bufs.append((dest, split_axis[n]))
    g_w_in, g_w_branch, g_w_out = _share_halves(bufs, "share_grad_halves")

    small_sum = _sum_slots(small_all, F32, "sum_small")
    small = {}
    for name, like in (("pre_g", pre_norm_g), ("pool_w", pool_w), ("pool_scale", pool_scale), ("conv_b", conv_b),
                       ("post_g", post_norm_g)):
        at, n = spans[name]
        small[name] = small_sum[at:at + n].reshape(like.shape)
    at, n = spans["conv_w"]
    g_conv_w_full = small_sum[at:at + n].reshape(DEPTH, 3, WIDTH)
    g_conv_w = lax.dynamic_slice_in_dim(g_conv_w_full, chip * conv_w.shape[2], conv_w.shape[2], axis=2)

    g = dict(pre_norm_g=small["pre_g"], w_in=g_w_in, pool_w=small["pool_w"], pool_scale=small["pool_scale"],
             conv_w=g_conv_w, conv_b=small["conv_b"], w_branch=g_w_branch, w_out=g_w_out, post_norm_g=small["post_g"])
    w = dict(pre_norm_g=pre_norm_g, w_in=w_in, pool_w=pool_w, pool_scale=pool_scale, conv_w=conv_w, conv_b=conv_b,
             w_branch=w_branch, w_out=w_out, post_norm_g=post_norm_g)
    m = dict(pre_norm_g=m_pre_norm_g, w_in=m_w_in, pool_w=m_pool_w, pool_scale=m_pool_scale, conv_w=m_conv_w,
             conv_b=m_conv_b, w_branch=m_w_branch, w_out=m_w_out, post_norm_g=m_post_norm_g)
    v = dict(pre_norm_g=v_pre_norm_g, w_in=v_w_in, pool_w=v_pool_w, pool_scale=v_pool_scale, conv_w=v_conv_w,
             conv_b=v_conv_b, w_branch=v_w_branch, w_out=v_w_out, post_norm_g=v_post_norm_g)
    order = ["pre_norm_g", "w_in", "pool_w", "pool_scale", "conv_w", "conv_b", "w_branch", "w_out", "post_norm_g"]
    upd = {n: _adamw(w[n], g[n], m[n], v[n], "adamw_" + n) for n in order}
    return (loss, grad_x, *[g[n] for n in order], *[upd[n][0] for n in order], *[upd[n][1] for n in order],
            *[upd[n][2] for n in order])
```

```python
import functools

import jax
import jax.numpy as jnp
from jax import lax
from jax.experimental import pallas as pl
from jax.experimental.pallas import tpu as pltpu

F32 = jnp.float32
BF16 = jnp.bfloat16
MESH = pl.DeviceIdType.MESH
ANY = pl.BlockSpec(memory_space=pl.ANY)

DEPTH = 2
D_MODEL = 1024
WIDTH = 512
N_IN = 8192
N_CHIPS = 4
HEAD_DIM = 64
RMS_EPS = 1e-6
POOL_HALO = 16
CONV_HALO = 8
LANE = 128
COL_POOL_V, COL_POOL_G = 0, 4
COL_CONV_X, COL_CONV_GB, COL_CONV_GC, COL_CONV_G = 8, 12, 16, 20
COL_SB_Q, COL_SB_K, COL_SB_V, COL_SB_G = 24, 28, 32, 36
COL_MERGE_1024 = 5

ADAM_LR, ADAM_B1, ADAM_B2, ADAM_EPS, ADAM_WD, ADAM_STEP = 0.001, 0.9, 0.999, 1e-08, 0.01, 10

NN = (((1,), (0,)), ((), ()))
NT = (((1,), (1,)), ((), ()))
TN = (((0,), (0,)), ((), ()))


def _sigmoid(x):
    return 1.0 / (1.0 + jnp.exp(-x))


def _silu_and_grad(x):
    s = _sigmoid(x)
    return x * s, s * (1.0 + x * (1.0 - s))


def _dot(a, b, dims):
    return lax.dot_general(a, b, dims, preferred_element_type=F32)


def _matmul(a, b, mode, out_dtype, name, tm=1024, tn=1024, tk=1024, b_lead=(), rider=None):
    b_shape = b.shape[len(b_lead):]
    if mode == "nn":
        (m, k), (k2, n) = a.shape, b_shape
    elif mode == "nt":
        (m, k), (n, k2) = a.shape, b_shape
    else:
        (k, m), (k2, n) = a.shape, b_shape
    assert k == k2 and a.dtype == BF16 and b.dtype == BF16
    tm, tn, tk = min(tm, m), min(tn, n), min(tk, k)
    assert m % tm == 0 and n % tn == 0 and k % tk == 0
    nk = k // tk
    dims = {"nn": NN, "nt": NT, "tn": TN}[mode]

    grid = (m // tm, n // tn, nk)

    def at_step(which):
        return functools.reduce(jnp.logical_and, [pl.program_id(d) == (g - 1 if which else 0) for d, g in enumerate(grid)])

    def body(*refs):
        (a_ref, b_ref, o_ref, *scratch), riding = _split_refs(refs, 2, 1, 1 if nk > 1 else 0, rider)
        _ride(rider, "start", at_step(0), riding)
        compute(a_ref, b_ref, o_ref, scratch)
        _ride(rider, "finish", at_step(1), riding)

    def compute(a_ref, b_ref, o_ref, scratch):
        p = _dot(a_ref[...], b_ref[...], dims)
        if nk == 1:
            o_ref[...] = p.astype(o_ref.dtype)
        else:
            acc = scratch[0]
            kk = pl.program_id(2)

            @pl.when(kk == 0)
            def _():
                acc[...] = p

            @pl.when(jnp.logical_and(kk > 0, kk < nk - 1))
            def _():
                acc[...] += p

            @pl.when(kk == nk - 1)
            def _():
                o_ref[...] = (acc[...] + p).astype(o_ref.dtype)

    if mode == "tn":
        a_spec = pl.BlockSpec((tk, tm), lambda i, j, kk: (kk, i))
    else:
        a_spec = pl.BlockSpec((tm, tk), lambda i, j, kk: (i, kk))
    squeezed = (None,) * len(b_lead)
    if mode == "nt":
        b_spec = pl.BlockSpec(squeezed + (tn, tk), lambda i, j, kk: (*b_lead, j, kk))
    else:
        b_spec = pl.BlockSpec(squeezed + (tk, tn), lambda i, j, kk: (*b_lead, kk, j))
    extra = _rider_call_args(rider, 2, 1)
    outs = pl.pallas_call(
        body, name=name, grid=grid,
        in_specs=[a_spec, b_spec] + extra["in_specs"],
        out_specs=[pl.BlockSpec((tm, tn), lambda i, j, kk: (i, j))] + extra["out_specs"],
        out_shape=[jax.ShapeDtypeStruct((m, n), out_dtype)] + extra["out_shape"],
        input_output_aliases=extra["aliases"],
        scratch_shapes=([pltpu.VMEM((tm, tn), F32)] if nk > 1 else []) + extra["scratch"],
        compiler_params=pltpu.CompilerParams(dimension_semantics=("arbitrary",) * 3 if rider else
                                             ("parallel", "parallel", "arbitrary")),
    )(a, b, *extra["inputs"])
    return (outs[0], outs[1:]) if rider else outs[0]


def _rms_fwd(x, g, name, ts=512):
    s, d = x.shape

    def body(x_ref, g_ref, h_ref):
        xv = x_ref[...]
        r = lax.rsqrt(jnp.mean(xv * xv, axis=-1, keepdims=True) + RMS_EPS)
        h_ref[...] = (xv * r * g_ref[...]).astype(BF16)

    return pl.pallas_call(
        body, name=name, grid=(s // ts,),
        in_specs=[pl.BlockSpec((ts, d), lambda i: (i, 0)), pl.BlockSpec((1, d), lambda i: (0, 0))],
        out_specs=pl.BlockSpec((ts, d), lambda i: (i, 0)),
        out_shape=jax.ShapeDtypeStruct((s, d), BF16),
    )(x, g)


def _rms_bwd(xin, g, dh, resid, out_dtype, name, ts=512):
    s, d = xin.shape
    has_resid = resid is not None

    def body(*refs):
        if has_resid:
            x_ref, g_ref, dh_ref, res_ref, dx_ref, dg_ref = refs
        else:
            x_ref, g_ref, dh_ref, dx_ref, dg_ref = refs
        xv = x_ref[...]
        dhv = dh_ref[...].astype(F32)
        r = lax.rsqrt(jnp.mean(xv * xv, axis=-1, keepdims=True) + RMS_EPS)
        nrm = xv * r
        dn = dhv * g_ref[...]
        dx = r * (dn - nrm * jnp.mean(dn * nrm, axis=-1, keepdims=True))
        if has_resid:
            dx = dx + res_ref[...]
        dx_ref[...] = dx.astype(dx_ref.dtype)
        part = jnp.sum(dhv * nrm, axis=0, keepdims=True)

        @pl.when(pl.program_id(0) == 0)
        def _():
            dg_ref[...] = part

        @pl.when(pl.program_id(0) > 0)
        def _():
            dg_ref[...] += part

    row = pl.BlockSpec((ts, d), lambda i: (i, 0))
    vec = pl.BlockSpec((1, d), lambda i: (0, 0))
    ins = [xin, g, dh] + ([resid] if has_resid else [])
    return pl.pallas_call(
        body, name=name, grid=(s // ts,),
        in_specs=[row, vec, row] + ([row] if has_resid else []),
        out_specs=[row, vec],
        out_shape=[jax.ShapeDtypeStruct((s, d), out_dtype), jax.ShapeDtypeStruct((1, d), F32)],
        compiler_params=pltpu.CompilerParams(dimension_semantics=("arbitrary",)),
    )(*ins)


def _resid_out(x, out, g, target, name, ts=512):
    s, d = x.shape
    has_loss = target is not None

    def body(*refs):
        if has_loss:
            x_ref, o_ref, g_ref, t_ref, dy_ref, loss_ref = refs
        else:
            x_ref, o_ref, g_ref, y_ref = refs
        ov = o_ref[...]
        r = lax.rsqrt(jnp.mean(ov * ov, axis=-1, keepdims=True) + RMS_EPS)
        yv = x_ref[...] + ov * r * g_ref[...]
        if not has_loss:
            y_ref[...] = yv
            return
        err = yv - t_ref[...]
        dy_ref[...] = err * (1.0 / d)
        part = jnp.sum(jnp.sum(err * err, axis=-1, keepdims=True), axis=0, keepdims=True) * (0.5 / d)
        part = jnp.broadcast_to(part, (1, LANE))

        @pl.when(pl.program_id(0) == 0)
        def _():
            loss_ref[...] = part

        @pl.when(pl.program_id(0) > 0)
        def _():
            loss_ref[...] += part

    row = pl.BlockSpec((ts, d), lambda i: (i, 0))
    vec = pl.BlockSpec((1, d), lambda i: (0, 0))
    if has_loss:
        return pl.pallas_call(
            body, name=name, grid=(s // ts,),
            in_specs=[row, row, vec, row],
            out_specs=[row, pl.BlockSpec((1, LANE), lambda i: (0, 0))],
            out_shape=[jax.ShapeDtypeStruct((s, d), F32), jax.ShapeDtypeStruct((1, LANE), F32)],
            compiler_params=pltpu.CompilerParams(dimension_semantics=("arbitrary",)),
        )(x, out, g, target)
    return pl.pallas_call(
        body, name=name, grid=(s // ts,),
        in_specs=[row, row, vec], out_specs=row,
        out_shape=jax.ShapeDtypeStruct((s, d), F32),
    )(x, out, g)


def _rows_before(ref, start, n, halo):
    if start == 0:
        return jnp.concatenate([jnp.zeros((halo, ref.shape[1]), F32), ref[0:n, :]], axis=0)
    return ref[start - halo:start + n, :]


def _rows_after(ref, start, n, halo):
    if start + n == ref.shape[0]:
        return jnp.concatenate([ref[start:start + n, :].astype(F32), jnp.zeros((halo, ref.shape[1]), F32)], axis=0)
    return ref[start:start + n + halo, :].astype(F32)


def _pick_window(group, s2, s4, s8, s16):
    return jnp.where(group == 0, s2, jnp.where(group == 1, s4, jnp.where(group == 2, s8, s16)))


def _trailing_sums(ext, group):
    s2 = ext + pltpu.roll(ext, 1, 0)
    s4 = s2 + pltpu.roll(s2, 2, 0)
    s8 = s4 + pltpu.roll(s4, 4, 0)
    s16 = s8 + pltpu.roll(s8, 8, 0)
    return _pick_window(group, s2, s4, s8, s16)


def _leading_sums(ext, group):
    n = ext.shape[0]
    s2 = ext + pltpu.roll(ext, n - 1, 0)
    s4 = s2 + pltpu.roll(s2, n - 2, 0)
    s8 = s4 + pltpu.roll(s4, n - 4, 0)
    s16 = s8 + pltpu.roll(s8, n - 8, 0)
    return _pick_window(group, s2, s4, s8, s16)


def _window_count(start, n, group):
    pos = start + lax.broadcasted_iota(jnp.int32, (n, LANE), 0)
    return jnp.minimum(pos + 1, 2 << group).astype(F32)


def _pooled(v_ref, start, n, group):
    ext = _rows_before(v_ref, start, n, POOL_HALO)
    sums = _trailing_sums(ext, group)[POOL_HALO:, :]
    return sums / _window_count(start, n, group) - ext[POOL_HALO:, :]


def _pool_fwd(u, pool_w, pool_scale, name, ts=512):
    s = u.shape[0]

    def body(v_ref, gate_ref, w_ref, sc_ref, y_ref):
        group = pl.program_id(0)
        for c in range(s // ts):
            a = c * ts
            pooled = _pooled(v_ref, a, ts, group)
            mixed = _dot(pooled.astype(BF16), w_ref[...], NN)
            gate = gate_ref[a:a + ts, :]
            y_ref[a:a + ts, :] = (mixed * sc_ref[...] * (gate * _sigmoid(gate))).astype(BF16)

    col = lambda base: pl.BlockSpec((s, LANE), lambda g: (0, base + g))
    return pl.pallas_call(
        body, name=name, grid=(4,),
        in_specs=[col(COL_POOL_V), col(COL_POOL_G),
                  pl.BlockSpec((None, LANE, LANE), lambda g: (g, 0, 0)),
                  pl.BlockSpec((1, LANE), lambda g: (0, g))],
        out_specs=pl.BlockSpec((s, LANE), lambda g: (0, g)),
        out_shape=jax.ShapeDtypeStruct((s, WIDTH), BF16),
    )(u, u, pool_w, pool_scale)


def _pool_bwd(u, dy, pool_w, pool_scale, name, ts=512):
    s = u.shape[0]

    def body(v_ref, gate_ref, dy_ref, w_ref, sc_ref, dv_ref, dgate_ref, dw_ref, dsc_ref):
        group = pl.program_id(0)
        w = w_ref[...]
        scale = sc_ref[...]
        dw = jnp.zeros((LANE, LANE), F32)
        dsc = jnp.zeros((1, LANE), F32)
        for c in range(s // ts):
            a = c * ts
            n_ext = ts + POOL_HALO
            gate_e = _rows_after(gate_ref, a, ts, POOL_HALO)
            dy_e = _rows_after(dy_ref, a, ts, POOL_HALO)
            silu_e, dsilu_e = _silu_and_grad(gate_e)
            dms_e = dy_e * silu_e
            dm_e = (dms_e * scale).astype(BF16)
            dpool_e = _dot(dm_e, w, NT)
            spread = _leading_sums(dpool_e / _window_count(a, n_ext, group), group)
            dv_ref[a:a + ts, :] = (spread[0:ts, :] - dpool_e[0:ts, :]).astype(BF16)
            pooled = _pooled(v_ref, a, ts, group).astype(BF16)
            mixed = _dot(pooled, w, NN)
            dgate_ref[a:a + ts, :] = (dy_e[0:ts, :] * mixed * scale * dsilu_e[0:ts, :]).astype(BF16)
            dsc = dsc + jnp.sum(dms_e[0:ts, :] * mixed, axis=0, keepdims=True)
            dw = dw + _dot(pooled, dm_e[0:ts, :], TN)
        dw_ref[...] = dw
        dsc_ref[...] = dsc

    col = lambda base: pl.BlockSpec((s, LANE), lambda g: (0, base + g))
    out_col = pl.BlockSpec((s, LANE), lambda g: (0, g))
    return pl.pallas_call(
        body, name=name, grid=(4,),
        in_specs=[col(COL_POOL_V), col(COL_POOL_G), out_col,
                  pl.BlockSpec((None, LANE, LANE), lambda g: (g, 0, 0)),
                  pl.BlockSpec((1, LANE), lambda g: (0, g))],
        out_specs=[out_col, out_col,
                   pl.BlockSpec((None, LANE, LANE), lambda g: (g, 0, 0)),
                   pl.BlockSpec((1, LANE), lambda g: (0, g))],
        out_shape=[jax.ShapeDtypeStruct((s, WIDTH), BF16), jax.ShapeDtypeStruct((s, WIDTH), BF16),
                   jax.ShapeDtypeStruct((4, LANE, LANE), F32), jax.ShapeDtypeStruct((1, WIDTH), F32)],
    )(u, u, dy, pool_w, pool_scale)


def _conv_taps(x_ref, gc_ref, start, n):
    z_ext = _rows_before(gc_ref, start, n, CONV_HALO) * _rows_before(x_ref, start, n, CONV_HALO)
    z0 = z_ext[CONV_HALO:, :]
    z1 = pltpu.roll(z_ext, 1, 0)[CONV_HALO:, :]
    z2 = pltpu.roll(z_ext, 2, 0)[CONV_HALO:, :]
    return z0, z1, z2


def _conv_fwd(u, conv_w, conv_b, name, ts=512):
    s = u.shape[0]

    def body(x_ref, gb_ref, gc_ref, g_ref, w_ref, b_ref, y_ref):
        w0, w1, w2 = w_ref[0:1, :], w_ref[1:2, :], w_ref[2:3, :]
        for c in range(s // ts):
            a = c * ts
            z0, z1, z2 = _conv_taps(x_ref, gc_ref, a, ts)
            y = w2 * z0 + w1 * z1 + w0 * z2 + b_ref[...]
            gate = g_ref[a:a + ts, :]
            y_ref[a:a + ts, :] = (gb_ref[a:a + ts, :] * y * (gate * _sigmoid(gate))).astype(BF16)

    col = lambda base: pl.BlockSpec((s, LANE), lambda j: (0, base + j))
    return pl.pallas_call(
        body, name=name, grid=(4,),
        in_specs=[col(COL_CONV_X), col(COL_CONV_GB), col(COL_CONV_GC), col(COL_CONV_G),
                  pl.BlockSpec((3, LANE), lambda j: (0, j)), pl.BlockSpec((1, LANE), lambda j: (0, j))],
        out_specs=pl.BlockSpec((s, LANE), lambda j: (0, j)),
        out_shape=jax.ShapeDtypeStruct((s, WIDTH), BF16),
    )(u, u, u, u, conv_w, conv_b)


def _conv_bwd(u, dy, conv_w, conv_b, name, ts=512):
    s = u.shape[0]

    def body(x_ref, gb_ref, gc_ref, g_ref, dy_ref, w_ref, b_ref,
             dx_ref, dgb_ref, dgc_ref, dg_ref, dw_ref, db_ref):
        w0, w1, w2 = w_ref[0:1, :], w_ref[1:2, :], w_ref[2:3, :]
        acc = [jnp.zeros((1, LANE), F32) for _ in range(4)]
        for c in range(s // ts):
            a = c * ts
            n_ext = ts + CONV_HALO
            gate_e = _rows_after(g_ref, a, ts, CONV_HALO)
            silu_e, dsilu_e = _silu_and_grad(gate_e)
            dy_e = _rows_after(dy_ref, a, ts, CONV_HALO)
            gb_e = _rows_after(gb_ref, a, ts, CONV_HALO)
            dyy_e = dy_e * silu_e * gb_e
            dz = (w2 * dyy_e + w1 * pltpu.roll(dyy_e, n_ext - 1, 0) + w0 * pltpu.roll(dyy_e, n_ext - 2, 0))[0:ts, :]
            z0, z1, z2 = _conv_taps(x_ref, gc_ref, a, ts)
            yb = w2 * z0 + w1 * z1 + w0 * z2 + b_ref[...]
            dyv = dy_e[0:ts, :]
            dyy = dyy_e[0:ts, :]
            dg_ref[a:a + ts, :] = (dyv * gb_e[0:ts, :] * yb * dsilu_e[0:ts, :]).astype(BF16)
            dgb_ref[a:a + ts, :] = (dyv * silu_e[0:ts, :] * yb).astype(BF16)
            dx_ref[a:a + ts, :] = (dz * gc_ref[a:a + ts, :]).astype(BF16)
            dgc_ref[a:a + ts, :] = (dz * x_ref[a:a + ts, :]).astype(BF16)
            for i, term in enumerate((dyy * z2, dyy * z1, dyy * z0, dyy)):
                acc[i] = acc[i] + jnp.sum(term, axis=0, keepdims=True)
        dw_ref[0:1, :] = acc[0]
        dw_ref[1:2, :] = acc[1]
        dw_ref[2:3, :] = acc[2]
        db_ref[...] = acc[3]

    col = lambda base: pl.BlockSpec((s, LANE), lambda j: (0, base + j))
    out_col = pl.BlockSpec((s, LANE), lambda j: (0, j))
    big = jax.ShapeDtypeStruct((s, WIDTH), BF16)
    return pl.pallas_call(
        body, name=name, grid=(4,),
        in_specs=[col(COL_CONV_X), col(COL_CONV_GB), col(COL_CONV_GC), col(COL_CONV_G), out_col,
                  pl.BlockSpec((3, LANE), lambda j: (0, j)), pl.BlockSpec((1, LANE), lambda j: (0, j))],
        out_specs=[out_col, out_col, out_col, out_col,
                   pl.BlockSpec((3, LANE), lambda j: (0, j)), pl.BlockSpec((1, LANE), lambda j: (0, j))],
        out_shape=[big, big, big, big,
                   jax.ShapeDtypeStruct((3, WIDTH), F32), jax.ShapeDtypeStruct((1, WIDTH), F32)],
    )(u, u, u, u, dy, conv_w, conv_b)


LOG2_E = 1.4426950408889634
LN_2 = 0.6931471805599453


def _sb_scores(q_h, k_blk, valid, later_mat, carry):
    z = _dot(q_h, k_blk, NT)
    neg_z = -z
    soft = jnp.log(1.0 + jnp.exp2(jnp.minimum(z, neg_z))) * LOG2_E
    log_keep = jnp.minimum(neg_z, 0.0) - soft
    log_beta = log_keep + z
    if valid is not None:
        log_keep = jnp.where(valid, log_keep, 0.0)
    later = _dot(log_keep.astype(BF16), later_mat, NN) + carry
    return log_keep, log_beta, later


def _masked(valid, x):
    return x if valid is None else jnp.where(valid, x, 0.0)


def _diagonal_masks(tq, tk):
    r = lax.broadcasted_iota(jnp.int32, (tq, tk), 0)
    cidx = lax.broadcasted_iota(jnp.int32, (tq, tk), 1)
    return [cidx + d * tk < r for d in range(tq // tk)]


def _triangle(tk, op):
    r = lax.broadcasted_iota(jnp.int32, (tk, tk), 0)
    cidx = lax.broadcasted_iota(jnp.int32, (tk, tk), 1)
    return op(r, cidx).astype(BF16)


def _split_refs(refs, n_in, n_out, n_scratch, rider):
    r_in = len(rider.inputs) if rider else 0
    r_out = len(rider.out_shape) if rider else 0
    a, b = n_in + r_in, n_in + r_in + n_out + r_out
    own = refs[:n_in] + refs[a:a + n_out] + refs[b:b + n_scratch]
    return own, (refs[n_in:a], refs[a + n_out:b], refs[b + n_scratch:])


def _rider_call_args(rider, n_in, n_out):
    if rider is None:
        return dict(in_specs=[], out_specs=[], out_shape=[], aliases={}, scratch=[], inputs=[])
    return dict(in_specs=[ANY] * len(rider.inputs), out_specs=[ANY] * len(rider.out_shape),
                out_shape=list(rider.out_shape), scratch=list(rider.scratch), inputs=list(rider.inputs),
                aliases={n_in + a: n_out + b for a, b in rider.aliases.items()})


def _ride(rider, phase, when, parts):
    fn = getattr(rider, phase) if rider else None
    if fn is not None:
        pl.when(when)(lambda: fn(*parts))


def _sb_fwd(u, name, t=512, tk=256, pairs=4, rider=None):
    s = u.shape[0]
    assert s // tk <= LANE and 4 % pairs == 0 and t % tk == 0
    scale = HEAD_DIM ** -0.5
    nh = 2 * pairs
    wide = pairs * LANE
    ratio = t // tk
    groups, nq = 4 // pairs, s // t

    def body(*refs):
        own, riding = _split_refs(refs, 4, 3, 4, rider)
        q_ref, k_ref, v_ref, g_ref, o_ref, y_ref, after_ref, kb_ref, vb_ref, acc_ref, carry_ref = own
        grp = pl.program_id(0)
        i = pl.program_id(1)
        _ride(rider, "start", jnp.logical_and(grp == 0, i == 0), riding)
        _ride(rider, "middle", jnp.logical_and(grp == groups - 1, i == (3 * nq) // 4), riding)

        @pl.when(i == 0)
        def _():
            kb_ref[...] = k_ref[...].astype(BF16)
            vb_ref[...] = v_ref[...].astype(BF16)

        lane = lax.broadcasted_iota(jnp.int32, (t, LANE), 1)
        first = lane < HEAD_DIM
        after_ref[...] = jnp.zeros_like(after_ref)
        qv = q_ref[...] * (scale * LOG2_E)
        q_heads = []
        for p in range(pairs):
            qp = qv[:, p * LANE:(p + 1) * LANE]
            q_heads += [jnp.where(first, qp, 0.0).astype(BF16), jnp.where(first, 0.0, qp).astype(BF16)]
        later_mat = _triangle(tk, lambda r, cidx: r > cidx)
        acc_ref[...] = jnp.zeros_like(acc_ref)
        carry_ref[...] = jnp.zeros_like(carry_ref)

        def block(kb, valid, lo=0):
            rows = pl.ds(pl.multiple_of(kb * tk, tk), tk)
            k_blk = kb_ref[rows, :]
            v_blk = vb_ref[rows, :]
            carries = [carry_ref[h, lo:, :] for h in range(nh)]
            afters = [after_ref[lo:, h * LANE:(h + 1) * LANE] for h in range(nh)]
            accs = [acc_ref[h, lo:, :] for h in range(nh)]
            outs = []
            for h in range(nh):
                cols = slice((h // 2) * LANE, (h // 2 + 1) * LANE)
                log_keep, log_beta, later = _sb_scores(q_heads[h][lo:], k_blk[:, cols], valid, later_mat, carries[h])
                a = _masked(valid, jnp.exp2(log_beta + later))
                outs.append((accs[h] + _dot(a.astype(BF16), v_blk[:, cols], NN),
                             carries[h] + jnp.sum(log_keep, axis=1, keepdims=True),
                             jnp.where(lane[lo:] == kb, carries[h], afters[h])))
            for h in range(nh):
                acc_ref[h, lo:, :] = outs[h][0]
                carry_ref[h, lo:, :] = outs[h][1]
                after_ref[lo:, h * LANE:(h + 1) * LANE] = outs[h][2]

        def step(j, _):
            block(ratio * i - 1 - j, None)
            return 0

        masks = _diagonal_masks(t, tk)
        for d in reversed(range(ratio)):
            block(ratio * i + d, masks[d][d * tk:], d * tk)
        lax.fori_loop(0, ratio * i, step, 0)
        for p in range(pairs):
            cols = slice(p * LANE, (p + 1) * LANE)
            o = jnp.where(first, acc_ref[2 * p], acc_ref[2 * p + 1])
            o_ref[:, cols] = o
            gate = g_ref[:, cols]
            y_ref[:, cols] = (o * gate * _sigmoid(gate)).astype(BF16)
        _ride(rider, "finish", jnp.logical_and(grp == groups - 1, i == nq - 1), riding)

    blk = lambda base: pl.BlockSpec((t, wide), lambda g, i: (i, base // pairs + g))
    full = lambda base: pl.BlockSpec((s, wide), lambda g, i: (0, base // pairs + g))
    out_blk = pl.BlockSpec((t, wide), lambda g, i: (i, g))
    extra = _rider_call_args(rider, 4, 3)
    outs = pl.pallas_call(
        body, name=name, grid=(groups, nq),
        in_specs=[blk(COL_SB_Q), full(COL_SB_K), full(COL_SB_V), blk(COL_SB_G)] + extra["in_specs"],
        out_specs=[out_blk, out_blk, pl.BlockSpec((t, nh * LANE), lambda g, i: (i, g))] + extra["out_specs"],
        out_shape=[jax.ShapeDtypeStruct((s, WIDTH), F32), jax.ShapeDtypeStruct((s, WIDTH), BF16),
                   jax.ShapeDtypeStruct((s, 8 * LANE), F32)] + extra["out_shape"],
        input_output_aliases=extra["aliases"],
        scratch_shapes=[pltpu.VMEM((s, wide), BF16), pltpu.VMEM((s, wide), BF16),
                        pltpu.VMEM((nh, t, LANE), F32), pltpu.VMEM((nh, t, 1), F32)] + extra["scratch"],
        compiler_params=pltpu.CompilerParams(dimension_semantics=("arbitrary", "arbitrary")),
    )(u, u, u, u, *extra["inputs"])
    return outs[:3], outs[3:]


def _sb_bwd(u, o, after, dy, name, t=512, tk=256, pairs=2, rider=None):
    s = u.shape[0]
    nq = s // t
    scale = HEAD_DIM ** -0.5
    nh = 2 * pairs
    wide = pairs * LANE
    ratio = t // tk
    groups = 4 // pairs

    def body(*refs):
        own, riding = _split_refs(refs, 7, 4, 6, rider)
        (q_ref, k_ref, v_ref, g_ref, o_ref, after_ref, dy_ref, dq_ref, dk_ref, dv_ref, dg_ref,
         kb_ref, vb_ref, dk_acc, dv_acc, dq_acc, carry_ref) = own
        grp = pl.program_id(0)
        i = pl.program_id(1)
        _ride(rider, "start", jnp.logical_and(grp == 0, i == 0), riding)

        @pl.when(i == 0)
        def _():
            kb_ref[...] = k_ref[...].astype(BF16)
            vb_ref[...] = v_ref[...].astype(BF16)
            dk_acc[...] = jnp.zeros_like(dk_acc)
            dv_acc[...] = jnp.zeros_like(dv_acc)

        lane = lax.broadcasted_iota(jnp.int32, (t, LANE), 1)
        first = lane < HEAD_DIM
        gate = g_ref[...]
        silu, dsilu = _silu_and_grad(gate)
        dyv = dy_ref[...]
        do = dyv * silu
        dg_ref[...] = (dyv * o_ref[...] * dsilu).astype(BF16)
        qv = q_ref[...] * (scale * LOG2_E)
        do_heads, q_heads = [], []
        for p in range(pairs):
            cols = slice(p * LANE, (p + 1) * LANE)
            do_heads += [jnp.where(first, do[:, cols], 0.0).astype(BF16), jnp.where(first, 0.0, do[:, cols]).astype(BF16)]
            q_heads += [jnp.where(first, qv[:, cols], 0.0).astype(BF16), jnp.where(first, 0.0, qv[:, cols]).astype(BF16)]
        later_mat = _triangle(tk, lambda r, cidx: r > cidx)
        before_mat = _triangle(tk, lambda r, cidx: r < cidx)
        dq_acc[...] = jnp.zeros_like(dq_acc)
        carry_ref[...] = jnp.zeros_like(carry_ref)

        def block(kb, valid, lo=0):
            rows = pl.ds(pl.multiple_of(kb * tk, tk), tk)
            k_blk = kb_ref[rows, :]
            v_blk = vb_ref[rows, :]
            carries = [carry_ref[h, lo:, :] for h in range(nh)]
            dq_old = [dq_acc[h, lo:, :] for h in range(nh)]
            dk_old = dk_acc[rows, :]
            dv_old = dv_acc[rows, :]
            outs = []
            for h in range(nh):
                cols = slice((h // 2) * LANE, (h // 2 + 1) * LANE)
                q_h, do_h = q_heads[h][lo:], do_heads[h][lo:]
                after = jnp.sum(jnp.where(lane[lo:] == kb, after_ref[lo:, h * LANE:(h + 1) * LANE], 0.0), axis=1,
                                keepdims=True)
                _, log_beta, later = _sb_scores(q_h, k_blk[:, cols], valid, later_mat, after)
                beta = jnp.exp2(log_beta)
                a = _masked(valid, jnp.exp2(log_beta + later))
                da = _dot(do_h, v_blk[:, cols], NT)
                gterm = a * da
                before = _dot(gterm.astype(BF16), before_mat, NN) + carries[h]
                dz_b = _masked(valid, gterm * (1.0 - beta) - beta * before).astype(BF16)
                outs.append((dq_old[h] + _dot(dz_b, k_blk[:, cols], NN), _dot(dz_b, q_h, TN),
                             _dot(a.astype(BF16), do_h, TN),
                             carries[h] + jnp.sum(gterm, axis=1, keepdims=True)))
            for h in range(nh):
                dq_acc[h, lo:, :] = outs[h][0]
                carry_ref[h, lo:, :] = outs[h][3]
            dk_new = [outs[2 * p][1] + outs[2 * p + 1][1] for p in range(pairs)]
            dv_new = [outs[2 * p][2] + outs[2 * p + 1][2] for p in range(pairs)]
            dk_acc[rows, :] = dk_old + (dk_new[0] if pairs == 1 else jnp.concatenate(dk_new, axis=1))
            dv_acc[rows, :] = dv_old + (dv_new[0] if pairs == 1 else jnp.concatenate(dv_new, axis=1))

        def step(kb, _):
            block(kb, None)
            return 0

        lax.fori_loop(0, ratio * i, step, 0)
        masks = _diagonal_masks(t, tk)
        for d in range(ratio):
            block(ratio * i + d, masks[d][d * tk:], d * tk)
        for p in range(pairs):
            dq_ref[:, p * LANE:(p + 1) * LANE] = (jnp.where(first, dq_acc[2 * p], dq_acc[2 * p + 1]) * scale).astype(BF16)

        @pl.when(i == nq - 1)
        def _():
            dk_ref[...] = (dk_acc[...] * LN_2).astype(BF16)
            dv_ref[...] = dv_acc[...].astype(BF16)

        _ride(rider, "finish", jnp.logical_and(grp == groups - 1, i == nq - 1), riding)

    blk = lambda base: pl.BlockSpec((t, wide), lambda g, i: (i, base // pairs + g))
    full = lambda base: pl.BlockSpec((s, wide), lambda g, i: (0, base // pairs + g))
    out_blk = pl.BlockSpec((t, wide), lambda g, i: (i, g))
    out_full = pl.BlockSpec((s, wide), lambda g, i: (0, g))
    big = jax.ShapeDtypeStruct((s, WIDTH), BF16)
    extra = _rider_call_args(rider, 7, 4)
    outs = pl.pallas_call(
        body, name=name, grid=(groups, nq),
        in_specs=[blk(COL_SB_Q), full(COL_SB_K), full(COL_SB_V), blk(COL_SB_G), out_blk,
                  pl.BlockSpec((t, nh * LANE), lambda g, i: (i, g)), out_blk] + extra["in_specs"],
        out_specs=[out_blk, out_full, out_full, out_blk] + extra["out_specs"],
        out_shape=[big, big, big, big] + extra["out_shape"],
        input_output_aliases=extra["aliases"],
        scratch_shapes=[pltpu.VMEM((s, wide), BF16), pltpu.VMEM((s, wide), BF16),
                        pltpu.VMEM((s, wide), F32), pltpu.VMEM((s, wide), F32),
                        pltpu.VMEM((nh, t, LANE), F32), pltpu.VMEM((nh, t, 1), F32)] + extra["scratch"],
        compiler_params=pltpu.CompilerParams(dimension_semantics=("arbitrary", "arbitrary")),
    )(u, u, u, u, o, after, dy, *extra["inputs"])
    return outs[:4], outs[4:]


def _gate_fwd(u, projs, name, ts=256):
    s = u.shape[0]

    def body(m0, m1, m2, p0, p1, p2, out_ref):
        tot = None
        for m_ref, p_ref in ((m0, p0), (m1, p1), (m2, p2)):
            term = _sigmoid(m_ref[...]) * p_ref[...].astype(F32)
            tot = term if tot is None else tot + term
        out_ref[...] = tot.astype(BF16)

    mspec = lambda n: pl.BlockSpec((ts, D_MODEL), lambda i: (i, COL_MERGE_1024 + n))
    row = pl.BlockSpec((ts, D_MODEL), lambda i: (i, 0))
    return pl.pallas_call(
        body, name=name, grid=(s // ts,),
        in_specs=[mspec(0), mspec(1), mspec(2), row, row, row],
        out_specs=row, out_shape=jax.ShapeDtypeStruct((s, D_MODEL), BF16),
    )(u, u, u, *projs)


def _gate_bwd(u, projs, dmerged, name, ts=256, rider=None):
    s = u.shape[0]
    steps = s // ts

    def body(*refs):
        own, riding = _split_refs(refs, 7, 6, 0, rider)
        m0, m1, m2, p0, p1, p2, dm_ref, dp0, dp1, dp2, dl0, dl1, dl2 = own
        _ride(rider, "start", pl.program_id(0) == 0, riding)
        dm = dm_ref[...].astype(F32)
        for m_ref, p_ref, dp_ref, dl_ref in ((m0, p0, dp0, dl0), (m1, p1, dp1, dl1), (m2, p2, dp2, dl2)):
            gate = _sigmoid(m_ref[...])
            dp_ref[...] = (dm * gate).astype(BF16)
            dl_ref[...] = (dm * p_ref[...].astype(F32) * gate * (1.0 - gate)).astype(BF16)
        _ride(rider, "finish", pl.program_id(0) == steps - 1, riding)

    mspec = lambda n: pl.BlockSpec((ts, D_MODEL), lambda i: (i, COL_MERGE_1024 + n))
    row = pl.BlockSpec((ts, D_MODEL), lambda i: (i, 0))
    big = jax.ShapeDtypeStruct((s, D_MODEL), BF16)
    extra = _rider_call_args(rider, 7, 6)
    outs = pl.pallas_call(
        body, name=name, grid=(steps,),
        in_specs=[mspec(0), mspec(1), mspec(2), row, row, row, row] + extra["in_specs"],
        out_specs=[row] * 6 + extra["out_specs"], out_shape=[big] * 6 + extra["out_shape"],
        input_output_aliases=extra["aliases"], scratch_shapes=extra["scratch"],
        compiler_params=pltpu.CompilerParams(dimension_semantics=("arbitrary",)),
    )(u, u, u, *projs, dmerged, *extra["inputs"])
    return outs[:3], outs[3:6], outs[6:]


def _as_rows(a):
    return a.reshape(-1, a.shape[-1])


def _row_tile(rows, cols, bytes_per_row_elem=4, cap=1 << 20):
    tr = rows
    while tr * cols * bytes_per_row_elem > cap and tr % 2 == 0 and (tr // 2) % 16 == 0:
        tr //= 2
    return tr


def _cast_bf16(a, name):
    a2 = _as_rows(a)
    rows, cols = a2.shape
    tr = _row_tile(rows, cols)

    def body(a_ref, o_ref):
        o_ref[...] = a_ref[...].astype(BF16)

    spec = pl.BlockSpec((tr, cols), lambda i: (i, 0))
    out = pl.pallas_call(body, name=name, grid=(rows // tr,), in_specs=[spec], out_specs=spec,
                         out_shape=jax.ShapeDtypeStruct((rows, cols), BF16))(a2)
    return out.reshape(a.shape)


def _adamw(w, g, m, v, name):
    shape = w.shape
    w2, g2, m2, v2 = (_as_rows(a) for a in (w, g, m, v))
    rows, cols = w2.shape
    tr = _row_tile(rows, cols)
    c1 = 1.0 - ADAM_B1 ** ADAM_STEP
    c2 = 1.0 - ADAM_B2 ** ADAM_STEP

    def body(w_ref, g_ref, m_ref, v_ref, d_ref, nm_ref, nv_ref):
        gv = g_ref[...]
        nm = ADAM_B1 * m_ref[...] + (1.0 - ADAM_B1) * gv
        nv = ADAM_B2 * v_ref[...] + (1.0 - ADAM_B2) * (gv * gv)
        nm_ref[...] = nm
        nv_ref[...] = nv
        d_ref[...] = -ADAM_LR * ((nm / c1) / (jnp.sqrt(nv / c2) + ADAM_EPS) + ADAM_WD * w_ref[...])

    spec = pl.BlockSpec((tr, cols), lambda i: (i, 0))
    sds = jax.ShapeDtypeStruct((rows, cols), F32)
    outs = pl.pallas_call(body, name=name, grid=(rows // tr,), in_specs=[spec] * 4, out_specs=[spec] * 3,
                          out_shape=[sds] * 3)(w2, g2, m2, v2)
    return tuple(o.reshape(shape) for o in outs)


def _sum_slots(a, out_dtype, name):
    n = a.shape[0]
    a3 = a.reshape(n, -1, a.shape[-1])
    _, rows, cols = a3.shape
    tr = _row_tile(rows, cols * n)

    def body(a_ref, o_ref):
        tot = a_ref[0].astype(F32)
        for k in range(1, n):
            tot = tot + a_ref[k].astype(F32)
        o_ref[...] = tot.astype(out_dtype)

    out = pl.pallas_call(
        body, name=name, grid=(rows // tr,),
        in_specs=[pl.BlockSpec((n, tr, cols), lambda i: (0, i, 0))],
        out_specs=pl.BlockSpec((tr, cols), lambda i: (i, 0)),
        out_shape=jax.ShapeDtypeStruct((rows, cols), out_dtype))(a3)
    return out.reshape(a.shape[1:])


def _chip_sum(own, recv, axis, core, name):
    half = recv.shape
    nd = len(half)
    last = nd - 1
    if axis == last:
        tl, nt = half[last], 1
    else:
        tl = min(half[last], 2048)
        nt = half[last] // tl
    block = half[:last] + (tl,)

    def own_index(i, core_ref):
        idx = [0] * nd
        idx[last] = i
        if axis == last:
            idx[last] = core_ref[0]
        else:
            idx[axis] = core_ref[0]
        return tuple(idx)

    def recv_index(i, core_ref):
        idx = [0] * nd
        idx[last] = i
        return tuple(idx)

    def body(core_ref, own_ref, recv_ref, o_ref):
        o_ref[...] = (own_ref[...] + recv_ref[...]).astype(BF16)

    return pl.pallas_call(
        body, name=name,
        grid_spec=pltpu.PrefetchScalarGridSpec(
            num_scalar_prefetch=1, grid=(nt,),
            in_specs=[pl.BlockSpec(block, own_index), pl.BlockSpec(block, recv_index)],
            out_specs=pl.BlockSpec(block, recv_index)),
        out_shape=jax.ShapeDtypeStruct(half, BF16),
    )(core, own, recv)


def _mesh_position():
    return lax.axis_index("x"), lax.axis_index("y"), lax.axis_index("c")


def _other_chips(x, y):
    return [(1 - x, y), (x, 1 - y), (1 - x, 1 - y)]


ALL_FLIPS = [(0, 0, 1), (1, 0, 0), (0, 1, 0), (1, 1, 0), (1, 0, 1), (0, 1, 1), (1, 1, 1)]


def _half(ref, axis, which, size):
    idx = [slice(None)] * len(ref.shape)
    idx[axis] = pl.ds(which * size, size)
    return ref.at[tuple(idx)]


def _sub(ref, picks):
    idx = [slice(None)] * len(ref.shape)
    for axis, start, size in picks:
        idx[axis] = pl.ds(start, size)
    return ref.at[tuple(idx)]


def _remote(src, dst, sems_send, sems_recv, k, to):
    return pltpu.make_async_remote_copy(src_ref=src, dst_ref=dst, send_sem=sems_send.at[k], recv_sem=sems_recv.at[k],
                                        device_id=to, device_id_type=MESH)


def _cast_shard(w, layer, shard_axis, pos, name, tr=512):
    shape = w.shape[1:]
    nd = len(shape)
    assert shard_axis in (nd - 1, nd - 2)
    rows, cols = shape[-2:]
    tr = min(tr, rows)
    nt = rows // tr
    lead = shape[:-2]
    full = list(shape)
    full[shard_axis] *= N_CHIPS
    block = (1,) * len(lead) + (tr, cols)

    def in_index(*args):
        return (layer, *args[:-1], 0)

    def out_index(*args):
        *g, pos_ref = args
        if shard_axis == nd - 1:
            return (*g, pos_ref[1])
        return (*g[:-1], pos_ref[1] * nt + g[-1], 0)

    def body(pos_ref, a_ref, o_ref):
        o_ref[...] = a_ref[...].astype(BF16)

    return pl.pallas_call(
        body, name=name,
        grid_spec=pltpu.PrefetchScalarGridSpec(
            num_scalar_prefetch=1, grid=lead + (nt,),
            in_specs=[pl.BlockSpec((None,) + block, in_index)], out_specs=pl.BlockSpec(block, out_index)),
        out_shape=jax.ShapeDtypeStruct(tuple(full), BF16),
    )(pos, w)


class _Rider:
    def __init__(self, inputs, out_shape, aliases, scratch, start, middle, finish):
        self.inputs, self.out_shape, self.aliases, self.scratch = inputs, out_shape, aliases, scratch
        self.start, self.middle, self.finish = start, middle, finish


def _weight_gather_rider(fulls, layout):
    n = len(fulls)

    def copies(outs, sems):
        send_sems, recv_sems = sems
        x, y, c = _mesh_position()
        chips = _other_chips(x, y)
        sibling = (x, y, 1 - c)
        mine = 2 * x + y

        def place(t, chip, core):
            sh_axis, sh_size, half_axis, half_size = layout[t]
            return _sub(outs[t], [(sh_axis, chip * sh_size, sh_size), (half_axis, core * half_size, half_size)])

        direct, arrive, forward, arrive_fwd = [], [], [], []
        for t in range(n):
            for k, (px, py) in enumerate(chips):
                theirs = 2 * px + py
                direct.append(_remote(place(t, mine, c), place(t, mine, c), send_sems, recv_sems, 6 * t + k, (px, py, c)))
                arrive.append(_remote(place(t, theirs, c), place(t, theirs, c), send_sems, recv_sems, 6 * t + k, (px, py, c)))
                forward.append(_remote(place(t, theirs, c), place(t, theirs, c), send_sems, recv_sems, 6 * t + 3 + k, sibling))
                arrive_fwd.append(_remote(place(t, theirs, 1 - c), place(t, theirs, 1 - c), send_sems, recv_sems,
                                          6 * t + 3 + k, sibling))
        return direct, arrive, forward, arrive_fwd

    def start(ins, outs, sems):
        for cp in copies(outs, sems)[0]:
            cp.start()

    def middle(ins, outs, sems):
        _, arrive, forward, _ = copies(outs, sems)
        for a, f in zip(arrive, forward):
            a.wait_recv()
            f.start()

    def finish(ins, outs, sems):
        direct, _, forward, arrive_fwd = copies(outs, sems)
        for cp in arrive_fwd:
            cp.wait_recv()
        for cp in direct + forward:
            cp.wait_send()

    return _Rider(list(fulls), [jax.ShapeDtypeStruct(a.shape, a.dtype) for a in fulls], {k: k for k in range(n)},
                  [pltpu.SemaphoreType.DMA((6 * n,)), pltpu.SemaphoreType.DMA((6 * n,))], start, middle, finish)


WEIGHT_LAYOUT = [(1, 2048, 0, 512), (2, 256, 1, 256), (0, 256, 1, 512)]


def _gather_weights(fulls, conv_w):
    rider = _weight_gather_rider(fulls, WEIGHT_LAYOUT)
    n = len(fulls)

    def body(*refs):
        cw, outs, cw_f = refs[n], refs[n + 1:2 * n + 1], refs[2 * n + 1]
        sems, (cw_send, cw_recv, local_sem) = refs[2 * n + 2:2 * n + 4], refs[2 * n + 4:]
        x, y, c = _mesh_position()
        chips = _other_chips(x, y)
        mine = 2 * x + y
        local = pltpu.make_async_copy(cw, cw_f.at[mine], local_sem.at[0])
        local.start()
        rider.start(None, outs, sems)
        small = [_remote(cw, cw_f.at[mine], cw_send, cw_recv, k, (*chip, c)) for k, chip in enumerate(chips)]
        for cp in small:
            cp.start()
        rider.middle(None, outs, sems)
        rider.finish(None, outs, sems)
        for k, (px, py) in enumerate(chips):
            _remote(cw, cw_f.at[2 * px + py], cw_send, cw_recv, k, (px, py, c)).wait_recv()
        for cp in small:
            cp.wait_send()
        local.wait()

    outs = pl.pallas_call(
        body, name="gather_weights",
        in_specs=[ANY] * (n + 1), out_specs=[ANY] * (n + 1),
        out_shape=rider.out_shape + [jax.ShapeDtypeStruct((N_CHIPS,) + conv_w.shape, F32)],
        input_output_aliases=rider.aliases,
        scratch_shapes=rider.scratch + [pltpu.SemaphoreType.DMA((3,)), pltpu.SemaphoreType.DMA((3,)),
                                        pltpu.SemaphoreType.DMA((1,))],
    )(*fulls, conv_w)
    return outs[:n], outs[n]


def _swap_rider(items):
    n = len(items)
    halves = []
    for a, axis in items:
        shp = list(a.shape)
        shp[axis] //= 2
        halves.append(tuple(shp))

    def copies(ins, outs, sems):
        x, y, c = _mesh_position()
        return [_remote(_half(ins[k], items[k][1], 1 - c, halves[k][items[k][1]]), outs[k], sems[0], sems[1], k,
                        (x, y, 1 - c)) for k in range(n)]

    def start(ins, outs, sems):
        for cp in copies(ins, outs, sems):
            cp.start()

    def finish(ins, outs, sems):
        for cp in copies(ins, outs, sems):
            cp.wait()

    return _Rider([a for a, _ in items], [jax.ShapeDtypeStruct(h, a.dtype) for h, (a, _) in zip(halves, items)], {},
                  [pltpu.SemaphoreType.DMA((n,)), pltpu.SemaphoreType.DMA((n,))], start, None, finish)


def _swap_halves(items, name):
    rider = _swap_rider(items)
    n = len(items)

    def body(*refs):
        parts = (refs[:n], refs[n:2 * n], refs[2 * n:])
        rider.start(*parts)
        rider.finish(*parts)

    return pl.pallas_call(
        body, name=name, in_specs=[ANY] * n, out_specs=[ANY] * n, out_shape=rider.out_shape,
        scratch_shapes=rider.scratch,
    )(*rider.inputs)


def _grad_exchange_rider(items):
    n = len(items)
    slices = []
    for a, axis in items:
        shp = list(a.shape)
        shp[axis] //= N_CHIPS
        slices.append(tuple(shp))

    def copies(ins, outs, sems):
        send_sems, recv_sems = sems
        x, y, c = _mesh_position()
        made = []
        for k in range(n):
            axis = items[k][1]
            for r, (px, py) in enumerate(_other_chips(x, y)):
                made.append(_remote(_half(ins[k], axis, 2 * px + py, slices[k][axis]), outs[k].at[r],
                                    send_sems, recv_sems, 3 * k + r, (px, py, c)))
        return made

    def start(ins, outs, sems):
        for cp in copies(ins, outs, sems):
            cp.start()

    def finish(ins, outs, sems):
        for cp in copies(ins, outs, sems):
            cp.wait()

    return _Rider([a for a, _ in items], [jax.ShapeDtypeStruct((N_CHIPS - 1,) + s, BF16) for s in slices], {},
                  [pltpu.SemaphoreType.DMA((3 * n,)), pltpu.SemaphoreType.DMA((3 * n,))], start, None, finish)


def _gather_small(small):
    def body(small_ref, small_all, send_sems, recv_sems, local_sem):
        x, y, c = _mesh_position()
        me = 4 * x + 2 * y + c
        local = pltpu.make_async_copy(small_ref, small_all.at[me], local_sem.at[0])
        local.start()
        copies = [_remote(small_ref, small_all.at[me], send_sems, recv_sems, r, (x ^ fx, y ^ fy, c ^ fc))
                  for r, (fx, fy, fc) in enumerate(ALL_FLIPS)]
        for cp in copies:
            cp.start()
        for cp in copies:
            cp.wait()
        local.wait()

    return pl.pallas_call(
        body, name="gather_small_grads", in_specs=[ANY], out_specs=ANY,
        out_shape=jax.ShapeDtypeStruct((2 * N_CHIPS,) + small.shape, F32),
        scratch_shapes=[pltpu.SemaphoreType.DMA((len(ALL_FLIPS),)), pltpu.SemaphoreType.DMA((len(ALL_FLIPS),)),
                        pltpu.SemaphoreType.DMA((1,))],
    )(small)


def _sum_chips(recv, own, shard_axis, split_axis, pos, dest, layer, name, tr=128):
    sl = recv.shape[1:]
    nd = len(sl)
    tiled = nd == 2 and sl[0] > tr
    nt = sl[0] // tr if tiled else 1
    block = ((tr,) + sl[1:]) if tiled else sl
    shard = list(sl)
    shard[split_axis] *= 2

    def recv_index(i, pos_ref):
        return (0, i) + (0,) * (nd - 1) if tiled else (0,) * (nd + 1)

    def own_index(i, pos_ref):
        idx = [0] * nd
        idx[shard_axis] = pos_ref[1]
        if tiled:
            idx[0] = pos_ref[1] * nt + i if shard_axis == 0 else i
        return tuple(idx)

    def out_index(i, pos_ref):
        idx = [0] * nd
        idx[split_axis] = pos_ref[0]
        if tiled:
            idx[0] = pos_ref[0] * nt + i if split_axis == 0 else i
        return (layer, *idx)

    def body(pos_ref, recv_ref, own_ref, *rest):
        o_ref = rest[-1]
        tot = own_ref[...].astype(F32)
        for k in range(N_CHIPS - 1):
            tot = tot + recv_ref[k].astype(F32)
        o_ref[0] = tot

    in_specs = [pl.BlockSpec((N_CHIPS - 1,) + block, recv_index), pl.BlockSpec(block, own_index)]
    args = [pos, recv, own]
    aliases = {}
    if dest is not None:
        in_specs.append(ANY)
        args.append(dest)
        aliases = {3: 0}
    return pl.pallas_call(
        body, name=name,
        grid_spec=pltpu.PrefetchScalarGridSpec(
            num_scalar_prefetch=1, grid=(nt,), in_specs=in_specs,
            out_specs=pl.BlockSpec((1,) + block, out_index)),
        out_shape=jax.ShapeDtypeStruct((DEPTH,) + tuple(shard), F32),
        input_output_aliases=aliases,
    )(*args)


def _share_halves(bufs, name):
    n = len(bufs)

    def body(*refs):
        outs, (send_sems, recv_sems) = refs[n:2 * n], refs[2 * n:]
        x, y, c = _mesh_position()
        copies = []
        for k, (a, axis) in enumerate(bufs):
            size = a.shape[1 + axis] // 2
            mine = _half(outs[k], 1 + axis, c, size)
            copies.append(_remote(mine, mine, send_sems, recv_sems, k, (x, y, 1 - c)))
        for cp in copies:
            cp.start()
        for cp in copies:
            cp.wait()

    return pl.pallas_call(
        body, name=name, in_specs=[ANY] * n, out_specs=[ANY] * n,
        out_shape=[jax.ShapeDtypeStruct(a.shape, F32) for a, _ in bufs],
        input_output_aliases={k: k for k in range(n)},
        scratch_shapes=[pltpu.SemaphoreType.DMA((n,)), pltpu.SemaphoreType.DMA((n,))],
    )(*[a for a, _ in bufs])


def _layer_fwd(x, p, l, rider=None):
    tag = f"l{l}_"
    h = _rms_fwd(x, p["pre_g"], tag + "pre_norm")
    u = _matmul(h, p["w_in"], "nn", F32, tag + "in_proj")
    y_pool = _pool_fwd(u, p["pool_w"], p["pool_scale"], tag + "pool")
    y_conv = _conv_fwd(u, p["conv_w"], p["conv_b"], tag + "conv")
    (o_sb, y_sb, sb_after), carried = _sb_fwd(u, tag + "stickbreak", rider=rider)
    ys = [y_pool, y_conv, y_sb]
    projs = [_matmul(ys[n], p["w_branch"], "nn", BF16, tag + f"branch_proj{n}", b_lead=(n,)) for n in range(3)]
    merged = _gate_fwd(u, projs, tag + "merge")
    out = _matmul(merged, p["w_out"], "nn", F32, tag + "out_proj")
    saved = dict(x=x, h=h, u=u, ys=ys, o_sb=o_sb, sb_after=sb_after, projs=projs, merged=merged, out=out)
    return out, saved, carried


def _layer_bwd(dy, p, saved, l, merge_rider=None, early=None, late=None):
    tag = f"l{l}_bwd_"
    u = saved["u"]
    d_out, g_post = _rms_bwd(saved["out"], p["post_g"], dy, None, BF16, tag + "post_norm")
    d_merged = _matmul(d_out, p["w_out"], "nt", BF16, tag + "out_proj_dx")
    g_w_out = _matmul(saved["merged"], d_out, "tn", F32, tag + "out_proj_dw", tk=2048)
    d_projs, d_logits, carried_merge = _gate_bwd(u, saved["projs"], d_merged, tag + "merge", rider=merge_rider)
    g_w_branch = jnp.stack([_matmul(saved["ys"][n], d_projs[n], "tn", F32, tag + f"branch_dw{n}", tk=2048)
                            for n in range(3)])
    rider = early(g_w_branch, g_w_out, carried_merge) if early else None
    d_ys = [_matmul(d_projs[n], p["w_branch"], "nt", F32, tag + f"branch_dx{n}", b_lead=(n,)) for n in range(3)]
    d_pv, d_pg, g_pool_w, g_pool_scale = _pool_bwd(u, d_ys[0], p["pool_w"], p["pool_scale"], tag + "pool")
    d_cx, d_cgb, d_cgc, d_cg, g_conv_w, g_conv_b = _conv_bwd(u, d_ys[1], p["conv_w"], p["conv_b"], tag + "conv")
    (d_q, d_k, d_v, d_sg), carried_attn = _sb_bwd(u, saved["o_sb"], saved["sb_after"], d_ys[2], tag + "stickbreak",
                                                  rider=rider)
    du = jnp.concatenate([d_pv, d_pg, d_cx, d_cgb, d_cgc, d_cg, d_q, d_k, d_v, d_sg] + list(d_logits), axis=1)
    g_w_in = _matmul(saved["h"], du, "tn", F32, tag + "in_proj_dw", tk=2048)
    rider = late(g_w_in) if late else None
    dh = _matmul(du, p["w_in"], "nt", BF16, tag + "in_proj_dx", tk=2048, rider=rider)
    dh, carried_dx = dh if rider else (dh, [])
    dx, g_pre = _rms_bwd(saved["x"], p["pre_g"], dh, dy, F32, tag + "pre_norm")
    grads = dict(w_in=g_w_in, w_branch=g_w_branch, w_out=g_w_out, pre_g=g_pre, post_g=g_post,
                 pool_w=g_pool_w, pool_scale=g_pool_scale, conv_w=g_conv_w, conv_b=g_conv_b)
    return dx, grads, carried_attn, carried_dx


SMALL_ORDER = ["pre_g", "pool_w", "pool_scale", "conv_w", "conv_b", "post_g"]


def _pack_small(per_layer, loss_part):
    parts, spans, at = [], {}, 0
    for name in SMALL_ORDER:
        a = jnp.stack([per_layer[l][name] for l in range(DEPTH)]).reshape(-1, LANE)
        parts.append(a)
        spans[name] = (at, a.shape[0])
        at += a.shape[0]
    parts.append(jnp.broadcast_to(loss_part, (8, LANE)))
    spans["loss"] = (at, 8)
    return jnp.concatenate(parts, axis=0), spans


def kernel(x, pre_norm_g, w_in, pool_w, pool_scale, conv_w, conv_b, w_branch, w_out, post_norm_g, loss_target, m_pre_norm_g, m_w_in, m_pool_w, m_pool_scale, m_conv_w, m_conv_b, m_w_branch, m_w_out, m_post_norm_g, v_pre_norm_g, v_w_in, v_pool_w, v_pool_scale, v_conv_w, v_conv_b, v_w_branch, v_w_out, v_post_norm_g):
    mx, my, mc = _mesh_position()
    chip = 2 * mx + my
    core = mc.astype(jnp.int32).reshape(1)
    pos = jnp.stack([mc, chip]).astype(jnp.int32)

    names = ["w_in", "w_branch", "w_out"]
    given = dict(w_in=w_in, w_branch=w_branch, w_out=w_out)
    in_place = [[_cast_shard(given[n], l, WEIGHT_LAYOUT[i][0], pos, f"cast_{n}{l}") for i, n in enumerate(names)]
                for l in range(DEPTH)]
    gathered, conv_w_by_chip = _gather_weights(in_place[0], conv_w)
    conv_w_f = conv_w_by_chip.transpose(1, 2, 0, 3).reshape(DEPTH, 3, WIDTH)
    pool_w_b = _cast_bf16(pool_w, "cast_pool_w")

    def layer_params(l, big):
        return dict(pre_g=pre_norm_g[l:l + 1], post_g=post_norm_g[l:l + 1], w_in=big[0], w_branch=big[1],
                    w_out=big[2], pool_w=pool_w_b[l], pool_scale=pool_scale[l:l + 1], conv_w=conv_w_f[l],
                    conv_b=conv_b[l:l + 1])

    act = x[0]
    params, saved = [], []
    for l in range(DEPTH):
        params.append(layer_params(l, gathered))
        rider = _weight_gather_rider(in_place[l + 1], WEIGHT_LAYOUT) if l + 1 < DEPTH else None
        out, sv, gathered = _layer_fwd(act, params[l], l, rider)
        saved.append(sv)
        if l < DEPTH - 1:
            act = _resid_out(act, out, params[l]["post_g"], None, f"l{l}_resid")
    dy, loss_part = _resid_out(act, saved[-1]["out"], params[-1]["post_g"], loss_target[0], "loss_head")

    split_axis = dict(w_in=0, w_branch=1, w_out=1)
    shard_axis = dict(w_in=1, w_branch=2, w_out=0)
    grads = [None] * DEPTH
    chip_sums = [dict() for _ in range(DEPTH)]
    by_chip = [dict() for _ in range(DEPTH)]

    def reduce_in_chip(l, which, g):
        items = [(g[n], split_axis[n]) for n in which]
        from_sibling = _swap_halves(items, f"swap_grad_halves{l}_{which[0]}")
        for n, (a, axis), r in zip(which, items, from_sibling):
            chip_sums[l][n] = _chip_sum(a, r, axis, core, f"chip_sum{l}_{n}")

    def exchange_rider(keys):
        return _grad_exchange_rider([(chip_sums[l][n], shard_axis[n]) for l, n in keys])

    waiting = []
    for l in reversed(range(DEPTH)):
        sent_early, sent_late = list(waiting) + [(l, "w_branch"), (l, "w_out")], [(l, "w_in")]
        waiting_items = [(grads[ll][n], split_axis[n]) for ll, n in waiting]

        def early(g_w_branch, g_w_out, from_sibling, l=l, keys=sent_early, above=tuple(waiting), items=waiting_items):
            for (ll, n), (a, axis), r in zip(above, items, from_sibling):
                chip_sums[ll][n] = _chip_sum(a, r, axis, core, f"chip_sum{ll}_{n}")
            reduce_in_chip(l, ["w_branch", "w_out"], dict(w_branch=g_w_branch, w_out=g_w_out))
            return exchange_rider(keys)

        def late(g_w_in, l=l, keys=sent_late):
            reduce_in_chip(l, ["w_in"], dict(w_in=g_w_in))
            return exchange_rider(keys)

        if l == DEPTH - 1:
            dy, grads[l], _, _ = _layer_bwd(dy, params[l], saved[l], l)
            waiting = [(l, n) for n in names]
        else:
            dy, grads[l], got_early, got_late = _layer_bwd(dy, params[l], saved[l], l, _swap_rider(waiting_items),
                                                           early, late)
            for (ll, n), r in zip(sent_early + sent_late, list(got_early) + list(got_late)):
                by_chip[ll][n] = r
            waiting = []
    assert not waiting
    grad_x = dy[None]
    small_part, spans = _pack_small(grads, loss_part)
    small_all = _gather_small(small_part)
    bufs = []
    for n in names:
        dest = None
        for l in range(DEPTH):
            dest = _sum_chips(by_chip[l][n], chip_sums[l][n], shard_axis[n], split_axis[n], pos, dest, l,
                              f"sum_chips{l}_{n}")
        bufs.append((dest, split_axis[n]))
    g_w_in, g_w_branch, g_w_out = _share_halves(bufs, "share_grad_halves")

    small_sum = _sum_slots(small_all, F32, "sum_small")
    loss = small_sum[spans["loss"][0], 0]
    small = {}
    for name, like in (("pre_g", pre_norm_g), ("pool_w", pool_w), ("pool_scale", pool_scale), ("conv_b", conv_b),
                       ("post_g", post_norm_g)):
        at, n = spans[name]
        small[name] = small_sum[at:at + n].reshape(like.shape)
    at, n = spans["conv_w"]
    g_conv_w_full = small_sum[at:at + n].reshape(DEPTH, 3, WIDTH)
    g_conv_w = lax.dynamic_slice_in_dim(g_conv_w_full, chip * conv_w.shape[2], conv_w.shape[2], axis=2)

    g = dict(pre_norm_g=small["pre_g"], w_in=g_w_in, pool_w=small["pool_w"], pool_scale=small["pool_scale"],
             conv_w=g_conv_w, conv_b=small["conv_b"], w_branch=g_w_branch, w_out=g_w_out, post_norm_g=small["post_g"])
    w = dict(pre_norm_g=pre_norm_g, w_in=w_in, pool_w=pool_w, pool_scale=pool_scale, conv_w=conv_w, conv_b=conv_b,
             w_branch=w_branch, w_out=w_out, post_norm_g=post_norm_g)
    m = dict(pre_norm_g=m_pre_norm_g, w_in=m_w_in, pool_w=m_pool_w, pool_scale=m_pool_scale, conv_w=m_conv_w,
             conv_b=m_conv_b, w_branch=m_w_branch, w_out=m_w_out, post_norm_g=m_post_norm_g)
    v = dict(pre_norm_g=v_pre_norm_g, w_in=v_w_in, pool_w=v_pool_w, pool_scale=v_pool_scale, conv_w=v_conv_w,
             conv_b=v_conv_b, w_branch=v_w_branch, w_out=v_w_out, post_norm_g=v_post_norm_g)
    order = ["pre_norm_g", "w_in", "pool_w", "pool_scale", "conv_w", "conv_b", "w_branch", "w_out", "post_norm_g"]
    upd = {n: _adamw(w[n], g[n], m[n], v[n], "adamw_" + n) for n in order}
    return (loss, grad_x, *[g[n] for n in order], *[upd[n][0] for n in order], *[upd[n][1] for n in order],
            *[upd[n][2] for n in order])
```

```python
import functools

import jax
import jax.numpy as jnp
from jax import lax
from jax.experimental import pallas as pl
from jax.experimental.pallas import tpu as pltpu

F32 = jnp.float32
BF16 = jnp.bfloat16
MESH = pl.DeviceIdType.MESH
ANY = pl.BlockSpec(memory_space=pl.ANY)

DEPTH = 2
D_MODEL = 1024
WIDTH = 512
N_IN = 8192
N_CHIPS = 4
HEAD_DIM = 64
RMS_EPS = 1e-6
POOL_HALO = 16
CONV_HALO = 8
LANE = 128
COL_POOL_V, COL_POOL_G = 0, 4
COL_CONV_X, COL_CONV_GB, COL_CONV_GC, COL_CONV_G = 8, 12, 16, 20
COL_SB_Q, COL_SB_K, COL_SB_V, COL_SB_G = 24, 28, 32, 36
COL_MERGE_1024 = 5

ADAM_LR, ADAM_B1, ADAM_B2, ADAM_EPS, ADAM_WD, ADAM_STEP = 0.001, 0.9, 0.999, 1e-08, 0.01, 10

NN = (((1,), (0,)), ((), ()))
NT = (((1,), (1,)), ((), ()))
TN = (((0,), (0,)), ((), ()))


def _sigmoid(x):
    return 1.0 / (1.0 + jnp.exp(-x))


def _silu_and_grad(x):
    s = _sigmoid(x)
    return x * s, s * (1.0 + x * (1.0 - s))


def _dot(a, b, dims):
    return lax.dot_general(a, b, dims, preferred_element_type=F32)


def _matmul(a, b, mode, out_dtype, name, tm=1024, tn=1024, tk=1024, b_lead=(), rider=None):
    b_shape = b.shape[len(b_lead):]
    if mode == "nn":
        (m, k), (k2, n) = a.shape, b_shape
    elif mode == "nt":
        (m, k), (n, k2) = a.shape, b_shape
    else:
        (k, m), (k2, n) = a.shape, b_shape
    assert k == k2 and a.dtype == BF16 and b.dtype == BF16
    tm, tn, tk = min(tm, m), min(tn, n), min(tk, k)
    assert m % tm == 0 and n % tn == 0 and k % tk == 0
    nk = k // tk
    dims = {"nn": NN, "nt": NT, "tn": TN}[mode]

    grid = (m // tm, n // tn, nk)

    def at_step(which):
        return functools.reduce(jnp.logical_and, [pl.program_id(d) == (g - 1 if which else 0) for d, g in enumerate(grid)])

    def body(*refs):
        (a_ref, b_ref, o_ref, *scratch), riding = _split_refs(refs, 2, 1, 1 if nk > 1 else 0, rider)
        _ride(rider, "start", at_step(0), riding)
        compute(a_ref, b_ref, o_ref, scratch)
        _ride(rider, "finish", at_step(1), riding)

    def compute(a_ref, b_ref, o_ref, scratch):
        p = _dot(a_ref[...], b_ref[...], dims)
        if nk == 1:
            o_ref[...] = p.astype(o_ref.dtype)
        else:
            acc = scratch[0]
            kk = pl.program_id(2)

            @pl.when(kk == 0)
            def _():
                acc[...] = p

            @pl.when(jnp.logical_and(kk > 0, kk < nk - 1))
            def _():
                acc[...] += p

            @pl.when(kk == nk - 1)
            def _():
                o_ref[...] = (acc[...] + p).astype(o_ref.dtype)

    if mode == "tn":
        a_spec = pl.BlockSpec((tk, tm), lambda i, j, kk: (kk, i))
    else:
        a_spec = pl.BlockSpec((tm, tk), lambda i, j, kk: (i, kk))
    squeezed = (None,) * len(b_lead)
    if mode == "nt":
        b_spec = pl.BlockSpec(squeezed + (tn, tk), lambda i, j, kk: (*b_lead, j, kk))
    else:
        b_spec = pl.BlockSpec(squeezed + (tk, tn), lambda i, j, kk: (*b_lead, kk, j))
    extra = _rider_call_args(rider, 2, 1)
    outs = pl.pallas_call(
        body, name=name, grid=grid,
        in_specs=[a_spec, b_spec] + extra["in_specs"],
        out_specs=[pl.BlockSpec((tm, tn), lambda i, j, kk: (i, j))] + extra["out_specs"],
        out_shape=[jax.ShapeDtypeStruct((m, n), out_dtype)] + extra["out_shape"],
        input_output_aliases=extra["aliases"],
        scratch_shapes=([pltpu.VMEM((tm, tn), F32)] if nk > 1 else []) + extra["scratch"],
        compiler_params=pltpu.CompilerParams(dimension_semantics=("arbitrary",) * 3 if rider else
                                             ("parallel", "parallel", "arbitrary")),
    )(a, b, *extra["inputs"])
    return (outs[0], outs[1:]) if rider else outs[0]


def _rms_fwd(x, g, name, ts=512):
    s, d = x.shape

    def body(x_ref, g_ref, h_ref):
        xv = x_ref[...]
        r = lax.rsqrt(jnp.mean(xv * xv, axis=-1, keepdims=True) + RMS_EPS)
        h_ref[...] = (xv * r * g_ref[...]).astype(BF16)

    return pl.pallas_call(
        body, name=name, grid=(s // ts,),
        in_specs=[pl.BlockSpec((ts, d), lambda i: (i, 0)), pl.BlockSpec((1, d), lambda i: (0, 0))],
        out_specs=pl.BlockSpec((ts, d), lambda i: (i, 0)),
        out_shape=jax.ShapeDtypeStruct((s, d), BF16),
    )(x, g)


def _rms_bwd(xin, g, dh, resid, out_dtype, name, ts=512):
    s, d = xin.shape
    has_resid = resid is not None

    def body(*refs):
        if has_resid:
            x_ref, g_ref, dh_ref, res_ref, dx_ref, dg_ref = refs
        else:
            x_ref, g_ref, dh_ref, dx_ref, dg_ref = refs
        xv = x_ref[...]
        dhv = dh_ref[...].astype(F32)
        r = lax.rsqrt(jnp.mean(xv * xv, axis=-1, keepdims=True) + RMS_EPS)
        nrm = xv * r
        dn = dhv * g_ref[...]
        dx = r * (dn - nrm * jnp.mean(dn * nrm, axis=-1, keepdims=True))
        if has_resid:
            dx = dx + res_ref[...]
        dx_ref[...] = dx.astype(dx_ref.dtype)
        part = jnp.sum(dhv * nrm, axis=0, keepdims=True)

        @pl.when(pl.program_id(0) == 0)
        def _():
            dg_ref[...] = part

        @pl.when(pl.program_id(0) > 0)
        def _():
            dg_ref[...] += part

    row = pl.BlockSpec((ts, d), lambda i: (i, 0))
    vec = pl.BlockSpec((1, d), lambda i: (0, 0))
    ins = [xin, g, dh] + ([resid] if has_resid else [])
    return pl.pallas_call(
        body, name=name, grid=(s // ts,),
        in_specs=[row, vec, row] + ([row] if has_resid else []),
        out_specs=[row, vec],
        out_shape=[jax.ShapeDtypeStruct((s, d), out_dtype), jax.ShapeDtypeStruct((1, d), F32)],
        compiler_params=pltpu.CompilerParams(dimension_semantics=("arbitrary",)),
    )(*ins)


def _resid_out(x, out, g, target, name, ts=512):
    s, d = x.shape
    has_loss = target is not None

    def body(*refs):
        if has_loss:
            x_ref, o_ref, g_ref, t_ref, dy_ref, loss_ref = refs
        else:
            x_ref, o_ref, g_ref, y_ref = refs
        ov = o_ref[...]
        r = lax.rsqrt(jnp.mean(ov * ov, axis=-1, keepdims=True) + RMS_EPS)
        yv = x_ref[...] + ov * r * g_ref[...]
        if not has_loss:
            y_ref[...] = yv
            return
        err = yv - t_ref[...]
        dy_ref[...] = err * (1.0 / d)
        part = jnp.sum(jnp.sum(err * err, axis=-1, keepdims=True), axis=0, keepdims=True) * (0.5 / d)
        part = jnp.broadcast_to(part, (1, LANE))

        @pl.when(pl.program_id(0) == 0)
        def _():
            loss_ref[...] = part

        @pl.when(pl.program_id(0) > 0)
        def _():
            loss_ref[...] += part

    row = pl.BlockSpec((ts, d), lambda i: (i, 0))
    vec = pl.BlockSpec((1, d), lambda i: (0, 0))
    if has_loss:
        return pl.pallas_call(
            body, name=name, grid=(s // ts,),
            in_specs=[row, row, vec, row],
            out_specs=[row, pl.BlockSpec((1, LANE), lambda i: (0, 0))],
            out_shape=[jax.ShapeDtypeStruct((s, d), F32), jax.ShapeDtypeStruct((1, LANE), F32)],
            compiler_params=pltpu.CompilerParams(dimension_semantics=("arbitrary",)),
        )(x, out, g, target)
    return pl.pallas_call(
        body, name=name, grid=(s // ts,),
        in_specs=[row, row, vec], out_specs=row,
        out_shape=jax.ShapeDtypeStruct((s, d), F32),
    )(x, out, g)


def _rows_before(ref, start, n, halo):
    if start == 0:
        return jnp.concatenate([jnp.zeros((halo, ref.shape[1]), F32), ref[0:n, :]], axis=0)
    return ref[start - halo:start + n, :]


def _rows_after(ref, start, n, halo):
    if start + n == ref.shape[0]:
        return jnp.concatenate([ref[start:start + n, :].astype(F32), jnp.zeros((halo, ref.shape[1]), F32)], axis=0)
    return ref[start:start + n + halo, :].astype(F32)


def _pick_window(group, s2, s4, s8, s16):
    return jnp.where(group == 0, s2, jnp.where(group == 1, s4, jnp.where(group == 2, s8, s16)))


def _trailing_sums(ext, group):
    s2 = ext + pltpu.roll(ext, 1, 0)
    s4 = s2 + pltpu.roll(s2, 2, 0)
    s8 = s4 + pltpu.roll(s4, 4, 0)
    s16 = s8 + pltpu.roll(s8, 8, 0)
    return _pick_window(group, s2, s4, s8, s16)


def _leading_sums(ext, group):
    n = ext.shape[0]
    s2 = ext + pltpu.roll(ext, n - 1, 0)
    s4 = s2 + pltpu.roll(s2, n - 2, 0)
    s8 = s4 + pltpu.roll(s4, n - 4, 0)
    s16 = s8 + pltpu.roll(s8, n - 8, 0)
    return _pick_window(group, s2, s4, s8, s16)


def _window_count(start, n, group):
    pos = start + lax.broadcasted_iota(jnp.int32, (n, LANE), 0)
    return jnp.minimum(pos + 1, 2 << group).astype(F32)


def _pooled(v_ref, start, n, group):
    ext = _rows_before(v_ref, start, n, POOL_HALO)
    sums = _trailing_sums(ext, group)[POOL_HALO:, :]
    return sums / _window_count(start, n, group) - ext[POOL_HALO:, :]


def _pool_fwd(u, pool_w, pool_scale, name, ts=512):
    s = u.shape[0]

    def body(v_ref, gate_ref, w_ref, sc_ref, y_ref):
        group = pl.program_id(0)
        for c in range(s // ts):
            a = c * ts
            pooled = _pooled(v_ref, a, ts, group)
            mixed = _dot(pooled.astype(BF16), w_ref[...], NN)
            gate = gate_ref[a:a + ts, :]
            y_ref[a:a + ts, :] = (mixed * sc_ref[...] * (gate * _sigmoid(gate))).astype(BF16)

    col = lambda base: pl.BlockSpec((s, LANE), lambda g: (0, base + g))
    return pl.pallas_call(
        body, name=name, grid=(4,),
        in_specs=[col(COL_POOL_V), col(COL_POOL_G),
                  pl.BlockSpec((None, LANE, LANE), lambda g: (g, 0, 0)),
                  pl.BlockSpec((1, LANE), lambda g: (0, g))],
        out_specs=pl.BlockSpec((s, LANE), lambda g: (0, g)),
        out_shape=jax.ShapeDtypeStruct((s, WIDTH), BF16),
    )(u, u, pool_w, pool_scale)


def _pool_bwd(u, dy, pool_w, pool_scale, name, ts=512):
    s = u.shape[0]

    def body(v_ref, gate_ref, dy_ref, w_ref, sc_ref, dv_ref, dgate_ref, dw_ref, dsc_ref):
        group = pl.program_id(0)
        w = w_ref[...]
        scale = sc_ref[...]
        dw = jnp.zeros((LANE, LANE), F32)
        dsc = jnp.zeros((1, LANE), F32)
        for c in range(s // ts):
            a = c * ts
            n_ext = ts + POOL_HALO
            gate_e = _rows_after(gate_ref, a, ts, POOL_HALO)
            dy_e = _rows_after(dy_ref, a, ts, POOL_HALO)
            silu_e, dsilu_e = _silu_and_grad(gate_e)
            dms_e = dy_e * silu_e
            dm_e = (dms_e * scale).astype(BF16)
            dpool_e = _dot(dm_e, w, NT)
            spread = _leading_sums(dpool_e / _window_count(a, n_ext, group), group)
            dv_ref[a:a + ts, :] = (spread[0:ts, :] - dpool_e[0:ts, :]).astype(BF16)
            pooled = _pooled(v_ref, a, ts, group).astype(BF16)
            mixed = _dot(pooled, w, NN)
            dgate_ref[a:a + ts, :] = (dy_e[0:ts, :] * mixed * scale * dsilu_e[0:ts, :]).astype(BF16)
            dsc = dsc + jnp.sum(dms_e[0:ts, :] * mixed, axis=0, keepdims=True)
            dw = dw + _dot(pooled, dm_e[0:ts, :], TN)
        dw_ref[...] = dw
        dsc_ref[...] = dsc

    col = lambda base: pl.BlockSpec((s, LANE), lambda g: (0, base + g))
    out_col = pl.BlockSpec((s, LANE), lambda g: (0, g))
    return pl.pallas_call(
        body, name=name, grid=(4,),
        in_specs=[col(COL_POOL_V), col(COL_POOL_G), out_col,
                  pl.BlockSpec((None, LANE, LANE), lambda g: (g, 0, 0)),
                  pl.BlockSpec((1, LANE), lambda g: (0, g))],
        out_specs=[out_col, out_col,
                   pl.BlockSpec((None, LANE, LANE), lambda g: (g, 0, 0)),
                   pl.BlockSpec((1, LANE), lambda g: (0, g))],
        out_shape=[jax.ShapeDtypeStruct((s, WIDTH), BF16), jax.ShapeDtypeStruct((s, WIDTH), BF16),
                   jax.ShapeDtypeStruct((4, LANE, LANE), F32), jax.ShapeDtypeStruct((1, WIDTH), F32)],
    )(u, u, dy, pool_w, pool_scale)


def _conv_taps(x_ref, gc_ref, start, n):
    z_ext = _rows_before(gc_ref, start, n, CONV_HALO) * _rows_before(x_ref, start, n, CONV_HALO)
    z0 = z_ext[CONV_HALO:, :]
    z1 = pltpu.roll(z_ext, 1, 0)[CONV_HALO:, :]
    z2 = pltpu.roll(z_ext, 2, 0)[CONV_HALO:, :]
    return z0, z1, z2


def _conv_fwd(u, conv_w, conv_b, name, ts=512):
    s = u.shape[0]

    def body(x_ref, gb_ref, gc_ref, g_ref, w_ref, b_ref, y_ref):
        w0, w1, w2 = w_ref[0:1, :], w_ref[1:2, :], w_ref[2:3, :]
        for c in range(s // ts):
            a = c * ts
            z0, z1, z2 = _conv_taps(x_ref, gc_ref, a, ts)
            y = w2 * z0 + w1 * z1 + w0 * z2 + b_ref[...]
            gate = g_ref[a:a + ts, :]
            y_ref[a:a + ts, :] = (gb_ref[a:a + ts, :] * y * (gate * _sigmoid(gate))).astype(BF16)

    col = lambda base: pl.BlockSpec((s, LANE), lambda j: (0, base + j))
    return pl.pallas_call(
        body, name=name, grid=(4,),
        in_specs=[col(COL_CONV_X), col(COL_CONV_GB), col(COL_CONV_GC), col(COL_CONV_G),
                  pl.BlockSpec((3, LANE), lambda j: (0, j)), pl.BlockSpec((1, LANE), lambda j: (0, j))],
        out_specs=pl.BlockSpec((s, LANE), lambda j: (0, j)),
        out_shape=jax.ShapeDtypeStruct((s, WIDTH), BF16),
    )(u, u, u, u, conv_w, conv_b)


def _conv_bwd(u, dy, conv_w, conv_b, name, ts=512):
    s = u.shape[0]

    def body(x_ref, gb_ref, gc_ref, g_ref, dy_ref, w_ref, b_ref,
             dx_ref, dgb_ref, dgc_ref, dg_ref, dw_ref, db_ref):
        w0, w1, w2 = w_ref[0:1, :], w_ref[1:2, :], w_ref[2:3, :]
        acc = [jnp.zeros((1, LANE), F32) for _ in range(4)]
        for c in range(s // ts):
            a = c * ts
            n_ext = ts + CONV_HALO
            gate_e = _rows_after(g_ref, a, ts, CONV_HALO)
            silu_e, dsilu_e = _silu_and_grad(gate_e)
            dy_e = _rows_after(dy_ref, a, ts, CONV_HALO)
            gb_e = _rows_after(gb_ref, a, ts, CONV_HALO)
            dyy_e = dy_e * silu_e * gb_e
            dz = (w2 * dyy_e + w1 * pltpu.roll(dyy_e, n_ext - 1, 0) + w0 * pltpu.roll(dyy_e, n_ext - 2, 0))[0:ts, :]
            z0, z1, z2 = _conv_taps(x_ref, gc_ref, a, ts)
            yb = w2 * z0 + w1 * z1 + w0 * z2 + b_ref[...]
            dyv = dy_e[0:ts, :]
            dyy = dyy_e[0:ts, :]
            dg_ref[a:a + ts, :] = (dyv * gb_e[0:ts, :] * yb * dsilu_e[0:ts, :]).astype(BF16)
            dgb_ref[a:a + ts, :] = (dyv * silu_e[0:ts, :] * yb).astype(BF16)
            dx_ref[a:a + ts, :] = (dz * gc_ref[a:a + ts, :]).astype(BF16)
            dgc_ref[a:a + ts, :] = (dz * x_ref[a:a + ts, :]).astype(BF16)
            for i, term in enumerate((dyy * z2, dyy * z1, dyy * z0, dyy)):
                acc[i] = acc[i] + jnp.sum(term, axis=0, keepdims=True)
        dw_ref[0:1, :] = acc[0]
        dw_ref[1:2, :] = acc[1]
        dw_ref[2:3, :] = acc[2]
        db_ref[...] = acc[3]

    col = lambda base: pl.BlockSpec((s, LANE), lambda j: (0, base + j))
    out_col = pl.BlockSpec((s, LANE), lambda j: (0, j))
    big = jax.ShapeDtypeStruct((s, WIDTH), BF16)
    return pl.pallas_call(
        body, name=name, grid=(4,),
        in_specs=[col(COL_CONV_X), col(COL_CONV_GB), col(COL_CONV_GC), col(COL_CONV_G), out_col,
                  pl.BlockSpec((3, LANE), lambda j: (0, j)), pl.BlockSpec((1, LANE), lambda j: (0, j))],
        out_specs=[out_col, out_col, out_col, out_col,
                   pl.BlockSpec((3, LANE), lambda j: (0, j)), pl.BlockSpec((1, LANE), lambda j: (0, j))],
        out_shape=[big, big, big, big,
                   jax.ShapeDtypeStruct((3, WIDTH), F32), jax.ShapeDtypeStruct((1, WIDTH), F32)],
    )(u, u, u, u, dy, conv_w, conv_b)


LOG2_E = 1.4426950408889634
LN_2 = 0.6931471805599453


def _sb_scores(q_h, k_blk, valid, later_mat, carry):
    z = _dot(q_h, k_blk, NT)
    neg_z = -z
    soft = jnp.log(1.0 + jnp.exp2(jnp.minimum(z, neg_z))) * LOG2_E
    log_keep = jnp.minimum(neg_z, 0.0) - soft
    log_beta = log_keep + z
    if valid is not None:
        log_keep = jnp.where(valid, log_keep, 0.0)
    later = _dot(log_keep.astype(BF16), later_mat, NN) + carry
    return log_keep, log_beta, later


def _masked(valid, x):
    return x if valid is None else jnp.where(valid, x, 0.0)


def _diagonal_masks(tq, tk):
    r = lax.broadcasted_iota(jnp.int32, (tq, tk), 0)
    cidx = lax.broadcasted_iota(jnp.int32, (tq, tk), 1)
    return [cidx + d * tk < r for d in range(tq // tk)]


def _triangle(tk, op):
    r = lax.broadcasted_iota(jnp.int32, (tk, tk), 0)
    cidx = lax.broadcasted_iota(jnp.int32, (tk, tk), 1)
    return op(r, cidx).astype(BF16)


def _split_refs(refs, n_in, n_out, n_scratch, rider):
    r_in = len(rider.inputs) if rider else 0
    r_out = len(rider.out_shape) if rider else 0
    a, b = n_in + r_in, n_in + r_in + n_out + r_out
    own = refs[:n_in] + refs[a:a + n_out] + refs[b:b + n_scratch]
    return own, (refs[n_in:a], refs[a + n_out:b], refs[b + n_scratch:])


def _rider_call_args(rider, n_in, n_out):
    if rider is None:
        return dict(in_specs=[], out_specs=[], out_shape=[], aliases={}, scratch=[], inputs=[])
    return dict(in_specs=[ANY] * len(rider.inputs), out_specs=[ANY] * len(rider.out_shape),
                out_shape=list(rider.out_shape), scratch=list(rider.scratch), inputs=list(rider.inputs),
                aliases={n_in + a: n_out + b for a, b in rider.aliases.items()})


def _ride(rider, phase, when, parts):
    fn = getattr(rider, phase) if rider else None
    if fn is not None:
        pl.when(when)(lambda: fn(*parts))


def _sb_fwd(u, name, t=512, tk=256, pairs=4, rider=None):
    s = u.shape[0]
    assert s // tk <= LANE and 4 % pairs == 0 and t % tk == 0
    scale = HEAD_DIM ** -0.5
    nh = 2 * pairs
    wide = pairs * LANE
    ratio = t // tk
    groups, nq = 4 // pairs, s // t

    def body(*refs):
        own, riding = _split_refs(refs, 4, 3, 4, rider)
        q_ref, k_ref, v_ref, g_ref, o_ref, y_ref, after_ref, kb_ref, vb_ref, acc_ref, carry_ref = own
        grp = pl.program_id(0)
        i = pl.program_id(1)
        _ride(rider, "start", jnp.logical_and(grp == 0, i == 0), riding)
        _ride(rider, "middle", jnp.logical_and(grp == groups - 1, i == (3 * nq) // 4), riding)

        @pl.when(i == 0)
        def _():
            kb_ref[...] = k_ref[...].astype(BF16)
            vb_ref[...] = v_ref[...].astype(BF16)

        lane = lax.broadcasted_iota(jnp.int32, (t, LANE), 1)
        first = lane < HEAD_DIM
        after_ref[...] = jnp.zeros_like(after_ref)
        qv = q_ref[...] * (scale * LOG2_E)
        q_heads = []
        for p in range(pairs):
            qp = qv[:, p * LANE:(p + 1) * LANE]
            q_heads += [jnp.where(first, qp, 0.0).astype(BF16), jnp.where(first, 0.0, qp).astype(BF16)]
        later_mat = _triangle(tk, lambda r, cidx: r > cidx)
        acc_ref[...] = jnp.zeros_like(acc_ref)
        carry_ref[...] = jnp.zeros_like(carry_ref)

        def block(kb, valid, lo=0):
            rows = pl.ds(pl.multiple_of(kb * tk, tk), tk)
            k_blk = kb_ref[rows, :]
            v_blk = vb_ref[rows, :]
            carries = [carry_ref[h, lo:, :] for h in range(nh)]
            afters = [after_ref[lo:, h * LANE:(h + 1) * LANE] for h in range(nh)]
            accs = [acc_ref[h, lo:, :] for h in range(nh)]
            outs = []
            for h in range(nh):
                cols = slice((h // 2) * LANE, (h // 2 + 1) * LANE)
                log_keep, log_beta, later = _sb_scores(q_heads[h][lo:], k_blk[:, cols], valid, later_mat, carries[h])
                a = _masked(valid, jnp.exp2(log_beta + later))
                outs.append((accs[h] + _dot(a.astype(BF16), v_blk[:, cols], NN),
                             carries[h] + jnp.sum(log_keep, axis=1, keepdims=True),
                             jnp.where(lane[lo:] == kb, carries[h], afters[h])))
            for h in range(nh):
                acc_ref[h, lo:, :] = outs[h][0]
                carry_ref[h, lo:, :] = outs[h][1]
                after_ref[lo:, h * LANE:(h + 1) * LANE] = outs[h][2]

        def step(j, _):
            block(ratio * i - 1 - j, None)
            return 0

        masks = _diagonal_masks(t, tk)
        for d in reversed(range(ratio)):
            block(ratio * i + d, masks[d][d * tk:], d * tk)
        lax.fori_loop(0, ratio * i, step, 0)
        for p in range(pairs):
            cols = slice(p * LANE, (p + 1) * LANE)
            o = jnp.where(first, acc_ref[2 * p], acc_ref[2 * p + 1])
            o_ref[:, cols] = o
            gate = g_ref[:, cols]
            y_ref[:, cols] = (o * gate * _sigmoid(gate)).astype(BF16)
        _ride(rider, "finish", jnp.logical_and(grp == groups - 1, i == nq - 1), riding)

    blk = lambda base: pl.BlockSpec((t, wide), lambda g, i: (i, base // pairs + g))
    full = lambda base: pl.BlockSpec((s, wide), lambda g, i: (0, base // pairs + g))
    out_blk = pl.BlockSpec((t, wide), lambda g, i: (i, g))
    extra = _rider_call_args(rider, 4, 3)
    outs = pl.pallas_call(
        body, name=name, grid=(groups, nq),
        in_specs=[blk(COL_SB_Q), full(COL_SB_K), full(COL_SB_V), blk(COL_SB_G)] + extra["in_specs"],
        out_specs=[out_blk, out_blk, pl.BlockSpec((t, nh * LANE), lambda g, i: (i, g))] + extra["out_specs"],
        out_shape=[jax.ShapeDtypeStruct((s, WIDTH), F32), jax.ShapeDtypeStruct((s, WIDTH), BF16),
                   jax.ShapeDtypeStruct((s, 8 * LANE), F32)] + extra["out_shape"],
        input_output_aliases=extra["aliases"],
        scratch_shapes=[pltpu.VMEM((s, wide), BF16), pltpu.VMEM((s, wide), BF16),
                        pltpu.VMEM((nh, t, LANE), F32), pltpu.VMEM((nh, t, 1), F32)] + extra["scratch"],
        compiler_params=pltpu.CompilerParams(dimension_semantics=("arbitrary", "arbitrary")),
    )(u, u, u, u, *extra["inputs"])
    return outs[:3], outs[3:]


def _sb_bwd(u, o, after, dy, name, t=512, tk=256, pairs=2, rider=None):
    s = u.shape[0]
    nq = s // t
    scale = HEAD_DIM ** -0.5
    nh = 2 * pairs
    wide = pairs * LANE
    ratio = t // tk
    groups = 4 // pairs

    def body(*refs):
        own, riding = _split_refs(refs, 7, 4, 6, rider)
        (q_ref, k_ref, v_ref, g_ref, o_ref, after_ref, dy_ref, dq_ref, dk_ref, dv_ref, dg_ref,
         kb_ref, vb_ref, dk_acc, dv_acc, dq_acc, carry_ref) = own
        grp = pl.program_id(0)
        i = pl.program_id(1)
        _ride(rider, "start", jnp.logical_and(grp == 0, i == 0), riding)

        @pl.when(i == 0)
        def _():
            kb_ref[...] = k_ref[...].astype(BF16)
            vb_ref[...] = v_ref[...].astype(BF16)
            dk_acc[...] = jnp.zeros_like(dk_acc)
            dv_acc[...] = jnp.zeros_like(dv_acc)

        lane = lax.broadcasted_iota(jnp.int32, (t, LANE), 1)
        first = lane < HEAD_DIM
        gate = g_ref[...]
        silu, dsilu = _silu_and_grad(gate)
        dyv = dy_ref[...]
        do = dyv * silu
        dg_ref[...] = (dyv * o_ref[...] * dsilu).astype(BF16)
        qv = q_ref[...] * (scale * LOG2_E)
        do_heads, q_heads = [], []
        for p in range(pairs):
            cols = slice(p * LANE, (p + 1) * LANE)
            do_heads += [jnp.where(first, do[:, cols], 0.0).astype(BF16), jnp.where(first, 0.0, do[:, cols]).astype(BF16)]
            q_heads += [jnp.where(first, qv[:, cols], 0.0).astype(BF16), jnp.where(first, 0.0, qv[:, cols]).astype(BF16)]
        later_mat = _triangle(tk, lambda r, cidx: r > cidx)
        before_mat = _triangle(tk, lambda r, cidx: r < cidx)
        dq_acc[...] = jnp.zeros_like(dq_acc)
        carry_ref[...] = jnp.zeros_like(carry_ref)

        def block(kb, valid, lo=0):
            rows = pl.ds(pl.multiple_of(kb * tk, tk), tk)
            k_blk = kb_ref[rows, :]
            v_blk = vb_ref[rows, :]
            carries = [carry_ref[h, lo:, :] for h in range(nh)]
            dq_old = [dq_acc[h, lo:, :] for h in range(nh)]
            dk_old = dk_acc[rows, :]
            dv_old = dv_acc[rows, :]
            outs = []
            for h in range(nh):
                cols = slice((h // 2) * LANE, (h // 2 + 1) * LANE)
                q_h, do_h = q_heads[h][lo:], do_heads[h][lo:]
                after = jnp.sum(jnp.where(lane[lo:] == kb, after_ref[lo:, h * LANE:(h + 1) * LANE], 0.0), axis=1,
                                keepdims=True)
                _, log_beta, later = _sb_scores(q_h, k_blk[:, cols], valid, later_mat, after)
                beta = jnp.exp2(log_beta)
                a = _masked(valid, jnp.exp2(log_beta + later))
                da = _dot(do_h, v_blk[:, cols], NT)
                gterm = a * da
                before = _dot(gterm.astype(BF16), before_mat, NN) + carries[h]
                dz_b = _masked(valid, gterm * (1.0 - beta) - beta * before).astype(BF16)
                outs.append((dq_old[h] + _dot(dz_b, k_blk[:, cols], NN), _dot(dz_b, q_h, TN),
                             _dot(a.astype(BF16), do_h, TN),
                             carries[h] + jnp.sum(gterm, axis=1, keepdims=True)))
            for h in range(nh):
                dq_acc[h, lo:, :] = outs[h][0]
                carry_ref[h, lo:, :] = outs[h][3]
            dk_new = [outs[2 * p][1] + outs[2 * p + 1][1] for p in range(pairs)]
            dv_new = [outs[2 * p][2] + outs[2 * p + 1][2] for p in range(pairs)]
            dk_acc[rows, :] = dk_old + (dk_new[0] if pairs == 1 else jnp.concatenate(dk_new, axis=1))
            dv_acc[rows, :] = dv_old + (dv_new[0] if pairs == 1 else jnp.concatenate(dv_new, axis=1))

        def step(kb, _):
            block(kb, None)
            return 0

        lax.fori_loop(0, ratio * i, step, 0)
        masks = _diagonal_masks(t, tk)
        for d in range(ratio):
            block(ratio * i + d, masks[d][d * tk:], d * tk)
        for p in range(pairs):
            dq_ref[:, p * LANE:(p + 1) * LANE] = (jnp.where(first, dq_acc[2 * p], dq_acc[2 * p + 1]) * scale).astype(BF16)

        @pl.when(i == nq - 1)
        def _():
            dk_ref[...] = (dk_acc[...] * LN_2).astype(BF16)
            dv_ref[...] = dv_acc[...].astype(BF16)

        _ride(rider, "finish", jnp.logical_and(grp == groups - 1, i == nq - 1), riding)

    blk = lambda base: pl.BlockSpec((t, wide), lambda g, i: (i, base // pairs + g))
    full = lambda base: pl.BlockSpec((s, wide), lambda g, i: (0, base // pairs + g))
    out_blk = pl.BlockSpec((t, wide), lambda g, i: (i, g))
    out_full = pl.BlockSpec((s, wide), lambda g, i: (0, g))
    big = jax.ShapeDtypeStruct((s, WIDTH), BF16)
    extra = _rider_call_args(rider, 7, 4)
    outs = pl.pallas_call(
        body, name=name, grid=(groups, nq),
        in_specs=[blk(COL_SB_Q), full(COL_SB_K), full(COL_SB_V), blk(COL_SB_G), out_blk,
                  pl.BlockSpec((t, nh * LANE), lambda g, i: (i, g)), out_blk] + extra["in_specs"],
        out_specs=[out_blk, out_full, out_full, out_blk] + extra["out_specs"],
        out_shape=[big, big, big, big] + extra["out_shape"],
        input_output_aliases=extra["aliases"],
        scratch_shapes=[pltpu.VMEM((s, wide), BF16), pltpu.VMEM((s, wide), BF16),
                        pltpu.VMEM((s, wide), F32), pltpu.VMEM((s, wide), F32),
                        pltpu.VMEM((nh, t, LANE), F32), pltpu.VMEM((nh, t, 1), F32)] + extra["scratch"],
        compiler_params=pltpu.CompilerParams(dimension_semantics=("arbitrary", "arbitrary")),
    )(u, u, u, u, o, after, dy, *extra["inputs"])
    return outs[:4], outs[4:]


def _gate_fwd(u, ys, w_branch, name, ts=256):
    s = u.shape[0]

    def body(m0, m1, m2, y0, y1, y2, w_ref, p0, p1, p2, out_ref):
        tot = None
        for n, (m_ref, y_ref, p_ref) in enumerate(((m0, y0, p0), (m1, y1, p1), (m2, y2, p2))):
            proj = _dot(y_ref[...], w_ref[n], NN)
            p_ref[...] = proj.astype(BF16)
            term = _sigmoid(m_ref[...]) * proj
            tot = term if tot is None else tot + term
        out_ref[...] = tot.astype(BF16)

    mspec = lambda n: pl.BlockSpec((ts, D_MODEL), lambda i: (i, COL_MERGE_1024 + n))
    row = pl.BlockSpec((ts, D_MODEL), lambda i: (i, 0))
    yspec = pl.BlockSpec((ts, WIDTH), lambda i: (i, 0))
    big = jax.ShapeDtypeStruct((s, D_MODEL), BF16)
    outs = pl.pallas_call(
        body, name=name, grid=(s // ts,),
        in_specs=[mspec(0), mspec(1), mspec(2), yspec, yspec, yspec,
                  pl.BlockSpec(w_branch.shape, lambda i: (0, 0, 0))],
        out_specs=[row] * 4, out_shape=[big] * 4,
    )(u, u, u, *ys, w_branch)
    return outs[:3], outs[3]


def _gate_bwd(u, projs, dmerged, w_branch, name, ts=256, rider=None):
    s = u.shape[0]
    steps = s // ts

    def body(*refs):
        own, riding = _split_refs(refs, 8, 9, 0, rider)
        m0, m1, m2, p0, p1, p2, dm_ref, w_ref, dp0, dp1, dp2, dl0, dl1, dl2, dy0, dy1, dy2 = own
        _ride(rider, "start", pl.program_id(0) == 0, riding)
        dm = dm_ref[...].astype(F32)
        for n, (m_ref, p_ref, dp_ref, dl_ref, dy_ref) in enumerate(((m0, p0, dp0, dl0, dy0), (m1, p1, dp1, dl1, dy1),
                                                                    (m2, p2, dp2, dl2, dy2))):
            gate = _sigmoid(m_ref[...])
            dp = (dm * gate).astype(BF16)
            dp_ref[...] = dp
            dl_ref[...] = (dm * p_ref[...].astype(F32) * gate * (1.0 - gate)).astype(BF16)
            dy_ref[...] = _dot(dp, w_ref[n], NT)
        _ride(rider, "finish", pl.program_id(0) == steps - 1, riding)

    mspec = lambda n: pl.BlockSpec((ts, D_MODEL), lambda i: (i, COL_MERGE_1024 + n))
    row = pl.BlockSpec((ts, D_MODEL), lambda i: (i, 0))
    yspec = pl.BlockSpec((ts, WIDTH), lambda i: (i, 0))
    big = jax.ShapeDtypeStruct((s, D_MODEL), BF16)
    extra = _rider_call_args(rider, 8, 9)
    outs = pl.pallas_call(
        body, name=name, grid=(steps,),
        in_specs=[mspec(0), mspec(1), mspec(2), row, row, row, row,
                  pl.BlockSpec(w_branch.shape, lambda i: (0, 0, 0))] + extra["in_specs"],
        out_specs=[row] * 6 + [yspec] * 3 + extra["out_specs"],
        out_shape=[big] * 6 + [jax.ShapeDtypeStruct((s, WIDTH), F32)] * 3 + extra["out_shape"],
        input_output_aliases=extra["aliases"], scratch_shapes=extra["scratch"],
        compiler_params=pltpu.CompilerParams(dimension_semantics=("arbitrary",)),
    )(u, u, u, *projs, dmerged, w_branch, *extra["inputs"])
    return outs[:3], outs[3:6], outs[6:9], outs[9:]


def _as_rows(a):
    return a.reshape(-1, a.shape[-1])


def _row_tile(rows, cols, bytes_per_row_elem=4, cap=1 << 20):
    tr = rows
    while tr * cols * bytes_per_row_elem > cap and tr % 2 == 0 and (tr // 2) % 16 == 0:
        tr //= 2
    return tr


def _cast_bf16(a, name):
    a2 = _as_rows(a)
    rows, cols = a2.shape
    tr = _row_tile(rows, cols)

    def body(a_ref, o_ref):
        o_ref[...] = a_ref[...].astype(BF16)

    spec = pl.BlockSpec((tr, cols), lambda i: (i, 0))
    out = pl.pallas_call(body, name=name, grid=(rows // tr,), in_specs=[spec], out_specs=spec,
                         out_shape=jax.ShapeDtypeStruct((rows, cols), BF16))(a2)
    return out.reshape(a.shape)


def _adamw(w, g, m, v, name):
    shape = w.shape
    w2, g2, m2, v2 = (_as_rows(a) for a in (w, g, m, v))
    rows, cols = w2.shape
    tr = _row_tile(rows, cols)
    c1 = 1.0 - ADAM_B1 ** ADAM_STEP
    c2 = 1.0 - ADAM_B2 ** ADAM_STEP

    def body(w_ref, g_ref, m_ref, v_ref, d_ref, nm_ref, nv_ref):
        gv = g_ref[...]
        nm = ADAM_B1 * m_ref[...] + (1.0 - ADAM_B1) * gv
        nv = ADAM_B2 * v_ref[...] + (1.0 - ADAM_B2) * (gv * gv)
        nm_ref[...] = nm
        nv_ref[...] = nv
        d_ref[...] = -ADAM_LR * ((nm / c1) / (jnp.sqrt(nv / c2) + ADAM_EPS) + ADAM_WD * w_ref[...])

    spec = pl.BlockSpec((tr, cols), lambda i: (i, 0))
    sds = jax.ShapeDtypeStruct((rows, cols), F32)
    outs = pl.pallas_call(body, name=name, grid=(rows // tr,), in_specs=[spec] * 4, out_specs=[spec] * 3,
                          out_shape=[sds] * 3)(w2, g2, m2, v2)
    return tuple(o.reshape(shape) for o in outs)


def _sum_slots(a, out_dtype, name):
    n = a.shape[0]
    a3 = a.reshape(n, -1, a.shape[-1])
    _, rows, cols = a3.shape
    tr = _row_tile(rows, cols * n)

    def body(a_ref, o_ref):
        tot = a_ref[0].astype(F32)
        for k in range(1, n):
            tot = tot + a_ref[k].astype(F32)
        o_ref[...] = tot.astype(out_dtype)

    out = pl.pallas_call(
        body, name=name, grid=(rows // tr,),
        in_specs=[pl.BlockSpec((n, tr, cols), lambda i: (0, i, 0))],
        out_specs=pl.BlockSpec((tr, cols), lambda i: (i, 0)),
        out_shape=jax.ShapeDtypeStruct((rows, cols), out_dtype))(a3)
    return out.reshape(a.shape[1:])


def _chip_sum(own, recv, axis, core, name):
    half = recv.shape
    nd = len(half)
    last = nd - 1
    if axis == last:
        tl, nt = half[last], 1
    else:
        tl = min(half[last], 2048)
        nt = half[last] // tl
    block = half[:last] + (tl,)

    def own_index(i, core_ref):
        idx = [0] * nd
        idx[last] = i
        if axis == last:
            idx[last] = core_ref[0]
        else:
            idx[axis] = core_ref[0]
        return tuple(idx)

    def recv_index(i, core_ref):
        idx = [0] * nd
        idx[last] = i
        return tuple(idx)

    def body(core_ref, own_ref, recv_ref, o_ref):
        o_ref[...] = (own_ref[...] + recv_ref[...]).astype(BF16)

    return pl.pallas_call(
        body, name=name,
        grid_spec=pltpu.PrefetchScalarGridSpec(
            num_scalar_prefetch=1, grid=(nt,),
            in_specs=[pl.BlockSpec(block, own_index), pl.BlockSpec(block, recv_index)],
            out_specs=pl.BlockSpec(block, recv_index)),
        out_shape=jax.ShapeDtypeStruct(half, BF16),
    )(core, own, recv)


def _mesh_position():
    return lax.axis_index("x"), lax.axis_index("y"), lax.axis_index("c")


def _other_chips(x, y):
    return [(1 - x, y), (x, 1 - y), (1 - x, 1 - y)]


ALL_FLIPS = [(0, 0, 1), (1, 0, 0), (0, 1, 0), (1, 1, 0), (1, 0, 1), (0, 1, 1), (1, 1, 1)]


def _half(ref, axis, which, size):
    idx = [slice(None)] * len(ref.shape)
    idx[axis] = pl.ds(which * size, size)
    return ref.at[tuple(idx)]


def _sub(ref, picks):
    idx = [slice(None)] * len(ref.shape)
    for axis, start, size in picks:
        idx[axis] = pl.ds(start, size)
    return ref.at[tuple(idx)]


def _remote(src, dst, sems_send, sems_recv, k, to):
    return pltpu.make_async_remote_copy(src_ref=src, dst_ref=dst, send_sem=sems_send.at[k], recv_sem=sems_recv.at[k],
                                        device_id=to, device_id_type=MESH)


def _cast_shard(w, layer, shard_axis, pos, name, tr=512):
    shape = w.shape[1:]
    nd = len(shape)
    assert shard_axis in (nd - 1, nd - 2)
    rows, cols = shape[-2:]
    tr = min(tr, rows)
    nt = rows // tr
    lead = shape[:-2]
    full = list(shape)
    full[shard_axis] *= N_CHIPS
    block = (1,) * len(lead) + (tr, cols)

    def in_index(*args):
        return (layer, *args[:-1], 0)

    def out_index(*args):
        *g, pos_ref = args
        if shard_axis == nd - 1:
            return (*g, pos_ref[1])
        return (*g[:-1], pos_ref[1] * nt + g[-1], 0)

    def body(pos_ref, a_ref, o_ref):
        o_ref[...] = a_ref[...].astype(BF16)

    return pl.pallas_call(
        body, name=name,
        grid_spec=pltpu.PrefetchScalarGridSpec(
            num_scalar_prefetch=1, grid=lead + (nt,),
            in_specs=[pl.BlockSpec((None,) + block, in_index)], out_specs=pl.BlockSpec(block, out_index)),
        out_shape=jax.ShapeDtypeStruct(tuple(full), BF16),
    )(pos, w)


class _Rider:
    def __init__(self, inputs, out_shape, aliases, scratch, start, middle, finish):
        self.inputs, self.out_shape, self.aliases, self.scratch = inputs, out_shape, aliases, scratch
        self.start, self.middle, self.finish = start, middle, finish


def _weight_gather_rider(fulls, layout):
    n = len(fulls)

    def copies(outs, sems):
        send_sems, recv_sems = sems
        x, y, c = _mesh_position()
        chips = _other_chips(x, y)
        sibling = (x, y, 1 - c)
        mine = 2 * x + y

        def place(t, chip, core):
            sh_axis, sh_size, half_axis, half_size = layout[t]
            return _sub(outs[t], [(sh_axis, chip * sh_size, sh_size), (half_axis, core * half_size, half_size)])

        direct, arrive, forward, arrive_fwd = [], [], [], []
        for t in range(n):
            for k, (px, py) in enumerate(chips):
                theirs = 2 * px + py
                direct.append(_remote(place(t, mine, c), place(t, mine, c), send_sems, recv_sems, 6 * t + k, (px, py, c)))
                arrive.append(_remote(place(t, theirs, c), place(t, theirs, c), send_sems, recv_sems, 6 * t + k, (px, py, c)))
                forward.append(_remote(place(t, theirs, c), place(t, theirs, c), send_sems, recv_sems, 6 * t + 3 + k, sibling))
                arrive_fwd.append(_remote(place(t, theirs, 1 - c), place(t, theirs, 1 - c), send_sems, recv_sems,
                                          6 * t + 3 + k, sibling))
        return direct, arrive, forward, arrive_fwd

    def start(ins, outs, sems):
        for cp in copies(outs, sems)[0]:
            cp.start()

    def middle(ins, outs, sems):
        _, arrive, forward, _ = copies(outs, sems)
        for a, f in zip(arrive, forward):
            a.wait_recv()
            f.start()

    def finish(ins, outs, sems):
        direct, _, forward, arrive_fwd = copies(outs, sems)
        for cp in arrive_fwd:
            cp.wait_recv()
        for cp in direct + forward:
            cp.wait_send()

    return _Rider(list(fulls), [jax.ShapeDtypeStruct(a.shape, a.dtype) for a in fulls], {k: k for k in range(n)},
                  [pltpu.SemaphoreType.DMA((6 * n,)), pltpu.SemaphoreType.DMA((6 * n,))], start, middle, finish)


WEIGHT_LAYOUT = [(1, 2048, 0, 512), (2, 256, 1, 256), (0, 256, 1, 512)]


def _gather_weights(fulls, conv_w):
    rider = _weight_gather_rider(fulls, WEIGHT_LAYOUT)
    n = len(fulls)

    def body(*refs):
        cw, outs, cw_f = refs[n], refs[n + 1:2 * n + 1], refs[2 * n + 1]
        sems, (cw_send, cw_recv, local_sem) = refs[2 * n + 2:2 * n + 4], refs[2 * n + 4:]
        x, y, c = _mesh_position()
        chips = _other_chips(x, y)
        mine = 2 * x + y
        local = pltpu.make_async_copy(cw, cw_f.at[mine], local_sem.at[0])
        local.start()
        rider.start(None, outs, sems)
        small = [_remote(cw, cw_f.at[mine], cw_send, cw_recv, k, (*chip, c)) for k, chip in enumerate(chips)]
        for cp in small:
            cp.start()
        rider.middle(None, outs, sems)
        rider.finish(None, outs, sems)
        for k, (px, py) in enumerate(chips):
            _remote(cw, cw_f.at[2 * px + py], cw_send, cw_recv, k, (px, py, c)).wait_recv()
        for cp in small:
            cp.wait_send()
        local.wait()

    outs = pl.pallas_call(
        body, name="gather_weights",
        in_specs=[ANY] * (n + 1), out_specs=[ANY] * (n + 1),
        out_shape=rider.out_shape + [jax.ShapeDtypeStruct((N_CHIPS,) + conv_w.shape, F32)],
        input_output_aliases=rider.aliases,
        scratch_shapes=rider.scratch + [pltpu.SemaphoreType.DMA((3,)), pltpu.SemaphoreType.DMA((3,)),
                                        pltpu.SemaphoreType.DMA((1,))],
    )(*fulls, conv_w)
    return outs[:n], outs[n]


def _swap_rider(items):
    n = len(items)
    halves = []
    for a, axis in items:
        shp = list(a.shape)
        shp[axis] //= 2
        halves.append(tuple(shp))

    def copies(ins, outs, sems):
        x, y, c = _mesh_position()
        return [_remote(_half(ins[k], items[k][1], 1 - c, halves[k][items[k][1]]), outs[k], sems[0], sems[1], k,
                        (x, y, 1 - c)) for k in range(n)]

    def start(ins, outs, sems):
        for cp in copies(ins, outs, sems):
            cp.start()

    def finish(ins, outs, sems):
        for cp in copies(ins, outs, sems):
            cp.wait()

    return _Rider([a for a, _ in items], [jax.ShapeDtypeStruct(h, a.dtype) for h, (a, _) in zip(halves, items)], {},
                  [pltpu.SemaphoreType.DMA((n,)), pltpu.SemaphoreType.DMA((n,))], start, None, finish)


def _swap_halves(items, name):
    rider = _swap_rider(items)
    n = len(items)

    def body(*refs):
        parts = (refs[:n], refs[n:2 * n], refs[2 * n:])
        rider.start(*parts)
        rider.finish(*parts)

    return pl.pallas_call(
        body, name=name, in_specs=[ANY] * n, out_specs=[ANY] * n, out_shape=rider.out_shape,
        scratch_shapes=rider.scratch,
    )(*rider.inputs)


def _grad_exchange_rider(items):
    n = len(items)
    slices = []
    for a, axis in items:
        shp = list(a.shape)
        shp[axis] //= N_CHIPS
        slices.append(tuple(shp))

    def copies(ins, outs, sems):
        send_sems, recv_sems = sems
        x, y, c = _mesh_position()
        made = []
        for k in range(n):
            axis = items[k][1]
            for r, (px, py) in enumerate(_other_chips(x, y)):
                made.append(_remote(_half(ins[k], axis, 2 * px + py, slices[k][axis]), outs[k].at[r],
                                    send_sems, recv_sems, 3 * k + r, (px, py, c)))
        return made

    def start(ins, outs, sems):
        for cp in copies(ins, outs, sems):
            cp.start()

    def finish(ins, outs, sems):
        for cp in copies(ins, outs, sems):
            cp.wait()

    return _Rider([a for a, _ in items], [jax.ShapeDtypeStruct((N_CHIPS - 1,) + s, BF16) for s in slices], {},
                  [pltpu.SemaphoreType.DMA((3 * n,)), pltpu.SemaphoreType.DMA((3 * n,))], start, None, finish)


def _gather_small(small):
    def body(small_ref, small_all, send_sems, recv_sems, local_sem):
        x, y, c = _mesh_position()
        me = 4 * x + 2 * y + c
        local = pltpu.make_async_copy(small_ref, small_all.at[me], local_sem.at[0])
        local.start()
        copies = [_remote(small_ref, small_all.at[me], send_sems, recv_sems, r, (x ^ fx, y ^ fy, c ^ fc))
                  for r, (fx, fy, fc) in enumerate(ALL_FLIPS)]
        for cp in copies:
            cp.start()
        for cp in copies:
            cp.wait()
        local.wait()

    return pl.pallas_call(
        body, name="gather_small_grads", in_specs=[ANY], out_specs=ANY,
        out_shape=jax.ShapeDtypeStruct((2 * N_CHIPS,) + small.shape, F32),
        scratch_shapes=[pltpu.SemaphoreType.DMA((len(ALL_FLIPS),)), pltpu.SemaphoreType.DMA((len(ALL_FLIPS),)),
                        pltpu.SemaphoreType.DMA((1,))],
    )(small)


def _sum_chips(recv, own, shard_axis, split_axis, pos, dest, layer, name, tr=128):
    sl = recv.shape[1:]
    nd = len(sl)
    tiled = nd == 2 and sl[0] > tr
    nt = sl[0] // tr if tiled else 1
    block = ((tr,) + sl[1:]) if tiled else sl
    shard = list(sl)
    shard[split_axis] *= 2

    def recv_index(i, pos_ref):
        return (0, i) + (0,) * (nd - 1) if tiled else (0,) * (nd + 1)

    def own_index(i, pos_ref):
        idx = [0] * nd
        idx[shard_axis] = pos_ref[1]
        if tiled:
            idx[0] = pos_ref[1] * nt + i if shard_axis == 0 else i
        return tuple(idx)

    def out_index(i, pos_ref):
        idx = [0] * nd
        idx[split_axis] = pos_ref[0]
        if tiled:
            idx[0] = pos_ref[0] * nt + i if split_axis == 0 else i
        return (layer, *idx)

    def body(pos_ref, recv_ref, own_ref, *rest):
        o_ref = rest[-1]
        tot = own_ref[...].astype(F32)
        for k in range(N_CHIPS - 1):
            tot = tot + recv_ref[k].astype(F32)
        o_ref[0] = tot

    in_specs = [pl.BlockSpec((N_CHIPS - 1,) + block, recv_index), pl.BlockSpec(block, own_index)]
    args = [pos, recv, own]
    aliases = {}
    if dest is not None:
        in_specs.append(ANY)
        args.append(dest)
        aliases = {3: 0}
    return pl.pallas_call(
        body, name=name,
        grid_spec=pltpu.PrefetchScalarGridSpec(
            num_scalar_prefetch=1, grid=(nt,), in_specs=in_specs,
            out_specs=pl.BlockSpec((1,) + block, out_index)),
        out_shape=jax.ShapeDtypeStruct((DEPTH,) + tuple(shard), F32),
        input_output_aliases=aliases,
    )(*args)


def _share_halves(bufs, name):
    n = len(bufs)

    def body(*refs):
        outs, (send_sems, recv_sems) = refs[n:2 * n], refs[2 * n:]
        x, y, c = _mesh_position()
        copies = []
        for k, (a, axis) in enumerate(bufs):
            size = a.shape[1 + axis] // 2
            mine = _half(outs[k], 1 + axis, c, size)
            copies.append(_remote(mine, mine, send_sems, recv_sems, k, (x, y, 1 - c)))
        for cp in copies:
            cp.start()
        for cp in copies:
            cp.wait()

    return pl.pallas_call(
        body, name=name, in_specs=[ANY] * n, out_specs=[ANY] * n,
        out_shape=[jax.ShapeDtypeStruct(a.shape, F32) for a, _ in bufs],
        input_output_aliases={k: k for k in range(n)},
        scratch_shapes=[pltpu.SemaphoreType.DMA((n,)), pltpu.SemaphoreType.DMA((n,))],
    )(*[a for a, _ in bufs])


def _layer_fwd(x, p, l, rider=None):
    tag = f"l{l}_"
    h = _rms_fwd(x, p["pre_g"], tag + "pre_norm")
    u = _matmul(h, p["w_in"], "nn", F32, tag + "in_proj")
    y_pool = _pool_fwd(u, p["pool_w"], p["pool_scale"], tag + "pool")
    y_conv = _conv_fwd(u, p["conv_w"], p["conv_b"], tag + "conv")
    (o_sb, y_sb, sb_after), carried = _sb_fwd(u, tag + "stickbreak", rider=rider)
    ys = [y_pool, y_conv, y_sb]
    projs, merged = _gate_fwd(u, ys, p["w_branch"], tag + "merge")
    out = _matmul(merged, p["w_out"], "nn", F32, tag + "out_proj")
    saved = dict(x=x, h=h, u=u, ys=ys, o_sb=o_sb, sb_after=sb_after, projs=projs, merged=merged, out=out)
    return out, saved, carried


def _layer_bwd(dy, p, saved, l, merge_rider=None, early=None, late=None):
    tag = f"l{l}_bwd_"
    u = saved["u"]
    d_out, g_post = _rms_bwd(saved["out"], p["post_g"], dy, None, BF16, tag + "post_norm")
    d_merged = _matmul(d_out, p["w_out"], "nt", BF16, tag + "out_proj_dx")
    g_w_out = _matmul(saved["merged"], d_out, "tn", F32, tag + "out_proj_dw", tk=2048)
    d_projs, d_logits, d_ys, carried_merge = _gate_bwd(u, saved["projs"], d_merged, p["w_branch"], tag + "merge",
                                                       rider=merge_rider)
    g_w_branch = jnp.stack([_matmul(saved["ys"][n], d_projs[n], "tn", F32, tag + f"branch_dw{n}", tk=2048)
                            for n in range(3)])
    rider = early(g_w_branch, g_w_out, carried_merge) if early else None
    d_pv, d_pg, g_pool_w, g_pool_scale = _pool_bwd(u, d_ys[0], p["pool_w"], p["pool_scale"], tag + "pool")
    d_cx, d_cgb, d_cgc, d_cg, g_conv_w, g_conv_b = _conv_bwd(u, d_ys[1], p["conv_w"], p["conv_b"], tag + "conv")
    (d_q, d_k, d_v, d_sg), carried_attn = _sb_bwd(u, saved["o_sb"], saved["sb_after"], d_ys[2], tag + "stickbreak",
                                                  rider=rider)
    du = jnp.concatenate([d_pv, d_pg, d_cx, d_cgb, d_cgc, d_cg, d_q, d_k, d_v, d_sg] + list(d_logits), axis=1)
    g_w_in = _matmul(saved["h"], du, "tn", F32, tag + "in_proj_dw", tk=2048)
    rider = late(g_w_in) if late else None
    dh = _matmul(du, p["w_in"], "nt", BF16, tag + "in_proj_dx", tk=2048, rider=rider)
    dh, carried_dx = dh if rider else (dh, [])
    dx, g_pre = _rms_bwd(saved["x"], p["pre_g"], dh, dy, F32, tag + "pre_norm")
    grads = dict(w_in=g_w_in, w_branch=g_w_branch, w_out=g_w_out, pre_g=g_pre, post_g=g_post,
                 pool_w=g_pool_w, pool_scale=g_pool_scale, conv_w=g_conv_w, conv_b=g_conv_b)
    return dx, grads, carried_attn, carried_dx


SMALL_ORDER = ["pre_g", "pool_w", "pool_scale", "conv_w", "conv_b", "post_g"]


def _pack_small(per_layer, loss_part):
    parts, spans, at = [], {}, 0
    for name in SMALL_ORDER:
        a = jnp.stack([per_layer[l][name] for l in range(DEPTH)]).reshape(-1, LANE)
        parts.append(a)
        spans[name] = (at, a.shape[0])
        at += a.shape[0]
    parts.append(jnp.broadcast_to(loss_part, (8, LANE)))
    spans["loss"] = (at, 8)
    return jnp.concatenate(parts, axis=0), spans


def kernel(x, pre_norm_g, w_in, pool_w, pool_scale, conv_w, conv_b, w_branch, w_out, post_norm_g, loss_target, m_pre_norm_g, m_w_in, m_pool_w, m_pool_scale, m_conv_w, m_conv_b, m_w_branch, m_w_out, m_post_norm_g, v_pre_norm_g, v_w_in, v_pool_w, v_pool_scale, v_conv_w, v_conv_b, v_w_branch, v_w_out, v_post_norm_g):
    mx, my, mc = _mesh_position()
    chip = 2 * mx + my
    core = mc.astype(jnp.int32).reshape(1)
    pos = jnp.stack([mc, chip]).astype(jnp.int32)

    names = ["w_in", "w_branch", "w_out"]
    given = dict(w_in=w_in, w_branch=w_branch, w_out=w_out)
    in_place = [[_cast_shard(given[n], l, WEIGHT_LAYOUT[i][0], pos, f"cast_{n}{l}") for i, n in enumerate(names)]
                for l in range(DEPTH)]
    gathered, conv_w_by_chip = _gather_weights(in_place[0], conv_w)
    conv_w_f = conv_w_by_chip.transpose(1, 2, 0, 3).reshape(DEPTH, 3, WIDTH)
    pool_w_b = _cast_bf16(pool_w, "cast_pool_w")

    def layer_params(l, big):
        return dict(pre_g=pre_norm_g[l:l + 1], post_g=post_norm_g[l:l + 1], w_in=big[0], w_branch=big[1],
                    w_out=big[2], pool_w=pool_w_b[l], pool_scale=pool_scale[l:l + 1], conv_w=conv_w_f[l],
                    conv_b=conv_b[l:l + 1])

    act = x[0]
    params, saved = [], []
    for l in range(DEPTH):
        params.append(layer_params(l, gathered))
        rider = _weight_gather_rider(in_place[l + 1], WEIGHT_LAYOUT) if l + 1 < DEPTH else None
        out, sv, gathered = _layer_fwd(act, params[l], l, rider)
        saved.append(sv)
        if l < DEPTH - 1:
            act = _resid_out(act, out, params[l]["post_g"], None, f"l{l}_resid")
    dy, loss_part = _resid_out(act, saved[-1]["out"], params[-1]["post_g"], loss_target[0], "loss_head")

    split_axis = dict(w_in=0, w_branch=1, w_out=1)
    shard_axis = dict(w_in=1, w_branch=2, w_out=0)
    grads = [None] * DEPTH
    chip_sums = [dict() for _ in range(DEPTH)]
    by_chip = [dict() for _ in range(DEPTH)]

    def reduce_in_chip(l, which, g):
        items = [(g[n], split_axis[n]) for n in which]
        from_sibling = _swap_halves(items, f"swap_grad_halves{l}_{which[0]}")
        for n, (a, axis), r in zip(which, items, from_sibling):
            chip_sums[l][n] = _chip_sum(a, r, axis, core, f"chip_sum{l}_{n}")

    def exchange_rider(keys):
        return _grad_exchange_rider([(chip_sums[l][n], shard_axis[n]) for l, n in keys])

    waiting = []
    for l in reversed(range(DEPTH)):
        sent_early, sent_late = list(waiting) + [(l, "w_branch"), (l, "w_out")], [(l, "w_in")]
        waiting_items = [(grads[ll][n], split_axis[n]) for ll, n in waiting]

        def early(g_w_branch, g_w_out, from_sibling, l=l, keys=sent_early, above=tuple(waiting), items=waiting_items):
            for (ll, n), (a, axis), r in zip(above, items, from_sibling):
                chip_sums[ll][n] = _chip_sum(a, r, axis, core, f"chip_sum{ll}_{n}")
            reduce_in_chip(l, ["w_branch", "w_out"], dict(w_branch=g_w_branch, w_out=g_w_out))
            return exchange_rider(keys)

        def late(g_w_in, l=l, keys=sent_late):
            reduce_in_chip(l, ["w_in"], dict(w_in=g_w_in))
            return exchange_rider(keys)

        if l == DEPTH - 1:
            dy, grads[l], _, _ = _layer_bwd(dy, params[l], saved[l], l)
            waiting = [(l, n) for n in names]
        else:
            dy, grads[l], got_early, got_late = _layer_bwd(dy, params[l], saved[l], l, _swap_rider(waiting_items),
                                                           early, late)
            for (ll, n), r in zip(sent_early + sent_late, list(got_early) + list(got_late)):
                by_chip[ll][n] = r
            waiting = []
    assert not waiting
    grad_x = dy[None]
    small_part, spans = _pack_small(grads, loss_part)
    small_all = _gather_small(small_part)
    bufs = []
    for n in names:
        dest = None
        for l in range(DEPTH):
            dest = _sum_chips(by_chip[l][n], chip_sums[l][n], shard_axis[n], split_axis[n], pos, dest, l,
                              f"sum_chips{l}_{n}")
        bufs.append((dest, split_axis[n]))
    g_w_in, g_w_branch, g_w_out = _share_halves(bufs, "share_grad_halves")

    small_sum = _sum_slots(small_all, F32, "sum_small")
    loss = small_sum[spans["loss"][0], 0]
    small = {}
    for name, like in (("pre_g", pre_norm_g), ("pool_w", pool_w), ("pool_scale", pool_scale), ("conv_b", conv_b),
                       ("post_g", post_norm_g)):
        at, n = spans[name]
        small[name] = small_sum[at:at + n].reshape(like.shape)
    at, n = spans["conv_w"]
    g_conv_w_full = small_sum[at:at + n].reshape(DEPTH, 3, WIDTH)
    g_conv_w = lax.dynamic_slice_in_dim(g_conv_w_full, chip * conv_w.shape[2], conv_w.shape[2], axis=2)

    g = dict(pre_norm_g=small["pre_g"], w_in=g_w_in, pool_w=small["pool_w"], pool_scale=small["pool_scale"],
             conv_w=g_conv_w, conv_b=small["conv_b"], w_branch=g_w_branch, w_out=g_w_out, post_norm_g=small["post_g"])
    w = dict(pre_norm_g=pre_norm_g, w_in=w_in, pool_w=pool_w, pool_scale=pool_scale, conv_w=conv_w, conv_b=conv_b,
             w_branch=w_branch, w_out=w_out, post_norm_g=post_norm_g)
    m = dict(pre_norm_g=m_pre_norm_g, w_in=m_w_in, pool_w=m_pool_w, pool_scale=m_pool_scale, conv_w=m_conv_w,
             conv_b=m_conv_b, w_branch=m_w_branch, w_out=m_w_out, post_norm_g=m_post_norm_g)
    v = dict(pre_norm_g=v_pre_norm_g, w_in=v_w_in, pool_w=v_pool_w, pool_scale=v_pool_scale, conv_w=v_conv_w,
             conv_b=v_conv_b, w_branch=v_w_branch, w_out=v_w_out, post_norm_g=v_post_norm_g)
    order = ["pre_norm_g", "w_in", "pool_w", "pool_scale", "conv_w", "conv_b", "w_branch", "w_out", "post_norm_g"]
    upd = {n: _adamw(w[n], g[n], m[n], v[n], "adamw_" + n) for n in order}
    return (loss, grad_x, *[g[n] for n in order], *[upd[n][0] for n in order], *[upd[n][1] for n in order],
            *[upd[n][2] for n in order])
```

```python
import functools

import jax
import jax.numpy as jnp
from jax import lax
from jax.experimental import pallas as pl
from jax.experimental.pallas import tpu as pltpu

F32 = jnp.float32
BF16 = jnp.bfloat16
MESH = pl.DeviceIdType.MESH
ANY = pl.BlockSpec(memory_space=pl.ANY)

DEPTH = 2
D_MODEL = 1024
WIDTH = 512
N_IN = 8192
N_CHIPS = 4
HEAD_DIM = 64
RMS_EPS = 1e-6
POOL_HALO = 16
CONV_HALO = 16
LANE = 128
COL_POOL_V, COL_POOL_G = 0, 4
COL_CONV_X, COL_CONV_GB, COL_CONV_GC, COL_CONV_G = 8, 12, 16, 20
COL_SB_Q, COL_SB_K, COL_SB_V, COL_SB_G = 24, 28, 32, 36
COL_MERGE_1024 = 5

ADAM_LR, ADAM_B1, ADAM_B2, ADAM_EPS, ADAM_WD, ADAM_STEP = 0.001, 0.9, 0.999, 1e-08, 0.01, 10

NN = (((1,), (0,)), ((), ()))
NT = (((1,), (1,)), ((), ()))
TN = (((0,), (0,)), ((), ()))


def _sigmoid(x):
    return 1.0 / (1.0 + jnp.exp(-x))


def _silu_and_grad(x):
    s = _sigmoid(x)
    return x * s, s * (1.0 + x * (1.0 - s))


def _dot(a, b, dims):
    return lax.dot_general(a, b, dims, preferred_element_type=F32)


def _matmul(a, b, mode, out_dtype, name, tm=1024, tn=1024, tk=1024, b_lead=(), rider=None):
    b_shape = b.shape[len(b_lead):]
    if mode == "nn":
        (m, k), (k2, n) = a.shape, b_shape
    elif mode == "nt":
        (m, k), (n, k2) = a.shape, b_shape
    else:
        (k, m), (k2, n) = a.shape, b_shape
    assert k == k2 and a.dtype == BF16 and b.dtype == BF16
    tm, tn, tk = min(tm, m), min(tn, n), min(tk, k)
    assert m % tm == 0 and n % tn == 0 and k % tk == 0
    nk = k // tk
    dims = {"nn": NN, "nt": NT, "tn": TN}[mode]

    grid = (m // tm, n // tn, nk)

    def at_step(which):
        return functools.reduce(jnp.logical_and, [pl.program_id(d) == (g - 1 if which else 0) for d, g in enumerate(grid)])

    def body(*refs):
        (a_ref, b_ref, o_ref, *scratch), riding = _split_refs(refs, 2, 1, 1 if nk > 1 else 0, rider)
        _ride(rider, "start", at_step(0), riding)
        compute(a_ref, b_ref, o_ref, scratch)
        _ride(rider, "finish", at_step(1), riding)

    def compute(a_ref, b_ref, o_ref, scratch):
        p = _dot(a_ref[...], b_ref[...], dims)
        if nk == 1:
            o_ref[...] = p.astype(o_ref.dtype)
        else:
            acc = scratch[0]
            kk = pl.program_id(2)

            @pl.when(kk == 0)
            def _():
                acc[...] = p

            @pl.when(jnp.logical_and(kk > 0, kk < nk - 1))
            def _():
                acc[...] += p

            @pl.when(kk == nk - 1)
            def _():
                o_ref[...] = (acc[...] + p).astype(o_ref.dtype)

    if mode == "tn":
        a_spec = pl.BlockSpec((tk, tm), lambda i, j, kk: (kk, i))
    else:
        a_spec = pl.BlockSpec((tm, tk), lambda i, j, kk: (i, kk))
    squeezed = (None,) * len(b_lead)
    if mode == "nt":
        b_spec = pl.BlockSpec(squeezed + (tn, tk), lambda i, j, kk: (*b_lead, j, kk))
    else:
        b_spec = pl.BlockSpec(squeezed + (tk, tn), lambda i, j, kk: (*b_lead, kk, j))
    extra = _rider_call_args(rider, 2, 1)
    outs = pl.pallas_call(
        body, name=name, grid=grid,
        in_specs=[a_spec, b_spec] + extra["in_specs"],
        out_specs=[pl.BlockSpec((tm, tn), lambda i, j, kk: (i, j))] + extra["out_specs"],
        out_shape=[jax.ShapeDtypeStruct((m, n), out_dtype)] + extra["out_shape"],
        input_output_aliases=extra["aliases"],
        scratch_shapes=([pltpu.VMEM((tm, tn), F32)] if nk > 1 else []) + extra["scratch"],
        compiler_params=pltpu.CompilerParams(dimension_semantics=("arbitrary",) * 3 if rider else
                                             ("parallel", "parallel", "arbitrary")),
    )(a, b, *extra["inputs"])
    return (outs[0], outs[1:]) if rider else outs[0]


def _rms_fwd(x, g, name, ts=512):
    s, d = x.shape

    def body(x_ref, g_ref, h_ref):
        xv = x_ref[...]
        r = lax.rsqrt(jnp.mean(xv * xv, axis=-1, keepdims=True) + RMS_EPS)
        h_ref[...] = (xv * r * g_ref[...]).astype(BF16)

    return pl.pallas_call(
        body, name=name, grid=(s // ts,),
        in_specs=[pl.BlockSpec((ts, d), lambda i: (i, 0)), pl.BlockSpec((1, d), lambda i: (0, 0))],
        out_specs=pl.BlockSpec((ts, d), lambda i: (i, 0)),
        out_shape=jax.ShapeDtypeStruct((s, d), BF16),
    )(x, g)


def _rms_bwd(xin, g, dh, resid, out_dtype, name, ts=512):
    s, d = xin.shape
    has_resid = resid is not None

    def body(*refs):
        if has_resid:
            x_ref, g_ref, dh_ref, res_ref, dx_ref, dg_ref = refs
        else:
            x_ref, g_ref, dh_ref, dx_ref, dg_ref = refs
        xv = x_ref[...]
        dhv = dh_ref[...].astype(F32)
        r = lax.rsqrt(jnp.mean(xv * xv, axis=-1, keepdims=True) + RMS_EPS)
        nrm = xv * r
        dn = dhv * g_ref[...]
        dx = r * (dn - nrm * jnp.mean(dn * nrm, axis=-1, keepdims=True))
        if has_resid:
            dx = dx + res_ref[...]
        dx_ref[...] = dx.astype(dx_ref.dtype)
        part = jnp.sum(dhv * nrm, axis=0, keepdims=True)

        @pl.when(pl.program_id(0) == 0)
        def _():
            dg_ref[...] = part

        @pl.when(pl.program_id(0) > 0)
        def _():
            dg_ref[...] += part

    row = pl.BlockSpec((ts, d), lambda i: (i, 0))
    vec = pl.BlockSpec((1, d), lambda i: (0, 0))
    ins = [xin, g, dh] + ([resid] if has_resid else [])
    return pl.pallas_call(
        body, name=name, grid=(s // ts,),
        in_specs=[row, vec, row] + ([row] if has_resid else []),
        out_specs=[row, vec],
        out_shape=[jax.ShapeDtypeStruct((s, d), out_dtype), jax.ShapeDtypeStruct((1, d), F32)],
        compiler_params=pltpu.CompilerParams(dimension_semantics=("arbitrary",)),
    )(*ins)


def _resid_out(x, out, g, target, name, ts=512):
    s, d = x.shape
    has_loss = target is not None

    def body(*refs):
        if has_loss:
            x_ref, o_ref, g_ref, t_ref, dy_ref, loss_ref = refs
        else:
            x_ref, o_ref, g_ref, y_ref = refs
        ov = o_ref[...]
        r = lax.rsqrt(jnp.mean(ov * ov, axis=-1, keepdims=True) + RMS_EPS)
        yv = x_ref[...] + ov * r * g_ref[...]
        if not has_loss:
            y_ref[...] = yv
            return
        err = yv - t_ref[...]
        dy_ref[...] = err * (1.0 / d)
        part = jnp.sum(jnp.sum(err * err, axis=-1, keepdims=True), axis=0, keepdims=True) * (0.5 / d)
        part = jnp.broadcast_to(part, (1, LANE))

        @pl.when(pl.program_id(0) == 0)
        def _():
            loss_ref[...] = part

        @pl.when(pl.program_id(0) > 0)
        def _():
            loss_ref[...] += part

    row = pl.BlockSpec((ts, d), lambda i: (i, 0))
    vec = pl.BlockSpec((1, d), lambda i: (0, 0))
    if has_loss:
        return pl.pallas_call(
            body, name=name, grid=(s // ts,),
            in_specs=[row, row, vec, row],
            out_specs=[row, pl.BlockSpec((1, LANE), lambda i: (0, 0))],
            out_shape=[jax.ShapeDtypeStruct((s, d), F32), jax.ShapeDtypeStruct((1, LANE), F32)],
            compiler_params=pltpu.CompilerParams(dimension_semantics=("arbitrary",)),
        )(x, out, g, target)
    return pl.pallas_call(
        body, name=name, grid=(s // ts,),
        in_specs=[row, row, vec], out_specs=row,
        out_shape=jax.ShapeDtypeStruct((s, d), F32),
    )(x, out, g)


def _rows_before(ref, start, n, halo):
    if start == 0:
        return jnp.concatenate([jnp.zeros((halo, ref.shape[1]), F32), ref[0:n, :].astype(F32)], axis=0)
    return ref[start - halo:start + n, :].astype(F32)


def _rows_after(ref, start, n, halo):
    if start + n == ref.shape[0]:
        return jnp.concatenate([ref[start:start + n, :].astype(F32), jnp.zeros((halo, ref.shape[1]), F32)], axis=0)
    return ref[start:start + n + halo, :].astype(F32)


def _pick_window(group, s2, s4, s8, s16):
    return jnp.where(group == 0, s2, jnp.where(group == 1, s4, jnp.where(group == 2, s8, s16)))


def _trailing_sums(ext, group):
    s2 = ext + pltpu.roll(ext, 1, 0)
    s4 = s2 + pltpu.roll(s2, 2, 0)
    s8 = s4 + pltpu.roll(s4, 4, 0)
    s16 = s8 + pltpu.roll(s8, 8, 0)
    return _pick_window(group, s2, s4, s8, s16)


def _leading_sums(ext, group):
    n = ext.shape[0]
    s2 = ext + pltpu.roll(ext, n - 1, 0)
    s4 = s2 + pltpu.roll(s2, n - 2, 0)
    s8 = s4 + pltpu.roll(s4, n - 4, 0)
    s16 = s8 + pltpu.roll(s8, n - 8, 0)
    return _pick_window(group, s2, s4, s8, s16)


def _window_count(start, n, group):
    pos = start + lax.broadcasted_iota(jnp.int32, (n, LANE), 0)
    return jnp.minimum(pos + 1, 2 << group).astype(F32)


def _pooled(v_ref, start, n, group):
    ext = _rows_before(v_ref, start, n, POOL_HALO)
    sums = _trailing_sums(ext, group)[POOL_HALO:, :]
    return sums / _window_count(start, n, group) - ext[POOL_HALO:, :]


def _pool_fwd(u, pool_w, pool_scale, name, ts=512):
    s = u.shape[0]

    def body(v_ref, gate_ref, w_ref, sc_ref, y_ref):
        group = pl.program_id(0)
        for c in range(s // ts):
            a = c * ts
            pooled = _pooled(v_ref, a, ts, group)
            mixed = _dot(pooled.astype(BF16), w_ref[...], NN)
            gate = gate_ref[a:a + ts, :].astype(F32)
            y_ref[a:a + ts, :] = (mixed * sc_ref[...] * (gate * _sigmoid(gate))).astype(BF16)

    col = lambda base: pl.BlockSpec((s, LANE), lambda g: (0, base + g))
    return pl.pallas_call(
        body, name=name, grid=(4,),
        in_specs=[col(COL_POOL_V), col(COL_POOL_G),
                  pl.BlockSpec((None, LANE, LANE), lambda g: (g, 0, 0)),
                  pl.BlockSpec((1, LANE), lambda g: (0, g))],
        out_specs=pl.BlockSpec((s, LANE), lambda g: (0, g)),
        out_shape=jax.ShapeDtypeStruct((s, WIDTH), BF16),
    )(u, u, pool_w, pool_scale)


def _pool_bwd(u, dy, pool_w, pool_scale, name, ts=512):
    s = u.shape[0]

    def body(v_ref, gate_ref, dy_ref, w_ref, sc_ref, dv_ref, dgate_ref, dw_ref, dsc_ref):
        group = pl.program_id(0)
        w = w_ref[...]
        scale = sc_ref[...]
        dw = jnp.zeros((LANE, LANE), F32)
        dsc = jnp.zeros((1, LANE), F32)
        for c in range(s // ts):
            a = c * ts
            n_ext = ts + POOL_HALO
            gate_e = _rows_after(gate_ref, a, ts, POOL_HALO)
            dy_e = _rows_after(dy_ref, a, ts, POOL_HALO)
            silu_e, dsilu_e = _silu_and_grad(gate_e)
            dms_e = dy_e * silu_e
            dm_e = (dms_e * scale).astype(BF16)
            dpool_e = _dot(dm_e, w, NT)
            spread = _leading_sums(dpool_e / _window_count(a, n_ext, group), group)
            dv_ref[a:a + ts, :] = (spread[0:ts, :] - dpool_e[0:ts, :]).astype(BF16)
            pooled = _pooled(v_ref, a, ts, group).astype(BF16)
            mixed = _dot(pooled, w, NN)
            dgate_ref[a:a + ts, :] = (dy_e[0:ts, :] * mixed * scale * dsilu_e[0:ts, :]).astype(BF16)
            dsc = dsc + jnp.sum(dms_e[0:ts, :] * mixed, axis=0, keepdims=True)
            dw = dw + _dot(pooled, dm_e[0:ts, :], TN)
        dw_ref[...] = dw
        dsc_ref[...] = dsc

    col = lambda base: pl.BlockSpec((s, LANE), lambda g: (0, base + g))
    out_col = pl.BlockSpec((s, LANE), lambda g: (0, g))
    return pl.pallas_call(
        body, name=name, grid=(4,),
        in_specs=[col(COL_POOL_V), col(COL_POOL_G), out_col,
                  pl.BlockSpec((None, LANE, LANE), lambda g: (g, 0, 0)),
                  pl.BlockSpec((1, LANE), lambda g: (0, g))],
        out_specs=[out_col, out_col,
                   pl.BlockSpec((None, LANE, LANE), lambda g: (g, 0, 0)),
                   pl.BlockSpec((1, LANE), lambda g: (0, g))],
        out_shape=[jax.ShapeDtypeStruct((s, WIDTH), BF16), jax.ShapeDtypeStruct((s, WIDTH), BF16),
                   jax.ShapeDtypeStruct((4, LANE, LANE), F32), jax.ShapeDtypeStruct((1, WIDTH), F32)],
    )(u, u, dy, pool_w, pool_scale)


def _conv_taps(x_ref, gc_ref, start, n):
    z_ext = _rows_before(gc_ref, start, n, CONV_HALO) * _rows_before(x_ref, start, n, CONV_HALO)
    z0 = z_ext[CONV_HALO:, :]
    z1 = pltpu.roll(z_ext, 1, 0)[CONV_HALO:, :]
    z2 = pltpu.roll(z_ext, 2, 0)[CONV_HALO:, :]
    return z0, z1, z2


def _conv_fwd(u, conv_w, conv_b, name, ts=512):
    s = u.shape[0]

    def body(x_ref, gb_ref, gc_ref, g_ref, w_ref, b_ref, y_ref):
        w0, w1, w2 = w_ref[0:1, :], w_ref[1:2, :], w_ref[2:3, :]
        for c in range(s // ts):
            a = c * ts
            z0, z1, z2 = _conv_taps(x_ref, gc_ref, a, ts)
            y = w2 * z0 + w1 * z1 + w0 * z2 + b_ref[...]
            gate = g_ref[a:a + ts, :].astype(F32)
            y_ref[a:a + ts, :] = (gb_ref[a:a + ts, :].astype(F32) * y * (gate * _sigmoid(gate))).astype(BF16)

    col = lambda base: pl.BlockSpec((s, LANE), lambda j: (0, base + j))
    return pl.pallas_call(
        body, name=name, grid=(4,),
        in_specs=[col(COL_CONV_X), col(COL_CONV_GB), col(COL_CONV_GC), col(COL_CONV_G),
                  pl.BlockSpec((3, LANE), lambda j: (0, j)), pl.BlockSpec((1, LANE), lambda j: (0, j))],
        out_specs=pl.BlockSpec((s, LANE), lambda j: (0, j)),
        out_shape=jax.ShapeDtypeStruct((s, WIDTH), BF16),
    )(u, u, u, u, conv_w, conv_b)


def _conv_bwd(u, dy, conv_w, conv_b, name, ts=512):
    s = u.shape[0]

    def body(x_ref, gb_ref, gc_ref, g_ref, dy_ref, w_ref, b_ref,
             dx_ref, dgb_ref, dgc_ref, dg_ref, dw_ref, db_ref):
        w0, w1, w2 = w_ref[0:1, :], w_ref[1:2, :], w_ref[2:3, :]
        acc = [jnp.zeros((1, LANE), F32) for _ in range(4)]
        for c in range(s // ts):
            a = c * ts
            n_ext = ts + CONV_HALO
            gate_e = _rows_after(g_ref, a, ts, CONV_HALO)
            silu_e, dsilu_e = _silu_and_grad(gate_e)
            dy_e = _rows_after(dy_ref, a, ts, CONV_HALO)
            gb_e = _rows_after(gb_ref, a, ts, CONV_HALO)
            dyy_e = dy_e * silu_e * gb_e
            dz = (w2 * dyy_e + w1 * pltpu.roll(dyy_e, n_ext - 1, 0) + w0 * pltpu.roll(dyy_e, n_ext - 2, 0))[0:ts, :]
            z0, z1, z2 = _conv_taps(x_ref, gc_ref, a, ts)
            yb = w2 * z0 + w1 * z1 + w0 * z2 + b_ref[...]
            dyv = dy_e[0:ts, :]
            dyy = dyy_e[0:ts, :]
            dg_ref[a:a + ts, :] = (dyv * gb_e[0:ts, :] * yb * dsilu_e[0:ts, :]).astype(BF16)
            dgb_ref[a:a + ts, :] = (dyv * silu_e[0:ts, :] * yb).astype(BF16)
            dx_ref[a:a + ts, :] = (dz * gc_ref[a:a + ts, :].astype(F32)).astype(BF16)
            dgc_ref[a:a + ts, :] = (dz * x_ref[a:a + ts, :].astype(F32)).astype(BF16)
            for i, term in enumerate((dyy * z2, dyy * z1, dyy * z0, dyy)):
                acc[i] = acc[i] + jnp.sum(term, axis=0, keepdims=True)
        dw_ref[0:1, :] = acc[0]
        dw_ref[1:2, :] = acc[1]
        dw_ref[2:3, :] = acc[2]
        db_ref[...] = acc[3]

    col = lambda base: pl.BlockSpec((s, LANE), lambda j: (0, base + j))
    out_col = pl.BlockSpec((s, LANE), lambda j: (0, j))
    big = jax.ShapeDtypeStruct((s, WIDTH), BF16)
    return pl.pallas_call(
        body, name=name, grid=(4,),
        in_specs=[col(COL_CONV_X), col(COL_CONV_GB), col(COL_CONV_GC), col(COL_CONV_G), out_col,
                  pl.BlockSpec((3, LANE), lambda j: (0, j)), pl.BlockSpec((1, LANE), lambda j: (0, j))],
        out_specs=[out_col, out_col, out_col, out_col,
                   pl.BlockSpec((3, LANE), lambda j: (0, j)), pl.BlockSpec((1, LANE), lambda j: (0, j))],
        out_shape=[big, big, big, big,
                   jax.ShapeDtypeStruct((3, WIDTH), F32), jax.ShapeDtypeStruct((1, WIDTH), F32)],
    )(u, u, u, u, dy, conv_w, conv_b)


LOG2_E = 1.4426950408889634
LN_2 = 0.6931471805599453


def _sb_scores(q_h, k_blk, valid, later_mat, carry):
    z = _dot(q_h, k_blk, NT)
    neg_z = -z
    soft = jnp.log(1.0 + jnp.exp2(jnp.minimum(z, neg_z))) * LOG2_E
    log_keep = jnp.minimum(neg_z, 0.0) - soft
    log_beta = log_keep + z
    if valid is not None:
        log_keep = jnp.where(valid, log_keep, 0.0)
    later = _dot(log_keep.astype(BF16), later_mat, NN) + carry
    return log_keep, log_beta, later


def _masked(valid, x):
    return x if valid is None else jnp.where(valid, x, 0.0)


def _diagonal_masks(tq, tk):
    r = lax.broadcasted_iota(jnp.int32, (tq, tk), 0)
    cidx = lax.broadcasted_iota(jnp.int32, (tq, tk), 1)
    return [cidx + d * tk < r for d in range(tq // tk)]


def _triangle(tk, op):
    r = lax.broadcasted_iota(jnp.int32, (tk, tk), 0)
    cidx = lax.broadcasted_iota(jnp.int32, (tk, tk), 1)
    return op(r, cidx).astype(BF16)


def _split_refs(refs, n_in, n_out, n_scratch, rider):
    r_in = len(rider.inputs) if rider else 0
    r_out = len(rider.out_shape) if rider else 0
    a, b = n_in + r_in, n_in + r_in + n_out + r_out
    own = refs[:n_in] + refs[a:a + n_out] + refs[b:b + n_scratch]
    return own, (refs[n_in:a], refs[a + n_out:b], refs[b + n_scratch:])


def _rider_call_args(rider, n_in, n_out):
    if rider is None:
        return dict(in_specs=[], out_specs=[], out_shape=[], aliases={}, scratch=[], inputs=[])
    return dict(in_specs=[ANY] * len(rider.inputs), out_specs=[ANY] * len(rider.out_shape),
                out_shape=list(rider.out_shape), scratch=list(rider.scratch), inputs=list(rider.inputs),
                aliases={n_in + a: n_out + b for a, b in rider.aliases.items()})


def _ride(rider, phase, when, parts):
    fn = getattr(rider, phase) if rider else None
    if fn is not None:
        pl.when(when)(lambda: fn(*parts))


def _sb_fwd(u, name, t=512, tk=256, pairs=4, rider=None):
    s = u.shape[0]
    assert s // tk <= LANE and 4 % pairs == 0 and t % tk == 0
    scale = HEAD_DIM ** -0.5
    nh = 2 * pairs
    wide = pairs * LANE
    ratio = t // tk
    groups, nq = 4 // pairs, s // t

    def body(*refs):
        own, riding = _split_refs(refs, 4, 3, 4, rider)
        q_ref, k_ref, v_ref, g_ref, o_ref, y_ref, after_ref, kb_ref, vb_ref, acc_ref, carry_ref = own
        grp = pl.program_id(0)
        i = pl.program_id(1)
        _ride(rider, "start", jnp.logical_and(grp == 0, i == 0), riding)
        _ride(rider, "middle", jnp.logical_and(grp == groups - 1, i == (3 * nq) // 4), riding)

        @pl.when(i == 0)
        def _():
            kb_ref[...] = k_ref[...].astype(BF16)
            vb_ref[...] = v_ref[...].astype(BF16)

        lane = lax.broadcasted_iota(jnp.int32, (t, LANE), 1)
        first = lane < HEAD_DIM
        after_ref[...] = jnp.zeros_like(after_ref)
        qv = q_ref[...].astype(F32) * (scale * LOG2_E)
        q_heads = []
        for p in range(pairs):
            qp = qv[:, p * LANE:(p + 1) * LANE]
            q_heads += [jnp.where(first, qp, 0.0).astype(BF16), jnp.where(first, 0.0, qp).astype(BF16)]
        later_mat = _triangle(tk, lambda r, cidx: r > cidx)
        acc_ref[...] = jnp.zeros_like(acc_ref)
        carry_ref[...] = jnp.zeros_like(carry_ref)

        def block(kb, valid, lo=0):
            rows = pl.ds(pl.multiple_of(kb * tk, tk), tk)
            k_blk = kb_ref[rows, :]
            v_blk = vb_ref[rows, :]
            carries = [carry_ref[h, lo:, :] for h in range(nh)]
            afters = [after_ref[lo:, h * LANE:(h + 1) * LANE] for h in range(nh)]
            accs = [acc_ref[h, lo:, :] for h in range(nh)]
            outs = []
            for h in range(nh):
                cols = slice((h // 2) * LANE, (h // 2 + 1) * LANE)
                log_keep, log_beta, later = _sb_scores(q_heads[h][lo:], k_blk[:, cols], valid, later_mat, carries[h])
                a = _masked(valid, jnp.exp2(log_beta + later))
                outs.append((accs[h] + _dot(a.astype(BF16), v_blk[:, cols], NN),
                             carries[h] + jnp.sum(log_keep, axis=1, keepdims=True),
                             jnp.where(lane[lo:] == kb, carries[h], afters[h])))
            for h in range(nh):
                acc_ref[h, lo:, :] = outs[h][0]
                carry_ref[h, lo:, :] = outs[h][1]
                after_ref[lo:, h * LANE:(h + 1) * LANE] = outs[h][2]

        def step(j, _):
            block(ratio * i - 1 - j, None)
            return 0

        masks = _diagonal_masks(t, tk)
        for d in reversed(range(ratio)):
            block(ratio * i + d, masks[d][d * tk:], d * tk)
        lax.fori_loop(0, ratio * i, step, 0)
        for p in range(pairs):
            cols = slice(p * LANE, (p + 1) * LANE)
            o = jnp.where(first, acc_ref[2 * p], acc_ref[2 * p + 1])
            o_ref[:, cols] = o
            gate = g_ref[:, cols].astype(F32)
            y_ref[:, cols] = (o * gate * _sigmoid(gate)).astype(BF16)
        _ride(rider, "finish", jnp.logical_and(grp == groups - 1, i == nq - 1), riding)

    blk = lambda base: pl.BlockSpec((t, wide), lambda g, i: (i, base // pairs + g))
    full = lambda base: pl.BlockSpec((s, wide), lambda g, i: (0, base // pairs + g))
    out_blk = pl.BlockSpec((t, wide), lambda g, i: (i, g))
    extra = _rider_call_args(rider, 4, 3)
    outs = pl.pallas_call(
        body, name=name, grid=(groups, nq),
        in_specs=[blk(COL_SB_Q), full(COL_SB_K), full(COL_SB_V), blk(COL_SB_G)] + extra["in_specs"],
        out_specs=[out_blk, out_blk, pl.BlockSpec((t, nh * LANE), lambda g, i: (i, g))] + extra["out_specs"],
        out_shape=[jax.ShapeDtypeStruct((s, WIDTH), F32), jax.ShapeDtypeStruct((s, WIDTH), BF16),
                   jax.ShapeDtypeStruct((s, 8 * LANE), F32)] + extra["out_shape"],
        input_output_aliases=extra["aliases"],
        scratch_shapes=[pltpu.VMEM((s, wide), BF16), pltpu.VMEM((s, wide), BF16),
                        pltpu.VMEM((nh, t, LANE), F32), pltpu.VMEM((nh, t, 1), F32)] + extra["scratch"],
        compiler_params=pltpu.CompilerParams(dimension_semantics=("arbitrary", "arbitrary")),
    )(u, u, u, u, *extra["inputs"])
    return outs[:3], outs[3:]


def _sb_bwd(u, o, after, dy, name, t=512, tk=256, pairs=2, rider=None):
    s = u.shape[0]
    nq = s // t
    scale = HEAD_DIM ** -0.5
    nh = 2 * pairs
    wide = pairs * LANE
    ratio = t // tk
    groups = 4 // pairs

    def body(*refs):
        own, riding = _split_refs(refs, 7, 4, 6, rider)
        (q_ref, k_ref, v_ref, g_ref, o_ref, after_ref, dy_ref, dq_ref, dk_ref, dv_ref, dg_ref,
         kb_ref, vb_ref, dk_acc, dv_acc, dq_acc, carry_ref) = own
        grp = pl.program_id(0)
        i = pl.program_id(1)
        _ride(rider, "start", jnp.logical_and(grp == 0, i == 0), riding)

        @pl.when(i == 0)
        def _():
            kb_ref[...] = k_ref[...].astype(BF16)
            vb_ref[...] = v_ref[...].astype(BF16)
            dk_acc[...] = jnp.zeros_like(dk_acc)
            dv_acc[...] = jnp.zeros_like(dv_acc)

        lane = lax.broadcasted_iota(jnp.int32, (t, LANE), 1)
        first = lane < HEAD_DIM
        gate = g_ref[...].astype(F32)
        silu, dsilu = _silu_and_grad(gate)
        dyv = dy_ref[...]
        do = dyv * silu
        dg_ref[...] = (dyv * o_ref[...] * dsilu).astype(BF16)
        qv = q_ref[...].astype(F32) * (scale * LOG2_E)
        do_heads, q_heads = [], []
        for p in range(pairs):
            cols = slice(p * LANE, (p + 1) * LANE)
            do_heads += [jnp.where(first, do[:, cols], 0.0).astype(BF16), jnp.where(first, 0.0, do[:, cols]).astype(BF16)]
            q_heads += [jnp.where(first, qv[:, cols], 0.0).astype(BF16), jnp.where(first, 0.0, qv[:, cols]).astype(BF16)]
        later_mat = _triangle(tk, lambda r, cidx: r > cidx)
        before_mat = _triangle(tk, lambda r, cidx: r < cidx)
        dq_acc[...] = jnp.zeros_like(dq_acc)
        carry_ref[...] = jnp.zeros_like(carry_ref)

        def block(kb, valid, lo=0):
            rows = pl.ds(pl.multiple_of(kb * tk, tk), tk)
            k_blk = kb_ref[rows, :]
            v_blk = vb_ref[rows, :]
            carries = [carry_ref[h, lo:, :] for h in range(nh)]
            dq_old = [dq_acc[h, lo:, :] for h in range(nh)]
            dk_old = dk_acc[rows, :]
            dv_old = dv_acc[rows, :]
            outs = []
            for h in range(nh):
                cols = slice((h // 2) * LANE, (h // 2 + 1) * LANE)
                q_h, do_h = q_heads[h][lo:], do_heads[h][lo:]
                after = jnp.sum(jnp.where(lane[lo:] == kb, after_ref[lo:, h * LANE:(h + 1) * LANE], 0.0), axis=1,
                                keepdims=True)
                _, log_beta, later = _sb_scores(q_h, k_blk[:, cols], valid, later_mat, after)
                beta = jnp.exp2(log_beta)
                a = _masked(valid, jnp.exp2(log_beta + later))
                da = _dot(do_h, v_blk[:, cols], NT)
                gterm = a * da
                before = _dot(gterm.astype(BF16), before_mat, NN) + carries[h]
                dz_b = _masked(valid, gterm * (1.0 - beta) - beta * before).astype(BF16)
                outs.append((dq_old[h] + _dot(dz_b, k_blk[:, cols], NN), _dot(dz_b, q_h, TN),
                             _dot(a.astype(BF16), do_h, TN),
                             carries[h] + jnp.sum(gterm, axis=1, keepdims=True)))
            for h in range(nh):
                dq_acc[h, lo:, :] = outs[h][0]
                carry_ref[h, lo:, :] = outs[h][3]
            dk_new = [outs[2 * p][1] + outs[2 * p + 1][1] for p in range(pairs)]
            dv_new = [outs[2 * p][2] + outs[2 * p + 1][2] for p in range(pairs)]
            dk_acc[rows, :] = dk_old + (dk_new[0] if pairs == 1 else jnp.concatenate(dk_new, axis=1))
            dv_acc[rows, :] = dv_old + (dv_new[0] if pairs == 1 else jnp.concatenate(dv_new, axis=1))

        def step(kb, _):
            block(kb, None)
            return 0

        lax.fori_loop(0, ratio * i, step, 0)
        masks = _diagonal_masks(t, tk)
        for d in range(ratio):
            block(ratio * i + d, masks[d][d * tk:], d * tk)
        for p in range(pairs):
            dq_ref[:, p * LANE:(p + 1) * LANE] = (jnp.where(first, dq_acc[2 * p], dq_acc[2 * p + 1]) * scale).astype(BF16)

        @pl.when(i == nq - 1)
        def _():
            dk_ref[...] = (dk_acc[...] * LN_2).astype(BF16)
            dv_ref[...] = dv_acc[...].astype(BF16)

        _ride(rider, "finish", jnp.logical_and(grp == groups - 1, i == nq - 1), riding)

    blk = lambda base: pl.BlockSpec((t, wide), lambda g, i: (i, base // pairs + g))
    full = lambda base: pl.BlockSpec((s, wide), lambda g, i: (0, base // pairs + g))
    out_blk = pl.BlockSpec((t, wide), lambda g, i: (i, g))
    out_full = pl.BlockSpec((s, wide), lambda g, i: (0, g))
    big = jax.ShapeDtypeStruct((s, WIDTH), BF16)
    extra = _rider_call_args(rider, 7, 4)
    outs = pl.pallas_call(
        body, name=name, grid=(groups, nq),
        in_specs=[blk(COL_SB_Q), full(COL_SB_K), full(COL_SB_V), blk(COL_SB_G), out_blk,
                  pl.BlockSpec((t, nh * LANE), lambda g, i: (i, g)), out_blk] + extra["in_specs"],
        out_specs=[out_blk, out_full, out_full, out_blk] + extra["out_specs"],
        out_shape=[big, big, big, big] + extra["out_shape"],
        input_output_aliases=extra["aliases"],
        scratch_shapes=[pltpu.VMEM((s, wide), BF16), pltpu.VMEM((s, wide), BF16),
                        pltpu.VMEM((s, wide), F32), pltpu.VMEM((s, wide), F32),
                        pltpu.VMEM((nh, t, LANE), F32), pltpu.VMEM((nh, t, 1), F32)] + extra["scratch"],
        compiler_params=pltpu.CompilerParams(dimension_semantics=("arbitrary", "arbitrary")),
    )(u, u, u, u, o, after, dy, *extra["inputs"])
    return outs[:4], outs[4:]


def _gate_fwd(u, ys, w_branch, name, ts=256):
    s = u.shape[0]

    def body(m0, m1, m2, y0, y1, y2, w_ref, p0, p1, p2, out_ref):
        tot = None
        for n, (m_ref, y_ref, p_ref) in enumerate(((m0, y0, p0), (m1, y1, p1), (m2, y2, p2))):
            proj = _dot(y_ref[...], w_ref[n], NN)
            p_ref[...] = proj.astype(BF16)
            term = _sigmoid(m_ref[...].astype(F32)) * proj
            tot = term if tot is None else tot + term
        out_ref[...] = tot.astype(BF16)

    mspec = lambda n: pl.BlockSpec((ts, D_MODEL), lambda i: (i, COL_MERGE_1024 + n))
    row = pl.BlockSpec((ts, D_MODEL), lambda i: (i, 0))
    yspec = pl.BlockSpec((ts, WIDTH), lambda i: (i, 0))
    big = jax.ShapeDtypeStruct((s, D_MODEL), BF16)
    outs = pl.pallas_call(
        body, name=name, grid=(s // ts,),
        in_specs=[mspec(0), mspec(1), mspec(2), yspec, yspec, yspec,
                  pl.BlockSpec(w_branch.shape, lambda i: (0, 0, 0))],
        out_specs=[row] * 4, out_shape=[big] * 4,
    )(u, u, u, *ys, w_branch)
    return outs[:3], outs[3]


def _gate_bwd(u, projs, dmerged, w_branch, name, ts=256, rider=None):
    s = u.shape[0]
    steps = s // ts

    def body(*refs):
        own, riding = _split_refs(refs, 8, 9, 0, rider)
        m0, m1, m2, p0, p1, p2, dm_ref, w_ref, dp0, dp1, dp2, dl0, dl1, dl2, dy0, dy1, dy2 = own
        _ride(rider, "start", pl.program_id(0) == 0, riding)
        dm = dm_ref[...].astype(F32)
        for n, (m_ref, p_ref, dp_ref, dl_ref, dy_ref) in enumerate(((m0, p0, dp0, dl0, dy0), (m1, p1, dp1, dl1, dy1),
                                                                    (m2, p2, dp2, dl2, dy2))):
            gate = _sigmoid(m_ref[...].astype(F32))
            dp = (dm * gate).astype(BF16)
            dp_ref[...] = dp
            dl_ref[...] = (dm * p_ref[...].astype(F32) * gate * (1.0 - gate)).astype(BF16)
            dy_ref[...] = _dot(dp, w_ref[n], NT)
        _ride(rider, "finish", pl.program_id(0) == steps - 1, riding)

    mspec = lambda n: pl.BlockSpec((ts, D_MODEL), lambda i: (i, COL_MERGE_1024 + n))
    row = pl.BlockSpec((ts, D_MODEL), lambda i: (i, 0))
    yspec = pl.BlockSpec((ts, WIDTH), lambda i: (i, 0))
    big = jax.ShapeDtypeStruct((s, D_MODEL), BF16)
    extra = _rider_call_args(rider, 8, 9)
    outs = pl.pallas_call(
        body, name=name, grid=(steps,),
        in_specs=[mspec(0), mspec(1), mspec(2), row, row, row, row,
                  pl.BlockSpec(w_branch.shape, lambda i: (0, 0, 0))] + extra["in_specs"],
        out_specs=[row] * 6 + [yspec] * 3 + extra["out_specs"],
        out_shape=[big] * 6 + [jax.ShapeDtypeStruct((s, WIDTH), F32)] * 3 + extra["out_shape"],
        input_output_aliases=extra["aliases"], scratch_shapes=extra["scratch"],
        compiler_params=pltpu.CompilerParams(dimension_semantics=("arbitrary",)),
    )(u, u, u, *projs, dmerged, w_branch, *extra["inputs"])
    return outs[:3], outs[3:6], outs[6:9], outs[9:]


def _as_rows(a):
    return a.reshape(-1, a.shape[-1])


def _row_tile(rows, cols, bytes_per_row_elem=4, cap=1 << 20):
    tr = rows
    while tr * cols * bytes_per_row_elem > cap and tr % 2 == 0 and (tr // 2) % 16 == 0:
        tr //= 2
    return tr


def _cast_bf16(a, name):
    a2 = _as_rows(a)
    rows, cols = a2.shape
    tr = _row_tile(rows, cols)

    def body(a_ref, o_ref):
        o_ref[...] = a_ref[...].astype(BF16)

    spec = pl.BlockSpec((tr, cols), lambda i: (i, 0))
    out = pl.pallas_call(body, name=name, grid=(rows // tr,), in_specs=[spec], out_specs=spec,
                         out_shape=jax.ShapeDtypeStruct((rows, cols), BF16))(a2)
    return out.reshape(a.shape)


def _adamw(w, g, m, v, name):
    shape = w.shape
    w2, g2, m2, v2 = (_as_rows(a) for a in (w, g, m, v))
    rows, cols = w2.shape
    tr = _row_tile(rows, cols)
    c1 = 1.0 - ADAM_B1 ** ADAM_STEP
    c2 = 1.0 - ADAM_B2 ** ADAM_STEP

    def body(w_ref, g_ref, m_ref, v_ref, d_ref, nm_ref, nv_ref):
        gv = g_ref[...]
        nm = ADAM_B1 * m_ref[...] + (1.0 - ADAM_B1) * gv
        nv = ADAM_B2 * v_ref[...] + (1.0 - ADAM_B2) * (gv * gv)
        nm_ref[...] = nm
        nv_ref[...] = nv
        d_ref[...] = -ADAM_LR * ((nm / c1) / (jnp.sqrt(nv / c2) + ADAM_EPS) + ADAM_WD * w_ref[...])

    spec = pl.BlockSpec((tr, cols), lambda i: (i, 0))
    sds = jax.ShapeDtypeStruct((rows, cols), F32)
    outs = pl.pallas_call(body, name=name, grid=(rows // tr,), in_specs=[spec] * 4, out_specs=[spec] * 3,
                          out_shape=[sds] * 3)(w2, g2, m2, v2)
    return tuple(o.reshape(shape) for o in outs)


def _sum_slots(a, out_dtype, name):
    n = a.shape[0]
    a3 = a.reshape(n, -1, a.shape[-1])
    _, rows, cols = a3.shape
    tr = _row_tile(rows, cols * n)

    def body(a_ref, o_ref):
        tot = a_ref[0].astype(F32)
        for k in range(1, n):
            tot = tot + a_ref[k].astype(F32)
        o_ref[...] = tot.astype(out_dtype)

    out = pl.pallas_call(
        body, name=name, grid=(rows // tr,),
        in_specs=[pl.BlockSpec((n, tr, cols), lambda i: (0, i, 0))],
        out_specs=pl.BlockSpec((tr, cols), lambda i: (i, 0)),
        out_shape=jax.ShapeDtypeStruct((rows, cols), out_dtype))(a3)
    return out.reshape(a.shape[1:])


def _chip_sum(own, recv, axis, core, name):
    half = recv.shape
    nd = len(half)
    last = nd - 1
    if axis == last:
        tl, nt = half[last], 1
    else:
        tl = min(half[last], 2048)
        nt = half[last] // tl
    block = half[:last] + (tl,)

    def own_index(i, core_ref):
        idx = [0] * nd
        idx[last] = i
        if axis == last:
            idx[last] = core_ref[0]
        else:
            idx[axis] = core_ref[0]
        return tuple(idx)

    def recv_index(i, core_ref):
        idx = [0] * nd
        idx[last] = i
        return tuple(idx)

    def body(core_ref, own_ref, recv_ref, o_ref):
        o_ref[...] = (own_ref[...] + recv_ref[...]).astype(BF16)

    return pl.pallas_call(
        body, name=name,
        grid_spec=pltpu.PrefetchScalarGridSpec(
            num_scalar_prefetch=1, grid=(nt,),
            in_specs=[pl.BlockSpec(block, own_index), pl.BlockSpec(block, recv_index)],
            out_specs=pl.BlockSpec(block, recv_index)),
        out_shape=jax.ShapeDtypeStruct(half, BF16),
    )(core, own, recv)


def _mesh_position():
    return lax.axis_index("x"), lax.axis_index("y"), lax.axis_index("c")


def _other_chips(x, y):
    return [(1 - x, y), (x, 1 - y), (1 - x, 1 - y)]


ALL_FLIPS = [(0, 0, 1), (1, 0, 0), (0, 1, 0), (1, 1, 0), (1, 0, 1), (0, 1, 1), (1, 1, 1)]


def _half(ref, axis, which, size):
    idx = [slice(None)] * len(ref.shape)
    idx[axis] = pl.ds(which * size, size)
    return ref.at[tuple(idx)]


def _sub(ref, picks):
    idx = [slice(None)] * len(ref.shape)
    for axis, start, size in picks:
        idx[axis] = pl.ds(start, size)
    return ref.at[tuple(idx)]


def _remote(src, dst, sems_send, sems_recv, k, to):
    return pltpu.make_async_remote_copy(src_ref=src, dst_ref=dst, send_sem=sems_send.at[k], recv_sem=sems_recv.at[k],
                                        device_id=to, device_id_type=MESH)


def _cast_shard(w, layer, shard_axis, pos, name, tr=512):
    shape = w.shape[1:]
    nd = len(shape)
    assert shard_axis in (nd - 1, nd - 2)
    rows, cols = shape[-2:]
    tr = min(tr, rows)
    nt = rows // tr
    lead = shape[:-2]
    full = list(shape)
    full[shard_axis] *= N_CHIPS
    block = (1,) * len(lead) + (tr, cols)

    def in_index(*args):
        return (layer, *args[:-1], 0)

    def out_index(*args):
        *g, pos_ref = args
        if shard_axis == nd - 1:
            return (*g, pos_ref[1])
        return (*g[:-1], pos_ref[1] * nt + g[-1], 0)

    def body(pos_ref, a_ref, o_ref):
        o_ref[...] = a_ref[...].astype(BF16)

    return pl.pallas_call(
        body, name=name,
        grid_spec=pltpu.PrefetchScalarGridSpec(
            num_scalar_prefetch=1, grid=lead + (nt,),
            in_specs=[pl.BlockSpec((None,) + block, in_index)], out_specs=pl.BlockSpec(block, out_index)),
        out_shape=jax.ShapeDtypeStruct(tuple(full), BF16),
    )(pos, w)


class _Rider:
    def __init__(self, inputs, out_shape, aliases, scratch, start, middle, finish):
        self.inputs, self.out_shape, self.aliases, self.scratch = inputs, out_shape, aliases, scratch
        self.start, self.middle, self.finish = start, middle, finish


def _weight_gather_rider(fulls, layout):
    n = len(fulls)

    def copies(outs, sems):
        send_sems, recv_sems = sems
        x, y, c = _mesh_position()
        chips = _other_chips(x, y)
        sibling = (x, y, 1 - c)
        mine = 2 * x + y

        def place(t, chip, core):
            sh_axis, sh_size, half_axis, half_size = layout[t]
            return _sub(outs[t], [(sh_axis, chip * sh_size, sh_size), (half_axis, core * half_size, half_size)])

        direct, arrive, forward, arrive_fwd = [], [], [], []
        for t in range(n):
            for k, (px, py) in enumerate(chips):
                theirs = 2 * px + py
                direct.append(_remote(place(t, mine, c), place(t, mine, c), send_sems, recv_sems, 6 * t + k, (px, py, c)))
                arrive.append(_remote(place(t, theirs, c), place(t, theirs, c), send_sems, recv_sems, 6 * t + k, (px, py, c)))
                forward.append(_remote(place(t, theirs, c), place(t, theirs, c), send_sems, recv_sems, 6 * t + 3 + k, sibling))
                arrive_fwd.append(_remote(place(t, theirs, 1 - c), place(t, theirs, 1 - c), send_sems, recv_sems,
                                          6 * t + 3 + k, sibling))
        return direct, arrive, forward, arrive_fwd

    def start(ins, outs, sems):
        for cp in copies(outs, sems)[0]:
            cp.start()

    def middle(ins, outs, sems):
        _, arrive, forward, _ = copies(outs, sems)
        for a, f in zip(arrive, forward):
            a.wait_recv()
            f.start()

    def finish(ins, outs, sems):
        direct, _, forward, arrive_fwd = copies(outs, sems)
        for cp in arrive_fwd:
            cp.wait_recv()
        for cp in direct + forward:
            cp.wait_send()

    return _Rider(list(fulls), [jax.ShapeDtypeStruct(a.shape, a.dtype) for a in fulls], {k: k for k in range(n)},
                  [pltpu.SemaphoreType.DMA((6 * n,)), pltpu.SemaphoreType.DMA((6 * n,))], start, middle, finish)


WEIGHT_LAYOUT = [(1, 2048, 0, 512), (2, 256, 1, 256), (0, 256, 1, 512)]


def _gather_weights(fulls, conv_w):
    rider = _weight_gather_rider(fulls, WEIGHT_LAYOUT)
    n = len(fulls)

    def body(*refs):
        cw, outs, cw_f = refs[n], refs[n + 1:2 * n + 1], refs[2 * n + 1]
        sems, (cw_send, cw_recv, local_sem) = refs[2 * n + 2:2 * n + 4], refs[2 * n + 4:]
        x, y, c = _mesh_position()
        chips = _other_chips(x, y)
        mine = 2 * x + y
        local = pltpu.make_async_copy(cw, cw_f.at[mine], local_sem.at[0])
        local.start()
        rider.start(None, outs, sems)
        small = [_remote(cw, cw_f.at[mine], cw_send, cw_recv, k, (*chip, c)) for k, chip in enumerate(chips)]
        for cp in small:
            cp.start()
        rider.middle(None, outs, sems)
        rider.finish(None, outs, sems)
        for k, (px, py) in enumerate(chips):
            _remote(cw, cw_f.at[2 * px + py], cw_send, cw_recv, k, (px, py, c)).wait_recv()
        for cp in small:
            cp.wait_send()
        local.wait()

    outs = pl.pallas_call(
        body, name="gather_weights",
        in_specs=[ANY] * (n + 1), out_specs=[ANY] * (n + 1),
        out_shape=rider.out_shape + [jax.ShapeDtypeStruct((N_CHIPS,) + conv_w.shape, F32)],
        input_output_aliases=rider.aliases,
        scratch_shapes=rider.scratch + [pltpu.SemaphoreType.DMA((3,)), pltpu.SemaphoreType.DMA((3,)),
                                        pltpu.SemaphoreType.DMA((1,))],
    )(*fulls, conv_w)
    return outs[:n], outs[n]


def _swap_rider(items):
    n = len(items)
    halves = []
    for a, axis in items:
        shp = list(a.shape)
        shp[axis] //= 2
        halves.append(tuple(shp))

    def copies(ins, outs, sems):
        x, y, c = _mesh_position()
        return [_remote(_half(ins[k], items[k][1], 1 - c, halves[k][items[k][1]]), outs[k], sems[0], sems[1], k,
                        (x, y, 1 - c)) for k in range(n)]

    def start(ins, outs, sems):
        for cp in copies(ins, outs, sems):
            cp.start()

    def finish(ins, outs, sems):
        for cp in copies(ins, outs, sems):
            cp.wait()

    return _Rider([a for a, _ in items], [jax.ShapeDtypeStruct(h, a.dtype) for h, (a, _) in zip(halves, items)], {},
                  [pltpu.SemaphoreType.DMA((n,)), pltpu.SemaphoreType.DMA((n,))], start, None, finish)


def _swap_halves(items, name):
    rider = _swap_rider(items)
    n = len(items)

    def body(*refs):
        parts = (refs[:n], refs[n:2 * n], refs[2 * n:])
        rider.start(*parts)
        rider.finish(*parts)

    return pl.pallas_call(
        body, name=name, in_specs=[ANY] * n, out_specs=[ANY] * n, out_shape=rider.out_shape,
        scratch_shapes=rider.scratch,
    )(*rider.inputs)


def _grad_exchange_rider(items):
    n = len(items)
    slices = []
    for a, axis in items:
        shp = list(a.shape)
        shp[axis] //= N_CHIPS
        slices.append(tuple(shp))

    def copies(ins, outs, sems):
        send_sems, recv_sems = sems
        x, y, c = _mesh_position()
        made = []
        for k in range(n):
            axis = items[k][1]
            for r, (px, py) in enumerate(_other_chips(x, y)):
                made.append(_remote(_half(ins[k], axis, 2 * px + py, slices[k][axis]), outs[k].at[r],
                                    send_sems, recv_sems, 3 * k + r, (px, py, c)))
        return made

    def start(ins, outs, sems):
        for cp in copies(ins, outs, sems):
            cp.start()

    def finish(ins, outs, sems):
        for cp in copies(ins, outs, sems):
            cp.wait()

    return _Rider([a for a, _ in items], [jax.ShapeDtypeStruct((N_CHIPS - 1,) + s, BF16) for s in slices], {},
                  [pltpu.SemaphoreType.DMA((3 * n,)), pltpu.SemaphoreType.DMA((3 * n,))], start, None, finish)


def _gather_small(small):
    def body(small_ref, small_all, send_sems, recv_sems, local_sem):
        x, y, c = _mesh_position()
        me = 4 * x + 2 * y + c
        local = pltpu.make_async_copy(small_ref, small_all.at[me], local_sem.at[0])
        local.start()
        copies = [_remote(small_ref, small_all.at[me], send_sems, recv_sems, r, (x ^ fx, y ^ fy, c ^ fc))
                  for r, (fx, fy, fc) in enumerate(ALL_FLIPS)]
        for cp in copies:
            cp.start()
        for cp in copies:
            cp.wait()
        local.wait()

    return pl.pallas_call(
        body, name="gather_small_grads", in_specs=[ANY], out_specs=ANY,
        out_shape=jax.ShapeDtypeStruct((2 * N_CHIPS,) + small.shape, F32),
        scratch_shapes=[pltpu.SemaphoreType.DMA((len(ALL_FLIPS),)), pltpu.SemaphoreType.DMA((len(ALL_FLIPS),)),
                        pltpu.SemaphoreType.DMA((1,))],
    )(small)


def _sum_chips(recv, own, shard_axis, split_axis, pos, dest, layer, name, tr=128):
    sl = recv.shape[1:]
    nd = len(sl)
    tiled = nd == 2 and sl[0] > tr
    nt = sl[0] // tr if tiled else 1
    block = ((tr,) + sl[1:]) if tiled else sl
    shard = list(sl)
    shard[split_axis] *= 2

    def recv_index(i, pos_ref):
        return (0, i) + (0,) * (nd - 1) if tiled else (0,) * (nd + 1)

    def own_index(i, pos_ref):
        idx = [0] * nd
        idx[shard_axis] = pos_ref[1]
        if tiled:
            idx[0] = pos_ref[1] * nt + i if shard_axis == 0 else i
        return tuple(idx)

    def out_index(i, pos_ref):
        idx = [0] * nd
        idx[split_axis] = pos_ref[0]
        if tiled:
            idx[0] = pos_ref[0] * nt + i if split_axis == 0 else i
        return (layer, *idx)

    def body(pos_ref, recv_ref, own_ref, *rest):
        o_ref = rest[-1]
        tot = own_ref[...].astype(F32)
        for k in range(N_CHIPS - 1):
            tot = tot + recv_ref[k].astype(F32)
        o_ref[0] = tot

    in_specs = [pl.BlockSpec((N_CHIPS - 1,) + block, recv_index), pl.BlockSpec(block, own_index)]
    args = [pos, recv, own]
    aliases = {}
    if dest is not None:
        in_specs.append(ANY)
        args.append(dest)
        aliases = {3: 0}
    return pl.pallas_call(
        body, name=name,
        grid_spec=pltpu.PrefetchScalarGridSpec(
            num_scalar_prefetch=1, grid=(nt,), in_specs=in_specs,
            out_specs=pl.BlockSpec((1,) + block, out_index)),
        out_shape=jax.ShapeDtypeStruct((DEPTH,) + tuple(shard), F32),
        input_output_aliases=aliases,
    )(*args)


def _share_halves(bufs, name):
    n = len(bufs)

    def body(*refs):
        outs, (send_sems, recv_sems) = refs[n:2 * n], refs[2 * n:]
        x, y, c = _mesh_position()
        copies = []
        for k, (a, axis) in enumerate(bufs):
            size = a.shape[1 + axis] // 2
            mine = _half(outs[k], 1 + axis, c, size)
            copies.append(_remote(mine, mine, send_sems, recv_sems, k, (x, y, 1 - c)))
        for cp in copies:
            cp.start()
        for cp in copies:
            cp.wait()

    return pl.pallas_call(
        body, name=name, in_specs=[ANY] * n, out_specs=[ANY] * n,
        out_shape=[jax.ShapeDtypeStruct(a.shape, F32) for a, _ in bufs],
        input_output_aliases={k: k for k in range(n)},
        scratch_shapes=[pltpu.SemaphoreType.DMA((n,)), pltpu.SemaphoreType.DMA((n,))],
    )(*[a for a, _ in bufs])


def _layer_fwd(x, p, l, rider=None):
    tag = f"l{l}_"
    h = _rms_fwd(x, p["pre_g"], tag + "pre_norm")
    u = _matmul(h, p["w_in"], "nn", BF16, tag + "in_proj")
    y_pool = _pool_fwd(u, p["pool_w"], p["pool_scale"], tag + "pool")
    y_conv = _conv_fwd(u, p["conv_w"], p["conv_b"], tag + "conv")
    (o_sb, y_sb, sb_after), carried = _sb_fwd(u, tag + "stickbreak", rider=rider)
    ys = [y_pool, y_conv, y_sb]
    projs, merged = _gate_fwd(u, ys, p["w_branch"], tag + "merge")
    out = _matmul(merged, p["w_out"], "nn", F32, tag + "out_proj")
    saved = dict(x=x, h=h, u=u, ys=ys, o_sb=o_sb, sb_after=sb_after, projs=projs, merged=merged, out=out)
    return out, saved, carried


def _layer_bwd(dy, p, saved, l, merge_rider=None, early=None, late=None):
    tag = f"l{l}_bwd_"
    u = saved["u"]
    d_out, g_post = _rms_bwd(saved["out"], p["post_g"], dy, None, BF16, tag + "post_norm")
    d_merged = _matmul(d_out, p["w_out"], "nt", BF16, tag + "out_proj_dx")
    g_w_out = _matmul(saved["merged"], d_out, "tn", F32, tag + "out_proj_dw", tk=2048)
    d_projs, d_logits, d_ys, carried_merge = _gate_bwd(u, saved["projs"], d_merged, p["w_branch"], tag + "merge",
                                                       rider=merge_rider)
    g_w_branch = jnp.stack([_matmul(saved["ys"][n], d_projs[n], "tn", F32, tag + f"branch_dw{n}", tk=2048)
                            for n in range(3)])
    rider = early(g_w_branch, g_w_out, carried_merge) if early else None
    d_pv, d_pg, g_pool_w, g_pool_scale = _pool_bwd(u, d_ys[0], p["pool_w"], p["pool_scale"], tag + "pool")
    d_cx, d_cgb, d_cgc, d_cg, g_conv_w, g_conv_b = _conv_bwd(u, d_ys[1], p["conv_w"], p["conv_b"], tag + "conv")
    (d_q, d_k, d_v, d_sg), carried_attn = _sb_bwd(u, saved["o_sb"], saved["sb_after"], d_ys[2], tag + "stickbreak",
                                                  rider=rider)
    du = jnp.concatenate([d_pv, d_pg, d_cx, d_cgb, d_cgc, d_cg, d_q, d_k, d_v, d_sg] + list(d_logits), axis=1)
    g_w_in = _matmul(saved["h"], du, "tn", F32, tag + "in_proj_dw", tk=2048)
    rider = late(g_w_in) if late else None
    dh = _matmul(du, p["w_in"], "nt", BF16, tag + "in_proj_dx", tk=2048, rider=rider)
    dh, carried_dx = dh if rider else (dh, [])
    dx, g_pre = _rms_bwd(saved["x"], p["pre_g"], dh, dy, F32, tag + "pre_norm")
    grads = dict(w_in=g_w_in, w_branch=g_w_branch, w_out=g_w_out, pre_g=g_pre, post_g=g_post,
                 pool_w=g_pool_w, pool_scale=g_pool_scale, conv_w=g_conv_w, conv_b=g_conv_b)
    return dx, grads, carried_attn, carried_dx


SMALL_ORDER = ["pre_g", "pool_w", "pool_scale", "conv_w", "conv_b", "post_g"]


def _pack_small(per_layer, loss_part):
    parts, spans, at = [], {}, 0
    for name in SMALL_ORDER:
        a = jnp.stack([per_layer[l][name] for l in range(DEPTH)]).reshape(-1, LANE)
        parts.append(a)
        spans[name] = (at, a.shape[0])
        at += a.shape[0]
    parts.append(jnp.broadcast_to(loss_part, (8, LANE)))
    spans["loss"] = (at, 8)
    return jnp.concatenate(parts, axis=0), spans


def kernel(x, pre_norm_g, w_in, pool_w, pool_scale, conv_w, conv_b, w_branch, w_out, post_norm_g, loss_target, m_pre_norm_g, m_w_in, m_pool_w, m_pool_scale, m_conv_w, m_conv_b, m_w_branch, m_w_out, m_post_norm_g, v_pre_norm_g, v_w_in, v_pool_w, v_pool_scale, v_conv_w, v_conv_b, v_w_branch, v_w_out, v_post_norm_g):
    mx, my, mc = _mesh_position()
    chip = 2 * mx + my
    core = mc.astype(jnp.int32).reshape(1)
    pos = jnp.stack([mc, chip]).astype(jnp.int32)

    names = ["w_in", "w_branch", "w_out"]
    given = dict(w_in=w_in, w_branch=w_branch, w_out=w_out)
    in_place = [[_cast_shard(given[n], l, WEIGHT_LAYOUT[i][0], pos, f"cast_{n}{l}") for i, n in enumerate(names)]
                for l in range(DEPTH)]
    gathered, conv_w_by_chip = _gather_weights(in_place[0], conv_w)
    conv_w_f = conv_w_by_chip.transpose(1, 2, 0, 3).reshape(DEPTH, 3, WIDTH)
    pool_w_b = _cast_bf16(pool_w, "cast_pool_w")

    def layer_params(l, big):
        return dict(pre_g=pre_norm_g[l:l + 1], post_g=post_norm_g[l:l + 1], w_in=big[0], w_branch=big[1],
                    w_out=big[2], pool_w=pool_w_b[l], pool_scale=pool_scale[l:l + 1], conv_w=conv_w_f[l],
                    conv_b=conv_b[l:l + 1])

    act = x[0]
    params, saved = [], []
    for l in range(DEPTH):
        params.append(layer_params(l, gathered))
        rider = _weight_gather_rider(in_place[l + 1], WEIGHT_LAYOUT) if l + 1 < DEPTH else None
        out, sv, gathered = _layer_fwd(act, params[l], l, rider)
        saved.append(sv)
        if l < DEPTH - 1:
            act = _resid_out(act, out, params[l]["post_g"], None, f"l{l}_resid")
    dy, loss_part = _resid_out(act, saved[-1]["out"], params[-1]["post_g"], loss_target[0], "loss_head")

    split_axis = dict(w_in=0, w_branch=1, w_out=1)
    shard_axis = dict(w_in=1, w_branch=2, w_out=0)
    grads = [None] * DEPTH
    chip_sums = [dict() for _ in range(DEPTH)]
    by_chip = [dict() for _ in range(DEPTH)]

    def reduce_in_chip(l, which, g):
        items = [(g[n], split_axis[n]) for n in which]
        from_sibling = _swap_halves(items, f"swap_grad_halves{l}_{which[0]}")
        for n, (a, axis), r in zip(which, items, from_sibling):
            chip_sums[l][n] = _chip_sum(a, r, axis, core, f"chip_sum{l}_{n}")

    def exchange_rider(keys):
        return _grad_exchange_rider([(chip_sums[l][n], shard_axis[n]) for l, n in keys])

    waiting = []
    for l in reversed(range(DEPTH)):
        sent_early, sent_late = list(waiting) + [(l, "w_branch"), (l, "w_out")], [(l, "w_in")]
        waiting_items = [(grads[ll][n], split_axis[n]) for ll, n in waiting]

        def early(g_w_branch, g_w_out, from_sibling, l=l, keys=sent_early, above=tuple(waiting), items=waiting_items):
            for (ll, n), (a, axis), r in zip(above, items, from_sibling):
                chip_sums[ll][n] = _chip_sum(a, r, axis, core, f"chip_sum{ll}_{n}")
            reduce_in_chip(l, ["w_branch", "w_out"], dict(w_branch=g_w_branch, w_out=g_w_out))
            return exchange_rider(keys)

        def late(g_w_in, l=l, keys=sent_late):
            reduce_in_chip(l, ["w_in"], dict(w_in=g_w_in))
            return exchange_rider(keys)

        if l == DEPTH - 1:
            dy, grads[l], _, _ = _layer_bwd(dy, params[l], saved[l], l)
            waiting = [(l, n) for n in names]
        else:
            dy, grads[l], got_early, got_late = _layer_bwd(dy, params[l], saved[l], l, _swap_rider(waiting_items),
                                                           early, late)
            for (ll, n), r in zip(sent_early + sent_late, list(got_early) + list(got_late)):
                by_chip[ll][n] = r
            waiting = []
    assert not waiting
    grad_x = dy[None]
    small_part, spans = _pack_small(grads, loss_part)
    small_all = _gather_small(small_part)
    bufs = []
    for n in names:
        dest = None
        for l in range(DEPTH):
            dest = _sum_chips(by_chip[l][n], chip_sums[l][n], shard_axis[n], split_axis[n], pos, dest, l,
                              f"sum_chips{l}_{n}")
        bufs.append((dest, split_axis[n]))
    g_w_in, g_w_branch, g_w_out = _share_halves(bufs, "share_grad_halves")

    small_sum = _sum_slots(small_all, F32, "sum_small")
    loss = small_sum[spans["loss"][0], 0]
    small = {}
    for name, like in (("pre_g", pre_norm_g), ("pool_w", pool_w), ("pool_scale", pool_scale), ("conv_b", conv_b),
                       ("post_g", post_norm_g)):
        at, n = spans[name]
        small[name] = small_sum[at:at + n].reshape(like.shape)
    at, n = spans["conv_w"]
    g_conv_w_full = small_sum[at:at + n].reshape(DEPTH, 3, WIDTH)
    g_conv_w = lax.dynamic_slice_in_dim(g_conv_w_full, chip * conv_w.shape[2], conv_w.shape[2], axis=2)

    g = dict(pre_norm_g=small["pre_g"], w_in=g_w_in, pool_w=small["pool_w"], pool_scale=small["pool_scale"],
             conv_w=g_conv_w, conv_b=small["conv_b"], w_branch=g_w_branch, w_out=g_w_out, post_norm_g=small["post_g"])
    w = dict(pre_norm_g=pre_norm_g, w_in=w_in, pool_w=pool_w, pool_scale=pool_scale, conv_w=conv_w, conv_b=conv_b,
             w_branch=w_branch, w_out=w_out, post_norm_g=post_norm_g)
    m = dict(pre_norm_g=m_pre_norm_g, w_in=m_w_in, pool_w=m_pool_w, pool_scale=m_pool_scale, conv_w=m_conv_w,
             conv_b=m_conv_b, w_branch=m_w_branch, w_out=m_w_out, post_norm_g=m_post_norm_g)
    v = dict(pre_norm_g=v_pre_norm_g, w_in=v_w_in, pool_w=v_pool_w, pool_scale=v_pool_scale, conv_w=v_conv_w,
             conv_b=v_conv_b, w_branch=v_w_branch, w_out=v_w_out, post_norm_g=v_post_norm_g)
    order = ["pre_norm_g", "w_in", "pool_w", "pool_scale", "conv_w", "conv_b", "w_branch", "w_out", "post_norm_g"]
    upd = {n: _adamw(w[n], g[n], m[n], v[n], "adamw_" + n) for n in order}
    return (loss, grad_x, *[g[n] for n in order], *[upd[n][0] for n in order], *[upd[n][1] for n in order],
            *[upd[n][2] for n in order])
```

```python
import functools

import jax
import jax.numpy as jnp
from jax import lax
from jax.experimental import pallas as pl
from jax.experimental.pallas import tpu as pltpu

F32 = jnp.float32
BF16 = jnp.bfloat16
MESH = pl.DeviceIdType.MESH
ANY = pl.BlockSpec(memory_space=pl.ANY)

DEPTH = 2
D_MODEL = 1024
WIDTH = 512
N_IN = 8192
N_CHIPS = 4
HEAD_DIM = 64
RMS_EPS = 1e-6
POOL_HALO = 16
CONV_HALO = 16
LANE = 128
COL_POOL_V, COL_POOL_G = 0, 4
COL_CONV_X, COL_CONV_GB, COL_CONV_GC, COL_CONV_G = 8, 12, 16, 20
COL_SB_Q, COL_SB_K, COL_SB_V, COL_SB_G = 24, 28, 32, 36
COL_MERGE_1024 = 5

ADAM_LR, ADAM_B1, ADAM_B2, ADAM_EPS, ADAM_WD, ADAM_STEP = 0.001, 0.9, 0.999, 1e-08, 0.01, 10

NN = (((1,), (0,)), ((), ()))
NT = (((1,), (1,)), ((), ()))
TN = (((0,), (0,)), ((), ()))


def _sigmoid(x):
    return 1.0 / (1.0 + jnp.exp(-x))


def _silu_and_grad(x):
    s = _sigmoid(x)
    return x * s, s * (1.0 + x * (1.0 - s))


def _dot(a, b, dims):
    return lax.dot_general(a, b, dims, preferred_element_type=F32)


def _matmul(a, b, mode, out_dtype, name, tm=1024, tn=1024, tk=1024, b_lead=(), rider=None):
    b_shape = b.shape[len(b_lead):]
    if mode == "nn":
        (m, k), (k2, n) = a.shape, b_shape
    elif mode == "nt":
        (m, k), (n, k2) = a.shape, b_shape
    else:
        (k, m), (k2, n) = a.shape, b_shape
    assert k == k2 and a.dtype == BF16 and b.dtype == BF16
    tm, tn, tk = min(tm, m), min(tn, n), min(tk, k)
    assert m % tm == 0 and n % tn == 0 and k % tk == 0
    nk = k // tk
    dims = {"nn": NN, "nt": NT, "tn": TN}[mode]

    grid = (m // tm, n // tn, nk)

    def at_step(which):
        return functools.reduce(jnp.logical_and, [pl.program_id(d) == (g - 1 if which else 0) for d, g in enumerate(grid)])

    def body(*refs):
        (a_ref, b_ref, o_ref, *scratch), riding = _split_refs(refs, 2, 1, 1 if nk > 1 else 0, rider)
        _ride(rider, "start", at_step(0), riding)
        compute(a_ref, b_ref, o_ref, scratch)
        _ride(rider, "finish", at_step(1), riding)

    def compute(a_ref, b_ref, o_ref, scratch):
        p = _dot(a_ref[...], b_ref[...], dims)
        if nk == 1:
            o_ref[...] = p.astype(o_ref.dtype)
        else:
            acc = scratch[0]
            kk = pl.program_id(2)

            @pl.when(kk == 0)
            def _():
                acc[...] = p

            @pl.when(jnp.logical_and(kk > 0, kk < nk - 1))
            def _():
                acc[...] += p

            @pl.when(kk == nk - 1)
            def _():
                o_ref[...] = (acc[...] + p).astype(o_ref.dtype)

    if mode == "tn":
        a_spec = pl.BlockSpec((tk, tm), lambda i, j, kk: (kk, i))
    else:
        a_spec = pl.BlockSpec((tm, tk), lambda i, j, kk: (i, kk))
    squeezed = (None,) * len(b_lead)
    if mode == "nt":
        b_spec = pl.BlockSpec(squeezed + (tn, tk), lambda i, j, kk: (*b_lead, j, kk))
    else:
        b_spec = pl.BlockSpec(squeezed + (tk, tn), lambda i, j, kk: (*b_lead, kk, j))
    extra = _rider_call_args(rider, 2, 1)
    outs = pl.pallas_call(
        body, name=name, grid=grid,
        in_specs=[a_spec, b_spec] + extra["in_specs"],
        out_specs=[pl.BlockSpec((tm, tn), lambda i, j, kk: (i, j))] + extra["out_specs"],
        out_shape=[jax.ShapeDtypeStruct((m, n), out_dtype)] + extra["out_shape"],
        input_output_aliases=extra["aliases"],
        scratch_shapes=([pltpu.VMEM((tm, tn), F32)] if nk > 1 else []) + extra["scratch"],
        compiler_params=pltpu.CompilerParams(dimension_semantics=("arbitrary",) * 3 if rider else
                                             ("parallel", "parallel", "arbitrary")),
    )(a, b, *extra["inputs"])
    return (outs[0], outs[1:]) if rider else outs[0]


def _rms_fwd(x, g, name, ts=512):
    s, d = x.shape

    def body(x_ref, g_ref, h_ref):
        xv = x_ref[...]
        r = lax.rsqrt(jnp.mean(xv * xv, axis=-1, keepdims=True) + RMS_EPS)
        h_ref[...] = (xv * r * g_ref[...]).astype(BF16)

    return pl.pallas_call(
        body, name=name, grid=(s // ts,),
        in_specs=[pl.BlockSpec((ts, d), lambda i: (i, 0)), pl.BlockSpec((1, d), lambda i: (0, 0))],
        out_specs=pl.BlockSpec((ts, d), lambda i: (i, 0)),
        out_shape=jax.ShapeDtypeStruct((s, d), BF16),
    )(x, g)


def _rms_bwd(xin, g, dh, resid, out_dtype, name, ts=512):
    s, d = xin.shape
    has_resid = resid is not None

    def body(*refs):
        if has_resid:
            x_ref, g_ref, dh_ref, res_ref, dx_ref, dg_ref = refs
        else:
            x_ref, g_ref, dh_ref, dx_ref, dg_ref = refs
        xv = x_ref[...]
        dhv = dh_ref[...].astype(F32)
        r = lax.rsqrt(jnp.mean(xv * xv, axis=-1, keepdims=True) + RMS_EPS)
        nrm = xv * r
        dn = dhv * g_ref[...]
        dx = r * (dn - nrm * jnp.mean(dn * nrm, axis=-1, keepdims=True))
        if has_resid:
            dx = dx + res_ref[...]
        dx_ref[...] = dx.astype(dx_ref.dtype)
        part = jnp.sum(dhv * nrm, axis=0, keepdims=True)

        @pl.when(pl.program_id(0) == 0)
        def _():
            dg_ref[...] = part

        @pl.when(pl.program_id(0) > 0)
        def _():
            dg_ref[...] += part

    row = pl.BlockSpec((ts, d), lambda i: (i, 0))
    vec = pl.BlockSpec((1, d), lambda i: (0, 0))
    ins = [xin, g, dh] + ([resid] if has_resid else [])
    return pl.pallas_call(
        body, name=name, grid=(s // ts,),
        in_specs=[row, vec, row] + ([row] if has_resid else []),
        out_specs=[row, vec],
        out_shape=[jax.ShapeDtypeStruct((s, d), out_dtype), jax.ShapeDtypeStruct((1, d), F32)],
        compiler_params=pltpu.CompilerParams(dimension_semantics=("arbitrary",)),
    )(*ins)


def _resid_out(x, out, g, target, name, ts=512):
    s, d = x.shape
    has_loss = target is not None

    def body(*refs):
        if has_loss:
            x_ref, o_ref, g_ref, t_ref, dy_ref, loss_ref = refs
        else:
            x_ref, o_ref, g_ref, y_ref = refs
        ov = o_ref[...]
        r = lax.rsqrt(jnp.mean(ov * ov, axis=-1, keepdims=True) + RMS_EPS)
        yv = x_ref[...] + ov * r * g_ref[...]
        if not has_loss:
            y_ref[...] = yv
            return
        err = yv - t_ref[...]
        dy_ref[...] = err * (1.0 / d)
        part = jnp.sum(jnp.sum(err * err, axis=-1, keepdims=True), axis=0, keepdims=True) * (0.5 / d)
        part = jnp.broadcast_to(part, (1, LANE))

        @pl.when(pl.program_id(0) == 0)
        def _():
            loss_ref[...] = part

        @pl.when(pl.program_id(0) > 0)
        def _():
            loss_ref[...] += part

    row = pl.BlockSpec((ts, d), lambda i: (i, 0))
    vec = pl.BlockSpec((1, d), lambda i: (0, 0))
    if has_loss:
        return pl.pallas_call(
            body, name=name, grid=(s // ts,),
            in_specs=[row, row, vec, row],
            out_specs=[row, pl.BlockSpec((1, LANE), lambda i: (0, 0))],
            out_shape=[jax.ShapeDtypeStruct((s, d), F32), jax.ShapeDtypeStruct((1, LANE), F32)],
            compiler_params=pltpu.CompilerParams(dimension_semantics=("arbitrary",)),
        )(x, out, g, target)
    return pl.pallas_call(
        body, name=name, grid=(s // ts,),
        in_specs=[row, row, vec], out_specs=row,
        out_shape=jax.ShapeDtypeStruct((s, d), F32),
    )(x, out, g)


def _rows_before(ref, start, n, halo):
    if start == 0:
        return jnp.concatenate([jnp.zeros((halo, ref.shape[1]), F32), ref[0:n, :].astype(F32)], axis=0)
    return ref[start - halo:start + n, :].astype(F32)


def _rows_after(ref, start, n, halo):
    if start + n == ref.shape[0]:
        return jnp.concatenate([ref[start:start + n, :].astype(F32), jnp.zeros((halo, ref.shape[1]), F32)], axis=0)
    return ref[start:start + n + halo, :].astype(F32)


def _pick_window(group, s2, s4, s8, s16):
    return jnp.where(group == 0, s2, jnp.where(group == 1, s4, jnp.where(group == 2, s8, s16)))


def _trailing_sums(ext, group):
    s2 = ext + pltpu.roll(ext, 1, 0)
    s4 = s2 + pltpu.roll(s2, 2, 0)
    s8 = s4 + pltpu.roll(s4, 4, 0)
    s16 = s8 + pltpu.roll(s8, 8, 0)
    return _pick_window(group, s2, s4, s8, s16)


def _leading_sums(ext, group):
    n = ext.shape[0]
    s2 = ext + pltpu.roll(ext, n - 1, 0)
    s4 = s2 + pltpu.roll(s2, n - 2, 0)
    s8 = s4 + pltpu.roll(s4, n - 4, 0)
    s16 = s8 + pltpu.roll(s8, n - 8, 0)
    return _pick_window(group, s2, s4, s8, s16)


def _window_count(start, n, group):
    pos = start + lax.broadcasted_iota(jnp.int32, (n, LANE), 0)
    return jnp.minimum(pos + 1, 2 << group).astype(F32)


def _pooled(v_ref, start, n, group):
    ext = _rows_before(v_ref, start, n, POOL_HALO)
    sums = _trailing_sums(ext, group)[POOL_HALO:, :]
    return sums / _window_count(start, n, group) - ext[POOL_HALO:, :]


def _pool_fwd(u, pool_w, pool_scale, name, ts=512):
    s = u.shape[0]

    def body(v_ref, gate_ref, w_ref, sc_ref, y_ref):
        group = pl.program_id(0)
        for c in range(s // ts):
            a = c * ts
            pooled = _pooled(v_ref, a, ts, group)
            mixed = _dot(pooled.astype(BF16), w_ref[...], NN)
            gate = gate_ref[a:a + ts, :].astype(F32)
            y_ref[a:a + ts, :] = (mixed * sc_ref[...] * (gate * _sigmoid(gate))).astype(BF16)

    col = lambda base: pl.BlockSpec((s, LANE), lambda g: (0, base + g))
    return pl.pallas_call(
        body, name=name, grid=(4,),
        in_specs=[col(COL_POOL_V), col(COL_POOL_G),
                  pl.BlockSpec((None, LANE, LANE), lambda g: (g, 0, 0)),
                  pl.BlockSpec((1, LANE), lambda g: (0, g))],
        out_specs=pl.BlockSpec((s, LANE), lambda g: (0, g)),
        out_shape=jax.ShapeDtypeStruct((s, WIDTH), BF16),
    )(u, u, pool_w, pool_scale)


def _pool_bwd(u, dy, pool_w, pool_scale, name, ts=512):
    s = u.shape[0]

    def body(v_ref, gate_ref, dy_ref, w_ref, sc_ref, dv_ref, dgate_ref, dw_ref, dsc_ref):
        group = pl.program_id(0)
        w = w_ref[...]
        scale = sc_ref[...]
        dw = jnp.zeros((LANE, LANE), F32)
        dsc = jnp.zeros((1, LANE), F32)
        for c in range(s // ts):
            a = c * ts
            n_ext = ts + POOL_HALO
            gate_e = _rows_after(gate_ref, a, ts, POOL_HALO)
            dy_e = _rows_after(dy_ref, a, ts, POOL_HALO)
            silu_e, dsilu_e = _silu_and_grad(gate_e)
            dms_e = dy_e * silu_e
            dm_e = (dms_e * scale).astype(BF16)
            dpool_e = _dot(dm_e, w, NT)
            spread = _leading_sums(dpool_e / _window_count(a, n_ext, group), group)
            dv_ref[a:a + ts, :] = (spread[0:ts, :] - dpool_e[0:ts, :]).astype(BF16)
            pooled = _pooled(v_ref, a, ts, group).astype(BF16)
            mixed = _dot(pooled, w, NN)
            dgate_ref[a:a + ts, :] = (dy_e[0:ts, :] * mixed * scale * dsilu_e[0:ts, :]).astype(BF16)
            dsc = dsc + jnp.sum(dms_e[0:ts, :] * mixed, axis=0, keepdims=True)
            dw = dw + _dot(pooled, dm_e[0:ts, :], TN)
        dw_ref[...] = dw
        dsc_ref[...] = dsc

    col = lambda base: pl.BlockSpec((s, LANE), lambda g: (0, base + g))
    out_col = pl.BlockSpec((s, LANE), lambda g: (0, g))
    return pl.pallas_call(
        body, name=name, grid=(4,),
        in_specs=[col(COL_POOL_V), col(COL_POOL_G), out_col,
                  pl.BlockSpec((None, LANE, LANE), lambda g: (g, 0, 0)),
                  pl.BlockSpec((1, LANE), lambda g: (0, g))],
        out_specs=[out_col, out_col,
                   pl.BlockSpec((None, LANE, LANE), lambda g: (g, 0, 0)),
                   pl.BlockSpec((1, LANE), lambda g: (0, g))],
        out_shape=[jax.ShapeDtypeStruct((s, WIDTH), BF16), jax.ShapeDtypeStruct((s, WIDTH), BF16),
                   jax.ShapeDtypeStruct((4, LANE, LANE), F32), jax.ShapeDtypeStruct((1, WIDTH), F32)],
    )(u, u, dy, pool_w, pool_scale)


def _conv_taps(x_ref, gc_ref, start, n):
    z_ext = _rows_before(gc_ref, start, n, CONV_HALO) * _rows_before(x_ref, start, n, CONV_HALO)
    z0 = z_ext[CONV_HALO:, :]
    z1 = pltpu.roll(z_ext, 1, 0)[CONV_HALO:, :]
    z2 = pltpu.roll(z_ext, 2, 0)[CONV_HALO:, :]
    return z0, z1, z2


def _conv_fwd(u, conv_w, conv_b, name, ts=512):
    s = u.shape[0]

    def body(x_ref, gb_ref, gc_ref, g_ref, w_ref, b_ref, y_ref):
        w0, w1, w2 = w_ref[0:1, :], w_ref[1:2, :], w_ref[2:3, :]
        for c in range(s // ts):
            a = c * ts
            z0, z1, z2 = _conv_taps(x_ref, gc_ref, a, ts)
            y = w2 * z0 + w1 * z1 + w0 * z2 + b_ref[...]
            gate = g_ref[a:a + ts, :].astype(F32)
            y_ref[a:a + ts, :] = (gb_ref[a:a + ts, :].astype(F32) * y * (gate * _sigmoid(gate))).astype(BF16)

    col = lambda base: pl.BlockSpec((s, LANE), lambda j: (0, base + j))
    return pl.pallas_call(
        body, name=name, grid=(4,),
        in_specs=[col(COL_CONV_X), col(COL_CONV_GB), col(COL_CONV_GC), col(COL_CONV_G),
                  pl.BlockSpec((3, LANE), lambda j: (0, j)), pl.BlockSpec((1, LANE), lambda j: (0, j))],
        out_specs=pl.BlockSpec((s, LANE), lambda j: (0, j)),
        out_shape=jax.ShapeDtypeStruct((s, WIDTH), BF16),
    )(u, u, u, u, conv_w, conv_b)


def _conv_bwd(u, dy, conv_w, conv_b, name, ts=512):
    s = u.shape[0]

    def body(x_ref, gb_ref, gc_ref, g_ref, dy_ref, w_ref, b_ref,
             dx_ref, dgb_ref, dgc_ref, dg_ref, dw_ref, db_ref):
        w0, w1, w2 = w_ref[0:1, :], w_ref[1:2, :], w_ref[2:3, :]
        acc = [jnp.zeros((1, LANE), F32) for _ in range(4)]
        for c in range(s // ts):
            a = c * ts
            n_ext = ts + CONV_HALO
            gate_e = _rows_after(g_ref, a, ts, CONV_HALO)
            silu_e, dsilu_e = _silu_and_grad(gate_e)
            dy_e = _rows_after(dy_ref, a, ts, CONV_HALO)
            gb_e = _rows_after(gb_ref, a, ts, CONV_HALO)
            dyy_e = dy_e * silu_e * gb_e
            dz = (w2 * dyy_e + w1 * pltpu.roll(dyy_e, n_ext - 1, 0) + w0 * pltpu.roll(dyy_e, n_ext - 2, 0))[0:ts, :]
            z0, z1, z2 = _conv_taps(x_ref, gc_ref, a, ts)
            yb = w2 * z0 + w1 * z1 + w0 * z2 + b_ref[...]
            dyv = dy_e[0:ts, :]
            dyy = dyy_e[0:ts, :]
            dg_ref[a:a + ts, :] = (dyv * gb_e[0:ts, :] * yb * dsilu_e[0:ts, :]).astype(BF16)
            dgb_ref[a:a + ts, :] = (dyv * silu_e[0:ts, :] * yb).astype(BF16)
            dx_ref[a:a + ts, :] = (dz * gc_ref[a:a + ts, :].astype(F32)).astype(BF16)
            dgc_ref[a:a + ts, :] = (dz * x_ref[a:a + ts, :].astype(F32)).astype(BF16)
            for i, term in enumerate((dyy * z2, dyy * z1, dyy * z0, dyy)):
                acc[i] = acc[i] + jnp.sum(term, axis=0, keepdims=True)
        dw_ref[0:1, :] = acc[0]
        dw_ref[1:2, :] = acc[1]
        dw_ref[2:3, :] = acc[2]
        db_ref[...] = acc[3]

    col = lambda base: pl.BlockSpec((s, LANE), lambda j: (0, base + j))
    out_col = pl.BlockSpec((s, LANE), lambda j: (0, j))
    big = jax.ShapeDtypeStruct((s, WIDTH), BF16)
    return pl.pallas_call(
        body, name=name, grid=(4,),
        in_specs=[col(COL_CONV_X), col(COL_CONV_GB), col(COL_CONV_GC), col(COL_CONV_G), out_col,
                  pl.BlockSpec((3, LANE), lambda j: (0, j)), pl.BlockSpec((1, LANE), lambda j: (0, j))],
        out_specs=[out_col, out_col, out_col, out_col,
                   pl.BlockSpec((3, LANE), lambda j: (0, j)), pl.BlockSpec((1, LANE), lambda j: (0, j))],
        out_shape=[big, big, big, big,
                   jax.ShapeDtypeStruct((3, WIDTH), F32), jax.ShapeDtypeStruct((1, WIDTH), F32)],
    )(u, u, u, u, dy, conv_w, conv_b)


LOG2_E = 1.4426950408889634
LN_2 = 0.6931471805599453


def _sb_scores(q_h, k_blk, valid, later_mat, carry):
    z = _dot(q_h, k_blk, NT)
    neg_z = -z
    soft = jnp.log(1.0 + jnp.exp2(jnp.minimum(z, neg_z))) * LOG2_E
    log_keep = jnp.minimum(neg_z, 0.0) - soft
    log_beta = log_keep + z
    if valid is not None:
        log_keep = jnp.where(valid, log_keep, 0.0)
    later = _dot(log_keep.astype(BF16), later_mat, NN) + carry
    return log_keep, log_beta, later


def _masked(valid, x):
    return x if valid is None else jnp.where(valid, x, 0.0)


def _diagonal_masks(tq, tk):
    r = lax.broadcasted_iota(jnp.int32, (tq, tk), 0)
    cidx = lax.broadcasted_iota(jnp.int32, (tq, tk), 1)
    return [cidx + d * tk < r for d in range(tq // tk)]


def _triangle(tk, op):
    r = lax.broadcasted_iota(jnp.int32, (tk, tk), 0)
    cidx = lax.broadcasted_iota(jnp.int32, (tk, tk), 1)
    return op(r, cidx).astype(BF16)


def _split_refs(refs, n_in, n_out, n_scratch, rider):
    r_in = len(rider.inputs) if rider else 0
    r_out = len(rider.out_shape) if rider else 0
    a, b = n_in + r_in, n_in + r_in + n_out + r_out
    own = refs[:n_in] + refs[a:a + n_out] + refs[b:b + n_scratch]
    return own, (refs[n_in:a], refs[a + n_out:b], refs[b + n_scratch:])


def _rider_call_args(rider, n_in, n_out):
    if rider is None:
        return dict(in_specs=[], out_specs=[], out_shape=[], aliases={}, scratch=[], inputs=[])
    return dict(in_specs=[ANY] * len(rider.inputs), out_specs=[ANY] * len(rider.out_shape),
                out_shape=list(rider.out_shape), scratch=list(rider.scratch), inputs=list(rider.inputs),
                aliases={n_in + a: n_out + b for a, b in rider.aliases.items()})


def _ride(rider, phase, when, parts):
    fn = getattr(rider, phase) if rider else None
    if fn is not None:
        pl.when(when)(lambda: fn(*parts))


def _sb_fwd(u, name, t=512, tk=256, pairs=4, rider=None):
    s = u.shape[0]
    assert s // tk <= LANE and 4 % pairs == 0 and t % tk == 0
    scale = HEAD_DIM ** -0.5
    nh = 2 * pairs
    wide = pairs * LANE
    ratio = t // tk
    groups, nq = 4 // pairs, s // t

    def body(*refs):
        own, riding = _split_refs(refs, 4, 3, 4, rider)
        q_ref, k_ref, v_ref, g_ref, o_ref, y_ref, after_ref, kb_ref, vb_ref, acc_ref, carry_ref = own
        grp = pl.program_id(0)
        i = pl.program_id(1)
        _ride(rider, "start", jnp.logical_and(grp == 0, i == 0), riding)
        _ride(rider, "middle", jnp.logical_and(grp == groups - 1, i == (3 * nq) // 4), riding)

        @pl.when(i == 0)
        def _():
            kb_ref[...] = k_ref[...].astype(BF16)
            vb_ref[...] = v_ref[...].astype(BF16)

        lane = lax.broadcasted_iota(jnp.int32, (t, LANE), 1)
        first = lane < HEAD_DIM
        after_ref[...] = jnp.zeros_like(after_ref)
        qv = q_ref[...].astype(F32) * (scale * LOG2_E)
        q_heads = []
        for p in range(pairs):
            qp = qv[:, p * LANE:(p + 1) * LANE]
            q_heads += [jnp.where(first, qp, 0.0).astype(BF16), jnp.where(first, 0.0, qp).astype(BF16)]
        later_mat = _triangle(tk, lambda r, cidx: r > cidx)
        acc_ref[...] = jnp.zeros_like(acc_ref)
        carry_ref[...] = jnp.zeros_like(carry_ref)

        def block(kb, valid, lo=0):
            rows = pl.ds(pl.multiple_of(kb * tk, tk), tk)
            k_blk = kb_ref[rows, :]
            v_blk = vb_ref[rows, :]
            carries = [carry_ref[h, lo:, :] for h in range(nh)]
            afters = [after_ref[lo:, h * LANE:(h + 1) * LANE] for h in range(nh)]
            accs = [acc_ref[h, lo:, :] for h in range(nh)]
            outs = []
            for h in range(nh):
                cols = slice((h // 2) * LANE, (h // 2 + 1) * LANE)
                log_keep, log_beta, later = _sb_scores(q_heads[h][lo:], k_blk[:, cols], valid, later_mat, carries[h])
                a = _masked(valid, jnp.exp2(log_beta + later))
                outs.append((accs[h] + _dot(a.astype(BF16), v_blk[:, cols], NN),
                             carries[h] + jnp.sum(log_keep, axis=1, keepdims=True),
                             jnp.where(lane[lo:] == kb, carries[h], afters[h])))
            for h in range(nh):
                acc_ref[h, lo:, :] = outs[h][0]
                carry_ref[h, lo:, :] = outs[h][1]
                after_ref[lo:, h * LANE:(h + 1) * LANE] = outs[h][2]

        def step(j, _):
            block(ratio * i - 1 - j, None)
            return 0

        masks = _diagonal_masks(t, tk)
        for d in reversed(range(ratio)):
            block(ratio * i + d, masks[d][d * tk:], d * tk)
        lax.fori_loop(0, ratio * i, step, 0)
        for p in range(pairs):
            cols = slice(p * LANE, (p + 1) * LANE)
            o = jnp.where(first, acc_ref[2 * p], acc_ref[2 * p + 1])
            o_ref[:, cols] = o
            gate = g_ref[:, cols].astype(F32)
            y_ref[:, cols] = (o * gate * _sigmoid(gate)).astype(BF16)
        _ride(rider, "finish", jnp.logical_and(grp == groups - 1, i == nq - 1), riding)

    blk = lambda base: pl.BlockSpec((t, wide), lambda g, i: (i, base // pairs + g))
    full = lambda base: pl.BlockSpec((s, wide), lambda g, i: (0, base // pairs + g))
    out_blk = pl.BlockSpec((t, wide), lambda g, i: (i, g))
    extra = _rider_call_args(rider, 4, 3)
    outs = pl.pallas_call(
        body, name=name, grid=(groups, nq),
        in_specs=[blk(COL_SB_Q), full(COL_SB_K), full(COL_SB_V), blk(COL_SB_G)] + extra["in_specs"],
        out_specs=[out_blk, out_blk, pl.BlockSpec((t, nh * LANE), lambda g, i: (i, g))] + extra["out_specs"],
        out_shape=[jax.ShapeDtypeStruct((s, WIDTH), F32), jax.ShapeDtypeStruct((s, WIDTH), BF16),
                   jax.ShapeDtypeStruct((s, 8 * LANE), F32)] + extra["out_shape"],
        input_output_aliases=extra["aliases"],
        scratch_shapes=[pltpu.VMEM((s, wide), BF16), pltpu.VMEM((s, wide), BF16),
                        pltpu.VMEM((nh, t, LANE), F32), pltpu.VMEM((nh, t, 1), F32)] + extra["scratch"],
        compiler_params=pltpu.CompilerParams(dimension_semantics=("arbitrary", "arbitrary")),
    )(u, u, u, u, *extra["inputs"])
    return outs[:3], outs[3:]


def _sb_bwd(u, o, after, dy, name, t=512, tk=256, pairs=2, rider=None):
    s = u.shape[0]
    nq = s // t
    scale = HEAD_DIM ** -0.5
    nh = 2 * pairs
    wide = pairs * LANE
    ratio = t // tk
    groups = 4 // pairs

    def body(*refs):
        own, riding = _split_refs(refs, 7, 4, 6, rider)
        (q_ref, k_ref, v_ref, g_ref, o_ref, after_ref, dy_ref, dq_ref, dk_ref, dv_ref, dg_ref,
         kb_ref, vb_ref, dk_acc, dv_acc, dq_acc, carry_ref) = own
        grp = pl.program_id(0)
        i = pl.program_id(1)
        _ride(rider, "start", jnp.logical_and(grp == 0, i == 0), riding)

        @pl.when(i == 0)
        def _():
            kb_ref[...] = k_ref[...].astype(BF16)
            vb_ref[...] = v_ref[...].astype(BF16)
            dk_acc[...] = jnp.zeros_like(dk_acc)
            dv_acc[...] = jnp.zeros_like(dv_acc)

        lane = lax.broadcasted_iota(jnp.int32, (t, LANE), 1)
        first = lane < HEAD_DIM
        gate = g_ref[...].astype(F32)
        silu, dsilu = _silu_and_grad(gate)
        dyv = dy_ref[...]
        do = dyv * silu
        dg_ref[...] = (dyv * o_ref[...] * dsilu).astype(BF16)
        qv = q_ref[...].astype(F32) * (scale * LOG2_E)
        do_heads, q_heads = [], []
        for p in range(pairs):
            cols = slice(p * LANE, (p + 1) * LANE)
            do_heads += [jnp.where(first, do[:, cols], 0.0).astype(BF16), jnp.where(first, 0.0, do[:, cols]).astype(BF16)]
            q_heads += [jnp.where(first, qv[:, cols], 0.0).astype(BF16), jnp.where(first, 0.0, qv[:, cols]).astype(BF16)]
        later_mat = _triangle(tk, lambda r, cidx: r > cidx)
        before_mat = _triangle(tk, lambda r, cidx: r < cidx)
        dq_acc[...] = jnp.zeros_like(dq_acc)
        carry_ref[...] = jnp.zeros_like(carry_ref)

        def block(kb, valid, lo=0):
            rows = pl.ds(pl.multiple_of(kb * tk, tk), tk)
            k_blk = kb_ref[rows, :]
            v_blk = vb_ref[rows, :]
            carries = [carry_ref[h, lo:, :] for h in range(nh)]
            dq_old = [dq_acc[h, lo:, :] for h in range(nh)]
            dk_old = dk_acc[rows, :]
            dv_old = dv_acc[rows, :]
            outs = []
            for h in range(nh):
                cols = slice((h // 2) * LANE, (h // 2 + 1) * LANE)
                q_h, do_h = q_heads[h][lo:], do_heads[h][lo:]
                after = jnp.sum(jnp.where(lane[lo:] == kb, after_ref[lo:, h * LANE:(h + 1) * LANE], 0.0), axis=1,
                                keepdims=True)
                _, log_beta, later = _sb_scores(q_h, k_blk[:, cols], valid, later_mat, after)
                beta = jnp.exp2(log_beta)
                a = _masked(valid, jnp.exp2(log_beta + later))
                da = _dot(do_h, v_blk[:, cols], NT)
                gterm = a * da
                before = _dot(gterm.astype(BF16), before_mat, NN) + carries[h]
                dz_b = _masked(valid, gterm * (1.0 - beta) - beta * before).astype(BF16)
                outs.append((dq_old[h] + _dot(dz_b, k_blk[:, cols], NN), _dot(dz_b, q_h, TN),
                             _dot(a.astype(BF16), do_h, TN),
                             carries[h] + jnp.sum(gterm, axis=1, keepdims=True)))
            for h in range(nh):
                dq_acc[h, lo:, :] = outs[h][0]
                carry_ref[h, lo:, :] = outs[h][3]
            dk_new = [outs[2 * p][1] + outs[2 * p + 1][1] for p in range(pairs)]
            dv_new = [outs[2 * p][2] + outs[2 * p + 1][2] for p in range(pairs)]
            dk_acc[rows, :] = dk_old + (dk_new[0] if pairs == 1 else jnp.concatenate(dk_new, axis=1))
            dv_acc[rows, :] = dv_old + (dv_new[0] if pairs == 1 else jnp.concatenate(dv_new, axis=1))

        def step(kb, _):
            block(kb, None)
            return 0

        lax.fori_loop(0, ratio * i, step, 0)
        masks = _diagonal_masks(t, tk)
        for d in range(ratio):
            block(ratio * i + d, masks[d][d * tk:], d * tk)
        for p in range(pairs):
            dq_ref[:, p * LANE:(p + 1) * LANE] = (jnp.where(first, dq_acc[2 * p], dq_acc[2 * p + 1]) * scale).astype(BF16)

        @pl.when(i == nq - 1)
        def _():
            dk_ref[...] = (dk_acc[...] * LN_2).astype(BF16)
            dv_ref[...] = dv_acc[...].astype(BF16)

        _ride(rider, "finish", jnp.logical_and(grp == groups - 1, i == nq - 1), riding)

    blk = lambda base: pl.BlockSpec((t, wide), lambda g, i: (i, base // pairs + g))
    full = lambda base: pl.BlockSpec((s, wide), lambda g, i: (0, base // pairs + g))
    out_blk = pl.BlockSpec((t, wide), lambda g, i: (i, g))
    out_full = pl.BlockSpec((s, wide), lambda g, i: (0, g))
    big = jax.ShapeDtypeStruct((s, WIDTH), BF16)
    extra = _rider_call_args(rider, 7, 4)
    outs = pl.pallas_call(
        body, name=name, grid=(groups, nq),
        in_specs=[blk(COL_SB_Q), full(COL_SB_K), full(COL_SB_V), blk(COL_SB_G), out_blk,
                  pl.BlockSpec((t, nh * LANE), lambda g, i: (i, g)), out_blk] + extra["in_specs"],
        out_specs=[out_blk, out_full, out_full, out_blk] + extra["out_specs"],
        out_shape=[big, big, big, big] + extra["out_shape"],
        input_output_aliases=extra["aliases"],
        scratch_shapes=[pltpu.VMEM((s, wide), BF16), pltpu.VMEM((s, wide), BF16),
                        pltpu.VMEM((s, wide), F32), pltpu.VMEM((s, wide), F32),
                        pltpu.VMEM((nh, t, LANE), F32), pltpu.VMEM((nh, t, 1), F32)] + extra["scratch"],
        compiler_params=pltpu.CompilerParams(dimension_semantics=("arbitrary", "arbitrary")),
    )(u, u, u, u, o, after, dy, *extra["inputs"])
    return outs[:4], outs[4:]


def _gate_fwd(u, ys, w_branch, name, ts=256):
    s = u.shape[0]

    def body(m0, m1, m2, y0, y1, y2, w_ref, p0, p1, p2, out_ref):
        tot = None
        for n, (m_ref, y_ref, p_ref) in enumerate(((m0, y0, p0), (m1, y1, p1), (m2, y2, p2))):
            proj = _dot(y_ref[...], w_ref[n], NN)
            p_ref[...] = proj.astype(BF16)
            term = _sigmoid(m_ref[...].astype(F32)) * proj
            tot = term if tot is None else tot + term
        out_ref[...] = tot.astype(BF16)

    mspec = lambda n: pl.BlockSpec((ts, D_MODEL), lambda i: (i, COL_MERGE_1024 + n))
    row = pl.BlockSpec((ts, D_MODEL), lambda i: (i, 0))
    yspec = pl.BlockSpec((ts, WIDTH), lambda i: (i, 0))
    big = jax.ShapeDtypeStruct((s, D_MODEL), BF16)
    outs = pl.pallas_call(
        body, name=name, grid=(s // ts,),
        in_specs=[mspec(0), mspec(1), mspec(2), yspec, yspec, yspec,
                  pl.BlockSpec(w_branch.shape, lambda i: (0, 0, 0))],
        out_specs=[row] * 4, out_shape=[big] * 4,
    )(u, u, u, *ys, w_branch)
    return outs[:3], outs[3]


def _gate_bwd(u, projs, dmerged, w_branch, name, ts=256, rider=None):
    s = u.shape[0]
    steps = s // ts

    def body(*refs):
        own, riding = _split_refs(refs, 8, 9, 0, rider)
        m0, m1, m2, p0, p1, p2, dm_ref, w_ref, dp0, dp1, dp2, dl0, dl1, dl2, dy0, dy1, dy2 = own
        _ride(rider, "start", pl.program_id(0) == 0, riding)
        dm = dm_ref[...].astype(F32)
        for n, (m_ref, p_ref, dp_ref, dl_ref, dy_ref) in enumerate(((m0, p0, dp0, dl0, dy0), (m1, p1, dp1, dl1, dy1),
                                                                    (m2, p2, dp2, dl2, dy2))):
            gate = _sigmoid(m_ref[...].astype(F32))
            dp = (dm * gate).astype(BF16)
            dp_ref[...] = dp
            dl_ref[...] = (dm * p_ref[...].astype(F32) * gate * (1.0 - gate)).astype(BF16)
            dy_ref[...] = _dot(dp, w_ref[n], NT)
        _ride(rider, "finish", pl.program_id(0) == steps - 1, riding)

    mspec = lambda n: pl.BlockSpec((ts, D_MODEL), lambda i: (i, COL_MERGE_1024 + n))
    row = pl.BlockSpec((ts, D_MODEL), lambda i: (i, 0))
    yspec = pl.BlockSpec((ts, WIDTH), lambda i: (i, 0))
    big = jax.ShapeDtypeStruct((s, D_MODEL), BF16)
    extra = _rider_call_args(rider, 8, 9)
    outs = pl.pallas_call(
        body, name=name, grid=(steps,),
        in_specs=[mspec(0), mspec(1), mspec(2), row, row, row, row,
                  pl.BlockSpec(w_branch.shape, lambda i: (0, 0, 0))] + extra["in_specs"],
        out_specs=[row] * 6 + [yspec] * 3 + extra["out_specs"],
        out_shape=[big] * 6 + [jax.ShapeDtypeStruct((s, WIDTH), F32)] * 3 + extra["out_shape"],
        input_output_aliases=extra["aliases"], scratch_shapes=extra["scratch"],
        compiler_params=pltpu.CompilerParams(dimension_semantics=("arbitrary",)),
    )(u, u, u, *projs, dmerged, w_branch, *extra["inputs"])
    return outs[:3], outs[3:6], outs[6:9], outs[9:]


def _as_rows(a):
    return a.reshape(-1, a.shape[-1])


def _row_tile(rows, cols, bytes_per_row_elem=4, cap=1 << 20):
    tr = rows
    while tr * cols * bytes_per_row_elem > cap and tr % 2 == 0 and (tr // 2) % 16 == 0:
        tr //= 2
    return tr


def _cast_bf16(a, name):
    a2 = _as_rows(a)
    rows, cols = a2.shape
    tr = _row_tile(rows, cols)

    def body(a_ref, o_ref):
        o_ref[...] = a_ref[...].astype(BF16)

    spec = pl.BlockSpec((tr, cols), lambda i: (i, 0))
    out = pl.pallas_call(body, name=name, grid=(rows // tr,), in_specs=[spec], out_specs=spec,
                         out_shape=jax.ShapeDtypeStruct((rows, cols), BF16))(a2)
    return out.reshape(a.shape)


def _adamw(w, g, m, v, name):
    shape = w.shape
    w2, g2, m2, v2 = (_as_rows(a) for a in (w, g, m, v))
    rows, cols = w2.shape
    tr = _row_tile(rows, cols)
    c1 = 1.0 - ADAM_B1 ** ADAM_STEP
    c2 = 1.0 - ADAM_B2 ** ADAM_STEP

    def body(w_ref, g_ref, m_ref, v_ref, d_ref, nm_ref, nv_ref):
        gv = g_ref[...]
        nm = ADAM_B1 * m_ref[...] + (1.0 - ADAM_B1) * gv
        nv = ADAM_B2 * v_ref[...] + (1.0 - ADAM_B2) * (gv * gv)
        nm_ref[...] = nm
        nv_ref[...] = nv
        d_ref[...] = -ADAM_LR * ((nm / c1) / (jnp.sqrt(nv / c2) + ADAM_EPS) + ADAM_WD * w_ref[...])

    spec = pl.BlockSpec((tr, cols), lambda i: (i, 0))
    sds = jax.ShapeDtypeStruct((rows, cols), F32)
    outs = pl.pallas_call(body, name=name, grid=(rows // tr,), in_specs=[spec] * 4, out_specs=[spec] * 3,
                          out_shape=[sds] * 3)(w2, g2, m2, v2)
    return tuple(o.reshape(shape) for o in outs)


def _sum_slots(a, out_dtype, name):
    n = a.shape[0]
    a3 = a.reshape(n, -1, a.shape[-1])
    _, rows, cols = a3.shape
    tr = _row_tile(rows, cols * n)

    def body(a_ref, o_ref):
        tot = a_ref[0].astype(F32)
        for k in range(1, n):
            tot = tot + a_ref[k].astype(F32)
        o_ref[...] = tot.astype(out_dtype)

    out = pl.pallas_call(
        body, name=name, grid=(rows // tr,),
        in_specs=[pl.BlockSpec((n, tr, cols), lambda i: (0, i, 0))],
        out_specs=pl.BlockSpec((tr, cols), lambda i: (i, 0)),
        out_shape=jax.ShapeDtypeStruct((rows, cols), out_dtype))(a3)
    return out.reshape(a.shape[1:])


def _chip_sum(own, recv, axis, core, name):
    half = recv.shape
    nd = len(half)
    last = nd - 1
    if axis == last:
        tl, nt = half[last], 1
    else:
        tl = min(half[last], 2048)
        nt = half[last] // tl
    block = half[:last] + (tl,)

    def own_index(i, core_ref):
        idx = [0] * nd
        idx[last] = i
        if axis == last:
            idx[last] = core_ref[0]
        else:
            idx[axis] = core_ref[0]
        return tuple(idx)

    def recv_index(i, core_ref):
        idx = [0] * nd
        idx[last] = i
        return tuple(idx)

    def body(core_ref, own_ref, recv_ref, o_ref):
        o_ref[...] = (own_ref[...] + recv_ref[...]).astype(BF16)

    return pl.pallas_call(
        body, name=name,
        grid_spec=pltpu.PrefetchScalarGridSpec(
            num_scalar_prefetch=1, grid=(nt,),
            in_specs=[pl.BlockSpec(block, own_index), pl.BlockSpec(block, recv_index)],
            out_specs=pl.BlockSpec(block, recv_index)),
        out_shape=jax.ShapeDtypeStruct(half, BF16),
    )(core, own, recv)


def _mesh_position():
    return lax.axis_index("x"), lax.axis_index("y"), lax.axis_index("c")


def _other_chips(x, y):
    return [(1 - x, y), (x, 1 - y), (1 - x, 1 - y)]


ALL_FLIPS = [(0, 0, 1), (1, 0, 0), (0, 1, 0), (1, 1, 0), (1, 0, 1), (0, 1, 1), (1, 1, 1)]


def _half(ref, axis, which, size):
    idx = [slice(None)] * len(ref.shape)
    idx[axis] = pl.ds(which * size, size)
    return ref.at[tuple(idx)]


def _sub(ref, picks):
    idx = [slice(None)] * len(ref.shape)
    for axis, start, size in picks:
        idx[axis] = pl.ds(start, size)
    return ref.at[tuple(idx)]


def _remote(src, dst, sems_send, sems_recv, k, to):
    return pltpu.make_async_remote_copy(src_ref=src, dst_ref=dst, send_sem=sems_send.at[k], recv_sem=sems_recv.at[k],
                                        device_id=to, device_id_type=MESH)


def _cast_shard(w, layer, shard_axis, pos, name, tr=512):
    shape = w.shape[1:]
    nd = len(shape)
    assert shard_axis in (nd - 1, nd - 2)
    rows, cols = shape[-2:]
    tr = min(tr, rows)
    nt = rows // tr
    lead = shape[:-2]
    full = list(shape)
    full[shard_axis] *= N_CHIPS
    block = (1,) * len(lead) + (tr, cols)

    def in_index(*args):
        return (layer, *args[:-1], 0)

    def out_index(*args):
        *g, pos_ref = args
        if shard_axis == nd - 1:
            return (*g, pos_ref[1])
        return (*g[:-1], pos_ref[1] * nt + g[-1], 0)

    def body(pos_ref, a_ref, o_ref):
        o_ref[...] = a_ref[...].astype(BF16)

    return pl.pallas_call(
        body, name=name,
        grid_spec=pltpu.PrefetchScalarGridSpec(
            num_scalar_prefetch=1, grid=lead + (nt,),
            in_specs=[pl.BlockSpec((None,) + block, in_index)], out_specs=pl.BlockSpec(block, out_index)),
        out_shape=jax.ShapeDtypeStruct(tuple(full), BF16),
    )(pos, w)


class _Rider:
    def __init__(self, inputs, out_shape, aliases, scratch, start, middle, finish):
        self.inputs, self.out_shape, self.aliases, self.scratch = inputs, out_shape, aliases, scratch
        self.start, self.middle, self.finish = start, middle, finish


def _weight_gather_rider(fulls, layout):
    n = len(fulls)

    def copies(outs, sems):
        send_sems, recv_sems = sems
        x, y, c = _mesh_position()
        chips = _other_chips(x, y)
        sibling = (x, y, 1 - c)
        mine = 2 * x + y

        def place(t, chip, core):
            sh_axis, sh_size, half_axis, half_size = layout[t]
            return _sub(outs[t], [(sh_axis, chip * sh_size, sh_size), (half_axis, core * half_size, half_size)])

        direct, arrive, forward, arrive_fwd = [], [], [], []
        for t in range(n):
            for k, (px, py) in enumerate(chips):
                theirs = 2 * px + py
                direct.append(_remote(place(t, mine, c), place(t, mine, c), send_sems, recv_sems, 6 * t + k, (px, py, c)))
                arrive.append(_remote(place(t, theirs, c), place(t, theirs, c), send_sems, recv_sems, 6 * t + k, (px, py, c)))
                forward.append(_remote(place(t, theirs, c), place(t, theirs, c), send_sems, recv_sems, 6 * t + 3 + k, sibling))
                arrive_fwd.append(_remote(place(t, theirs, 1 - c), place(t, theirs, 1 - c), send_sems, recv_sems,
                                          6 * t + 3 + k, sibling))
        return direct, arrive, forward, arrive_fwd

    def start(ins, outs, sems):
        for cp in copies(outs, sems)[0]:
            cp.start()

    def middle(ins, outs, sems):
        _, arrive, forward, _ = copies(outs, sems)
        for a, f in zip(arrive, forward):
            a.wait_recv()
            f.start()

    def finish(ins, outs, sems):
        direct, _, forward, arrive_fwd = copies(outs, sems)
        for cp in arrive_fwd:
            cp.wait_recv()
        for cp in direct + forward:
            cp.wait_send()

    return _Rider(list(fulls), [jax.ShapeDtypeStruct(a.shape, a.dtype) for a in fulls], {k: k for k in range(n)},
                  [pltpu.SemaphoreType.DMA((6 * n,)), pltpu.SemaphoreType.DMA((6 * n,))], start, middle, finish)


WEIGHT_LAYOUT = [(1, 2048, 0, 512), (2, 256, 1, 256), (0, 256, 1, 512)]


def _gather_weights(fulls, conv_w):
    rider = _weight_gather_rider(fulls, WEIGHT_LAYOUT)
    n = len(fulls)

    def body(*refs):
        cw, outs, cw_f = refs[n], refs[n + 1:2 * n + 1], refs[2 * n + 1]
        sems, (cw_send, cw_recv, local_sem) = refs[2 * n + 2:2 * n + 4], refs[2 * n + 4:]
        x, y, c = _mesh_position()
        chips = _other_chips(x, y)
        mine = 2 * x + y
        local = pltpu.make_async_copy(cw, cw_f.at[mine], local_sem.at[0])
        local.start()
        rider.start(None, outs, sems)
        small = [_remote(cw, cw_f.at[mine], cw_send, cw_recv, k, (*chip, c)) for k, chip in enumerate(chips)]
        for cp in small:
            cp.start()
        rider.middle(None, outs, sems)
        rider.finish(None, outs, sems)
        for k, (px, py) in enumerate(chips):
            _remote(cw, cw_f.at[2 * px + py], cw_send, cw_recv, k, (px, py, c)).wait_recv()
        for cp in small:
            cp.wait_send()
        local.wait()

    outs = pl.pallas_call(
        body, name="gather_weights",
        in_specs=[ANY] * (n + 1), out_specs=[ANY] * (n + 1),
        out_shape=rider.out_shape + [jax.ShapeDtypeStruct((N_CHIPS,) + conv_w.shape, F32)],
        input_output_aliases=rider.aliases,
        scratch_shapes=rider.scratch + [pltpu.SemaphoreType.DMA((3,)), pltpu.SemaphoreType.DMA((3,)),
                                        pltpu.SemaphoreType.DMA((1,))],
    )(*fulls, conv_w)
    return outs[:n], outs[n]


def _swap_rider(items):
    n = len(items)
    halves = []
    for a, axis in items:
        shp = list(a.shape)
        shp[axis] //= 2
        halves.append(tuple(shp))

    def copies(ins, outs, sems):
        x, y, c = _mesh_position()
        return [_remote(_half(ins[k], items[k][1], 1 - c, halves[k][items[k][1]]), outs[k], sems[0], sems[1], k,
                        (x, y, 1 - c)) for k in range(n)]

    def start(ins, outs, sems):
        for cp in copies(ins, outs, sems):
            cp.start()

    def finish(ins, outs, sems):
        for cp in copies(ins, outs, sems):
            cp.wait()

    return _Rider([a for a, _ in items], [jax.ShapeDtypeStruct(h, a.dtype) for h, (a, _) in zip(halves, items)], {},
                  [pltpu.SemaphoreType.DMA((n,)), pltpu.SemaphoreType.DMA((n,))], start, None, finish)


def _swap_halves(items, name):
    rider = _swap_rider(items)
    n = len(items)

    def body(*refs):
        parts = (refs[:n], refs[n:2 * n], refs[2 * n:])
        rider.start(*parts)
        rider.finish(*parts)

    return pl.pallas_call(
        body, name=name, in_specs=[ANY] * n, out_specs=[ANY] * n, out_shape=rider.out_shape,
        scratch_shapes=rider.scratch,
    )(*rider.inputs)


def _grad_exchange_rider(items):
    n = len(items)
    slices = []
    for a, axis in items:
        shp = list(a.shape)
        shp[axis] //= N_CHIPS
        slices.append(tuple(shp))

    def copies(ins, outs, sems):
        send_sems, recv_sems = sems
        x, y, c = _mesh_position()
        made = []
        for k in range(n):
            axis = items[k][1]
            for r, (px, py) in enumerate(_other_chips(x, y)):
                made.append(_remote(_half(ins[k], axis, 2 * px + py, slices[k][axis]), outs[k].at[r],
                                    send_sems, recv_sems, 3 * k + r, (px, py, c)))
        return made

    def start(ins, outs, sems):
        for cp in copies(ins, outs, sems):
            cp.start()

    def finish(ins, outs, sems):
        for cp in copies(ins, outs, sems):
            cp.wait()

    return _Rider([a for a, _ in items], [jax.ShapeDtypeStruct((N_CHIPS - 1,) + s, BF16) for s in slices], {},
                  [pltpu.SemaphoreType.DMA((3 * n,)), pltpu.SemaphoreType.DMA((3 * n,))], start, None, finish)


def _small_gather_rider(small):
    def copies(ins, outs, sems):
        x, y, c = _mesh_position()
        me = 4 * x + 2 * y + c
        local = pltpu.make_async_copy(ins[0], outs[0].at[me], sems[2].at[0])
        remote = [_remote(ins[0], outs[0].at[me], sems[0], sems[1], r, (x ^ fx, y ^ fy, c ^ fc))
                  for r, (fx, fy, fc) in enumerate(ALL_FLIPS)]
        return local, remote

    def start(ins, outs, sems):
        local, remote = copies(ins, outs, sems)
        local.start()
        for cp in remote:
            cp.start()

    def finish(ins, outs, sems):
        local, remote = copies(ins, outs, sems)
        for cp in remote:
            cp.wait()
        local.wait()

    n = len(ALL_FLIPS)
    return _Rider([small], [jax.ShapeDtypeStruct((2 * N_CHIPS,) + small.shape, F32)], {},
                  [pltpu.SemaphoreType.DMA((n,)), pltpu.SemaphoreType.DMA((n,)), pltpu.SemaphoreType.DMA((1,))],
                  start, None, finish)


def _sum_chips(recv, own, shard_axis, split_axis, pos, dest, layer, name, tr=128):
    sl = recv.shape[1:]
    nd = len(sl)
    tiled = nd == 2 and sl[0] > tr
    nt = sl[0] // tr if tiled else 1
    block = ((tr,) + sl[1:]) if tiled else sl
    shard = list(sl)
    shard[split_axis] *= 2

    def recv_index(i, pos_ref):
        return (0, i) + (0,) * (nd - 1) if tiled else (0,) * (nd + 1)

    def own_index(i, pos_ref):
        idx = [0] * nd
        idx[shard_axis] = pos_ref[1]
        if tiled:
            idx[0] = pos_ref[1] * nt + i if shard_axis == 0 else i
        return tuple(idx)

    def out_index(i, pos_ref):
        idx = [0] * nd
        idx[split_axis] = pos_ref[0]
        if tiled:
            idx[0] = pos_ref[0] * nt + i if split_axis == 0 else i
        return (layer, *idx)

    def body(pos_ref, recv_ref, own_ref, *rest):
        o_ref = rest[-1]
        tot = own_ref[...].astype(F32)
        for k in range(N_CHIPS - 1):
            tot = tot + recv_ref[k].astype(F32)
        o_ref[0] = tot

    in_specs = [pl.BlockSpec((N_CHIPS - 1,) + block, recv_index), pl.BlockSpec(block, own_index)]
    args = [pos, recv, own]
    aliases = {}
    if dest is not None:
        in_specs.append(ANY)
        args.append(dest)
        aliases = {3: 0}
    return pl.pallas_call(
        body, name=name,
        grid_spec=pltpu.PrefetchScalarGridSpec(
            num_scalar_prefetch=1, grid=(nt,), in_specs=in_specs,
            out_specs=pl.BlockSpec((1,) + block, out_index)),
        out_shape=jax.ShapeDtypeStruct((DEPTH,) + tuple(shard), F32),
        input_output_aliases=aliases,
    )(*args)


def _share_halves(bufs, late_small, name):
    n = len(bufs)
    small = _small_gather_rider(late_small)

    def body(*refs):
        outs, (send_sems, recv_sems) = refs[n + 1:2 * n + 1], refs[2 * n + 2:2 * n + 4]
        small_parts = ([refs[n]], [refs[2 * n + 1]], refs[2 * n + 4:])
        x, y, c = _mesh_position()
        small.start(*small_parts)
        copies = []
        for k, (a, axis) in enumerate(bufs):
            size = a.shape[1 + axis] // 2
            mine = _half(outs[k], 1 + axis, c, size)
            copies.append(_remote(mine, mine, send_sems, recv_sems, k, (x, y, 1 - c)))
        for cp in copies:
            cp.start()
        for cp in copies:
            cp.wait()
        small.finish(*small_parts)

    outs = pl.pallas_call(
        body, name=name, in_specs=[ANY] * (n + 1), out_specs=[ANY] * (n + 1),
        out_shape=[jax.ShapeDtypeStruct(a.shape, F32) for a, _ in bufs] + small.out_shape,
        input_output_aliases={k: k for k in range(n)},
        scratch_shapes=[pltpu.SemaphoreType.DMA((n,)), pltpu.SemaphoreType.DMA((n,))] + small.scratch,
    )(*[a for a, _ in bufs], late_small)
    return outs[:n], outs[n]


def _layer_fwd(x, p, l, rider=None):
    tag = f"l{l}_"
    h = _rms_fwd(x, p["pre_g"], tag + "pre_norm")
    u = _matmul(h, p["w_in"], "nn", BF16, tag + "in_proj")
    y_pool = _pool_fwd(u, p["pool_w"], p["pool_scale"], tag + "pool")
    y_conv = _conv_fwd(u, p["conv_w"], p["conv_b"], tag + "conv")
    (o_sb, y_sb, sb_after), carried = _sb_fwd(u, tag + "stickbreak", rider=rider)
    ys = [y_pool, y_conv, y_sb]
    projs, merged = _gate_fwd(u, ys, p["w_branch"], tag + "merge")
    out = _matmul(merged, p["w_out"], "nn", F32, tag + "out_proj")
    saved = dict(x=x, h=h, u=u, ys=ys, o_sb=o_sb, sb_after=sb_after, projs=projs, merged=merged, out=out)
    return out, saved, carried


def _layer_bwd(dy, p, saved, l, merge_rider=None, early=None, before_dw=None, late=None):
    tag = f"l{l}_bwd_"
    u = saved["u"]
    d_out, g_post = _rms_bwd(saved["out"], p["post_g"], dy, None, BF16, tag + "post_norm")
    d_merged = _matmul(d_out, p["w_out"], "nt", BF16, tag + "out_proj_dx")
    g_w_out = _matmul(saved["merged"], d_out, "tn", F32, tag + "out_proj_dw", tk=2048)
    d_projs, d_logits, d_ys, carried_merge = _gate_bwd(u, saved["projs"], d_merged, p["w_branch"], tag + "merge",
                                                       rider=merge_rider)
    g_w_branch = jnp.stack([_matmul(saved["ys"][n], d_projs[n], "tn", F32, tag + f"branch_dw{n}", tk=2048)
                            for n in range(3)])
    rider = early(g_w_branch, g_w_out, carried_merge) if early else None
    d_pv, d_pg, g_pool_w, g_pool_scale = _pool_bwd(u, d_ys[0], p["pool_w"], p["pool_scale"], tag + "pool")
    d_cx, d_cgb, d_cgc, d_cg, g_conv_w, g_conv_b = _conv_bwd(u, d_ys[1], p["conv_w"], p["conv_b"], tag + "conv")
    (d_q, d_k, d_v, d_sg), carried_attn = _sb_bwd(u, saved["o_sb"], saved["sb_after"], d_ys[2], tag + "stickbreak",
                                                  rider=rider)
    du = jnp.concatenate([d_pv, d_pg, d_cx, d_cgb, d_cgc, d_cg, d_q, d_k, d_v, d_sg] + list(d_logits), axis=1)
    grads = dict(w_branch=g_w_branch, w_out=g_w_out, post_g=g_post, pool_w=g_pool_w, pool_scale=g_pool_scale,
                 conv_w=g_conv_w, conv_b=g_conv_b)
    rider = before_dw(grads) if before_dw else None
    g_w_in = _matmul(saved["h"], du, "tn", F32, tag + "in_proj_dw", tk=2048, rider=rider)
    g_w_in, carried_dw = g_w_in if rider else (g_w_in, [])
    rider = late(g_w_in) if late else None
    dh = _matmul(du, p["w_in"], "nt", BF16, tag + "in_proj_dx", tk=2048, rider=rider)
    dh, carried_dx = dh if rider else (dh, [])
    dx, g_pre = _rms_bwd(saved["x"], p["pre_g"], dh, dy, F32, tag + "pre_norm")
    grads.update(w_in=g_w_in, pre_g=g_pre)
    return dx, grads, carried_attn, carried_dw, carried_dx


SMALL_ORDER = ["pre_g", "pool_w", "pool_scale", "conv_w", "conv_b", "post_g"]


def _pack_small(per_layer):
    parts, spans, at = [], {}, 0
    for name in SMALL_ORDER:
        a = jnp.stack([per_layer[l][name] for l in range(DEPTH)]).reshape(-1, LANE)
        parts.append(a)
        spans[name] = (at, a.shape[0])
        at += a.shape[0]
    return jnp.concatenate(parts, axis=0), spans


def kernel(x, pre_norm_g, w_in, pool_w, pool_scale, conv_w, conv_b, w_branch, w_out, post_norm_g, loss_target, m_pre_norm_g, m_w_in, m_pool_w, m_pool_scale, m_conv_w, m_conv_b, m_w_branch, m_w_out, m_post_norm_g, v_pre_norm_g, v_w_in, v_pool_w, v_pool_scale, v_conv_w, v_conv_b, v_w_branch, v_w_out, v_post_norm_g):
    mx, my, mc = _mesh_position()
    chip = 2 * mx + my
    core = mc.astype(jnp.int32).reshape(1)
    pos = jnp.stack([mc, chip]).astype(jnp.int32)

    names = ["w_in", "w_branch", "w_out"]
    given = dict(w_in=w_in, w_branch=w_branch, w_out=w_out)
    in_place = [[_cast_shard(given[n], l, WEIGHT_LAYOUT[i][0], pos, f"cast_{n}{l}") for i, n in enumerate(names)]
                for l in range(DEPTH)]
    gathered, conv_w_by_chip = _gather_weights(in_place[0], conv_w)
    conv_w_f = conv_w_by_chip.transpose(1, 2, 0, 3).reshape(DEPTH, 3, WIDTH)
    pool_w_b = _cast_bf16(pool_w, "cast_pool_w")

    def layer_params(l, big):
        return dict(pre_g=pre_norm_g[l:l + 1], post_g=post_norm_g[l:l + 1], w_in=big[0], w_branch=big[1],
                    w_out=big[2], pool_w=pool_w_b[l], pool_scale=pool_scale[l:l + 1], conv_w=conv_w_f[l],
                    conv_b=conv_b[l:l + 1])

    act = x[0]
    params, saved = [], []
    for l in range(DEPTH):
        params.append(layer_params(l, gathered))
        rider = _weight_gather_rider(in_place[l + 1], WEIGHT_LAYOUT) if l + 1 < DEPTH else None
        out, sv, gathered = _layer_fwd(act, params[l], l, rider)
        saved.append(sv)
        if l < DEPTH - 1:
            act = _resid_out(act, out, params[l]["post_g"], None, f"l{l}_resid")
    dy, loss_part = _resid_out(act, saved[-1]["out"], params[-1]["post_g"], loss_target[0], "loss_head")

    split_axis = dict(w_in=0, w_branch=1, w_out=1)
    shard_axis = dict(w_in=1, w_branch=2, w_out=0)
    grads = [None] * DEPTH
    chip_sums = [dict() for _ in range(DEPTH)]
    by_chip = [dict() for _ in range(DEPTH)]

    def reduce_in_chip(l, which, g):
        items = [(g[n], split_axis[n]) for n in which]
        from_sibling = _swap_halves(items, f"swap_grad_halves{l}_{which[0]}")
        for n, (a, axis), r in zip(which, items, from_sibling):
            chip_sums[l][n] = _chip_sum(a, r, axis, core, f"chip_sum{l}_{n}")

    def exchange_rider(keys):
        return _grad_exchange_rider([(chip_sums[l][n], shard_axis[n]) for l, n in keys])

    waiting = []
    for l in reversed(range(DEPTH)):
        sent_early, sent_late = list(waiting) + [(l, "w_branch"), (l, "w_out")], [(l, "w_in")]
        waiting_items = [(grads[ll][n], split_axis[n]) for ll, n in waiting]

        def early(g_w_branch, g_w_out, from_sibling, l=l, keys=sent_early, above=tuple(waiting), items=waiting_items):
            for (ll, n), (a, axis), r in zip(above, items, from_sibling):
                chip_sums[ll][n] = _chip_sum(a, r, axis, core, f"chip_sum{ll}_{n}")
            reduce_in_chip(l, ["w_branch", "w_out"], dict(w_branch=g_w_branch, w_out=g_w_out))
            return exchange_rider(keys)

        def late(g_w_in, l=l, keys=sent_late):
            reduce_in_chip(l, ["w_in"], dict(w_in=g_w_in))
            return exchange_rider(keys)

        def before_dw(partial, l=l):
            layers = [dict(partial, pre_g=jnp.zeros_like(pre_norm_g[:1])) if ll == l else grads[ll]
                      for ll in range(DEPTH)]
            return _small_gather_rider(_pack_small(layers)[0])

        if l == DEPTH - 1:
            dy, grads[l], _, _, _ = _layer_bwd(dy, params[l], saved[l], l)
            waiting = [(l, n) for n in names]
        else:
            dy, grads[l], got_early, got_dw, got_late = _layer_bwd(
                dy, params[l], saved[l], l, _swap_rider(waiting_items), early, before_dw if l == 0 else None, late)
            for (ll, n), r in zip(sent_early + sent_late, list(got_early) + list(got_late)):
                by_chip[ll][n] = r
            if l == 0:
                small_all = got_dw[0]
            waiting = []
    assert not waiting
    grad_x = dy[None]
    _, spans = _pack_small(grads)
    late_small = jnp.concatenate([grads[0]["pre_g"].reshape(-1, LANE), jnp.broadcast_to(loss_part, (8, LANE))])
    bufs = []
    for n in names:
        dest = None
        for l in range(DEPTH):
            dest = _sum_chips(by_chip[l][n], chip_sums[l][n], shard_axis[n], split_axis[n], pos, dest, l,
                              f"sum_chips{l}_{n}")
        bufs.append((dest, split_axis[n]))
    (g_w_in, g_w_branch, g_w_out), late_all = _share_halves(bufs, late_small, "share_grad_halves")

    late_sum = _sum_slots(late_all, F32, "sum_late_small")
    n_gain = late_small.shape[0] - 8
    loss = late_sum[n_gain, 0]
    small_sum = _sum_slots(small_all, F32, "sum_small")
    at, _ = spans["pre_g"]
    small_sum = jnp.concatenate([small_sum[:at], late_sum[:n_gain], small_sum[at + n_gain:]])
    small = {}
    for name, like in (("pre_g", pre_norm_g), ("pool_w", pool_w), ("pool_scale", pool_scale), ("conv_b", conv_b),
                       ("post_g", post_norm_g)):
        at, n = spans[name]
        small[name] = small_sum[at:at + n].reshape(like.shape)
    at, n = spans["conv_w"]
    g_conv_w_full = small_sum[at:at + n].reshape(DEPTH, 3, WIDTH)
    g_conv_w = lax.dynamic_slice_in_dim(g_conv_w_full, chip * conv_w.shape[2], conv_w.shape[2], axis=2)

    g = dict(pre_norm_g=small["pre_g"], w_in=g_w_in, pool_w=small["pool_w"], pool_scale=small["pool_scale"],
             conv_w=g_conv_w, conv_b=small["conv_b"], w_branch=g_w_branch, w_out=g_w_out, post_norm_g=small["post_g"])
    w = dict(pre_norm_g=pre_norm_g, w_in=w_in, pool_w=pool_w, pool_scale=pool_scale, conv_w=conv_w, conv_b=conv_b,
             w_branch=w_branch, w_out=w_out, post_norm_g=post_norm_g)
    m = dict(pre_norm_g=m_pre_norm_g, w_in=m_w_in, pool_w=m_pool_w, pool_scale=m_pool_scale, conv_w=m_conv_w,
             conv_b=m_conv_b, w_branch=m_w_branch, w_out=m_w_out, post_norm_g=m_post_norm_g)
    v = dict(pre_norm_g=v_pre_norm_g, w_in=v_w_in, pool_w=v_pool_w, pool_scale=v_pool_scale, conv_w=v_conv_w,
             conv_b=v_conv_b, w_branch=v_w_branch, w_out=v_w_out, post_norm_g=v_post_norm_g)
    order = ["pre_norm_g", "w_in", "pool_w", "pool_scale", "conv_w", "conv_b", "w_branch", "w_out", "post_norm_g"]
    upd = {n: _adamw(w[n], g[n], m[n], v[n], "adamw_" + n) for n in order}
    return (loss, grad_x, *[g[n] for n in order], *[upd[n][0] for n in order], *[upd[n][1] for n in order],
            *[upd[n][2] for n in order])
```

```python
import functools

import jax
import jax.numpy as jnp
from jax import lax
from jax.experimental import pallas as pl
from jax.experimental.pallas import tpu as pltpu

F32 = jnp.float32
BF16 = jnp.bfloat16
MESH = pl.DeviceIdType.MESH
ANY = pl.BlockSpec(memory_space=pl.ANY)

DEPTH = 2
D_MODEL = 1024
WIDTH = 512
N_IN = 8192
N_CHIPS = 4
HEAD_DIM = 64
RMS_EPS = 1e-6
POOL_HALO = 16
CONV_HALO = 16
LANE = 128
COL_POOL_V, COL_POOL_G = 0, 4
COL_CONV_X, COL_CONV_GB, COL_CONV_GC, COL_CONV_G = 8, 12, 16, 20
COL_SB_Q, COL_SB_K, COL_SB_V, COL_SB_G = 24, 28, 32, 36
COL_MERGE_1024 = 5

ADAM_LR, ADAM_B1, ADAM_B2, ADAM_EPS, ADAM_WD, ADAM_STEP = 0.001, 0.9, 0.999, 1e-08, 0.01, 10

NN = (((1,), (0,)), ((), ()))
NT = (((1,), (1,)), ((), ()))
TN = (((0,), (0,)), ((), ()))


def _sigmoid(x):
    return 1.0 / (1.0 + jnp.exp(-x))


def _silu_and_grad(x):
    s = _sigmoid(x)
    return x * s, s * (1.0 + x * (1.0 - s))


def _dot(a, b, dims):
    return lax.dot_general(a, b, dims, preferred_element_type=F32)


def _matmul(a, b, mode, out_dtype, name, tm=1024, tn=1024, tk=1024, b_lead=(), rider=None):
    b_shape = b.shape[len(b_lead):]
    if mode == "nn":
        (m, k), (k2, n) = a.shape, b_shape
    elif mode == "nt":
        (m, k), (n, k2) = a.shape, b_shape
    else:
        (k, m), (k2, n) = a.shape, b_shape
    assert k == k2 and a.dtype == BF16 and b.dtype == BF16
    tm, tn, tk = min(tm, m), min(tn, n), min(tk, k)
    assert m % tm == 0 and n % tn == 0 and k % tk == 0
    nk = k // tk
    dims = {"nn": NN, "nt": NT, "tn": TN}[mode]

    grid = (m // tm, n // tn, nk)

    def at_step(step):
        return functools.reduce(jnp.logical_and, [pl.program_id(d) == s for d, s in enumerate(step)])

    def body(*refs):
        (a_ref, b_ref, o_ref, *scratch), riding = _split_refs(refs, 2, 1, 1 if nk > 1 else 0, rider)
        _ride(rider, "start", at_step((0, 0, 0)), riding)
        _ride(rider, "middle", at_step(((3 * grid[0]) // 4, 0, 0)), riding)
        compute(a_ref, b_ref, o_ref, scratch)
        _ride(rider, "finish", at_step([g - 1 for g in grid]), riding)

    def compute(a_ref, b_ref, o_ref, scratch):
        p = _dot(a_ref[...], b_ref[...], dims)
        if nk == 1:
            o_ref[...] = p.astype(o_ref.dtype)
        else:
            acc = scratch[0]
            kk = pl.program_id(2)

            @pl.when(kk == 0)
            def _():
                acc[...] = p

            @pl.when(jnp.logical_and(kk > 0, kk < nk - 1))
            def _():
                acc[...] += p

            @pl.when(kk == nk - 1)
            def _():
                o_ref[...] = (acc[...] + p).astype(o_ref.dtype)

    if mode == "tn":
        a_spec = pl.BlockSpec((tk, tm), lambda i, j, kk: (kk, i))
    else:
        a_spec = pl.BlockSpec((tm, tk), lambda i, j, kk: (i, kk))
    squeezed = (None,) * len(b_lead)
    if mode == "nt":
        b_spec = pl.BlockSpec(squeezed + (tn, tk), lambda i, j, kk: (*b_lead, j, kk))
    else:
        b_spec = pl.BlockSpec(squeezed + (tk, tn), lambda i, j, kk: (*b_lead, kk, j))
    extra = _rider_call_args(rider, 2, 1)
    outs = pl.pallas_call(
        body, name=name, grid=grid,
        in_specs=[a_spec, b_spec] + extra["in_specs"],
        out_specs=[pl.BlockSpec((tm, tn), lambda i, j, kk: (i, j))] + extra["out_specs"],
        out_shape=[jax.ShapeDtypeStruct((m, n), out_dtype)] + extra["out_shape"],
        input_output_aliases=extra["aliases"],
        scratch_shapes=([pltpu.VMEM((tm, tn), F32)] if nk > 1 else []) + extra["scratch"],
        compiler_params=pltpu.CompilerParams(dimension_semantics=("arbitrary",) * 3 if rider else
                                             ("parallel", "parallel", "arbitrary")),
    )(a, b, *extra["inputs"])
    return (outs[0], outs[1:]) if rider else outs[0]


def _rms_fwd(x, g, name, ts=512):
    s, d = x.shape

    def body(x_ref, g_ref, h_ref):
        xv = x_ref[...]
        r = lax.rsqrt(jnp.mean(xv * xv, axis=-1, keepdims=True) + RMS_EPS)
        h_ref[...] = (xv * r * g_ref[...]).astype(BF16)

    return pl.pallas_call(
        body, name=name, grid=(s // ts,),
        in_specs=[pl.BlockSpec((ts, d), lambda i: (i, 0)), pl.BlockSpec((1, d), lambda i: (0, 0))],
        out_specs=pl.BlockSpec((ts, d), lambda i: (i, 0)),
        out_shape=jax.ShapeDtypeStruct((s, d), BF16),
    )(x, g)


def _rms_bwd(xin, g, dh, resid, out_dtype, name, ts=512):
    s, d = xin.shape
    has_resid = resid is not None

    def body(*refs):
        if has_resid:
            x_ref, g_ref, dh_ref, res_ref, dx_ref, dg_ref = refs
        else:
            x_ref, g_ref, dh_ref, dx_ref, dg_ref = refs
        xv = x_ref[...]
        dhv = dh_ref[...].astype(F32)
        r = lax.rsqrt(jnp.mean(xv * xv, axis=-1, keepdims=True) + RMS_EPS)
        nrm = xv * r
        dn = dhv * g_ref[...]
        dx = r * (dn - nrm * jnp.mean(dn * nrm, axis=-1, keepdims=True))
        if has_resid:
            dx = dx + res_ref[...]
        dx_ref[...] = dx.astype(dx_ref.dtype)
        part = jnp.sum(dhv * nrm, axis=0, keepdims=True)

        @pl.when(pl.program_id(0) == 0)
        def _():
            dg_ref[...] = part

        @pl.when(pl.program_id(0) > 0)
        def _():
            dg_ref[...] += part

    row = pl.BlockSpec((ts, d), lambda i: (i, 0))
    vec = pl.BlockSpec((1, d), lambda i: (0, 0))
    ins = [xin, g, dh] + ([resid] if has_resid else [])
    return pl.pallas_call(
        body, name=name, grid=(s // ts,),
        in_specs=[row, vec, row] + ([row] if has_resid else []),
        out_specs=[row, vec],
        out_shape=[jax.ShapeDtypeStruct((s, d), out_dtype), jax.ShapeDtypeStruct((1, d), F32)],
        compiler_params=pltpu.CompilerParams(dimension_semantics=("arbitrary",)),
    )(*ins)


def _resid_out(x, out, g, target, name, ts=512):
    s, d = x.shape
    has_loss = target is not None

    def body(*refs):
        if has_loss:
            x_ref, o_ref, g_ref, t_ref, dy_ref, loss_ref = refs
        else:
            x_ref, o_ref, g_ref, y_ref = refs
        ov = o_ref[...]
        r = lax.rsqrt(jnp.mean(ov * ov, axis=-1, keepdims=True) + RMS_EPS)
        yv = x_ref[...] + ov * r * g_ref[...]
        if not has_loss:
            y_ref[...] = yv
            return
        err = yv - t_ref[...]
        dy_ref[...] = err * (1.0 / d)
        part = jnp.sum(jnp.sum(err * err, axis=-1, keepdims=True), axis=0, keepdims=True) * (0.5 / d)
        part = jnp.broadcast_to(part, (1, LANE))

        @pl.when(pl.program_id(0) == 0)
        def _():
            loss_ref[...] = part

        @pl.when(pl.program_id(0) > 0)
        def _():
            loss_ref[...] += part

    row = pl.BlockSpec((ts, d), lambda i: (i, 0))
    vec = pl.BlockSpec((1, d), lambda i: (0, 0))
    if has_loss:
        return pl.pallas_call(
            body, name=name, grid=(s // ts,),
            in_specs=[row, row, vec, row],
            out_specs=[row, pl.BlockSpec((1, LANE), lambda i: (0, 0))],
            out_shape=[jax.ShapeDtypeStruct((s, d), F32), jax.ShapeDtypeStruct((1, LANE), F32)],
            compiler_params=pltpu.CompilerParams(dimension_semantics=("arbitrary",)),
        )(x, out, g, target)
    return pl.pallas_call(
        body, name=name, grid=(s // ts,),
        in_specs=[row, row, vec], out_specs=row,
        out_shape=jax.ShapeDtypeStruct((s, d), F32),
    )(x, out, g)


def _rows_before(ref, start, n, halo):
    if start == 0:
        return jnp.concatenate([jnp.zeros((halo, ref.shape[1]), F32), ref[0:n, :].astype(F32)], axis=0)
    return ref[start - halo:start + n, :].astype(F32)


def _rows_after(ref, start, n, halo):
    if start + n == ref.shape[0]:
        return jnp.concatenate([ref[start:start + n, :].astype(F32), jnp.zeros((halo, ref.shape[1]), F32)], axis=0)
    return ref[start:start + n + halo, :].astype(F32)


def _pick_window(group, s2, s4, s8, s16):
    return jnp.where(group == 0, s2, jnp.where(group == 1, s4, jnp.where(group == 2, s8, s16)))


def _trailing_sums(ext, group):
    s2 = ext + pltpu.roll(ext, 1, 0)
    s4 = s2 + pltpu.roll(s2, 2, 0)
    s8 = s4 + pltpu.roll(s4, 4, 0)
    s16 = s8 + pltpu.roll(s8, 8, 0)
    return _pick_window(group, s2, s4, s8, s16)


def _leading_sums(ext, group):
    n = ext.shape[0]
    s2 = ext + pltpu.roll(ext, n - 1, 0)
    s4 = s2 + pltpu.roll(s2, n - 2, 0)
    s8 = s4 + pltpu.roll(s4, n - 4, 0)
    s16 = s8 + pltpu.roll(s8, n - 8, 0)
    return _pick_window(group, s2, s4, s8, s16)


def _window_count(start, n, group):
    pos = start + lax.broadcasted_iota(jnp.int32, (n, LANE), 0)
    return jnp.minimum(pos + 1, 2 << group).astype(F32)


def _pooled(v_ref, start, n, group):
    ext = _rows_before(v_ref, start, n, POOL_HALO)
    sums = _trailing_sums(ext, group)[POOL_HALO:, :]
    return sums / _window_count(start, n, group) - ext[POOL_HALO:, :]


def _pool_fwd(u, pool_w, pool_scale, name, ts=512):
    s = u.shape[0]

    def body(v_ref, gate_ref, w_ref, sc_ref, y_ref):
        group = pl.program_id(0)
        for c in range(s // ts):
            a = c * ts
            pooled = _pooled(v_ref, a, ts, group)
            mixed = _dot(pooled.astype(BF16), w_ref[...], NN)
            gate = gate_ref[a:a + ts, :].astype(F32)
            y_ref[a:a + ts, :] = (mixed * sc_ref[...] * (gate * _sigmoid(gate))).astype(BF16)

    col = lambda base: pl.BlockSpec((s, LANE), lambda g: (0, base + g))
    return pl.pallas_call(
        body, name=name, grid=(4,),
        in_specs=[col(COL_POOL_V), col(COL_POOL_G),
                  pl.BlockSpec((None, LANE, LANE), lambda g: (g, 0, 0)),
                  pl.BlockSpec((1, LANE), lambda g: (0, g))],
        out_specs=pl.BlockSpec((s, LANE), lambda g: (0, g)),
        out_shape=jax.ShapeDtypeStruct((s, WIDTH), BF16),
    )(u, u, pool_w, pool_scale)


def _pool_bwd(u, dy, pool_w, pool_scale, name, ts=512):
    s = u.shape[0]

    def body(v_ref, gate_ref, dy_ref, w_ref, sc_ref, dv_ref, dgate_ref, dw_ref, dsc_ref):
        group = pl.program_id(0)
        w = w_ref[...]
        scale = sc_ref[...]
        dw = jnp.zeros((LANE, LANE), F32)
        dsc = jnp.zeros((1, LANE), F32)
        for c in range(s // ts):
            a = c * ts
            n_ext = ts + POOL_HALO
            gate_e = _rows_after(gate_ref, a, ts, POOL_HALO)
            dy_e = _rows_after(dy_ref, a, ts, POOL_HALO)
            silu_e, dsilu_e = _silu_and_grad(gate_e)
            dms_e = dy_e * silu_e
            dm_e = (dms_e * scale).astype(BF16)
            dpool_e = _dot(dm_e, w, NT)
            spread = _leading_sums(dpool_e / _window_count(a, n_ext, group), group)
            dv_ref[a:a + ts, :] = (spread[0:ts, :] - dpool_e[0:ts, :]).astype(BF16)
            pooled = _pooled(v_ref, a, ts, group).astype(BF16)
            mixed = _dot(pooled, w, NN)
            dgate_ref[a:a + ts, :] = (dy_e[0:ts, :] * mixed * scale * dsilu_e[0:ts, :]).astype(BF16)
            dsc = dsc + jnp.sum(dms_e[0:ts, :] * mixed, axis=0, keepdims=True)
            dw = dw + _dot(pooled, dm_e[0:ts, :], TN)
        dw_ref[...] = dw
        dsc_ref[...] = dsc

    col = lambda base: pl.BlockSpec((s, LANE), lambda g: (0, base + g))
    out_col = pl.BlockSpec((s, LANE), lambda g: (0, g))
    return pl.pallas_call(
        body, name=name, grid=(4,),
        in_specs=[col(COL_POOL_V), col(COL_POOL_G), out_col,
                  pl.BlockSpec((None, LANE, LANE), lambda g: (g, 0, 0)),
                  pl.BlockSpec((1, LANE), lambda g: (0, g))],
        out_specs=[out_col, out_col,
                   pl.BlockSpec((None, LANE, LANE), lambda g: (g, 0, 0)),
                   pl.BlockSpec((1, LANE), lambda g: (0, g))],
        out_shape=[jax.ShapeDtypeStruct((s, WIDTH), BF16), jax.ShapeDtypeStruct((s, WIDTH), BF16),
                   jax.ShapeDtypeStruct((4, LANE, LANE), F32), jax.ShapeDtypeStruct((1, WIDTH), F32)],
    )(u, u, dy, pool_w, pool_scale)


def _conv_taps(x_ref, gc_ref, start, n):
    z_ext = _rows_before(gc_ref, start, n, CONV_HALO) * _rows_before(x_ref, start, n, CONV_HALO)
    z0 = z_ext[CONV_HALO:, :]
    z1 = pltpu.roll(z_ext, 1, 0)[CONV_HALO:, :]
    z2 = pltpu.roll(z_ext, 2, 0)[CONV_HALO:, :]
    return z0, z1, z2


def _conv_fwd(u, conv_w, conv_b, name, ts=512):
    s = u.shape[0]

    def body(x_ref, gb_ref, gc_ref, g_ref, w_ref, b_ref, y_ref):
        w0, w1, w2 = w_ref[0:1, :], w_ref[1:2, :], w_ref[2:3, :]
        for c in range(s // ts):
            a = c * ts
            z0, z1, z2 = _conv_taps(x_ref, gc_ref, a, ts)
            y = w2 * z0 + w1 * z1 + w0 * z2 + b_ref[...]
            gate = g_ref[a:a + ts, :].astype(F32)
            y_ref[a:a + ts, :] = (gb_ref[a:a + ts, :].astype(F32) * y * (gate * _sigmoid(gate))).astype(BF16)

    col = lambda base: pl.BlockSpec((s, LANE), lambda j: (0, base + j))
    return pl.pallas_call(
        body, name=name, grid=(4,),
        in_specs=[col(COL_CONV_X), col(COL_CONV_GB), col(COL_CONV_GC), col(COL_CONV_G),
                  pl.BlockSpec((3, LANE), lambda j: (0, j)), pl.BlockSpec((1, LANE), lambda j: (0, j))],
        out_specs=pl.BlockSpec((s, LANE), lambda j: (0, j)),
        out_shape=jax.ShapeDtypeStruct((s, WIDTH), BF16),
    )(u, u, u, u, conv_w, conv_b)


def _conv_bwd(u, dy, conv_w, conv_b, name, ts=512):
    s = u.shape[0]

    def body(x_ref, gb_ref, gc_ref, g_ref, dy_ref, w_ref, b_ref,
             dx_ref, dgb_ref, dgc_ref, dg_ref, dw_ref, db_ref):
        w0, w1, w2 = w_ref[0:1, :], w_ref[1:2, :], w_ref[2:3, :]
        acc = [jnp.zeros((1, LANE), F32) for _ in range(4)]
        for c in range(s // ts):
            a = c * ts
            n_ext = ts + CONV_HALO
            gate_e = _rows_after(g_ref, a, ts, CONV_HALO)
            silu_e, dsilu_e = _silu_and_grad(gate_e)
            dy_e = _rows_after(dy_ref, a, ts, CONV_HALO)
            gb_e = _rows_after(gb_ref, a, ts, CONV_HALO)
            dyy_e = dy_e * silu_e * gb_e
            dz = (w2 * dyy_e + w1 * pltpu.roll(dyy_e, n_ext - 1, 0) + w0 * pltpu.roll(dyy_e, n_ext - 2, 0))[0:ts, :]
            z0, z1, z2 = _conv_taps(x_ref, gc_ref, a, ts)
            yb = w2 * z0 + w1 * z1 + w0 * z2 + b_ref[...]
            dyv = dy_e[0:ts, :]
            dyy = dyy_e[0:ts, :]
            dg_ref[a:a + ts, :] = (dyv * gb_e[0:ts, :] * yb * dsilu_e[0:ts, :]).astype(BF16)
            dgb_ref[a:a + ts, :] = (dyv * silu_e[0:ts, :] * yb).astype(BF16)
            dx_ref[a:a + ts, :] = (dz * gc_ref[a:a + ts, :].astype(F32)).astype(BF16)
            dgc_ref[a:a + ts, :] = (dz * x_ref[a:a + ts, :].astype(F32)).astype(BF16)
            for i, term in enumerate((dyy * z2, dyy * z1, dyy * z0, dyy)):
                acc[i] = acc[i] + jnp.sum(term, axis=0, keepdims=True)
        dw_ref[0:1, :] = acc[0]
        dw_ref[1:2, :] = acc[1]
        dw_ref[2:3, :] = acc[2]
        db_ref[...] = acc[3]

    col = lambda base: pl.BlockSpec((s, LANE), lambda j: (0, base + j))
    out_col = pl.BlockSpec((s, LANE), lambda j: (0, j))
    big = jax.ShapeDtypeStruct((s, WIDTH), BF16)
    return pl.pallas_call(
        body, name=name, grid=(4,),
        in_specs=[col(COL_CONV_X), col(COL_CONV_GB), col(COL_CONV_GC), col(COL_CONV_G), out_col,
                  pl.BlockSpec((3, LANE), lambda j: (0, j)), pl.BlockSpec((1, LANE), lambda j: (0, j))],
        out_specs=[out_col, out_col, out_col, out_col,
                   pl.BlockSpec((3, LANE), lambda j: (0, j)), pl.BlockSpec((1, LANE), lambda j: (0, j))],
        out_shape=[big, big, big, big,
                   jax.ShapeDtypeStruct((3, WIDTH), F32), jax.ShapeDtypeStruct((1, WIDTH), F32)],
    )(u, u, u, u, dy, conv_w, conv_b)


LOG2_E = 1.4426950408889634
LN_2 = 0.6931471805599453


def _sb_scores(q_h, k_blk, valid, later_mat, carry):
    z = _dot(q_h, k_blk, NT)
    neg_z = -z
    soft = jnp.log(1.0 + jnp.exp2(jnp.minimum(z, neg_z))) * LOG2_E
    log_keep = jnp.minimum(neg_z, 0.0) - soft
    log_beta = log_keep + z
    if valid is not None:
        log_keep = jnp.where(valid, log_keep, 0.0)
    later = _dot(log_keep.astype(BF16), later_mat, NN) + carry
    return log_keep, log_beta, later


def _masked(valid, x):
    return x if valid is None else jnp.where(valid, x, 0.0)


def _diagonal_masks(tq, tk):
    r = lax.broadcasted_iota(jnp.int32, (tq, tk), 0)
    cidx = lax.broadcasted_iota(jnp.int32, (tq, tk), 1)
    return [cidx + d * tk < r for d in range(tq // tk)]


def _triangle(tk, op):
    r = lax.broadcasted_iota(jnp.int32, (tk, tk), 0)
    cidx = lax.broadcasted_iota(jnp.int32, (tk, tk), 1)
    return op(r, cidx).astype(BF16)


def _split_refs(refs, n_in, n_out, n_scratch, rider):
    r_in = len(rider.inputs) if rider else 0
    r_out = len(rider.out_shape) if rider else 0
    a, b = n_in + r_in, n_in + r_in + n_out + r_out
    own = refs[:n_in] + refs[a:a + n_out] + refs[b:b + n_scratch]
    return own, (refs[n_in:a], refs[a + n_out:b], refs[b + n_scratch:])


def _rider_call_args(rider, n_in, n_out):
    if rider is None:
        return dict(in_specs=[], out_specs=[], out_shape=[], aliases={}, scratch=[], inputs=[])
    return dict(in_specs=[ANY] * len(rider.inputs), out_specs=[ANY] * len(rider.out_shape),
                out_shape=list(rider.out_shape), scratch=list(rider.scratch), inputs=list(rider.inputs),
                aliases={n_in + a: n_out + b for a, b in rider.aliases.items()})


def _ride(rider, phase, when, parts):
    fn = getattr(rider, phase) if rider else None
    if fn is not None:
        pl.when(when)(lambda: fn(*parts))


def _sb_fwd(u, name, t=512, tk=256, pairs=4, rider=None):
    s = u.shape[0]
    assert s // tk <= LANE and 4 % pairs == 0 and t % tk == 0
    scale = HEAD_DIM ** -0.5
    nh = 2 * pairs
    wide = pairs * LANE
    ratio = t // tk
    groups, nq = 4 // pairs, s // t

    def body(*refs):
        own, riding = _split_refs(refs, 4, 3, 4, rider)
        q_ref, k_ref, v_ref, g_ref, o_ref, y_ref, after_ref, kb_ref, vb_ref, acc_ref, carry_ref = own
        grp = pl.program_id(0)
        i = pl.program_id(1)
        _ride(rider, "start", jnp.logical_and(grp == 0, i == 0), riding)
        _ride(rider, "middle", jnp.logical_and(grp == groups - 1, i == (3 * nq) // 4), riding)

        @pl.when(i == 0)
        def _():
            kb_ref[...] = k_ref[...].astype(BF16)
            vb_ref[...] = v_ref[...].astype(BF16)

        lane = lax.broadcasted_iota(jnp.int32, (t, LANE), 1)
        first = lane < HEAD_DIM
        after_ref[...] = jnp.zeros_like(after_ref)
        qv = q_ref[...].astype(F32) * (scale * LOG2_E)
        q_heads = []
        for p in range(pairs):
            qp = qv[:, p * LANE:(p + 1) * LANE]
            q_heads += [jnp.where(first, qp, 0.0).astype(BF16), jnp.where(first, 0.0, qp).astype(BF16)]
        later_mat = _triangle(tk, lambda r, cidx: r > cidx)
        acc_ref[...] = jnp.zeros_like(acc_ref)
        carry_ref[...] = jnp.zeros_like(carry_ref)

        def block(kb, valid, lo=0):
            rows = pl.ds(pl.multiple_of(kb * tk, tk), tk)
            k_blk = kb_ref[rows, :]
            v_blk = vb_ref[rows, :]
            carries = [carry_ref[h, lo:, :] for h in range(nh)]
            afters = [after_ref[lo:, h * LANE:(h + 1) * LANE] for h in range(nh)]
            accs = [acc_ref[h, lo:, :] for h in range(nh)]
            outs = []
            for h in range(nh):
                cols = slice((h // 2) * LANE, (h // 2 + 1) * LANE)
                log_keep, log_beta, later = _sb_scores(q_heads[h][lo:], k_blk[:, cols], valid, later_mat, carries[h])
                a = _masked(valid, jnp.exp2(log_beta + later))
                outs.append((accs[h] + _dot(a.astype(BF16), v_blk[:, cols], NN),
                             carries[h] + jnp.sum(log_keep, axis=1, keepdims=True),
                             jnp.where(lane[lo:] == kb, carries[h], afters[h])))
            for h in range(nh):
                acc_ref[h, lo:, :] = outs[h][0]
                carry_ref[h, lo:, :] = outs[h][1]
                after_ref[lo:, h * LANE:(h + 1) * LANE] = outs[h][2]

        def step(j, _):
            block(ratio * i - 1 - j, None)
            return 0

        masks = _diagonal_masks(t, tk)
        for d in reversed(range(ratio)):
            block(ratio * i + d, masks[d][d * tk:], d * tk)
        lax.fori_loop(0, ratio * i, step, 0)
        for p in range(pairs):
            cols = slice(p * LANE, (p + 1) * LANE)
            o = jnp.where(first, acc_ref[2 * p], acc_ref[2 * p + 1])
            o_ref[:, cols] = o
            gate = g_ref[:, cols].astype(F32)
            y_ref[:, cols] = (o * gate * _sigmoid(gate)).astype(BF16)
        _ride(rider, "finish", jnp.logical_and(grp == groups - 1, i == nq - 1), riding)

    blk = lambda base: pl.BlockSpec((t, wide), lambda g, i: (i, base // pairs + g))
    full = lambda base: pl.BlockSpec((s, wide), lambda g, i: (0, base // pairs + g))
    out_blk = pl.BlockSpec((t, wide), lambda g, i: (i, g))
    extra = _rider_call_args(rider, 4, 3)
    outs = pl.pallas_call(
        body, name=name, grid=(groups, nq),
        in_specs=[blk(COL_SB_Q), full(COL_SB_K), full(COL_SB_V), blk(COL_SB_G)] + extra["in_specs"],
        out_specs=[out_blk, out_blk, pl.BlockSpec((t, nh * LANE), lambda g, i: (i, g))] + extra["out_specs"],
        out_shape=[jax.ShapeDtypeStruct((s, WIDTH), F32), jax.ShapeDtypeStruct((s, WIDTH), BF16),
                   jax.ShapeDtypeStruct((s, 8 * LANE), F32)] + extra["out_shape"],
        input_output_aliases=extra["aliases"],
        scratch_shapes=[pltpu.VMEM((s, wide), BF16), pltpu.VMEM((s, wide), BF16),
                        pltpu.VMEM((nh, t, LANE), F32), pltpu.VMEM((nh, t, 1), F32)] + extra["scratch"],
        compiler_params=pltpu.CompilerParams(dimension_semantics=("arbitrary", "arbitrary")),
    )(u, u, u, u, *extra["inputs"])
    return outs[:3], outs[3:]


def _sb_bwd(u, o, after, dy, name, t=512, tk=256, pairs=2, rider=None):
    s = u.shape[0]
    nq = s // t
    scale = HEAD_DIM ** -0.5
    nh = 2 * pairs
    wide = pairs * LANE
    ratio = t // tk
    groups = 4 // pairs

    def body(*refs):
        own, riding = _split_refs(refs, 7, 4, 6, rider)
        (q_ref, k_ref, v_ref, g_ref, o_ref, after_ref, dy_ref, dq_ref, dk_ref, dv_ref, dg_ref,
         kb_ref, vb_ref, dk_acc, dv_acc, dq_acc, carry_ref) = own
        grp = pl.program_id(0)
        i = pl.program_id(1)
        _ride(rider, "start", jnp.logical_and(grp == 0, i == 0), riding)

        @pl.when(i == 0)
        def _():
            kb_ref[...] = k_ref[...].astype(BF16)
            vb_ref[...] = v_ref[...].astype(BF16)
            dk_acc[...] = jnp.zeros_like(dk_acc)
            dv_acc[...] = jnp.zeros_like(dv_acc)

        lane = lax.broadcasted_iota(jnp.int32, (t, LANE), 1)
        first = lane < HEAD_DIM
        gate = g_ref[...].astype(F32)
        silu, dsilu = _silu_and_grad(gate)
        dyv = dy_ref[...]
        do = dyv * silu
        dg_ref[...] = (dyv * o_ref[...] * dsilu).astype(BF16)
        qv = q_ref[...].astype(F32) * (scale * LOG2_E)
        do_heads, q_heads = [], []
        for p in range(pairs):
            cols = slice(p * LANE, (p + 1) * LANE)
            do_heads += [jnp.where(first, do[:, cols], 0.0).astype(BF16), jnp.where(first, 0.0, do[:, cols]).astype(BF16)]
            q_heads += [jnp.where(first, qv[:, cols], 0.0).astype(BF16), jnp.where(first, 0.0, qv[:, cols]).astype(BF16)]
        later_mat = _triangle(tk, lambda r, cidx: r > cidx)
        before_mat = _triangle(tk, lambda r, cidx: r < cidx)
        dq_acc[...] = jnp.zeros_like(dq_acc)
        carry_ref[...] = jnp.zeros_like(carry_ref)

        def block(kb, valid, lo=0):
            rows = pl.ds(pl.multiple_of(kb * tk, tk), tk)
            k_blk = kb_ref[rows, :]
            v_blk = vb_ref[rows, :]
            carries = [carry_ref[h, lo:, :] for h in range(nh)]
            dq_old = [dq_acc[h, lo:, :] for h in range(nh)]
            dk_old = dk_acc[rows, :]
            dv_old = dv_acc[rows, :]
            outs = []
            for h in range(nh):
                cols = slice((h // 2) * LANE, (h // 2 + 1) * LANE)
                q_h, do_h = q_heads[h][lo:], do_heads[h][lo:]
                after = jnp.sum(jnp.where(lane[lo:] == kb, after_ref[lo:, h * LANE:(h + 1) * LANE], 0.0), axis=1,
                                keepdims=True)
                _, log_beta, later = _sb_scores(q_h, k_blk[:, cols], valid, later_mat, after)
                beta = jnp.exp2(log_beta)
                a = _masked(valid, jnp.exp2(log_beta + later))
                da = _dot(do_h, v_blk[:, cols], NT)
                gterm = a * da
                before = _dot(gterm.astype(BF16), before_mat, NN) + carries[h]
                dz_b = _masked(valid, gterm * (1.0 - beta) - beta * before).astype(BF16)
                outs.append((dq_old[h] + _dot(dz_b, k_blk[:, cols], NN), _dot(dz_b, q_h, TN),
                             _dot(a.astype(BF16), do_h, TN),
                             carries[h] + jnp.sum(gterm, axis=1, keepdims=True)))
            for h in range(nh):
                dq_acc[h, lo:, :] = outs[h][0]
                carry_ref[h, lo:, :] = outs[h][3]
            dk_new = [outs[2 * p][1] + outs[2 * p + 1][1] for p in range(pairs)]
            dv_new = [outs[2 * p][2] + outs[2 * p + 1][2] for p in range(pairs)]
            dk_acc[rows, :] = dk_old + (dk_new[0] if pairs == 1 else jnp.concatenate(dk_new, axis=1))
            dv_acc[rows, :] = dv_old + (dv_new[0] if pairs == 1 else jnp.concatenate(dv_new, axis=1))

        def step(kb, _):
            block(kb, None)
            return 0

        lax.fori_loop(0, ratio * i, step, 0)
        masks = _diagonal_masks(t, tk)
        for d in range(ratio):
            block(ratio * i + d, masks[d][d * tk:], d * tk)
        for p in range(pairs):
            dq_ref[:, p * LANE:(p + 1) * LANE] = (jnp.where(first, dq_acc[2 * p], dq_acc[2 * p + 1]) * scale).astype(BF16)

        @pl.when(i == nq - 1)
        def _():
            dk_ref[...] = (dk_acc[...] * LN_2).astype(BF16)
            dv_ref[...] = dv_acc[...].astype(BF16)

        _ride(rider, "finish", jnp.logical_and(grp == groups - 1, i == nq - 1), riding)

    blk = lambda base: pl.BlockSpec((t, wide), lambda g, i: (i, base // pairs + g))
    full = lambda base: pl.BlockSpec((s, wide), lambda g, i: (0, base // pairs + g))
    out_blk = pl.BlockSpec((t, wide), lambda g, i: (i, g))
    out_full = pl.BlockSpec((s, wide), lambda g, i: (0, g))
    big = jax.ShapeDtypeStruct((s, WIDTH), BF16)
    extra = _rider_call_args(rider, 7, 4)
    outs = pl.pallas_call(
        body, name=name, grid=(groups, nq),
        in_specs=[blk(COL_SB_Q), full(COL_SB_K), full(COL_SB_V), blk(COL_SB_G), out_blk,
                  pl.BlockSpec((t, nh * LANE), lambda g, i: (i, g)), out_blk] + extra["in_specs"],
        out_specs=[out_blk, out_full, out_full, out_blk] + extra["out_specs"],
        out_shape=[big, big, big, big] + extra["out_shape"],
        input_output_aliases=extra["aliases"],
        scratch_shapes=[pltpu.VMEM((s, wide), BF16), pltpu.VMEM((s, wide), BF16),
                        pltpu.VMEM((s, wide), F32), pltpu.VMEM((s, wide), F32),
                        pltpu.VMEM((nh, t, LANE), F32), pltpu.VMEM((nh, t, 1), F32)] + extra["scratch"],
        compiler_params=pltpu.CompilerParams(dimension_semantics=("arbitrary", "arbitrary")),
    )(u, u, u, u, o, after, dy, *extra["inputs"])
    return outs[:4], outs[4:]


def _gate_fwd(u, ys, w_branch, name, ts=256):
    s = u.shape[0]

    def body(m0, m1, m2, y0, y1, y2, w_ref, p0, p1, p2, out_ref):
        tot = None
        for n, (m_ref, y_ref, p_ref) in enumerate(((m0, y0, p0), (m1, y1, p1), (m2, y2, p2))):
            proj = _dot(y_ref[...], w_ref[n], NN)
            p_ref[...] = proj.astype(BF16)
            term = _sigmoid(m_ref[...].astype(F32)) * proj
            tot = term if tot is None else tot + term
        out_ref[...] = tot.astype(BF16)

    mspec = lambda n: pl.BlockSpec((ts, D_MODEL), lambda i: (i, COL_MERGE_1024 + n))
    row = pl.BlockSpec((ts, D_MODEL), lambda i: (i, 0))
    yspec = pl.BlockSpec((ts, WIDTH), lambda i: (i, 0))
    big = jax.ShapeDtypeStruct((s, D_MODEL), BF16)
    outs = pl.pallas_call(
        body, name=name, grid=(s // ts,),
        in_specs=[mspec(0), mspec(1), mspec(2), yspec, yspec, yspec,
                  pl.BlockSpec(w_branch.shape, lambda i: (0, 0, 0))],
        out_specs=[row] * 4, out_shape=[big] * 4,
    )(u, u, u, *ys, w_branch)
    return outs[:3], outs[3]


def _gate_bwd(u, projs, dmerged, w_branch, name, ts=256, rider=None):
    s = u.shape[0]
    steps = s // ts

    def body(*refs):
        own, riding = _split_refs(refs, 8, 9, 0, rider)
        m0, m1, m2, p0, p1, p2, dm_ref, w_ref, dp0, dp1, dp2, dl0, dl1, dl2, dy0, dy1, dy2 = own
        _ride(rider, "start", pl.program_id(0) == 0, riding)
        dm = dm_ref[...].astype(F32)
        for n, (m_ref, p_ref, dp_ref, dl_ref, dy_ref) in enumerate(((m0, p0, dp0, dl0, dy0), (m1, p1, dp1, dl1, dy1),
                                                                    (m2, p2, dp2, dl2, dy2))):
            gate = _sigmoid(m_ref[...].astype(F32))
            dp = (dm * gate).astype(BF16)
            dp_ref[...] = dp
            dl_ref[...] = (dm * p_ref[...].astype(F32) * gate * (1.0 - gate)).astype(BF16)
            dy_ref[...] = _dot(dp, w_ref[n], NT)
        _ride(rider, "finish", pl.program_id(0) == steps - 1, riding)

    mspec = lambda n: pl.BlockSpec((ts, D_MODEL), lambda i: (i, COL_MERGE_1024 + n))
    row = pl.BlockSpec((ts, D_MODEL), lambda i: (i, 0))
    yspec = pl.BlockSpec((ts, WIDTH), lambda i: (i, 0))
    big = jax.ShapeDtypeStruct((s, D_MODEL), BF16)
    extra = _rider_call_args(rider, 8, 9)
    outs = pl.pallas_call(
        body, name=name, grid=(steps,),
        in_specs=[mspec(0), mspec(1), mspec(2), row, row, row, row,
                  pl.BlockSpec(w_branch.shape, lambda i: (0, 0, 0))] + extra["in_specs"],
        out_specs=[row] * 6 + [yspec] * 3 + extra["out_specs"],
        out_shape=[big] * 6 + [jax.ShapeDtypeStruct((s, WIDTH), F32)] * 3 + extra["out_shape"],
        input_output_aliases=extra["aliases"], scratch_shapes=extra["scratch"],
        compiler_params=pltpu.CompilerParams(dimension_semantics=("arbitrary",)),
    )(u, u, u, *projs, dmerged, w_branch, *extra["inputs"])
    return outs[:3], outs[3:6], outs[6:9], outs[9:]


def _as_rows(a):
    return a.reshape(-1, a.shape[-1])


def _row_tile(rows, cols, bytes_per_row_elem=4, cap=1 << 20):
    tr = rows
    while tr * cols * bytes_per_row_elem > cap and tr % 2 == 0 and (tr // 2) % 16 == 0:
        tr //= 2
    return tr


def _cast_bf16(a, name):
    a2 = _as_rows(a)
    rows, cols = a2.shape
    tr = _row_tile(rows, cols)

    def body(a_ref, o_ref):
        o_ref[...] = a_ref[...].astype(BF16)

    spec = pl.BlockSpec((tr, cols), lambda i: (i, 0))
    out = pl.pallas_call(body, name=name, grid=(rows // tr,), in_specs=[spec], out_specs=spec,
                         out_shape=jax.ShapeDtypeStruct((rows, cols), BF16))(a2)
    return out.reshape(a.shape)


def _adamw(w, g, m, v, name):
    shape = w.shape
    w2, g2, m2, v2 = (_as_rows(a) for a in (w, g, m, v))
    rows, cols = w2.shape
    tr = _row_tile(rows, cols)
    c1 = 1.0 - ADAM_B1 ** ADAM_STEP
    c2 = 1.0 - ADAM_B2 ** ADAM_STEP

    def body(w_ref, g_ref, m_ref, v_ref, d_ref, nm_ref, nv_ref):
        gv = g_ref[...]
        nm = ADAM_B1 * m_ref[...] + (1.0 - ADAM_B1) * gv
        nv = ADAM_B2 * v_ref[...] + (1.0 - ADAM_B2) * (gv * gv)
        nm_ref[...] = nm
        nv_ref[...] = nv
        d_ref[...] = -ADAM_LR * ((nm / c1) / (jnp.sqrt(nv / c2) + ADAM_EPS) + ADAM_WD * w_ref[...])

    spec = pl.BlockSpec((tr, cols), lambda i: (i, 0))
    sds = jax.ShapeDtypeStruct((rows, cols), F32)
    outs = pl.pallas_call(body, name=name, grid=(rows // tr,), in_specs=[spec] * 4, out_specs=[spec] * 3,
                          out_shape=[sds] * 3)(w2, g2, m2, v2)
    return tuple(o.reshape(shape) for o in outs)


def _sum_slots(a, out_dtype, name):
    n = a.shape[0]
    a3 = a.reshape(n, -1, a.shape[-1])
    _, rows, cols = a3.shape
    tr = _row_tile(rows, cols * n)

    def body(a_ref, o_ref):
        tot = a_ref[0].astype(F32)
        for k in range(1, n):
            tot = tot + a_ref[k].astype(F32)
        o_ref[...] = tot.astype(out_dtype)

    out = pl.pallas_call(
        body, name=name, grid=(rows // tr,),
        in_specs=[pl.BlockSpec((n, tr, cols), lambda i: (0, i, 0))],
        out_specs=pl.BlockSpec((tr, cols), lambda i: (i, 0)),
        out_shape=jax.ShapeDtypeStruct((rows, cols), out_dtype))(a3)
    return out.reshape(a.shape[1:])


def _chip_sum(own, recv, axis, core, name):
    half = recv.shape
    nd = len(half)
    last = nd - 1
    if axis == last:
        tl, nt = half[last], 1
    else:
        tl = min(half[last], 2048)
        nt = half[last] // tl
    block = half[:last] + (tl,)

    def own_index(i, core_ref):
        idx = [0] * nd
        idx[last] = i
        if axis == last:
            idx[last] = core_ref[0]
        else:
            idx[axis] = core_ref[0]
        return tuple(idx)

    def recv_index(i, core_ref):
        idx = [0] * nd
        idx[last] = i
        return tuple(idx)

    def body(core_ref, own_ref, recv_ref, o_ref):
        o_ref[...] = (own_ref[...] + recv_ref[...]).astype(BF16)

    return pl.pallas_call(
        body, name=name,
        grid_spec=pltpu.PrefetchScalarGridSpec(
            num_scalar_prefetch=1, grid=(nt,),
            in_specs=[pl.BlockSpec(block, own_index), pl.BlockSpec(block, recv_index)],
            out_specs=pl.BlockSpec(block, recv_index)),
        out_shape=jax.ShapeDtypeStruct(half, BF16),
    )(core, own, recv)


def _mesh_position():
    return lax.axis_index("x"), lax.axis_index("y"), lax.axis_index("c")


def _other_chips(x, y):
    return [(1 - x, y), (x, 1 - y), (1 - x, 1 - y)]


ALL_FLIPS = [(0, 0, 1), (1, 0, 0), (0, 1, 0), (1, 1, 0), (1, 0, 1), (0, 1, 1), (1, 1, 1)]


def _half(ref, axis, which, size):
    idx = [slice(None)] * len(ref.shape)
    idx[axis] = pl.ds(which * size, size)
    return ref.at[tuple(idx)]


def _sub(ref, picks):
    idx = [slice(None)] * len(ref.shape)
    for axis, start, size in picks:
        idx[axis] = pl.ds(start, size)
    return ref.at[tuple(idx)]


def _remote(src, dst, sems_send, sems_recv, k, to):
    return pltpu.make_async_remote_copy(src_ref=src, dst_ref=dst, send_sem=sems_send.at[k], recv_sem=sems_recv.at[k],
                                        device_id=to, device_id_type=MESH)


def _cast_shard(w, layer, shard_axis, pos, name, tr=512):
    shape = w.shape[1:]
    nd = len(shape)
    assert shard_axis in (nd - 1, nd - 2)
    rows, cols = shape[-2:]
    tr = min(tr, rows)
    nt = rows // tr
    lead = shape[:-2]
    full = list(shape)
    full[shard_axis] *= N_CHIPS
    block = (1,) * len(lead) + (tr, cols)

    def in_index(*args):
        return (layer, *args[:-1], 0)

    def out_index(*args):
        *g, pos_ref = args
        if shard_axis == nd - 1:
            return (*g, pos_ref[1])
        return (*g[:-1], pos_ref[1] * nt + g[-1], 0)

    def body(pos_ref, a_ref, o_ref):
        o_ref[...] = a_ref[...].astype(BF16)

    return pl.pallas_call(
        body, name=name,
        grid_spec=pltpu.PrefetchScalarGridSpec(
            num_scalar_prefetch=1, grid=lead + (nt,),
            in_specs=[pl.BlockSpec((None,) + block, in_index)], out_specs=pl.BlockSpec(block, out_index)),
        out_shape=jax.ShapeDtypeStruct(tuple(full), BF16),
    )(pos, w)


class _Rider:
    def __init__(self, inputs, out_shape, aliases, scratch, start, middle, finish):
        self.inputs, self.out_shape, self.aliases, self.scratch = inputs, out_shape, aliases, scratch
        self.start, self.middle, self.finish = start, middle, finish


def _weight_gather_rider(fulls, layout):
    n = len(fulls)

    def copies(outs, sems):
        send_sems, recv_sems = sems
        x, y, c = _mesh_position()
        chips = _other_chips(x, y)
        sibling = (x, y, 1 - c)
        mine = 2 * x + y

        def place(t, chip, core):
            sh_axis, sh_size, half_axis, half_size = layout[t]
            return _sub(outs[t], [(sh_axis, chip * sh_size, sh_size), (half_axis, core * half_size, half_size)])

        direct, arrive, forward, arrive_fwd = [], [], [], []
        for t in range(n):
            for k, (px, py) in enumerate(chips):
                theirs = 2 * px + py
                direct.append(_remote(place(t, mine, c), place(t, mine, c), send_sems, recv_sems, 6 * t + k, (px, py, c)))
                arrive.append(_remote(place(t, theirs, c), place(t, theirs, c), send_sems, recv_sems, 6 * t + k, (px, py, c)))
                forward.append(_remote(place(t, theirs, c), place(t, theirs, c), send_sems, recv_sems, 6 * t + 3 + k, sibling))
                arrive_fwd.append(_remote(place(t, theirs, 1 - c), place(t, theirs, 1 - c), send_sems, recv_sems,
                                          6 * t + 3 + k, sibling))
        return direct, arrive, forward, arrive_fwd

    def start(ins, outs, sems):
        for cp in copies(outs, sems)[0]:
            cp.start()

    def middle(ins, outs, sems):
        _, arrive, forward, _ = copies(outs, sems)
        for a, f in zip(arrive, forward):
            a.wait_recv()
            f.start()

    def finish(ins, outs, sems):
        direct, _, forward, arrive_fwd = copies(outs, sems)
        for cp in arrive_fwd:
            cp.wait_recv()
        for cp in direct + forward:
            cp.wait_send()

    return _Rider(list(fulls), [jax.ShapeDtypeStruct(a.shape, a.dtype) for a in fulls], {k: k for k in range(n)},
                  [pltpu.SemaphoreType.DMA((6 * n,)), pltpu.SemaphoreType.DMA((6 * n,))], start, middle, finish)


WEIGHT_LAYOUT = [(1, 2048, 0, 512), (2, 256, 1, 256), (0, 256, 1, 512)]


def _gather_weights(fulls, layout, conv_w):
    rider = _weight_gather_rider(fulls, layout)
    n = len(fulls)

    def body(*refs):
        cw, outs, cw_f = refs[n], refs[n + 1:2 * n + 1], refs[2 * n + 1]
        sems, (cw_send, cw_recv, local_sem) = refs[2 * n + 2:2 * n + 4], refs[2 * n + 4:]
        x, y, c = _mesh_position()
        chips = _other_chips(x, y)
        mine = 2 * x + y
        local = pltpu.make_async_copy(cw, cw_f.at[mine], local_sem.at[0])
        local.start()
        rider.start(None, outs, sems)
        small = [_remote(cw, cw_f.at[mine], cw_send, cw_recv, k, (*chip, c)) for k, chip in enumerate(chips)]
        for cp in small:
            cp.start()
        rider.middle(None, outs, sems)
        rider.finish(None, outs, sems)
        for k, (px, py) in enumerate(chips):
            _remote(cw, cw_f.at[2 * px + py], cw_send, cw_recv, k, (px, py, c)).wait_recv()
        for cp in small:
            cp.wait_send()
        local.wait()

    outs = pl.pallas_call(
        body, name="gather_weights",
        in_specs=[ANY] * (n + 1), out_specs=[ANY] * (n + 1),
        out_shape=rider.out_shape + [jax.ShapeDtypeStruct((N_CHIPS,) + conv_w.shape, F32)],
        input_output_aliases=rider.aliases,
        scratch_shapes=rider.scratch + [pltpu.SemaphoreType.DMA((3,)), pltpu.SemaphoreType.DMA((3,)),
                                        pltpu.SemaphoreType.DMA((1,))],
    )(*fulls, conv_w)
    return outs[:n], outs[n]


def _swap_rider(items):
    n = len(items)
    halves = []
    for a, axis in items:
        shp = list(a.shape)
        shp[axis] //= 2
        halves.append(tuple(shp))

    def copies(ins, outs, sems):
        x, y, c = _mesh_position()
        return [_remote(_half(ins[k], items[k][1], 1 - c, halves[k][items[k][1]]), outs[k], sems[0], sems[1], k,
                        (x, y, 1 - c)) for k in range(n)]

    def start(ins, outs, sems):
        for cp in copies(ins, outs, sems):
            cp.start()

    def finish(ins, outs, sems):
        for cp in copies(ins, outs, sems):
            cp.wait()

    return _Rider([a for a, _ in items], [jax.ShapeDtypeStruct(h, a.dtype) for h, (a, _) in zip(halves, items)], {},
                  [pltpu.SemaphoreType.DMA((n,)), pltpu.SemaphoreType.DMA((n,))], start, None, finish)


def _swap_halves(items, name):
    rider = _swap_rider(items)
    n = len(items)

    def body(*refs):
        parts = (refs[:n], refs[n:2 * n], refs[2 * n:])
        rider.start(*parts)
        rider.finish(*parts)

    return pl.pallas_call(
        body, name=name, in_specs=[ANY] * n, out_specs=[ANY] * n, out_shape=rider.out_shape,
        scratch_shapes=rider.scratch,
    )(*rider.inputs)


def _grad_exchange_rider(items):
    n = len(items)
    slices = []
    for a, axis in items:
        shp = list(a.shape)
        shp[axis] //= N_CHIPS
        slices.append(tuple(shp))

    def copies(ins, outs, sems):
        send_sems, recv_sems = sems
        x, y, c = _mesh_position()
        made = []
        for k in range(n):
            axis = items[k][1]
            for r, (px, py) in enumerate(_other_chips(x, y)):
                made.append(_remote(_half(ins[k], axis, 2 * px + py, slices[k][axis]), outs[k].at[r],
                                    send_sems, recv_sems, 3 * k + r, (px, py, c)))
        return made

    def start(ins, outs, sems):
        for cp in copies(ins, outs, sems):
            cp.start()

    def finish(ins, outs, sems):
        for cp in copies(ins, outs, sems):
            cp.wait()

    return _Rider([a for a, _ in items], [jax.ShapeDtypeStruct((N_CHIPS - 1,) + s, BF16) for s in slices], {},
                  [pltpu.SemaphoreType.DMA((3 * n,)), pltpu.SemaphoreType.DMA((3 * n,))], start, None, finish)


def _small_gather_rider(small):
    def copies(ins, outs, sems):
        x, y, c = _mesh_position()
        me = 4 * x + 2 * y + c
        local = pltpu.make_async_copy(ins[0], outs[0].at[me], sems[2].at[0])
        remote = [_remote(ins[0], outs[0].at[me], sems[0], sems[1], r, (x ^ fx, y ^ fy, c ^ fc))
                  for r, (fx, fy, fc) in enumerate(ALL_FLIPS)]
        return local, remote

    def start(ins, outs, sems):
        local, remote = copies(ins, outs, sems)
        local.start()
        for cp in remote:
            cp.start()

    def finish(ins, outs, sems):
        local, remote = copies(ins, outs, sems)
        for cp in remote:
            cp.wait()
        local.wait()

    n = len(ALL_FLIPS)
    return _Rider([small], [jax.ShapeDtypeStruct((2 * N_CHIPS,) + small.shape, F32)], {},
                  [pltpu.SemaphoreType.DMA((n,)), pltpu.SemaphoreType.DMA((n,)), pltpu.SemaphoreType.DMA((1,))],
                  start, None, finish)


def _sum_chips(recv, own, shard_axis, split_axis, pos, dest, layer, name, tr=128):
    sl = recv.shape[1:]
    nd = len(sl)
    tiled = nd == 2 and sl[0] > tr
    nt = sl[0] // tr if tiled else 1
    block = ((tr,) + sl[1:]) if tiled else sl
    shard = list(sl)
    shard[split_axis] *= 2

    def recv_index(i, pos_ref):
        return (0, i) + (0,) * (nd - 1) if tiled else (0,) * (nd + 1)

    def own_index(i, pos_ref):
        idx = [0] * nd
        idx[shard_axis] = pos_ref[1]
        if tiled:
            idx[0] = pos_ref[1] * nt + i if shard_axis == 0 else i
        return tuple(idx)

    def out_index(i, pos_ref):
        idx = [0] * nd
        idx[split_axis] = pos_ref[0]
        if tiled:
            idx[0] = pos_ref[0] * nt + i if split_axis == 0 else i
        return (layer, *idx)

    def body(pos_ref, recv_ref, own_ref, *rest):
        o_ref = rest[-1]
        tot = own_ref[...].astype(F32)
        for k in range(N_CHIPS - 1):
            tot = tot + recv_ref[k].astype(F32)
        o_ref[0] = tot

    in_specs = [pl.BlockSpec((N_CHIPS - 1,) + block, recv_index), pl.BlockSpec(block, own_index)]
    args = [pos, recv, own]
    aliases = {}
    if dest is not None:
        in_specs.append(ANY)
        args.append(dest)
        aliases = {3: 0}
    return pl.pallas_call(
        body, name=name,
        grid_spec=pltpu.PrefetchScalarGridSpec(
            num_scalar_prefetch=1, grid=(nt,), in_specs=in_specs,
            out_specs=pl.BlockSpec((1,) + block, out_index)),
        out_shape=jax.ShapeDtypeStruct((DEPTH,) + tuple(shard), F32),
        input_output_aliases=aliases,
    )(*args)


def _share_halves(bufs, late_small, name):
    n = len(bufs)
    small = _small_gather_rider(late_small)

    def body(*refs):
        outs, (send_sems, recv_sems) = refs[n + 1:2 * n + 1], refs[2 * n + 2:2 * n + 4]
        small_parts = ([refs[n]], [refs[2 * n + 1]], refs[2 * n + 4:])
        x, y, c = _mesh_position()
        small.start(*small_parts)
        copies = []
        for k, (a, axis) in enumerate(bufs):
            size = a.shape[1 + axis] // 2
            mine = _half(outs[k], 1 + axis, c, size)
            copies.append(_remote(mine, mine, send_sems, recv_sems, k, (x, y, 1 - c)))
        for cp in copies:
            cp.start()
        for cp in copies:
            cp.wait()
        small.finish(*small_parts)

    outs = pl.pallas_call(
        body, name=name, in_specs=[ANY] * (n + 1), out_specs=[ANY] * (n + 1),
        out_shape=[jax.ShapeDtypeStruct(a.shape, F32) for a, _ in bufs] + small.out_shape,
        input_output_aliases={k: k for k in range(n)},
        scratch_shapes=[pltpu.SemaphoreType.DMA((n,)), pltpu.SemaphoreType.DMA((n,))] + small.scratch,
    )(*[a for a, _ in bufs], late_small)
    return outs[:n], outs[n]


def _layer_fwd(x, p, l, rider=None, proj_rider=None):
    tag = f"l{l}_"
    h = _rms_fwd(x, p["pre_g"], tag + "pre_norm")
    u = _matmul(h, p["w_in"], "nn", BF16, tag + "in_proj", rider=proj_rider)
    if proj_rider is not None:
        u, (w_branch, w_out) = u
        p = dict(p, w_branch=w_branch, w_out=w_out)
    y_pool = _pool_fwd(u, p["pool_w"], p["pool_scale"], tag + "pool")
    y_conv = _conv_fwd(u, p["conv_w"], p["conv_b"], tag + "conv")
    (o_sb, y_sb, sb_after), carried = _sb_fwd(u, tag + "stickbreak", rider=rider)
    ys = [y_pool, y_conv, y_sb]
    projs, merged = _gate_fwd(u, ys, p["w_branch"], tag + "merge")
    out = _matmul(merged, p["w_out"], "nn", F32, tag + "out_proj")
    saved = dict(x=x, h=h, u=u, ys=ys, o_sb=o_sb, sb_after=sb_after, projs=projs, merged=merged, out=out)
    return out, saved, carried, p


def _layer_bwd(dy, p, saved, l, merge_rider=None, early=None, before_dw=None, late=None):
    tag = f"l{l}_bwd_"
    u = saved["u"]
    d_out, g_post = _rms_bwd(saved["out"], p["post_g"], dy, None, BF16, tag + "post_norm")
    d_merged = _matmul(d_out, p["w_out"], "nt", BF16, tag + "out_proj_dx")
    g_w_out = _matmul(saved["merged"], d_out, "tn", F32, tag + "out_proj_dw", tk=2048)
    d_projs, d_logits, d_ys, carried_merge = _gate_bwd(u, saved["projs"], d_merged, p["w_branch"], tag + "merge",
                                                       rider=merge_rider)
    g_w_branch = jnp.stack([_matmul(saved["ys"][n], d_projs[n], "tn", F32, tag + f"branch_dw{n}", tk=2048)
                            for n in range(3)])
    rider = early(g_w_branch, g_w_out, carried_merge) if early else None
    d_pv, d_pg, g_pool_w, g_pool_scale = _pool_bwd(u, d_ys[0], p["pool_w"], p["pool_scale"], tag + "pool")
    d_cx, d_cgb, d_cgc, d_cg, g_conv_w, g_conv_b = _conv_bwd(u, d_ys[1], p["conv_w"], p["conv_b"], tag + "conv")
    (d_q, d_k, d_v, d_sg), carried_attn = _sb_bwd(u, saved["o_sb"], saved["sb_after"], d_ys[2], tag + "stickbreak",
                                                  rider=rider)
    du = jnp.concatenate([d_pv, d_pg, d_cx, d_cgb, d_cgc, d_cg, d_q, d_k, d_v, d_sg] + list(d_logits), axis=1)
    grads = dict(w_branch=g_w_branch, w_out=g_w_out, post_g=g_post, pool_w=g_pool_w, pool_scale=g_pool_scale,
                 conv_w=g_conv_w, conv_b=g_conv_b)
    rider = before_dw(grads) if before_dw else None
    g_w_in = _matmul(saved["h"], du, "tn", F32, tag + "in_proj_dw", tk=2048, rider=rider)
    g_w_in, carried_dw = g_w_in if rider else (g_w_in, [])
    rider = late(g_w_in) if late else None
    dh = _matmul(du, p["w_in"], "nt", BF16, tag + "in_proj_dx", tk=2048, rider=rider)
    dh, carried_dx = dh if rider else (dh, [])
    dx, g_pre = _rms_bwd(saved["x"], p["pre_g"], dh, dy, F32, tag + "pre_norm")
    grads.update(w_in=g_w_in, pre_g=g_pre)
    return dx, grads, carried_attn, carried_dw, carried_dx


SMALL_ORDER = ["pre_g", "pool_w", "pool_scale", "conv_w", "conv_b", "post_g"]


def _pack_small(per_layer):
    parts, spans, at = [], {}, 0
    for name in SMALL_ORDER:
        a = jnp.stack([per_layer[l][name] for l in range(DEPTH)]).reshape(-1, LANE)
        parts.append(a)
        spans[name] = (at, a.shape[0])
        at += a.shape[0]
    return jnp.concatenate(parts, axis=0), spans


def kernel(x, pre_norm_g, w_in, pool_w, pool_scale, conv_w, conv_b, w_branch, w_out, post_norm_g, loss_target, m_pre_norm_g, m_w_in, m_pool_w, m_pool_scale, m_conv_w, m_conv_b, m_w_branch, m_w_out, m_post_norm_g, v_pre_norm_g, v_w_in, v_pool_w, v_pool_scale, v_conv_w, v_conv_b, v_w_branch, v_w_out, v_post_norm_g):
    mx, my, mc = _mesh_position()
    chip = 2 * mx + my
    core = mc.astype(jnp.int32).reshape(1)
    pos = jnp.stack([mc, chip]).astype(jnp.int32)

    names = ["w_in", "w_branch", "w_out"]
    given = dict(w_in=w_in, w_branch=w_branch, w_out=w_out)
    in_place = [[_cast_shard(given[n], l, WEIGHT_LAYOUT[i][0], pos, f"cast_{n}{l}") for i, n in enumerate(names)]
                for l in range(DEPTH)]
    (w_in_0,), conv_w_by_chip = _gather_weights(in_place[0][:1], WEIGHT_LAYOUT[:1], conv_w)
    gathered = [w_in_0] + in_place[0][1:]
    conv_w_f = conv_w_by_chip.transpose(1, 2, 0, 3).reshape(DEPTH, 3, WIDTH)
    pool_w_b = _cast_bf16(pool_w, "cast_pool_w")

    def layer_params(l, big):
        return dict(pre_g=pre_norm_g[l:l + 1], post_g=post_norm_g[l:l + 1], w_in=big[0], w_branch=big[1],
                    w_out=big[2], pool_w=pool_w_b[l], pool_scale=pool_scale[l:l + 1], conv_w=conv_w_f[l],
                    conv_b=conv_b[l:l + 1])

    act = x[0]
    params, saved = [], []
    for l in range(DEPTH):
        rider = _weight_gather_rider(in_place[l + 1], WEIGHT_LAYOUT) if l + 1 < DEPTH else None
        proj_rider = _weight_gather_rider(gathered[1:], WEIGHT_LAYOUT[1:]) if l == 0 else None
        out, sv, gathered, layer_p = _layer_fwd(act, layer_params(l, gathered), l, rider, proj_rider)
        params.append(layer_p)
        saved.append(sv)
        if l < DEPTH - 1:
            act = _resid_out(act, out, params[l]["post_g"], None, f"l{l}_resid")
    dy, loss_part = _resid_out(act, saved[-1]["out"], params[-1]["post_g"], loss_target[0], "loss_head")

    split_axis = dict(w_in=0, w_branch=1, w_out=1)
    shard_axis = dict(w_in=1, w_branch=2, w_out=0)
    grads = [None] * DEPTH
    chip_sums = [dict() for _ in range(DEPTH)]
    by_chip = [dict() for _ in range(DEPTH)]

    def reduce_in_chip(l, which, g):
        items = [(g[n], split_axis[n]) for n in which]
        from_sibling = _swap_halves(items, f"swap_grad_halves{l}_{which[0]}")
        for n, (a, axis), r in zip(which, items, from_sibling):
            chip_sums[l][n] = _chip_sum(a, r, axis, core, f"chip_sum{l}_{n}")

    def exchange_rider(keys):
        return _grad_exchange_rider([(chip_sums[l][n], shard_axis[n]) for l, n in keys])

    waiting = []
    for l in reversed(range(DEPTH)):
        sent_early, sent_late = list(waiting) + [(l, "w_branch"), (l, "w_out")], [(l, "w_in")]
        waiting_items = [(grads[ll][n], split_axis[n]) for ll, n in waiting]

        def early(g_w_branch, g_w_out, from_sibling, l=l, keys=sent_early, above=tuple(waiting), items=waiting_items):
            for (ll, n), (a, axis), r in zip(above, items, from_sibling):
                chip_sums[ll][n] = _chip_sum(a, r, axis, core, f"chip_sum{ll}_{n}")
            reduce_in_chip(l, ["w_branch", "w_out"], dict(w_branch=g_w_branch, w_out=g_w_out))
            return exchange_rider(keys)

        def late(g_w_in, l=l, keys=sent_late):
            reduce_in_chip(l, ["w_in"], dict(w_in=g_w_in))
            return exchange_rider(keys)

        def before_dw(partial, l=l):
            layers = [dict(partial, pre_g=jnp.zeros_like(pre_norm_g[:1])) if ll == l else grads[ll]
                      for ll in range(DEPTH)]
            return _small_gather_rider(_pack_small(layers)[0])

        if l == DEPTH - 1:
            dy, grads[l], _, _, _ = _layer_bwd(dy, params[l], saved[l], l)
            waiting = [(l, n) for n in names]
        else:
            dy, grads[l], got_early, got_dw, got_late = _layer_bwd(
                dy, params[l], saved[l], l, _swap_rider(waiting_items), early, before_dw if l == 0 else None, late)
            for (ll, n), r in zip(sent_early + sent_late, list(got_early) + list(got_late)):
                by_chip[ll][n] = r
            if l == 0:
                small_all = got_dw[0]
            waiting = []
    assert not waiting
    grad_x = dy[None]
    _, spans = _pack_small(grads)
    late_small = jnp.concatenate([grads[0]["pre_g"].reshape(-1, LANE), jnp.broadcast_to(loss_part, (8, LANE))])
    bufs = []
    for n in names:
        dest = None
        for l in range(DEPTH):
            dest = _sum_chips(by_chip[l][n], chip_sums[l][n], shard_axis[n], split_axis[n], pos, dest, l,
                              f"sum_chips{l}_{n}")
        bufs.append((dest, split_axis[n]))
    (g_w_in, g_w_branch, g_w_out), late_all = _share_halves(bufs, late_small, "share_grad_halves")

    late_sum = _sum_slots(late_all, F32, "sum_late_small")
    n_gain = late_small.shape[0] - 8
    loss = late_sum[n_gain, 0]
    small_sum = _sum_slots(small_all, F32, "sum_small")
    at, _ = spans["pre_g"]
    small_sum = jnp.concatenate([small_sum[:at], late_sum[:n_gain], small_sum[at + n_gain:]])
    small = {}
    for name, like in (("pre_g", pre_norm_g), ("pool_w", pool_w), ("pool_scale", pool_scale), ("conv_b", conv_b),
                       ("post_g", post_norm_g)):
        at, n = spans[name]
        small[name] = small_sum[at:at + n].reshape(like.shape)
    at, n = spans["conv_w"]
    g_conv_w_full = small_sum[at:at + n].reshape(DEPTH, 3, WIDTH)
    g_conv_w = lax.dynamic_slice_in_dim(g_conv_w_full, chip * conv_w.shape[2], conv_w.shape[2], axis=2)

    g = dict(pre_norm_g=small["pre_g"], w_in=g_w_in, pool_w=small["pool_w"], pool_scale=small["pool_scale"],
             conv_w=g_conv_w, conv_b=small["conv_b"], w_branch=g_w_branch, w_out=g_w_out, post_norm_g=small["post_g"])
    w = dict(pre_norm_g=pre_norm_g, w_in=w_in, pool_w=pool_w, pool_scale=pool_scale, conv_w=conv_w, conv_b=conv_b,
             w_branch=w_branch, w_out=w_out, post_norm_g=post_norm_g)
    m = dict(pre_norm_g=m_pre_norm_g, w_in=m_w_in, pool_w=m_pool_w, pool_scale=m_pool_scale, conv_w=m_conv_w,
             conv_b=m_conv_b, w_branch=m_w_branch, w_out=m_w_out, post_norm_g=m_post_norm_g)
    v = dict(pre_norm_g=v_pre_norm_g, w_in=v_w_in, pool_w=v_pool_w, pool_scale=v_pool_scale, conv_w=v_conv_w,
             conv_b=v_conv_b, w_branch=v_w_branch, w_out=v_w_out, post_norm_g=v_post_norm_g)
    order = ["pre_norm_g", "w_in", "pool_w", "pool_scale", "conv_w", "conv_b", "w_branch", "w_out", "post_norm_g"]
    upd = {n: _adamw(w[n], g[n], m[n], v[n], "adamw_" + n) for n in order}
    return (loss, grad_x, *[g[n] for n in order], *[upd[n][0] for n in order], *[upd[n][1] for n in order],
            *[upd[n][2] for n in order])
```

```python
import functools

import jax
import jax.numpy as jnp
from jax import lax
from jax.experimental import pallas as pl
from jax.experimental.pallas import tpu as pltpu

F32 = jnp.float32
BF16 = jnp.bfloat16
MESH = pl.DeviceIdType.MESH
ANY = pl.BlockSpec(memory_space=pl.ANY)

DEPTH = 2
D_MODEL = 1024
WIDTH = 512
N_IN = 8192
N_CHIPS = 4
HEAD_DIM = 64
RMS_EPS = 1e-6
POOL_HALO = 16
CONV_HALO = 16
LANE = 128
COL_POOL_V, COL_POOL_G = 0, 4
COL_CONV_X, COL_CONV_GB, COL_CONV_GC, COL_CONV_G = 8, 12, 16, 20
COL_SB_Q, COL_SB_K, COL_SB_V, COL_SB_G = 24, 28, 32, 36
COL_MERGE_1024 = 5

ADAM_LR, ADAM_B1, ADAM_B2, ADAM_EPS, ADAM_WD, ADAM_STEP = 0.001, 0.9, 0.999, 1e-08, 0.01, 10

NN = (((1,), (0,)), ((), ()))
NT = (((1,), (1,)), ((), ()))
TN = (((0,), (0,)), ((), ()))


def _sigmoid(x):
    return 1.0 / (1.0 + jnp.exp(-x))


def _silu_and_grad(x):
    s = _sigmoid(x)
    return x * s, s * (1.0 + x * (1.0 - s))


def _dot(a, b, dims):
    return lax.dot_general(a, b, dims, preferred_element_type=F32)


def _matmul(a, b, mode, out_dtype, name, tm=1024, tn=1024, tk=1024, b_lead=(), rider=None):
    b_shape = b.shape[len(b_lead):]
    if mode == "nn":
        (m, k), (k2, n) = a.shape, b_shape
    elif mode == "nt":
        (m, k), (n, k2) = a.shape, b_shape
    else:
        (k, m), (k2, n) = a.shape, b_shape
    assert k == k2 and a.dtype == BF16 and b.dtype == BF16
    tm, tn, tk = min(tm, m), min(tn, n), min(tk, k)
    assert m % tm == 0 and n % tn == 0 and k % tk == 0
    nk = k // tk
    dims = {"nn": NN, "nt": NT, "tn": TN}[mode]

    grid = (m // tm, n // tn, nk)

    def at_step(step):
        return functools.reduce(jnp.logical_and, [pl.program_id(d) == s for d, s in enumerate(step)])

    def body(*refs):
        (a_ref, b_ref, o_ref, *scratch), riding = _split_refs(refs, 2, 1, 1 if nk > 1 else 0, rider)
        _ride(rider, "start", at_step((0, 0, 0)), riding)
        _ride(rider, "middle", at_step(((3 * grid[0]) // 4, 0, 0)), riding)
        compute(a_ref, b_ref, o_ref, scratch)
        _ride(rider, "finish", at_step([g - 1 for g in grid]), riding)

    def compute(a_ref, b_ref, o_ref, scratch):
        p = _dot(a_ref[...], b_ref[...], dims)
        if nk == 1:
            o_ref[...] = p.astype(o_ref.dtype)
        else:
            acc = scratch[0]
            kk = pl.program_id(2)

            @pl.when(kk == 0)
            def _():
                acc[...] = p

            @pl.when(jnp.logical_and(kk > 0, kk < nk - 1))
            def _():
                acc[...] += p

            @pl.when(kk == nk - 1)
            def _():
                o_ref[...] = (acc[...] + p).astype(o_ref.dtype)

    if mode == "tn":
        a_spec = pl.BlockSpec((tk, tm), lambda i, j, kk: (kk, i))
    else:
        a_spec = pl.BlockSpec((tm, tk), lambda i, j, kk: (i, kk))
    squeezed = (None,) * len(b_lead)
    if mode == "nt":
        b_spec = pl.BlockSpec(squeezed + (tn, tk), lambda i, j, kk: (*b_lead, j, kk))
    else:
        b_spec = pl.BlockSpec(squeezed + (tk, tn), lambda i, j, kk: (*b_lead, kk, j))
    extra = _rider_call_args(rider, 2, 1)
    outs = pl.pallas_call(
        body, name=name, grid=grid,
        in_specs=[a_spec, b_spec] + extra["in_specs"],
        out_specs=[pl.BlockSpec((tm, tn), lambda i, j, kk: (i, j))] + extra["out_specs"],
        out_shape=[jax.ShapeDtypeStruct((m, n), out_dtype)] + extra["out_shape"],
        input_output_aliases=extra["aliases"],
        scratch_shapes=([pltpu.VMEM((tm, tn), F32)] if nk > 1 else []) + extra["scratch"],
        compiler_params=pltpu.CompilerParams(dimension_semantics=("arbitrary",) * 3 if rider else
                                             ("parallel", "parallel", "arbitrary")),
    )(a, b, *extra["inputs"])
    return (outs[0], outs[1:]) if rider else outs[0]


def _rms_fwd(x, g, name, ts=512):
    s, d = x.shape

    def body(x_ref, g_ref, h_ref):
        xv = x_ref[...]
        r = lax.rsqrt(jnp.mean(xv * xv, axis=-1, keepdims=True) + RMS_EPS)
        h_ref[...] = (xv * r * g_ref[...]).astype(BF16)

    return pl.pallas_call(
        body, name=name, grid=(s // ts,),
        in_specs=[pl.BlockSpec((ts, d), lambda i: (i, 0)), pl.BlockSpec((1, d), lambda i: (0, 0))],
        out_specs=pl.BlockSpec((ts, d), lambda i: (i, 0)),
        out_shape=jax.ShapeDtypeStruct((s, d), BF16),
    )(x, g)


def _rms_bwd(xin, g, dh, resid, out_dtype, name, ts=512):
    s, d = xin.shape
    has_resid = resid is not None

    def body(*refs):
        if has_resid:
            x_ref, g_ref, dh_ref, res_ref, dx_ref, dg_ref = refs
        else:
            x_ref, g_ref, dh_ref, dx_ref, dg_ref = refs
        xv = x_ref[...]
        dhv = dh_ref[...].astype(F32)
        r = lax.rsqrt(jnp.mean(xv * xv, axis=-1, keepdims=True) + RMS_EPS)
        nrm = xv * r
        dn = dhv * g_ref[...]
        dx = r * (dn - nrm * jnp.mean(dn * nrm, axis=-1, keepdims=True))
        if has_resid:
            dx = dx + res_ref[...]
        dx_ref[...] = dx.astype(dx_ref.dtype)
        part = jnp.sum(dhv * nrm, axis=0, keepdims=True)

        @pl.when(pl.program_id(0) == 0)
        def _():
            dg_ref[...] = part

        @pl.when(pl.program_id(0) > 0)
        def _():
            dg_ref[...] += part

    row = pl.BlockSpec((ts, d), lambda i: (i, 0))
    vec = pl.BlockSpec((1, d), lambda i: (0, 0))
    ins = [xin, g, dh] + ([resid] if has_resid else [])
    return pl.pallas_call(
        body, name=name, grid=(s // ts,),
        in_specs=[row, vec, row] + ([row] if has_resid else []),
        out_specs=[row, vec],
        out_shape=[jax.ShapeDtypeStruct((s, d), out_dtype), jax.ShapeDtypeStruct((1, d), F32)],
        compiler_params=pltpu.CompilerParams(dimension_semantics=("arbitrary",)),
    )(*ins)


def _resid_out(x, out, g, target, name, ts=512):
    s, d = x.shape
    has_loss = target is not None

    def body(*refs):
        if has_loss:
            x_ref, o_ref, g_ref, t_ref, dy_ref, loss_ref = refs
        else:
            x_ref, o_ref, g_ref, y_ref = refs
        ov = o_ref[...]
        r = lax.rsqrt(jnp.mean(ov * ov, axis=-1, keepdims=True) + RMS_EPS)
        yv = x_ref[...] + ov * r * g_ref[...]
        if not has_loss:
            y_ref[...] = yv
            return
        err = yv - t_ref[...]
        dy_ref[...] = err * (1.0 / d)
        part = jnp.sum(jnp.sum(err * err, axis=-1, keepdims=True), axis=0, keepdims=True) * (0.5 / d)
        part = jnp.broadcast_to(part, (1, LANE))

        @pl.when(pl.program_id(0) == 0)
        def _():
            loss_ref[...] = part

        @pl.when(pl.program_id(0) > 0)
        def _():
            loss_ref[...] += part

    row = pl.BlockSpec((ts, d), lambda i: (i, 0))
    vec = pl.BlockSpec((1, d), lambda i: (0, 0))
    if has_loss:
        return pl.pallas_call(
            body, name=name, grid=(s // ts,),
            in_specs=[row, row, vec, row],
            out_specs=[row, pl.BlockSpec((1, LANE), lambda i: (0, 0))],
            out_shape=[jax.ShapeDtypeStruct((s, d), F32), jax.ShapeDtypeStruct((1, LANE), F32)],
            compiler_params=pltpu.CompilerParams(dimension_semantics=("arbitrary",)),
        )(x, out, g, target)
    return pl.pallas_call(
        body, name=name, grid=(s // ts,),
        in_specs=[row, row, vec], out_specs=row,
        out_shape=jax.ShapeDtypeStruct((s, d), F32),
    )(x, out, g)


def _rows_before(ref, start, n, halo):
    if start == 0:
        return jnp.concatenate([jnp.zeros((halo, ref.shape[1]), F32), ref[0:n, :].astype(F32)], axis=0)
    return ref[start - halo:start + n, :].astype(F32)


def _rows_after(ref, start, n, halo):
    if start + n == ref.shape[0]:
        return jnp.concatenate([ref[start:start + n, :].astype(F32), jnp.zeros((halo, ref.shape[1]), F32)], axis=0)
    return ref[start:start + n + halo, :].astype(F32)


def _pick_window(group, s2, s4, s8, s16):
    return jnp.where(group == 0, s2, jnp.where(group == 1, s4, jnp.where(group == 2, s8, s16)))


def _trailing_sums(ext, group):
    s2 = ext + pltpu.roll(ext, 1, 0)
    s4 = s2 + pltpu.roll(s2, 2, 0)
    s8 = s4 + pltpu.roll(s4, 4, 0)
    s16 = s8 + pltpu.roll(s8, 8, 0)
    return _pick_window(group, s2, s4, s8, s16)


def _leading_sums(ext, group):
    n = ext.shape[0]
    s2 = ext + pltpu.roll(ext, n - 1, 0)
    s4 = s2 + pltpu.roll(s2, n - 2, 0)
    s8 = s4 + pltpu.roll(s4, n - 4, 0)
    s16 = s8 + pltpu.roll(s8, n - 8, 0)
    return _pick_window(group, s2, s4, s8, s16)


def _window_count(start, n, group):
    pos = start + lax.broadcasted_iota(jnp.int32, (n, LANE), 0)
    return jnp.minimum(pos + 1, 2 << group).astype(F32)


def _pooled(v_ref, start, n, group):
    ext = _rows_before(v_ref, start, n, POOL_HALO)
    sums = _trailing_sums(ext, group)[POOL_HALO:, :]
    return sums / _window_count(start, n, group) - ext[POOL_HALO:, :]


def _pool_fwd(u, pool_w, pool_scale, name, ts=512):
    s = u.shape[0]

    def body(v_ref, gate_ref, w_ref, sc_ref, y_ref):
        group = pl.program_id(0)
        for c in range(s // ts):
            a = c * ts
            pooled = _pooled(v_ref, a, ts, group)
            mixed = _dot(pooled.astype(BF16), w_ref[...], NN)
            gate = gate_ref[a:a + ts, :].astype(F32)
            y_ref[a:a + ts, :] = (mixed * sc_ref[...] * (gate * _sigmoid(gate))).astype(BF16)

    col = lambda base: pl.BlockSpec((s, LANE), lambda g: (0, base + g))
    return pl.pallas_call(
        body, name=name, grid=(4,),
        in_specs=[col(COL_POOL_V), col(COL_POOL_G),
                  pl.BlockSpec((None, LANE, LANE), lambda g: (g, 0, 0)),
                  pl.BlockSpec((1, LANE), lambda g: (0, g))],
        out_specs=pl.BlockSpec((s, LANE), lambda g: (0, g)),
        out_shape=jax.ShapeDtypeStruct((s, WIDTH), BF16),
    )(u, u, pool_w, pool_scale)


def _pool_bwd(u, dy, pool_w, pool_scale, name, ts=512):
    s = u.shape[0]

    def body(v_ref, gate_ref, dy_ref, w_ref, sc_ref, dv_ref, dgate_ref, dw_ref, dsc_ref):
        group = pl.program_id(0)
        w = w_ref[...]
        scale = sc_ref[...]
        dw = jnp.zeros((LANE, LANE), F32)
        dsc = jnp.zeros((1, LANE), F32)
        for c in range(s // ts):
            a = c * ts
            n_ext = ts + POOL_HALO
            gate_e = _rows_after(gate_ref, a, ts, POOL_HALO)
            dy_e = _rows_after(dy_ref, a, ts, POOL_HALO)
            silu_e, dsilu_e = _silu_and_grad(gate_e)
            dms_e = dy_e * silu_e
            dm_e = (dms_e * scale).astype(BF16)
            dpool_e = _dot(dm_e, w, NT)
            spread = _leading_sums(dpool_e / _window_count(a, n_ext, group), group)
            dv_ref[a:a + ts, :] = (spread[0:ts, :] - dpool_e[0:ts, :]).astype(BF16)
            pooled = _pooled(v_ref, a, ts, group).astype(BF16)
            mixed = _dot(pooled, w, NN)
            dgate_ref[a:a + ts, :] = (dy_e[0:ts, :] * mixed * scale * dsilu_e[0:ts, :]).astype(BF16)
            dsc = dsc + jnp.sum(dms_e[0:ts, :] * mixed, axis=0, keepdims=True)
            dw = dw + _dot(pooled, dm_e[0:ts, :], TN)
        dw_ref[...] = dw
        dsc_ref[...] = dsc

    col = lambda base: pl.BlockSpec((s, LANE), lambda g: (0, base + g))
    out_col = pl.BlockSpec((s, LANE), lambda g: (0, g))
    return pl.pallas_call(
        body, name=name, grid=(4,),
        in_specs=[col(COL_POOL_V), col(COL_POOL_G), out_col,
                  pl.BlockSpec((None, LANE, LANE), lambda g: (g, 0, 0)),
                  pl.BlockSpec((1, LANE), lambda g: (0, g))],
        out_specs=[out_col, out_col,
                   pl.BlockSpec((None, LANE, LANE), lambda g: (g, 0, 0)),
                   pl.BlockSpec((1, LANE), lambda g: (0, g))],
        out_shape=[jax.ShapeDtypeStruct((s, WIDTH), BF16), jax.ShapeDtypeStruct((s, WIDTH), BF16),
                   jax.ShapeDtypeStruct((4, LANE, LANE), F32), jax.ShapeDtypeStruct((1, WIDTH), F32)],
    )(u, u, dy, pool_w, pool_scale)


def _conv_taps(x_ref, gc_ref, start, n):
    z_ext = _rows_before(gc_ref, start, n, CONV_HALO) * _rows_before(x_ref, start, n, CONV_HALO)
    z0 = z_ext[CONV_HALO:, :]
    z1 = pltpu.roll(z_ext, 1, 0)[CONV_HALO:, :]
    z2 = pltpu.roll(z_ext, 2, 0)[CONV_HALO:, :]
    return z0, z1, z2


def _conv_fwd(u, conv_w, conv_b, name, ts=512):
    s = u.shape[0]

    def body(x_ref, gb_ref, gc_ref, g_ref, w_ref, b_ref, y_ref):
        w0, w1, w2 = w_ref[0:1, :], w_ref[1:2, :], w_ref[2:3, :]
        for c in range(s // ts):
            a = c * ts
            z0, z1, z2 = _conv_taps(x_ref, gc_ref, a, ts)
            y = w2 * z0 + w1 * z1 + w0 * z2 + b_ref[...]
            gate = g_ref[a:a + ts, :].astype(F32)
            y_ref[a:a + ts, :] = (gb_ref[a:a + ts, :].astype(F32) * y * (gate * _sigmoid(gate))).astype(BF16)

    col = lambda base: pl.BlockSpec((s, LANE), lambda j: (0, base + j))
    return pl.pallas_call(
        body, name=name, grid=(4,),
        in_specs=[col(COL_CONV_X), col(COL_CONV_GB), col(COL_CONV_GC), col(COL_CONV_G),
                  pl.BlockSpec((3, LANE), lambda j: (0, j)), pl.BlockSpec((1, LANE), lambda j: (0, j))],
        out_specs=pl.BlockSpec((s, LANE), lambda j: (0, j)),
        out_shape=jax.ShapeDtypeStruct((s, WIDTH), BF16),
    )(u, u, u, u, conv_w, conv_b)


def _conv_bwd(u, dy, conv_w, conv_b, name, ts=512):
    s = u.shape[0]

    def body(x_ref, gb_ref, gc_ref, g_ref, dy_ref, w_ref, b_ref,
             dx_ref, dgb_ref, dgc_ref, dg_ref, dw_ref, db_ref):
        w0, w1, w2 = w_ref[0:1, :], w_ref[1:2, :], w_ref[2:3, :]
        acc = [jnp.zeros((1, LANE), F32) for _ in range(4)]
        for c in range(s // ts):
            a = c * ts
            n_ext = ts + CONV_HALO
            gate_e = _rows_after(g_ref, a, ts, CONV_HALO)
            silu_e, dsilu_e = _silu_and_grad(gate_e)
            dy_e = _rows_after(dy_ref, a, ts, CONV_HALO)
            gb_e = _rows_after(gb_ref, a, ts, CONV_HALO)
            dyy_e = dy_e * silu_e * gb_e
            dz = (w2 * dyy_e + w1 * pltpu.roll(dyy_e, n_ext - 1, 0) + w0 * pltpu.roll(dyy_e, n_ext - 2, 0))[0:ts, :]
            z0, z1, z2 = _conv_taps(x_ref, gc_ref, a, ts)
            yb = w2 * z0 + w1 * z1 + w0 * z2 + b_ref[...]
            dyv = dy_e[0:ts, :]
            dyy = dyy_e[0:ts, :]
            dg_ref[a:a + ts, :] = (dyv * gb_e[0:ts, :] * yb * dsilu_e[0:ts, :]).astype(BF16)
            dgb_ref[a:a + ts, :] = (dyv * silu_e[0:ts, :] * yb).astype(BF16)
            dx_ref[a:a + ts, :] = (dz * gc_ref[a:a + ts, :].astype(F32)).astype(BF16)
            dgc_ref[a:a + ts, :] = (dz * x_ref[a:a + ts, :].astype(F32)).astype(BF16)
            for i, term in enumerate((dyy * z2, dyy * z1, dyy * z0, dyy)):
                acc[i] = acc[i] + jnp.sum(term, axis=0, keepdims=True)
        dw_ref[0:1, :] = acc[0]
        dw_ref[1:2, :] = acc[1]
        dw_ref[2:3, :] = acc[2]
        db_ref[...] = acc[3]

    col = lambda base: pl.BlockSpec((s, LANE), lambda j: (0, base + j))
    out_col = pl.BlockSpec((s, LANE), lambda j: (0, j))
    big = jax.ShapeDtypeStruct((s, WIDTH), BF16)
    return pl.pallas_call(
        body, name=name, grid=(4,),
        in_specs=[col(COL_CONV_X), col(COL_CONV_GB), col(COL_CONV_GC), col(COL_CONV_G), out_col,
                  pl.BlockSpec((3, LANE), lambda j: (0, j)), pl.BlockSpec((1, LANE), lambda j: (0, j))],
        out_specs=[out_col, out_col, out_col, out_col,
                   pl.BlockSpec((3, LANE), lambda j: (0, j)), pl.BlockSpec((1, LANE), lambda j: (0, j))],
        out_shape=[big, big, big, big,
                   jax.ShapeDtypeStruct((3, WIDTH), F32), jax.ShapeDtypeStruct((1, WIDTH), F32)],
    )(u, u, u, u, dy, conv_w, conv_b)


LOG2_E = 1.4426950408889634
LN_2 = 0.6931471805599453


def _sb_scores(q_h, k_blk, valid, later_mat, carry):
    z = _dot(q_h, k_blk, NT)
    neg_z = -z
    soft = jnp.log(1.0 + jnp.exp2(jnp.minimum(z, neg_z))) * LOG2_E
    log_keep = jnp.minimum(neg_z, 0.0) - soft
    log_beta = log_keep + z
    if valid is not None:
        log_keep = jnp.where(valid, log_keep, 0.0)
    later = _dot(log_keep.astype(BF16), later_mat, NN) + carry
    return log_keep, log_beta, later


def _masked(valid, x):
    return x if valid is None else jnp.where(valid, x, 0.0)


def _diagonal_masks(tq, tk):
    r = lax.broadcasted_iota(jnp.int32, (tq, tk), 0)
    cidx = lax.broadcasted_iota(jnp.int32, (tq, tk), 1)
    return [cidx + d * tk < r for d in range(tq // tk)]


def _triangle(tk, op):
    r = lax.broadcasted_iota(jnp.int32, (tk, tk), 0)
    cidx = lax.broadcasted_iota(jnp.int32, (tk, tk), 1)
    return op(r, cidx).astype(BF16)


def _split_refs(refs, n_in, n_out, n_scratch, rider):
    r_in = len(rider.inputs) if rider else 0
    r_out = len(rider.out_shape) if rider else 0
    a, b = n_in + r_in, n_in + r_in + n_out + r_out
    own = refs[:n_in] + refs[a:a + n_out] + refs[b:b + n_scratch]
    return own, (refs[n_in:a], refs[a + n_out:b], refs[b + n_scratch:])


def _rider_call_args(rider, n_in, n_out):
    if rider is None:
        return dict(in_specs=[], out_specs=[], out_shape=[], aliases={}, scratch=[], inputs=[])
    return dict(in_specs=[ANY] * len(rider.inputs), out_specs=[ANY] * len(rider.out_shape),
                out_shape=list(rider.out_shape), scratch=list(rider.scratch), inputs=list(rider.inputs),
                aliases={n_in + a: n_out + b for a, b in rider.aliases.items()})


def _ride(rider, phase, when, parts):
    fn = getattr(rider, phase) if rider else None
    if fn is not None:
        pl.when(when)(lambda: fn(*parts))


def _sb_fwd(u, name, t=512, tk=256, pairs=4, rider=None):
    s = u.shape[0]
    assert s // tk <= LANE and 4 % pairs == 0 and t % tk == 0
    scale = HEAD_DIM ** -0.5
    nh = 2 * pairs
    wide = pairs * LANE
    ratio = t // tk
    groups, nq = 4 // pairs, s // t

    def body(*refs):
        own, riding = _split_refs(refs, 4, 3, 4, rider)
        q_ref, k_ref, v_ref, g_ref, o_ref, y_ref, after_ref, kb_ref, vb_ref, acc_ref, carry_ref = own
        grp = pl.program_id(0)
        i = pl.program_id(1)
        _ride(rider, "start", jnp.logical_and(grp == 0, i == 0), riding)
        _ride(rider, "middle", jnp.logical_and(grp == groups - 1, i == (3 * nq) // 4), riding)

        @pl.when(i == 0)
        def _():
            kb_ref[...] = k_ref[...].astype(BF16)
            vb_ref[...] = v_ref[...].astype(BF16)

        lane = lax.broadcasted_iota(jnp.int32, (t, LANE), 1)
        first = lane < HEAD_DIM
        after_ref[...] = jnp.zeros_like(after_ref)
        qv = q_ref[...].astype(F32) * (scale * LOG2_E)
        q_heads = []
        for p in range(pairs):
            qp = qv[:, p * LANE:(p + 1) * LANE]
            q_heads += [jnp.where(first, qp, 0.0).astype(BF16), jnp.where(first, 0.0, qp).astype(BF16)]
        later_mat = _triangle(tk, lambda r, cidx: r > cidx)
        acc_ref[...] = jnp.zeros_like(acc_ref)
        carry_ref[...] = jnp.zeros_like(carry_ref)

        def block(kb, valid, lo=0):
            rows = pl.ds(pl.multiple_of(kb * tk, tk), tk)
            k_blk = kb_ref[rows, :]
            v_blk = vb_ref[rows, :]
            carries = [carry_ref[h, lo:, :] for h in range(nh)]
            afters = [after_ref[lo:, h * LANE:(h + 1) * LANE] for h in range(nh)]
            accs = [acc_ref[h, lo:, :] for h in range(nh)]
            outs = []
            for h in range(nh):
                cols = slice((h // 2) * LANE, (h // 2 + 1) * LANE)
                log_keep, log_beta, later = _sb_scores(q_heads[h][lo:], k_blk[:, cols], valid, later_mat, carries[h])
                a = _masked(valid, jnp.exp2(log_beta + later))
                outs.append((accs[h] + _dot(a.astype(BF16), v_blk[:, cols], NN),
                             carries[h] + jnp.sum(log_keep, axis=1, keepdims=True),
                             jnp.where(lane[lo:] == kb, carries[h], afters[h])))
            for h in range(nh):
                acc_ref[h, lo:, :] = outs[h][0]
                carry_ref[h, lo:, :] = outs[h][1]
                after_ref[lo:, h * LANE:(h + 1) * LANE] = outs[h][2]

        def step(j, _):
            block(ratio * i - 1 - j, None)
            return 0

        masks = _diagonal_masks(t, tk)
        for d in reversed(range(ratio)):
            block(ratio * i + d, masks[d][d * tk:], d * tk)
        lax.fori_loop(0, ratio * i, step, 0)
        for p in range(pairs):
            cols = slice(p * LANE, (p + 1) * LANE)
            o = jnp.where(first, acc_ref[2 * p], acc_ref[2 * p + 1])
            o_ref[:, cols] = o
            gate = g_ref[:, cols].astype(F32)
            y_ref[:, cols] = (o * gate * _sigmoid(gate)).astype(BF16)
        _ride(rider, "finish", jnp.logical_and(grp == groups - 1, i == nq - 1), riding)

    blk = lambda base: pl.BlockSpec((t, wide), lambda g, i: (i, base // pairs + g))
    full = lambda base: pl.BlockSpec((s, wide), lambda g, i: (0, base // pairs + g))
    out_blk = pl.BlockSpec((t, wide), lambda g, i: (i, g))
    extra = _rider_call_args(rider, 4, 3)
    outs = pl.pallas_call(
        body, name=name, grid=(groups, nq),
        in_specs=[blk(COL_SB_Q), full(COL_SB_K), full(COL_SB_V), blk(COL_SB_G)] + extra["in_specs"],
        out_specs=[out_blk, out_blk, pl.BlockSpec((t, nh * LANE), lambda g, i: (i, g))] + extra["out_specs"],
        out_shape=[jax.ShapeDtypeStruct((s, WIDTH), F32), jax.ShapeDtypeStruct((s, WIDTH), BF16),
                   jax.ShapeDtypeStruct((s, 8 * LANE), F32)] + extra["out_shape"],
        input_output_aliases=extra["aliases"],
        scratch_shapes=[pltpu.VMEM((s, wide), BF16), pltpu.VMEM((s, wide), BF16),
                        pltpu.VMEM((nh, t, LANE), F32), pltpu.VMEM((nh, t, 1), F32)] + extra["scratch"],
        compiler_params=pltpu.CompilerParams(dimension_semantics=("arbitrary", "arbitrary")),
    )(u, u, u, u, *extra["inputs"])
    return outs[:3], outs[3:]


def _sb_bwd(u, o, after, dy, name, t=512, tk=256, pairs=2, rider=None):
    s = u.shape[0]
    nq = s // t
    scale = HEAD_DIM ** -0.5
    nh = 2 * pairs
    wide = pairs * LANE
    ratio = t // tk
    groups = 4 // pairs

    def body(*refs):
        own, riding = _split_refs(refs, 7, 4, 6, rider)
        (q_ref, k_ref, v_ref, g_ref, o_ref, after_ref, dy_ref, dq_ref, dk_ref, dv_ref, dg_ref,
         kb_ref, vb_ref, dk_acc, dv_acc, dq_acc, carry_ref) = own
        grp = pl.program_id(0)
        i = pl.program_id(1)
        _ride(rider, "start", jnp.logical_and(grp == 0, i == 0), riding)

        @pl.when(i == 0)
        def _():
            kb_ref[...] = k_ref[...].astype(BF16)
            vb_ref[...] = v_ref[...].astype(BF16)
            dk_acc[...] = jnp.zeros_like(dk_acc)
            dv_acc[...] = jnp.zeros_like(dv_acc)

        lane = lax.broadcasted_iota(jnp.int32, (t, LANE), 1)
        first = lane < HEAD_DIM
        gate = g_ref[...].astype(F32)
        silu, dsilu = _silu_and_grad(gate)
        dyv = dy_ref[...]
        do = dyv * silu
        dg_ref[...] = (dyv * o_ref[...] * dsilu).astype(BF16)
        qv = q_ref[...].astype(F32) * (scale * LOG2_E)
        do_heads, q_heads = [], []
        for p in range(pairs):
            cols = slice(p * LANE, (p + 1) * LANE)
            do_heads += [jnp.where(first, do[:, cols], 0.0).astype(BF16), jnp.where(first, 0.0, do[:, cols]).astype(BF16)]
            q_heads += [jnp.where(first, qv[:, cols], 0.0).astype(BF16), jnp.where(first, 0.0, qv[:, cols]).astype(BF16)]
        later_mat = _triangle(tk, lambda r, cidx: r > cidx)
        before_mat = _triangle(tk, lambda r, cidx: r < cidx)
        dq_acc[...] = jnp.zeros_like(dq_acc)
        carry_ref[...] = jnp.zeros_like(carry_ref)

        def block(kb, valid, lo=0):
            rows = pl.ds(pl.multiple_of(kb * tk, tk), tk)
            k_blk = kb_ref[rows, :]
            v_blk = vb_ref[rows, :]
            carries = [carry_ref[h, lo:, :] for h in range(nh)]
            dq_old = [dq_acc[h, lo:, :] for h in range(nh)]
            dk_old = dk_acc[rows, :]
            dv_old = dv_acc[rows, :]
            outs = []
            for h in range(nh):
                cols = slice((h // 2) * LANE, (h // 2 + 1) * LANE)
                q_h, do_h = q_heads[h][lo:], do_heads[h][lo:]
                after = jnp.sum(jnp.where(lane[lo:] == kb, after_ref[lo:, h * LANE:(h + 1) * LANE], 0.0), axis=1,
                                keepdims=True)
                _, log_beta, later = _sb_scores(q_h, k_blk[:, cols], valid, later_mat, after)
                beta = jnp.exp2(log_beta)
                a = _masked(valid, jnp.exp2(log_beta + later))
                da = _dot(do_h, v_blk[:, cols], NT)
                gterm = a * da
                before = _dot(gterm.astype(BF16), before_mat, NN) + carries[h]
                dz_b = _masked(valid, gterm * (1.0 - beta) - beta * before).astype(BF16)
                outs.append((dq_old[h] + _dot(dz_b, k_blk[:, cols], NN), _dot(dz_b, q_h, TN),
                             _dot(a.astype(BF16), do_h, TN),
                             carries[h] + jnp.sum(gterm, axis=1, keepdims=True)))
            for h in range(nh):
                dq_acc[h, lo:, :] = outs[h][0]
                carry_ref[h, lo:, :] = outs[h][3]
            dk_new = [outs[2 * p][1] + outs[2 * p + 1][1] for p in range(pairs)]
            dv_new = [outs[2 * p][2] + outs[2 * p + 1][2] for p in range(pairs)]
            dk_acc[rows, :] = dk_old + (dk_new[0] if pairs == 1 else jnp.concatenate(dk_new, axis=1))
            dv_acc[rows, :] = dv_old + (dv_new[0] if pairs == 1 else jnp.concatenate(dv_new, axis=1))

        def step(kb, _):
            block(kb, None)
            return 0

        lax.fori_loop(0, ratio * i, step, 0)
        masks = _diagonal_masks(t, tk)
        for d in range(ratio):
            block(ratio * i + d, masks[d][d * tk:], d * tk)
        for p in range(pairs):
            dq_ref[:, p * LANE:(p + 1) * LANE] = (jnp.where(first, dq_acc[2 * p], dq_acc[2 * p + 1]) * scale).astype(BF16)

        @pl.when(i == nq - 1)
        def _():
            dk_ref[...] = (dk_acc[...] * LN_2).astype(BF16)
            dv_ref[...] = dv_acc[...].astype(BF16)

        _ride(rider, "finish", jnp.logical_and(grp == groups - 1, i == nq - 1), riding)

    blk = lambda base: pl.BlockSpec((t, wide), lambda g, i: (i, base // pairs + g))
    full = lambda base: pl.BlockSpec((s, wide), lambda g, i: (0, base // pairs + g))
    out_blk = pl.BlockSpec((t, wide), lambda g, i: (i, g))
    out_full = pl.BlockSpec((s, wide), lambda g, i: (0, g))
    big = jax.ShapeDtypeStruct((s, WIDTH), BF16)
    extra = _rider_call_args(rider, 7, 4)
    outs = pl.pallas_call(
        body, name=name, grid=(groups, nq),
        in_specs=[blk(COL_SB_Q), full(COL_SB_K), full(COL_SB_V), blk(COL_SB_G), out_blk,
                  pl.BlockSpec((t, nh * LANE), lambda g, i: (i, g)), out_blk] + extra["in_specs"],
        out_specs=[out_blk, out_full, out_full, out_blk] + extra["out_specs"],
        out_shape=[big, big, big, big] + extra["out_shape"],
        input_output_aliases=extra["aliases"],
        scratch_shapes=[pltpu.VMEM((s, wide), BF16), pltpu.VMEM((s, wide), BF16),
                        pltpu.VMEM((s, wide), F32), pltpu.VMEM((s, wide), F32),
                        pltpu.VMEM((nh, t, LANE), F32), pltpu.VMEM((nh, t, 1), F32)] + extra["scratch"],
        compiler_params=pltpu.CompilerParams(dimension_semantics=("arbitrary", "arbitrary")),
    )(u, u, u, u, o, after, dy, *extra["inputs"])
    return outs[:4], outs[4:]


def _gate_fwd(u, ys, w_branch, name, ts=256):
    s = u.shape[0]

    def body(m0, m1, m2, y0, y1, y2, w_ref, p0, p1, p2, out_ref):
        tot = None
        for n, (m_ref, y_ref, p_ref) in enumerate(((m0, y0, p0), (m1, y1, p1), (m2, y2, p2))):
            proj = _dot(y_ref[...], w_ref[n], NN)
            p_ref[...] = proj.astype(BF16)
            term = _sigmoid(m_ref[...].astype(F32)) * proj
            tot = term if tot is None else tot + term
        out_ref[...] = tot.astype(BF16)

    mspec = lambda n: pl.BlockSpec((ts, D_MODEL), lambda i: (i, COL_MERGE_1024 + n))
    row = pl.BlockSpec((ts, D_MODEL), lambda i: (i, 0))
    yspec = pl.BlockSpec((ts, WIDTH), lambda i: (i, 0))
    big = jax.ShapeDtypeStruct((s, D_MODEL), BF16)
    outs = pl.pallas_call(
        body, name=name, grid=(s // ts,),
        in_specs=[mspec(0), mspec(1), mspec(2), yspec, yspec, yspec,
                  pl.BlockSpec(w_branch.shape, lambda i: (0, 0, 0))],
        out_specs=[row] * 4, out_shape=[big] * 4,
    )(u, u, u, *ys, w_branch)
    return outs[:3], outs[3]


def _gate_bwd(u, projs, dmerged, w_branch, name, ts=256, rider=None):
    s = u.shape[0]
    steps = s // ts

    def body(*refs):
        own, riding = _split_refs(refs, 8, 9, 0, rider)
        m0, m1, m2, p0, p1, p2, dm_ref, w_ref, dp0, dp1, dp2, dl0, dl1, dl2, dy0, dy1, dy2 = own
        _ride(rider, "start", pl.program_id(0) == 0, riding)
        dm = dm_ref[...].astype(F32)
        for n, (m_ref, p_ref, dp_ref, dl_ref, dy_ref) in enumerate(((m0, p0, dp0, dl0, dy0), (m1, p1, dp1, dl1, dy1),
                                                                    (m2, p2, dp2, dl2, dy2))):
            gate = _sigmoid(m_ref[...].astype(F32))
            dp = (dm * gate).astype(BF16)
            dp_ref[...] = dp
            dl_ref[...] = (dm * p_ref[...].astype(F32) * gate * (1.0 - gate)).astype(BF16)
            dy_ref[...] = _dot(dp, w_ref[n], NT)
        _ride(rider, "finish", pl.program_id(0) == steps - 1, riding)

    mspec = lambda n: pl.BlockSpec((ts, D_MODEL), lambda i: (i, COL_MERGE_1024 + n))
    row = pl.BlockSpec((ts, D_MODEL), lambda i: (i, 0))
    yspec = pl.BlockSpec((ts, WIDTH), lambda i: (i, 0))
    big = jax.ShapeDtypeStruct((s, D_MODEL), BF16)
    extra = _rider_call_args(rider, 8, 9)
    outs = pl.pallas_call(
        body, name=name, grid=(steps,),
        in_specs=[mspec(0), mspec(1), mspec(2), row, row, row, row,
                  pl.BlockSpec(w_branch.shape, lambda i: (0, 0, 0))] + extra["in_specs"],
        out_specs=[row] * 6 + [yspec] * 3 + extra["out_specs"],
        out_shape=[big] * 6 + [jax.ShapeDtypeStruct((s, WIDTH), F32)] * 3 + extra["out_shape"],
        input_output_aliases=extra["aliases"], scratch_shapes=extra["scratch"],
        compiler_params=pltpu.CompilerParams(dimension_semantics=("arbitrary",)),
    )(u, u, u, *projs, dmerged, w_branch, *extra["inputs"])
    return outs[:3], outs[3:6], outs[6:9], outs[9:]


def _as_rows(a):
    return a.reshape(-1, a.shape[-1])


def _row_tile(rows, cols, bytes_per_row_elem=4, cap=1 << 20):
    tr = rows
    while tr * cols * bytes_per_row_elem > cap and tr % 2 == 0 and (tr // 2) % 16 == 0:
        tr //= 2
    return tr


def _cast_bf16(a, name):
    a2 = _as_rows(a)
    rows, cols = a2.shape
    tr = _row_tile(rows, cols)

    def body(a_ref, o_ref):
        o_ref[...] = a_ref[...].astype(BF16)

    spec = pl.BlockSpec((tr, cols), lambda i: (i, 0))
    out = pl.pallas_call(body, name=name, grid=(rows // tr,), in_specs=[spec], out_specs=spec,
                         out_shape=jax.ShapeDtypeStruct((rows, cols), BF16))(a2)
    return out.reshape(a.shape)


def _adamw(w, g, m, v, name):
    shape = w.shape
    w2, g2, m2, v2 = (_as_rows(a) for a in (w, g, m, v))
    rows, cols = w2.shape
    tr = _row_tile(rows, cols)
    c1 = 1.0 - ADAM_B1 ** ADAM_STEP
    c2 = 1.0 - ADAM_B2 ** ADAM_STEP

    def body(w_ref, g_ref, m_ref, v_ref, d_ref, nm_ref, nv_ref):
        gv = g_ref[...]
        nm = ADAM_B1 * m_ref[...] + (1.0 - ADAM_B1) * gv
        nv = ADAM_B2 * v_ref[...] + (1.0 - ADAM_B2) * (gv * gv)
        nm_ref[...] = nm
        nv_ref[...] = nv
        d_ref[...] = -ADAM_LR * ((nm / c1) / (jnp.sqrt(nv / c2) + ADAM_EPS) + ADAM_WD * w_ref[...])

    spec = pl.BlockSpec((tr, cols), lambda i: (i, 0))
    sds = jax.ShapeDtypeStruct((rows, cols), F32)
    outs = pl.pallas_call(body, name=name, grid=(rows // tr,), in_specs=[spec] * 4, out_specs=[spec] * 3,
                          out_shape=[sds] * 3)(w2, g2, m2, v2)
    return tuple(o.reshape(shape) for o in outs)


def _sum_slots(a, out_dtype, name):
    n = a.shape[0]
    a3 = a.reshape(n, -1, a.shape[-1])
    _, rows, cols = a3.shape
    tr = _row_tile(rows, cols * n)

    def body(a_ref, o_ref):
        tot = a_ref[0].astype(F32)
        for k in range(1, n):
            tot = tot + a_ref[k].astype(F32)
        o_ref[...] = tot.astype(out_dtype)

    out = pl.pallas_call(
        body, name=name, grid=(rows // tr,),
        in_specs=[pl.BlockSpec((n, tr, cols), lambda i: (0, i, 0))],
        out_specs=pl.BlockSpec((tr, cols), lambda i: (i, 0)),
        out_shape=jax.ShapeDtypeStruct((rows, cols), out_dtype))(a3)
    return out.reshape(a.shape[1:])


def _chip_sum(own, recv, axis, core, name):
    half = recv.shape
    nd = len(half)
    last = nd - 1
    if axis == last:
        tl, nt = half[last], 1
    else:
        tl = min(half[last], 2048)
        nt = half[last] // tl
    block = half[:last] + (tl,)

    def own_index(i, core_ref):
        idx = [0] * nd
        idx[last] = i
        if axis == last:
            idx[last] = core_ref[0]
        else:
            idx[axis] = core_ref[0]
        return tuple(idx)

    def recv_index(i, core_ref):
        idx = [0] * nd
        idx[last] = i
        return tuple(idx)

    def body(core_ref, own_ref, recv_ref, o_ref):
        o_ref[...] = (own_ref[...].astype(F32) + recv_ref[...].astype(F32)).astype(BF16)

    return pl.pallas_call(
        body, name=name,
        grid_spec=pltpu.PrefetchScalarGridSpec(
            num_scalar_prefetch=1, grid=(nt,),
            in_specs=[pl.BlockSpec(block, own_index), pl.BlockSpec(block, recv_index)],
            out_specs=pl.BlockSpec(block, recv_index)),
        out_shape=jax.ShapeDtypeStruct(half, BF16),
    )(core, own, recv)


def _mesh_position():
    return lax.axis_index("x"), lax.axis_index("y"), lax.axis_index("c")


def _other_chips(x, y):
    return [(1 - x, y), (x, 1 - y), (1 - x, 1 - y)]


ALL_FLIPS = [(0, 0, 1), (1, 0, 0), (0, 1, 0), (1, 1, 0), (1, 0, 1), (0, 1, 1), (1, 1, 1)]


def _half(ref, axis, which, size):
    idx = [slice(None)] * len(ref.shape)
    idx[axis] = pl.ds(which * size, size)
    return ref.at[tuple(idx)]


def _sub(ref, picks):
    idx = [slice(None)] * len(ref.shape)
    for axis, start, size in picks:
        idx[axis] = pl.ds(start, size)
    return ref.at[tuple(idx)]


def _remote(src, dst, sems_send, sems_recv, k, to):
    return pltpu.make_async_remote_copy(src_ref=src, dst_ref=dst, send_sem=sems_send.at[k], recv_sem=sems_recv.at[k],
                                        device_id=to, device_id_type=MESH)


def _cast_shard(w, layer, shard_axis, pos, name, tr=512):
    shape = w.shape[1:]
    nd = len(shape)
    assert shard_axis in (nd - 1, nd - 2)
    rows, cols = shape[-2:]
    tr = min(tr, rows)
    nt = rows // tr
    lead = shape[:-2]
    full = list(shape)
    full[shard_axis] *= N_CHIPS
    block = (1,) * len(lead) + (tr, cols)

    def in_index(*args):
        return (layer, *args[:-1], 0)

    def out_index(*args):
        *g, pos_ref = args
        if shard_axis == nd - 1:
            return (*g, pos_ref[1])
        return (*g[:-1], pos_ref[1] * nt + g[-1], 0)

    def body(pos_ref, a_ref, o_ref):
        o_ref[...] = a_ref[...].astype(BF16)

    return pl.pallas_call(
        body, name=name,
        grid_spec=pltpu.PrefetchScalarGridSpec(
            num_scalar_prefetch=1, grid=lead + (nt,),
            in_specs=[pl.BlockSpec((None,) + block, in_index)], out_specs=pl.BlockSpec(block, out_index)),
        out_shape=jax.ShapeDtypeStruct(tuple(full), BF16),
    )(pos, w)


class _Rider:
    def __init__(self, inputs, out_shape, aliases, scratch, start, middle, finish):
        self.inputs, self.out_shape, self.aliases, self.scratch = inputs, out_shape, aliases, scratch
        self.start, self.middle, self.finish = start, middle, finish


def _weight_gather_rider(fulls, layout):
    n = len(fulls)

    def copies(outs, sems):
        send_sems, recv_sems = sems
        x, y, c = _mesh_position()
        chips = _other_chips(x, y)
        sibling = (x, y, 1 - c)
        mine = 2 * x + y

        def place(t, chip, core):
            sh_axis, sh_size, half_axis, half_size = layout[t]
            return _sub(outs[t], [(sh_axis, chip * sh_size, sh_size), (half_axis, core * half_size, half_size)])

        direct, arrive, forward, arrive_fwd = [], [], [], []
        for t in range(n):
            for k, (px, py) in enumerate(chips):
                theirs = 2 * px + py
                direct.append(_remote(place(t, mine, c), place(t, mine, c), send_sems, recv_sems, 6 * t + k, (px, py, c)))
                arrive.append(_remote(place(t, theirs, c), place(t, theirs, c), send_sems, recv_sems, 6 * t + k, (px, py, c)))
                forward.append(_remote(place(t, theirs, c), place(t, theirs, c), send_sems, recv_sems, 6 * t + 3 + k, sibling))
                arrive_fwd.append(_remote(place(t, theirs, 1 - c), place(t, theirs, 1 - c), send_sems, recv_sems,
                                          6 * t + 3 + k, sibling))
        return direct, arrive, forward, arrive_fwd

    def start(ins, outs, sems):
        for cp in copies(outs, sems)[0]:
            cp.start()

    def middle(ins, outs, sems):
        _, arrive, forward, _ = copies(outs, sems)
        for a, f in zip(arrive, forward):
            a.wait_recv()
            f.start()

    def finish(ins, outs, sems):
        direct, _, forward, arrive_fwd = copies(outs, sems)
        for cp in arrive_fwd:
            cp.wait_recv()
        for cp in direct + forward:
            cp.wait_send()

    return _Rider(list(fulls), [jax.ShapeDtypeStruct(a.shape, a.dtype) for a in fulls], {k: k for k in range(n)},
                  [pltpu.SemaphoreType.DMA((6 * n,)), pltpu.SemaphoreType.DMA((6 * n,))], start, middle, finish)


WEIGHT_LAYOUT = [(1, 2048, 0, 512), (2, 256, 1, 256), (0, 256, 1, 512)]


def _gather_weights(fulls, layout, conv_w):
    rider = _weight_gather_rider(fulls, layout)
    n = len(fulls)

    def body(*refs):
        cw, outs, cw_f = refs[n], refs[n + 1:2 * n + 1], refs[2 * n + 1]
        sems, (cw_send, cw_recv, local_sem) = refs[2 * n + 2:2 * n + 4], refs[2 * n + 4:]
        x, y, c = _mesh_position()
        chips = _other_chips(x, y)
        mine = 2 * x + y
        local = pltpu.make_async_copy(cw, cw_f.at[mine], local_sem.at[0])
        local.start()
        rider.start(None, outs, sems)
        small = [_remote(cw, cw_f.at[mine], cw_send, cw_recv, k, (*chip, c)) for k, chip in enumerate(chips)]
        for cp in small:
            cp.start()
        rider.middle(None, outs, sems)
        rider.finish(None, outs, sems)
        for k, (px, py) in enumerate(chips):
            _remote(cw, cw_f.at[2 * px + py], cw_send, cw_recv, k, (px, py, c)).wait_recv()
        for cp in small:
            cp.wait_send()
        local.wait()

    outs = pl.pallas_call(
        body, name="gather_weights",
        in_specs=[ANY] * (n + 1), out_specs=[ANY] * (n + 1),
        out_shape=rider.out_shape + [jax.ShapeDtypeStruct((N_CHIPS,) + conv_w.shape, F32)],
        input_output_aliases=rider.aliases,
        scratch_shapes=rider.scratch + [pltpu.SemaphoreType.DMA((3,)), pltpu.SemaphoreType.DMA((3,)),
                                        pltpu.SemaphoreType.DMA((1,))],
    )(*fulls, conv_w)
    return outs[:n], outs[n]


def _swap_rider(items):
    n = len(items)
    halves = []
    for a, axis in items:
        shp = list(a.shape)
        shp[axis] //= 2
        halves.append(tuple(shp))

    def copies(ins, outs, sems):
        x, y, c = _mesh_position()
        return [_remote(_half(ins[k], items[k][1], 1 - c, halves[k][items[k][1]]), outs[k], sems[0], sems[1], k,
                        (x, y, 1 - c)) for k in range(n)]

    def start(ins, outs, sems):
        for cp in copies(ins, outs, sems):
            cp.start()

    def finish(ins, outs, sems):
        for cp in copies(ins, outs, sems):
            cp.wait()

    return _Rider([a for a, _ in items], [jax.ShapeDtypeStruct(h, a.dtype) for h, (a, _) in zip(halves, items)], {},
                  [pltpu.SemaphoreType.DMA((n,)), pltpu.SemaphoreType.DMA((n,))], start, None, finish)


def _swap_halves(items, name):
    rider = _swap_rider(items)
    n = len(items)

    def body(*refs):
        parts = (refs[:n], refs[n:2 * n], refs[2 * n:])
        rider.start(*parts)
        rider.finish(*parts)

    return pl.pallas_call(
        body, name=name, in_specs=[ANY] * n, out_specs=[ANY] * n, out_shape=rider.out_shape,
        scratch_shapes=rider.scratch,
    )(*rider.inputs)


def _grad_exchange_rider(items):
    n = len(items)
    slices = []
    for a, axis in items:
        shp = list(a.shape)
        shp[axis] //= N_CHIPS
        slices.append(tuple(shp))

    def copies(ins, outs, sems):
        send_sems, recv_sems = sems
        x, y, c = _mesh_position()
        made = []
        for k in range(n):
            axis = items[k][1]
            for r, (px, py) in enumerate(_other_chips(x, y)):
                made.append(_remote(_half(ins[k], axis, 2 * px + py, slices[k][axis]), outs[k].at[r],
                                    send_sems, recv_sems, 3 * k + r, (px, py, c)))
        return made

    def start(ins, outs, sems):
        for cp in copies(ins, outs, sems):
            cp.start()

    def finish(ins, outs, sems):
        for cp in copies(ins, outs, sems):
            cp.wait()

    return _Rider([a for a, _ in items], [jax.ShapeDtypeStruct((N_CHIPS - 1,) + s, BF16) for s in slices], {},
                  [pltpu.SemaphoreType.DMA((3 * n,)), pltpu.SemaphoreType.DMA((3 * n,))], start, None, finish)


def _small_gather_rider(small):
    def copies(ins, outs, sems):
        x, y, c = _mesh_position()
        me = 4 * x + 2 * y + c
        local = pltpu.make_async_copy(ins[0], outs[0].at[me], sems[2].at[0])
        remote = [_remote(ins[0], outs[0].at[me], sems[0], sems[1], r, (x ^ fx, y ^ fy, c ^ fc))
                  for r, (fx, fy, fc) in enumerate(ALL_FLIPS)]
        return local, remote

    def start(ins, outs, sems):
        local, remote = copies(ins, outs, sems)
        local.start()
        for cp in remote:
            cp.start()

    def finish(ins, outs, sems):
        local, remote = copies(ins, outs, sems)
        for cp in remote:
            cp.wait()
        local.wait()

    n = len(ALL_FLIPS)
    return _Rider([small], [jax.ShapeDtypeStruct((2 * N_CHIPS,) + small.shape, F32)], {},
                  [pltpu.SemaphoreType.DMA((n,)), pltpu.SemaphoreType.DMA((n,)), pltpu.SemaphoreType.DMA((1,))],
                  start, None, finish)


def _sum_chips(recv, own, shard_axis, split_axis, pos, dest, layer, name, tr=128):
    sl = recv.shape[1:]
    nd = len(sl)
    tiled = nd == 2 and sl[0] > tr
    nt = sl[0] // tr if tiled else 1
    block = ((tr,) + sl[1:]) if tiled else sl
    shard = list(sl)
    shard[split_axis] *= 2

    def recv_index(i, pos_ref):
        return (0, i) + (0,) * (nd - 1) if tiled else (0,) * (nd + 1)

    def own_index(i, pos_ref):
        idx = [0] * nd
        idx[shard_axis] = pos_ref[1]
        if tiled:
            idx[0] = pos_ref[1] * nt + i if shard_axis == 0 else i
        return tuple(idx)

    def out_index(i, pos_ref):
        idx = [0] * nd
        idx[split_axis] = pos_ref[0]
        if tiled:
            idx[0] = pos_ref[0] * nt + i if split_axis == 0 else i
        return (layer, *idx)

    def body(pos_ref, recv_ref, own_ref, *rest):
        o_ref = rest[-1]
        tot = own_ref[...].astype(F32)
        for k in range(N_CHIPS - 1):
            tot = tot + recv_ref[k].astype(F32)
        o_ref[0] = tot

    in_specs = [pl.BlockSpec((N_CHIPS - 1,) + block, recv_index), pl.BlockSpec(block, own_index)]
    args = [pos, recv, own]
    aliases = {}
    if dest is not None:
        in_specs.append(ANY)
        args.append(dest)
        aliases = {3: 0}
    return pl.pallas_call(
        body, name=name,
        grid_spec=pltpu.PrefetchScalarGridSpec(
            num_scalar_prefetch=1, grid=(nt,), in_specs=in_specs,
            out_specs=pl.BlockSpec((1,) + block, out_index)),
        out_shape=jax.ShapeDtypeStruct((DEPTH,) + tuple(shard), F32),
        input_output_aliases=aliases,
    )(*args)


def _share_halves(bufs, late_small, name):
    n = len(bufs)
    small = _small_gather_rider(late_small)

    def body(*refs):
        outs, (send_sems, recv_sems) = refs[n + 1:2 * n + 1], refs[2 * n + 2:2 * n + 4]
        small_parts = ([refs[n]], [refs[2 * n + 1]], refs[2 * n + 4:])
        x, y, c = _mesh_position()
        small.start(*small_parts)
        copies = []
        for k, (a, axis) in enumerate(bufs):
            size = a.shape[1 + axis] // 2
            mine = _half(outs[k], 1 + axis, c, size)
            copies.append(_remote(mine, mine, send_sems, recv_sems, k, (x, y, 1 - c)))
        for cp in copies:
            cp.start()
        for cp in copies:
            cp.wait()
        small.finish(*small_parts)

    outs = pl.pallas_call(
        body, name=name, in_specs=[ANY] * (n + 1), out_specs=[ANY] * (n + 1),
        out_shape=[jax.ShapeDtypeStruct(a.shape, F32) for a, _ in bufs] + small.out_shape,
        input_output_aliases={k: k for k in range(n)},
        scratch_shapes=[pltpu.SemaphoreType.DMA((n,)), pltpu.SemaphoreType.DMA((n,))] + small.scratch,
    )(*[a for a, _ in bufs], late_small)
    return outs[:n], outs[n]


def _layer_fwd(x, p, l, rider=None, proj_rider=None):
    tag = f"l{l}_"
    h = _rms_fwd(x, p["pre_g"], tag + "pre_norm")
    u = _matmul(h, p["w_in"], "nn", BF16, tag + "in_proj", rider=proj_rider)
    if proj_rider is not None:
        u, (w_branch, w_out) = u
        p = dict(p, w_branch=w_branch, w_out=w_out)
    y_pool = _pool_fwd(u, p["pool_w"], p["pool_scale"], tag + "pool")
    y_conv = _conv_fwd(u, p["conv_w"], p["conv_b"], tag + "conv")
    (o_sb, y_sb, sb_after), carried = _sb_fwd(u, tag + "stickbreak", rider=rider)
    ys = [y_pool, y_conv, y_sb]
    projs, merged = _gate_fwd(u, ys, p["w_branch"], tag + "merge")
    out = _matmul(merged, p["w_out"], "nn", F32, tag + "out_proj")
    saved = dict(x=x, h=h, u=u, ys=ys, o_sb=o_sb, sb_after=sb_after, projs=projs, merged=merged, out=out)
    return out, saved, carried, p


def _layer_bwd(dy, p, saved, l, merge_rider=None, early=None, before_dw=None, late=None):
    tag = f"l{l}_bwd_"
    u = saved["u"]
    d_out, g_post = _rms_bwd(saved["out"], p["post_g"], dy, None, BF16, tag + "post_norm")
    d_merged = _matmul(d_out, p["w_out"], "nt", BF16, tag + "out_proj_dx")
    g_w_out = _matmul(saved["merged"], d_out, "tn", BF16, tag + "out_proj_dw", tk=2048)
    d_projs, d_logits, d_ys, carried_merge = _gate_bwd(u, saved["projs"], d_merged, p["w_branch"], tag + "merge",
                                                       rider=merge_rider)
    g_w_branch = jnp.stack([_matmul(saved["ys"][n], d_projs[n], "tn", BF16, tag + f"branch_dw{n}", tk=2048)
                            for n in range(3)])
    rider = early(g_w_branch, g_w_out, carried_merge) if early else None
    d_pv, d_pg, g_pool_w, g_pool_scale = _pool_bwd(u, d_ys[0], p["pool_w"], p["pool_scale"], tag + "pool")
    d_cx, d_cgb, d_cgc, d_cg, g_conv_w, g_conv_b = _conv_bwd(u, d_ys[1], p["conv_w"], p["conv_b"], tag + "conv")
    (d_q, d_k, d_v, d_sg), carried_attn = _sb_bwd(u, saved["o_sb"], saved["sb_after"], d_ys[2], tag + "stickbreak",
                                                  rider=rider)
    du = jnp.concatenate([d_pv, d_pg, d_cx, d_cgb, d_cgc, d_cg, d_q, d_k, d_v, d_sg] + list(d_logits), axis=1)
    grads = dict(w_branch=g_w_branch, w_out=g_w_out, post_g=g_post, pool_w=g_pool_w, pool_scale=g_pool_scale,
                 conv_w=g_conv_w, conv_b=g_conv_b)
    rider = before_dw(grads) if before_dw else None
    g_w_in = _matmul(saved["h"], du, "tn", BF16, tag + "in_proj_dw", tk=2048, rider=rider)
    g_w_in, carried_dw = g_w_in if rider else (g_w_in, [])
    rider = late(g_w_in) if late else None
    dh = _matmul(du, p["w_in"], "nt", BF16, tag + "in_proj_dx", tk=2048, rider=rider)
    dh, carried_dx = dh if rider else (dh, [])
    dx, g_pre = _rms_bwd(saved["x"], p["pre_g"], dh, dy, F32, tag + "pre_norm")
    grads.update(w_in=g_w_in, pre_g=g_pre)
    return dx, grads, carried_attn, carried_dw, carried_dx


SMALL_ORDER = ["pre_g", "pool_w", "pool_scale", "conv_w", "conv_b", "post_g"]


def _pack_small(per_layer):
    parts, spans, at = [], {}, 0
    for name in SMALL_ORDER:
        a = jnp.stack([per_layer[l][name] for l in range(DEPTH)]).reshape(-1, LANE)
        parts.append(a)
        spans[name] = (at, a.shape[0])
        at += a.shape[0]
    return jnp.concatenate(parts, axis=0), spans


def kernel(x, pre_norm_g, w_in, pool_w, pool_scale, conv_w, conv_b, w_branch, w_out, post_norm_g, loss_target, m_pre_norm_g, m_w_in, m_pool_w, m_pool_scale, m_conv_w, m_conv_b, m_w_branch, m_w_out, m_post_norm_g, v_pre_norm_g, v_w_in, v_pool_w, v_pool_scale, v_conv_w, v_conv_b, v_w_branch, v_w_out, v_post_norm_g):
    mx, my, mc = _mesh_position()
    chip = 2 * mx + my
    core = mc.astype(jnp.int32).reshape(1)
    pos = jnp.stack([mc, chip]).astype(jnp.int32)

    names = ["w_in", "w_branch", "w_out"]
    given = dict(w_in=w_in, w_branch=w_branch, w_out=w_out)
    in_place = [[_cast_shard(given[n], l, WEIGHT_LAYOUT[i][0], pos, f"cast_{n}{l}") for i, n in enumerate(names)]
                for l in range(DEPTH)]
    (w_in_0,), conv_w_by_chip = _gather_weights(in_place[0][:1], WEIGHT_LAYOUT[:1], conv_w)
    gathered = [w_in_0] + in_place[0][1:]
    conv_w_f = conv_w_by_chip.transpose(1, 2, 0, 3).reshape(DEPTH, 3, WIDTH)
    pool_w_b = _cast_bf16(pool_w, "cast_pool_w")

    def layer_params(l, big):
        return dict(pre_g=pre_norm_g[l:l + 1], post_g=post_norm_g[l:l + 1], w_in=big[0], w_branch=big[1],
                    w_out=big[2], pool_w=pool_w_b[l], pool_scale=pool_scale[l:l + 1], conv_w=conv_w_f[l],
                    conv_b=conv_b[l:l + 1])

    act = x[0]
    params, saved = [], []
    for l in range(DEPTH):
        rider = _weight_gather_rider(in_place[l + 1], WEIGHT_LAYOUT) if l + 1 < DEPTH else None
        proj_rider = _weight_gather_rider(gathered[1:], WEIGHT_LAYOUT[1:]) if l == 0 else None
        out, sv, gathered, layer_p = _layer_fwd(act, layer_params(l, gathered), l, rider, proj_rider)
        params.append(layer_p)
        saved.append(sv)
        if l < DEPTH - 1:
            act = _resid_out(act, out, params[l]["post_g"], None, f"l{l}_resid")
    dy, loss_part = _resid_out(act, saved[-1]["out"], params[-1]["post_g"], loss_target[0], "loss_head")

    split_axis = dict(w_in=0, w_branch=1, w_out=1)
    shard_axis = dict(w_in=1, w_branch=2, w_out=0)
    grads = [None] * DEPTH
    chip_sums = [dict() for _ in range(DEPTH)]
    by_chip = [dict() for _ in range(DEPTH)]

    def reduce_in_chip(l, which, g):
        items = [(g[n], split_axis[n]) for n in which]
        from_sibling = _swap_halves(items, f"swap_grad_halves{l}_{which[0]}")
        for n, (a, axis), r in zip(which, items, from_sibling):
            chip_sums[l][n] = _chip_sum(a, r, axis, core, f"chip_sum{l}_{n}")

    def exchange_rider(keys):
        return _grad_exchange_rider([(chip_sums[l][n], shard_axis[n]) for l, n in keys])

    waiting = []
    for l in reversed(range(DEPTH)):
        sent_early, sent_late = list(waiting) + [(l, "w_branch"), (l, "w_out")], [(l, "w_in")]
        waiting_items = [(grads[ll][n], split_axis[n]) for ll, n in waiting]

        def early(g_w_branch, g_w_out, from_sibling, l=l, keys=sent_early, above=tuple(waiting), items=waiting_items):
            for (ll, n), (a, axis), r in zip(above, items, from_sibling):
                chip_sums[ll][n] = _chip_sum(a, r, axis, core, f"chip_sum{ll}_{n}")
            reduce_in_chip(l, ["w_branch", "w_out"], dict(w_branch=g_w_branch, w_out=g_w_out))
            return exchange_rider(keys)

        def late(g_w_in, l=l, keys=sent_late):
            reduce_in_chip(l, ["w_in"], dict(w_in=g_w_in))
            return exchange_rider(keys)

        def before_dw(partial, l=l):
            layers = [dict(partial, pre_g=jnp.zeros_like(pre_norm_g[:1])) if ll == l else grads[ll]
                      for ll in range(DEPTH)]
            return _small_gather_rider(_pack_small(layers)[0])

        if l == DEPTH - 1:
            dy, grads[l], _, _, _ = _layer_bwd(dy, params[l], saved[l], l)
            waiting = [(l, n) for n in names]
        else:
            dy, grads[l], got_early, got_dw, got_late = _layer_bwd(
                dy, params[l], saved[l], l, _swap_rider(waiting_items), early, before_dw if l == 0 else None, late)
            for (ll, n), r in zip(sent_early + sent_late, list(got_early) + list(got_late)):
                by_chip[ll][n] = r
            if l == 0:
                small_all = got_dw[0]
            waiting = []
    assert not waiting
    grad_x = dy[None]
    _, spans = _pack_small(grads)
    late_small = jnp.concatenate([grads[0]["pre_g"].reshape(-1, LANE), jnp.broadcast_to(loss_part, (8, LANE))])
    bufs = []
    for n in names:
        dest = None
        for l in range(DEPTH):
            dest = _sum_chips(by_chip[l][n], chip_sums[l][n], shard_axis[n], split_axis[n], pos, dest, l,
                              f"sum_chips{l}_{n}")
        bufs.append((dest, split_axis[n]))
    (g_w_in, g_w_branch, g_w_out), late_all = _share_halves(bufs, late_small, "share_grad_halves")

    late_sum = _sum_slots(late_all, F32, "sum_late_small")
    n_gain = late_small.shape[0] - 8
    loss = late_sum[n_gain, 0]
    small_sum = _sum_slots(small_all, F32, "sum_small")
    at, _ = spans["pre_g"]
    small_sum = jnp.concatenate([small_sum[:at], late_sum[:n_gain], small_sum[at + n_gain:]])
    small = {}
    for name, like in (("pre_g", pre_norm_g), ("pool_w", pool_w), ("pool_scale", pool_scale), ("conv_b", conv_b),
                       ("post_g", post_norm_g)):
        at, n = spans[name]
        small[name] = small_sum[at:at + n].reshape(like.shape)
    at, n = spans["conv_w"]
    g_conv_w_full = small_sum[at:at + n].reshape(DEPTH, 3, WIDTH)
    g_conv_w = lax.dynamic_slice_in_dim(g_conv_w_full, chip * conv_w.shape[2], conv_w.shape[2], axis=2)

    g = dict(pre_norm_g=small["pre_g"], w_in=g_w_in, pool_w=small["pool_w"], pool_scale=small["pool_scale"],
             conv_w=g_conv_w, conv_b=small["conv_b"], w_branch=g_w_branch, w_out=g_w_out, post_norm_g=small["post_g"])
    w = dict(pre_norm_g=pre_norm_g, w_in=w_in, pool_w=pool_w, pool_scale=pool_scale, conv_w=conv_w, conv_b=conv_b,
             w_branch=w_branch, w_out=w_out, post_norm_g=post_norm_g)
    m = dict(pre_norm_g=m_pre_norm_g, w_in=m_w_in, pool_w=m_pool_w, pool_scale=m_pool_scale, conv_w=m_conv_w,
             conv_b=m_conv_b, w_branch=m_w_branch, w_out=m_w_out, post_norm_g=m_post_norm_g)
    v = dict(pre_norm_g=v_pre_norm_g, w_in=v_w_in, pool_w=v_pool_w, pool_scale=v_pool_scale, conv_w=v_conv_w,
             conv_b=v_conv_b, w_branch=v_w_branch, w_out=v_w_out, post_norm_g=v_post_norm_g)
    order = ["pre_norm_g", "w_in", "pool_w", "pool_scale", "conv_w", "conv_b", "w_branch", "w_out", "post_norm_g"]
    upd = {n: _adamw(w[n], g[n], m[n], v[n], "adamw_" + n) for n in order}
    return (loss, grad_x, *[g[n] for n in order], *[upd[n][0] for n in order], *[upd[n][1] for n in order],
            *[upd[n][2] for n in order])
```

```python
import functools

import jax
import jax.numpy as jnp
from jax import lax
from jax.experimental import pallas as pl
from jax.experimental.pallas import tpu as pltpu

F32 = jnp.float32
BF16 = jnp.bfloat16
MESH = pl.DeviceIdType.MESH
ANY = pl.BlockSpec(memory_space=pl.ANY)

DEPTH = 2
D_MODEL = 1024
WIDTH = 512
N_IN = 8192
N_CHIPS = 4
HEAD_DIM = 64
RMS_EPS = 1e-6
POOL_HALO = 16
CONV_HALO = 16
LANE = 128
COL_POOL_V, COL_POOL_G = 0, 4
COL_CONV_X, COL_CONV_GB, COL_CONV_GC, COL_CONV_G = 8, 12, 16, 20
COL_SB_Q, COL_SB_K, COL_SB_V, COL_SB_G = 24, 28, 32, 36
COL_MERGE_1024 = 5

ADAM_LR, ADAM_B1, ADAM_B2, ADAM_EPS, ADAM_WD, ADAM_STEP = 0.001, 0.9, 0.999, 1e-08, 0.01, 10

NN = (((1,), (0,)), ((), ()))
NT = (((1,), (1,)), ((), ()))
TN = (((0,), (0,)), ((), ()))


def _sigmoid(x):
    return 1.0 / (1.0 + jnp.exp(-x))


def _silu_and_grad(x):
    s = _sigmoid(x)
    return x * s, s * (1.0 + x * (1.0 - s))


def _dot(a, b, dims):
    return lax.dot_general(a, b, dims, preferred_element_type=F32)


def _matmul(a, b, mode, out_dtype, name, tm=1024, tn=1024, tk=1024, b_lead=(), rider=None):
    b_shape = b.shape[len(b_lead):]
    if mode == "nn":
        (m, k), (k2, n) = a.shape, b_shape
    elif mode == "nt":
        (m, k), (n, k2) = a.shape, b_shape
    else:
        (k, m), (k2, n) = a.shape, b_shape
    assert k == k2 and a.dtype == BF16 and b.dtype == BF16
    tm, tn, tk = min(tm, m), min(tn, n), min(tk, k)
    assert m % tm == 0 and n % tn == 0 and k % tk == 0
    nk = k // tk
    dims = {"nn": NN, "nt": NT, "tn": TN}[mode]

    grid = (m // tm, n // tn, nk)

    def at_step(step):
        return functools.reduce(jnp.logical_and, [pl.program_id(d) == s for d, s in enumerate(step)])

    def body(*refs):
        (a_ref, b_ref, o_ref, *scratch), riding = _split_refs(refs, 2, 1, 1 if nk > 1 else 0, rider)
        _ride(rider, "start", at_step((0, 0, 0)), riding)
        _ride(rider, "middle", at_step(((3 * grid[0]) // 4, 0, 0)), riding)
        compute(a_ref, b_ref, o_ref, scratch)
        _ride(rider, "finish", at_step([g - 1 for g in grid]), riding)

    def compute(a_ref, b_ref, o_ref, scratch):
        p = _dot(a_ref[...], b_ref[...], dims)
        if nk == 1:
            o_ref[...] = p.astype(o_ref.dtype)
        else:
            acc = scratch[0]
            kk = pl.program_id(2)

            @pl.when(kk == 0)
            def _():
                acc[...] = p

            @pl.when(jnp.logical_and(kk > 0, kk < nk - 1))
            def _():
                acc[...] += p

            @pl.when(kk == nk - 1)
            def _():
                o_ref[...] = (acc[...] + p).astype(o_ref.dtype)

    if mode == "tn":
        a_spec = pl.BlockSpec((tk, tm), lambda i, j, kk: (kk, i))
    else:
        a_spec = pl.BlockSpec((tm, tk), lambda i, j, kk: (i, kk))
    squeezed = (None,) * len(b_lead)
    if mode == "nt":
        b_spec = pl.BlockSpec(squeezed + (tn, tk), lambda i, j, kk: (*b_lead, j, kk))
    else:
        b_spec = pl.BlockSpec(squeezed + (tk, tn), lambda i, j, kk: (*b_lead, kk, j))
    extra = _rider_call_args(rider, 2, 1)
    outs = pl.pallas_call(
        body, name=name, grid=grid,
        in_specs=[a_spec, b_spec] + extra["in_specs"],
        out_specs=[pl.BlockSpec((tm, tn), lambda i, j, kk: (i, j))] + extra["out_specs"],
        out_shape=[jax.ShapeDtypeStruct((m, n), out_dtype)] + extra["out_shape"],
        input_output_aliases=extra["aliases"],
        scratch_shapes=([pltpu.VMEM((tm, tn), F32)] if nk > 1 else []) + extra["scratch"],
        compiler_params=pltpu.CompilerParams(dimension_semantics=("arbitrary",) * 3 if rider else
                                             ("parallel", "parallel", "arbitrary")),
    )(a, b, *extra["inputs"])
    return (outs[0], outs[1:]) if rider else outs[0]


def _rms_fwd(x, g, name, ts=512):
    s, d = x.shape

    def body(x_ref, g_ref, h_ref):
        xv = x_ref[...]
        r = lax.rsqrt(jnp.mean(xv * xv, axis=-1, keepdims=True) + RMS_EPS)
        h_ref[...] = (xv * r * g_ref[...]).astype(BF16)

    return pl.pallas_call(
        body, name=name, grid=(s // ts,),
        in_specs=[pl.BlockSpec((ts, d), lambda i: (i, 0)), pl.BlockSpec((1, d), lambda i: (0, 0))],
        out_specs=pl.BlockSpec((ts, d), lambda i: (i, 0)),
        out_shape=jax.ShapeDtypeStruct((s, d), BF16),
    )(x, g)


def _rms_bwd(xin, g, dh, resid, out_dtype, name, ts=512):
    s, d = xin.shape
    has_resid = resid is not None

    def body(*refs):
        if has_resid:
            x_ref, g_ref, dh_ref, res_ref, dx_ref, dg_ref = refs
        else:
            x_ref, g_ref, dh_ref, dx_ref, dg_ref = refs
        xv = x_ref[...]
        dhv = dh_ref[...].astype(F32)
        r = lax.rsqrt(jnp.mean(xv * xv, axis=-1, keepdims=True) + RMS_EPS)
        nrm = xv * r
        dn = dhv * g_ref[...]
        dx = r * (dn - nrm * jnp.mean(dn * nrm, axis=-1, keepdims=True))
        if has_resid:
            dx = dx + res_ref[...]
        dx_ref[...] = dx.astype(dx_ref.dtype)
        part = jnp.sum(dhv * nrm, axis=0, keepdims=True)

        @pl.when(pl.program_id(0) == 0)
        def _():
            dg_ref[...] = part

        @pl.when(pl.program_id(0) > 0)
        def _():
            dg_ref[...] += part

    row = pl.BlockSpec((ts, d), lambda i: (i, 0))
    vec = pl.BlockSpec((1, d), lambda i: (0, 0))
    ins = [xin, g, dh] + ([resid] if has_resid else [])
    return pl.pallas_call(
        body, name=name, grid=(s // ts,),
        in_specs=[row, vec, row] + ([row] if has_resid else []),
        out_specs=[row, vec],
        out_shape=[jax.ShapeDtypeStruct((s, d), out_dtype), jax.ShapeDtypeStruct((1, d), F32)],
        compiler_params=pltpu.CompilerParams(dimension_semantics=("arbitrary",)),
    )(*ins)


def _resid_out(x, out, g, target, name, ts=512):
    s, d = x.shape
    has_loss = target is not None

    def body(*refs):
        if has_loss:
            x_ref, o_ref, g_ref, t_ref, dy_ref, loss_ref = refs
        else:
            x_ref, o_ref, g_ref, y_ref = refs
        ov = o_ref[...]
        r = lax.rsqrt(jnp.mean(ov * ov, axis=-1, keepdims=True) + RMS_EPS)
        yv = x_ref[...] + ov * r * g_ref[...]
        if not has_loss:
            y_ref[...] = yv
            return
        err = yv - t_ref[...]
        dy_ref[...] = err * (1.0 / d)
        part = jnp.sum(jnp.sum(err * err, axis=-1, keepdims=True), axis=0, keepdims=True) * (0.5 / d)
        part = jnp.broadcast_to(part, (1, LANE))

        @pl.when(pl.program_id(0) == 0)
        def _():
            loss_ref[...] = part

        @pl.when(pl.program_id(0) > 0)
        def _():
            loss_ref[...] += part

    row = pl.BlockSpec((ts, d), lambda i: (i, 0))
    vec = pl.BlockSpec((1, d), lambda i: (0, 0))
    if has_loss:
        return pl.pallas_call(
            body, name=name, grid=(s // ts,),
            in_specs=[row, row, vec, row],
            out_specs=[row, pl.BlockSpec((1, LANE), lambda i: (0, 0))],
            out_shape=[jax.ShapeDtypeStruct((s, d), F32), jax.ShapeDtypeStruct((1, LANE), F32)],
            compiler_params=pltpu.CompilerParams(dimension_semantics=("arbitrary",)),
        )(x, out, g, target)
    return pl.pallas_call(
        body, name=name, grid=(s // ts,),
        in_specs=[row, row, vec], out_specs=row,
        out_shape=jax.ShapeDtypeStruct((s, d), F32),
    )(x, out, g)


def _rows_before(ref, start, n, halo):
    if start == 0:
        return jnp.concatenate([jnp.zeros((halo, ref.shape[1]), F32), ref[0:n, :].astype(F32)], axis=0)
    return ref[start - halo:start + n, :].astype(F32)


def _rows_after(ref, start, n, halo):
    if start + n == ref.shape[0]:
        return jnp.concatenate([ref[start:start + n, :].astype(F32), jnp.zeros((halo, ref.shape[1]), F32)], axis=0)
    return ref[start:start + n + halo, :].astype(F32)


def _pick_window(group, s2, s4, s8, s16):
    return jnp.where(group == 0, s2, jnp.where(group == 1, s4, jnp.where(group == 2, s8, s16)))


def _trailing_sums(ext, group):
    s2 = ext + pltpu.roll(ext, 1, 0)
    s4 = s2 + pltpu.roll(s2, 2, 0)
    s8 = s4 + pltpu.roll(s4, 4, 0)
    s16 = s8 + pltpu.roll(s8, 8, 0)
    return _pick_window(group, s2, s4, s8, s16)


def _leading_sums(ext, group):
    n = ext.shape[0]
    s2 = ext + pltpu.roll(ext, n - 1, 0)
    s4 = s2 + pltpu.roll(s2, n - 2, 0)
    s8 = s4 + pltpu.roll(s4, n - 4, 0)
    s16 = s8 + pltpu.roll(s8, n - 8, 0)
    return _pick_window(group, s2, s4, s8, s16)


def _window_count(start, n, group):
    pos = start + lax.broadcasted_iota(jnp.int32, (n, LANE), 0)
    return jnp.minimum(pos + 1, 2 << group).astype(F32)


def _pooled(v_ref, start, n, group):
    ext = _rows_before(v_ref, start, n, POOL_HALO)
    sums = _trailing_sums(ext, group)[POOL_HALO:, :]
    return sums / _window_count(start, n, group) - ext[POOL_HALO:, :]


def _pool_fwd(u, pool_w, pool_scale, name, ts=512):
    s = u.shape[0]

    def body(v_ref, gate_ref, w_ref, sc_ref, y_ref):
        group = pl.program_id(0)
        for c in range(s // ts):
            a = c * ts
            pooled = _pooled(v_ref, a, ts, group)
            mixed = _dot(pooled.astype(BF16), w_ref[...], NN)
            gate = gate_ref[a:a + ts, :].astype(F32)
            y_ref[a:a + ts, :] = (mixed * sc_ref[...] * (gate * _sigmoid(gate))).astype(BF16)

    col = lambda base: pl.BlockSpec((s, LANE), lambda g: (0, base + g))
    return pl.pallas_call(
        body, name=name, grid=(4,),
        in_specs=[col(COL_POOL_V), col(COL_POOL_G),
                  pl.BlockSpec((None, LANE, LANE), lambda g: (g, 0, 0)),
                  pl.BlockSpec((1, LANE), lambda g: (0, g))],
        out_specs=pl.BlockSpec((s, LANE), lambda g: (0, g)),
        out_shape=jax.ShapeDtypeStruct((s, WIDTH), BF16),
    )(u, u, pool_w, pool_scale)


def _pool_bwd(u, dy, pool_w, pool_scale, du, name, ts=512):
    s = u.shape[0]

    def body(v_ref, gate_ref, dy_ref, w_ref, sc_ref, du_in, du_ref, dw_ref, dsc_ref, dgate_ref):
        pl.when(pl.program_id(1) == 0)(lambda: compute(v_ref, gate_ref, dy_ref, w_ref, sc_ref, du_ref, dgate_ref,
                                                       dw_ref, dsc_ref))

        @pl.when(pl.program_id(1) == 1)
        def _():
            du_ref[...] = dgate_ref[...]

    def compute(v_ref, gate_ref, dy_ref, w_ref, sc_ref, dv_ref, dgate_ref, dw_ref, dsc_ref):
        group = pl.program_id(0)
        w = w_ref[...]
        scale = sc_ref[...]
        dw = jnp.zeros((LANE, LANE), F32)
        dsc = jnp.zeros((1, LANE), F32)
        for c in range(s // ts):
            a = c * ts
            n_ext = ts + POOL_HALO
            gate_e = _rows_after(gate_ref, a, ts, POOL_HALO)
            dy_e = _rows_after(dy_ref, a, ts, POOL_HALO)
            silu_e, dsilu_e = _silu_and_grad(gate_e)
            dms_e = dy_e * silu_e
            dm_e = (dms_e * scale).astype(BF16)
            dpool_e = _dot(dm_e, w, NT)
            spread = _leading_sums(dpool_e / _window_count(a, n_ext, group), group)
            dv_ref[a:a + ts, :] = (spread[0:ts, :] - dpool_e[0:ts, :]).astype(BF16)
            pooled = _pooled(v_ref, a, ts, group).astype(BF16)
            mixed = _dot(pooled, w, NN)
            dgate_ref[a:a + ts, :] = (dy_e[0:ts, :] * mixed * scale * dsilu_e[0:ts, :]).astype(BF16)
            dsc = dsc + jnp.sum(dms_e[0:ts, :] * mixed, axis=0, keepdims=True)
            dw = dw + _dot(pooled, dm_e[0:ts, :], TN)
        dw_ref[...] = dw
        dsc_ref[...] = dsc

    col = lambda base: pl.BlockSpec((s, LANE), lambda g, k: (0, base + g))
    return pl.pallas_call(
        body, name=name, grid=(4, 2),
        in_specs=[col(COL_POOL_V), col(COL_POOL_G), col(0),
                  pl.BlockSpec((None, LANE, LANE), lambda g, k: (g, 0, 0)),
                  pl.BlockSpec((1, LANE), lambda g, k: (0, g)), ANY],
        out_specs=[pl.BlockSpec((s, LANE), lambda g, k: (0, COL_POOL_V + (COL_POOL_G - COL_POOL_V) * k + g)),
                   pl.BlockSpec((None, LANE, LANE), lambda g, k: (g, 0, 0)),
                   pl.BlockSpec((1, LANE), lambda g, k: (0, g))],
        out_shape=[jax.ShapeDtypeStruct(du.shape, BF16),
                   jax.ShapeDtypeStruct((4, LANE, LANE), F32), jax.ShapeDtypeStruct((1, WIDTH), F32)],
        input_output_aliases={5: 0},
        scratch_shapes=[pltpu.VMEM((s, LANE), BF16)],
        compiler_params=pltpu.CompilerParams(dimension_semantics=("arbitrary", "arbitrary")),
    )(u, u, dy, pool_w, pool_scale, du)


def _conv_taps(x_ref, gc_ref, start, n):
    z_ext = _rows_before(gc_ref, start, n, CONV_HALO) * _rows_before(x_ref, start, n, CONV_HALO)
    z0 = z_ext[CONV_HALO:, :]
    z1 = pltpu.roll(z_ext, 1, 0)[CONV_HALO:, :]
    z2 = pltpu.roll(z_ext, 2, 0)[CONV_HALO:, :]
    return z0, z1, z2


def _conv_fwd(u, conv_w, conv_b, name, ts=512):
    s = u.shape[0]

    def body(x_ref, gb_ref, gc_ref, g_ref, w_ref, b_ref, y_ref):
        w0, w1, w2 = w_ref[0:1, :], w_ref[1:2, :], w_ref[2:3, :]
        for c in range(s // ts):
            a = c * ts
            z0, z1, z2 = _conv_taps(x_ref, gc_ref, a, ts)
            y = w2 * z0 + w1 * z1 + w0 * z2 + b_ref[...]
            gate = g_ref[a:a + ts, :].astype(F32)
            y_ref[a:a + ts, :] = (gb_ref[a:a + ts, :].astype(F32) * y * (gate * _sigmoid(gate))).astype(BF16)

    col = lambda base: pl.BlockSpec((s, LANE), lambda j: (0, base + j))
    return pl.pallas_call(
        body, name=name, grid=(4,),
        in_specs=[col(COL_CONV_X), col(COL_CONV_GB), col(COL_CONV_GC), col(COL_CONV_G),
                  pl.BlockSpec((3, LANE), lambda j: (0, j)), pl.BlockSpec((1, LANE), lambda j: (0, j))],
        out_specs=pl.BlockSpec((s, LANE), lambda j: (0, j)),
        out_shape=jax.ShapeDtypeStruct((s, WIDTH), BF16),
    )(u, u, u, u, conv_w, conv_b)


def _conv_bwd(u, dy, conv_w, conv_b, du, name, ts=512):
    s = u.shape[0]

    def body(x_ref, gb_ref, gc_ref, g_ref, dy_ref, w_ref, b_ref, du_in, du_ref, dw_ref, db_ref, held):
        piece = pl.program_id(1)
        pl.when(piece == 0)(lambda: compute(x_ref, gb_ref, gc_ref, g_ref, dy_ref, w_ref, b_ref, du_ref,
                                            held.at[0], held.at[1], held.at[2], dw_ref, db_ref))
        for k in range(3):
            @pl.when(piece == k + 1)
            def _(k=k):
                du_ref[...] = held[k]

    def compute(x_ref, gb_ref, gc_ref, g_ref, dy_ref, w_ref, b_ref,
                dx_ref, dgb_ref, dgc_ref, dg_ref, dw_ref, db_ref):
        w0, w1, w2 = w_ref[0:1, :], w_ref[1:2, :], w_ref[2:3, :]
        acc = [jnp.zeros((1, LANE), F32) for _ in range(4)]
        for c in range(s // ts):
            a = c * ts
            n_ext = ts + CONV_HALO
            gate_e = _rows_after(g_ref, a, ts, CONV_HALO)
            silu_e, dsilu_e = _silu_and_grad(gate_e)
            dy_e = _rows_after(dy_ref, a, ts, CONV_HALO)
            gb_e = _rows_after(gb_ref, a, ts, CONV_HALO)
            dyy_e = dy_e * silu_e * gb_e
            dz = (w2 * dyy_e + w1 * pltpu.roll(dyy_e, n_ext - 1, 0) + w0 * pltpu.roll(dyy_e, n_ext - 2, 0))[0:ts, :]
            z0, z1, z2 = _conv_taps(x_ref, gc_ref, a, ts)
            yb = w2 * z0 + w1 * z1 + w0 * z2 + b_ref[...]
            dyv = dy_e[0:ts, :]
            dyy = dyy_e[0:ts, :]
            dg_ref[a:a + ts, :] = (dyv * gb_e[0:ts, :] * yb * dsilu_e[0:ts, :]).astype(BF16)
            dgb_ref[a:a + ts, :] = (dyv * silu_e[0:ts, :] * yb).astype(BF16)
            dx_ref[a:a + ts, :] = (dz * gc_ref[a:a + ts, :].astype(F32)).astype(BF16)
            dgc_ref[a:a + ts, :] = (dz * x_ref[a:a + ts, :].astype(F32)).astype(BF16)
            for i, term in enumerate((dyy * z2, dyy * z1, dyy * z0, dyy)):
                acc[i] = acc[i] + jnp.sum(term, axis=0, keepdims=True)
        dw_ref[0:1, :] = acc[0]
        dw_ref[1:2, :] = acc[1]
        dw_ref[2:3, :] = acc[2]
        db_ref[...] = acc[3]

    col = lambda base: pl.BlockSpec((s, LANE), lambda j, k: (0, base + j))
    step = COL_CONV_GB - COL_CONV_X
    return pl.pallas_call(
        body, name=name, grid=(4, 4),
        in_specs=[col(COL_CONV_X), col(COL_CONV_GB), col(COL_CONV_GC), col(COL_CONV_G), col(0),
                  pl.BlockSpec((3, LANE), lambda j, k: (0, j)), pl.BlockSpec((1, LANE), lambda j, k: (0, j)), ANY],
        out_specs=[pl.BlockSpec((s, LANE), lambda j, k: (0, COL_CONV_X + step * k + j)),
                   pl.BlockSpec((3, LANE), lambda j, k: (0, j)), pl.BlockSpec((1, LANE), lambda j, k: (0, j))],
        out_shape=[jax.ShapeDtypeStruct(du.shape, BF16),
                   jax.ShapeDtypeStruct((3, WIDTH), F32), jax.ShapeDtypeStruct((1, WIDTH), F32)],
        input_output_aliases={7: 0},
        scratch_shapes=[pltpu.VMEM((3, s, LANE), BF16)],
        compiler_params=pltpu.CompilerParams(dimension_semantics=("arbitrary", "arbitrary")),
    )(u, u, u, u, dy, conv_w, conv_b, du)


LOG2_E = 1.4426950408889634
LN_2 = 0.6931471805599453


def _sb_scores(q_h, k_blk, valid, later_mat, carry):
    z = _dot(q_h, k_blk, NT)
    neg_z = -z
    soft = jnp.log(1.0 + jnp.exp2(jnp.minimum(z, neg_z))) * LOG2_E
    log_keep = jnp.minimum(neg_z, 0.0) - soft
    log_beta = log_keep + z
    if valid is not None:
        log_keep = jnp.where(valid, log_keep, 0.0)
    later = _dot(log_keep.astype(BF16), later_mat, NN) + carry
    return log_keep, log_beta, later


def _masked(valid, x):
    return x if valid is None else jnp.where(valid, x, 0.0)


def _diagonal_masks(tq, tk):
    r = lax.broadcasted_iota(jnp.int32, (tq, tk), 0)
    cidx = lax.broadcasted_iota(jnp.int32, (tq, tk), 1)
    return [cidx + d * tk < r for d in range(tq // tk)]


def _triangle(tk, op):
    r = lax.broadcasted_iota(jnp.int32, (tk, tk), 0)
    cidx = lax.broadcasted_iota(jnp.int32, (tk, tk), 1)
    return op(r, cidx).astype(BF16)


def _split_refs(refs, n_in, n_out, n_scratch, rider):
    r_in = len(rider.inputs) if rider else 0
    r_out = len(rider.out_shape) if rider else 0
    a, b = n_in + r_in, n_in + r_in + n_out + r_out
    own = refs[:n_in] + refs[a:a + n_out] + refs[b:b + n_scratch]
    return own, (refs[n_in:a], refs[a + n_out:b], refs[b + n_scratch:])


def _rider_call_args(rider, n_in, n_out):
    if rider is None:
        return dict(in_specs=[], out_specs=[], out_shape=[], aliases={}, scratch=[], inputs=[])
    return dict(in_specs=[ANY] * len(rider.inputs), out_specs=[ANY] * len(rider.out_shape),
                out_shape=list(rider.out_shape), scratch=list(rider.scratch), inputs=list(rider.inputs),
                aliases={n_in + a: n_out + b for a, b in rider.aliases.items()})


def _ride(rider, phase, when, parts):
    fn = getattr(rider, phase) if rider else None
    if fn is not None:
        pl.when(when)(lambda: fn(*parts))


def _sb_fwd(u, name, t=512, tk=256, pairs=4, rider=None):
    s = u.shape[0]
    assert s // tk <= LANE and 4 % pairs == 0 and t % tk == 0
    scale = HEAD_DIM ** -0.5
    nh = 2 * pairs
    wide = pairs * LANE
    ratio = t // tk
    groups, nq = 4 // pairs, s // t

    def body(*refs):
        own, riding = _split_refs(refs, 4, 3, 4, rider)
        q_ref, k_ref, v_ref, g_ref, o_ref, y_ref, after_ref, kb_ref, vb_ref, acc_ref, carry_ref = own
        grp = pl.program_id(0)
        i = pl.program_id(1)
        _ride(rider, "start", jnp.logical_and(grp == 0, i == 0), riding)
        _ride(rider, "middle", jnp.logical_and(grp == groups - 1, i == (3 * nq) // 4), riding)

        @pl.when(i == 0)
        def _():
            kb_ref[...] = k_ref[...].astype(BF16)
            vb_ref[...] = v_ref[...].astype(BF16)

        lane = lax.broadcasted_iota(jnp.int32, (t, LANE), 1)
        first = lane < HEAD_DIM
        after_ref[...] = jnp.zeros_like(after_ref)
        qv = q_ref[...].astype(F32) * (scale * LOG2_E)
        q_heads = []
        for p in range(pairs):
            qp = qv[:, p * LANE:(p + 1) * LANE]
            q_heads += [jnp.where(first, qp, 0.0).astype(BF16), jnp.where(first, 0.0, qp).astype(BF16)]
        later_mat = _triangle(tk, lambda r, cidx: r > cidx)
        acc_ref[...] = jnp.zeros_like(acc_ref)
        carry_ref[...] = jnp.zeros_like(carry_ref)

        def block(kb, valid, lo=0):
            rows = pl.ds(pl.multiple_of(kb * tk, tk), tk)
            k_blk = kb_ref[rows, :]
            v_blk = vb_ref[rows, :]
            carries = [carry_ref[h, lo:, :] for h in range(nh)]
            afters = [after_ref[lo:, h * LANE:(h + 1) * LANE] for h in range(nh)]
            accs = [acc_ref[h, lo:, :] for h in range(nh)]
            outs = []
            for h in range(nh):
                cols = slice((h // 2) * LANE, (h // 2 + 1) * LANE)
                log_keep, log_beta, later = _sb_scores(q_heads[h][lo:], k_blk[:, cols], valid, later_mat, carries[h])
                a = _masked(valid, jnp.exp2(log_beta + later))
                outs.append((accs[h] + _dot(a.astype(BF16), v_blk[:, cols], NN),
                             carries[h] + jnp.sum(log_keep, axis=1, keepdims=True),
                             jnp.where(lane[lo:] == kb, carries[h], afters[h])))
            for h in range(nh):
                acc_ref[h, lo:, :] = outs[h][0]
                carry_ref[h, lo:, :] = outs[h][1]
                after_ref[lo:, h * LANE:(h + 1) * LANE] = outs[h][2]

        def step(j, _):
            block(ratio * i - 1 - j, None)
            return 0

        masks = _diagonal_masks(t, tk)
        for d in reversed(range(ratio)):
            block(ratio * i + d, masks[d][d * tk:], d * tk)
        lax.fori_loop(0, ratio * i, step, 0)
        for p in range(pairs):
            cols = slice(p * LANE, (p + 1) * LANE)
            o = jnp.where(first, acc_ref[2 * p], acc_ref[2 * p + 1])
            o_ref[:, cols] = o
            gate = g_ref[:, cols].astype(F32)
            y_ref[:, cols] = (o * gate * _sigmoid(gate)).astype(BF16)
        _ride(rider, "finish", jnp.logical_and(grp == groups - 1, i == nq - 1), riding)

    blk = lambda base: pl.BlockSpec((t, wide), lambda g, i: (i, base // pairs + g))
    full = lambda base: pl.BlockSpec((s, wide), lambda g, i: (0, base // pairs + g))
    out_blk = pl.BlockSpec((t, wide), lambda g, i: (i, g))
    extra = _rider_call_args(rider, 4, 3)
    outs = pl.pallas_call(
        body, name=name, grid=(groups, nq),
        in_specs=[blk(COL_SB_Q), full(COL_SB_K), full(COL_SB_V), blk(COL_SB_G)] + extra["in_specs"],
        out_specs=[out_blk, out_blk, pl.BlockSpec((t, nh * LANE), lambda g, i: (i, g))] + extra["out_specs"],
        out_shape=[jax.ShapeDtypeStruct((s, WIDTH), F32), jax.ShapeDtypeStruct((s, WIDTH), BF16),
                   jax.ShapeDtypeStruct((s, 8 * LANE), F32)] + extra["out_shape"],
        input_output_aliases=extra["aliases"],
        scratch_shapes=[pltpu.VMEM((s, wide), BF16), pltpu.VMEM((s, wide), BF16),
                        pltpu.VMEM((nh, t, LANE), F32), pltpu.VMEM((nh, t, 1), F32)] + extra["scratch"],
        compiler_params=pltpu.CompilerParams(dimension_semantics=("arbitrary", "arbitrary")),
    )(u, u, u, u, *extra["inputs"])
    return outs[:3], outs[3:]


def _sb_bwd(u, o, after, dy, name, t=512, tk=256, pairs=2, rider=None):
    s = u.shape[0]
    nq = s // t
    scale = HEAD_DIM ** -0.5
    nh = 2 * pairs
    wide = pairs * LANE
    ratio = t // tk
    groups = 4 // pairs

    def body(*refs):
        own, riding = _split_refs(refs, 7, 4, 6, rider)
        (q_ref, k_ref, v_ref, g_ref, o_ref, after_ref, dy_ref, dq_ref, dk_ref, dv_ref, dg_ref,
         kb_ref, vb_ref, dk_acc, dv_acc, dq_acc, carry_ref) = own
        grp = pl.program_id(0)
        i = pl.program_id(1)
        _ride(rider, "start", jnp.logical_and(grp == 0, i == 0), riding)

        @pl.when(i == 0)
        def _():
            kb_ref[...] = k_ref[...].astype(BF16)
            vb_ref[...] = v_ref[...].astype(BF16)
            dk_acc[...] = jnp.zeros_like(dk_acc)
            dv_acc[...] = jnp.zeros_like(dv_acc)

        lane = lax.broadcasted_iota(jnp.int32, (t, LANE), 1)
        first = lane < HEAD_DIM
        gate = g_ref[...].astype(F32)
        silu, dsilu = _silu_and_grad(gate)
        dyv = dy_ref[...]
        do = dyv * silu
        dg_ref[...] = (dyv * o_ref[...] * dsilu).astype(BF16)
        qv = q_ref[...].astype(F32) * (scale * LOG2_E)
        do_heads, q_heads = [], []
        for p in range(pairs):
            cols = slice(p * LANE, (p + 1) * LANE)
            do_heads += [jnp.where(first, do[:, cols], 0.0).astype(BF16), jnp.where(first, 0.0, do[:, cols]).astype(BF16)]
            q_heads += [jnp.where(first, qv[:, cols], 0.0).astype(BF16), jnp.where(first, 0.0, qv[:, cols]).astype(BF16)]
        later_mat = _triangle(tk, lambda r, cidx: r > cidx)
        before_mat = _triangle(tk, lambda r, cidx: r < cidx)
        dq_acc[...] = jnp.zeros_like(dq_acc)
        carry_ref[...] = jnp.zeros_like(carry_ref)

        def block(kb, valid, lo=0):
            rows = pl.ds(pl.multiple_of(kb * tk, tk), tk)
            k_blk = kb_ref[rows, :]
            v_blk = vb_ref[rows, :]
            carries = [carry_ref[h, lo:, :] for h in range(nh)]
            dq_old = [dq_acc[h, lo:, :] for h in range(nh)]
            dk_old = dk_acc[rows, :]
            dv_old = dv_acc[rows, :]
            outs = []
            for h in range(nh):
                cols = slice((h // 2) * LANE, (h // 2 + 1) * LANE)
                q_h, do_h = q_heads[h][lo:], do_heads[h][lo:]
                after = jnp.sum(jnp.where(lane[lo:] == kb, after_ref[lo:, h * LANE:(h + 1) * LANE], 0.0), axis=1,
                                keepdims=True)
                _, log_beta, later = _sb_scores(q_h, k_blk[:, cols], valid, later_mat, after)
                beta = jnp.exp2(log_beta)
                a = _masked(valid, jnp.exp2(log_beta + later))
                da = _dot(do_h, v_blk[:, cols], NT)
                gterm = a * da
                before = _dot(gterm.astype(BF16), before_mat, NN) + carries[h]
                dz_b = _masked(valid, gterm * (1.0 - beta) - beta * before).astype(BF16)
                outs.append((dq_old[h] + _dot(dz_b, k_blk[:, cols], NN), _dot(dz_b, q_h, TN),
                             _dot(a.astype(BF16), do_h, TN),
                             carries[h] + jnp.sum(gterm, axis=1, keepdims=True)))
            for h in range(nh):
                dq_acc[h, lo:, :] = outs[h][0]
                carry_ref[h, lo:, :] = outs[h][3]
            dk_new = [outs[2 * p][1] + outs[2 * p + 1][1] for p in range(pairs)]
            dv_new = [outs[2 * p][2] + outs[2 * p + 1][2] for p in range(pairs)]
            dk_acc[rows, :] = dk_old + (dk_new[0] if pairs == 1 else jnp.concatenate(dk_new, axis=1))
            dv_acc[rows, :] = dv_old + (dv_new[0] if pairs == 1 else jnp.concatenate(dv_new, axis=1))

        def step(kb, _):
            block(kb, None)
            return 0

        lax.fori_loop(0, ratio * i, step, 0)
        masks = _diagonal_masks(t, tk)
        for d in range(ratio):
            block(ratio * i + d, masks[d][d * tk:], d * tk)
        for p in range(pairs):
            dq_ref[:, p * LANE:(p + 1) * LANE] = (jnp.where(first, dq_acc[2 * p], dq_acc[2 * p + 1]) * scale).astype(BF16)

        @pl.when(i == nq - 1)
        def _():
            dk_ref[...] = (dk_acc[...] * LN_2).astype(BF16)
            dv_ref[...] = dv_acc[...].astype(BF16)

        _ride(rider, "finish", jnp.logical_and(grp == groups - 1, i == nq - 1), riding)

    blk = lambda base: pl.BlockSpec((t, wide), lambda g, i: (i, base // pairs + g))
    full = lambda base: pl.BlockSpec((s, wide), lambda g, i: (0, base // pairs + g))
    out_blk = pl.BlockSpec((t, wide), lambda g, i: (i, g))
    out_full = pl.BlockSpec((s, wide), lambda g, i: (0, g))
    big = jax.ShapeDtypeStruct((s, WIDTH), BF16)
    extra = _rider_call_args(rider, 7, 4)
    outs = pl.pallas_call(
        body, name=name, grid=(groups, nq),
        in_specs=[blk(COL_SB_Q), full(COL_SB_K), full(COL_SB_V), blk(COL_SB_G), out_blk,
                  pl.BlockSpec((t, nh * LANE), lambda g, i: (i, g)), out_blk] + extra["in_specs"],
        out_specs=[out_blk, out_full, out_full, out_blk] + extra["out_specs"],
        out_shape=[big, big, big, big] + extra["out_shape"],
        input_output_aliases=extra["aliases"],
        scratch_shapes=[pltpu.VMEM((s, wide), BF16), pltpu.VMEM((s, wide), BF16),
                        pltpu.VMEM((s, wide), F32), pltpu.VMEM((s, wide), F32),
                        pltpu.VMEM((nh, t, LANE), F32), pltpu.VMEM((nh, t, 1), F32)] + extra["scratch"],
        compiler_params=pltpu.CompilerParams(dimension_semantics=("arbitrary", "arbitrary")),
    )(u, u, u, u, o, after, dy, *extra["inputs"])
    return outs[:4], outs[4:]


def _gate_fwd(u, ys, w_branch, name, ts=256):
    s = u.shape[0]

    def body(m0, m1, m2, y0, y1, y2, w_ref, p0, p1, p2, out_ref):
        tot = None
        for n, (m_ref, y_ref, p_ref) in enumerate(((m0, y0, p0), (m1, y1, p1), (m2, y2, p2))):
            proj = _dot(y_ref[...], w_ref[n], NN)
            p_ref[...] = proj.astype(BF16)
            term = _sigmoid(m_ref[...].astype(F32)) * proj
            tot = term if tot is None else tot + term
        out_ref[...] = tot.astype(BF16)

    mspec = lambda n: pl.BlockSpec((ts, D_MODEL), lambda i: (i, COL_MERGE_1024 + n))
    row = pl.BlockSpec((ts, D_MODEL), lambda i: (i, 0))
    yspec = pl.BlockSpec((ts, WIDTH), lambda i: (i, 0))
    big = jax.ShapeDtypeStruct((s, D_MODEL), BF16)
    outs = pl.pallas_call(
        body, name=name, grid=(s // ts,),
        in_specs=[mspec(0), mspec(1), mspec(2), yspec, yspec, yspec,
                  pl.BlockSpec(w_branch.shape, lambda i: (0, 0, 0))],
        out_specs=[row] * 4, out_shape=[big] * 4,
    )(u, u, u, *ys, w_branch)
    return outs[:3], outs[3]


def _gate_bwd(u, projs, dmerged, w_branch, name, ts=256, rider=None):
    s = u.shape[0]
    steps = s // ts

    def body(*refs):
        own, riding = _split_refs(refs, 6, 7, 0, rider)
        m_ref, p0, p1, p2, dm_ref, w_ref, dp0, dp1, dp2, du_ref, dy0, dy1, dy2 = own
        i, n = pl.program_id(0), pl.program_id(1)
        _ride(rider, "start", jnp.logical_and(i == 0, n == 0), riding)
        dm = dm_ref[...].astype(F32)
        gate = _sigmoid(m_ref[...].astype(F32))
        dp = (dm * gate).astype(BF16)
        slope = dm * gate * (1.0 - gate)
        for k, (p_ref, dp_ref, dy_ref) in enumerate(((p0, dp0, dy0), (p1, dp1, dy1), (p2, dp2, dy2))):
            @pl.when(n == k)
            def _(k=k, p_ref=p_ref, dp_ref=dp_ref, dy_ref=dy_ref):
                dp_ref[...] = dp
                du_ref[...] = (slope * p_ref[...].astype(F32)).astype(BF16)
                dy_ref[...] = _dot(dp, w_ref[k], NT)
        _ride(rider, "finish", jnp.logical_and(i == steps - 1, n == 2), riding)

    row = pl.BlockSpec((ts, D_MODEL), lambda i, n: (i, 0))
    logits = pl.BlockSpec((ts, D_MODEL), lambda i, n: (i, COL_MERGE_1024 + n))
    yspec = pl.BlockSpec((ts, WIDTH), lambda i, n: (i, 0))
    big = jax.ShapeDtypeStruct((s, D_MODEL), BF16)
    extra = _rider_call_args(rider, 6, 7)
    outs = pl.pallas_call(
        body, name=name, grid=(steps, 3),
        in_specs=[logits, row, row, row, row, pl.BlockSpec(w_branch.shape, lambda i, n: (0, 0, 0))] + extra["in_specs"],
        out_specs=[row] * 3 + [logits] + [yspec] * 3 + extra["out_specs"],
        out_shape=[big] * 3 + [jax.ShapeDtypeStruct((s, N_IN), BF16)] + [jax.ShapeDtypeStruct((s, WIDTH), F32)] * 3
        + extra["out_shape"],
        input_output_aliases=extra["aliases"], scratch_shapes=extra["scratch"],
        compiler_params=pltpu.CompilerParams(dimension_semantics=("arbitrary", "arbitrary")),
    )(u, *projs, dmerged, w_branch, *extra["inputs"])
    return outs[:3], outs[3], outs[4:7], outs[7:]


def _as_rows(a):
    return a.reshape(-1, a.shape[-1])


def _row_tile(rows, cols, bytes_per_row_elem=4, cap=1 << 20):
    tr = rows
    while tr * cols * bytes_per_row_elem > cap and tr % 2 == 0 and (tr // 2) % 16 == 0:
        tr //= 2
    return tr


def _cast_bf16(a, name):
    a2 = _as_rows(a)
    rows, cols = a2.shape
    tr = _row_tile(rows, cols)

    def body(a_ref, o_ref):
        o_ref[...] = a_ref[...].astype(BF16)

    spec = pl.BlockSpec((tr, cols), lambda i: (i, 0))
    out = pl.pallas_call(body, name=name, grid=(rows // tr,), in_specs=[spec], out_specs=spec,
                         out_shape=jax.ShapeDtypeStruct((rows, cols), BF16))(a2)
    return out.reshape(a.shape)


def _adamw(w, g, m, v, name):
    shape = w.shape
    w2, g2, m2, v2 = (_as_rows(a) for a in (w, g, m, v))
    rows, cols = w2.shape
    tr = _row_tile(rows, cols)
    c1 = 1.0 - ADAM_B1 ** ADAM_STEP
    c2 = 1.0 - ADAM_B2 ** ADAM_STEP

    def body(w_ref, g_ref, m_ref, v_ref, go_ref, d_ref, nm_ref, nv_ref):
        gv = g_ref[...]
        go_ref[...] = gv
        nm = ADAM_B1 * m_ref[...] + (1.0 - ADAM_B1) * gv
        nv = ADAM_B2 * v_ref[...] + (1.0 - ADAM_B2) * (gv * gv)
        nm_ref[...] = nm
        nv_ref[...] = nv
        d_ref[...] = -ADAM_LR * ((nm / c1) / (jnp.sqrt(nv / c2) + ADAM_EPS) + ADAM_WD * w_ref[...])

    spec = pl.BlockSpec((tr, cols), lambda i: (i, 0))
    sds = jax.ShapeDtypeStruct((rows, cols), F32)
    outs = pl.pallas_call(body, name=name, grid=(rows // tr,), in_specs=[spec] * 4, out_specs=[spec] * 4,
                          out_shape=[sds] * 4)(w2, g2, m2, v2)
    return tuple(o.reshape(shape) for o in outs)


def _sum_slots(a, out_dtype, name):
    n = a.shape[0]
    a3 = a.reshape(n, -1, a.shape[-1])
    _, rows, cols = a3.shape
    tr = _row_tile(rows, cols * n)

    def body(a_ref, o_ref):
        tot = a_ref[0].astype(F32)
        for k in range(1, n):
            tot = tot + a_ref[k].astype(F32)
        o_ref[...] = tot.astype(out_dtype)

    out = pl.pallas_call(
        body, name=name, grid=(rows // tr,),
        in_specs=[pl.BlockSpec((n, tr, cols), lambda i: (0, i, 0))],
        out_specs=pl.BlockSpec((tr, cols), lambda i: (i, 0)),
        out_shape=jax.ShapeDtypeStruct((rows, cols), out_dtype))(a3)
    return out.reshape(a.shape[1:])


def _chip_sum(own, recv, axis, core, name):
    half = recv.shape
    nd = len(half)
    last = nd - 1
    if axis == last:
        tl, nt = half[last], 1
    else:
        tl = min(half[last], 2048)
        nt = half[last] // tl
    block = half[:last] + (tl,)

    def own_index(i, core_ref):
        idx = [0] * nd
        idx[last] = i
        if axis == last:
            idx[last] = core_ref[0]
        else:
            idx[axis] = core_ref[0]
        return tuple(idx)

    def recv_index(i, core_ref):
        idx = [0] * nd
        idx[last] = i
        return tuple(idx)

    def body(core_ref, own_ref, recv_ref, o_ref):
        o_ref[...] = (own_ref[...].astype(F32) + recv_ref[...].astype(F32)).astype(BF16)

    return pl.pallas_call(
        body, name=name,
        grid_spec=pltpu.PrefetchScalarGridSpec(
            num_scalar_prefetch=1, grid=(nt,),
            in_specs=[pl.BlockSpec(block, own_index), pl.BlockSpec(block, recv_index)],
            out_specs=pl.BlockSpec(block, recv_index)),
        out_shape=jax.ShapeDtypeStruct(half, BF16),
    )(core, own, recv)


def _mesh_position():
    return lax.axis_index("x"), lax.axis_index("y"), lax.axis_index("c")


def _other_chips(x, y):
    return [(1 - x, y), (x, 1 - y), (1 - x, 1 - y)]


ALL_FLIPS = [(0, 0, 1), (1, 0, 0), (0, 1, 0), (1, 1, 0), (1, 0, 1), (0, 1, 1), (1, 1, 1)]


def _half(ref, axis, which, size):
    idx = [slice(None)] * len(ref.shape)
    idx[axis] = pl.ds(which * size, size)
    return ref.at[tuple(idx)]


def _sub(ref, picks):
    idx = [slice(None)] * len(ref.shape)
    for axis, start, size in picks:
        idx[axis] = pl.ds(start, size)
    return ref.at[tuple(idx)]


def _remote(src, dst, sems_send, sems_recv, k, to):
    return pltpu.make_async_remote_copy(src_ref=src, dst_ref=dst, send_sem=sems_send.at[k], recv_sem=sems_recv.at[k],
                                        device_id=to, device_id_type=MESH)


def _cast_shard(w, layer, shard_axis, pos, name, tr=512):
    shape = w.shape[1:]
    nd = len(shape)
    assert shard_axis in (nd - 1, nd - 2)
    rows, cols = shape[-2:]
    tr = min(tr, rows)
    nt = rows // tr
    lead = shape[:-2]
    full = list(shape)
    full[shard_axis] *= N_CHIPS
    block = (1,) * len(lead) + (tr, cols)

    def in_index(*args):
        return (layer, *args[:-1], 0)

    def out_index(*args):
        *g, pos_ref = args
        if shard_axis == nd - 1:
            return (*g, pos_ref[1])
        return (*g[:-1], pos_ref[1] * nt + g[-1], 0)

    def body(pos_ref, a_ref, o_ref):
        o_ref[...] = a_ref[...].astype(BF16)

    return pl.pallas_call(
        body, name=name,
        grid_spec=pltpu.PrefetchScalarGridSpec(
            num_scalar_prefetch=1, grid=lead + (nt,),
            in_specs=[pl.BlockSpec((None,) + block, in_index)], out_specs=pl.BlockSpec(block, out_index)),
        out_shape=jax.ShapeDtypeStruct(tuple(full), BF16),
    )(pos, w)


class _Rider:
    def __init__(self, inputs, out_shape, aliases, scratch, start, middle, finish):
        self.inputs, self.out_shape, self.aliases, self.scratch = inputs, out_shape, aliases, scratch
        self.start, self.middle, self.finish = start, middle, finish


def _weight_gather_rider(fulls, layout):
    n = len(fulls)

    def copies(outs, sems):
        send_sems, recv_sems = sems
        x, y, c = _mesh_position()
        chips = _other_chips(x, y)
        sibling = (x, y, 1 - c)
        mine = 2 * x + y

        def place(t, chip, core):
            sh_axis, sh_size, half_axis, half_size = layout[t]
            return _sub(outs[t], [(sh_axis, chip * sh_size, sh_size), (half_axis, core * half_size, half_size)])

        direct, arrive, forward, arrive_fwd = [], [], [], []
        for t in range(n):
            for k, (px, py) in enumerate(chips):
                theirs = 2 * px + py
                direct.append(_remote(place(t, mine, c), place(t, mine, c), send_sems, recv_sems, 6 * t + k, (px, py, c)))
                arrive.append(_remote(place(t, theirs, c), place(t, theirs, c), send_sems, recv_sems, 6 * t + k, (px, py, c)))
                forward.append(_remote(place(t, theirs, c), place(t, theirs, c), send_sems, recv_sems, 6 * t + 3 + k, sibling))
                arrive_fwd.append(_remote(place(t, theirs, 1 - c), place(t, theirs, 1 - c), send_sems, recv_sems,
                                          6 * t + 3 + k, sibling))
        return direct, arrive, forward, arrive_fwd

    def start(ins, outs, sems):
        for cp in copies(outs, sems)[0]:
            cp.start()

    def middle(ins, outs, sems):
        _, arrive, forward, _ = copies(outs, sems)
        for a, f in zip(arrive, forward):
            a.wait_recv()
            f.start()

    def finish(ins, outs, sems):
        direct, _, forward, arrive_fwd = copies(outs, sems)
        for cp in arrive_fwd:
            cp.wait_recv()
        for cp in direct + forward:
            cp.wait_send()

    return _Rider(list(fulls), [jax.ShapeDtypeStruct(a.shape, a.dtype) for a in fulls], {k: k for k in range(n)},
                  [pltpu.SemaphoreType.DMA((6 * n,)), pltpu.SemaphoreType.DMA((6 * n,))], start, middle, finish)


WEIGHT_LAYOUT = [(1, 2048, 0, 512), (2, 256, 1, 256), (0, 256, 1, 512)]


def _gather_weights(fulls, layout, conv_w):
    rider = _weight_gather_rider(fulls, layout)
    n = len(fulls)

    def body(*refs):
        cw, outs, cw_f = refs[n], refs[n + 1:2 * n + 1], refs[2 * n + 1]
        sems, (cw_send, cw_recv, local_sem) = refs[2 * n + 2:2 * n + 4], refs[2 * n + 4:]
        x, y, c = _mesh_position()
        chips = _other_chips(x, y)
        mine = 2 * x + y
        local = pltpu.make_async_copy(cw, cw_f.at[mine], local_sem.at[0])
        local.start()
        rider.start(None, outs, sems)
        small = [_remote(cw, cw_f.at[mine], cw_send, cw_recv, k, (*chip, c)) for k, chip in enumerate(chips)]
        for cp in small:
            cp.start()
        rider.middle(None, outs, sems)
        rider.finish(None, outs, sems)
        for k, (px, py) in enumerate(chips):
            _remote(cw, cw_f.at[2 * px + py], cw_send, cw_recv, k, (px, py, c)).wait_recv()
        for cp in small:
            cp.wait_send()
        local.wait()

    outs = pl.pallas_call(
        body, name="gather_weights",
        in_specs=[ANY] * (n + 1), out_specs=[ANY] * (n + 1),
        out_shape=rider.out_shape + [jax.ShapeDtypeStruct((N_CHIPS,) + conv_w.shape, F32)],
        input_output_aliases=rider.aliases,
        scratch_shapes=rider.scratch + [pltpu.SemaphoreType.DMA((3,)), pltpu.SemaphoreType.DMA((3,)),
                                        pltpu.SemaphoreType.DMA((1,))],
    )(*fulls, conv_w)
    return outs[:n], outs[n]


def _swap_rider(items):
    n = len(items)
    halves = []
    for a, axis in items:
        shp = list(a.shape)
        shp[axis] //= 2
        halves.append(tuple(shp))

    def copies(ins, outs, sems):
        x, y, c = _mesh_position()
        return [_remote(_half(ins[k], items[k][1], 1 - c, halves[k][items[k][1]]), outs[k], sems[0], sems[1], k,
                        (x, y, 1 - c)) for k in range(n)]

    def start(ins, outs, sems):
        for cp in copies(ins, outs, sems):
            cp.start()

    def finish(ins, outs, sems):
        for cp in copies(ins, outs, sems):
            cp.wait()

    return _Rider([a for a, _ in items], [jax.ShapeDtypeStruct(h, a.dtype) for h, (a, _) in zip(halves, items)], {},
                  [pltpu.SemaphoreType.DMA((n,)), pltpu.SemaphoreType.DMA((n,))], start, None, finish)


def _swap_halves(items, name):
    rider = _swap_rider(items)
    n = len(items)

    def body(*refs):
        parts = (refs[:n], refs[n:2 * n], refs[2 * n:])
        rider.start(*parts)
        rider.finish(*parts)

    return pl.pallas_call(
        body, name=name, in_specs=[ANY] * n, out_specs=[ANY] * n, out_shape=rider.out_shape,
        scratch_shapes=rider.scratch,
    )(*rider.inputs)


def _grad_exchange_rider(items):
    n = len(items)
    slices = []
    for a, axis in items:
        shp = list(a.shape)
        shp[axis] //= N_CHIPS
        slices.append(tuple(shp))

    def copies(ins, outs, sems):
        send_sems, recv_sems = sems
        x, y, c = _mesh_position()
        made = []
        for k in range(n):
            axis = items[k][1]
            for r, (px, py) in enumerate(_other_chips(x, y)):
                made.append(_remote(_half(ins[k], axis, 2 * px + py, slices[k][axis]), outs[k].at[r],
                                    send_sems, recv_sems, 3 * k + r, (px, py, c)))
        return made

    def start(ins, outs, sems):
        for cp in copies(ins, outs, sems):
            cp.start()

    def finish(ins, outs, sems):
        for cp in copies(ins, outs, sems):
            cp.wait()

    return _Rider([a for a, _ in items], [jax.ShapeDtypeStruct((N_CHIPS - 1,) + s, BF16) for s in slices], {},
                  [pltpu.SemaphoreType.DMA((3 * n,)), pltpu.SemaphoreType.DMA((3 * n,))], start, None, finish)


def _small_gather_rider(small):
    def copies(ins, outs, sems):
        x, y, c = _mesh_position()
        me = 4 * x + 2 * y + c
        local = pltpu.make_async_copy(ins[0], outs[0].at[me], sems[2].at[0])
        remote = [_remote(ins[0], outs[0].at[me], sems[0], sems[1], r, (x ^ fx, y ^ fy, c ^ fc))
                  for r, (fx, fy, fc) in enumerate(ALL_FLIPS)]
        return local, remote

    def start(ins, outs, sems):
        local, remote = copies(ins, outs, sems)
        local.start()
        for cp in remote:
            cp.start()

    def finish(ins, outs, sems):
        local, remote = copies(ins, outs, sems)
        for cp in remote:
            cp.wait()
        local.wait()

    n = len(ALL_FLIPS)
    return _Rider([small], [jax.ShapeDtypeStruct((2 * N_CHIPS,) + small.shape, F32)], {},
                  [pltpu.SemaphoreType.DMA((n,)), pltpu.SemaphoreType.DMA((n,)), pltpu.SemaphoreType.DMA((1,))],
                  start, None, finish)


def _sum_chips(recv, own, shard_axis, split_axis, pos, dest, layer, name, tr=128):
    sl = recv.shape[1:]
    nd = len(sl)
    tiled = nd == 2 and sl[0] > tr
    nt = sl[0] // tr if tiled else 1
    block = ((tr,) + sl[1:]) if tiled else sl
    shard = list(sl)
    shard[split_axis] *= 2

    def recv_index(i, pos_ref):
        return (0, i) + (0,) * (nd - 1) if tiled else (0,) * (nd + 1)

    def own_index(i, pos_ref):
        idx = [0] * nd
        idx[shard_axis] = pos_ref[1]
        if tiled:
            idx[0] = pos_ref[1] * nt + i if shard_axis == 0 else i
        return tuple(idx)

    def out_index(i, pos_ref):
        idx = [0] * nd
        idx[split_axis] = pos_ref[0]
        if tiled:
            idx[0] = pos_ref[0] * nt + i if split_axis == 0 else i
        return (layer, *idx)

    def body(pos_ref, recv_ref, own_ref, *rest):
        o_ref = rest[-1]
        tot = own_ref[...].astype(F32)
        for k in range(N_CHIPS - 1):
            tot = tot + recv_ref[k].astype(F32)
        o_ref[0] = tot

    in_specs = [pl.BlockSpec((N_CHIPS - 1,) + block, recv_index), pl.BlockSpec(block, own_index)]
    args = [pos, recv, own]
    aliases = {}
    if dest is not None:
        in_specs.append(ANY)
        args.append(dest)
        aliases = {3: 0}
    return pl.pallas_call(
        body, name=name,
        grid_spec=pltpu.PrefetchScalarGridSpec(
            num_scalar_prefetch=1, grid=(nt,), in_specs=in_specs,
            out_specs=pl.BlockSpec((1,) + block, out_index)),
        out_shape=jax.ShapeDtypeStruct((DEPTH,) + tuple(shard), F32),
        input_output_aliases=aliases,
    )(*args)


def _share_halves(bufs, late_small, name):
    n = len(bufs)
    small = _small_gather_rider(late_small)

    def body(*refs):
        outs, (send_sems, recv_sems) = refs[n + 1:2 * n + 1], refs[2 * n + 2:2 * n + 4]
        small_parts = ([refs[n]], [refs[2 * n + 1]], refs[2 * n + 4:])
        x, y, c = _mesh_position()
        small.start(*small_parts)
        copies = []
        for k, (a, axis) in enumerate(bufs):
            size = a.shape[1 + axis] // 2
            mine = _half(outs[k], 1 + axis, c, size)
            copies.append(_remote(mine, mine, send_sems, recv_sems, k, (x, y, 1 - c)))
        for cp in copies:
            cp.start()
        for cp in copies:
            cp.wait()
        small.finish(*small_parts)

    outs = pl.pallas_call(
        body, name=name, in_specs=[ANY] * (n + 1), out_specs=[ANY] * (n + 1),
        out_shape=[jax.ShapeDtypeStruct(a.shape, F32) for a, _ in bufs] + small.out_shape,
        input_output_aliases={k: k for k in range(n)},
        scratch_shapes=[pltpu.SemaphoreType.DMA((n,)), pltpu.SemaphoreType.DMA((n,))] + small.scratch,
    )(*[a for a, _ in bufs], late_small)
    return outs[:n], outs[n]


def _layer_fwd(x, p, l, rider=None, proj_rider=None):
    tag = f"l{l}_"
    h = _rms_fwd(x, p["pre_g"], tag + "pre_norm")
    u = _matmul(h, p["w_in"], "nn", BF16, tag + "in_proj", rider=proj_rider)
    if proj_rider is not None:
        u, (w_branch, w_out) = u
        p = dict(p, w_branch=w_branch, w_out=w_out)
    y_pool = _pool_fwd(u, p["pool_w"], p["pool_scale"], tag + "pool")
    y_conv = _conv_fwd(u, p["conv_w"], p["conv_b"], tag + "conv")
    (o_sb, y_sb, sb_after), carried = _sb_fwd(u, tag + "stickbreak", rider=rider)
    ys = [y_pool, y_conv, y_sb]
    projs, merged = _gate_fwd(u, ys, p["w_branch"], tag + "merge")
    out = _matmul(merged, p["w_out"], "nn", F32, tag + "out_proj")
    saved = dict(x=x, h=h, u=u, ys=ys, o_sb=o_sb, sb_after=sb_after, projs=projs, merged=merged, out=out)
    return out, saved, carried, p


def _layer_bwd(dy, p, saved, l, merge_rider=None, early=None, before_dw=None, late=None):
    tag = f"l{l}_bwd_"
    u = saved["u"]
    d_out, g_post = _rms_bwd(saved["out"], p["post_g"], dy, None, BF16, tag + "post_norm")
    d_merged = _matmul(d_out, p["w_out"], "nt", BF16, tag + "out_proj_dx")
    g_w_out = _matmul(saved["merged"], d_out, "tn", BF16, tag + "out_proj_dw", tk=2048)
    d_projs, du, d_ys, carried_merge = _gate_bwd(u, saved["projs"], d_merged, p["w_branch"], tag + "merge",
                                                 rider=merge_rider)
    g_w_branch = jnp.stack([_matmul(saved["ys"][n], d_projs[n], "tn", BF16, tag + f"branch_dw{n}", tk=2048)
                            for n in range(3)])
    rider = early(g_w_branch, g_w_out, carried_merge) if early else None
    du, g_pool_w, g_pool_scale = _pool_bwd(u, d_ys[0], p["pool_w"], p["pool_scale"], du, tag + "pool")
    du, g_conv_w, g_conv_b = _conv_bwd(u, d_ys[1], p["conv_w"], p["conv_b"], du, tag + "conv")
    attn_pieces, carried_attn = _sb_bwd(u, saved["o_sb"], saved["sb_after"], d_ys[2], tag + "stickbreak", rider=rider)
    for piece, col in zip(attn_pieces, (COL_SB_Q, COL_SB_K, COL_SB_V, COL_SB_G)):
        du = lax.dynamic_update_slice(du, piece, (0, col * LANE))
    grads = dict(w_branch=g_w_branch, w_out=g_w_out, post_g=g_post, pool_w=g_pool_w, pool_scale=g_pool_scale,
                 conv_w=g_conv_w, conv_b=g_conv_b)
    rider = before_dw(grads) if before_dw else None
    g_w_in = _matmul(saved["h"], du, "tn", BF16, tag + "in_proj_dw", tk=2048, rider=rider)
    g_w_in, carried_dw = g_w_in if rider else (g_w_in, [])
    rider = late(g_w_in) if late else None
    dh = _matmul(du, p["w_in"], "nt", BF16, tag + "in_proj_dx", tk=2048, rider=rider)
    dh, carried_dx = dh if rider else (dh, [])
    dx, g_pre = _rms_bwd(saved["x"], p["pre_g"], dh, dy, F32, tag + "pre_norm")
    grads.update(w_in=g_w_in, pre_g=g_pre)
    return dx, grads, carried_attn, carried_dw, carried_dx


SMALL_ORDER = ["pre_g", "pool_w", "pool_scale", "conv_w", "conv_b", "post_g"]


def _pack_small(per_layer):
    parts, spans, at = [], {}, 0
    for name in SMALL_ORDER:
        a = jnp.stack([per_layer[l][name] for l in range(DEPTH)]).reshape(-1, LANE)
        parts.append(a)
        spans[name] = (at, a.shape[0])
        at += a.shape[0]
    return jnp.concatenate(parts, axis=0), spans


def kernel(x, pre_norm_g, w_in, pool_w, pool_scale, conv_w, conv_b, w_branch, w_out, post_norm_g, loss_target, m_pre_norm_g, m_w_in, m_pool_w, m_pool_scale, m_conv_w, m_conv_b, m_w_branch, m_w_out, m_post_norm_g, v_pre_norm_g, v_w_in, v_pool_w, v_pool_scale, v_conv_w, v_conv_b, v_w_branch, v_w_out, v_post_norm_g):
    mx, my, mc = _mesh_position()
    chip = 2 * mx + my
    core = mc.astype(jnp.int32).reshape(1)
    pos = jnp.stack([mc, chip]).astype(jnp.int32)

    names = ["w_in", "w_branch", "w_out"]
    given = dict(w_in=w_in, w_branch=w_branch, w_out=w_out)
    in_place = [[_cast_shard(given[n], l, WEIGHT_LAYOUT[i][0], pos, f"cast_{n}{l}") for i, n in enumerate(names)]
                for l in range(DEPTH)]
    (w_in_0,), conv_w_by_chip = _gather_weights(in_place[0][:1], WEIGHT_LAYOUT[:1], conv_w)
    gathered = [w_in_0] + in_place[0][1:]
    conv_w_f = conv_w_by_chip.transpose(1, 2, 0, 3).reshape(DEPTH, 3, WIDTH)
    pool_w_b = _cast_bf16(pool_w, "cast_pool_w")

    def layer_params(l, big):
        return dict(pre_g=pre_norm_g[l:l + 1], post_g=post_norm_g[l:l + 1], w_in=big[0], w_branch=big[1],
                    w_out=big[2], pool_w=pool_w_b[l], pool_scale=pool_scale[l:l + 1], conv_w=conv_w_f[l],
                    conv_b=conv_b[l:l + 1])

    act = x[0]
    params, saved = [], []
    for l in range(DEPTH):
        rider = _weight_gather_rider(in_place[l + 1], WEIGHT_LAYOUT) if l + 1 < DEPTH else None
        proj_rider = _weight_gather_rider(gathered[1:], WEIGHT_LAYOUT[1:]) if l == 0 else None
        out, sv, gathered, layer_p = _layer_fwd(act, layer_params(l, gathered), l, rider, proj_rider)
        params.append(layer_p)
        saved.append(sv)
        if l < DEPTH - 1:
            act = _resid_out(act, out, params[l]["post_g"], None, f"l{l}_resid")
    dy, loss_part = _resid_out(act, saved[-1]["out"], params[-1]["post_g"], loss_target[0], "loss_head")

    split_axis = dict(w_in=0, w_branch=1, w_out=1)
    shard_axis = dict(w_in=1, w_branch=2, w_out=0)
    grads = [None] * DEPTH
    chip_sums = [dict() for _ in range(DEPTH)]
    by_chip = [dict() for _ in range(DEPTH)]

    def reduce_in_chip(l, which, g):
        items = [(g[n], split_axis[n]) for n in which]
        from_sibling = _swap_halves(items, f"swap_grad_halves{l}_{which[0]}")
        for n, (a, axis), r in zip(which, items, from_sibling):
            chip_sums[l][n] = _chip_sum(a, r, axis, core, f"chip_sum{l}_{n}")

    def exchange_rider(keys):
        return _grad_exchange_rider([(chip_sums[l][n], shard_axis[n]) for l, n in keys])

    waiting = []
    for l in reversed(range(DEPTH)):
        sent_early, sent_late = list(waiting) + [(l, "w_branch"), (l, "w_out")], [(l, "w_in")]
        waiting_items = [(grads[ll][n], split_axis[n]) for ll, n in waiting]

        def early(g_w_branch, g_w_out, from_sibling, l=l, keys=sent_early, above=tuple(waiting), items=waiting_items):
            for (ll, n), (a, axis), r in zip(above, items, from_sibling):
                chip_sums[ll][n] = _chip_sum(a, r, axis, core, f"chip_sum{ll}_{n}")
            reduce_in_chip(l, ["w_branch", "w_out"], dict(w_branch=g_w_branch, w_out=g_w_out))
            return exchange_rider(keys)

        def late(g_w_in, l=l, keys=sent_late):
            reduce_in_chip(l, ["w_in"], dict(w_in=g_w_in))
            return exchange_rider(keys)

        def before_dw(partial, l=l):
            layers = [dict(partial, pre_g=jnp.zeros_like(pre_norm_g[:1])) if ll == l else grads[ll]
                      for ll in range(DEPTH)]
            return _small_gather_rider(_pack_small(layers)[0])

        if l == DEPTH - 1:
            dy, grads[l], _, _, _ = _layer_bwd(dy, params[l], saved[l], l)
            waiting = [(l, n) for n in names]
        else:
            dy, grads[l], got_early, got_dw, got_late = _layer_bwd(
                dy, params[l], saved[l], l, _swap_rider(waiting_items), early, before_dw if l == 0 else None, late)
            for (ll, n), r in zip(sent_early + sent_late, list(got_early) + list(got_late)):
                by_chip[ll][n] = r
            if l == 0:
                small_all = got_dw[0]
            waiting = []
    assert not waiting
    grad_x = dy[None]
    _, spans = _pack_small(grads)
    late_small = jnp.concatenate([grads[0]["pre_g"].reshape(-1, LANE), jnp.broadcast_to(loss_part, (8, LANE))])
    bufs = []
    for n in names:
        dest = None
        for l in range(DEPTH):
            dest = _sum_chips(by_chip[l][n], chip_sums[l][n], shard_axis[n], split_axis[n], pos, dest, l,
                              f"sum_chips{l}_{n}")
        bufs.append((dest, split_axis[n]))
    (g_w_in, g_w_branch, g_w_out), late_all = _share_halves(bufs, late_small, "share_grad_halves")

    late_sum = _sum_slots(late_all, F32, "sum_late_small")
    n_gain = late_small.shape[0] - 8
    loss = late_sum[n_gain, 0]
    small_sum = _sum_slots(small_all, F32, "sum_small")
    at, _ = spans["pre_g"]
    small_sum = jnp.concatenate([small_sum[:at], late_sum[:n_gain], small_sum[at + n_gain:]])
    small = {}
    for name, like in (("pre_g", pre_norm_g), ("pool_w", pool_w), ("pool_scale", pool_scale), ("conv_b", conv_b),
                       ("post_g", post_norm_g)):
        at, n = spans[name]
        small[name] = small_sum[at:at + n].reshape(like.shape)
    at, n = spans["conv_w"]
    g_conv_w_full = small_sum[at:at + n].reshape(DEPTH, 3, WIDTH)
    g_conv_w = lax.dynamic_slice_in_dim(g_conv_w_full, chip * conv_w.shape[2], conv_w.shape[2], axis=2)

    g = dict(pre_norm_g=small["pre_g"], w_in=g_w_in, pool_w=small["pool_w"], pool_scale=small["pool_scale"],
             conv_w=g_conv_w, conv_b=small["conv_b"], w_branch=g_w_branch, w_out=g_w_out, post_norm_g=small["post_g"])
    w = dict(pre_norm_g=pre_norm_g, w_in=w_in, pool_w=pool_w, pool_scale=pool_scale, conv_w=conv_w, conv_b=conv_b,
             w_branch=w_branch, w_out=w_out, post_norm_g=post_norm_g)
    m = dict(pre_norm_g=m_pre_norm_g, w_in=m_w_in, pool_w=m_pool_w, pool_scale=m_pool_scale, conv_w=m_conv_w,
             conv_b=m_conv_b, w_branch=m_w_branch, w_out=m_w_out, post_norm_g=m_post_norm_g)
    v = dict(pre_norm_g=v_pre_norm_g, w_in=v_w_in, pool_w=v_pool_w, pool_scale=v_pool_scale, conv_w=v_conv_w,
             conv_b=v_conv_b, w_branch=v_w_branch, w_out=v_w_out, post_norm_g=v_post_norm_g)
    order = ["pre_norm_g", "w_in", "pool_w", "pool_scale", "conv_w", "conv_b", "w_branch", "w_out", "post_norm_g"]
    upd = {n: _adamw(w[n], g[n], m[n], v[n], "adamw_" + n) for n in order}
    return (loss, grad_x, *[upd[n][0] for n in order], *[upd[n][1] for n in order], *[upd[n][2] for n in order],
            *[upd[n][3] for n in order])
```

```python
import functools

import jax
import jax.numpy as jnp
from jax import lax
from jax.experimental import pallas as pl
from jax.experimental.pallas import tpu as pltpu

F32 = jnp.float32
BF16 = jnp.bfloat16
MESH = pl.DeviceIdType.MESH
ANY = pl.BlockSpec(memory_space=pl.ANY)

DEPTH = 2
D_MODEL = 1024
WIDTH = 512
N_IN = 8192
N_CHIPS = 4
HEAD_DIM = 64
RMS_EPS = 1e-6
POOL_HALO = 16
CONV_HALO = 16
LANE = 128
COL_POOL_V, COL_POOL_G = 0, 4
COL_CONV_X, COL_CONV_GB, COL_CONV_GC, COL_CONV_G = 8, 12, 16, 20
COL_SB_Q, COL_SB_K, COL_SB_V, COL_SB_G = 24, 28, 32, 36
COL_MERGE_1024 = 5

ADAM_LR, ADAM_B1, ADAM_B2, ADAM_EPS, ADAM_WD, ADAM_STEP = 0.001, 0.9, 0.999, 1e-08, 0.01, 10

NN = (((1,), (0,)), ((), ()))
NT = (((1,), (1,)), ((), ()))
TN = (((0,), (0,)), ((), ()))


def _sigmoid(x):
    return 1.0 / (1.0 + jnp.exp(-x))


def _silu_and_grad(x):
    s = _sigmoid(x)
    return x * s, s * (1.0 + x * (1.0 - s))


def _dot(a, b, dims):
    return lax.dot_general(a, b, dims, preferred_element_type=F32)


def _matmul(a, b, mode, out_dtype, name, tm=1024, tn=1024, tk=1024, b_lead=(), rider=None):
    b_shape = b.shape[len(b_lead):]
    if mode == "nn":
        (m, k), (k2, n) = a.shape, b_shape
    elif mode == "nt":
        (m, k), (n, k2) = a.shape, b_shape
    else:
        (k, m), (k2, n) = a.shape, b_shape
    assert k == k2 and a.dtype == BF16 and b.dtype == BF16
    tm, tn, tk = min(tm, m), min(tn, n), min(tk, k)
    assert m % tm == 0 and n % tn == 0 and k % tk == 0
    nk = k // tk
    dims = {"nn": NN, "nt": NT, "tn": TN}[mode]

    grid = (m // tm, n // tn, nk)

    def at_step(step):
        return functools.reduce(jnp.logical_and, [pl.program_id(d) == s for d, s in enumerate(step)])

    def body(*refs):
        (a_ref, b_ref, o_ref, *scratch), riding = _split_refs(refs, 2, 1, 1 if nk > 1 else 0, rider)
        _ride(rider, "start", at_step((0, 0, 0)), riding)
        _ride(rider, "middle", at_step(((3 * grid[0]) // 4, 0, 0)), riding)
        compute(a_ref, b_ref, o_ref, scratch)
        _ride(rider, "finish", at_step([g - 1 for g in grid]), riding)

    def compute(a_ref, b_ref, o_ref, scratch):
        p = _dot(a_ref[...], b_ref[...], dims)
        if nk == 1:
            o_ref[...] = p.astype(o_ref.dtype)
        else:
            acc = scratch[0]
            kk = pl.program_id(2)

            @pl.when(kk == 0)
            def _():
                acc[...] = p

            @pl.when(jnp.logical_and(kk > 0, kk < nk - 1))
            def _():
                acc[...] += p

            @pl.when(kk == nk - 1)
            def _():
                o_ref[...] = (acc[...] + p).astype(o_ref.dtype)

    if mode == "tn":
        a_spec = pl.BlockSpec((tk, tm), lambda i, j, kk: (kk, i))
    else:
        a_spec = pl.BlockSpec((tm, tk), lambda i, j, kk: (i, kk))
    squeezed = (None,) * len(b_lead)
    if mode == "nt":
        b_spec = pl.BlockSpec(squeezed + (tn, tk), lambda i, j, kk: (*b_lead, j, kk))
    else:
        b_spec = pl.BlockSpec(squeezed + (tk, tn), lambda i, j, kk: (*b_lead, kk, j))
    extra = _rider_call_args(rider, 2, 1)
    outs = pl.pallas_call(
        body, name=name, grid=grid,
        in_specs=[a_spec, b_spec] + extra["in_specs"],
        out_specs=[pl.BlockSpec((tm, tn), lambda i, j, kk: (i, j))] + extra["out_specs"],
        out_shape=[jax.ShapeDtypeStruct((m, n), out_dtype)] + extra["out_shape"],
        input_output_aliases=extra["aliases"],
        scratch_shapes=([pltpu.VMEM((tm, tn), F32)] if nk > 1 else []) + extra["scratch"],
        compiler_params=pltpu.CompilerParams(dimension_semantics=("arbitrary",) * 3 if rider else
                                             ("parallel", "parallel", "arbitrary")),
    )(a, b, *extra["inputs"])
    return (outs[0], outs[1:]) if rider else outs[0]


def _rms_fwd(x, g, name, ts=512):
    s, d = x.shape

    def body(x_ref, g_ref, h_ref):
        xv = x_ref[...]
        r = lax.rsqrt(jnp.mean(xv * xv, axis=-1, keepdims=True) + RMS_EPS)
        h_ref[...] = (xv * r * g_ref[...]).astype(BF16)

    return pl.pallas_call(
        body, name=name, grid=(s // ts,),
        in_specs=[pl.BlockSpec((ts, d), lambda i: (i, 0)), pl.BlockSpec((1, d), lambda i: (0, 0))],
        out_specs=pl.BlockSpec((ts, d), lambda i: (i, 0)),
        out_shape=jax.ShapeDtypeStruct((s, d), BF16),
    )(x, g)


def _rms_bwd(xin, g, dh, resid, out_dtype, name, ts=512):
    s, d = xin.shape
    has_resid = resid is not None

    def body(*refs):
        if has_resid:
            x_ref, g_ref, dh_ref, res_ref, dx_ref, dg_ref = refs
        else:
            x_ref, g_ref, dh_ref, dx_ref, dg_ref = refs
        xv = x_ref[...]
        dhv = dh_ref[...].astype(F32)
        r = lax.rsqrt(jnp.mean(xv * xv, axis=-1, keepdims=True) + RMS_EPS)
        nrm = xv * r
        dn = dhv * g_ref[...]
        dx = r * (dn - nrm * jnp.mean(dn * nrm, axis=-1, keepdims=True))
        if has_resid:
            dx = dx + res_ref[...]
        dx_ref[...] = dx.astype(dx_ref.dtype)
        part = jnp.sum(dhv * nrm, axis=0, keepdims=True)

        @pl.when(pl.program_id(0) == 0)
        def _():
            dg_ref[...] = part

        @pl.when(pl.program_id(0) > 0)
        def _():
            dg_ref[...] += part

    row = pl.BlockSpec((ts, d), lambda i: (i, 0))
    vec = pl.BlockSpec((1, d), lambda i: (0, 0))
    ins = [xin, g, dh] + ([resid] if has_resid else [])
    return pl.pallas_call(
        body, name=name, grid=(s // ts,),
        in_specs=[row, vec, row] + ([row] if has_resid else []),
        out_specs=[row, vec],
        out_shape=[jax.ShapeDtypeStruct((s, d), out_dtype), jax.ShapeDtypeStruct((1, d), F32)],
        compiler_params=pltpu.CompilerParams(dimension_semantics=("arbitrary",)),
    )(*ins)


def _resid_out(x, out, g, target, name, ts=512):
    s, d = x.shape
    has_loss = target is not None

    def body(*refs):
        if has_loss:
            x_ref, o_ref, g_ref, t_ref, dy_ref, loss_ref = refs
        else:
            x_ref, o_ref, g_ref, y_ref = refs
        ov = o_ref[...]
        r = lax.rsqrt(jnp.mean(ov * ov, axis=-1, keepdims=True) + RMS_EPS)
        yv = x_ref[...] + ov * r * g_ref[...]
        if not has_loss:
            y_ref[...] = yv
            return
        err = yv - t_ref[...]
        dy_ref[...] = err * (1.0 / d)
        part = jnp.sum(jnp.sum(err * err, axis=-1, keepdims=True), axis=0, keepdims=True) * (0.5 / d)
        part = jnp.broadcast_to(part, (1, LANE))

        @pl.when(pl.program_id(0) == 0)
        def _():
            loss_ref[...] = part

        @pl.when(pl.program_id(0) > 0)
        def _():
            loss_ref[...] += part

    row = pl.BlockSpec((ts, d), lambda i: (i, 0))
    vec = pl.BlockSpec((1, d), lambda i: (0, 0))
    if has_loss:
        return pl.pallas_call(
            body, name=name, grid=(s // ts,),
            in_specs=[row, row, vec, row],
            out_specs=[row, pl.BlockSpec((1, LANE), lambda i: (0, 0))],
            out_shape=[jax.ShapeDtypeStruct((s, d), F32), jax.ShapeDtypeStruct((1, LANE), F32)],
            compiler_params=pltpu.CompilerParams(dimension_semantics=("arbitrary",)),
        )(x, out, g, target)
    return pl.pallas_call(
        body, name=name, grid=(s // ts,),
        in_specs=[row, row, vec], out_specs=row,
        out_shape=jax.ShapeDtypeStruct((s, d), F32),
    )(x, out, g)


def _rows_before(ref, start, n, halo):
    if start == 0:
        return jnp.concatenate([jnp.zeros((halo, ref.shape[1]), F32), ref[0:n, :].astype(F32)], axis=0)
    return ref[start - halo:start + n, :].astype(F32)


def _rows_after(ref, start, n, halo):
    if start + n == ref.shape[0]:
        return jnp.concatenate([ref[start:start + n, :].astype(F32), jnp.zeros((halo, ref.shape[1]), F32)], axis=0)
    return ref[start:start + n + halo, :].astype(F32)


def _pick_window(group, s2, s4, s8, s16):
    return jnp.where(group == 0, s2, jnp.where(group == 1, s4, jnp.where(group == 2, s8, s16)))


def _trailing_sums(ext, group):
    s2 = ext + pltpu.roll(ext, 1, 0)
    s4 = s2 + pltpu.roll(s2, 2, 0)
    s8 = s4 + pltpu.roll(s4, 4, 0)
    s16 = s8 + pltpu.roll(s8, 8, 0)
    return _pick_window(group, s2, s4, s8, s16)


def _leading_sums(ext, group):
    n = ext.shape[0]
    s2 = ext + pltpu.roll(ext, n - 1, 0)
    s4 = s2 + pltpu.roll(s2, n - 2, 0)
    s8 = s4 + pltpu.roll(s4, n - 4, 0)
    s16 = s8 + pltpu.roll(s8, n - 8, 0)
    return _pick_window(group, s2, s4, s8, s16)


def _window_count(start, n, group):
    pos = start + lax.broadcasted_iota(jnp.int32, (n, LANE), 0)
    return jnp.minimum(pos + 1, 2 << group).astype(F32)


def _pooled(v_ref, start, n, group):
    ext = _rows_before(v_ref, start, n, POOL_HALO)
    sums = _trailing_sums(ext, group)[POOL_HALO:, :]
    return sums / _window_count(start, n, group) - ext[POOL_HALO:, :]


def _pool_fwd(u, pool_w, pool_scale, name, ts=512):
    s = u.shape[0]

    def body(v_ref, gate_ref, w_ref, sc_ref, y_ref):
        group = pl.program_id(0)
        for c in range(s // ts):
            a = c * ts
            pooled = _pooled(v_ref, a, ts, group)
            mixed = _dot(pooled.astype(BF16), w_ref[...], NN)
            gate = gate_ref[a:a + ts, :].astype(F32)
            y_ref[a:a + ts, :] = (mixed * sc_ref[...] * (gate * _sigmoid(gate))).astype(BF16)

    col = lambda base: pl.BlockSpec((s, LANE), lambda g: (0, base + g))
    return pl.pallas_call(
        body, name=name, grid=(4,),
        in_specs=[col(COL_POOL_V), col(COL_POOL_G),
                  pl.BlockSpec((None, LANE, LANE), lambda g: (g, 0, 0)),
                  pl.BlockSpec((1, LANE), lambda g: (0, g))],
        out_specs=pl.BlockSpec((s, LANE), lambda g: (0, g)),
        out_shape=jax.ShapeDtypeStruct((s, WIDTH), BF16),
    )(u, u, pool_w, pool_scale)


def _pool_bwd(u, dy, pool_w, pool_scale, name, ts=512):
    s = u.shape[0]

    def body(v_ref, gate_ref, dy_ref, w_ref, sc_ref, dv_ref, dgate_ref, dw_ref, dsc_ref):
        group = pl.program_id(0)
        w = w_ref[...]
        scale = sc_ref[...]
        dw = jnp.zeros((LANE, LANE), F32)
        dsc = jnp.zeros((1, LANE), F32)
        for c in range(s // ts):
            a = c * ts
            n_ext = ts + POOL_HALO
            gate_e = _rows_after(gate_ref, a, ts, POOL_HALO)
            dy_e = _rows_after(dy_ref, a, ts, POOL_HALO)
            silu_e, dsilu_e = _silu_and_grad(gate_e)
            dms_e = dy_e * silu_e
            dm_e = (dms_e * scale).astype(BF16)
            dpool_e = _dot(dm_e, w, NT)
            spread = _leading_sums(dpool_e / _window_count(a, n_ext, group), group)
            dv_ref[a:a + ts, :] = (spread[0:ts, :] - dpool_e[0:ts, :]).astype(BF16)
            pooled = _pooled(v_ref, a, ts, group).astype(BF16)
            mixed = _dot(pooled, w, NN)
            dgate_ref[a:a + ts, :] = (dy_e[0:ts, :] * mixed * scale * dsilu_e[0:ts, :]).astype(BF16)
            dsc = dsc + jnp.sum(dms_e[0:ts, :] * mixed, axis=0, keepdims=True)
            dw = dw + _dot(pooled, dm_e[0:ts, :], TN)
        dw_ref[...] = dw
        dsc_ref[...] = dsc

    col = lambda base: pl.BlockSpec((s, LANE), lambda g: (0, base + g))
    out_col = pl.BlockSpec((s, LANE), lambda g: (0, g))
    return pl.pallas_call(
        body, name=name, grid=(4,),
        in_specs=[col(COL_POOL_V), col(COL_POOL_G), out_col,
                  pl.BlockSpec((None, LANE, LANE), lambda g: (g, 0, 0)),
                  pl.BlockSpec((1, LANE), lambda g: (0, g))],
        out_specs=[out_col, out_col,
                   pl.BlockSpec((None, LANE, LANE), lambda g: (g, 0, 0)),
                   pl.BlockSpec((1, LANE), lambda g: (0, g))],
        out_shape=[jax.ShapeDtypeStruct((s, WIDTH), BF16), jax.ShapeDtypeStruct((s, WIDTH), BF16),
                   jax.ShapeDtypeStruct((4, LANE, LANE), F32), jax.ShapeDtypeStruct((1, WIDTH), F32)],
    )(u, u, dy, pool_w, pool_scale)


def _conv_taps(x_ref, gc_ref, start, n):
    z_ext = _rows_before(gc_ref, start, n, CONV_HALO) * _rows_before(x_ref, start, n, CONV_HALO)
    z0 = z_ext[CONV_HALO:, :]
    z1 = pltpu.roll(z_ext, 1, 0)[CONV_HALO:, :]
    z2 = pltpu.roll(z_ext, 2, 0)[CONV_HALO:, :]
    return z0, z1, z2


def _conv_fwd(u, conv_w, conv_b, name, ts=512):
    s = u.shape[0]

    def body(x_ref, gb_ref, gc_ref, g_ref, w_ref, b_ref, y_ref):
        w0, w1, w2 = w_ref[0:1, :], w_ref[1:2, :], w_ref[2:3, :]
        for c in range(s // ts):
            a = c * ts
            z0, z1, z2 = _conv_taps(x_ref, gc_ref, a, ts)
            y = w2 * z0 + w1 * z1 + w0 * z2 + b_ref[...]
            gate = g_ref[a:a + ts, :].astype(F32)
            y_ref[a:a + ts, :] = (gb_ref[a:a + ts, :].astype(F32) * y * (gate * _sigmoid(gate))).astype(BF16)

    col = lambda base: pl.BlockSpec((s, LANE), lambda j: (0, base + j))
    return pl.pallas_call(
        body, name=name, grid=(4,),
        in_specs=[col(COL_CONV_X), col(COL_CONV_GB), col(COL_CONV_GC), col(COL_CONV_G),
                  pl.BlockSpec((3, LANE), lambda j: (0, j)), pl.BlockSpec((1, LANE), lambda j: (0, j))],
        out_specs=pl.BlockSpec((s, LANE), lambda j: (0, j)),
        out_shape=jax.ShapeDtypeStruct((s, WIDTH), BF16),
    )(u, u, u, u, conv_w, conv_b)


def _conv_bwd(u, dy, conv_w, conv_b, name, ts=512):
    s = u.shape[0]

    def body(x_ref, gb_ref, gc_ref, g_ref, dy_ref, w_ref, b_ref,
             dx_ref, dgb_ref, dgc_ref, dg_ref, dw_ref, db_ref):
        w0, w1, w2 = w_ref[0:1, :], w_ref[1:2, :], w_ref[2:3, :]
        acc = [jnp.zeros((1, LANE), F32) for _ in range(4)]
        for c in range(s // ts):
            a = c * ts
            n_ext = ts + CONV_HALO
            gate_e = _rows_after(g_ref, a, ts, CONV_HALO)
            silu_e, dsilu_e = _silu_and_grad(gate_e)
            dy_e = _rows_after(dy_ref, a, ts, CONV_HALO)
            gb_e = _rows_after(gb_ref, a, ts, CONV_HALO)
            dyy_e = dy_e * silu_e * gb_e
            dz = (w2 * dyy_e + w1 * pltpu.roll(dyy_e, n_ext - 1, 0) + w0 * pltpu.roll(dyy_e, n_ext - 2, 0))[0:ts, :]
            z0, z1, z2 = _conv_taps(x_ref, gc_ref, a, ts)
            yb = w2 * z0 + w1 * z1 + w0 * z2 + b_ref[...]
            dyv = dy_e[0:ts, :]
            dyy = dyy_e[0:ts, :]
            dg_ref[a:a + ts, :] = (dyv * gb_e[0:ts, :] * yb * dsilu_e[0:ts, :]).astype(BF16)
            dgb_ref[a:a + ts, :] = (dyv * silu_e[0:ts, :] * yb).astype(BF16)
            dx_ref[a:a + ts, :] = (dz * gc_ref[a:a + ts, :].astype(F32)).astype(BF16)
            dgc_ref[a:a + ts, :] = (dz * x_ref[a:a + ts, :].astype(F32)).astype(BF16)
            for i, term in enumerate((dyy * z2, dyy * z1, dyy * z0, dyy)):
                acc[i] = acc[i] + jnp.sum(term, axis=0, keepdims=True)
        dw_ref[0:1, :] = acc[0]
        dw_ref[1:2, :] = acc[1]
        dw_ref[2:3, :] = acc[2]
        db_ref[...] = acc[3]

    col = lambda base: pl.BlockSpec((s, LANE), lambda j: (0, base + j))
    out_col = pl.BlockSpec((s, LANE), lambda j: (0, j))
    big = jax.ShapeDtypeStruct((s, WIDTH), BF16)
    return pl.pallas_call(
        body, name=name, grid=(4,),
        in_specs=[col(COL_CONV_X), col(COL_CONV_GB), col(COL_CONV_GC), col(COL_CONV_G), out_col,
                  pl.BlockSpec((3, LANE), lambda j: (0, j)), pl.BlockSpec((1, LANE), lambda j: (0, j))],
        out_specs=[out_col, out_col, out_col, out_col,
                   pl.BlockSpec((3, LANE), lambda j: (0, j)), pl.BlockSpec((1, LANE), lambda j: (0, j))],
        out_shape=[big, big, big, big,
                   jax.ShapeDtypeStruct((3, WIDTH), F32), jax.ShapeDtypeStruct((1, WIDTH), F32)],
    )(u, u, u, u, dy, conv_w, conv_b)


LOG2_E = 1.4426950408889634
LN_2 = 0.6931471805599453


def _sb_scores(q_h, k_blk, valid, later_mat, carry):
    z = _dot(q_h, k_blk, NT)
    neg_z = -z
    soft = jnp.log(1.0 + jnp.exp2(jnp.minimum(z, neg_z))) * LOG2_E
    log_keep = jnp.minimum(neg_z, 0.0) - soft
    log_beta = log_keep + z
    if valid is not None:
        log_keep = jnp.where(valid, log_keep, 0.0)
    later = _dot(log_keep.astype(BF16), later_mat, NN) + carry
    return log_keep, log_beta, later


def _masked(valid, x):
    return x if valid is None else jnp.where(valid, x, 0.0)


def _diagonal_masks(tq, tk):
    r = lax.broadcasted_iota(jnp.int32, (tq, tk), 0)
    cidx = lax.broadcasted_iota(jnp.int32, (tq, tk), 1)
    return [cidx + d * tk < r for d in range(tq // tk)]


def _triangle(tk, op):
    r = lax.broadcasted_iota(jnp.int32, (tk, tk), 0)
    cidx = lax.broadcasted_iota(jnp.int32, (tk, tk), 1)
    return op(r, cidx).astype(BF16)


def _split_refs(refs, n_in, n_out, n_scratch, rider):
    r_in = len(rider.inputs) if rider else 0
    r_out = len(rider.out_shape) if rider else 0
    a, b = n_in + r_in, n_in + r_in + n_out + r_out
    own = refs[:n_in] + refs[a:a + n_out] + refs[b:b + n_scratch]
    return own, (refs[n_in:a], refs[a + n_out:b], refs[b + n_scratch:])


def _rider_call_args(rider, n_in, n_out):
    if rider is None:
        return dict(in_specs=[], out_specs=[], out_shape=[], aliases={}, scratch=[], inputs=[])
    return dict(in_specs=[ANY] * len(rider.inputs), out_specs=[ANY] * len(rider.out_shape),
                out_shape=list(rider.out_shape), scratch=list(rider.scratch), inputs=list(rider.inputs),
                aliases={n_in + a: n_out + b for a, b in rider.aliases.items()})


def _ride(rider, phase, when, parts):
    fn = getattr(rider, phase) if rider else None
    if fn is not None:
        pl.when(when)(lambda: fn(*parts))


def _sb_fwd(u, name, t=512, tk=256, pairs=4, rider=None):
    s = u.shape[0]
    assert s // tk <= LANE and 4 % pairs == 0 and t % tk == 0
    scale = HEAD_DIM ** -0.5
    nh = 2 * pairs
    wide = pairs * LANE
    ratio = t // tk
    groups, nq = 4 // pairs, s // t

    def body(*refs):
        own, riding = _split_refs(refs, 4, 3, 4, rider)
        q_ref, k_ref, v_ref, g_ref, o_ref, y_ref, after_ref, kb_ref, vb_ref, acc_ref, carry_ref = own
        grp = pl.program_id(0)
        i = pl.program_id(1)
        _ride(rider, "start", jnp.logical_and(grp == 0, i == 0), riding)
        _ride(rider, "middle", jnp.logical_and(grp == groups - 1, i == (3 * nq) // 4), riding)

        @pl.when(i == 0)
        def _():
            kb_ref[...] = k_ref[...].astype(BF16)
            vb_ref[...] = v_ref[...].astype(BF16)

        lane = lax.broadcasted_iota(jnp.int32, (t, LANE), 1)
        first = lane < HEAD_DIM
        after_ref[...] = jnp.zeros_like(after_ref)
        qv = q_ref[...].astype(F32) * (scale * LOG2_E)
        q_heads = []
        for p in range(pairs):
            qp = qv[:, p * LANE:(p + 1) * LANE]
            q_heads += [jnp.where(first, qp, 0.0).astype(BF16), jnp.where(first, 0.0, qp).astype(BF16)]
        later_mat = _triangle(tk, lambda r, cidx: r > cidx)
        acc_ref[...] = jnp.zeros_like(acc_ref)
        carry_ref[...] = jnp.zeros_like(carry_ref)

        def block(kb, valid, lo=0):
            rows = pl.ds(pl.multiple_of(kb * tk, tk), tk)
            k_blk = kb_ref[rows, :]
            v_blk = vb_ref[rows, :]
            carries = [carry_ref[h, lo:, :] for h in range(nh)]
            afters = [after_ref[lo:, h * LANE:(h + 1) * LANE] for h in range(nh)]
            accs = [acc_ref[h, lo:, :] for h in range(nh)]
            outs = []
            for h in range(nh):
                cols = slice((h // 2) * LANE, (h // 2 + 1) * LANE)
                log_keep, log_beta, later = _sb_scores(q_heads[h][lo:], k_blk[:, cols], valid, later_mat, carries[h])
                a = _masked(valid, jnp.exp2(log_beta + later))
                outs.append((accs[h] + _dot(a.astype(BF16), v_blk[:, cols], NN),
                             carries[h] + jnp.sum(log_keep, axis=1, keepdims=True),
                             jnp.where(lane[lo:] == kb, carries[h], afters[h])))
            for h in range(nh):
                acc_ref[h, lo:, :] = outs[h][0]
                carry_ref[h, lo:, :] = outs[h][1]
                after_ref[lo:, h * LANE:(h + 1) * LANE] = outs[h][2]

        def step(j, _):
            block(ratio * i - 1 - j, None)
            return 0

        masks = _diagonal_masks(t, tk)
        for d in reversed(range(ratio)):
            block(ratio * i + d, masks[d][d * tk:], d * tk)
        lax.fori_loop(0, ratio * i, step, 0)
        for p in range(pairs):
            cols = slice(p * LANE, (p + 1) * LANE)
            o = jnp.where(first, acc_ref[2 * p], acc_ref[2 * p + 1])
            o_ref[:, cols] = o
            gate = g_ref[:, cols].astype(F32)
            y_ref[:, cols] = (o * gate * _sigmoid(gate)).astype(BF16)
        _ride(rider, "finish", jnp.logical_and(grp == groups - 1, i == nq - 1), riding)

    blk = lambda base: pl.BlockSpec((t, wide), lambda g, i: (i, base // pairs + g))
    full = lambda base: pl.BlockSpec((s, wide), lambda g, i: (0, base // pairs + g))
    out_blk = pl.BlockSpec((t, wide), lambda g, i: (i, g))
    extra = _rider_call_args(rider, 4, 3)
    outs = pl.pallas_call(
        body, name=name, grid=(groups, nq),
        in_specs=[blk(COL_SB_Q), full(COL_SB_K), full(COL_SB_V), blk(COL_SB_G)] + extra["in_specs"],
        out_specs=[out_blk, out_blk, pl.BlockSpec((t, nh * LANE), lambda g, i: (i, g))] + extra["out_specs"],
        out_shape=[jax.ShapeDtypeStruct((s, WIDTH), F32), jax.ShapeDtypeStruct((s, WIDTH), BF16),
                   jax.ShapeDtypeStruct((s, 8 * LANE), F32)] + extra["out_shape"],
        input_output_aliases=extra["aliases"],
        scratch_shapes=[pltpu.VMEM((s, wide), BF16), pltpu.VMEM((s, wide), BF16),
                        pltpu.VMEM((nh, t, LANE), F32), pltpu.VMEM((nh, t, 1), F32)] + extra["scratch"],
        compiler_params=pltpu.CompilerParams(dimension_semantics=("arbitrary", "arbitrary")),
    )(u, u, u, u, *extra["inputs"])
    return outs[:3], outs[3:]


def _sb_bwd(u, o, after, dy, name, t=512, tk=256, pairs=2, rider=None):
    s = u.shape[0]
    nq = s // t
    scale = HEAD_DIM ** -0.5
    nh = 2 * pairs
    wide = pairs * LANE
    ratio = t // tk
    groups = 4 // pairs

    def body(*refs):
        own, riding = _split_refs(refs, 7, 4, 6, rider)
        (q_ref, k_ref, v_ref, g_ref, o_ref, after_ref, dy_ref, dq_ref, dk_ref, dv_ref, dg_ref,
         kb_ref, vb_ref, dk_acc, dv_acc, dq_acc, carry_ref) = own
        grp = pl.program_id(0)
        i = pl.program_id(1)
        _ride(rider, "start", jnp.logical_and(grp == 0, i == 0), riding)

        @pl.when(i == 0)
        def _():
            kb_ref[...] = k_ref[...].astype(BF16)
            vb_ref[...] = v_ref[...].astype(BF16)
            dk_acc[...] = jnp.zeros_like(dk_acc)
            dv_acc[...] = jnp.zeros_like(dv_acc)

        lane = lax.broadcasted_iota(jnp.int32, (t, LANE), 1)
        first = lane < HEAD_DIM
        gate = g_ref[...].astype(F32)
        silu, dsilu = _silu_and_grad(gate)
        dyv = dy_ref[...]
        do = dyv * silu
        dg_ref[...] = (dyv * o_ref[...] * dsilu).astype(BF16)
        qv = q_ref[...].astype(F32) * (scale * LOG2_E)
        do_heads, q_heads = [], []
        for p in range(pairs):
            cols = slice(p * LANE, (p + 1) * LANE)
            do_heads += [jnp.where(first, do[:, cols], 0.0).astype(BF16), jnp.where(first, 0.0, do[:, cols]).astype(BF16)]
            q_heads += [jnp.where(first, qv[:, cols], 0.0).astype(BF16), jnp.where(first, 0.0, qv[:, cols]).astype(BF16)]
        later_mat = _triangle(tk, lambda r, cidx: r > cidx)
        before_mat = _triangle(tk, lambda r, cidx: r < cidx)
        dq_acc[...] = jnp.zeros_like(dq_acc)
        carry_ref[...] = jnp.zeros_like(carry_ref)

        def block(kb, valid, lo=0):
            rows = pl.ds(pl.multiple_of(kb * tk, tk), tk)
            k_blk = kb_ref[rows, :]
            v_blk = vb_ref[rows, :]
            carries = [carry_ref[h, lo:, :] for h in range(nh)]
            dq_old = [dq_acc[h, lo:, :] for h in range(nh)]
            dk_old = dk_acc[rows, :]
            dv_old = dv_acc[rows, :]
            outs = []
            for h in range(nh):
                cols = slice((h // 2) * LANE, (h // 2 + 1) * LANE)
                q_h, do_h = q_heads[h][lo:], do_heads[h][lo:]
                after = jnp.sum(jnp.where(lane[lo:] == kb, after_ref[lo:, h * LANE:(h + 1) * LANE], 0.0), axis=1,
                                keepdims=True)
                _, log_beta, later = _sb_scores(q_h, k_blk[:, cols], valid, later_mat, after)
                beta = jnp.exp2(log_beta)
                a = _masked(valid, jnp.exp2(log_beta + later))
                da = _dot(do_h, v_blk[:, cols], NT)
                gterm = a * da
                before = _dot(gterm.astype(BF16), before_mat, NN) + carries[h]
                dz_b = _masked(valid, gterm * (1.0 - beta) - beta * before).astype(BF16)
                outs.append((dq_old[h] + _dot(dz_b, k_blk[:, cols], NN), _dot(dz_b, q_h, TN),
                             _dot(a.astype(BF16), do_h, TN),
                             carries[h] + jnp.sum(gterm, axis=1, keepdims=True)))
            for h in range(nh):
                dq_acc[h, lo:, :] = outs[h][0]
                carry_ref[h, lo:, :] = outs[h][3]
            dk_new = [outs[2 * p][1] + outs[2 * p + 1][1] for p in range(pairs)]
            dv_new = [outs[2 * p][2] + outs[2 * p + 1][2] for p in range(pairs)]
            dk_acc[rows, :] = dk_old + (dk_new[0] if pairs == 1 else jnp.concatenate(dk_new, axis=1))
            dv_acc[rows, :] = dv_old + (dv_new[0] if pairs == 1 else jnp.concatenate(dv_new, axis=1))

        def step(kb, _):
            block(kb, None)
            return 0

        lax.fori_loop(0, ratio * i, step, 0)
        masks = _diagonal_masks(t, tk)
        for d in range(ratio):
            block(ratio * i + d, masks[d][d * tk:], d * tk)
        for p in range(pairs):
            dq_ref[:, p * LANE:(p + 1) * LANE] = (jnp.where(first, dq_acc[2 * p], dq_acc[2 * p + 1]) * scale).astype(BF16)

        @pl.when(i == nq - 1)
        def _():
            dk_ref[...] = (dk_acc[...] * LN_2).astype(BF16)
            dv_ref[...] = dv_acc[...].astype(BF16)

        _ride(rider, "finish", jnp.logical_and(grp == groups - 1, i == nq - 1), riding)

    blk = lambda base: pl.BlockSpec((t, wide), lambda g, i: (i, base // pairs + g))
    full = lambda base: pl.BlockSpec((s, wide), lambda g, i: (0, base // pairs + g))
    out_blk = pl.BlockSpec((t, wide), lambda g, i: (i, g))
    out_full = pl.BlockSpec((s, wide), lambda g, i: (0, g))
    big = jax.ShapeDtypeStruct((s, WIDTH), BF16)
    extra = _rider_call_args(rider, 7, 4)
    outs = pl.pallas_call(
        body, name=name, grid=(groups, nq),
        in_specs=[blk(COL_SB_Q), full(COL_SB_K), full(COL_SB_V), blk(COL_SB_G), out_blk,
                  pl.BlockSpec((t, nh * LANE), lambda g, i: (i, g)), out_blk] + extra["in_specs"],
        out_specs=[out_blk, out_full, out_full, out_blk] + extra["out_specs"],
        out_shape=[big, big, big, big] + extra["out_shape"],
        input_output_aliases=extra["aliases"],
        scratch_shapes=[pltpu.VMEM((s, wide), BF16), pltpu.VMEM((s, wide), BF16),
                        pltpu.VMEM((s, wide), F32), pltpu.VMEM((s, wide), F32),
                        pltpu.VMEM((nh, t, LANE), F32), pltpu.VMEM((nh, t, 1), F32)] + extra["scratch"],
        compiler_params=pltpu.CompilerParams(dimension_semantics=("arbitrary", "arbitrary")),
    )(u, u, u, u, o, after, dy, *extra["inputs"])
    return outs[:4], outs[4:]


def _gate_fwd(u, ys, w_branch, name, ts=256):
    s = u.shape[0]

    def body(m0, m1, m2, y0, y1, y2, w_ref, p0, p1, p2, out_ref):
        tot = None
        for n, (m_ref, y_ref, p_ref) in enumerate(((m0, y0, p0), (m1, y1, p1), (m2, y2, p2))):
            proj = _dot(y_ref[...], w_ref[n], NN)
            p_ref[...] = proj.astype(BF16)
            term = _sigmoid(m_ref[...].astype(F32)) * proj
            tot = term if tot is None else tot + term
        out_ref[...] = tot.astype(BF16)

    mspec = lambda n: pl.BlockSpec((ts, D_MODEL), lambda i: (i, COL_MERGE_1024 + n))
    row = pl.BlockSpec((ts, D_MODEL), lambda i: (i, 0))
    yspec = pl.BlockSpec((ts, WIDTH), lambda i: (i, 0))
    big = jax.ShapeDtypeStruct((s, D_MODEL), BF16)
    outs = pl.pallas_call(
        body, name=name, grid=(s // ts,),
        in_specs=[mspec(0), mspec(1), mspec(2), yspec, yspec, yspec,
                  pl.BlockSpec(w_branch.shape, lambda i: (0, 0, 0))],
        out_specs=[row] * 4, out_shape=[big] * 4,
    )(u, u, u, *ys, w_branch)
    return outs[:3], outs[3]


def _gate_bwd(u, projs, dmerged, w_branch, name, ts=256, rider=None):
    s = u.shape[0]
    steps = s // ts

    def body(*refs):
        own, riding = _split_refs(refs, 8, 9, 0, rider)
        m0, m1, m2, p0, p1, p2, dm_ref, w_ref, dp0, dp1, dp2, dl0, dl1, dl2, dy0, dy1, dy2 = own
        _ride(rider, "start", pl.program_id(0) == 0, riding)
        dm = dm_ref[...].astype(F32)
        for n, (m_ref, p_ref, dp_ref, dl_ref, dy_ref) in enumerate(((m0, p0, dp0, dl0, dy0), (m1, p1, dp1, dl1, dy1),
                                                                    (m2, p2, dp2, dl2, dy2))):
            gate = _sigmoid(m_ref[...].astype(F32))
            dp = (dm * gate).astype(BF16)
            dp_ref[...] = dp
            dl_ref[...] = (dm * p_ref[...].astype(F32) * gate * (1.0 - gate)).astype(BF16)
            dy_ref[...] = _dot(dp, w_ref[n], NT)
        _ride(rider, "finish", pl.program_id(0) == steps - 1, riding)

    mspec = lambda n: pl.BlockSpec((ts, D_MODEL), lambda i: (i, COL_MERGE_1024 + n))
    row = pl.BlockSpec((ts, D_MODEL), lambda i: (i, 0))
    yspec = pl.BlockSpec((ts, WIDTH), lambda i: (i, 0))
    big = jax.ShapeDtypeStruct((s, D_MODEL), BF16)
    extra = _rider_call_args(rider, 8, 9)
    outs = pl.pallas_call(
        body, name=name, grid=(steps,),
        in_specs=[mspec(0), mspec(1), mspec(2), row, row, row, row,
                  pl.BlockSpec(w_branch.shape, lambda i: (0, 0, 0))] + extra["in_specs"],
        out_specs=[row] * 6 + [yspec] * 3 + extra["out_specs"],
        out_shape=[big] * 6 + [jax.ShapeDtypeStruct((s, WIDTH), F32)] * 3 + extra["out_shape"],
        input_output_aliases=extra["aliases"], scratch_shapes=extra["scratch"],
        compiler_params=pltpu.CompilerParams(dimension_semantics=("arbitrary",)),
    )(u, u, u, *projs, dmerged, w_branch, *extra["inputs"])
    return outs[:3], outs[3:6], outs[6:9], outs[9:]


def _as_rows(a):
    return a.reshape(-1, a.shape[-1])


def _row_tile(rows, cols, bytes_per_row_elem=4, cap=1 << 20):
    tr = rows
    while tr * cols * bytes_per_row_elem > cap and tr % 2 == 0 and (tr // 2) % 16 == 0:
        tr //= 2
    return tr


def _cast_bf16(a, name):
    a2 = _as_rows(a)
    rows, cols = a2.shape
    tr = _row_tile(rows, cols)

    def body(a_ref, o_ref):
        o_ref[...] = a_ref[...].astype(BF16)

    spec = pl.BlockSpec((tr, cols), lambda i: (i, 0))
    out = pl.pallas_call(body, name=name, grid=(rows // tr,), in_specs=[spec], out_specs=spec,
                         out_shape=jax.ShapeDtypeStruct((rows, cols), BF16))(a2)
    return out.reshape(a.shape)


def _adamw(w, g, m, v, name):
    shape = w.shape
    w2, g2, m2, v2 = (_as_rows(a) for a in (w, g, m, v))
    rows, cols = w2.shape
    tr = _row_tile(rows, cols)
    c1 = 1.0 - ADAM_B1 ** ADAM_STEP
    c2 = 1.0 - ADAM_B2 ** ADAM_STEP

    def body(w_ref, g_ref, m_ref, v_ref, go_ref, d_ref, nm_ref, nv_ref):
        gv = g_ref[...]
        go_ref[...] = gv
        nm = ADAM_B1 * m_ref[...] + (1.0 - ADAM_B1) * gv
        nv = ADAM_B2 * v_ref[...] + (1.0 - ADAM_B2) * (gv * gv)
        nm_ref[...] = nm
        nv_ref[...] = nv
        d_ref[...] = -ADAM_LR * ((nm / c1) / (jnp.sqrt(nv / c2) + ADAM_EPS) + ADAM_WD * w_ref[...])

    spec = pl.BlockSpec((tr, cols), lambda i: (i, 0))
    sds = jax.ShapeDtypeStruct((rows, cols), F32)
    outs = pl.pallas_call(body, name=name, grid=(rows // tr,), in_specs=[spec] * 4, out_specs=[spec] * 4,
                          out_shape=[sds] * 4)(w2, g2, m2, v2)
    return tuple(o.reshape(shape) for o in outs)


def _sum_slots(a, out_dtype, name):
    n = a.shape[0]
    a3 = a.reshape(n, -1, a.shape[-1])
    _, rows, cols = a3.shape
    tr = _row_tile(rows, cols * n)

    def body(a_ref, o_ref):
        tot = a_ref[0].astype(F32)
        for k in range(1, n):
            tot = tot + a_ref[k].astype(F32)
        o_ref[...] = tot.astype(out_dtype)

    out = pl.pallas_call(
        body, name=name, grid=(rows // tr,),
        in_specs=[pl.BlockSpec((n, tr, cols), lambda i: (0, i, 0))],
        out_specs=pl.BlockSpec((tr, cols), lambda i: (i, 0)),
        out_shape=jax.ShapeDtypeStruct((rows, cols), out_dtype))(a3)
    return out.reshape(a.shape[1:])


def _chip_sum(own, recv, axis, core, name):
    half = recv.shape
    nd = len(half)
    last = nd - 1
    if axis == last:
        tl, nt = half[last], 1
    else:
        tl = min(half[last], 2048)
        nt = half[last] // tl
    block = half[:last] + (tl,)

    def own_index(i, core_ref):
        idx = [0] * nd
        idx[last] = i
        if axis == last:
            idx[last] = core_ref[0]
        else:
            idx[axis] = core_ref[0]
        return tuple(idx)

    def recv_index(i, core_ref):
        idx = [0] * nd
        idx[last] = i
        return tuple(idx)

    def body(core_ref, own_ref, recv_ref, o_ref):
        o_ref[...] = (own_ref[...].astype(F32) + recv_ref[...].astype(F32)).astype(BF16)

    return pl.pallas_call(
        body, name=name,
        grid_spec=pltpu.PrefetchScalarGridSpec(
            num_scalar_prefetch=1, grid=(nt,),
            in_specs=[pl.BlockSpec(block, own_index), pl.BlockSpec(block, recv_index)],
            out_specs=pl.BlockSpec(block, recv_index)),
        out_shape=jax.ShapeDtypeStruct(half, BF16),
    )(core, own, recv)


def _mesh_position():
    return lax.axis_index("x"), lax.axis_index("y"), lax.axis_index("c")


def _other_chips(x, y):
    return [(1 - x, y), (x, 1 - y), (1 - x, 1 - y)]


ALL_FLIPS = [(0, 0, 1), (1, 0, 0), (0, 1, 0), (1, 1, 0), (1, 0, 1), (0, 1, 1), (1, 1, 1)]


def _half(ref, axis, which, size):
    idx = [slice(None)] * len(ref.shape)
    idx[axis] = pl.ds(which * size, size)
    return ref.at[tuple(idx)]


def _sub(ref, picks):
    idx = [slice(None)] * len(ref.shape)
    for axis, start, size in picks:
        idx[axis] = pl.ds(start, size)
    return ref.at[tuple(idx)]


def _remote(src, dst, sems_send, sems_recv, k, to):
    return pltpu.make_async_remote_copy(src_ref=src, dst_ref=dst, send_sem=sems_send.at[k], recv_sem=sems_recv.at[k],
                                        device_id=to, device_id_type=MESH)


def _cast_shard(w, layer, shard_axis, pos, name, tr=512):
    shape = w.shape[1:]
    nd = len(shape)
    assert shard_axis in (nd - 1, nd - 2)
    rows, cols = shape[-2:]
    tr = min(tr, rows)
    nt = rows // tr
    lead = shape[:-2]
    full = list(shape)
    full[shard_axis] *= N_CHIPS
    block = (1,) * len(lead) + (tr, cols)

    def in_index(*args):
        return (layer, *args[:-1], 0)

    def out_index(*args):
        *g, pos_ref = args
        if shard_axis == nd - 1:
            return (*g, pos_ref[1])
        return (*g[:-1], pos_ref[1] * nt + g[-1], 0)

    def body(pos_ref, a_ref, o_ref):
        o_ref[...] = a_ref[...].astype(BF16)

    return pl.pallas_call(
        body, name=name,
        grid_spec=pltpu.PrefetchScalarGridSpec(
            num_scalar_prefetch=1, grid=lead + (nt,),
            in_specs=[pl.BlockSpec((None,) + block, in_index)], out_specs=pl.BlockSpec(block, out_index)),
        out_shape=jax.ShapeDtypeStruct(tuple(full), BF16),
    )(pos, w)


class _Rider:
    def __init__(self, inputs, out_shape, aliases, scratch, start, middle, finish):
        self.inputs, self.out_shape, self.aliases, self.scratch = inputs, out_shape, aliases, scratch
        self.start, self.middle, self.finish = start, middle, finish


def _weight_gather_rider(fulls, layout):
    n = len(fulls)

    def copies(outs, sems):
        send_sems, recv_sems = sems
        x, y, c = _mesh_position()
        chips = _other_chips(x, y)
        sibling = (x, y, 1 - c)
        mine = 2 * x + y

        def place(t, chip, core):
            sh_axis, sh_size, half_axis, half_size = layout[t]
            return _sub(outs[t], [(sh_axis, chip * sh_size, sh_size), (half_axis, core * half_size, half_size)])

        direct, arrive, forward, arrive_fwd = [], [], [], []
        for t in range(n):
            for k, (px, py) in enumerate(chips):
                theirs = 2 * px + py
                direct.append(_remote(place(t, mine, c), place(t, mine, c), send_sems, recv_sems, 6 * t + k, (px, py, c)))
                arrive.append(_remote(place(t, theirs, c), place(t, theirs, c), send_sems, recv_sems, 6 * t + k, (px, py, c)))
                forward.append(_remote(place(t, theirs, c), place(t, theirs, c), send_sems, recv_sems, 6 * t + 3 + k, sibling))
                arrive_fwd.append(_remote(place(t, theirs, 1 - c), place(t, theirs, 1 - c), send_sems, recv_sems,
                                          6 * t + 3 + k, sibling))
        return direct, arrive, forward, arrive_fwd

    def start(ins, outs, sems):
        for cp in copies(outs, sems)[0]:
            cp.start()

    def middle(ins, outs, sems):
        _, arrive, forward, _ = copies(outs, sems)
        for a, f in zip(arrive, forward):
            a.wait_recv()
            f.start()

    def finish(ins, outs, sems):
        direct, _, forward, arrive_fwd = copies(outs, sems)
        for cp in arrive_fwd:
            cp.wait_recv()
        for cp in direct + forward:
            cp.wait_send()

    return _Rider(list(fulls), [jax.ShapeDtypeStruct(a.shape, a.dtype) for a in fulls], {k: k for k in range(n)},
                  [pltpu.SemaphoreType.DMA((6 * n,)), pltpu.SemaphoreType.DMA((6 * n,))], start, middle, finish)


WEIGHT_LAYOUT = [(1, 2048, 0, 512), (2, 256, 1, 256), (0, 256, 1, 512)]


def _gather_weights(fulls, layout, conv_w):
    rider = _weight_gather_rider(fulls, layout)
    n = len(fulls)

    def body(*refs):
        cw, outs, cw_f = refs[n], refs[n + 1:2 * n + 1], refs[2 * n + 1]
        sems, (cw_send, cw_recv, local_sem) = refs[2 * n + 2:2 * n + 4], refs[2 * n + 4:]
        x, y, c = _mesh_position()
        chips = _other_chips(x, y)
        mine = 2 * x + y
        local = pltpu.make_async_copy(cw, cw_f.at[mine], local_sem.at[0])
        local.start()
        rider.start(None, outs, sems)
        small = [_remote(cw, cw_f.at[mine], cw_send, cw_recv, k, (*chip, c)) for k, chip in enumerate(chips)]
        for cp in small:
            cp.start()
        rider.middle(None, outs, sems)
        rider.finish(None, outs, sems)
        for k, (px, py) in enumerate(chips):
            _remote(cw, cw_f.at[2 * px + py], cw_send, cw_recv, k, (px, py, c)).wait_recv()
        for cp in small:
            cp.wait_send()
        local.wait()

    outs = pl.pallas_call(
        body, name="gather_weights",
        in_specs=[ANY] * (n + 1), out_specs=[ANY] * (n + 1),
        out_shape=rider.out_shape + [jax.ShapeDtypeStruct((N_CHIPS,) + conv_w.shape, F32)],
        input_output_aliases=rider.aliases,
        scratch_shapes=rider.scratch + [pltpu.SemaphoreType.DMA((3,)), pltpu.SemaphoreType.DMA((3,)),
                                        pltpu.SemaphoreType.DMA((1,))],
    )(*fulls, conv_w)
    return outs[:n], outs[n]


def _swap_rider(items):
    n = len(items)
    halves = []
    for a, axis in items:
        shp = list(a.shape)
        shp[axis] //= 2
        halves.append(tuple(shp))

    def copies(ins, outs, sems):
        x, y, c = _mesh_position()
        return [_remote(_half(ins[k], items[k][1], 1 - c, halves[k][items[k][1]]), outs[k], sems[0], sems[1], k,
                        (x, y, 1 - c)) for k in range(n)]

    def start(ins, outs, sems):
        for cp in copies(ins, outs, sems):
            cp.start()

    def finish(ins, outs, sems):
        for cp in copies(ins, outs, sems):
            cp.wait()

    return _Rider([a for a, _ in items], [jax.ShapeDtypeStruct(h, a.dtype) for h, (a, _) in zip(halves, items)], {},
                  [pltpu.SemaphoreType.DMA((n,)), pltpu.SemaphoreType.DMA((n,))], start, None, finish)


def _swap_halves(items, name):
    rider = _swap_rider(items)
    n = len(items)

    def body(*refs):
        parts = (refs[:n], refs[n:2 * n], refs[2 * n:])
        rider.start(*parts)
        rider.finish(*parts)

    return pl.pallas_call(
        body, name=name, in_specs=[ANY] * n, out_specs=[ANY] * n, out_shape=rider.out_shape,
        scratch_shapes=rider.scratch,
    )(*rider.inputs)


def _grad_exchange_rider(items):
    n = len(items)
    slices = []
    for a, axis in items:
        shp = list(a.shape)
        shp[axis] //= N_CHIPS
        slices.append(tuple(shp))

    def copies(ins, outs, sems):
        send_sems, recv_sems = sems
        x, y, c = _mesh_position()
        made = []
        for k in range(n):
            axis = items[k][1]
            for r, (px, py) in enumerate(_other_chips(x, y)):
                made.append(_remote(_half(ins[k], axis, 2 * px + py, slices[k][axis]), outs[k].at[r],
                                    send_sems, recv_sems, 3 * k + r, (px, py, c)))
        return made

    def start(ins, outs, sems):
        for cp in copies(ins, outs, sems):
            cp.start()

    def finish(ins, outs, sems):
        for cp in copies(ins, outs, sems):
            cp.wait()

    return _Rider([a for a, _ in items], [jax.ShapeDtypeStruct((N_CHIPS - 1,) + s, BF16) for s in slices], {},
                  [pltpu.SemaphoreType.DMA((3 * n,)), pltpu.SemaphoreType.DMA((3 * n,))], start, None, finish)


def _small_gather_rider(small):
    def copies(ins, outs, sems):
        x, y, c = _mesh_position()
        me = 4 * x + 2 * y + c
        local = pltpu.make_async_copy(ins[0], outs[0].at[me], sems[2].at[0])
        remote = [_remote(ins[0], outs[0].at[me], sems[0], sems[1], r, (x ^ fx, y ^ fy, c ^ fc))
                  for r, (fx, fy, fc) in enumerate(ALL_FLIPS)]
        return local, remote

    def start(ins, outs, sems):
        local, remote = copies(ins, outs, sems)
        local.start()
        for cp in remote:
            cp.start()

    def finish(ins, outs, sems):
        local, remote = copies(ins, outs, sems)
        for cp in remote:
            cp.wait()
        local.wait()

    n = len(ALL_FLIPS)
    return _Rider([small], [jax.ShapeDtypeStruct((2 * N_CHIPS,) + small.shape, F32)], {},
                  [pltpu.SemaphoreType.DMA((n,)), pltpu.SemaphoreType.DMA((n,)), pltpu.SemaphoreType.DMA((1,))],
                  start, None, finish)


def _sum_chips(recv, own, shard_axis, split_axis, pos, dest, layer, name, tr=128):
    sl = recv.shape[1:]
    nd = len(sl)
    tiled = nd == 2 and sl[0] > tr
    nt = sl[0] // tr if tiled else 1
    block = ((tr,) + sl[1:]) if tiled else sl
    shard = list(sl)
    shard[split_axis] *= 2

    def recv_index(i, pos_ref):
        return (0, i) + (0,) * (nd - 1) if tiled else (0,) * (nd + 1)

    def own_index(i, pos_ref):
        idx = [0] * nd
        idx[shard_axis] = pos_ref[1]
        if tiled:
            idx[0] = pos_ref[1] * nt + i if shard_axis == 0 else i
        return tuple(idx)

    def out_index(i, pos_ref):
        idx = [0] * nd
        idx[split_axis] = pos_ref[0]
        if tiled:
            idx[0] = pos_ref[0] * nt + i if split_axis == 0 else i
        return (layer, *idx)

    def body(pos_ref, recv_ref, own_ref, *rest):
        o_ref = rest[-1]
        tot = own_ref[...].astype(F32)
        for k in range(N_CHIPS - 1):
            tot = tot + recv_ref[k].astype(F32)
        o_ref[0] = tot

    in_specs = [pl.BlockSpec((N_CHIPS - 1,) + block, recv_index), pl.BlockSpec(block, own_index)]
    args = [pos, recv, own]
    aliases = {}
    if dest is not None:
        in_specs.append(ANY)
        args.append(dest)
        aliases = {3: 0}
    return pl.pallas_call(
        body, name=name,
        grid_spec=pltpu.PrefetchScalarGridSpec(
            num_scalar_prefetch=1, grid=(nt,), in_specs=in_specs,
            out_specs=pl.BlockSpec((1,) + block, out_index)),
        out_shape=jax.ShapeDtypeStruct((DEPTH,) + tuple(shard), F32),
        input_output_aliases=aliases,
    )(*args)


def _share_halves(bufs, late_small, name):
    n = len(bufs)
    small = _small_gather_rider(late_small)

    def body(*refs):
        outs, (send_sems, recv_sems) = refs[n + 1:2 * n + 1], refs[2 * n + 2:2 * n + 4]
        small_parts = ([refs[n]], [refs[2 * n + 1]], refs[2 * n + 4:])
        x, y, c = _mesh_position()
        small.start(*small_parts)
        copies = []
        for k, (a, axis) in enumerate(bufs):
            size = a.shape[1 + axis] // 2
            mine = _half(outs[k], 1 + axis, c, size)
            copies.append(_remote(mine, mine, send_sems, recv_sems, k, (x, y, 1 - c)))
        for cp in copies:
            cp.start()
        for cp in copies:
            cp.wait()
        small.finish(*small_parts)

    outs = pl.pallas_call(
        body, name=name, in_specs=[ANY] * (n + 1), out_specs=[ANY] * (n + 1),
        out_shape=[jax.ShapeDtypeStruct(a.shape, F32) for a, _ in bufs] + small.out_shape,
        input_output_aliases={k: k for k in range(n)},
        scratch_shapes=[pltpu.SemaphoreType.DMA((n,)), pltpu.SemaphoreType.DMA((n,))] + small.scratch,
    )(*[a for a, _ in bufs], late_small)
    return outs[:n], outs[n]


def _layer_fwd(x, p, l, rider=None, proj_rider=None):
    tag = f"l{l}_"
    h = _rms_fwd(x, p["pre_g"], tag + "pre_norm")
    u = _matmul(h, p["w_in"], "nn", BF16, tag + "in_proj", rider=proj_rider)
    if proj_rider is not None:
        u, (w_branch, w_out) = u
        p = dict(p, w_branch=w_branch, w_out=w_out)
    y_pool = _pool_fwd(u, p["pool_w"], p["pool_scale"], tag + "pool")
    y_conv = _conv_fwd(u, p["conv_w"], p["conv_b"], tag + "conv")
    (o_sb, y_sb, sb_after), carried = _sb_fwd(u, tag + "stickbreak", rider=rider)
    ys = [y_pool, y_conv, y_sb]
    projs, merged = _gate_fwd(u, ys, p["w_branch"], tag + "merge")
    out = _matmul(merged, p["w_out"], "nn", F32, tag + "out_proj")
    saved = dict(x=x, h=h, u=u, ys=ys, o_sb=o_sb, sb_after=sb_after, projs=projs, merged=merged, out=out)
    return out, saved, carried, p


def _layer_bwd(dy, p, saved, l, merge_rider=None, early=None, before_dw=None, late=None):
    tag = f"l{l}_bwd_"
    u = saved["u"]
    d_out, g_post = _rms_bwd(saved["out"], p["post_g"], dy, None, BF16, tag + "post_norm")
    d_merged = _matmul(d_out, p["w_out"], "nt", BF16, tag + "out_proj_dx")
    g_w_out = _matmul(saved["merged"], d_out, "tn", BF16, tag + "out_proj_dw", tk=2048)
    d_projs, d_logits, d_ys, carried_merge = _gate_bwd(u, saved["projs"], d_merged, p["w_branch"], tag + "merge",
                                                       rider=merge_rider)
    g_w_branch = jnp.stack([_matmul(saved["ys"][n], d_projs[n], "tn", BF16, tag + f"branch_dw{n}", tk=2048)
                            for n in range(3)])
    rider = early(g_w_branch, g_w_out, carried_merge) if early else None
    d_pv, d_pg, g_pool_w, g_pool_scale = _pool_bwd(u, d_ys[0], p["pool_w"], p["pool_scale"], tag + "pool")
    d_cx, d_cgb, d_cgc, d_cg, g_conv_w, g_conv_b = _conv_bwd(u, d_ys[1], p["conv_w"], p["conv_b"], tag + "conv")
    (d_q, d_k, d_v, d_sg), carried_attn = _sb_bwd(u, saved["o_sb"], saved["sb_after"], d_ys[2], tag + "stickbreak",
                                                  rider=rider)
    du = jnp.concatenate([d_pv, d_pg, d_cx, d_cgb, d_cgc, d_cg, d_q, d_k, d_v, d_sg] + list(d_logits), axis=1)
    grads = dict(w_branch=g_w_branch, w_out=g_w_out, post_g=g_post, pool_w=g_pool_w, pool_scale=g_pool_scale,
                 conv_w=g_conv_w, conv_b=g_conv_b)
    rider = before_dw(grads) if before_dw else None
    g_w_in = _matmul(saved["h"], du, "tn", BF16, tag + "in_proj_dw", tk=2048, rider=rider)
    g_w_in, carried_dw = g_w_in if rider else (g_w_in, [])
    rider = late(g_w_in) if late else None
    dh = _matmul(du, p["w_in"], "nt", BF16, tag + "in_proj_dx", tk=2048, rider=rider)
    dh, carried_dx = dh if rider else (dh, [])
    dx, g_pre = _rms_bwd(saved["x"], p["pre_g"], dh, dy, F32, tag + "pre_norm")
    grads.update(w_in=g_w_in, pre_g=g_pre)
    return dx, grads, carried_attn, carried_dw, carried_dx


SMALL_ORDER = ["pre_g", "pool_w", "pool_scale", "conv_w", "conv_b", "post_g"]


def _pack_small(per_layer):
    parts, spans, at = [], {}, 0
    for name in SMALL_ORDER:
        a = jnp.stack([per_layer[l][name] for l in range(DEPTH)]).reshape(-1, LANE)
        parts.append(a)
        spans[name] = (at, a.shape[0])
        at += a.shape[0]
    return jnp.concatenate(parts, axis=0), spans


def kernel(x, pre_norm_g, w_in, pool_w, pool_scale, conv_w, conv_b, w_branch, w_out, post_norm_g, loss_target, m_pre_norm_g, m_w_in, m_pool_w, m_pool_scale, m_conv_w, m_conv_b, m_w_branch, m_w_out, m_post_norm_g, v_pre_norm_g, v_w_in, v_pool_w, v_pool_scale, v_conv_w, v_conv_b, v_w_branch, v_w_out, v_post_norm_g):
    mx, my, mc = _mesh_position()
    chip = 2 * mx + my
    core = mc.astype(jnp.int32).reshape(1)
    pos = jnp.stack([mc, chip]).astype(jnp.int32)

    names = ["w_in", "w_branch", "w_out"]
    given = dict(w_in=w_in, w_branch=w_branch, w_out=w_out)
    in_place = [[_cast_shard(given[n], l, WEIGHT_LAYOUT[i][0], pos, f"cast_{n}{l}") for i, n in enumerate(names)]
                for l in range(DEPTH)]
    (w_in_0,), conv_w_by_chip = _gather_weights(in_place[0][:1], WEIGHT_LAYOUT[:1], conv_w)
    gathered = [w_in_0] + in_place[0][1:]
    conv_w_f = conv_w_by_chip.transpose(1, 2, 0, 3).reshape(DEPTH, 3, WIDTH)
    pool_w_b = _cast_bf16(pool_w, "cast_pool_w")

    def layer_params(l, big):
        return dict(pre_g=pre_norm_g[l:l + 1], post_g=post_norm_g[l:l + 1], w_in=big[0], w_branch=big[1],
                    w_out=big[2], pool_w=pool_w_b[l], pool_scale=pool_scale[l:l + 1], conv_w=conv_w_f[l],
                    conv_b=conv_b[l:l + 1])

    act = x[0]
    params, saved = [], []
    for l in range(DEPTH):
        rider = _weight_gather_rider(in_place[l + 1], WEIGHT_LAYOUT) if l + 1 < DEPTH else None
        proj_rider = _weight_gather_rider(gathered[1:], WEIGHT_LAYOUT[1:]) if l == 0 else None
        out, sv, gathered, layer_p = _layer_fwd(act, layer_params(l, gathered), l, rider, proj_rider)
        params.append(layer_p)
        saved.append(sv)
        if l < DEPTH - 1:
            act = _resid_out(act, out, params[l]["post_g"], None, f"l{l}_resid")
    dy, loss_part = _resid_out(act, saved[-1]["out"], params[-1]["post_g"], loss_target[0], "loss_head")

    split_axis = dict(w_in=0, w_branch=1, w_out=1)
    shard_axis = dict(w_in=1, w_branch=2, w_out=0)
    grads = [None] * DEPTH
    chip_sums = [dict() for _ in range(DEPTH)]
    by_chip = [dict() for _ in range(DEPTH)]

    def reduce_in_chip(l, which, g):
        items = [(g[n], split_axis[n]) for n in which]
        from_sibling = _swap_halves(items, f"swap_grad_halves{l}_{which[0]}")
        for n, (a, axis), r in zip(which, items, from_sibling):
            chip_sums[l][n] = _chip_sum(a, r, axis, core, f"chip_sum{l}_{n}")

    def exchange_rider(keys):
        return _grad_exchange_rider([(chip_sums[l][n], shard_axis[n]) for l, n in keys])

    waiting = []
    for l in reversed(range(DEPTH)):
        sent_early, sent_late = list(waiting) + [(l, "w_branch"), (l, "w_out")], [(l, "w_in")]
        waiting_items = [(grads[ll][n], split_axis[n]) for ll, n in waiting]

        def early(g_w_branch, g_w_out, from_sibling, l=l, keys=sent_early, above=tuple(waiting), items=waiting_items):
            for (ll, n), (a, axis), r in zip(above, items, from_sibling):
                chip_sums[ll][n] = _chip_sum(a, r, axis, core, f"chip_sum{ll}_{n}")
            reduce_in_chip(l, ["w_branch", "w_out"], dict(w_branch=g_w_branch, w_out=g_w_out))
            return exchange_rider(keys)

        def late(g_w_in, l=l, keys=sent_late):
            reduce_in_chip(l, ["w_in"], dict(w_in=g_w_in))
            return exchange_rider(keys)

        def before_dw(partial, l=l):
            layers = [dict(partial, pre_g=jnp.zeros_like(pre_norm_g[:1])) if ll == l else grads[ll]
                      for ll in range(DEPTH)]
            return _small_gather_rider(_pack_small(layers)[0])

        if l == DEPTH - 1:
            dy, grads[l], _, _, _ = _layer_bwd(dy, params[l], saved[l], l)
            waiting = [(l, n) for n in names]
        else:
            dy, grads[l], got_early, got_dw, got_late = _layer_bwd(
                dy, params[l], saved[l], l, _swap_rider(waiting_items), early, before_dw if l == 0 else None, late)
            for (ll, n), r in zip(sent_early + sent_late, list(got_early) + list(got_late)):
                by_chip[ll][n] = r
            if l == 0:
                small_all = got_dw[0]
            waiting = []
    assert not waiting
    grad_x = dy[None]
    _, spans = _pack_small(grads)
    late_small = jnp.concatenate([grads[0]["pre_g"].reshape(-1, LANE), jnp.broadcast_to(loss_part, (8, LANE))])
    bufs = []
    for n in names:
        dest = None
        for l in range(DEPTH):
            dest = _sum_chips(by_chip[l][n], chip_sums[l][n], shard_axis[n], split_axis[n], pos, dest, l,
                              f"sum_chips{l}_{n}")
        bufs.append((dest, split_axis[n]))
    (g_w_in, g_w_branch, g_w_out), late_all = _share_halves(bufs, late_small, "share_grad_halves")

    late_sum = _sum_slots(late_all, F32, "sum_late_small")
    n_gain = late_small.shape[0] - 8
    loss = late_sum[n_gain, 0]
    small_sum = _sum_slots(small_all, F32, "sum_small")
    at, _ = spans["pre_g"]
    small_sum = jnp.concatenate([small_sum[:at], late_sum[:n_gain], small_sum[at + n_gain:]])
    small = {}
    for name, like in (("pre_g", pre_norm_g), ("pool_w", pool_w), ("pool_scale", pool_scale), ("conv_b", conv_b),
                       ("post_g", post_norm_g)):
        at, n = spans[name]
        small[name] = small_sum[at:at + n].reshape(like.shape)
    at, n = spans["conv_w"]
    g_conv_w_full = small_sum[at:at + n].reshape(DEPTH, 3, WIDTH)
    g_conv_w = lax.dynamic_slice_in_dim(g_conv_w_full, chip * conv_w.shape[2], conv_w.shape[2], axis=2)

    g = dict(pre_norm_g=small["pre_g"], w_in=g_w_in, pool_w=small["pool_w"], pool_scale=small["pool_scale"],
             conv_w=g_conv_w, conv_b=small["conv_b"], w_branch=g_w_branch, w_out=g_w_out, post_norm_g=small["post_g"])
    w = dict(pre_norm_g=pre_norm_g, w_in=w_in, pool_w=pool_w, pool_scale=pool_scale, conv_w=conv_w, conv_b=conv_b,
             w_branch=w_branch, w_out=w_out, post_norm_g=post_norm_g)
    m = dict(pre_norm_g=m_pre_norm_g, w_in=m_w_in, pool_w=m_pool_w, pool_scale=m_pool_scale, conv_w=m_conv_w,
             conv_b=m_conv_b, w_branch=m_w_branch, w_out=m_w_out, post_norm_g=m_post_norm_g)
    v = dict(pre_norm_g=v_pre_norm_g, w_in=v_w_in, pool_w=v_pool_w, pool_scale=v_pool_scale, conv_w=v_conv_w,
             conv_b=v_conv_b, w_branch=v_w_branch, w_out=v_w_out, post_norm_g=v_post_norm_g)
    order = ["pre_norm_g", "w_in", "pool_w", "pool_scale", "conv_w", "conv_b", "w_branch", "w_out", "post_norm_g"]
    upd = {n: _adamw(w[n], g[n], m[n], v[n], "adamw_" + n) for n in order}
    return (loss, grad_x, *[upd[n][0] for n in order], *[upd[n][1] for n in order], *[upd[n][2] for n in order],
            *[upd[n][3] for n in order])
```

```python
import functools

import jax
import jax.numpy as jnp
from jax import lax
from jax.experimental import pallas as pl
from jax.experimental.pallas import tpu as pltpu

F32 = jnp.float32
BF16 = jnp.bfloat16
MESH = pl.DeviceIdType.MESH
ANY = pl.BlockSpec(memory_space=pl.ANY)

DEPTH = 2
D_MODEL = 1024
WIDTH = 512
N_IN = 8192
N_CHIPS = 4
HEAD_DIM = 64
RMS_EPS = 1e-6
POOL_HALO = 16
CONV_HALO = 16
LANE = 128
COL_POOL_V, COL_POOL_G = 0, 4
COL_CONV_X, COL_CONV_GB, COL_CONV_GC, COL_CONV_G = 8, 12, 16, 20
COL_SB_Q, COL_SB_K, COL_SB_V, COL_SB_G = 24, 28, 32, 36
COL_MERGE_1024 = 5

ADAM_LR, ADAM_B1, ADAM_B2, ADAM_EPS, ADAM_WD, ADAM_STEP = 0.001, 0.9, 0.999, 1e-08, 0.01, 10

NN = (((1,), (0,)), ((), ()))
NT = (((1,), (1,)), ((), ()))
TN = (((0,), (0,)), ((), ()))


def _sigmoid(x):
    return 1.0 / (1.0 + jnp.exp(-x))


def _silu_and_grad(x):
    s = _sigmoid(x)
    return x * s, s * (1.0 + x * (1.0 - s))


def _dot(a, b, dims):
    return lax.dot_general(a, b, dims, preferred_element_type=F32)


def _matmul(a, b, mode, out_dtype, name, tm=1024, tn=1024, tk=1024, b_lead=(), rider=None):
    b_shape = b.shape[len(b_lead):]
    if mode == "nn":
        (m, k), (k2, n) = a.shape, b_shape
    elif mode == "nt":
        (m, k), (n, k2) = a.shape, b_shape
    else:
        (k, m), (k2, n) = a.shape, b_shape
    assert k == k2 and a.dtype == BF16 and b.dtype == BF16
    tm, tn, tk = min(tm, m), min(tn, n), min(tk, k)
    assert m % tm == 0 and n % tn == 0 and k % tk == 0
    nk = k // tk
    dims = {"nn": NN, "nt": NT, "tn": TN}[mode]

    grid = (m // tm, n // tn, nk)

    def at_step(step):
        return functools.reduce(jnp.logical_and, [pl.program_id(d) == s for d, s in enumerate(step)])

    def body(*refs):
        (a_ref, b_ref, o_ref, *scratch), riding = _split_refs(refs, 2, 1, 1 if nk > 1 else 0, rider)
        _ride(rider, "start", at_step((0, 0, 0)), riding)
        _ride(rider, "middle", at_step(((3 * grid[0]) // 4, 0, 0)), riding)
        compute(a_ref, b_ref, o_ref, scratch)
        _ride(rider, "finish", at_step([g - 1 for g in grid]), riding)

    def compute(a_ref, b_ref, o_ref, scratch):
        p = _dot(a_ref[...], b_ref[...], dims)
        if nk == 1:
            o_ref[...] = p.astype(o_ref.dtype)
        else:
            acc = scratch[0]
            kk = pl.program_id(2)

            @pl.when(kk == 0)
            def _():
                acc[...] = p

            @pl.when(jnp.logical_and(kk > 0, kk < nk - 1))
            def _():
                acc[...] += p

            @pl.when(kk == nk - 1)
            def _():
                o_ref[...] = (acc[...] + p).astype(o_ref.dtype)

    if mode == "tn":
        a_spec = pl.BlockSpec((tk, tm), lambda i, j, kk: (kk, i))
    else:
        a_spec = pl.BlockSpec((tm, tk), lambda i, j, kk: (i, kk))
    squeezed = (None,) * len(b_lead)
    if mode == "nt":
        b_spec = pl.BlockSpec(squeezed + (tn, tk), lambda i, j, kk: (*b_lead, j, kk))
    else:
        b_spec = pl.BlockSpec(squeezed + (tk, tn), lambda i, j, kk: (*b_lead, kk, j))
    extra = _rider_call_args(rider, 2, 1)
    outs = pl.pallas_call(
        body, name=name, grid=grid,
        in_specs=[a_spec, b_spec] + extra["in_specs"],
        out_specs=[pl.BlockSpec((tm, tn), lambda i, j, kk: (i, j))] + extra["out_specs"],
        out_shape=[jax.ShapeDtypeStruct((m, n), out_dtype)] + extra["out_shape"],
        input_output_aliases=extra["aliases"],
        scratch_shapes=([pltpu.VMEM((tm, tn), F32)] if nk > 1 else []) + extra["scratch"],
        compiler_params=pltpu.CompilerParams(dimension_semantics=("arbitrary",) * 3 if rider else
                                             ("parallel", "parallel", "arbitrary")),
    )(a, b, *extra["inputs"])
    return (outs[0], outs[1:]) if rider else outs[0]


def _rms_fwd(x, g, name, ts=512):
    s, d = x.shape

    def body(x_ref, g_ref, h_ref):
        xv = x_ref[...]
        r = lax.rsqrt(jnp.mean(xv * xv, axis=-1, keepdims=True) + RMS_EPS)
        h_ref[...] = (xv * r * g_ref[...]).astype(BF16)

    return pl.pallas_call(
        body, name=name, grid=(s // ts,),
        in_specs=[pl.BlockSpec((ts, d), lambda i: (i, 0)), pl.BlockSpec((1, d), lambda i: (0, 0))],
        out_specs=pl.BlockSpec((ts, d), lambda i: (i, 0)),
        out_shape=jax.ShapeDtypeStruct((s, d), BF16),
    )(x, g)


def _rms_bwd(xin, g, dh, resid, out_dtype, name, ts=512):
    s, d = xin.shape
    has_resid = resid is not None

    def body(*refs):
        if has_resid:
            x_ref, g_ref, dh_ref, res_ref, dx_ref, dg_ref = refs
        else:
            x_ref, g_ref, dh_ref, dx_ref, dg_ref = refs
        xv = x_ref[...]
        dhv = dh_ref[...].astype(F32)
        r = lax.rsqrt(jnp.mean(xv * xv, axis=-1, keepdims=True) + RMS_EPS)
        nrm = xv * r
        dn = dhv * g_ref[...]
        dx = r * (dn - nrm * jnp.mean(dn * nrm, axis=-1, keepdims=True))
        if has_resid:
            dx = dx + res_ref[...]
        dx_ref[...] = dx.astype(dx_ref.dtype)
        part = jnp.sum(dhv * nrm, axis=0, keepdims=True)

        @pl.when(pl.program_id(0) == 0)
        def _():
            dg_ref[...] = part

        @pl.when(pl.program_id(0) > 0)
        def _():
            dg_ref[...] += part

    row = pl.BlockSpec((ts, d), lambda i: (i, 0))
    vec = pl.BlockSpec((1, d), lambda i: (0, 0))
    ins = [xin, g, dh] + ([resid] if has_resid else [])
    return pl.pallas_call(
        body, name=name, grid=(s // ts,),
        in_specs=[row, vec, row] + ([row] if has_resid else []),
        out_specs=[row, vec],
        out_shape=[jax.ShapeDtypeStruct((s, d), out_dtype), jax.ShapeDtypeStruct((1, d), F32)],
        compiler_params=pltpu.CompilerParams(dimension_semantics=("arbitrary",)),
    )(*ins)


def _resid_out(x, out, g, target, name, ts=512):
    s, d = x.shape
    has_loss = target is not None

    def body(*refs):
        if has_loss:
            x_ref, o_ref, g_ref, t_ref, dy_ref, loss_ref = refs
        else:
            x_ref, o_ref, g_ref, y_ref = refs
        ov = o_ref[...]
        r = lax.rsqrt(jnp.mean(ov * ov, axis=-1, keepdims=True) + RMS_EPS)
        yv = x_ref[...] + ov * r * g_ref[...]
        if not has_loss:
            y_ref[...] = yv
            return
        err = yv - t_ref[...]
        dy_ref[...] = err * (1.0 / d)
        part = jnp.sum(jnp.sum(err * err, axis=-1, keepdims=True), axis=0, keepdims=True) * (0.5 / d)
        part = jnp.broadcast_to(part, (1, LANE))

        @pl.when(pl.program_id(0) == 0)
        def _():
            loss_ref[...] = part

        @pl.when(pl.program_id(0) > 0)
        def _():
            loss_ref[...] += part

    row = pl.BlockSpec((ts, d), lambda i: (i, 0))
    vec = pl.BlockSpec((1, d), lambda i: (0, 0))
    if has_loss:
        return pl.pallas_call(
            body, name=name, grid=(s // ts,),
            in_specs=[row, row, vec, row],
            out_specs=[row, pl.BlockSpec((1, LANE), lambda i: (0, 0))],
            out_shape=[jax.ShapeDtypeStruct((s, d), F32), jax.ShapeDtypeStruct((1, LANE), F32)],
            compiler_params=pltpu.CompilerParams(dimension_semantics=("arbitrary",)),
        )(x, out, g, target)
    return pl.pallas_call(
        body, name=name, grid=(s // ts,),
        in_specs=[row, row, vec], out_specs=row,
        out_shape=jax.ShapeDtypeStruct((s, d), F32),
    )(x, out, g)


def _rows_before(ref, start, n, halo):
    if start == 0:
        return jnp.concatenate([jnp.zeros((halo, ref.shape[1]), F32), ref[0:n, :].astype(F32)], axis=0)
    return ref[start - halo:start + n, :].astype(F32)


def _rows_after(ref, start, n, halo):
    if start + n == ref.shape[0]:
        return jnp.concatenate([ref[start:start + n, :].astype(F32), jnp.zeros((halo, ref.shape[1]), F32)], axis=0)
    return ref[start:start + n + halo, :].astype(F32)


def _pick_window(group, s2, s4, s8, s16):
    return jnp.where(group == 0, s2, jnp.where(group == 1, s4, jnp.where(group == 2, s8, s16)))


def _trailing_sums(ext, group):
    s2 = ext + pltpu.roll(ext, 1, 0)
    s4 = s2 + pltpu.roll(s2, 2, 0)
    s8 = s4 + pltpu.roll(s4, 4, 0)
    s16 = s8 + pltpu.roll(s8, 8, 0)
    return _pick_window(group, s2, s4, s8, s16)


def _leading_sums(ext, group):
    n = ext.shape[0]
    s2 = ext + pltpu.roll(ext, n - 1, 0)
    s4 = s2 + pltpu.roll(s2, n - 2, 0)
    s8 = s4 + pltpu.roll(s4, n - 4, 0)
    s16 = s8 + pltpu.roll(s8, n - 8, 0)
    return _pick_window(group, s2, s4, s8, s16)


def _window_count(start, n, group):
    pos = start + lax.broadcasted_iota(jnp.int32, (n, LANE), 0)
    return jnp.minimum(pos + 1, 2 << group).astype(F32)


def _pooled(v_ref, start, n, group):
    ext = _rows_before(v_ref, start, n, POOL_HALO)
    sums = _trailing_sums(ext, group)[POOL_HALO:, :]
    return sums / _window_count(start, n, group) - ext[POOL_HALO:, :]


def _pool_fwd(u, pool_w, pool_scale, name, ts=512):
    s = u.shape[0]

    def body(v_ref, gate_ref, w_ref, sc_ref, y_ref):
        group = pl.program_id(0)
        for c in range(s // ts):
            a = c * ts
            pooled = _pooled(v_ref, a, ts, group)
            mixed = _dot(pooled.astype(BF16), w_ref[...], NN)
            gate = gate_ref[a:a + ts, :].astype(F32)
            y_ref[a:a + ts, :] = (mixed * sc_ref[...] * (gate * _sigmoid(gate))).astype(BF16)

    col = lambda base: pl.BlockSpec((s, LANE), lambda g: (0, base + g))
    return pl.pallas_call(
        body, name=name, grid=(4,),
        in_specs=[col(COL_POOL_V), col(COL_POOL_G),
                  pl.BlockSpec((None, LANE, LANE), lambda g: (g, 0, 0)),
                  pl.BlockSpec((1, LANE), lambda g: (0, g))],
        out_specs=pl.BlockSpec((s, LANE), lambda g: (0, g)),
        out_shape=jax.ShapeDtypeStruct((s, WIDTH), BF16),
    )(u, u, pool_w, pool_scale)


def _pool_bwd(u, dy, pool_w, pool_scale, name, ts=512):
    s = u.shape[0]

    def body(v_ref, gate_ref, dy_ref, w_ref, sc_ref, dv_ref, dgate_ref, dw_ref, dsc_ref):
        group = pl.program_id(0)
        w = w_ref[...]
        scale = sc_ref[...]
        dw = jnp.zeros((LANE, LANE), F32)
        dsc = jnp.zeros((1, LANE), F32)
        for c in range(s // ts):
            a = c * ts
            n_ext = ts + POOL_HALO
            gate_e = _rows_after(gate_ref, a, ts, POOL_HALO)
            dy_e = _rows_after(dy_ref, a, ts, POOL_HALO)
            silu_e, dsilu_e = _silu_and_grad(gate_e)
            dms_e = dy_e * silu_e
            dm_e = (dms_e * scale).astype(BF16)
            dpool_e = _dot(dm_e, w, NT)
            spread = _leading_sums(dpool_e / _window_count(a, n_ext, group), group)
            dv_ref[a:a + ts, :] = (spread[0:ts, :] - dpool_e[0:ts, :]).astype(BF16)
            pooled = _pooled(v_ref, a, ts, group).astype(BF16)
            mixed = _dot(pooled, w, NN)
            dgate_ref[a:a + ts, :] = (dy_e[0:ts, :] * mixed * scale * dsilu_e[0:ts, :]).astype(BF16)
            dsc = dsc + jnp.sum(dms_e[0:ts, :] * mixed, axis=0, keepdims=True)
            dw = dw + _dot(pooled, dm_e[0:ts, :], TN)
        dw_ref[...] = dw
        dsc_ref[...] = dsc

    col = lambda base: pl.BlockSpec((s, LANE), lambda g: (0, base + g))
    out_col = pl.BlockSpec((s, LANE), lambda g: (0, g))
    return pl.pallas_call(
        body, name=name, grid=(4,),
        in_specs=[col(COL_POOL_V), col(COL_POOL_G), out_col,
                  pl.BlockSpec((None, LANE, LANE), lambda g: (g, 0, 0)),
                  pl.BlockSpec((1, LANE), lambda g: (0, g))],
        out_specs=[out_col, out_col,
                   pl.BlockSpec((None, LANE, LANE), lambda g: (g, 0, 0)),
                   pl.BlockSpec((1, LANE), lambda g: (0, g))],
        out_shape=[jax.ShapeDtypeStruct((s, WIDTH), BF16), jax.ShapeDtypeStruct((s, WIDTH), BF16),
                   jax.ShapeDtypeStruct((4, LANE, LANE), F32), jax.ShapeDtypeStruct((1, WIDTH), F32)],
    )(u, u, dy, pool_w, pool_scale)


def _conv_taps(x_ref, gc_ref, start, n):
    z_ext = _rows_before(gc_ref, start, n, CONV_HALO) * _rows_before(x_ref, start, n, CONV_HALO)
    z0 = z_ext[CONV_HALO:, :]
    z1 = pltpu.roll(z_ext, 1, 0)[CONV_HALO:, :]
    z2 = pltpu.roll(z_ext, 2, 0)[CONV_HALO:, :]
    return z0, z1, z2


def _conv_fwd(u, conv_w, conv_b, name, ts=512):
    s = u.shape[0]

    def body(x_ref, gb_ref, gc_ref, g_ref, w_ref, b_ref, y_ref):
        w0, w1, w2 = w_ref[0:1, :], w_ref[1:2, :], w_ref[2:3, :]
        for c in range(s // ts):
            a = c * ts
            z0, z1, z2 = _conv_taps(x_ref, gc_ref, a, ts)
            y = w2 * z0 + w1 * z1 + w0 * z2 + b_ref[...]
            gate = g_ref[a:a + ts, :].astype(F32)
            y_ref[a:a + ts, :] = (gb_ref[a:a + ts, :].astype(F32) * y * (gate * _sigmoid(gate))).astype(BF16)

    col = lambda base: pl.BlockSpec((s, LANE), lambda j: (0, base + j))
    return pl.pallas_call(
        body, name=name, grid=(4,),
        in_specs=[col(COL_CONV_X), col(COL_CONV_GB), col(COL_CONV_GC), col(COL_CONV_G),
                  pl.BlockSpec((3, LANE), lambda j: (0, j)), pl.BlockSpec((1, LANE), lambda j: (0, j))],
        out_specs=pl.BlockSpec((s, LANE), lambda j: (0, j)),
        out_shape=jax.ShapeDtypeStruct((s, WIDTH), BF16),
    )(u, u, u, u, conv_w, conv_b)


def _conv_bwd(u, dy, conv_w, conv_b, name, ts=512):
    s = u.shape[0]

    def body(x_ref, gb_ref, gc_ref, g_ref, dy_ref, w_ref, b_ref,
             dx_ref, dgb_ref, dgc_ref, dg_ref, dw_ref, db_ref):
        w0, w1, w2 = w_ref[0:1, :], w_ref[1:2, :], w_ref[2:3, :]
        acc = [jnp.zeros((1, LANE), F32) for _ in range(4)]
        for c in range(s // ts):
            a = c * ts
            n_ext = ts + CONV_HALO
            gate_e = _rows_after(g_ref, a, ts, CONV_HALO)
            silu_e, dsilu_e = _silu_and_grad(gate_e)
            dy_e = _rows_after(dy_ref, a, ts, CONV_HALO)
            gb_e = _rows_after(gb_ref, a, ts, CONV_HALO)
            dyy_e = dy_e * silu_e * gb_e
            dz = (w2 * dyy_e + w1 * pltpu.roll(dyy_e, n_ext - 1, 0) + w0 * pltpu.roll(dyy_e, n_ext - 2, 0))[0:ts, :]
            z0, z1, z2 = _conv_taps(x_ref, gc_ref, a, ts)
            yb = w2 * z0 + w1 * z1 + w0 * z2 + b_ref[...]
            dyv = dy_e[0:ts, :]
            dyy = dyy_e[0:ts, :]
            dg_ref[a:a + ts, :] = (dyv * gb_e[0:ts, :] * yb * dsilu_e[0:ts, :]).astype(BF16)
            dgb_ref[a:a + ts, :] = (dyv * silu_e[0:ts, :] * yb).astype(BF16)
            dx_ref[a:a + ts, :] = (dz * gc_ref[a:a + ts, :].astype(F32)).astype(BF16)
            dgc_ref[a:a + ts, :] = (dz * x_ref[a:a + ts, :].astype(F32)).astype(BF16)
            for i, term in enumerate((dyy * z2, dyy * z1, dyy * z0, dyy)):
                acc[i] = acc[i] + jnp.sum(term, axis=0, keepdims=True)
        dw_ref[0:1, :] = acc[0]
        dw_ref[1:2, :] = acc[1]
        dw_ref[2:3, :] = acc[2]
        db_ref[...] = acc[3]

    col = lambda base: pl.BlockSpec((s, LANE), lambda j: (0, base + j))
    out_col = pl.BlockSpec((s, LANE), lambda j: (0, j))
    big = jax.ShapeDtypeStruct((s, WIDTH), BF16)
    return pl.pallas_call(
        body, name=name, grid=(4,),
        in_specs=[col(COL_CONV_X), col(COL_CONV_GB), col(COL_CONV_GC), col(COL_CONV_G), out_col,
                  pl.BlockSpec((3, LANE), lambda j: (0, j)), pl.BlockSpec((1, LANE), lambda j: (0, j))],
        out_specs=[out_col, out_col, out_col, out_col,
                   pl.BlockSpec((3, LANE), lambda j: (0, j)), pl.BlockSpec((1, LANE), lambda j: (0, j))],
        out_shape=[big, big, big, big,
                   jax.ShapeDtypeStruct((3, WIDTH), F32), jax.ShapeDtypeStruct((1, WIDTH), F32)],
    )(u, u, u, u, dy, conv_w, conv_b)


LOG2_E = 1.4426950408889634
LN_2 = 0.6931471805599453


def _sb_scores(q_h, k_blk, valid, later_mat, carry):
    z = _dot(q_h, k_blk, NT)
    neg_z = -z
    soft = jnp.log(1.0 + jnp.exp2(jnp.minimum(z, neg_z))) * LOG2_E
    log_keep = jnp.minimum(neg_z, 0.0) - soft
    log_beta = log_keep + z
    if valid is not None:
        log_keep = jnp.where(valid, log_keep, 0.0)
    later = _dot(log_keep.astype(BF16), later_mat, NN) + carry
    return log_keep, log_beta, later


def _masked(valid, x):
    return x if valid is None else jnp.where(valid, x, 0.0)


def _diagonal_masks(tq, tk):
    r = lax.broadcasted_iota(jnp.int32, (tq, tk), 0)
    cidx = lax.broadcasted_iota(jnp.int32, (tq, tk), 1)
    return [cidx + d * tk < r for d in range(tq // tk)]


def _triangle(tk, op):
    r = lax.broadcasted_iota(jnp.int32, (tk, tk), 0)
    cidx = lax.broadcasted_iota(jnp.int32, (tk, tk), 1)
    return op(r, cidx).astype(BF16)


def _split_refs(refs, n_in, n_out, n_scratch, rider):
    r_in = len(rider.inputs) if rider else 0
    r_out = len(rider.out_shape) if rider else 0
    a, b = n_in + r_in, n_in + r_in + n_out + r_out
    own = refs[:n_in] + refs[a:a + n_out] + refs[b:b + n_scratch]
    return own, (refs[n_in:a], refs[a + n_out:b], refs[b + n_scratch:])


def _rider_call_args(rider, n_in, n_out):
    if rider is None:
        return dict(in_specs=[], out_specs=[], out_shape=[], aliases={}, scratch=[], inputs=[])
    return dict(in_specs=[ANY] * len(rider.inputs), out_specs=[ANY] * len(rider.out_shape),
                out_shape=list(rider.out_shape), scratch=list(rider.scratch), inputs=list(rider.inputs),
                aliases={n_in + a: n_out + b for a, b in rider.aliases.items()})


def _ride(rider, phase, when, parts):
    fn = getattr(rider, phase) if rider else None
    if fn is not None:
        pl.when(when)(lambda: fn(*parts))


def _sb_fwd(u, name, t=512, tk=256, pairs=4, rider=None):
    s = u.shape[0]
    assert s // tk <= LANE and 4 % pairs == 0 and t % tk == 0
    scale = HEAD_DIM ** -0.5
    nh = 2 * pairs
    wide = pairs * LANE
    ratio = t // tk
    groups, nq = 4 // pairs, s // t

    def body(*refs):
        own, riding = _split_refs(refs, 4, 3, 2, rider)
        q_ref, kb_ref, vb_ref, g_ref, o_ref, y_ref, after_ref, acc_ref, carry_ref = own
        grp = pl.program_id(0)
        i = pl.program_id(1)
        _ride(rider, "start", jnp.logical_and(grp == 0, i == 0), riding)
        _ride(rider, "middle", jnp.logical_and(grp == groups - 1, i == (3 * nq) // 4), riding)

        lane = lax.broadcasted_iota(jnp.int32, (t, LANE), 1)
        first = lane < HEAD_DIM
        after_ref[...] = jnp.zeros_like(after_ref)
        qv = q_ref[...].astype(F32) * (scale * LOG2_E)
        q_heads = []
        for p in range(pairs):
            qp = qv[:, p * LANE:(p + 1) * LANE]
            q_heads += [jnp.where(first, qp, 0.0).astype(BF16), jnp.where(first, 0.0, qp).astype(BF16)]
        later_mat = _triangle(tk, lambda r, cidx: r > cidx)
        acc_ref[...] = jnp.zeros_like(acc_ref)
        carry_ref[...] = jnp.zeros_like(carry_ref)

        def block(kb, valid, lo=0):
            rows = pl.ds(pl.multiple_of(kb * tk, tk), tk)
            k_blk = kb_ref[rows, :]
            v_blk = vb_ref[rows, :]
            carries = [carry_ref[h, lo:, :] for h in range(nh)]
            afters = [after_ref[lo:, h * LANE:(h + 1) * LANE] for h in range(nh)]
            accs = [acc_ref[h, lo:, :] for h in range(nh)]
            outs = []
            for h in range(nh):
                cols = slice((h // 2) * LANE, (h // 2 + 1) * LANE)
                log_keep, log_beta, later = _sb_scores(q_heads[h][lo:], k_blk[:, cols], valid, later_mat, carries[h])
                a = _masked(valid, jnp.exp2(log_beta + later))
                outs.append((accs[h] + _dot(a.astype(BF16), v_blk[:, cols], NN),
                             carries[h] + jnp.sum(log_keep, axis=1, keepdims=True),
                             jnp.where(lane[lo:] == kb, carries[h], afters[h])))
            for h in range(nh):
                acc_ref[h, lo:, :] = outs[h][0]
                carry_ref[h, lo:, :] = outs[h][1]
                after_ref[lo:, h * LANE:(h + 1) * LANE] = outs[h][2]

        def step(j, _):
            block(ratio * i - 1 - j, None)
            return 0

        masks = _diagonal_masks(t, tk)
        for d in reversed(range(ratio)):
            block(ratio * i + d, masks[d][d * tk:], d * tk)
        lax.fori_loop(0, ratio * i, step, 0)
        for p in range(pairs):
            cols = slice(p * LANE, (p + 1) * LANE)
            o = jnp.where(first, acc_ref[2 * p], acc_ref[2 * p + 1])
            o_ref[:, cols] = o
            gate = g_ref[:, cols].astype(F32)
            y_ref[:, cols] = (o * gate * _sigmoid(gate)).astype(BF16)
        _ride(rider, "finish", jnp.logical_and(grp == groups - 1, i == nq - 1), riding)

    blk = lambda base: pl.BlockSpec((t, wide), lambda g, i: (i, base // pairs + g))
    full = lambda base: pl.BlockSpec((s, wide), lambda g, i: (0, base // pairs + g))
    out_blk = pl.BlockSpec((t, wide), lambda g, i: (i, g))
    extra = _rider_call_args(rider, 4, 3)
    outs = pl.pallas_call(
        body, name=name, grid=(groups, nq),
        in_specs=[blk(COL_SB_Q), full(COL_SB_K), full(COL_SB_V), blk(COL_SB_G)] + extra["in_specs"],
        out_specs=[out_blk, out_blk, pl.BlockSpec((t, nh * LANE), lambda g, i: (i, g))] + extra["out_specs"],
        out_shape=[jax.ShapeDtypeStruct((s, WIDTH), F32), jax.ShapeDtypeStruct((s, WIDTH), BF16),
                   jax.ShapeDtypeStruct((s, 8 * LANE), F32)] + extra["out_shape"],
        input_output_aliases=extra["aliases"],
        scratch_shapes=[pltpu.VMEM((nh, t, LANE), F32), pltpu.VMEM((nh, t, 1), F32)] + extra["scratch"],
        compiler_params=pltpu.CompilerParams(dimension_semantics=("arbitrary", "arbitrary")),
    )(u, u, u, u, *extra["inputs"])
    return outs[:3], outs[3:]


def _sb_bwd(u, o, after, dy, name, t=512, tk=256, pairs=2, rider=None):
    s = u.shape[0]
    nq = s // t
    scale = HEAD_DIM ** -0.5
    nh = 2 * pairs
    wide = pairs * LANE
    ratio = t // tk
    groups = 4 // pairs

    def body(*refs):
        own, riding = _split_refs(refs, 7, 4, 4, rider)
        (q_ref, kb_ref, vb_ref, g_ref, o_ref, after_ref, dy_ref, dq_ref, dk_ref, dv_ref, dg_ref,
         dk_acc, dv_acc, dq_acc, carry_ref) = own
        grp = pl.program_id(0)
        i = pl.program_id(1)
        _ride(rider, "start", jnp.logical_and(grp == 0, i == 0), riding)

        @pl.when(i == 0)
        def _():
            dk_acc[...] = jnp.zeros_like(dk_acc)
            dv_acc[...] = jnp.zeros_like(dv_acc)

        lane = lax.broadcasted_iota(jnp.int32, (t, LANE), 1)
        first = lane < HEAD_DIM
        gate = g_ref[...].astype(F32)
        silu, dsilu = _silu_and_grad(gate)
        dyv = dy_ref[...]
        do = dyv * silu
        dg_ref[...] = (dyv * o_ref[...] * dsilu).astype(BF16)
        qv = q_ref[...].astype(F32) * (scale * LOG2_E)
        do_heads, q_heads = [], []
        for p in range(pairs):
            cols = slice(p * LANE, (p + 1) * LANE)
            do_heads += [jnp.where(first, do[:, cols], 0.0).astype(BF16), jnp.where(first, 0.0, do[:, cols]).astype(BF16)]
            q_heads += [jnp.where(first, qv[:, cols], 0.0).astype(BF16), jnp.where(first, 0.0, qv[:, cols]).astype(BF16)]
        later_mat = _triangle(tk, lambda r, cidx: r > cidx)
        before_mat = _triangle(tk, lambda r, cidx: r < cidx)
        dq_acc[...] = jnp.zeros_like(dq_acc)
        carry_ref[...] = jnp.zeros_like(carry_ref)

        def block(kb, valid, lo=0):
            rows = pl.ds(pl.multiple_of(kb * tk, tk), tk)
            k_blk = kb_ref[rows, :]
            v_blk = vb_ref[rows, :]
            carries = [carry_ref[h, lo:, :] for h in range(nh)]
            dq_old = [dq_acc[h, lo:, :] for h in range(nh)]
            dk_old = dk_acc[rows, :]
            dv_old = dv_acc[rows, :]
            outs = []
            for h in range(nh):
                cols = slice((h // 2) * LANE, (h // 2 + 1) * LANE)
                q_h, do_h = q_heads[h][lo:], do_heads[h][lo:]
                after = jnp.sum(jnp.where(lane[lo:] == kb, after_ref[lo:, h * LANE:(h + 1) * LANE], 0.0), axis=1,
                                keepdims=True)
                _, log_beta, later = _sb_scores(q_h, k_blk[:, cols], valid, later_mat, after)
                beta = jnp.exp2(log_beta)
                a = _masked(valid, jnp.exp2(log_beta + later))
                da = _dot(do_h, v_blk[:, cols], NT)
                gterm = a * da
                before = _dot(gterm.astype(BF16), before_mat, NN) + carries[h]
                dz_b = _masked(valid, gterm * (1.0 - beta) - beta * before).astype(BF16)
                outs.append((dq_old[h] + _dot(dz_b, k_blk[:, cols], NN), _dot(dz_b, q_h, TN),
                             _dot(a.astype(BF16), do_h, TN),
                             carries[h] + jnp.sum(gterm, axis=1, keepdims=True)))
            for h in range(nh):
                dq_acc[h, lo:, :] = outs[h][0]
                carry_ref[h, lo:, :] = outs[h][3]
            dk_new = [outs[2 * p][1] + outs[2 * p + 1][1] for p in range(pairs)]
            dv_new = [outs[2 * p][2] + outs[2 * p + 1][2] for p in range(pairs)]
            dk_acc[rows, :] = dk_old + (dk_new[0] if pairs == 1 else jnp.concatenate(dk_new, axis=1))
            dv_acc[rows, :] = dv_old + (dv_new[0] if pairs == 1 else jnp.concatenate(dv_new, axis=1))

        def step(kb, _):
            block(kb, None)
            return 0

        lax.fori_loop(0, ratio * i, step, 0)
        masks = _diagonal_masks(t, tk)
        for d in range(ratio):
            block(ratio * i + d, masks[d][d * tk:], d * tk)
        for p in range(pairs):
            dq_ref[:, p * LANE:(p + 1) * LANE] = (jnp.where(first, dq_acc[2 * p], dq_acc[2 * p + 1]) * scale).astype(BF16)

        @pl.when(i == nq - 1)
        def _():
            dk_ref[...] = (dk_acc[...] * LN_2).astype(BF16)
            dv_ref[...] = dv_acc[...].astype(BF16)

        _ride(rider, "finish", jnp.logical_and(grp == groups - 1, i == nq - 1), riding)

    blk = lambda base: pl.BlockSpec((t, wide), lambda g, i: (i, base // pairs + g))
    full = lambda base: pl.BlockSpec((s, wide), lambda g, i: (0, base // pairs + g))
    out_blk = pl.BlockSpec((t, wide), lambda g, i: (i, g))
    out_full = pl.BlockSpec((s, wide), lambda g, i: (0, g))
    big = jax.ShapeDtypeStruct((s, WIDTH), BF16)
    extra = _rider_call_args(rider, 7, 4)
    outs = pl.pallas_call(
        body, name=name, grid=(groups, nq),
        in_specs=[blk(COL_SB_Q), full(COL_SB_K), full(COL_SB_V), blk(COL_SB_G), out_blk,
                  pl.BlockSpec((t, nh * LANE), lambda g, i: (i, g)), out_blk] + extra["in_specs"],
        out_specs=[out_blk, out_full, out_full, out_blk] + extra["out_specs"],
        out_shape=[big, big, big, big] + extra["out_shape"],
        input_output_aliases=extra["aliases"],
        scratch_shapes=[pltpu.VMEM((s, wide), F32), pltpu.VMEM((s, wide), F32),
                        pltpu.VMEM((nh, t, LANE), F32), pltpu.VMEM((nh, t, 1), F32)] + extra["scratch"],
        compiler_params=pltpu.CompilerParams(dimension_semantics=("arbitrary", "arbitrary")),
    )(u, u, u, u, o, after, dy, *extra["inputs"])
    return outs[:4], outs[4:]


def _gate_fwd(u, ys, w_branch, name, ts=256):
    s = u.shape[0]

    def body(m0, m1, m2, y0, y1, y2, w_ref, p0, p1, p2, out_ref):
        tot = None
        for n, (m_ref, y_ref, p_ref) in enumerate(((m0, y0, p0), (m1, y1, p1), (m2, y2, p2))):
            proj = _dot(y_ref[...], w_ref[n], NN)
            p_ref[...] = proj.astype(BF16)
            term = _sigmoid(m_ref[...].astype(F32)) * proj
            tot = term if tot is None else tot + term
        out_ref[...] = tot.astype(BF16)

    mspec = lambda n: pl.BlockSpec((ts, D_MODEL), lambda i: (i, COL_MERGE_1024 + n))
    row = pl.BlockSpec((ts, D_MODEL), lambda i: (i, 0))
    yspec = pl.BlockSpec((ts, WIDTH), lambda i: (i, 0))
    big = jax.ShapeDtypeStruct((s, D_MODEL), BF16)
    outs = pl.pallas_call(
        body, name=name, grid=(s // ts,),
        in_specs=[mspec(0), mspec(1), mspec(2), yspec, yspec, yspec,
                  pl.BlockSpec(w_branch.shape, lambda i: (0, 0, 0))],
        out_specs=[row] * 4, out_shape=[big] * 4,
    )(u, u, u, *ys, w_branch)
    return outs[:3], outs[3]


def _gate_bwd(u, projs, dmerged, w_branch, name, ts=256, rider=None):
    s = u.shape[0]
    steps = s // ts

    def body(*refs):
        own, riding = _split_refs(refs, 8, 9, 0, rider)
        m0, m1, m2, p0, p1, p2, dm_ref, w_ref, dp0, dp1, dp2, dl0, dl1, dl2, dy0, dy1, dy2 = own
        _ride(rider, "start", pl.program_id(0) == 0, riding)
        dm = dm_ref[...].astype(F32)
        for n, (m_ref, p_ref, dp_ref, dl_ref, dy_ref) in enumerate(((m0, p0, dp0, dl0, dy0), (m1, p1, dp1, dl1, dy1),
                                                                    (m2, p2, dp2, dl2, dy2))):
            gate = _sigmoid(m_ref[...].astype(F32))
            dp = (dm * gate).astype(BF16)
            dp_ref[...] = dp
            dl_ref[...] = (dm * p_ref[...].astype(F32) * gate * (1.0 - gate)).astype(BF16)
            dy_ref[...] = _dot(dp, w_ref[n], NT)
        _ride(rider, "finish", pl.program_id(0) == steps - 1, riding)

    mspec = lambda n: pl.BlockSpec((ts, D_MODEL), lambda i: (i, COL_MERGE_1024 + n))
    row = pl.BlockSpec((ts, D_MODEL), lambda i: (i, 0))
    yspec = pl.BlockSpec((ts, WIDTH), lambda i: (i, 0))
    big = jax.ShapeDtypeStruct((s, D_MODEL), BF16)
    extra = _rider_call_args(rider, 8, 9)
    outs = pl.pallas_call(
        body, name=name, grid=(steps,),
        in_specs=[mspec(0), mspec(1), mspec(2), row, row, row, row,
                  pl.BlockSpec(w_branch.shape, lambda i: (0, 0, 0))] + extra["in_specs"],
        out_specs=[row] * 6 + [yspec] * 3 + extra["out_specs"],
        out_shape=[big] * 6 + [jax.ShapeDtypeStruct((s, WIDTH), F32)] * 3 + extra["out_shape"],
        input_output_aliases=extra["aliases"], scratch_shapes=extra["scratch"],
        compiler_params=pltpu.CompilerParams(dimension_semantics=("arbitrary",)),
    )(u, u, u, *projs, dmerged, w_branch, *extra["inputs"])
    return outs[:3], outs[3:6], outs[6:9], outs[9:]


def _as_rows(a):
    return a.reshape(-1, a.shape[-1])


def _row_tile(rows, cols, bytes_per_row_elem=4, cap=1 << 20):
    tr = rows
    while tr * cols * bytes_per_row_elem > cap and tr % 2 == 0 and (tr // 2) % 16 == 0:
        tr //= 2
    return tr


def _cast_bf16(a, name):
    a2 = _as_rows(a)
    rows, cols = a2.shape
    tr = _row_tile(rows, cols)

    def body(a_ref, o_ref):
        o_ref[...] = a_ref[...].astype(BF16)

    spec = pl.BlockSpec((tr, cols), lambda i: (i, 0))
    out = pl.pallas_call(body, name=name, grid=(rows // tr,), in_specs=[spec], out_specs=spec,
                         out_shape=jax.ShapeDtypeStruct((rows, cols), BF16))(a2)
    return out.reshape(a.shape)


def _adamw(w, g, m, v, name):
    shape = w.shape
    w2, g2, m2, v2 = (_as_rows(a) for a in (w, g, m, v))
    rows, cols = w2.shape
    tr = _row_tile(rows, cols)
    c1 = 1.0 - ADAM_B1 ** ADAM_STEP
    c2 = 1.0 - ADAM_B2 ** ADAM_STEP

    def body(w_ref, g_ref, m_ref, v_ref, go_ref, d_ref, nm_ref, nv_ref):
        gv = g_ref[...]
        go_ref[...] = gv
        nm = ADAM_B1 * m_ref[...] + (1.0 - ADAM_B1) * gv
        nv = ADAM_B2 * v_ref[...] + (1.0 - ADAM_B2) * (gv * gv)
        nm_ref[...] = nm
        nv_ref[...] = nv
        d_ref[...] = -ADAM_LR * ((nm / c1) / (jnp.sqrt(nv / c2) + ADAM_EPS) + ADAM_WD * w_ref[...])

    spec = pl.BlockSpec((tr, cols), lambda i: (i, 0))
    sds = jax.ShapeDtypeStruct((rows, cols), F32)
    outs = pl.pallas_call(body, name=name, grid=(rows // tr,), in_specs=[spec] * 4, out_specs=[spec] * 4,
                          out_shape=[sds] * 4)(w2, g2, m2, v2)
    return tuple(o.reshape(shape) for o in outs)


def _sum_slots(a, out_dtype, name):
    n = a.shape[0]
    a3 = a.reshape(n, -1, a.shape[-1])
    _, rows, cols = a3.shape
    tr = _row_tile(rows, cols * n)

    def body(a_ref, o_ref):
        tot = a_ref[0].astype(F32)
        for k in range(1, n):
            tot = tot + a_ref[k].astype(F32)
        o_ref[...] = tot.astype(out_dtype)

    out = pl.pallas_call(
        body, name=name, grid=(rows // tr,),
        in_specs=[pl.BlockSpec((n, tr, cols), lambda i: (0, i, 0))],
        out_specs=pl.BlockSpec((tr, cols), lambda i: (i, 0)),
        out_shape=jax.ShapeDtypeStruct((rows, cols), out_dtype))(a3)
    return out.reshape(a.shape[1:])


def _chip_sum(own, recv, axis, core, name):
    half = recv.shape
    nd = len(half)
    last = nd - 1
    if axis == last:
        tl, nt = half[last], 1
    else:
        tl = min(half[last], 2048)
        nt = half[last] // tl
    block = half[:last] + (tl,)

    def own_index(i, core_ref):
        idx = [0] * nd
        idx[last] = i
        if axis == last:
            idx[last] = core_ref[0]
        else:
            idx[axis] = core_ref[0]
        return tuple(idx)

    def recv_index(i, core_ref):
        idx = [0] * nd
        idx[last] = i
        return tuple(idx)

    def body(core_ref, own_ref, recv_ref, o_ref):
        o_ref[...] = (own_ref[...].astype(F32) + recv_ref[...].astype(F32)).astype(BF16)

    return pl.pallas_call(
        body, name=name,
        grid_spec=pltpu.PrefetchScalarGridSpec(
            num_scalar_prefetch=1, grid=(nt,),
            in_specs=[pl.BlockSpec(block, own_index), pl.BlockSpec(block, recv_index)],
            out_specs=pl.BlockSpec(block, recv_index)),
        out_shape=jax.ShapeDtypeStruct(half, BF16),
    )(core, own, recv)


def _mesh_position():
    return lax.axis_index("x"), lax.axis_index("y"), lax.axis_index("c")


def _other_chips(x, y):
    return [(1 - x, y), (x, 1 - y), (1 - x, 1 - y)]


ALL_FLIPS = [(0, 0, 1), (1, 0, 0), (0, 1, 0), (1, 1, 0), (1, 0, 1), (0, 1, 1), (1, 1, 1)]


def _half(ref, axis, which, size):
    idx = [slice(None)] * len(ref.shape)
    idx[axis] = pl.ds(which * size, size)
    return ref.at[tuple(idx)]


def _sub(ref, picks):
    idx = [slice(None)] * len(ref.shape)
    for axis, start, size in picks:
        idx[axis] = pl.ds(start, size)
    return ref.at[tuple(idx)]


def _remote(src, dst, sems_send, sems_recv, k, to):
    return pltpu.make_async_remote_copy(src_ref=src, dst_ref=dst, send_sem=sems_send.at[k], recv_sem=sems_recv.at[k],
                                        device_id=to, device_id_type=MESH)


def _cast_shard(w, layer, shard_axis, pos, name, tr=512):
    shape = w.shape[1:]
    nd = len(shape)
    assert shard_axis in (nd - 1, nd - 2)
    rows, cols = shape[-2:]
    tr = min(tr, rows)
    nt = rows // tr
    lead = shape[:-2]
    full = list(shape)
    full[shard_axis] *= N_CHIPS
    block = (1,) * len(lead) + (tr, cols)

    def in_index(*args):
        return (layer, *args[:-1], 0)

    def out_index(*args):
        *g, pos_ref = args
        if shard_axis == nd - 1:
            return (*g, pos_ref[1])
        return (*g[:-1], pos_ref[1] * nt + g[-1], 0)

    def body(pos_ref, a_ref, o_ref):
        o_ref[...] = a_ref[...].astype(BF16)

    return pl.pallas_call(
        body, name=name,
        grid_spec=pltpu.PrefetchScalarGridSpec(
            num_scalar_prefetch=1, grid=lead + (nt,),
            in_specs=[pl.BlockSpec((None,) + block, in_index)], out_specs=pl.BlockSpec(block, out_index)),
        out_shape=jax.ShapeDtypeStruct(tuple(full), BF16),
    )(pos, w)


class _Rider:
    def __init__(self, inputs, out_shape, aliases, scratch, start, middle, finish):
        self.inputs, self.out_shape, self.aliases, self.scratch = inputs, out_shape, aliases, scratch
        self.start, self.middle, self.finish = start, middle, finish


def _weight_gather_rider(fulls, layout):
    n = len(fulls)

    def copies(outs, sems):
        send_sems, recv_sems = sems
        x, y, c = _mesh_position()
        chips = _other_chips(x, y)
        sibling = (x, y, 1 - c)
        mine = 2 * x + y

        def place(t, chip, core):
            sh_axis, sh_size, half_axis, half_size = layout[t]
            return _sub(outs[t], [(sh_axis, chip * sh_size, sh_size), (half_axis, core * half_size, half_size)])

        direct, arrive, forward, arrive_fwd = [], [], [], []
        for t in range(n):
            for k, (px, py) in enumerate(chips):
                theirs = 2 * px + py
                direct.append(_remote(place(t, mine, c), place(t, mine, c), send_sems, recv_sems, 6 * t + k, (px, py, c)))
                arrive.append(_remote(place(t, theirs, c), place(t, theirs, c), send_sems, recv_sems, 6 * t + k, (px, py, c)))
                forward.append(_remote(place(t, theirs, c), place(t, theirs, c), send_sems, recv_sems, 6 * t + 3 + k, sibling))
                arrive_fwd.append(_remote(place(t, theirs, 1 - c), place(t, theirs, 1 - c), send_sems, recv_sems,
                                          6 * t + 3 + k, sibling))
        return direct, arrive, forward, arrive_fwd

    def start(ins, outs, sems):
        for cp in copies(outs, sems)[0]:
            cp.start()

    def middle(ins, outs, sems):
        _, arrive, forward, _ = copies(outs, sems)
        for a, f in zip(arrive, forward):
            a.wait_recv()
            f.start()

    def finish(ins, outs, sems):
        direct, _, forward, arrive_fwd = copies(outs, sems)
        for cp in arrive_fwd:
            cp.wait_recv()
        for cp in direct + forward:
            cp.wait_send()

    return _Rider(list(fulls), [jax.ShapeDtypeStruct(a.shape, a.dtype) for a in fulls], {k: k for k in range(n)},
                  [pltpu.SemaphoreType.DMA((6 * n,)), pltpu.SemaphoreType.DMA((6 * n,))], start, middle, finish)


WEIGHT_LAYOUT = [(1, N_IN // N_CHIPS, 0, D_MODEL // 2), (2, D_MODEL // N_CHIPS, 1, WIDTH // 2),
                 (0, D_MODEL // N_CHIPS, 1, D_MODEL // 2)]


def _gather_weights(fulls, layout, conv_w):
    rider = _weight_gather_rider(fulls, layout)
    n = len(fulls)

    def body(*refs):
        cw, outs, cw_f = refs[n], refs[n + 1:2 * n + 1], refs[2 * n + 1]
        sems, (cw_send, cw_recv, local_sem) = refs[2 * n + 2:2 * n + 4], refs[2 * n + 4:]
        x, y, c = _mesh_position()
        chips = _other_chips(x, y)
        mine = 2 * x + y
        local = pltpu.make_async_copy(cw, cw_f.at[mine], local_sem.at[0])
        local.start()
        rider.start(None, outs, sems)
        small = [_remote(cw, cw_f.at[mine], cw_send, cw_recv, k, (*chip, c)) for k, chip in enumerate(chips)]
        for cp in small:
            cp.start()
        rider.middle(None, outs, sems)
        rider.finish(None, outs, sems)
        for k, (px, py) in enumerate(chips):
            _remote(cw, cw_f.at[2 * px + py], cw_send, cw_recv, k, (px, py, c)).wait_recv()
        for cp in small:
            cp.wait_send()
        local.wait()

    outs = pl.pallas_call(
        body, name="gather_weights",
        in_specs=[ANY] * (n + 1), out_specs=[ANY] * (n + 1),
        out_shape=rider.out_shape + [jax.ShapeDtypeStruct((N_CHIPS,) + conv_w.shape, F32)],
        input_output_aliases=rider.aliases,
        scratch_shapes=rider.scratch + [pltpu.SemaphoreType.DMA((3,)), pltpu.SemaphoreType.DMA((3,)),
                                        pltpu.SemaphoreType.DMA((1,))],
    )(*fulls, conv_w)
    return outs[:n], outs[n]


def _swap_rider(items):
    n = len(items)
    halves = []
    for a, axis in items:
        shp = list(a.shape)
        shp[axis] //= 2
        halves.append(tuple(shp))

    def copies(ins, outs, sems):
        x, y, c = _mesh_position()
        return [_remote(_half(ins[k], items[k][1], 1 - c, halves[k][items[k][1]]), outs[k], sems[0], sems[1], k,
                        (x, y, 1 - c)) for k in range(n)]

    def start(ins, outs, sems):
        for cp in copies(ins, outs, sems):
            cp.start()

    def finish(ins, outs, sems):
        for cp in copies(ins, outs, sems):
            cp.wait()

    return _Rider([a for a, _ in items], [jax.ShapeDtypeStruct(h, a.dtype) for h, (a, _) in zip(halves, items)], {},
                  [pltpu.SemaphoreType.DMA((n,)), pltpu.SemaphoreType.DMA((n,))], start, None, finish)


def _swap_halves(items, name):
    rider = _swap_rider(items)
    n = len(items)

    def body(*refs):
        parts = (refs[:n], refs[n:2 * n], refs[2 * n:])
        rider.start(*parts)
        rider.finish(*parts)

    return pl.pallas_call(
        body, name=name, in_specs=[ANY] * n, out_specs=[ANY] * n, out_shape=rider.out_shape,
        scratch_shapes=rider.scratch,
    )(*rider.inputs)


def _grad_exchange_rider(items):
    n = len(items)
    slices = []
    for a, axis in items:
        shp = list(a.shape)
        shp[axis] //= N_CHIPS
        slices.append(tuple(shp))

    def copies(ins, outs, sems):
        send_sems, recv_sems = sems
        x, y, c = _mesh_position()
        made = []
        for k in range(n):
            axis = items[k][1]
            for r, (px, py) in enumerate(_other_chips(x, y)):
                made.append(_remote(_half(ins[k], axis, 2 * px + py, slices[k][axis]), outs[k].at[r],
                                    send_sems, recv_sems, 3 * k + r, (px, py, c)))
        return made

    def start(ins, outs, sems):
        for cp in copies(ins, outs, sems):
            cp.start()

    def finish(ins, outs, sems):
        for cp in copies(ins, outs, sems):
            cp.wait()

    return _Rider([a for a, _ in items], [jax.ShapeDtypeStruct((N_CHIPS - 1,) + s, BF16) for s in slices], {},
                  [pltpu.SemaphoreType.DMA((3 * n,)), pltpu.SemaphoreType.DMA((3 * n,))], start, None, finish)


def _small_gather_rider(small):
    def copies(ins, outs, sems):
        x, y, c = _mesh_position()
        me = 4 * x + 2 * y + c
        local = pltpu.make_async_copy(ins[0], outs[0].at[me], sems[2].at[0])
        remote = [_remote(ins[0], outs[0].at[me], sems[0], sems[1], r, (x ^ fx, y ^ fy, c ^ fc))
                  for r, (fx, fy, fc) in enumerate(ALL_FLIPS)]
        return local, remote

    def start(ins, outs, sems):
        local, remote = copies(ins, outs, sems)
        local.start()
        for cp in remote:
            cp.start()

    def finish(ins, outs, sems):
        local, remote = copies(ins, outs, sems)
        for cp in remote:
            cp.wait()
        local.wait()

    n = len(ALL_FLIPS)
    return _Rider([small], [jax.ShapeDtypeStruct((2 * N_CHIPS,) + small.shape, F32)], {},
                  [pltpu.SemaphoreType.DMA((n,)), pltpu.SemaphoreType.DMA((n,)), pltpu.SemaphoreType.DMA((1,))],
                  start, None, finish)


def _sum_chips(recv, own, shard_axis, split_axis, pos, dest, layer, name, tr=128):
    sl = recv.shape[1:]
    nd = len(sl)
    tiled = nd == 2 and sl[0] > tr
    nt = sl[0] // tr if tiled else 1
    block = ((tr,) + sl[1:]) if tiled else sl
    shard = list(sl)
    shard[split_axis] *= 2

    def recv_index(i, pos_ref):
        return (0, i) + (0,) * (nd - 1) if tiled else (0,) * (nd + 1)

    def own_index(i, pos_ref):
        idx = [0] * nd
        idx[shard_axis] = pos_ref[1]
        if tiled:
            idx[0] = pos_ref[1] * nt + i if shard_axis == 0 else i
        return tuple(idx)

    def out_index(i, pos_ref):
        idx = [0] * nd
        idx[split_axis] = pos_ref[0]
        if tiled:
            idx[0] = pos_ref[0] * nt + i if split_axis == 0 else i
        return (layer, *idx)

    def body(pos_ref, recv_ref, own_ref, *rest):
        o_ref = rest[-1]
        tot = own_ref[...].astype(F32)
        for k in range(N_CHIPS - 1):
            tot = tot + recv_ref[k].astype(F32)
        o_ref[0] = tot

    in_specs = [pl.BlockSpec((N_CHIPS - 1,) + block, recv_index), pl.BlockSpec(block, own_index)]
    args = [pos, recv, own]
    aliases = {}
    if dest is not None:
        in_specs.append(ANY)
        args.append(dest)
        aliases = {3: 0}
    return pl.pallas_call(
        body, name=name,
        grid_spec=pltpu.PrefetchScalarGridSpec(
            num_scalar_prefetch=1, grid=(nt,), in_specs=in_specs,
            out_specs=pl.BlockSpec((1,) + block, out_index)),
        out_shape=jax.ShapeDtypeStruct((DEPTH,) + tuple(shard), F32),
        input_output_aliases=aliases,
    )(*args)


def _share_halves(bufs, late_small, name):
    n = len(bufs)
    small = _small_gather_rider(late_small)

    def body(*refs):
        outs, (send_sems, recv_sems) = refs[n + 1:2 * n + 1], refs[2 * n + 2:2 * n + 4]
        small_parts = ([refs[n]], [refs[2 * n + 1]], refs[2 * n + 4:])
        x, y, c = _mesh_position()
        small.start(*small_parts)
        copies = []
        for k, (a, axis) in enumerate(bufs):
            size = a.shape[1 + axis] // 2
            mine = _half(outs[k], 1 + axis, c, size)
            copies.append(_remote(mine, mine, send_sems, recv_sems, k, (x, y, 1 - c)))
        for cp in copies:
            cp.start()
        for cp in copies:
            cp.wait()
        small.finish(*small_parts)

    outs = pl.pallas_call(
        body, name=name, in_specs=[ANY] * (n + 1), out_specs=[ANY] * (n + 1),
        out_shape=[jax.ShapeDtypeStruct(a.shape, F32) for a, _ in bufs] + small.out_shape,
        input_output_aliases={k: k for k in range(n)},
        scratch_shapes=[pltpu.SemaphoreType.DMA((n,)), pltpu.SemaphoreType.DMA((n,))] + small.scratch,
    )(*[a for a, _ in bufs], late_small)
    return outs[:n], outs[n]


def _layer_fwd(x, p, l, rider=None, proj_rider=None):
    tag = f"l{l}_"
    h = _rms_fwd(x, p["pre_g"], tag + "pre_norm")
    u = _matmul(h, p["w_in"], "nn", BF16, tag + "in_proj", rider=proj_rider)
    if proj_rider is not None:
        u, (w_branch, w_out) = u
        p = dict(p, w_branch=w_branch, w_out=w_out)
    y_pool = _pool_fwd(u, p["pool_w"], p["pool_scale"], tag + "pool")
    y_conv = _conv_fwd(u, p["conv_w"], p["conv_b"], tag + "conv")
    (o_sb, y_sb, sb_after), carried = _sb_fwd(u, tag + "stickbreak", rider=rider)
    ys = [y_pool, y_conv, y_sb]
    projs, merged = _gate_fwd(u, ys, p["w_branch"], tag + "merge")
    out = _matmul(merged, p["w_out"], "nn", F32, tag + "out_proj")
    saved = dict(x=x, h=h, u=u, ys=ys, o_sb=o_sb, sb_after=sb_after, projs=projs, merged=merged, out=out)
    return out, saved, carried, p


def _layer_bwd(dy, p, saved, l, merge_rider=None, early=None, before_dw=None, late=None):
    tag = f"l{l}_bwd_"
    u = saved["u"]
    d_out, g_post = _rms_bwd(saved["out"], p["post_g"], dy, None, BF16, tag + "post_norm")
    d_merged = _matmul(d_out, p["w_out"], "nt", BF16, tag + "out_proj_dx")
    g_w_out = _matmul(saved["merged"], d_out, "tn", BF16, tag + "out_proj_dw", tk=4096)
    d_projs, d_logits, d_ys, carried_merge = _gate_bwd(u, saved["projs"], d_merged, p["w_branch"], tag + "merge",
                                                       rider=merge_rider)
    g_w_branch = jnp.stack([_matmul(saved["ys"][n], d_projs[n], "tn", BF16, tag + f"branch_dw{n}", tk=4096)
                            for n in range(3)])
    rider = early(g_w_branch, g_w_out, carried_merge) if early else None
    d_pv, d_pg, g_pool_w, g_pool_scale = _pool_bwd(u, d_ys[0], p["pool_w"], p["pool_scale"], tag + "pool")
    d_cx, d_cgb, d_cgc, d_cg, g_conv_w, g_conv_b = _conv_bwd(u, d_ys[1], p["conv_w"], p["conv_b"], tag + "conv")
    (d_q, d_k, d_v, d_sg), carried_attn = _sb_bwd(u, saved["o_sb"], saved["sb_after"], d_ys[2], tag + "stickbreak",
                                                  rider=rider)
    du = jnp.concatenate([d_pv, d_pg, d_cx, d_cgb, d_cgc, d_cg, d_q, d_k, d_v, d_sg] + list(d_logits), axis=1)
    grads = dict(w_branch=g_w_branch, w_out=g_w_out, post_g=g_post, pool_w=g_pool_w, pool_scale=g_pool_scale,
                 conv_w=g_conv_w, conv_b=g_conv_b)
    rider = before_dw(grads) if before_dw else None
    g_w_in = _matmul(saved["h"], du, "tn", BF16, tag + "in_proj_dw", tk=4096, rider=rider)
    g_w_in, carried_dw = g_w_in if rider else (g_w_in, [])
    rider = late(g_w_in) if late else None
    dh = _matmul(du, p["w_in"], "nt", BF16, tag + "in_proj_dx", tk=4096, rider=rider)
    dh, carried_dx = dh if rider else (dh, [])
    dx, g_pre = _rms_bwd(saved["x"], p["pre_g"], dh, dy, F32, tag + "pre_norm")
    grads.update(w_in=g_w_in, pre_g=g_pre)
    return dx, grads, carried_attn, carried_dw, carried_dx


SMALL_ORDER = ["pre_g", "pool_w", "pool_scale", "conv_w", "conv_b", "post_g"]


def _pack_small(per_layer):
    parts, spans, at = [], {}, 0
    for name in SMALL_ORDER:
        a = jnp.stack([per_layer[l][name] for l in range(DEPTH)]).reshape(-1, LANE)
        parts.append(a)
        spans[name] = (at, a.shape[0])
        at += a.shape[0]
    return jnp.concatenate(parts, axis=0), spans


def kernel(x, pre_norm_g, w_in, pool_w, pool_scale, conv_w, conv_b, w_branch, w_out, post_norm_g, loss_target, m_pre_norm_g, m_w_in, m_pool_w, m_pool_scale, m_conv_w, m_conv_b, m_w_branch, m_w_out, m_post_norm_g, v_pre_norm_g, v_w_in, v_pool_w, v_pool_scale, v_conv_w, v_conv_b, v_w_branch, v_w_out, v_post_norm_g):
    mx, my, mc = _mesh_position()
    chip = 2 * mx + my
    core = mc.astype(jnp.int32).reshape(1)
    pos = jnp.stack([mc, chip]).astype(jnp.int32)

    names = ["w_in", "w_branch", "w_out"]
    given = dict(w_in=w_in, w_branch=w_branch, w_out=w_out)
    in_place = [[_cast_shard(given[n], l, WEIGHT_LAYOUT[i][0], pos, f"cast_{n}{l}") for i, n in enumerate(names)]
                for l in range(DEPTH)]
    (w_in_0,), conv_w_by_chip = _gather_weights(in_place[0][:1], WEIGHT_LAYOUT[:1], conv_w)
    gathered = [w_in_0] + in_place[0][1:]
    conv_w_f = conv_w_by_chip.transpose(1, 2, 0, 3).reshape(DEPTH, 3, WIDTH)
    pool_w_b = _cast_bf16(pool_w, "cast_pool_w")

    def layer_params(l, big):
        return dict(pre_g=pre_norm_g[l:l + 1], post_g=post_norm_g[l:l + 1], w_in=big[0], w_branch=big[1],
                    w_out=big[2], pool_w=pool_w_b[l], pool_scale=pool_scale[l:l + 1], conv_w=conv_w_f[l],
                    conv_b=conv_b[l:l + 1])

    act = x[0]
    params, saved = [], []
    for l in range(DEPTH):
        rider = _weight_gather_rider(in_place[l + 1], WEIGHT_LAYOUT) if l + 1 < DEPTH else None
        proj_rider = _weight_gather_rider(gathered[1:], WEIGHT_LAYOUT[1:]) if l == 0 else None
        out, sv, gathered, layer_p = _layer_fwd(act, layer_params(l, gathered), l, rider, proj_rider)
        params.append(layer_p)
        saved.append(sv)
        if l < DEPTH - 1:
            act = _resid_out(act, out, params[l]["post_g"], None, f"l{l}_resid")
    dy, loss_part = _resid_out(act, saved[-1]["out"], params[-1]["post_g"], loss_target[0], "loss_head")

    split_axis = dict(w_in=0, w_branch=1, w_out=1)
    shard_axis = dict(w_in=1, w_branch=2, w_out=0)
    grads = [None] * DEPTH
    chip_sums = [dict() for _ in range(DEPTH)]
    by_chip = [dict() for _ in range(DEPTH)]

    def reduce_in_chip(l, which, g):
        items = [(g[n], split_axis[n]) for n in which]
        from_sibling = _swap_halves(items, f"swap_grad_halves{l}_{which[0]}")
        for n, (a, axis), r in zip(which, items, from_sibling):
            chip_sums[l][n] = _chip_sum(a, r, axis, core, f"chip_sum{l}_{n}")

    def exchange_rider(keys):
        return _grad_exchange_rider([(chip_sums[l][n], shard_axis[n]) for l, n in keys])

    waiting = []
    for l in reversed(range(DEPTH)):
        sent_early, sent_late = list(waiting) + [(l, "w_branch"), (l, "w_out")], [(l, "w_in")]
        waiting_items = [(grads[ll][n], split_axis[n]) for ll, n in waiting]

        def early(g_w_branch, g_w_out, from_sibling, l=l, keys=sent_early, above=tuple(waiting), items=waiting_items):
            for (ll, n), (a, axis), r in zip(above, items, from_sibling):
                chip_sums[ll][n] = _chip_sum(a, r, axis, core, f"chip_sum{ll}_{n}")
            reduce_in_chip(l, ["w_branch", "w_out"], dict(w_branch=g_w_branch, w_out=g_w_out))
            return exchange_rider(keys)

        def late(g_w_in, l=l, keys=sent_late):
            reduce_in_chip(l, ["w_in"], dict(w_in=g_w_in))
            return exchange_rider(keys)

        def before_dw(partial, l=l):
            layers = [dict(partial, pre_g=jnp.zeros_like(pre_norm_g[:1])) if ll == l else grads[ll]
                      for ll in range(DEPTH)]
            return _small_gather_rider(_pack_small(layers)[0])

        if l == DEPTH - 1:
            dy, grads[l], _, _, _ = _layer_bwd(dy, params[l], saved[l], l)
            waiting = [(l, n) for n in names]
        else:
            dy, grads[l], got_early, got_dw, got_late = _layer_bwd(
                dy, params[l], saved[l], l, _swap_rider(waiting_items), early, before_dw if l == 0 else None, late)
            for (ll, n), r in zip(sent_early + sent_late, list(got_early) + list(got_late)):
                by_chip[ll][n] = r
            if l == 0:
                small_all = got_dw[0]
            waiting = []
    assert not waiting
    grad_x = dy[None]
    _, spans = _pack_small(grads)
    late_small = jnp.concatenate([grads[0]["pre_g"].reshape(-1, LANE), jnp.broadcast_to(loss_part, (8, LANE))])
    bufs = []
    for n in names:
        dest = None
        for l in range(DEPTH):
            dest = _sum_chips(by_chip[l][n], chip_sums[l][n], shard_axis[n], split_axis[n], pos, dest, l,
                              f"sum_chips{l}_{n}")
        bufs.append((dest, split_axis[n]))
    (g_w_in, g_w_branch, g_w_out), late_all = _share_halves(bufs, late_small, "share_grad_halves")

    late_sum = _sum_slots(late_all, F32, "sum_late_small")
    n_gain = late_small.shape[0] - 8
    loss = late_sum[n_gain, 0]
    small_sum = _sum_slots(small_all, F32, "sum_small")
    at, _ = spans["pre_g"]
    small_sum = jnp.concatenate([small_sum[:at], late_sum[:n_gain], small_sum[at + n_gain:]])
    small = {}
    for name, like in (("pre_g", pre_norm_g), ("pool_w", pool_w), ("pool_scale", pool_scale), ("conv_b", conv_b),
                       ("post_g", post_norm_g)):
        at, n = spans[name]
        small[name] = small_sum[at:at + n].reshape(like.shape)
    at, n = spans["conv_w"]
    g_conv_w_full = small_sum[at:at + n].reshape(DEPTH, 3, WIDTH)
    g_conv_w = lax.dynamic_slice_in_dim(g_conv_w_full, chip * conv_w.shape[2], conv_w.shape[2], axis=2)

    g = dict(pre_norm_g=small["pre_g"], w_in=g_w_in, pool_w=small["pool_w"], pool_scale=small["pool_scale"],
             conv_w=g_conv_w, conv_b=small["conv_b"], w_branch=g_w_branch, w_out=g_w_out, post_norm_g=small["post_g"])
    w = dict(pre_norm_g=pre_norm_g, w_in=w_in, pool_w=pool_w, pool_scale=pool_scale, conv_w=conv_w, conv_b=conv_b,
             w_branch=w_branch, w_out=w_out, post_norm_g=post_norm_g)
    m = dict(pre_norm_g=m_pre_norm_g, w_in=m_w_in, pool_w=m_pool_w, pool_scale=m_pool_scale, conv_w=m_conv_w,
             conv_b=m_conv_b, w_branch=m_w_branch, w_out=m_w_out, post_norm_g=m_post_norm_g)
    v = dict(pre_norm_g=v_pre_norm_g, w_in=v_w_in, pool_w=v_pool_w, pool_scale=v_pool_scale, conv_w=v_conv_w,
             conv_b=v_conv_b, w_branch=v_w_branch, w_out=v_w_out, post_norm_g=v_post_norm_g)
    order = ["pre_norm_g", "w_in", "pool_w", "pool_scale", "conv_w", "conv_b", "w_branch", "w_out", "post_norm_g"]
    upd = {n: _adamw(w[n], g[n], m[n], v[n], "adamw_" + n) for n in order}
    return (loss, grad_x, *[upd[n][0] for n in order], *[upd[n][1] for n in order], *[upd[n][2] for n in order],
            *[upd[n][3] for n in order])
```

```python
import functools

import jax
import jax.numpy as jnp
from jax import lax
from jax.experimental import pallas as pl
from jax.experimental.pallas import tpu as pltpu

F32 = jnp.float32
BF16 = jnp.bfloat16
MESH = pl.DeviceIdType.MESH
ANY = pl.BlockSpec(memory_space=pl.ANY)

DEPTH = 2
D_MODEL = 1024
WIDTH = 512
N_IN = 8192
N_CHIPS = 4
HEAD_DIM = 64
RMS_EPS = 1e-6
POOL_HALO = 16
CONV_HALO = 16
LANE = 128
COL_POOL_V, COL_POOL_G = 0, 4
COL_CONV_X, COL_CONV_GB, COL_CONV_GC, COL_CONV_G = 8, 12, 16, 20
COL_SB_Q, COL_SB_K, COL_SB_V, COL_SB_G = 24, 28, 32, 36
COL_MERGE_1024 = 5

ADAM_LR, ADAM_B1, ADAM_B2, ADAM_EPS, ADAM_WD, ADAM_STEP = 0.001, 0.9, 0.999, 1e-08, 0.01, 10

NN = (((1,), (0,)), ((), ()))
NT = (((1,), (1,)), ((), ()))
TN = (((0,), (0,)), ((), ()))


def _sigmoid(x):
    return 1.0 / (1.0 + jnp.exp(-x))


def _silu_and_grad(x):
    s = _sigmoid(x)
    return x * s, s * (1.0 + x * (1.0 - s))


def _dot(a, b, dims):
    return lax.dot_general(a, b, dims, preferred_element_type=F32)


def _matmul(a, b, mode, out_dtype, name, tm=1024, tn=1024, tk=1024, b_lead=(), rider=None):
    b_shape = b.shape[len(b_lead):]
    if mode == "nn":
        (m, k), (k2, n) = a.shape, b_shape
    elif mode == "nt":
        (m, k), (n, k2) = a.shape, b_shape
    else:
        (k, m), (k2, n) = a.shape, b_shape
    assert k == k2 and a.dtype == BF16 and b.dtype == BF16
    tm, tn, tk = min(tm, m), min(tn, n), min(tk, k)
    assert m % tm == 0 and n % tn == 0 and k % tk == 0
    nk = k // tk
    dims = {"nn": NN, "nt": NT, "tn": TN}[mode]

    grid = (m // tm, n // tn, nk)

    def at_step(step):
        return functools.reduce(jnp.logical_and, [pl.program_id(d) == s for d, s in enumerate(step)])

    def body(*refs):
        (a_ref, b_ref, o_ref, *scratch), riding = _split_refs(refs, 2, 1, 1 if nk > 1 else 0, rider)
        _ride(rider, "start", at_step((0, 0, 0)), riding)
        _ride(rider, "middle", at_step(((3 * grid[0]) // 4, 0, 0)), riding)
        compute(a_ref, b_ref, o_ref, scratch)
        _ride(rider, "finish", at_step([g - 1 for g in grid]), riding)

    def compute(a_ref, b_ref, o_ref, scratch):
        p = _dot(a_ref[...], b_ref[...], dims)
        if nk == 1:
            o_ref[...] = p.astype(o_ref.dtype)
        else:
            acc = scratch[0]
            kk = pl.program_id(2)

            @pl.when(kk == 0)
            def _():
                acc[...] = p

            @pl.when(jnp.logical_and(kk > 0, kk < nk - 1))
            def _():
                acc[...] += p

            @pl.when(kk == nk - 1)
            def _():
                o_ref[...] = (acc[...] + p).astype(o_ref.dtype)

    if mode == "tn":
        a_spec = pl.BlockSpec((tk, tm), lambda i, j, kk: (kk, i))
    else:
        a_spec = pl.BlockSpec((tm, tk), lambda i, j, kk: (i, kk))
    squeezed = (None,) * len(b_lead)
    if mode == "nt":
        b_spec = pl.BlockSpec(squeezed + (tn, tk), lambda i, j, kk: (*b_lead, j, kk))
    else:
        b_spec = pl.BlockSpec(squeezed + (tk, tn), lambda i, j, kk: (*b_lead, kk, j))
    extra = _rider_call_args(rider, 2, 1)
    outs = pl.pallas_call(
        body, name=name, grid=grid,
        in_specs=[a_spec, b_spec] + extra["in_specs"],
        out_specs=[pl.BlockSpec((tm, tn), lambda i, j, kk: (i, j))] + extra["out_specs"],
        out_shape=[jax.ShapeDtypeStruct((m, n), out_dtype)] + extra["out_shape"],
        input_output_aliases=extra["aliases"],
        scratch_shapes=([pltpu.VMEM((tm, tn), F32)] if nk > 1 else []) + extra["scratch"],
        compiler_params=pltpu.CompilerParams(dimension_semantics=("arbitrary",) * 3 if rider else
                                             ("parallel", "parallel", "arbitrary")),
    )(a, b, *extra["inputs"])
    return (outs[0], outs[1:]) if rider else outs[0]


def _rms_fwd(x, g, name, ts=512):
    s, d = x.shape

    def body(x_ref, g_ref, h_ref):
        xv = x_ref[...]
        r = lax.rsqrt(jnp.mean(xv * xv, axis=-1, keepdims=True) + RMS_EPS)
        h_ref[...] = (xv * r * g_ref[...]).astype(BF16)

    return pl.pallas_call(
        body, name=name, grid=(s // ts,),
        in_specs=[pl.BlockSpec((ts, d), lambda i: (i, 0)), pl.BlockSpec((1, d), lambda i: (0, 0))],
        out_specs=pl.BlockSpec((ts, d), lambda i: (i, 0)),
        out_shape=jax.ShapeDtypeStruct((s, d), BF16),
    )(x, g)


def _rms_bwd(xin, g, dh, resid, out_dtype, name, ts=512):
    s, d = xin.shape
    has_resid = resid is not None

    def body(*refs):
        if has_resid:
            x_ref, g_ref, dh_ref, res_ref, dx_ref, dg_ref = refs
        else:
            x_ref, g_ref, dh_ref, dx_ref, dg_ref = refs
        xv = x_ref[...]
        dhv = dh_ref[...].astype(F32)
        r = lax.rsqrt(jnp.mean(xv * xv, axis=-1, keepdims=True) + RMS_EPS)
        nrm = xv * r
        dn = dhv * g_ref[...]
        dx = r * (dn - nrm * jnp.mean(dn * nrm, axis=-1, keepdims=True))
        if has_resid:
            dx = dx + res_ref[...]
        dx_ref[...] = dx.astype(dx_ref.dtype)
        part = jnp.sum(dhv * nrm, axis=0, keepdims=True)

        @pl.when(pl.program_id(0) == 0)
        def _():
            dg_ref[...] = part

        @pl.when(pl.program_id(0) > 0)
        def _():
            dg_ref[...] += part

    row = pl.BlockSpec((ts, d), lambda i: (i, 0))
    vec = pl.BlockSpec((1, d), lambda i: (0, 0))
    ins = [xin, g, dh] + ([resid] if has_resid else [])
    return pl.pallas_call(
        body, name=name, grid=(s // ts,),
        in_specs=[row, vec, row] + ([row] if has_resid else []),
        out_specs=[row, vec],
        out_shape=[jax.ShapeDtypeStruct((s, d), out_dtype), jax.ShapeDtypeStruct((1, d), F32)],
        compiler_params=pltpu.CompilerParams(dimension_semantics=("arbitrary",)),
    )(*ins)


def _resid_out(x, out, g, target, name, ts=512):
    s, d = x.shape
    has_loss = target is not None

    def body(*refs):
        if has_loss:
            x_ref, o_ref, g_ref, t_ref, dy_ref, loss_ref = refs
        else:
            x_ref, o_ref, g_ref, y_ref = refs
        ov = o_ref[...]
        r = lax.rsqrt(jnp.mean(ov * ov, axis=-1, keepdims=True) + RMS_EPS)
        yv = x_ref[...] + ov * r * g_ref[...]
        if not has_loss:
            y_ref[...] = yv
            return
        err = yv - t_ref[...]
        dy_ref[...] = err * (1.0 / d)
        part = jnp.sum(jnp.sum(err * err, axis=-1, keepdims=True), axis=0, keepdims=True) * (0.5 / d)
        part = jnp.broadcast_to(part, (1, LANE))

        @pl.when(pl.program_id(0) == 0)
        def _():
            loss_ref[...] = part

        @pl.when(pl.program_id(0) > 0)
        def _():
            loss_ref[...] += part

    row = pl.BlockSpec((ts, d), lambda i: (i, 0))
    vec = pl.BlockSpec((1, d), lambda i: (0, 0))
    if has_loss:
        return pl.pallas_call(
            body, name=name, grid=(s // ts,),
            in_specs=[row, row, vec, row],
            out_specs=[row, pl.BlockSpec((1, LANE), lambda i: (0, 0))],
            out_shape=[jax.ShapeDtypeStruct((s, d), F32), jax.ShapeDtypeStruct((1, LANE), F32)],
            compiler_params=pltpu.CompilerParams(dimension_semantics=("arbitrary",)),
        )(x, out, g, target)
    return pl.pallas_call(
        body, name=name, grid=(s // ts,),
        in_specs=[row, row, vec], out_specs=row,
        out_shape=jax.ShapeDtypeStruct((s, d), F32),
    )(x, out, g)


def _rows_before(ref, start, n, halo):
    if start == 0:
        return jnp.concatenate([jnp.zeros((halo, ref.shape[1]), F32), ref[0:n, :].astype(F32)], axis=0)
    return ref[start - halo:start + n, :].astype(F32)


def _rows_after(ref, start, n, halo):
    if start + n == ref.shape[0]:
        return jnp.concatenate([ref[start:start + n, :].astype(F32), jnp.zeros((halo, ref.shape[1]), F32)], axis=0)
    return ref[start:start + n + halo, :].astype(F32)


def _pick_window(group, s2, s4, s8, s16):
    return jnp.where(group == 0, s2, jnp.where(group == 1, s4, jnp.where(group == 2, s8, s16)))


def _trailing_sums(ext, group):
    s2 = ext + pltpu.roll(ext, 1, 0)
    s4 = s2 + pltpu.roll(s2, 2, 0)
    s8 = s4 + pltpu.roll(s4, 4, 0)
    s16 = s8 + pltpu.roll(s8, 8, 0)
    return _pick_window(group, s2, s4, s8, s16)


def _leading_sums(ext, group):
    n = ext.shape[0]
    s2 = ext + pltpu.roll(ext, n - 1, 0)
    s4 = s2 + pltpu.roll(s2, n - 2, 0)
    s8 = s4 + pltpu.roll(s4, n - 4, 0)
    s16 = s8 + pltpu.roll(s8, n - 8, 0)
    return _pick_window(group, s2, s4, s8, s16)


def _window_count(start, n, group):
    pos = start + lax.broadcasted_iota(jnp.int32, (n, LANE), 0)
    return jnp.minimum(pos + 1, 2 << group).astype(F32)


def _pooled(v_ref, start, n, group):
    ext = _rows_before(v_ref, start, n, POOL_HALO)
    sums = _trailing_sums(ext, group)[POOL_HALO:, :]
    return sums / _window_count(start, n, group) - ext[POOL_HALO:, :]


def _pool_fwd(u, pool_w, pool_scale, name, ts=512):
    s = u.shape[0]

    def body(v_ref, gate_ref, w_ref, sc_ref, y_ref):
        group = pl.program_id(0)
        for c in range(s // ts):
            a = c * ts
            pooled = _pooled(v_ref, a, ts, group)
            mixed = _dot(pooled.astype(BF16), w_ref[...], NN)
            gate = gate_ref[a:a + ts, :].astype(F32)
            y_ref[a:a + ts, :] = (mixed * sc_ref[...] * (gate * _sigmoid(gate))).astype(BF16)

    col = lambda base: pl.BlockSpec((s, LANE), lambda g: (0, base + g))
    return pl.pallas_call(
        body, name=name, grid=(4,),
        in_specs=[col(COL_POOL_V), col(COL_POOL_G),
                  pl.BlockSpec((None, LANE, LANE), lambda g: (g, 0, 0)),
                  pl.BlockSpec((1, LANE), lambda g: (0, g))],
        out_specs=pl.BlockSpec((s, LANE), lambda g: (0, g)),
        out_shape=jax.ShapeDtypeStruct((s, WIDTH), BF16),
    )(u, u, pool_w, pool_scale)


def _pool_bwd(u, dy, pool_w, pool_scale, name, ts=512):
    s = u.shape[0]

    def body(v_ref, gate_ref, dy_ref, w_ref, sc_ref, dv_ref, dgate_ref, dw_ref, dsc_ref):
        group = pl.program_id(0)
        w = w_ref[...]
        scale = sc_ref[...]
        dw = jnp.zeros((LANE, LANE), F32)
        dsc = jnp.zeros((1, LANE), F32)
        for c in range(s // ts):
            a = c * ts
            n_ext = ts + POOL_HALO
            gate_e = _rows_after(gate_ref, a, ts, POOL_HALO)
            dy_e = _rows_after(dy_ref, a, ts, POOL_HALO)
            silu_e, dsilu_e = _silu_and_grad(gate_e)
            dms_e = dy_e * silu_e
            dm_e = (dms_e * scale).astype(BF16)
            dpool_e = _dot(dm_e, w, NT)
            spread = _leading_sums(dpool_e / _window_count(a, n_ext, group), group)
            dv_ref[a:a + ts, :] = (spread[0:ts, :] - dpool_e[0:ts, :]).astype(BF16)
            pooled = _pooled(v_ref, a, ts, group).astype(BF16)
            mixed = _dot(pooled, w, NN)
            dgate_ref[a:a + ts, :] = (dy_e[0:ts, :] * mixed * scale * dsilu_e[0:ts, :]).astype(BF16)
            dsc = dsc + jnp.sum(dms_e[0:ts, :] * mixed, axis=0, keepdims=True)
            dw = dw + _dot(pooled, dm_e[0:ts, :], TN)
        dw_ref[...] = dw
        dsc_ref[...] = dsc

    col = lambda base: pl.BlockSpec((s, LANE), lambda g: (0, base + g))
    out_col = pl.BlockSpec((s, LANE), lambda g: (0, g))
    return pl.pallas_call(
        body, name=name, grid=(4,),
        in_specs=[col(COL_POOL_V), col(COL_POOL_G), out_col,
                  pl.BlockSpec((None, LANE, LANE), lambda g: (g, 0, 0)),
                  pl.BlockSpec((1, LANE), lambda g: (0, g))],
        out_specs=[out_col, out_col,
                   pl.BlockSpec((None, LANE, LANE), lambda g: (g, 0, 0)),
                   pl.BlockSpec((1, LANE), lambda g: (0, g))],
        out_shape=[jax.ShapeDtypeStruct((s, WIDTH), BF16), jax.ShapeDtypeStruct((s, WIDTH), BF16),
                   jax.ShapeDtypeStruct((4, LANE, LANE), F32), jax.ShapeDtypeStruct((1, WIDTH), F32)],
    )(u, u, dy, pool_w, pool_scale)


def _conv_taps(x_ref, gc_ref, start, n):
    z_ext = _rows_before(gc_ref, start, n, CONV_HALO) * _rows_before(x_ref, start, n, CONV_HALO)
    z0 = z_ext[CONV_HALO:, :]
    z1 = pltpu.roll(z_ext, 1, 0)[CONV_HALO:, :]
    z2 = pltpu.roll(z_ext, 2, 0)[CONV_HALO:, :]
    return z0, z1, z2


def _conv_fwd(u, conv_w, conv_b, name, ts=512):
    s = u.shape[0]

    def body(x_ref, gb_ref, gc_ref, g_ref, w_ref, b_ref, y_ref):
        w0, w1, w2 = w_ref[0:1, :], w_ref[1:2, :], w_ref[2:3, :]
        for c in range(s // ts):
            a = c * ts
            z0, z1, z2 = _conv_taps(x_ref, gc_ref, a, ts)
            y = w2 * z0 + w1 * z1 + w0 * z2 + b_ref[...]
            gate = g_ref[a:a + ts, :].astype(F32)
            y_ref[a:a + ts, :] = (gb_ref[a:a + ts, :].astype(F32) * y * (gate * _sigmoid(gate))).astype(BF16)

    col = lambda base: pl.BlockSpec((s, LANE), lambda j: (0, base + j))
    return pl.pallas_call(
        body, name=name, grid=(4,),
        in_specs=[col(COL_CONV_X), col(COL_CONV_GB), col(COL_CONV_GC), col(COL_CONV_G),
                  pl.BlockSpec((3, LANE), lambda j: (0, j)), pl.BlockSpec((1, LANE), lambda j: (0, j))],
        out_specs=pl.BlockSpec((s, LANE), lambda j: (0, j)),
        out_shape=jax.ShapeDtypeStruct((s, WIDTH), BF16),
    )(u, u, u, u, conv_w, conv_b)


def _conv_bwd(u, dy, conv_w, conv_b, name, ts=512):
    s = u.shape[0]

    def body(x_ref, gb_ref, gc_ref, g_ref, dy_ref, w_ref, b_ref,
             dx_ref, dgb_ref, dgc_ref, dg_ref, dw_ref, db_ref):
        w0, w1, w2 = w_ref[0:1, :], w_ref[1:2, :], w_ref[2:3, :]
        acc = [jnp.zeros((1, LANE), F32) for _ in range(4)]
        for c in range(s // ts):
            a = c * ts
            n_ext = ts + CONV_HALO
            gate_e = _rows_after(g_ref, a, ts, CONV_HALO)
            silu_e, dsilu_e = _silu_and_grad(gate_e)
            dy_e = _rows_after(dy_ref, a, ts, CONV_HALO)
            gb_e = _rows_after(gb_ref, a, ts, CONV_HALO)
            dyy_e = dy_e * silu_e * gb_e
            dz = (w2 * dyy_e + w1 * pltpu.roll(dyy_e, n_ext - 1, 0) + w0 * pltpu.roll(dyy_e, n_ext - 2, 0))[0:ts, :]
            z0, z1, z2 = _conv_taps(x_ref, gc_ref, a, ts)
            yb = w2 * z0 + w1 * z1 + w0 * z2 + b_ref[...]
            dyv = dy_e[0:ts, :]
            dyy = dyy_e[0:ts, :]
            dg_ref[a:a + ts, :] = (dyv * gb_e[0:ts, :] * yb * dsilu_e[0:ts, :]).astype(BF16)
            dgb_ref[a:a + ts, :] = (dyv * silu_e[0:ts, :] * yb).astype(BF16)
            dx_ref[a:a + ts, :] = (dz * gc_ref[a:a + ts, :].astype(F32)).astype(BF16)
            dgc_ref[a:a + ts, :] = (dz * x_ref[a:a + ts, :].astype(F32)).astype(BF16)
            for i, term in enumerate((dyy * z2, dyy * z1, dyy * z0, dyy)):
                acc[i] = acc[i] + jnp.sum(term, axis=0, keepdims=True)
        dw_ref[0:1, :] = acc[0]
        dw_ref[1:2, :] = acc[1]
        dw_ref[2:3, :] = acc[2]
        db_ref[...] = acc[3]

    col = lambda base: pl.BlockSpec((s, LANE), lambda j: (0, base + j))
    out_col = pl.BlockSpec((s, LANE), lambda j: (0, j))
    big = jax.ShapeDtypeStruct((s, WIDTH), BF16)
    return pl.pallas_call(
        body, name=name, grid=(4,),
        in_specs=[col(COL_CONV_X), col(COL_CONV_GB), col(COL_CONV_GC), col(COL_CONV_G), out_col,
                  pl.BlockSpec((3, LANE), lambda j: (0, j)), pl.BlockSpec((1, LANE), lambda j: (0, j))],
        out_specs=[out_col, out_col, out_col, out_col,
                   pl.BlockSpec((3, LANE), lambda j: (0, j)), pl.BlockSpec((1, LANE), lambda j: (0, j))],
        out_shape=[big, big, big, big,
                   jax.ShapeDtypeStruct((3, WIDTH), F32), jax.ShapeDtypeStruct((1, WIDTH), F32)],
    )(u, u, u, u, dy, conv_w, conv_b)


LOG2_E = 1.4426950408889634
LN_2 = 0.6931471805599453


def _sb_scores(q_h, k_blk, valid, later_mat, carry):
    z = _dot(q_h, k_blk, NT)
    neg_z = -z
    soft = jnp.log(1.0 + jnp.exp2(jnp.minimum(z, neg_z))) * LOG2_E
    log_keep = jnp.minimum(neg_z, 0.0) - soft
    log_beta = log_keep + z
    if valid is not None:
        log_keep = jnp.where(valid, log_keep, 0.0)
    later = _dot(log_keep.astype(BF16), later_mat, NN) + carry
    return log_keep, log_beta, later


def _masked(valid, x):
    return x if valid is None else jnp.where(valid, x, 0.0)


def _diagonal_masks(tq, tk):
    r = lax.broadcasted_iota(jnp.int32, (tq, tk), 0)
    cidx = lax.broadcasted_iota(jnp.int32, (tq, tk), 1)
    return [cidx + d * tk < r for d in range(tq // tk)]


def _triangle(tk, op):
    r = lax.broadcasted_iota(jnp.int32, (tk, tk), 0)
    cidx = lax.broadcasted_iota(jnp.int32, (tk, tk), 1)
    return op(r, cidx).astype(BF16)


def _split_refs(refs, n_in, n_out, n_scratch, rider):
    r_in = len(rider.inputs) if rider else 0
    r_out = len(rider.out_shape) if rider else 0
    a, b = n_in + r_in, n_in + r_in + n_out + r_out
    own = refs[:n_in] + refs[a:a + n_out] + refs[b:b + n_scratch]
    return own, (refs[n_in:a], refs[a + n_out:b], refs[b + n_scratch:])


def _rider_call_args(rider, n_in, n_out):
    if rider is None:
        return dict(in_specs=[], out_specs=[], out_shape=[], aliases={}, scratch=[], inputs=[])
    return dict(in_specs=[ANY] * len(rider.inputs), out_specs=[ANY] * len(rider.out_shape),
                out_shape=list(rider.out_shape), scratch=list(rider.scratch), inputs=list(rider.inputs),
                aliases={n_in + a: n_out + b for a, b in rider.aliases.items()})


def _ride(rider, phase, when, parts):
    fn = getattr(rider, phase) if rider else None
    if fn is not None:
        pl.when(when)(lambda: fn(*parts))


def _sb_fwd(u, name, t=512, tk=256, pairs=4, rider=None):
    s = u.shape[0]
    assert s // tk <= LANE and 4 % pairs == 0 and t % tk == 0
    scale = HEAD_DIM ** -0.5
    nh = 2 * pairs
    wide = pairs * LANE
    ratio = t // tk
    groups, nq = 4 // pairs, s // t

    def body(*refs):
        own, riding = _split_refs(refs, 4, 3, 4, rider)
        q_ref, k_ref, v_ref, g_ref, o_ref, y_ref, after_ref, kb_ref, vb_ref, acc_ref, carry_ref = own
        grp = pl.program_id(0)
        i = pl.program_id(1)
        _ride(rider, "start", jnp.logical_and(grp == 0, i == 0), riding)
        _ride(rider, "middle", jnp.logical_and(grp == groups - 1, i == (3 * nq) // 4), riding)

        @pl.when(i == 0)
        def _():
            kb_ref[...] = k_ref[...].astype(BF16)
            vb_ref[...] = v_ref[...].astype(BF16)

        lane = lax.broadcasted_iota(jnp.int32, (t, LANE), 1)
        first = lane < HEAD_DIM
        after_ref[...] = jnp.zeros_like(after_ref)
        qv = q_ref[...].astype(F32) * (scale * LOG2_E)
        q_heads = []
        for p in range(pairs):
            qp = qv[:, p * LANE:(p + 1) * LANE]
            q_heads += [jnp.where(first, qp, 0.0).astype(BF16), jnp.where(first, 0.0, qp).astype(BF16)]
        later_mat = _triangle(tk, lambda r, cidx: r > cidx)
        acc_ref[...] = jnp.zeros_like(acc_ref)
        carry_ref[...] = jnp.zeros_like(carry_ref)

        def block(kb, valid, lo=0):
            rows = pl.ds(pl.multiple_of(kb * tk, tk), tk)
            k_blk = kb_ref[rows, :]
            v_blk = vb_ref[rows, :]
            carries = [carry_ref[h, lo:, :] for h in range(nh)]
            afters = [after_ref[lo:, h * LANE:(h + 1) * LANE] for h in range(nh)]
            accs = [acc_ref[h, lo:, :] for h in range(nh)]
            outs = []
            for h in range(nh):
                cols = slice((h // 2) * LANE, (h // 2 + 1) * LANE)
                log_keep, log_beta, later = _sb_scores(q_heads[h][lo:], k_blk[:, cols], valid, later_mat, carries[h])
                a = _masked(valid, jnp.exp2(log_beta + later))
                outs.append((accs[h] + _dot(a.astype(BF16), v_blk[:, cols], NN),
                             carries[h] + jnp.sum(log_keep, axis=1, keepdims=True),
                             jnp.where(lane[lo:] == kb, carries[h], afters[h])))
            for h in range(nh):
                acc_ref[h, lo:, :] = outs[h][0]
                carry_ref[h, lo:, :] = outs[h][1]
                after_ref[lo:, h * LANE:(h + 1) * LANE] = outs[h][2]

        def step(j, _):
            block(ratio * i - 1 - j, None)
            return 0

        masks = _diagonal_masks(t, tk)
        for d in reversed(range(ratio)):
            block(ratio * i + d, masks[d][d * tk:], d * tk)
        lax.fori_loop(0, ratio * i, step, 0)
        for p in range(pairs):
            cols = slice(p * LANE, (p + 1) * LANE)
            o = jnp.where(first, acc_ref[2 * p], acc_ref[2 * p + 1])
            o_ref[:, cols] = o
            gate = g_ref[:, cols].astype(F32)
            y_ref[:, cols] = (o * gate * _sigmoid(gate)).astype(BF16)
        _ride(rider, "finish", jnp.logical_and(grp == groups - 1, i == nq - 1), riding)

    blk = lambda base: pl.BlockSpec((t, wide), lambda g, i: (i, base // pairs + g))
    full = lambda base: pl.BlockSpec((s, wide), lambda g, i: (0, base // pairs + g))
    out_blk = pl.BlockSpec((t, wide), lambda g, i: (i, g))
    extra = _rider_call_args(rider, 4, 3)
    outs = pl.pallas_call(
        body, name=name, grid=(groups, nq),
        in_specs=[blk(COL_SB_Q), full(COL_SB_K), full(COL_SB_V), blk(COL_SB_G)] + extra["in_specs"],
        out_specs=[out_blk, out_blk, pl.BlockSpec((t, nh * LANE), lambda g, i: (i, g))] + extra["out_specs"],
        out_shape=[jax.ShapeDtypeStruct((s, WIDTH), F32), jax.ShapeDtypeStruct((s, WIDTH), BF16),
                   jax.ShapeDtypeStruct((s, 8 * LANE), F32)] + extra["out_shape"],
        input_output_aliases=extra["aliases"],
        scratch_shapes=[pltpu.VMEM((s, wide), BF16), pltpu.VMEM((s, wide), BF16),
                        pltpu.VMEM((nh, t, LANE), F32), pltpu.VMEM((nh, t, 1), F32)] + extra["scratch"],
        compiler_params=pltpu.CompilerParams(dimension_semantics=("arbitrary", "arbitrary")),
    )(u, u, u, u, *extra["inputs"])
    return outs[:3], outs[3:]


def _sb_bwd(u, o, after, dy, name, t=512, tk=256, pairs=2, rider=None):
    s = u.shape[0]
    nq = s // t
    scale = HEAD_DIM ** -0.5
    nh = 2 * pairs
    wide = pairs * LANE
    ratio = t // tk
    groups = 4 // pairs

    def body(*refs):
        own, riding = _split_refs(refs, 7, 4, 6, rider)
        (q_ref, k_ref, v_ref, g_ref, o_ref, after_ref, dy_ref, dq_ref, dk_ref, dv_ref, dg_ref,
         kb_ref, vb_ref, dk_acc, dv_acc, dq_acc, carry_ref) = own
        grp = pl.program_id(0)
        i = pl.program_id(1)
        _ride(rider, "start", jnp.logical_and(grp == 0, i == 0), riding)

        @pl.when(i == 0)
        def _():
            kb_ref[...] = k_ref[...].astype(BF16)
            vb_ref[...] = v_ref[...].astype(BF16)
            dk_acc[...] = jnp.zeros_like(dk_acc)
            dv_acc[...] = jnp.zeros_like(dv_acc)

        lane = lax.broadcasted_iota(jnp.int32, (t, LANE), 1)
        first = lane < HEAD_DIM
        gate = g_ref[...].astype(F32)
        silu, dsilu = _silu_and_grad(gate)
        dyv = dy_ref[...]
        do = dyv * silu
        dg_ref[...] = (dyv * o_ref[...] * dsilu).astype(BF16)
        qv = q_ref[...].astype(F32) * (scale * LOG2_E)
        do_heads, q_heads = [], []
        for p in range(pairs):
            cols = slice(p * LANE, (p + 1) * LANE)
            do_heads += [jnp.where(first, do[:, cols], 0.0).astype(BF16), jnp.where(first, 0.0, do[:, cols]).astype(BF16)]
            q_heads += [jnp.where(first, qv[:, cols], 0.0).astype(BF16), jnp.where(first, 0.0, qv[:, cols]).astype(BF16)]
        later_mat = _triangle(tk, lambda r, cidx: r > cidx)
        before_mat = _triangle(tk, lambda r, cidx: r < cidx)
        dq_acc[...] = jnp.zeros_like(dq_acc)
        carry_ref[...] = jnp.zeros_like(carry_ref)

        def block(kb, valid, lo=0):
            rows = pl.ds(pl.multiple_of(kb * tk, tk), tk)
            k_blk = kb_ref[rows, :]
            v_blk = vb_ref[rows, :]
            carries = [carry_ref[h, lo:, :] for h in range(nh)]
            dq_old = [dq_acc[h, lo:, :] for h in range(nh)]
            dk_old = dk_acc[rows, :]
            dv_old = dv_acc[rows, :]
            outs = []
            for h in range(nh):
                cols = slice((h // 2) * LANE, (h // 2 + 1) * LANE)
                q_h, do_h = q_heads[h][lo:], do_heads[h][lo:]
                after = jnp.sum(jnp.where(lane[lo:] == kb, after_ref[lo:, h * LANE:(h + 1) * LANE], 0.0), axis=1,
                                keepdims=True)
                _, log_beta, later = _sb_scores(q_h, k_blk[:, cols], valid, later_mat, after)
                beta = jnp.exp2(log_beta)
                a = _masked(valid, jnp.exp2(log_beta + later))
                da = _dot(do_h, v_blk[:, cols], NT)
                gterm = a * da
                before = _dot(gterm.astype(BF16), before_mat, NN) + carries[h]
                dz_b = _masked(valid, gterm * (1.0 - beta) - beta * before).astype(BF16)
                outs.append((dq_old[h] + _dot(dz_b, k_blk[:, cols], NN), _dot(dz_b, q_h, TN),
                             _dot(a.astype(BF16), do_h, TN),
                             carries[h] + jnp.sum(gterm, axis=1, keepdims=True)))
            for h in range(nh):
                dq_acc[h, lo:, :] = outs[h][0]
                carry_ref[h, lo:, :] = outs[h][3]
            dk_new = [outs[2 * p][1] + outs[2 * p + 1][1] for p in range(pairs)]
            dv_new = [outs[2 * p][2] + outs[2 * p + 1][2] for p in range(pairs)]
            dk_acc[rows, :] = dk_old + (dk_new[0] if pairs == 1 else jnp.concatenate(dk_new, axis=1))
            dv_acc[rows, :] = dv_old + (dv_new[0] if pairs == 1 else jnp.concatenate(dv_new, axis=1))

        def step(kb, _):
            block(kb, None)
            return 0

        lax.fori_loop(0, ratio * i, step, 0)
        masks = _diagonal_masks(t, tk)
        for d in range(ratio):
            block(ratio * i + d, masks[d][d * tk:], d * tk)
        for p in range(pairs):
            dq_ref[:, p * LANE:(p + 1) * LANE] = (jnp.where(first, dq_acc[2 * p], dq_acc[2 * p + 1]) * scale).astype(BF16)

        @pl.when(i == nq - 1)
        def _():
            dk_ref[...] = (dk_acc[...] * LN_2).astype(BF16)
            dv_ref[...] = dv_acc[...].astype(BF16)

        _ride(rider, "finish", jnp.logical_and(grp == groups - 1, i == nq - 1), riding)

    blk = lambda base: pl.BlockSpec((t, wide), lambda g, i: (i, base // pairs + g))
    full = lambda base: pl.BlockSpec((s, wide), lambda g, i: (0, base // pairs + g))
    out_blk = pl.BlockSpec((t, wide), lambda g, i: (i, g))
    out_full = pl.BlockSpec((s, wide), lambda g, i: (0, g))
    big = jax.ShapeDtypeStruct((s, WIDTH), BF16)
    extra = _rider_call_args(rider, 7, 4)
    outs = pl.pallas_call(
        body, name=name, grid=(groups, nq),
        in_specs=[blk(COL_SB_Q), full(COL_SB_K), full(COL_SB_V), blk(COL_SB_G), out_blk,
                  pl.BlockSpec((t, nh * LANE), lambda g, i: (i, g)), out_blk] + extra["in_specs"],
        out_specs=[out_blk, out_full, out_full, out_blk] + extra["out_specs"],
        out_shape=[big, big, big, big] + extra["out_shape"],
        input_output_aliases=extra["aliases"],
        scratch_shapes=[pltpu.VMEM((s, wide), BF16), pltpu.VMEM((s, wide), BF16),
                        pltpu.VMEM((s, wide), F32), pltpu.VMEM((s, wide), F32),
                        pltpu.VMEM((nh, t, LANE), F32), pltpu.VMEM((nh, t, 1), F32)] + extra["scratch"],
        compiler_params=pltpu.CompilerParams(dimension_semantics=("arbitrary", "arbitrary")),
    )(u, u, u, u, o, after, dy, *extra["inputs"])
    return outs[:4], outs[4:]


def _gate_fwd(u, ys, w_branch, name, ts=256):
    s = u.shape[0]

    def body(m0, m1, m2, y0, y1, y2, w_ref, p0, p1, p2, out_ref):
        tot = None
        for n, (m_ref, y_ref, p_ref) in enumerate(((m0, y0, p0), (m1, y1, p1), (m2, y2, p2))):
            proj = _dot(y_ref[...], w_ref[n], NN)
            p_ref[...] = proj.astype(BF16)
            term = _sigmoid(m_ref[...].astype(F32)) * proj
            tot = term if tot is None else tot + term
        out_ref[...] = tot.astype(BF16)

    mspec = lambda n: pl.BlockSpec((ts, D_MODEL), lambda i: (i, COL_MERGE_1024 + n))
    row = pl.BlockSpec((ts, D_MODEL), lambda i: (i, 0))
    yspec = pl.BlockSpec((ts, WIDTH), lambda i: (i, 0))
    big = jax.ShapeDtypeStruct((s, D_MODEL), BF16)
    outs = pl.pallas_call(
        body, name=name, grid=(s // ts,),
        in_specs=[mspec(0), mspec(1), mspec(2), yspec, yspec, yspec,
                  pl.BlockSpec(w_branch.shape, lambda i: (0, 0, 0))],
        out_specs=[row] * 4, out_shape=[big] * 4,
    )(u, u, u, *ys, w_branch)
    return outs[:3], outs[3]


def _gate_bwd(u, projs, dmerged, w_branch, name, ts=256, rider=None):
    s = u.shape[0]
    steps = s // ts

    def body(*refs):
        own, riding = _split_refs(refs, 8, 9, 0, rider)
        m0, m1, m2, p0, p1, p2, dm_ref, w_ref, dp0, dp1, dp2, dl0, dl1, dl2, dy0, dy1, dy2 = own
        _ride(rider, "start", pl.program_id(0) == 0, riding)
        dm = dm_ref[...].astype(F32)
        for n, (m_ref, p_ref, dp_ref, dl_ref, dy_ref) in enumerate(((m0, p0, dp0, dl0, dy0), (m1, p1, dp1, dl1, dy1),
                                                                    (m2, p2, dp2, dl2, dy2))):
            gate = _sigmoid(m_ref[...].astype(F32))
            dp = (dm * gate).astype(BF16)
            dp_ref[...] = dp
            dl_ref[...] = (dm * p_ref[...].astype(F32) * gate * (1.0 - gate)).astype(BF16)
            dy_ref[...] = _dot(dp, w_ref[n], NT)
        _ride(rider, "finish", pl.program_id(0) == steps - 1, riding)

    mspec = lambda n: pl.BlockSpec((ts, D_MODEL), lambda i: (i, COL_MERGE_1024 + n))
    row = pl.BlockSpec((ts, D_MODEL), lambda i: (i, 0))
    yspec = pl.BlockSpec((ts, WIDTH), lambda i: (i, 0))
    big = jax.ShapeDtypeStruct((s, D_MODEL), BF16)
    extra = _rider_call_args(rider, 8, 9)
    outs = pl.pallas_call(
        body, name=name, grid=(steps,),
        in_specs=[mspec(0), mspec(1), mspec(2), row, row, row, row,
                  pl.BlockSpec(w_branch.shape, lambda i: (0, 0, 0))] + extra["in_specs"],
        out_specs=[row] * 6 + [yspec] * 3 + extra["out_specs"],
        out_shape=[big] * 6 + [jax.ShapeDtypeStruct((s, WIDTH), F32)] * 3 + extra["out_shape"],
        input_output_aliases=extra["aliases"], scratch_shapes=extra["scratch"],
        compiler_params=pltpu.CompilerParams(dimension_semantics=("arbitrary",)),
    )(u, u, u, *projs, dmerged, w_branch, *extra["inputs"])
    return outs[:3], outs[3:6], outs[6:9], outs[9:]


def _as_rows(a):
    return a.reshape(-1, a.shape[-1])


def _row_tile(rows, cols, bytes_per_row_elem=4, cap=1 << 20):
    tr = rows
    while tr * cols * bytes_per_row_elem > cap and tr % 2 == 0 and (tr // 2) % 16 == 0:
        tr //= 2
    return tr


def _cast_bf16(a, name):
    a2 = _as_rows(a)
    rows, cols = a2.shape
    tr = _row_tile(rows, cols)

    def body(a_ref, o_ref):
        o_ref[...] = a_ref[...].astype(BF16)

    spec = pl.BlockSpec((tr, cols), lambda i: (i, 0))
    out = pl.pallas_call(body, name=name, grid=(rows // tr,), in_specs=[spec], out_specs=spec,
                         out_shape=jax.ShapeDtypeStruct((rows, cols), BF16))(a2)
    return out.reshape(a.shape)


def _adamw(w, g, m, v, name):
    shape = w.shape
    w2, g2, m2, v2 = (_as_rows(a) for a in (w, g, m, v))
    rows, cols = w2.shape
    tr = _row_tile(rows, cols)
    c1 = 1.0 - ADAM_B1 ** ADAM_STEP
    c2 = 1.0 - ADAM_B2 ** ADAM_STEP

    def body(w_ref, g_ref, m_ref, v_ref, go_ref, d_ref, nm_ref, nv_ref):
        gv = g_ref[...]
        go_ref[...] = gv
        nm = ADAM_B1 * m_ref[...] + (1.0 - ADAM_B1) * gv
        nv = ADAM_B2 * v_ref[...] + (1.0 - ADAM_B2) * (gv * gv)
        nm_ref[...] = nm
        nv_ref[...] = nv
        d_ref[...] = -ADAM_LR * ((nm / c1) / (jnp.sqrt(nv / c2) + ADAM_EPS) + ADAM_WD * w_ref[...])

    spec = pl.BlockSpec((tr, cols), lambda i: (i, 0))
    sds = jax.ShapeDtypeStruct((rows, cols), F32)
    outs = pl.pallas_call(body, name=name, grid=(rows // tr,), in_specs=[spec] * 4, out_specs=[spec] * 4,
                          out_shape=[sds] * 4)(w2, g2, m2, v2)
    return tuple(o.reshape(shape) for o in outs)


def _sum_slots(a, out_dtype, name):
    n = a.shape[0]
    a3 = a.reshape(n, -1, a.shape[-1])
    _, rows, cols = a3.shape
    tr = _row_tile(rows, cols * n)

    def body(a_ref, o_ref):
        tot = a_ref[0].astype(F32)
        for k in range(1, n):
            tot = tot + a_ref[k].astype(F32)
        o_ref[...] = tot.astype(out_dtype)

    out = pl.pallas_call(
        body, name=name, grid=(rows // tr,),
        in_specs=[pl.BlockSpec((n, tr, cols), lambda i: (0, i, 0))],
        out_specs=pl.BlockSpec((tr, cols), lambda i: (i, 0)),
        out_shape=jax.ShapeDtypeStruct((rows, cols), out_dtype))(a3)
    return out.reshape(a.shape[1:])


def _chip_sum(own, recv, axis, core, name):
    half = recv.shape
    nd = len(half)
    last = nd - 1
    if axis == last:
        tl, nt = half[last], 1
    else:
        tl = min(half[last], 2048)
        nt = half[last] // tl
    block = half[:last] + (tl,)

    def own_index(i, core_ref):
        idx = [0] * nd
        idx[last] = i
        if axis == last:
            idx[last] = core_ref[0]
        else:
            idx[axis] = core_ref[0]
        return tuple(idx)

    def recv_index(i, core_ref):
        idx = [0] * nd
        idx[last] = i
        return tuple(idx)

    def body(core_ref, own_ref, recv_ref, o_ref):
        o_ref[...] = (own_ref[...].astype(F32) + recv_ref[...].astype(F32)).astype(BF16)

    return pl.pallas_call(
        body, name=name,
        grid_spec=pltpu.PrefetchScalarGridSpec(
            num_scalar_prefetch=1, grid=(nt,),
            in_specs=[pl.BlockSpec(block, own_index), pl.BlockSpec(block, recv_index)],
            out_specs=pl.BlockSpec(block, recv_index)),
        out_shape=jax.ShapeDtypeStruct(half, BF16),
    )(core, own, recv)


def _mesh_position():
    return lax.axis_index("x"), lax.axis_index("y"), lax.axis_index("c")


def _other_chips(x, y):
    return [(1 - x, y), (x, 1 - y), (1 - x, 1 - y)]


ALL_FLIPS = [(0, 0, 1), (1, 0, 0), (0, 1, 0), (1, 1, 0), (1, 0, 1), (0, 1, 1), (1, 1, 1)]


def _half(ref, axis, which, size):
    idx = [slice(None)] * len(ref.shape)
    idx[axis] = pl.ds(which * size, size)
    return ref.at[tuple(idx)]


def _sub(ref, picks):
    idx = [slice(None)] * len(ref.shape)
    for axis, start, size in picks:
        idx[axis] = pl.ds(start, size)
    return ref.at[tuple(idx)]


def _remote(src, dst, sems_send, sems_recv, k, to):
    return pltpu.make_async_remote_copy(src_ref=src, dst_ref=dst, send_sem=sems_send.at[k], recv_sem=sems_recv.at[k],
                                        device_id=to, device_id_type=MESH)


def _cast_shard(w, layer, shard_axis, pos, name, tr=512):
    shape = w.shape[1:]
    nd = len(shape)
    assert shard_axis in (nd - 1, nd - 2)
    rows, cols = shape[-2:]
    tr = min(tr, rows)
    nt = rows // tr
    lead = shape[:-2]
    full = list(shape)
    full[shard_axis] *= N_CHIPS
    block = (1,) * len(lead) + (tr, cols)

    def in_index(*args):
        return (layer, *args[:-1], 0)

    def out_index(*args):
        *g, pos_ref = args
        if shard_axis == nd - 1:
            return (*g, pos_ref[1])
        return (*g[:-1], pos_ref[1] * nt + g[-1], 0)

    def body(pos_ref, a_ref, o_ref):
        o_ref[...] = a_ref[...].astype(BF16)

    return pl.pallas_call(
        body, name=name,
        grid_spec=pltpu.PrefetchScalarGridSpec(
            num_scalar_prefetch=1, grid=lead + (nt,),
            in_specs=[pl.BlockSpec((None,) + block, in_index)], out_specs=pl.BlockSpec(block, out_index)),
        out_shape=jax.ShapeDtypeStruct(tuple(full), BF16),
    )(pos, w)


class _Rider:
    def __init__(self, inputs, out_shape, aliases, scratch, start, middle, finish):
        self.inputs, self.out_shape, self.aliases, self.scratch = inputs, out_shape, aliases, scratch
        self.start, self.middle, self.finish = start, middle, finish


def _weight_gather_rider(fulls, layout):
    n = len(fulls)

    def copies(outs, sems):
        send_sems, recv_sems = sems
        x, y, c = _mesh_position()
        chips = _other_chips(x, y)
        sibling = (x, y, 1 - c)
        mine = 2 * x + y

        def place(t, chip, core):
            sh_axis, sh_size, half_axis, half_size = layout[t]
            return _sub(outs[t], [(sh_axis, chip * sh_size, sh_size), (half_axis, core * half_size, half_size)])

        direct, arrive, forward, arrive_fwd = [], [], [], []
        for t in range(n):
            for k, (px, py) in enumerate(chips):
                theirs = 2 * px + py
                direct.append(_remote(place(t, mine, c), place(t, mine, c), send_sems, recv_sems, 6 * t + k, (px, py, c)))
                arrive.append(_remote(place(t, theirs, c), place(t, theirs, c), send_sems, recv_sems, 6 * t + k, (px, py, c)))
                forward.append(_remote(place(t, theirs, c), place(t, theirs, c), send_sems, recv_sems, 6 * t + 3 + k, sibling))
                arrive_fwd.append(_remote(place(t, theirs, 1 - c), place(t, theirs, 1 - c), send_sems, recv_sems,
                                          6 * t + 3 + k, sibling))
        return direct, arrive, forward, arrive_fwd

    def start(ins, outs, sems):
        for cp in copies(outs, sems)[0]:
            cp.start()

    def middle(ins, outs, sems):
        _, arrive, forward, _ = copies(outs, sems)
        for a, f in zip(arrive, forward):
            a.wait_recv()
            f.start()

    def finish(ins, outs, sems):
        direct, _, forward, arrive_fwd = copies(outs, sems)
        for cp in arrive_fwd:
            cp.wait_recv()
        for cp in direct + forward:
            cp.wait_send()

    rider = _Rider(list(fulls), [jax.ShapeDtypeStruct(a.shape, a.dtype) for a in fulls], {k: k for k in range(n)},
                   [pltpu.SemaphoreType.DMA((6 * n,)), pltpu.SemaphoreType.DMA((6 * n,))], start, middle, finish)
    rider.copies = copies
    return rider


WEIGHT_LAYOUT = [(1, 2048, 0, 512), (2, 256, 1, 256), (0, 256, 1, 512)]


def _conv_w_rider(conv_w):
    def copies(ins, outs, sems):
        x, y, c = _mesh_position()
        mine = 2 * x + y
        local = pltpu.make_async_copy(ins[0], outs[0].at[mine], sems[2].at[0])
        send = [_remote(ins[0], outs[0].at[mine], sems[0], sems[1], k, (px, py, c))
                for k, (px, py) in enumerate(_other_chips(x, y))]
        arrive = [_remote(ins[0], outs[0].at[2 * px + py], sems[0], sems[1], k, (px, py, c))
                  for k, (px, py) in enumerate(_other_chips(x, y))]
        return local, send, arrive

    def start(ins, outs, sems):
        local, send, _ = copies(ins, outs, sems)
        local.start()
        for cp in send:
            cp.start()

    def finish(ins, outs, sems):
        local, send, arrive = copies(ins, outs, sems)
        for cp in arrive:
            cp.wait_recv()
        for cp in send:
            cp.wait_send()
        local.wait()

    return _Rider([conv_w], [jax.ShapeDtypeStruct((N_CHIPS,) + conv_w.shape, F32)], {},
                  [pltpu.SemaphoreType.DMA((3,)), pltpu.SemaphoreType.DMA((3,)), pltpu.SemaphoreType.DMA((1,))],
                  start, None, finish)


def _join_riders(first, second):
    ni, no, ns = len(first.inputs), len(first.out_shape), len(first.scratch)

    def phase(name):
        fns = [getattr(first, name), getattr(second, name)]
        if fns[0] is None and fns[1] is None:
            return None

        def run(ins, outs, sems):
            if fns[0] is not None:
                fns[0](ins[:ni], outs[:no], sems[:ns])
            if fns[1] is not None:
                fns[1](ins[ni:], outs[no:], sems[ns:])
        return run

    aliases = dict(first.aliases)
    aliases.update({ni + a: no + b for a, b in second.aliases.items()})
    return _Rider(first.inputs + second.inputs, first.out_shape + second.out_shape, aliases,
                  first.scratch + second.scratch, phase("start"), phase("middle"), phase("finish"))


def _in_proj_streamed(h, w_in, order, name, rider=None, tm=1024, tn=1024):
    m, k = h.shape
    n = w_in.shape[1]
    tm = min(tm, m)
    nj, ni = n // tn, m // tm
    second = min(1, ni - 1)
    per_chip = WEIGHT_LAYOUT[0][1] // tn
    gather = _weight_gather_rider([w_in], WEIGHT_LAYOUT[:1])

    def body(order_ref, *refs):
        own, riding = _split_refs(refs, 2, 2, 4, rider)
        h_ref, _, u_ref, w_ref, wbuf, tile_sems, send_sems, recv_sems = own
        j, i = pl.program_id(0), pl.program_id(1)
        first = jnp.logical_and(j == 0, i == 0)
        last = jnp.logical_and(j == nj - 1, i == ni - 1)
        direct, arrive, forward, arrive_fwd = gather.copies([w_ref], (send_sems, recv_sems))

        def tile_copy(jj, slot):
            cols = pl.ds(pl.multiple_of(order_ref[jj] * tn, tn), tn)
            return pltpu.make_async_copy(w_ref.at[:, cols], wbuf.at[slot], tile_sems.at[slot])

        @pl.when(first)
        def _():
            for cp in direct:
                cp.start()
            tile_copy(0, 0).start()

        _ride(rider, "start", first, riding)
        for kk in range(N_CHIPS - 1):
            before = per_chip * (kk + 1) - 1

            @pl.when(jnp.logical_and(j == before, i == 0))
            def _(kk=kk):
                arrive[kk].wait_recv()
                forward[kk].start()

            @pl.when(jnp.logical_and(j == before, i == second))
            def _(kk=kk):
                arrive_fwd[kk].wait_recv()

        @pl.when(i == 0)
        def _():
            tile_copy(j, j % 2).wait()

        @pl.when(jnp.logical_and(i == second, j + 1 < nj))
        def _():
            tile_copy(j + 1, (j + 1) % 2).start()

        _ride(rider, "middle", jnp.logical_and(j == (3 * nj) // 4, i == 0), riding)
        u_ref[...] = _dot(h_ref[...], wbuf[j % 2], NN).astype(BF16)

        @pl.when(last)
        def _():
            for cp in direct + forward:
                cp.wait_send()

        _ride(rider, "finish", last, riding)

    extra = _rider_call_args(rider, 3, 2)
    aliases = {2: 1}
    aliases.update(extra["aliases"])
    outs = pl.pallas_call(
        body, name=name,
        grid_spec=pltpu.PrefetchScalarGridSpec(
            num_scalar_prefetch=1, grid=(nj, ni),
            in_specs=[pl.BlockSpec((tm, k), lambda j, i, order_ref: (i, 0)), ANY] + extra["in_specs"],
            out_specs=[pl.BlockSpec((tm, tn), lambda j, i, order_ref: (i, order_ref[j])), ANY] + extra["out_specs"],
            scratch_shapes=[pltpu.VMEM((2, k, tn), BF16), pltpu.SemaphoreType.DMA((2,))] + gather.scratch
            + extra["scratch"]),
        out_shape=[jax.ShapeDtypeStruct((m, n), BF16), jax.ShapeDtypeStruct(w_in.shape, BF16)] + extra["out_shape"],
        input_output_aliases=aliases,
        compiler_params=pltpu.CompilerParams(dimension_semantics=("arbitrary", "arbitrary")),
    )(order, h, w_in, *extra["inputs"])
    return outs[0], outs[1], outs[2:]


def _gather_weights(fulls, layout, conv_w):
    rider = _weight_gather_rider(fulls, layout)
    n = len(fulls)

    def body(*refs):
        cw, outs, cw_f = refs[n], refs[n + 1:2 * n + 1], refs[2 * n + 1]
        sems, (cw_send, cw_recv, local_sem) = refs[2 * n + 2:2 * n + 4], refs[2 * n + 4:]
        x, y, c = _mesh_position()
        chips = _other_chips(x, y)
        mine = 2 * x + y
        local = pltpu.make_async_copy(cw, cw_f.at[mine], local_sem.at[0])
        local.start()
        rider.start(None, outs, sems)
        small = [_remote(cw, cw_f.at[mine], cw_send, cw_recv, k, (*chip, c)) for k, chip in enumerate(chips)]
        for cp in small:
            cp.start()
        rider.middle(None, outs, sems)
        rider.finish(None, outs, sems)
        for k, (px, py) in enumerate(chips):
            _remote(cw, cw_f.at[2 * px + py], cw_send, cw_recv, k, (px, py, c)).wait_recv()
        for cp in small:
            cp.wait_send()
        local.wait()

    outs = pl.pallas_call(
        body, name="gather_weights",
        in_specs=[ANY] * (n + 1), out_specs=[ANY] * (n + 1),
        out_shape=rider.out_shape + [jax.ShapeDtypeStruct((N_CHIPS,) + conv_w.shape, F32)],
        input_output_aliases=rider.aliases,
        scratch_shapes=rider.scratch + [pltpu.SemaphoreType.DMA((3,)), pltpu.SemaphoreType.DMA((3,)),
                                        pltpu.SemaphoreType.DMA((1,))],
    )(*fulls, conv_w)
    return outs[:n], outs[n]


def _swap_rider(items):
    n = len(items)
    halves = []
    for a, axis in items:
        shp = list(a.shape)
        shp[axis] //= 2
        halves.append(tuple(shp))

    def copies(ins, outs, sems):
        x, y, c = _mesh_position()
        return [_remote(_half(ins[k], items[k][1], 1 - c, halves[k][items[k][1]]), outs[k], sems[0], sems[1], k,
                        (x, y, 1 - c)) for k in range(n)]

    def start(ins, outs, sems):
        for cp in copies(ins, outs, sems):
            cp.start()

    def finish(ins, outs, sems):
        for cp in copies(ins, outs, sems):
            cp.wait()

    return _Rider([a for a, _ in items], [jax.ShapeDtypeStruct(h, a.dtype) for h, (a, _) in zip(halves, items)], {},
                  [pltpu.SemaphoreType.DMA((n,)), pltpu.SemaphoreType.DMA((n,))], start, None, finish)


def _swap_halves(items, name):
    rider = _swap_rider(items)
    n = len(items)

    def body(*refs):
        parts = (refs[:n], refs[n:2 * n], refs[2 * n:])
        rider.start(*parts)
        rider.finish(*parts)

    return pl.pallas_call(
        body, name=name, in_specs=[ANY] * n, out_specs=[ANY] * n, out_shape=rider.out_shape,
        scratch_shapes=rider.scratch,
    )(*rider.inputs)


def _grad_exchange_rider(items):
    n = len(items)
    slices = []
    for a, axis in items:
        shp = list(a.shape)
        shp[axis] //= N_CHIPS
        slices.append(tuple(shp))

    def copies(ins, outs, sems):
        send_sems, recv_sems = sems
        x, y, c = _mesh_position()
        made = []
        for k in range(n):
            axis = items[k][1]
            for r, (px, py) in enumerate(_other_chips(x, y)):
                made.append(_remote(_half(ins[k], axis, 2 * px + py, slices[k][axis]), outs[k].at[r],
                                    send_sems, recv_sems, 3 * k + r, (px, py, c)))
        return made

    def start(ins, outs, sems):
        for cp in copies(ins, outs, sems):
            cp.start()

    def finish(ins, outs, sems):
        for cp in copies(ins, outs, sems):
            cp.wait()

    return _Rider([a for a, _ in items], [jax.ShapeDtypeStruct((N_CHIPS - 1,) + s, BF16) for s in slices], {},
                  [pltpu.SemaphoreType.DMA((3 * n,)), pltpu.SemaphoreType.DMA((3 * n,))], start, None, finish)


def _small_gather_rider(small):
    def copies(ins, outs, sems):
        x, y, c = _mesh_position()
        me = 4 * x + 2 * y + c
        local = pltpu.make_async_copy(ins[0], outs[0].at[me], sems[2].at[0])
        remote = [_remote(ins[0], outs[0].at[me], sems[0], sems[1], r, (x ^ fx, y ^ fy, c ^ fc))
                  for r, (fx, fy, fc) in enumerate(ALL_FLIPS)]
        return local, remote

    def start(ins, outs, sems):
        local, remote = copies(ins, outs, sems)
        local.start()
        for cp in remote:
            cp.start()

    def finish(ins, outs, sems):
        local, remote = copies(ins, outs, sems)
        for cp in remote:
            cp.wait()
        local.wait()

    n = len(ALL_FLIPS)
    return _Rider([small], [jax.ShapeDtypeStruct((2 * N_CHIPS,) + small.shape, F32)], {},
                  [pltpu.SemaphoreType.DMA((n,)), pltpu.SemaphoreType.DMA((n,)), pltpu.SemaphoreType.DMA((1,))],
                  start, None, finish)


def _sum_chips(recv, own, shard_axis, split_axis, pos, dest, layer, name, tr=128):
    sl = recv.shape[1:]
    nd = len(sl)
    tiled = nd == 2 and sl[0] > tr
    nt = sl[0] // tr if tiled else 1
    block = ((tr,) + sl[1:]) if tiled else sl
    shard = list(sl)
    shard[split_axis] *= 2

    def recv_index(i, pos_ref):
        return (0, i) + (0,) * (nd - 1) if tiled else (0,) * (nd + 1)

    def own_index(i, pos_ref):
        idx = [0] * nd
        idx[shard_axis] = pos_ref[1]
        if tiled:
            idx[0] = pos_ref[1] * nt + i if shard_axis == 0 else i
        return tuple(idx)

    def out_index(i, pos_ref):
        idx = [0] * nd
        idx[split_axis] = pos_ref[0]
        if tiled:
            idx[0] = pos_ref[0] * nt + i if split_axis == 0 else i
        return (layer, *idx)

    def body(pos_ref, recv_ref, own_ref, *rest):
        o_ref = rest[-1]
        tot = own_ref[...].astype(F32)
        for k in range(N_CHIPS - 1):
            tot = tot + recv_ref[k].astype(F32)
        o_ref[0] = tot

    in_specs = [pl.BlockSpec((N_CHIPS - 1,) + block, recv_index), pl.BlockSpec(block, own_index)]
    args = [pos, recv, own]
    aliases = {}
    if dest is not None:
        in_specs.append(ANY)
        args.append(dest)
        aliases = {3: 0}
    return pl.pallas_call(
        body, name=name,
        grid_spec=pltpu.PrefetchScalarGridSpec(
            num_scalar_prefetch=1, grid=(nt,), in_specs=in_specs,
            out_specs=pl.BlockSpec((1,) + block, out_index)),
        out_shape=jax.ShapeDtypeStruct((DEPTH,) + tuple(shard), F32),
        input_output_aliases=aliases,
    )(*args)


def _share_halves(bufs, late_small, name):
    n = len(bufs)
    small = _small_gather_rider(late_small)

    def body(*refs):
        outs, (send_sems, recv_sems) = refs[n + 1:2 * n + 1], refs[2 * n + 2:2 * n + 4]
        small_parts = ([refs[n]], [refs[2 * n + 1]], refs[2 * n + 4:])
        x, y, c = _mesh_position()
        small.start(*small_parts)
        copies = []
        for k, (a, axis) in enumerate(bufs):
            size = a.shape[1 + axis] // 2
            mine = _half(outs[k], 1 + axis, c, size)
            copies.append(_remote(mine, mine, send_sems, recv_sems, k, (x, y, 1 - c)))
        for cp in copies:
            cp.start()
        for cp in copies:
            cp.wait()
        small.finish(*small_parts)

    outs = pl.pallas_call(
        body, name=name, in_specs=[ANY] * (n + 1), out_specs=[ANY] * (n + 1),
        out_shape=[jax.ShapeDtypeStruct(a.shape, F32) for a, _ in bufs] + small.out_shape,
        input_output_aliases={k: k for k in range(n)},
        scratch_shapes=[pltpu.SemaphoreType.DMA((n,)), pltpu.SemaphoreType.DMA((n,))] + small.scratch,
    )(*[a for a, _ in bufs], late_small)
    return outs[:n], outs[n]


def _layer_fwd(x, p, l, rider=None, first_layer=None):
    tag = f"l{l}_"
    h = _rms_fwd(x, p["pre_g"], tag + "pre_norm")
    if first_layer is None:
        u = _matmul(h, p["w_in"], "nn", BF16, tag + "in_proj")
    else:
        order, proj_rider = first_layer
        u, w_in, (w_branch, w_out, conv_w_by_chip) = _in_proj_streamed(h, p["w_in"], order, tag + "in_proj", proj_rider)
        conv_w_all = conv_w_by_chip.transpose(1, 2, 0, 3).reshape(DEPTH, 3, WIDTH)
        p = dict(p, w_in=w_in, w_branch=w_branch, w_out=w_out, conv_w=conv_w_all[l], conv_w_all=conv_w_all)
    y_pool = _pool_fwd(u, p["pool_w"], p["pool_scale"], tag + "pool")
    y_conv = _conv_fwd(u, p["conv_w"], p["conv_b"], tag + "conv")
    (o_sb, y_sb, sb_after), carried = _sb_fwd(u, tag + "stickbreak", rider=rider)
    ys = [y_pool, y_conv, y_sb]
    projs, merged = _gate_fwd(u, ys, p["w_branch"], tag + "merge")
    out = _matmul(merged, p["w_out"], "nn", F32, tag + "out_proj")
    saved = dict(x=x, h=h, u=u, ys=ys, o_sb=o_sb, sb_after=sb_after, projs=projs, merged=merged, out=out)
    return out, saved, carried, p


def _layer_bwd(dy, p, saved, l, merge_rider=None, early=None, before_dw=None, late=None):
    tag = f"l{l}_bwd_"
    u = saved["u"]
    d_out, g_post = _rms_bwd(saved["out"], p["post_g"], dy, None, BF16, tag + "post_norm")
    d_merged = _matmul(d_out, p["w_out"], "nt", BF16, tag + "out_proj_dx")
    g_w_out = _matmul(saved["merged"], d_out, "tn", BF16, tag + "out_proj_dw", tk=2048)
    d_projs, d_logits, d_ys, carried_merge = _gate_bwd(u, saved["projs"], d_merged, p["w_branch"], tag + "merge",
                                                       rider=merge_rider)
    g_w_branch = jnp.stack([_matmul(saved["ys"][n], d_projs[n], "tn", BF16, tag + f"branch_dw{n}", tk=2048)
                            for n in range(3)])
    rider = early(g_w_branch, g_w_out, carried_merge) if early else None
    d_pv, d_pg, g_pool_w, g_pool_scale = _pool_bwd(u, d_ys[0], p["pool_w"], p["pool_scale"], tag + "pool")
    d_cx, d_cgb, d_cgc, d_cg, g_conv_w, g_conv_b = _conv_bwd(u, d_ys[1], p["conv_w"], p["conv_b"], tag + "conv")
    (d_q, d_k, d_v, d_sg), carried_attn = _sb_bwd(u, saved["o_sb"], saved["sb_after"], d_ys[2], tag + "stickbreak",
                                                  rider=rider)
    du = jnp.concatenate([d_pv, d_pg, d_cx, d_cgb, d_cgc, d_cg, d_q, d_k, d_v, d_sg] + list(d_logits), axis=1)
    grads = dict(w_branch=g_w_branch, w_out=g_w_out, post_g=g_post, pool_w=g_pool_w, pool_scale=g_pool_scale,
                 conv_w=g_conv_w, conv_b=g_conv_b)
    rider = before_dw(grads) if before_dw else None
    g_w_in = _matmul(saved["h"], du, "tn", BF16, tag + "in_proj_dw", tk=2048, rider=rider)
    g_w_in, carried_dw = g_w_in if rider else (g_w_in, [])
    rider = late(g_w_in) if late else None
    dh = _matmul(du, p["w_in"], "nt", BF16, tag + "in_proj_dx", tk=2048, rider=rider)
    dh, carried_dx = dh if rider else (dh, [])
    dx, g_pre = _rms_bwd(saved["x"], p["pre_g"], dh, dy, F32, tag + "pre_norm")
    grads.update(w_in=g_w_in, pre_g=g_pre)
    return dx, grads, carried_attn, carried_dw, carried_dx


SMALL_ORDER = ["pre_g", "pool_w", "pool_scale", "conv_w", "conv_b", "post_g"]


def _pack_small(per_layer):
    parts, spans, at = [], {}, 0
    for name in SMALL_ORDER:
        a = jnp.stack([per_layer[l][name] for l in range(DEPTH)]).reshape(-1, LANE)
        parts.append(a)
        spans[name] = (at, a.shape[0])
        at += a.shape[0]
    return jnp.concatenate(parts, axis=0), spans


def kernel(x, pre_norm_g, w_in, pool_w, pool_scale, conv_w, conv_b, w_branch, w_out, post_norm_g, loss_target, m_pre_norm_g, m_w_in, m_pool_w, m_pool_scale, m_conv_w, m_conv_b, m_w_branch, m_w_out, m_post_norm_g, v_pre_norm_g, v_w_in, v_pool_w, v_pool_scale, v_conv_w, v_conv_b, v_w_branch, v_w_out, v_post_norm_g):
    mx, my, mc = _mesh_position()
    chip = 2 * mx + my
    core = mc.astype(jnp.int32).reshape(1)
    pos = jnp.stack([mc, chip]).astype(jnp.int32)

    names = ["w_in", "w_branch", "w_out"]
    given = dict(w_in=w_in, w_branch=w_branch, w_out=w_out)
    in_place = [[_cast_shard(given[n], l, WEIGHT_LAYOUT[i][0], pos, f"cast_{n}{l}") for i, n in enumerate(names)]
                for l in range(DEPTH)]
    gathered = in_place[0]
    pool_w_b = _cast_bf16(pool_w, "cast_pool_w")
    per_chip = WEIGHT_LAYOUT[0][1] // 1024
    chips = [chip] + [2 * px + py for px, py in _other_chips(mx, my)]
    order = jnp.stack([per_chip * c_ + t_ for c_ in chips for t_ in range(per_chip)]).astype(jnp.int32)
    conv_w_f = None

    def layer_params(l, big):
        return dict(pre_g=pre_norm_g[l:l + 1], post_g=post_norm_g[l:l + 1], w_in=big[0], w_branch=big[1],
                    w_out=big[2], pool_w=pool_w_b[l], pool_scale=pool_scale[l:l + 1],
                    conv_w=None if conv_w_f is None else conv_w_f[l], conv_b=conv_b[l:l + 1])

    act = x[0]
    params, saved = [], []
    for l in range(DEPTH):
        rider = _weight_gather_rider(in_place[l + 1], WEIGHT_LAYOUT) if l + 1 < DEPTH else None
        first_layer = None
        if l == 0:
            first_layer = (order, _join_riders(_weight_gather_rider(gathered[1:], WEIGHT_LAYOUT[1:]),
                                               _conv_w_rider(conv_w)))
        out, sv, gathered, layer_p = _layer_fwd(act, layer_params(l, gathered), l, rider, first_layer)
        if l == 0:
            conv_w_f = layer_p["conv_w_all"]
        params.append(layer_p)
        saved.append(sv)
        if l < DEPTH - 1:
            act = _resid_out(act, out, params[l]["post_g"], None, f"l{l}_resid")
    dy, loss_part = _resid_out(act, saved[-1]["out"], params[-1]["post_g"], loss_target[0], "loss_head")

    split_axis = dict(w_in=0, w_branch=1, w_out=1)
    shard_axis = dict(w_in=1, w_branch=2, w_out=0)
    grads = [None] * DEPTH
    chip_sums = [dict() for _ in range(DEPTH)]
    by_chip = [dict() for _ in range(DEPTH)]

    def reduce_in_chip(l, which, g):
        items = [(g[n], split_axis[n]) for n in which]
        from_sibling = _swap_halves(items, f"swap_grad_halves{l}_{which[0]}")
        for n, (a, axis), r in zip(which, items, from_sibling):
            chip_sums[l][n] = _chip_sum(a, r, axis, core, f"chip_sum{l}_{n}")

    def exchange_rider(keys):
        return _grad_exchange_rider([(chip_sums[l][n], shard_axis[n]) for l, n in keys])

    waiting = []
    for l in reversed(range(DEPTH)):
        sent_early, sent_late = list(waiting) + [(l, "w_branch"), (l, "w_out")], [(l, "w_in")]
        waiting_items = [(grads[ll][n], split_axis[n]) for ll, n in waiting]

        def early(g_w_branch, g_w_out, from_sibling, l=l, keys=sent_early, above=tuple(waiting), items=waiting_items):
            for (ll, n), (a, axis), r in zip(above, items, from_sibling):
                chip_sums[ll][n] = _chip_sum(a, r, axis, core, f"chip_sum{ll}_{n}")
            reduce_in_chip(l, ["w_branch", "w_out"], dict(w_branch=g_w_branch, w_out=g_w_out))
            return exchange_rider(keys)

        def late(g_w_in, l=l, keys=sent_late):
            reduce_in_chip(l, ["w_in"], dict(w_in=g_w_in))
            return exchange_rider(keys)

        def before_dw(partial, l=l):
            layers = [dict(partial, pre_g=jnp.zeros_like(pre_norm_g[:1])) if ll == l else grads[ll]
                      for ll in range(DEPTH)]
            return _small_gather_rider(_pack_small(layers)[0])

        if l == DEPTH - 1:
            dy, grads[l], _, _, _ = _layer_bwd(dy, params[l], saved[l], l)
            waiting = [(l, n) for n in names]
        else:
            dy, grads[l], got_early, got_dw, got_late = _layer_bwd(
                dy, params[l], saved[l], l, _swap_rider(waiting_items), early, before_dw if l == 0 else None, late)
            for (ll, n), r in zip(sent_early + sent_late, list(got_early) + list(got_late)):
                by_chip[ll][n] = r
            if l == 0:
                small_all = got_dw[0]
            waiting = []
    assert not waiting
    grad_x = dy[None]
    _, spans = _pack_small(grads)
    late_small = jnp.concatenate([grads[0]["pre_g"].reshape(-1, LANE), jnp.broadcast_to(loss_part, (8, LANE))])
    bufs = []
    for n in names:
        dest = None
        for l in range(DEPTH):
            dest = _sum_chips(by_chip[l][n], chip_sums[l][n], shard_axis[n], split_axis[n], pos, dest, l,
                              f"sum_chips{l}_{n}")
        bufs.append((dest, split_axis[n]))
    (g_w_in, g_w_branch, g_w_out), late_all = _share_halves(bufs, late_small, "share_grad_halves")

    late_sum = _sum_slots(late_all, F32, "sum_late_small")
    n_gain = late_small.shape[0] - 8
    loss = late_sum[n_gain, 0]
    small_sum = _sum_slots(small_all, F32, "sum_small")
    at, _ = spans["pre_g"]
    small_sum = jnp.concatenate([small_sum[:at], late_sum[:n_gain], small_sum[at + n_gain:]])
    small = {}
    for name, like in (("pre_g", pre_norm_g), ("pool_w", pool_w), ("pool_scale", pool_scale), ("conv_b", conv_b),
                       ("post_g", post_norm_g)):
        at, n = spans[name]
        small[name] = small_sum[at:at + n].reshape(like.shape)
    at, n = spans["conv_w"]
    g_conv_w_full = small_sum[at:at + n].reshape(DEPTH, 3, WIDTH)
    g_conv_w = lax.dynamic_slice_in_dim(g_conv_w_full, chip * conv_w.shape[2], conv_w.shape[2], axis=2)

    g = dict(pre_norm_g=small["pre_g"], w_in=g_w_in, pool_w=small["pool_w"], pool_scale=small["pool_scale"],
             conv_w=g_conv_w, conv_b=small["conv_b"], w_branch=g_w_branch, w_out=g_w_out, post_norm_g=small["post_g"])
    w = dict(pre_norm_g=pre_norm_g, w_in=w_in, pool_w=pool_w, pool_scale=pool_scale, conv_w=conv_w, conv_b=conv_b,
             w_branch=w_branch, w_out=w_out, post_norm_g=post_norm_g)
    m = dict(pre_norm_g=m_pre_norm_g, w_in=m_w_in, pool_w=m_pool_w, pool_scale=m_pool_scale, conv_w=m_conv_w,
             conv_b=m_conv_b, w_branch=m_w_branch, w_out=m_w_out, post_norm_g=m_post_norm_g)
    v = dict(pre_norm_g=v_pre_norm_g, w_in=v_w_in, pool_w=v_pool_w, pool_scale=v_pool_scale, conv_w=v_conv_w,
             conv_b=v_conv_b, w_branch=v_w_branch, w_out=v_w_out, post_norm_g=v_post_norm_g)
    order = ["pre_norm_g", "w_in", "pool_w", "pool_scale", "conv_w", "conv_b", "w_branch", "w_out", "post_norm_g"]
    upd = {n: _adamw(w[n], g[n], m[n], v[n], "adamw_" + n) for n in order}
    return (loss, grad_x, *[upd[n][0] for n in order], *[upd[n][1] for n in order], *[upd[n][2] for n in order],
            *[upd[n][3] for n in order])
```

```python
import functools

import jax
import jax.numpy as jnp
from jax import lax
from jax.experimental import pallas as pl
from jax.experimental.pallas import tpu as pltpu

F32 = jnp.float32
BF16 = jnp.bfloat16
MESH = pl.DeviceIdType.MESH
ANY = pl.BlockSpec(memory_space=pl.ANY)

DEPTH = 2
D_MODEL = 1024
WIDTH = 512
N_IN = 8192
N_CHIPS = 4
HEAD_DIM = 64
RMS_EPS = 1e-6
POOL_HALO = 16
CONV_HALO = 16
LANE = 128
COL_POOL_V, COL_POOL_G = 0, 4
COL_CONV_X, COL_CONV_GB, COL_CONV_GC, COL_CONV_G = 8, 12, 16, 20
COL_SB_Q, COL_SB_K, COL_SB_V, COL_SB_G = 24, 28, 32, 36
COL_MERGE_1024 = 5

ADAM_LR, ADAM_B1, ADAM_B2, ADAM_EPS, ADAM_WD, ADAM_STEP = 0.001, 0.9, 0.999, 1e-08, 0.01, 10

NN = (((1,), (0,)), ((), ()))
NT = (((1,), (1,)), ((), ()))
TN = (((0,), (0,)), ((), ()))


def _sigmoid(x):
    return 1.0 / (1.0 + jnp.exp(-x))


def _silu_and_grad(x):
    s = _sigmoid(x)
    return x * s, s * (1.0 + x * (1.0 - s))


def _dot(a, b, dims):
    return lax.dot_general(a, b, dims, preferred_element_type=F32)


def _matmul(a, b, mode, out_dtype, name, tm=1024, tn=1024, tk=1024, b_lead=(), rider=None):
    b_shape = b.shape[len(b_lead):]
    if mode == "nn":
        (m, k), (k2, n) = a.shape, b_shape
    elif mode == "nt":
        (m, k), (n, k2) = a.shape, b_shape
    else:
        (k, m), (k2, n) = a.shape, b_shape
    assert k == k2 and a.dtype == BF16 and b.dtype == BF16
    tm, tn, tk = min(tm, m), min(tn, n), min(tk, k)
    assert m % tm == 0 and n % tn == 0 and k % tk == 0
    nk = k // tk
    dims = {"nn": NN, "nt": NT, "tn": TN}[mode]

    grid = (m // tm, n // tn, nk)

    def at_step(step):
        return functools.reduce(jnp.logical_and, [pl.program_id(d) == s for d, s in enumerate(step)])

    def body(*refs):
        (a_ref, b_ref, o_ref, *scratch), riding = _split_refs(refs, 2, 1, 1 if nk > 1 else 0, rider)
        _ride(rider, "start", at_step((0, 0, 0)), riding)
        _ride(rider, "middle", at_step(((3 * grid[0]) // 4, 0, 0)), riding)
        compute(a_ref, b_ref, o_ref, scratch)
        _ride(rider, "finish", at_step([g - 1 for g in grid]), riding)

    def compute(a_ref, b_ref, o_ref, scratch):
        p = _dot(a_ref[...], b_ref[...], dims)
        if nk == 1:
            o_ref[...] = p.astype(o_ref.dtype)
        else:
            acc = scratch[0]
            kk = pl.program_id(2)

            @pl.when(kk == 0)
            def _():
                acc[...] = p

            @pl.when(jnp.logical_and(kk > 0, kk < nk - 1))
            def _():
                acc[...] += p

            @pl.when(kk == nk - 1)
            def _():
                o_ref[...] = (acc[...] + p).astype(o_ref.dtype)

    if mode == "tn":
        a_spec = pl.BlockSpec((tk, tm), lambda i, j, kk: (kk, i))
    else:
        a_spec = pl.BlockSpec((tm, tk), lambda i, j, kk: (i, kk))
    squeezed = (None,) * len(b_lead)
    if mode == "nt":
        b_spec = pl.BlockSpec(squeezed + (tn, tk), lambda i, j, kk: (*b_lead, j, kk))
    else:
        b_spec = pl.BlockSpec(squeezed + (tk, tn), lambda i, j, kk: (*b_lead, kk, j))
    extra = _rider_call_args(rider, 2, 1)
    outs = pl.pallas_call(
        body, name=name, grid=grid,
        in_specs=[a_spec, b_spec] + extra["in_specs"],
        out_specs=[pl.BlockSpec((tm, tn), lambda i, j, kk: (i, j))] + extra["out_specs"],
        out_shape=[jax.ShapeDtypeStruct((m, n), out_dtype)] + extra["out_shape"],
        input_output_aliases=extra["aliases"],
        scratch_shapes=([pltpu.VMEM((tm, tn), F32)] if nk > 1 else []) + extra["scratch"],
        compiler_params=pltpu.CompilerParams(dimension_semantics=("arbitrary",) * 3 if rider else
                                             ("parallel", "parallel", "arbitrary")),
    )(a, b, *extra["inputs"])
    return (outs[0], outs[1:]) if rider else outs[0]


def _rms_fwd(x, g, name, ts=512):
    s, d = x.shape

    def body(x_ref, g_ref, h_ref):
        xv = x_ref[...]
        r = lax.rsqrt(jnp.mean(xv * xv, axis=-1, keepdims=True) + RMS_EPS)
        h_ref[...] = (xv * r * g_ref[...]).astype(BF16)

    return pl.pallas_call(
        body, name=name, grid=(s // ts,),
        in_specs=[pl.BlockSpec((ts, d), lambda i: (i, 0)), pl.BlockSpec((1, d), lambda i: (0, 0))],
        out_specs=pl.BlockSpec((ts, d), lambda i: (i, 0)),
        out_shape=jax.ShapeDtypeStruct((s, d), BF16),
    )(x, g)


def _rms_bwd(xin, g, dh, resid, out_dtype, name, ts=512):
    s, d = xin.shape
    has_resid = resid is not None

    def body(*refs):
        if has_resid:
            x_ref, g_ref, dh_ref, res_ref, dx_ref, dg_ref = refs
        else:
            x_ref, g_ref, dh_ref, dx_ref, dg_ref = refs
        xv = x_ref[...]
        dhv = dh_ref[...].astype(F32)
        r = lax.rsqrt(jnp.mean(xv * xv, axis=-1, keepdims=True) + RMS_EPS)
        nrm = xv * r
        dn = dhv * g_ref[...]
        dx = r * (dn - nrm * jnp.mean(dn * nrm, axis=-1, keepdims=True))
        if has_resid:
            dx = dx + res_ref[...]
        dx_ref[...] = dx.astype(dx_ref.dtype)
        part = jnp.sum(dhv * nrm, axis=0, keepdims=True)

        @pl.when(pl.program_id(0) == 0)
        def _():
            dg_ref[...] = part

        @pl.when(pl.program_id(0) > 0)
        def _():
            dg_ref[...] += part

    row = pl.BlockSpec((ts, d), lambda i: (i, 0))
    vec = pl.BlockSpec((1, d), lambda i: (0, 0))
    ins = [xin, g, dh] + ([resid] if has_resid else [])
    return pl.pallas_call(
        body, name=name, grid=(s // ts,),
        in_specs=[row, vec, row] + ([row] if has_resid else []),
        out_specs=[row, vec],
        out_shape=[jax.ShapeDtypeStruct((s, d), out_dtype), jax.ShapeDtypeStruct((1, d), F32)],
        compiler_params=pltpu.CompilerParams(dimension_semantics=("arbitrary",)),
    )(*ins)


def _resid_out(x, out, g, target, name, ts=512):
    s, d = x.shape
    has_loss = target is not None

    def body(*refs):
        if has_loss:
            x_ref, o_ref, g_ref, t_ref, dy_ref, loss_ref = refs
        else:
            x_ref, o_ref, g_ref, y_ref = refs
        ov = o_ref[...]
        r = lax.rsqrt(jnp.mean(ov * ov, axis=-1, keepdims=True) + RMS_EPS)
        yv = x_ref[...] + ov * r * g_ref[...]
        if not has_loss:
            y_ref[...] = yv
            return
        err = yv - t_ref[...]
        dy_ref[...] = err * (1.0 / d)
        part = jnp.sum(jnp.sum(err * err, axis=-1, keepdims=True), axis=0, keepdims=True) * (0.5 / d)
        part = jnp.broadcast_to(part, (1, LANE))

        @pl.when(pl.program_id(0) == 0)
        def _():
            loss_ref[...] = part

        @pl.when(pl.program_id(0) > 0)
        def _():
            loss_ref[...] += part

    row = pl.BlockSpec((ts, d), lambda i: (i, 0))
    vec = pl.BlockSpec((1, d), lambda i: (0, 0))
    if has_loss:
        return pl.pallas_call(
            body, name=name, grid=(s // ts,),
            in_specs=[row, row, vec, row],
            out_specs=[row, pl.BlockSpec((1, LANE), lambda i: (0, 0))],
            out_shape=[jax.ShapeDtypeStruct((s, d), F32), jax.ShapeDtypeStruct((1, LANE), F32)],
            compiler_params=pltpu.CompilerParams(dimension_semantics=("arbitrary",)),
        )(x, out, g, target)
    return pl.pallas_call(
        body, name=name, grid=(s // ts,),
        in_specs=[row, row, vec], out_specs=row,
        out_shape=jax.ShapeDtypeStruct((s, d), F32),
    )(x, out, g)


def _rows_before(ref, start, n, halo):
    if start == 0:
        return jnp.concatenate([jnp.zeros((halo, ref.shape[1]), F32), ref[0:n, :].astype(F32)], axis=0)
    return ref[start - halo:start + n, :].astype(F32)


def _rows_after(ref, start, n, halo):
    if start + n == ref.shape[0]:
        return jnp.concatenate([ref[start:start + n, :].astype(F32), jnp.zeros((halo, ref.shape[1]), F32)], axis=0)
    return ref[start:start + n + halo, :].astype(F32)


def _pick_window(group, s2, s4, s8, s16):
    return jnp.where(group == 0, s2, jnp.where(group == 1, s4, jnp.where(group == 2, s8, s16)))


def _trailing_sums(ext, group):
    s2 = ext + pltpu.roll(ext, 1, 0)
    s4 = s2 + pltpu.roll(s2, 2, 0)
    s8 = s4 + pltpu.roll(s4, 4, 0)
    s16 = s8 + pltpu.roll(s8, 8, 0)
    return _pick_window(group, s2, s4, s8, s16)


def _leading_sums(ext, group):
    n = ext.shape[0]
    s2 = ext + pltpu.roll(ext, n - 1, 0)
    s4 = s2 + pltpu.roll(s2, n - 2, 0)
    s8 = s4 + pltpu.roll(s4, n - 4, 0)
    s16 = s8 + pltpu.roll(s8, n - 8, 0)
    return _pick_window(group, s2, s4, s8, s16)


def _window_count(start, n, group):
    pos = start + lax.broadcasted_iota(jnp.int32, (n, LANE), 0)
    return jnp.minimum(pos + 1, 2 << group).astype(F32)


def _pooled(v_ref, start, n, group):
    ext = _rows_before(v_ref, start, n, POOL_HALO)
    sums = _trailing_sums(ext, group)[POOL_HALO:, :]
    return sums / _window_count(start, n, group) - ext[POOL_HALO:, :]


def _pool_fwd(u, pool_w, pool_scale, name, ts=512):
    s = u.shape[0]

    def body(v_ref, gate_ref, w_ref, sc_ref, y_ref):
        group = pl.program_id(0)
        for c in range(s // ts):
            a = c * ts
            pooled = _pooled(v_ref, a, ts, group)
            mixed = _dot(pooled.astype(BF16), w_ref[...], NN)
            gate = gate_ref[a:a + ts, :].astype(F32)
            y_ref[a:a + ts, :] = (mixed * sc_ref[...] * (gate * _sigmoid(gate))).astype(BF16)

    col = lambda base: pl.BlockSpec((s, LANE), lambda g: (0, base + g))
    return pl.pallas_call(
        body, name=name, grid=(4,),
        in_specs=[col(COL_POOL_V), col(COL_POOL_G),
                  pl.BlockSpec((None, LANE, LANE), lambda g: (g, 0, 0)),
                  pl.BlockSpec((1, LANE), lambda g: (0, g))],
        out_specs=pl.BlockSpec((s, LANE), lambda g: (0, g)),
        out_shape=jax.ShapeDtypeStruct((s, WIDTH), BF16),
    )(u, u, pool_w, pool_scale)


def _pool_bwd(u, dy, pool_w, pool_scale, name, ts=512):
    s = u.shape[0]

    def body(v_ref, gate_ref, dy_ref, w_ref, sc_ref, dv_ref, dgate_ref, dw_ref, dsc_ref):
        group = pl.program_id(0)
        w = w_ref[...]
        scale = sc_ref[...]
        dw = jnp.zeros((LANE, LANE), F32)
        dsc = jnp.zeros((1, LANE), F32)
        for c in range(s // ts):
            a = c * ts
            n_ext = ts + POOL_HALO
            gate_e = _rows_after(gate_ref, a, ts, POOL_HALO)
            dy_e = _rows_after(dy_ref, a, ts, POOL_HALO)
            silu_e, dsilu_e = _silu_and_grad(gate_e)
            dms_e = dy_e * silu_e
            dm_e = (dms_e * scale).astype(BF16)
            dpool_e = _dot(dm_e, w, NT)
            spread = _leading_sums(dpool_e / _window_count(a, n_ext, group), group)
            dv_ref[a:a + ts, :] = (spread[0:ts, :] - dpool_e[0:ts, :]).astype(BF16)
            pooled = _pooled(v_ref, a, ts, group).astype(BF16)
            mixed = _dot(pooled, w, NN)
            dgate_ref[a:a + ts, :] = (dy_e[0:ts, :] * mixed * scale * dsilu_e[0:ts, :]).astype(BF16)
            dsc = dsc + jnp.sum(dms_e[0:ts, :] * mixed, axis=0, keepdims=True)
            dw = dw + _dot(pooled, dm_e[0:ts, :], TN)
        dw_ref[...] = dw
        dsc_ref[...] = dsc

    col = lambda base: pl.BlockSpec((s, LANE), lambda g: (0, base + g))
    out_col = pl.BlockSpec((s, LANE), lambda g: (0, g))
    return pl.pallas_call(
        body, name=name, grid=(4,),
        in_specs=[col(COL_POOL_V), col(COL_POOL_G), out_col,
                  pl.BlockSpec((None, LANE, LANE), lambda g: (g, 0, 0)),
                  pl.BlockSpec((1, LANE), lambda g: (0, g))],
        out_specs=[out_col, out_col,
                   pl.BlockSpec((None, LANE, LANE), lambda g: (g, 0, 0)),
                   pl.BlockSpec((1, LANE), lambda g: (0, g))],
        out_shape=[jax.ShapeDtypeStruct((s, WIDTH), BF16), jax.ShapeDtypeStruct((s, WIDTH), BF16),
                   jax.ShapeDtypeStruct((4, LANE, LANE), F32), jax.ShapeDtypeStruct((1, WIDTH), F32)],
    )(u, u, dy, pool_w, pool_scale)


def _conv_taps(x_ref, gc_ref, start, n):
    z_ext = _rows_before(gc_ref, start, n, CONV_HALO) * _rows_before(x_ref, start, n, CONV_HALO)
    z0 = z_ext[CONV_HALO:, :]
    z1 = pltpu.roll(z_ext, 1, 0)[CONV_HALO:, :]
    z2 = pltpu.roll(z_ext, 2, 0)[CONV_HALO:, :]
    return z0, z1, z2


def _conv_fwd(u, conv_w, conv_b, name, ts=512):
    s = u.shape[0]

    def body(x_ref, gb_ref, gc_ref, g_ref, w_ref, b_ref, y_ref):
        w0, w1, w2 = w_ref[0:1, :], w_ref[1:2, :], w_ref[2:3, :]
        for c in range(s // ts):
            a = c * ts
            z0, z1, z2 = _conv_taps(x_ref, gc_ref, a, ts)
            y = w2 * z0 + w1 * z1 + w0 * z2 + b_ref[...]
            gate = g_ref[a:a + ts, :].astype(F32)
            y_ref[a:a + ts, :] = (gb_ref[a:a + ts, :].astype(F32) * y * (gate * _sigmoid(gate))).astype(BF16)

    col = lambda base: pl.BlockSpec((s, LANE), lambda j: (0, base + j))
    return pl.pallas_call(
        body, name=name, grid=(4,),
        in_specs=[col(COL_CONV_X), col(COL_CONV_GB), col(COL_CONV_GC), col(COL_CONV_G),
                  pl.BlockSpec((3, LANE), lambda j: (0, j)), pl.BlockSpec((1, LANE), lambda j: (0, j))],
        out_specs=pl.BlockSpec((s, LANE), lambda j: (0, j)),
        out_shape=jax.ShapeDtypeStruct((s, WIDTH), BF16),
    )(u, u, u, u, conv_w, conv_b)


def _conv_bwd(u, dy, conv_w, conv_b, name, ts=512):
    s = u.shape[0]

    def body(x_ref, gb_ref, gc_ref, g_ref, dy_ref, w_ref, b_ref,
             dx_ref, dgb_ref, dgc_ref, dg_ref, dw_ref, db_ref):
        w0, w1, w2 = w_ref[0:1, :], w_ref[1:2, :], w_ref[2:3, :]
        acc = [jnp.zeros((1, LANE), F32) for _ in range(4)]
        for c in range(s // ts):
            a = c * ts
            n_ext = ts + CONV_HALO
            gate_e = _rows_after(g_ref, a, ts, CONV_HALO)
            silu_e, dsilu_e = _silu_and_grad(gate_e)
            dy_e = _rows_after(dy_ref, a, ts, CONV_HALO)
            gb_e = _rows_after(gb_ref, a, ts, CONV_HALO)
            dyy_e = dy_e * silu_e * gb_e
            dz = (w2 * dyy_e + w1 * pltpu.roll(dyy_e, n_ext - 1, 0) + w0 * pltpu.roll(dyy_e, n_ext - 2, 0))[0:ts, :]
            z0, z1, z2 = _conv_taps(x_ref, gc_ref, a, ts)
            yb = w2 * z0 + w1 * z1 + w0 * z2 + b_ref[...]
            dyv = dy_e[0:ts, :]
            dyy = dyy_e[0:ts, :]
            dg_ref[a:a + ts, :] = (dyv * gb_e[0:ts, :] * yb * dsilu_e[0:ts, :]).astype(BF16)
            dgb_ref[a:a + ts, :] = (dyv * silu_e[0:ts, :] * yb).astype(BF16)
            dx_ref[a:a + ts, :] = (dz * gc_ref[a:a + ts, :].astype(F32)).astype(BF16)
            dgc_ref[a:a + ts, :] = (dz * x_ref[a:a + ts, :].astype(F32)).astype(BF16)
            for i, term in enumerate((dyy * z2, dyy * z1, dyy * z0, dyy)):
                acc[i] = acc[i] + jnp.sum(term, axis=0, keepdims=True)
        dw_ref[0:1, :] = acc[0]
        dw_ref[1:2, :] = acc[1]
        dw_ref[2:3, :] = acc[2]
        db_ref[...] = acc[3]

    col = lambda base: pl.BlockSpec((s, LANE), lambda j: (0, base + j))
    out_col = pl.BlockSpec((s, LANE), lambda j: (0, j))
    big = jax.ShapeDtypeStruct((s, WIDTH), BF16)
    return pl.pallas_call(
        body, name=name, grid=(4,),
        in_specs=[col(COL_CONV_X), col(COL_CONV_GB), col(COL_CONV_GC), col(COL_CONV_G), out_col,
                  pl.BlockSpec((3, LANE), lambda j: (0, j)), pl.BlockSpec((1, LANE), lambda j: (0, j))],
        out_specs=[out_col, out_col, out_col, out_col,
                   pl.BlockSpec((3, LANE), lambda j: (0, j)), pl.BlockSpec((1, LANE), lambda j: (0, j))],
        out_shape=[big, big, big, big,
                   jax.ShapeDtypeStruct((3, WIDTH), F32), jax.ShapeDtypeStruct((1, WIDTH), F32)],
    )(u, u, u, u, dy, conv_w, conv_b)


LOG2_E = 1.4426950408889634
LN_2 = 0.6931471805599453


def _sb_scores(q_h, k_blk, valid, later_mat, carry):
    z = _dot(q_h, k_blk, NT)
    neg_z = -z
    soft = jnp.log(1.0 + jnp.exp2(jnp.minimum(z, neg_z))) * LOG2_E
    log_keep = jnp.minimum(neg_z, 0.0) - soft
    log_beta = log_keep + z
    if valid is not None:
        log_keep = jnp.where(valid, log_keep, 0.0)
    later = _dot(log_keep.astype(BF16), later_mat, NN) + carry
    return log_keep, log_beta, later


def _masked(valid, x):
    return x if valid is None else jnp.where(valid, x, 0.0)


def _diagonal_masks(tq, tk):
    r = lax.broadcasted_iota(jnp.int32, (tq, tk), 0)
    cidx = lax.broadcasted_iota(jnp.int32, (tq, tk), 1)
    return [cidx + d * tk < r for d in range(tq // tk)]


def _triangle(tk, op):
    r = lax.broadcasted_iota(jnp.int32, (tk, tk), 0)
    cidx = lax.broadcasted_iota(jnp.int32, (tk, tk), 1)
    return op(r, cidx).astype(BF16)


def _split_refs(refs, n_in, n_out, n_scratch, rider):
    r_in = len(rider.inputs) if rider else 0
    r_out = len(rider.out_shape) if rider else 0
    a, b = n_in + r_in, n_in + r_in + n_out + r_out
    own = refs[:n_in] + refs[a:a + n_out] + refs[b:b + n_scratch]
    return own, (refs[n_in:a], refs[a + n_out:b], refs[b + n_scratch:])


def _rider_call_args(rider, n_in, n_out):
    if rider is None:
        return dict(in_specs=[], out_specs=[], out_shape=[], aliases={}, scratch=[], inputs=[])
    return dict(in_specs=[ANY] * len(rider.inputs), out_specs=[ANY] * len(rider.out_shape),
                out_shape=list(rider.out_shape), scratch=list(rider.scratch), inputs=list(rider.inputs),
                aliases={n_in + a: n_out + b for a, b in rider.aliases.items()})


def _ride(rider, phase, when, parts):
    fn = getattr(rider, phase) if rider else None
    if fn is not None:
        pl.when(when)(lambda: fn(*parts))


def _sb_fwd(u, name, t=512, tk=256, pairs=4, rider=None):
    s = u.shape[0]
    assert s // tk <= LANE and 4 % pairs == 0 and t % tk == 0
    scale = HEAD_DIM ** -0.5
    nh = 2 * pairs
    wide = pairs * LANE
    ratio = t // tk
    groups, nq = 4 // pairs, s // t

    def body(*refs):
        own, riding = _split_refs(refs, 4, 3, 4, rider)
        q_ref, k_ref, v_ref, g_ref, o_ref, y_ref, after_ref, kb_ref, vb_ref, acc_ref, carry_ref = own
        grp = pl.program_id(0)
        i = pl.program_id(1)
        _ride(rider, "start", jnp.logical_and(grp == 0, i == 0), riding)
        _ride(rider, "middle", jnp.logical_and(grp == groups - 1, i == (3 * nq) // 4), riding)

        @pl.when(i == 0)
        def _():
            kb_ref[...] = k_ref[...].astype(BF16)
            vb_ref[...] = v_ref[...].astype(BF16)

        lane = lax.broadcasted_iota(jnp.int32, (t, LANE), 1)
        first = lane < HEAD_DIM
        after_ref[...] = jnp.zeros_like(after_ref)
        qv = q_ref[...].astype(F32) * (scale * LOG2_E)
        q_heads = []
        for p in range(pairs):
            qp = qv[:, p * LANE:(p + 1) * LANE]
            q_heads += [jnp.where(first, qp, 0.0).astype(BF16), jnp.where(first, 0.0, qp).astype(BF16)]
        later_mat = _triangle(tk, lambda r, cidx: r > cidx)
        acc_ref[...] = jnp.zeros_like(acc_ref)
        carry_ref[...] = jnp.zeros_like(carry_ref)

        def block(kb, valid, lo=0):
            rows = pl.ds(pl.multiple_of(kb * tk, tk), tk)
            k_blk = kb_ref[rows, :]
            v_blk = vb_ref[rows, :]
            carries = [carry_ref[h, lo:, :] for h in range(nh)]
            afters = [after_ref[lo:, h * LANE:(h + 1) * LANE] for h in range(nh)]
            accs = [acc_ref[h, lo:, :] for h in range(nh)]
            outs = []
            for h in range(nh):
                cols = slice((h // 2) * LANE, (h // 2 + 1) * LANE)
                log_keep, log_beta, later = _sb_scores(q_heads[h][lo:], k_blk[:, cols], valid, later_mat, carries[h])
                a = _masked(valid, jnp.exp2(log_beta + later))
                outs.append((accs[h] + _dot(a.astype(BF16), v_blk[:, cols], NN),
                             carries[h] + jnp.sum(log_keep, axis=1, keepdims=True),
                             jnp.where(lane[lo:] == kb, carries[h], afters[h])))
            for h in range(nh):
                acc_ref[h, lo:, :] = outs[h][0]
                carry_ref[h, lo:, :] = outs[h][1]
                after_ref[lo:, h * LANE:(h + 1) * LANE] = outs[h][2]

        def step(j, _):
            block(ratio * i - 1 - j, None)
            return 0

        masks = _diagonal_masks(t, tk)
        for d in reversed(range(ratio)):
            block(ratio * i + d, masks[d][d * tk:], d * tk)
        lax.fori_loop(0, ratio * i, step, 0)
        for p in range(pairs):
            cols = slice(p * LANE, (p + 1) * LANE)
            o = jnp.where(first, acc_ref[2 * p], acc_ref[2 * p + 1])
            o_ref[:, cols] = o
            gate = g_ref[:, cols].astype(F32)
            y_ref[:, cols] = (o * gate * _sigmoid(gate)).astype(BF16)
        _ride(rider, "finish", jnp.logical_and(grp == groups - 1, i == nq - 1), riding)

    blk = lambda base: pl.BlockSpec((t, wide), lambda g, i: (i, base // pairs + g))
    full = lambda base: pl.BlockSpec((s, wide), lambda g, i: (0, base // pairs + g))
    out_blk = pl.BlockSpec((t, wide), lambda g, i: (i, g))
    extra = _rider_call_args(rider, 4, 3)
    outs = pl.pallas_call(
        body, name=name, grid=(groups, nq),
        in_specs=[blk(COL_SB_Q), full(COL_SB_K), full(COL_SB_V), blk(COL_SB_G)] + extra["in_specs"],
        out_specs=[out_blk, out_blk, pl.BlockSpec((t, nh * LANE), lambda g, i: (i, g))] + extra["out_specs"],
        out_shape=[jax.ShapeDtypeStruct((s, WIDTH), F32), jax.ShapeDtypeStruct((s, WIDTH), BF16),
                   jax.ShapeDtypeStruct((s, 8 * LANE), F32)] + extra["out_shape"],
        input_output_aliases=extra["aliases"],
        scratch_shapes=[pltpu.VMEM((s, wide), BF16), pltpu.VMEM((s, wide), BF16),
                        pltpu.VMEM((nh, t, LANE), F32), pltpu.VMEM((nh, t, 1), F32)] + extra["scratch"],
        compiler_params=pltpu.CompilerParams(dimension_semantics=("arbitrary", "arbitrary")),
    )(u, u, u, u, *extra["inputs"])
    return outs[:3], outs[3:]


def _sb_bwd(u, o, after, dy, name, t=512, tk=256, pairs=2, rider=None):
    s = u.shape[0]
    nq = s // t
    scale = HEAD_DIM ** -0.5
    nh = 2 * pairs
    wide = pairs * LANE
    ratio = t // tk
    groups = 4 // pairs

    def body(*refs):
        own, riding = _split_refs(refs, 7, 4, 6, rider)
        (q_ref, k_ref, v_ref, g_ref, o_ref, after_ref, dy_ref, dq_ref, dk_ref, dv_ref, dg_ref,
         kb_ref, vb_ref, dk_acc, dv_acc, dq_acc, carry_ref) = own
        grp = pl.program_id(0)
        i = pl.program_id(1)
        _ride(rider, "start", jnp.logical_and(grp == 0, i == 0), riding)

        @pl.when(i == 0)
        def _():
            kb_ref[...] = k_ref[...].astype(BF16)
            vb_ref[...] = v_ref[...].astype(BF16)
            dk_acc[...] = jnp.zeros_like(dk_acc)
            dv_acc[...] = jnp.zeros_like(dv_acc)

        lane = lax.broadcasted_iota(jnp.int32, (t, LANE), 1)
        first = lane < HEAD_DIM
        gate = g_ref[...].astype(F32)
        silu, dsilu = _silu_and_grad(gate)
        dyv = dy_ref[...]
        do = dyv * silu
        dg_ref[...] = (dyv * o_ref[...] * dsilu).astype(BF16)
        qv = q_ref[...].astype(F32) * (scale * LOG2_E)
        do_heads, q_heads = [], []
        for p in range(pairs):
            cols = slice(p * LANE, (p + 1) * LANE)
            do_heads += [jnp.where(first, do[:, cols], 0.0).astype(BF16), jnp.where(first, 0.0, do[:, cols]).astype(BF16)]
            q_heads += [jnp.where(first, qv[:, cols], 0.0).astype(BF16), jnp.where(first, 0.0, qv[:, cols]).astype(BF16)]
        later_mat = _triangle(tk, lambda r, cidx: r > cidx)
        before_mat = _triangle(tk, lambda r, cidx: r < cidx)
        dq_acc[...] = jnp.zeros_like(dq_acc)
        carry_ref[...] = jnp.zeros_like(carry_ref)

        def block(kb, valid, lo=0):
            rows = pl.ds(pl.multiple_of(kb * tk, tk), tk)
            k_blk = kb_ref[rows, :]
            v_blk = vb_ref[rows, :]
            carries = [carry_ref[h, lo:, :] for h in range(nh)]
            dq_old = [dq_acc[h, lo:, :] for h in range(nh)]
            dk_old = dk_acc[rows, :]
            dv_old = dv_acc[rows, :]
            outs = []
            for h in range(nh):
                cols = slice((h // 2) * LANE, (h // 2 + 1) * LANE)
                q_h, do_h = q_heads[h][lo:], do_heads[h][lo:]
                after = jnp.sum(jnp.where(lane[lo:] == kb, after_ref[lo:, h * LANE:(h + 1) * LANE], 0.0), axis=1,
                                keepdims=True)
                _, log_beta, later = _sb_scores(q_h, k_blk[:, cols], valid, later_mat, after)
                beta = jnp.exp2(log_beta)
                a = _masked(valid, jnp.exp2(log_beta + later))
                da = _dot(do_h, v_blk[:, cols], NT)
                gterm = a * da
                before = _dot(gterm.astype(BF16), before_mat, NN) + carries[h]
                dz_b = _masked(valid, gterm * (1.0 - beta) - beta * before).astype(BF16)
                outs.append((dq_old[h] + _dot(dz_b, k_blk[:, cols], NN), _dot(dz_b, q_h, TN),
                             _dot(a.astype(BF16), do_h, TN),
                             carries[h] + jnp.sum(gterm, axis=1, keepdims=True)))
            for h in range(nh):
                dq_acc[h, lo:, :] = outs[h][0]
                carry_ref[h, lo:, :] = outs[h][3]
            dk_new = [outs[2 * p][1] + outs[2 * p + 1][1] for p in range(pairs)]
            dv_new = [outs[2 * p][2] + outs[2 * p + 1][2] for p in range(pairs)]
            dk_acc[rows, :] = dk_old + (dk_new[0] if pairs == 1 else jnp.concatenate(dk_new, axis=1))
            dv_acc[rows, :] = dv_old + (dv_new[0] if pairs == 1 else jnp.concatenate(dv_new, axis=1))

        def step(kb, _):
            block(kb, None)
            return 0

        lax.fori_loop(0, ratio * i, step, 0)
        masks = _diagonal_masks(t, tk)
        for d in range(ratio):
            block(ratio * i + d, masks[d][d * tk:], d * tk)
        for p in range(pairs):
            dq_ref[:, p * LANE:(p + 1) * LANE] = (jnp.where(first, dq_acc[2 * p], dq_acc[2 * p + 1]) * scale).astype(BF16)

        @pl.when(i == nq - 1)
        def _():
            dk_ref[...] = (dk_acc[...] * LN_2).astype(BF16)
            dv_ref[...] = dv_acc[...].astype(BF16)

        _ride(rider, "finish", jnp.logical_and(grp == groups - 1, i == nq - 1), riding)

    blk = lambda base: pl.BlockSpec((t, wide), lambda g, i: (i, base // pairs + g))
    full = lambda base: pl.BlockSpec((s, wide), lambda g, i: (0, base // pairs + g))
    out_blk = pl.BlockSpec((t, wide), lambda g, i: (i, g))
    out_full = pl.BlockSpec((s, wide), lambda g, i: (0, g))
    big = jax.ShapeDtypeStruct((s, WIDTH), BF16)
    extra = _rider_call_args(rider, 7, 4)
    outs = pl.pallas_call(
        body, name=name, grid=(groups, nq),
        in_specs=[blk(COL_SB_Q), full(COL_SB_K), full(COL_SB_V), blk(COL_SB_G), out_blk,
                  pl.BlockSpec((t, nh * LANE), lambda g, i: (i, g)), out_blk] + extra["in_specs"],
        out_specs=[out_blk, out_full, out_full, out_blk] + extra["out_specs"],
        out_shape=[big, big, big, big] + extra["out_shape"],
        input_output_aliases=extra["aliases"],
        scratch_shapes=[pltpu.VMEM((s, wide), BF16), pltpu.VMEM((s, wide), BF16),
                        pltpu.VMEM((s, wide), F32), pltpu.VMEM((s, wide), F32),
                        pltpu.VMEM((nh, t, LANE), F32), pltpu.VMEM((nh, t, 1), F32)] + extra["scratch"],
        compiler_params=pltpu.CompilerParams(dimension_semantics=("arbitrary", "arbitrary")),
    )(u, u, u, u, o, after, dy, *extra["inputs"])
    return outs[:4], outs[4:]


def _gate_fwd(u, ys, w_branch, name, ts=256):
    s = u.shape[0]

    def body(m0, m1, m2, y0, y1, y2, w_ref, p0, p1, p2, out_ref):
        tot = None
        for n, (m_ref, y_ref, p_ref) in enumerate(((m0, y0, p0), (m1, y1, p1), (m2, y2, p2))):
            proj = _dot(y_ref[...], w_ref[n], NN)
            p_ref[...] = proj.astype(BF16)
            term = _sigmoid(m_ref[...].astype(F32)) * proj
            tot = term if tot is None else tot + term
        out_ref[...] = tot.astype(BF16)

    mspec = lambda n: pl.BlockSpec((ts, D_MODEL), lambda i: (i, COL_MERGE_1024 + n))
    row = pl.BlockSpec((ts, D_MODEL), lambda i: (i, 0))
    yspec = pl.BlockSpec((ts, WIDTH), lambda i: (i, 0))
    big = jax.ShapeDtypeStruct((s, D_MODEL), BF16)
    outs = pl.pallas_call(
        body, name=name, grid=(s // ts,),
        in_specs=[mspec(0), mspec(1), mspec(2), yspec, yspec, yspec,
                  pl.BlockSpec(w_branch.shape, lambda i: (0, 0, 0))],
        out_specs=[row] * 4, out_shape=[big] * 4,
    )(u, u, u, *ys, w_branch)
    return outs[:3], outs[3]


def _gate_bwd(u, projs, dmerged, w_branch, name, ts=256, rider=None):
    s = u.shape[0]
    steps = s // ts

    def body(*refs):
        own, riding = _split_refs(refs, 8, 9, 0, rider)
        m0, m1, m2, p0, p1, p2, dm_ref, w_ref, dp0, dp1, dp2, dl0, dl1, dl2, dy0, dy1, dy2 = own
        _ride(rider, "start", pl.program_id(0) == 0, riding)
        dm = dm_ref[...].astype(F32)
        for n, (m_ref, p_ref, dp_ref, dl_ref, dy_ref) in enumerate(((m0, p0, dp0, dl0, dy0), (m1, p1, dp1, dl1, dy1),
                                                                    (m2, p2, dp2, dl2, dy2))):
            gate = _sigmoid(m_ref[...].astype(F32))
            dp = (dm * gate).astype(BF16)
            dp_ref[...] = dp
            dl_ref[...] = (dm * p_ref[...].astype(F32) * gate * (1.0 - gate)).astype(BF16)
            dy_ref[...] = _dot(dp, w_ref[n], NT)
        _ride(rider, "finish", pl.program_id(0) == steps - 1, riding)

    mspec = lambda n: pl.BlockSpec((ts, D_MODEL), lambda i: (i, COL_MERGE_1024 + n))
    row = pl.BlockSpec((ts, D_MODEL), lambda i: (i, 0))
    yspec = pl.BlockSpec((ts, WIDTH), lambda i: (i, 0))
    big = jax.ShapeDtypeStruct((s, D_MODEL), BF16)
    extra = _rider_call_args(rider, 8, 9)
    outs = pl.pallas_call(
        body, name=name, grid=(steps,),
        in_specs=[mspec(0), mspec(1), mspec(2), row, row, row, row,
                  pl.BlockSpec(w_branch.shape, lambda i: (0, 0, 0))] + extra["in_specs"],
        out_specs=[row] * 6 + [yspec] * 3 + extra["out_specs"],
        out_shape=[big] * 6 + [jax.ShapeDtypeStruct((s, WIDTH), F32)] * 3 + extra["out_shape"],
        input_output_aliases=extra["aliases"], scratch_shapes=extra["scratch"],
        compiler_params=pltpu.CompilerParams(dimension_semantics=("arbitrary",)),
    )(u, u, u, *projs, dmerged, w_branch, *extra["inputs"])
    return outs[:3], outs[3:6], outs[6:9], outs[9:]


def _as_rows(a):
    return a.reshape(-1, a.shape[-1])


def _row_tile(rows, cols, bytes_per_row_elem=4, cap=1 << 20):
    tr = rows
    while tr * cols * bytes_per_row_elem > cap and tr % 2 == 0 and (tr // 2) % 16 == 0:
        tr //= 2
    return tr


def _cast_bf16(a, name):
    a2 = _as_rows(a)
    rows, cols = a2.shape
    tr = _row_tile(rows, cols)

    def body(a_ref, o_ref):
        o_ref[...] = a_ref[...].astype(BF16)

    spec = pl.BlockSpec((tr, cols), lambda i: (i, 0))
    out = pl.pallas_call(body, name=name, grid=(rows // tr,), in_specs=[spec], out_specs=spec,
                         out_shape=jax.ShapeDtypeStruct((rows, cols), BF16))(a2)
    return out.reshape(a.shape)


def _adamw(w, g, m, v, name):
    shape = w.shape
    w2, g2, m2, v2 = (_as_rows(a) for a in (w, g, m, v))
    rows, cols = w2.shape
    tr = _row_tile(rows, cols)
    c1 = 1.0 - ADAM_B1 ** ADAM_STEP
    c2 = 1.0 - ADAM_B2 ** ADAM_STEP

    def body(w_ref, g_ref, m_ref, v_ref, go_ref, d_ref, nm_ref, nv_ref):
        gv = g_ref[...]
        go_ref[...] = gv
        nm = ADAM_B1 * m_ref[...] + (1.0 - ADAM_B1) * gv
        nv = ADAM_B2 * v_ref[...] + (1.0 - ADAM_B2) * (gv * gv)
        nm_ref[...] = nm
        nv_ref[...] = nv
        d_ref[...] = -ADAM_LR * ((nm / c1) / (jnp.sqrt(nv / c2) + ADAM_EPS) + ADAM_WD * w_ref[...])

    spec = pl.BlockSpec((tr, cols), lambda i: (i, 0))
    sds = jax.ShapeDtypeStruct((rows, cols), F32)
    outs = pl.pallas_call(body, name=name, grid=(rows // tr,), in_specs=[spec] * 4, out_specs=[spec] * 4,
                          out_shape=[sds] * 4)(w2, g2, m2, v2)
    return tuple(o.reshape(shape) for o in outs)


def _sum_slots(a, out_dtype, name):
    n = a.shape[0]
    a3 = a.reshape(n, -1, a.shape[-1])
    _, rows, cols = a3.shape
    tr = _row_tile(rows, cols * n)

    def body(a_ref, o_ref):
        tot = a_ref[0].astype(F32)
        for k in range(1, n):
            tot = tot + a_ref[k].astype(F32)
        o_ref[...] = tot.astype(out_dtype)

    out = pl.pallas_call(
        body, name=name, grid=(rows // tr,),
        in_specs=[pl.BlockSpec((n, tr, cols), lambda i: (0, i, 0))],
        out_specs=pl.BlockSpec((tr, cols), lambda i: (i, 0)),
        out_shape=jax.ShapeDtypeStruct((rows, cols), out_dtype))(a3)
    return out.reshape(a.shape[1:])


def _chip_sum(own, recv, axis, core, name):
    half = recv.shape
    nd = len(half)
    last = nd - 1
    if axis == last:
        tl, nt = half[last], 1
    else:
        tl = min(half[last], 2048)
        nt = half[last] // tl
    block = half[:last] + (tl,)

    def own_index(i, core_ref):
        idx = [0] * nd
        idx[last] = i
        if axis == last:
            idx[last] = core_ref[0]
        else:
            idx[axis] = core_ref[0]
        return tuple(idx)

    def recv_index(i, core_ref):
        idx = [0] * nd
        idx[last] = i
        return tuple(idx)

    def body(core_ref, own_ref, recv_ref, o_ref):
        o_ref[...] = (own_ref[...].astype(F32) + recv_ref[...].astype(F32)).astype(BF16)

    return pl.pallas_call(
        body, name=name,
        grid_spec=pltpu.PrefetchScalarGridSpec(
            num_scalar_prefetch=1, grid=(nt,),
            in_specs=[pl.BlockSpec(block, own_index), pl.BlockSpec(block, recv_index)],
            out_specs=pl.BlockSpec(block, recv_index)),
        out_shape=jax.ShapeDtypeStruct(half, BF16),
    )(core, own, recv)


def _mesh_position():
    return lax.axis_index("x"), lax.axis_index("y"), lax.axis_index("c")


def _other_chips(x, y):
    return [(1 - x, y), (x, 1 - y), (1 - x, 1 - y)]


ALL_FLIPS = [(0, 0, 1), (1, 0, 0), (0, 1, 0), (1, 1, 0), (1, 0, 1), (0, 1, 1), (1, 1, 1)]


def _half(ref, axis, which, size):
    idx = [slice(None)] * len(ref.shape)
    idx[axis] = pl.ds(which * size, size)
    return ref.at[tuple(idx)]


def _sub(ref, picks):
    idx = [slice(None)] * len(ref.shape)
    for axis, start, size in picks:
        idx[axis] = pl.ds(start, size)
    return ref.at[tuple(idx)]


def _remote(src, dst, sems_send, sems_recv, k, to):
    return pltpu.make_async_remote_copy(src_ref=src, dst_ref=dst, send_sem=sems_send.at[k], recv_sem=sems_recv.at[k],
                                        device_id=to, device_id_type=MESH)


def _cast_shard(w, layer, shard_axis, pos, name, tr=512):
    shape = w.shape[1:]
    nd = len(shape)
    assert shard_axis in (nd - 1, nd - 2)
    rows, cols = shape[-2:]
    tr = min(tr, rows)
    nt = rows // tr
    lead = shape[:-2]
    full = list(shape)
    full[shard_axis] *= N_CHIPS
    block = (1,) * len(lead) + (tr, cols)

    def in_index(*args):
        return (layer, *args[:-1], 0)

    def out_index(*args):
        *g, pos_ref = args
        if shard_axis == nd - 1:
            return (*g, pos_ref[1])
        return (*g[:-1], pos_ref[1] * nt + g[-1], 0)

    def body(pos_ref, a_ref, o_ref):
        o_ref[...] = a_ref[...].astype(BF16)

    return pl.pallas_call(
        body, name=name,
        grid_spec=pltpu.PrefetchScalarGridSpec(
            num_scalar_prefetch=1, grid=lead + (nt,),
            in_specs=[pl.BlockSpec((None,) + block, in_index)], out_specs=pl.BlockSpec(block, out_index)),
        out_shape=jax.ShapeDtypeStruct(tuple(full), BF16),
    )(pos, w)


class _Rider:
    def __init__(self, inputs, out_shape, aliases, scratch, start, middle, finish):
        self.inputs, self.out_shape, self.aliases, self.scratch = inputs, out_shape, aliases, scratch
        self.start, self.middle, self.finish = start, middle, finish


def _weight_gather_rider(fulls, layout):
    n = len(fulls)

    def copies(outs, sems):
        send_sems, recv_sems = sems
        x, y, c = _mesh_position()
        chips = _other_chips(x, y)
        sibling = (x, y, 1 - c)
        mine = 2 * x + y

        def place(t, chip, core):
            sh_axis, sh_size, half_axis, half_size, *part = layout[t]
            offset, size = part if part else (0, sh_size)
            return _sub(outs[t], [(sh_axis, chip * sh_size + offset, size), (half_axis, core * half_size, half_size)])

        direct, arrive, forward, arrive_fwd = [], [], [], []
        for t in range(n):
            for k, (px, py) in enumerate(chips):
                theirs = 2 * px + py
                direct.append(_remote(place(t, mine, c), place(t, mine, c), send_sems, recv_sems, 6 * t + k, (px, py, c)))
                arrive.append(_remote(place(t, theirs, c), place(t, theirs, c), send_sems, recv_sems, 6 * t + k, (px, py, c)))
                forward.append(_remote(place(t, theirs, c), place(t, theirs, c), send_sems, recv_sems, 6 * t + 3 + k, sibling))
                arrive_fwd.append(_remote(place(t, theirs, 1 - c), place(t, theirs, 1 - c), send_sems, recv_sems,
                                          6 * t + 3 + k, sibling))
        return direct, arrive, forward, arrive_fwd

    def start(ins, outs, sems):
        for cp in copies(outs, sems)[0]:
            cp.start()

    def middle(ins, outs, sems):
        _, arrive, forward, _ = copies(outs, sems)
        for a, f in zip(arrive, forward):
            a.wait_recv()
            f.start()

    def finish(ins, outs, sems):
        direct, _, forward, arrive_fwd = copies(outs, sems)
        for cp in arrive_fwd:
            cp.wait_recv()
        for cp in direct + forward:
            cp.wait_send()

    rider = _Rider(list(fulls), [jax.ShapeDtypeStruct(a.shape, a.dtype) for a in fulls], {k: k for k in range(n)},
                   [pltpu.SemaphoreType.DMA((6 * n,)), pltpu.SemaphoreType.DMA((6 * n,))], start, middle, finish)
    rider.copies = copies
    return rider


WEIGHT_LAYOUT = [(1, 2048, 0, 512), (2, 256, 1, 256), (0, 256, 1, 512)]


def _conv_w_rider(conv_w):
    def copies(ins, outs, sems):
        x, y, c = _mesh_position()
        mine = 2 * x + y
        local = pltpu.make_async_copy(ins[0], outs[0].at[mine], sems[2].at[0])
        send = [_remote(ins[0], outs[0].at[mine], sems[0], sems[1], k, (px, py, c))
                for k, (px, py) in enumerate(_other_chips(x, y))]
        arrive = [_remote(ins[0], outs[0].at[2 * px + py], sems[0], sems[1], k, (px, py, c))
                  for k, (px, py) in enumerate(_other_chips(x, y))]
        return local, send, arrive

    def start(ins, outs, sems):
        local, send, _ = copies(ins, outs, sems)
        local.start()
        for cp in send:
            cp.start()

    def finish(ins, outs, sems):
        local, send, arrive = copies(ins, outs, sems)
        for cp in arrive:
            cp.wait_recv()
        for cp in send:
            cp.wait_send()
        local.wait()

    return _Rider([conv_w], [jax.ShapeDtypeStruct((N_CHIPS,) + conv_w.shape, F32)], {},
                  [pltpu.SemaphoreType.DMA((3,)), pltpu.SemaphoreType.DMA((3,)), pltpu.SemaphoreType.DMA((1,))],
                  start, None, finish)


def _join_riders(first, second):
    ni, no, ns = len(first.inputs), len(first.out_shape), len(first.scratch)

    def phase(name):
        fns = [getattr(first, name), getattr(second, name)]
        if fns[0] is None and fns[1] is None:
            return None

        def run(ins, outs, sems):
            if fns[0] is not None:
                fns[0](ins[:ni], outs[:no], sems[:ns])
            if fns[1] is not None:
                fns[1](ins[ni:], outs[no:], sems[ns:])
        return run

    aliases = dict(first.aliases)
    aliases.update({ni + a: no + b for a, b in second.aliases.items()})
    return _Rider(first.inputs + second.inputs, first.out_shape + second.out_shape, aliases,
                  first.scratch + second.scratch, phase("start"), phase("middle"), phase("finish"))


def _in_proj_streamed(h, w_in, order, name, rider=None, tm=1024, tn=1024):
    m, k = h.shape
    n = w_in.shape[1]
    tm = min(tm, m)
    nj, ni = n // tn, m // tm
    second = min(1, ni - 1)
    per_chip = WEIGHT_LAYOUT[0][1] // tn
    gather = _weight_gather_rider([w_in] * per_chip, [WEIGHT_LAYOUT[0] + (t * tn, tn) for t in range(per_chip)])

    def body(order_ref, *refs):
        own, riding = _split_refs(refs, 2, 2, 4, rider)
        h_ref, _, u_ref, w_ref, wbuf, tile_sems, send_sems, recv_sems = own
        j, i = pl.program_id(0), pl.program_id(1)
        first = jnp.logical_and(j == 0, i == 0)
        last = jnp.logical_and(j == nj - 1, i == ni - 1)
        direct, arrive, forward, arrive_fwd = gather.copies([w_ref] * per_chip, (send_sems, recv_sems))

        def tile_copy(jj, slot):
            cols = pl.ds(pl.multiple_of(order_ref[jj] * tn, tn), tn)
            return pltpu.make_async_copy(w_ref.at[:, cols], wbuf.at[slot], tile_sems.at[slot])

        @pl.when(first)
        def _():
            for cp in direct:
                cp.start()
            tile_copy(0, 0).start()

        _ride(rider, "start", first, riding)
        for kk in range(len(direct)):
            before = per_chip + kk - 1

            @pl.when(jnp.logical_and(j == before, i == 0))
            def _(kk=kk):
                arrive[kk].wait_recv()
                forward[kk].start()

            @pl.when(jnp.logical_and(j == before, i == second))
            def _(kk=kk):
                arrive_fwd[kk].wait_recv()

        @pl.when(i == 0)
        def _():
            tile_copy(j, j % 2).wait()

        @pl.when(jnp.logical_and(i == second, j + 1 < nj))
        def _():
            tile_copy(j + 1, (j + 1) % 2).start()

        _ride(rider, "middle", jnp.logical_and(j == (3 * nj) // 4, i == 0), riding)
        u_ref[...] = _dot(h_ref[...], wbuf[j % 2], NN).astype(BF16)

        @pl.when(last)
        def _():
            for cp in direct + forward:
                cp.wait_send()

        _ride(rider, "finish", last, riding)

    extra = _rider_call_args(rider, 3, 2)
    aliases = {2: 1}
    aliases.update(extra["aliases"])
    outs = pl.pallas_call(
        body, name=name,
        grid_spec=pltpu.PrefetchScalarGridSpec(
            num_scalar_prefetch=1, grid=(nj, ni),
            in_specs=[pl.BlockSpec((tm, k), lambda j, i, order_ref: (i, 0)), ANY] + extra["in_specs"],
            out_specs=[pl.BlockSpec((tm, tn), lambda j, i, order_ref: (i, order_ref[j])), ANY] + extra["out_specs"],
            scratch_shapes=[pltpu.VMEM((2, k, tn), BF16), pltpu.SemaphoreType.DMA((2,))] + gather.scratch
            + extra["scratch"]),
        out_shape=[jax.ShapeDtypeStruct((m, n), BF16), jax.ShapeDtypeStruct(w_in.shape, BF16)] + extra["out_shape"],
        input_output_aliases=aliases,
        compiler_params=pltpu.CompilerParams(dimension_semantics=("arbitrary", "arbitrary")),
    )(order, h, w_in, *extra["inputs"])
    return outs[0], outs[1], outs[2:]


def _gather_weights(fulls, layout, conv_w):
    rider = _weight_gather_rider(fulls, layout)
    n = len(fulls)

    def body(*refs):
        cw, outs, cw_f = refs[n], refs[n + 1:2 * n + 1], refs[2 * n + 1]
        sems, (cw_send, cw_recv, local_sem) = refs[2 * n + 2:2 * n + 4], refs[2 * n + 4:]
        x, y, c = _mesh_position()
        chips = _other_chips(x, y)
        mine = 2 * x + y
        local = pltpu.make_async_copy(cw, cw_f.at[mine], local_sem.at[0])
        local.start()
        rider.start(None, outs, sems)
        small = [_remote(cw, cw_f.at[mine], cw_send, cw_recv, k, (*chip, c)) for k, chip in enumerate(chips)]
        for cp in small:
            cp.start()
        rider.middle(None, outs, sems)
        rider.finish(None, outs, sems)
        for k, (px, py) in enumerate(chips):
            _remote(cw, cw_f.at[2 * px + py], cw_send, cw_recv, k, (px, py, c)).wait_recv()
        for cp in small:
            cp.wait_send()
        local.wait()

    outs = pl.pallas_call(
        body, name="gather_weights",
        in_specs=[ANY] * (n + 1), out_specs=[ANY] * (n + 1),
        out_shape=rider.out_shape + [jax.ShapeDtypeStruct((N_CHIPS,) + conv_w.shape, F32)],
        input_output_aliases=rider.aliases,
        scratch_shapes=rider.scratch + [pltpu.SemaphoreType.DMA((3,)), pltpu.SemaphoreType.DMA((3,)),
                                        pltpu.SemaphoreType.DMA((1,))],
    )(*fulls, conv_w)
    return outs[:n], outs[n]


def _swap_rider(items):
    n = len(items)
    halves = []
    for a, axis in items:
        shp = list(a.shape)
        shp[axis] //= 2
        halves.append(tuple(shp))

    def copies(ins, outs, sems):
        x, y, c = _mesh_position()
        return [_remote(_half(ins[k], items[k][1], 1 - c, halves[k][items[k][1]]), outs[k], sems[0], sems[1], k,
                        (x, y, 1 - c)) for k in range(n)]

    def start(ins, outs, sems):
        for cp in copies(ins, outs, sems):
            cp.start()

    def finish(ins, outs, sems):
        for cp in copies(ins, outs, sems):
            cp.wait()

    return _Rider([a for a, _ in items], [jax.ShapeDtypeStruct(h, a.dtype) for h, (a, _) in zip(halves, items)], {},
                  [pltpu.SemaphoreType.DMA((n,)), pltpu.SemaphoreType.DMA((n,))], start, None, finish)


def _swap_halves(items, name):
    rider = _swap_rider(items)
    n = len(items)

    def body(*refs):
        parts = (refs[:n], refs[n:2 * n], refs[2 * n:])
        rider.start(*parts)
        rider.finish(*parts)

    return pl.pallas_call(
        body, name=name, in_specs=[ANY] * n, out_specs=[ANY] * n, out_shape=rider.out_shape,
        scratch_shapes=rider.scratch,
    )(*rider.inputs)


def _grad_exchange_rider(items):
    n = len(items)
    slices = []
    for a, axis in items:
        shp = list(a.shape)
        shp[axis] //= N_CHIPS
        slices.append(tuple(shp))

    def copies(ins, outs, sems):
        send_sems, recv_sems = sems
        x, y, c = _mesh_position()
        made = []
        for k in range(n):
            axis = items[k][1]
            for r, (px, py) in enumerate(_other_chips(x, y)):
                made.append(_remote(_half(ins[k], axis, 2 * px + py, slices[k][axis]), outs[k].at[r],
                                    send_sems, recv_sems, 3 * k + r, (px, py, c)))
        return made

    def start(ins, outs, sems):
        for cp in copies(ins, outs, sems):
            cp.start()

    def finish(ins, outs, sems):
        for cp in copies(ins, outs, sems):
            cp.wait()

    return _Rider([a for a, _ in items], [jax.ShapeDtypeStruct((N_CHIPS - 1,) + s, BF16) for s in slices], {},
                  [pltpu.SemaphoreType.DMA((3 * n,)), pltpu.SemaphoreType.DMA((3 * n,))], start, None, finish)


def _small_gather_rider(small):
    def copies(ins, outs, sems):
        x, y, c = _mesh_position()
        me = 4 * x + 2 * y + c
        local = pltpu.make_async_copy(ins[0], outs[0].at[me], sems[2].at[0])
        remote = [_remote(ins[0], outs[0].at[me], sems[0], sems[1], r, (x ^ fx, y ^ fy, c ^ fc))
                  for r, (fx, fy, fc) in enumerate(ALL_FLIPS)]
        return local, remote

    def start(ins, outs, sems):
        local, remote = copies(ins, outs, sems)
        local.start()
        for cp in remote:
            cp.start()

    def finish(ins, outs, sems):
        local, remote = copies(ins, outs, sems)
        for cp in remote:
            cp.wait()
        local.wait()

    n = len(ALL_FLIPS)
    return _Rider([small], [jax.ShapeDtypeStruct((2 * N_CHIPS,) + small.shape, F32)], {},
                  [pltpu.SemaphoreType.DMA((n,)), pltpu.SemaphoreType.DMA((n,)), pltpu.SemaphoreType.DMA((1,))],
                  start, None, finish)


def _sum_chips(recv, own, shard_axis, split_axis, pos, dest, layer, name, tr=128):
    sl = recv.shape[1:]
    nd = len(sl)
    tiled = nd == 2 and sl[0] > tr
    nt = sl[0] // tr if tiled else 1
    block = ((tr,) + sl[1:]) if tiled else sl
    shard = list(sl)
    shard[split_axis] *= 2

    def recv_index(i, pos_ref):
        return (0, i) + (0,) * (nd - 1) if tiled else (0,) * (nd + 1)

    def own_index(i, pos_ref):
        idx = [0] * nd
        idx[shard_axis] = pos_ref[1]
        if tiled:
            idx[0] = pos_ref[1] * nt + i if shard_axis == 0 else i
        return tuple(idx)

    def out_index(i, pos_ref):
        idx = [0] * nd
        idx[split_axis] = pos_ref[0]
        if tiled:
            idx[0] = pos_ref[0] * nt + i if split_axis == 0 else i
        return (layer, *idx)

    def body(pos_ref, recv_ref, own_ref, *rest):
        o_ref = rest[-1]
        tot = own_ref[...].astype(F32)
        for k in range(N_CHIPS - 1):
            tot = tot + recv_ref[k].astype(F32)
        o_ref[0] = tot

    in_specs = [pl.BlockSpec((N_CHIPS - 1,) + block, recv_index), pl.BlockSpec(block, own_index)]
    args = [pos, recv, own]
    aliases = {}
    if dest is not None:
        in_specs.append(ANY)
        args.append(dest)
        aliases = {3: 0}
    return pl.pallas_call(
        body, name=name,
        grid_spec=pltpu.PrefetchScalarGridSpec(
            num_scalar_prefetch=1, grid=(nt,), in_specs=in_specs,
            out_specs=pl.BlockSpec((1,) + block, out_index)),
        out_shape=jax.ShapeDtypeStruct((DEPTH,) + tuple(shard), F32),
        input_output_aliases=aliases,
    )(*args)


def _share_halves(bufs, late_small, name):
    n = len(bufs)
    small = _small_gather_rider(late_small)

    def body(*refs):
        outs, (send_sems, recv_sems) = refs[n + 1:2 * n + 1], refs[2 * n + 2:2 * n + 4]
        small_parts = ([refs[n]], [refs[2 * n + 1]], refs[2 * n + 4:])
        x, y, c = _mesh_position()
        small.start(*small_parts)
        copies = []
        for k, (a, axis) in enumerate(bufs):
            size = a.shape[1 + axis] // 2
            mine = _half(outs[k], 1 + axis, c, size)
            copies.append(_remote(mine, mine, send_sems, recv_sems, k, (x, y, 1 - c)))
        for cp in copies:
            cp.start()
        for cp in copies:
            cp.wait()
        small.finish(*small_parts)

    outs = pl.pallas_call(
        body, name=name, in_specs=[ANY] * (n + 1), out_specs=[ANY] * (n + 1),
        out_shape=[jax.ShapeDtypeStruct(a.shape, F32) for a, _ in bufs] + small.out_shape,
        input_output_aliases={k: k for k in range(n)},
        scratch_shapes=[pltpu.SemaphoreType.DMA((n,)), pltpu.SemaphoreType.DMA((n,))] + small.scratch,
    )(*[a for a, _ in bufs], late_small)
    return outs[:n], outs[n]


def _layer_fwd(x, p, l, rider=None, first_layer=None):
    tag = f"l{l}_"
    h = _rms_fwd(x, p["pre_g"], tag + "pre_norm")
    if first_layer is None:
        u = _matmul(h, p["w_in"], "nn", BF16, tag + "in_proj")
    else:
        order, proj_rider = first_layer
        u, w_in, (w_branch, w_out, conv_w_by_chip) = _in_proj_streamed(h, p["w_in"], order, tag + "in_proj", proj_rider)
        conv_w_all = conv_w_by_chip.transpose(1, 2, 0, 3).reshape(DEPTH, 3, WIDTH)
        p = dict(p, w_in=w_in, w_branch=w_branch, w_out=w_out, conv_w=conv_w_all[l], conv_w_all=conv_w_all)
    y_pool = _pool_fwd(u, p["pool_w"], p["pool_scale"], tag + "pool")
    y_conv = _conv_fwd(u, p["conv_w"], p["conv_b"], tag + "conv")
    (o_sb, y_sb, sb_after), carried = _sb_fwd(u, tag + "stickbreak", rider=rider)
    ys = [y_pool, y_conv, y_sb]
    projs, merged = _gate_fwd(u, ys, p["w_branch"], tag + "merge")
    out = _matmul(merged, p["w_out"], "nn", F32, tag + "out_proj")
    saved = dict(x=x, h=h, u=u, ys=ys, o_sb=o_sb, sb_after=sb_after, projs=projs, merged=merged, out=out)
    return out, saved, carried, p


def _layer_bwd(dy, p, saved, l, merge_rider=None, early=None, before_dw=None, late=None):
    tag = f"l{l}_bwd_"
    u = saved["u"]
    d_out, g_post = _rms_bwd(saved["out"], p["post_g"], dy, None, BF16, tag + "post_norm")
    d_merged = _matmul(d_out, p["w_out"], "nt", BF16, tag + "out_proj_dx")
    g_w_out = _matmul(saved["merged"], d_out, "tn", BF16, tag + "out_proj_dw", tk=2048)
    d_projs, d_logits, d_ys, carried_merge = _gate_bwd(u, saved["projs"], d_merged, p["w_branch"], tag + "merge",
                                                       rider=merge_rider)
    g_w_branch = jnp.stack([_matmul(saved["ys"][n], d_projs[n], "tn", BF16, tag + f"branch_dw{n}", tk=2048)
                            for n in range(3)])
    rider = early(g_w_branch, g_w_out, carried_merge) if early else None
    d_pv, d_pg, g_pool_w, g_pool_scale = _pool_bwd(u, d_ys[0], p["pool_w"], p["pool_scale"], tag + "pool")
    d_cx, d_cgb, d_cgc, d_cg, g_conv_w, g_conv_b = _conv_bwd(u, d_ys[1], p["conv_w"], p["conv_b"], tag + "conv")
    (d_q, d_k, d_v, d_sg), carried_attn = _sb_bwd(u, saved["o_sb"], saved["sb_after"], d_ys[2], tag + "stickbreak",
                                                  rider=rider)
    du = jnp.concatenate([d_pv, d_pg, d_cx, d_cgb, d_cgc, d_cg, d_q, d_k, d_v, d_sg] + list(d_logits), axis=1)
    grads = dict(w_branch=g_w_branch, w_out=g_w_out, post_g=g_post, pool_w=g_pool_w, pool_scale=g_pool_scale,
                 conv_w=g_conv_w, conv_b=g_conv_b)
    rider = before_dw(grads) if before_dw else None
    g_w_in = _matmul(saved["h"], du, "tn", BF16, tag + "in_proj_dw", tk=2048, rider=rider)
    g_w_in, carried_dw = g_w_in if rider else (g_w_in, [])
    rider = late(g_w_in) if late else None
    dh = _matmul(du, p["w_in"], "nt", BF16, tag + "in_proj_dx", tk=2048, rider=rider)
    dh, carried_dx = dh if rider else (dh, [])
    dx, g_pre = _rms_bwd(saved["x"], p["pre_g"], dh, dy, F32, tag + "pre_norm")
    grads.update(w_in=g_w_in, pre_g=g_pre)
    return dx, grads, carried_attn, carried_dw, carried_dx


SMALL_ORDER = ["pre_g", "pool_w", "pool_scale", "conv_w", "conv_b", "post_g"]


def _pack_small(per_layer):
    parts, spans, at = [], {}, 0
    for name in SMALL_ORDER:
        a = jnp.stack([per_layer[l][name] for l in range(DEPTH)]).reshape(-1, LANE)
        parts.append(a)
        spans[name] = (at, a.shape[0])
        at += a.shape[0]
    return jnp.concatenate(parts, axis=0), spans


def kernel(x, pre_norm_g, w_in, pool_w, pool_scale, conv_w, conv_b, w_branch, w_out, post_norm_g, loss_target, m_pre_norm_g, m_w_in, m_pool_w, m_pool_scale, m_conv_w, m_conv_b, m_w_branch, m_w_out, m_post_norm_g, v_pre_norm_g, v_w_in, v_pool_w, v_pool_scale, v_conv_w, v_conv_b, v_w_branch, v_w_out, v_post_norm_g):
    mx, my, mc = _mesh_position()
    chip = 2 * mx + my
    core = mc.astype(jnp.int32).reshape(1)
    pos = jnp.stack([mc, chip]).astype(jnp.int32)

    names = ["w_in", "w_branch", "w_out"]
    given = dict(w_in=w_in, w_branch=w_branch, w_out=w_out)
    in_place = [[_cast_shard(given[n], l, WEIGHT_LAYOUT[i][0], pos, f"cast_{n}{l}") for i, n in enumerate(names)]
                for l in range(DEPTH)]
    gathered = in_place[0]
    pool_w_b = _cast_bf16(pool_w, "cast_pool_w")
    per_chip = WEIGHT_LAYOUT[0][1] // 1024
    others = [2 * px + py for px, py in _other_chips(mx, my)]
    order = jnp.stack([per_chip * chip + t_ for t_ in range(per_chip)]
                      + [per_chip * c_ + t_ for t_ in range(per_chip) for c_ in others]).astype(jnp.int32)
    conv_w_f = None

    def layer_params(l, big):
        return dict(pre_g=pre_norm_g[l:l + 1], post_g=post_norm_g[l:l + 1], w_in=big[0], w_branch=big[1],
                    w_out=big[2], pool_w=pool_w_b[l], pool_scale=pool_scale[l:l + 1],
                    conv_w=None if conv_w_f is None else conv_w_f[l], conv_b=conv_b[l:l + 1])

    act = x[0]
    params, saved = [], []
    for l in range(DEPTH):
        rider = _weight_gather_rider(in_place[l + 1], WEIGHT_LAYOUT) if l + 1 < DEPTH else None
        first_layer = None
        if l == 0:
            first_layer = (order, _join_riders(_weight_gather_rider(gathered[1:], WEIGHT_LAYOUT[1:]),
                                               _conv_w_rider(conv_w)))
        out, sv, gathered, layer_p = _layer_fwd(act, layer_params(l, gathered), l, rider, first_layer)
        if l == 0:
            conv_w_f = layer_p["conv_w_all"]
        params.append(layer_p)
        saved.append(sv)
        if l < DEPTH - 1:
            act = _resid_out(act, out, params[l]["post_g"], None, f"l{l}_resid")
    dy, loss_part = _resid_out(act, saved[-1]["out"], params[-1]["post_g"], loss_target[0], "loss_head")

    split_axis = dict(w_in=0, w_branch=1, w_out=1)
    shard_axis = dict(w_in=1, w_branch=2, w_out=0)
    grads = [None] * DEPTH
    chip_sums = [dict() for _ in range(DEPTH)]
    by_chip = [dict() for _ in range(DEPTH)]

    def reduce_in_chip(l, which, g):
        items = [(g[n], split_axis[n]) for n in which]
        from_sibling = _swap_halves(items, f"swap_grad_halves{l}_{which[0]}")
        for n, (a, axis), r in zip(which, items, from_sibling):
            chip_sums[l][n] = _chip_sum(a, r, axis, core, f"chip_sum{l}_{n}")

    def exchange_rider(keys):
        return _grad_exchange_rider([(chip_sums[l][n], shard_axis[n]) for l, n in keys])

    waiting = []
    for l in reversed(range(DEPTH)):
        sent_early, sent_late = list(waiting) + [(l, "w_branch"), (l, "w_out")], [(l, "w_in")]
        waiting_items = [(grads[ll][n], split_axis[n]) for ll, n in waiting]

        def early(g_w_branch, g_w_out, from_sibling, l=l, keys=sent_early, above=tuple(waiting), items=waiting_items):
            for (ll, n), (a, axis), r in zip(above, items, from_sibling):
                chip_sums[ll][n] = _chip_sum(a, r, axis, core, f"chip_sum{ll}_{n}")
            reduce_in_chip(l, ["w_branch", "w_out"], dict(w_branch=g_w_branch, w_out=g_w_out))
            return exchange_rider(keys)

        def late(g_w_in, l=l, keys=sent_late):
            reduce_in_chip(l, ["w_in"], dict(w_in=g_w_in))
            return exchange_rider(keys)

        def before_dw(partial, l=l):
            layers = [dict(partial, pre_g=jnp.zeros_like(pre_norm_g[:1])) if ll == l else grads[ll]
                      for ll in range(DEPTH)]
            return _small_gather_rider(_pack_small(layers)[0])

        if l == DEPTH - 1:
            dy, grads[l], _, _, _ = _layer_bwd(dy, params[l], saved[l], l)
            waiting = [(l, n) for n in names]
        else:
            dy, grads[l], got_early, got_dw, got_late = _layer_bwd(
                dy, params[l], saved[l], l, _swap_rider(waiting_items), early, before_dw if l == 0 else None, late)
            for (ll, n), r in zip(sent_early + sent_late, list(got_early) + list(got_late)):
                by_chip[ll][n] = r
            if l == 0:
                small_all = got_dw[0]
            waiting = []
    assert not waiting
    grad_x = dy[None]
    _, spans = _pack_small(grads)
    late_small = jnp.concatenate([grads[0]["pre_g"].reshape(-1, LANE), jnp.broadcast_to(loss_part, (8, LANE))])
    bufs = []
    for n in names:
        dest = None
        for l in range(DEPTH):
            dest = _sum_chips(by_chip[l][n], chip_sums[l][n], shard_axis[n], split_axis[n], pos, dest, l,
                              f"sum_chips{l}_{n}")
        bufs.append((dest, split_axis[n]))
    (g_w_in, g_w_branch, g_w_out), late_all = _share_halves(bufs, late_small, "share_grad_halves")

    late_sum = _sum_slots(late_all, F32, "sum_late_small")
    n_gain = late_small.shape[0] - 8
    loss = late_sum[n_gain, 0]
    small_sum = _sum_slots(small_all, F32, "sum_small")
    at, _ = spans["pre_g"]
    small_sum = jnp.concatenate([small_sum[:at], late_sum[:n_gain], small_sum[at + n_gain:]])
    small = {}
    for name, like in (("pre_g", pre_norm_g), ("pool_w", pool_w), ("pool_scale", pool_scale), ("conv_b", conv_b),
                       ("post_g", post_norm_g)):
        at, n = spans[name]
        small[name] = small_sum[at:at + n].reshape(like.shape)
    at, n = spans["conv_w"]
    g_conv_w_full = small_sum[at:at + n].reshape(DEPTH, 3, WIDTH)
    g_conv_w = lax.dynamic_slice_in_dim(g_conv_w_full, chip * conv_w.shape[2], conv_w.shape[2], axis=2)

    g = dict(pre_norm_g=small["pre_g"], w_in=g_w_in, pool_w=small["pool_w"], pool_scale=small["pool_scale"],
             conv_w=g_conv_w, conv_b=small["conv_b"], w_branch=g_w_branch, w_out=g_w_out, post_norm_g=small["post_g"])
    w = dict(pre_norm_g=pre_norm_g, w_in=w_in, pool_w=pool_w, pool_scale=pool_scale, conv_w=conv_w, conv_b=conv_b,
             w_branch=w_branch, w_out=w_out, post_norm_g=post_norm_g)
    m = dict(pre_norm_g=m_pre_norm_g, w_in=m_w_in, pool_w=m_pool_w, pool_scale=m_pool_scale, conv_w=m_conv_w,
             conv_b=m_conv_b, w_branch=m_w_branch, w_out=m_w_out, post_norm_g=m_post_norm_g)
    v = dict(pre_norm_g=v_pre_norm_g, w_in=v_w_in, pool_w=v_pool_w, pool_scale=v_pool_scale, conv_w=v_conv_w,
             conv_b=v_conv_b, w_branch=v_w_branch, w_out=v_w_out, post_norm_g=v_post_norm_g)
    order = ["pre_norm_g", "w_in", "pool_w", "pool_scale", "conv_w", "conv_b", "w_branch", "w_out", "post_norm_g"]
    upd = {n: _adamw(w[n], g[n], m[n], v[n], "adamw_" + n) for n in order}
    return (loss, grad_x, *[upd[n][0] for n in order], *[upd[n][1] for n in order], *[upd[n][2] for n in order],
            *[upd[n][3] for n in order])
```

```python
import functools

import jax
import jax.numpy as jnp
from jax import lax
from jax.experimental import pallas as pl
from jax.experimental.pallas import tpu as pltpu

F32 = jnp.float32
BF16 = jnp.bfloat16
MESH = pl.DeviceIdType.MESH
ANY = pl.BlockSpec(memory_space=pl.ANY)

DEPTH = 2
D_MODEL = 1024
WIDTH = 512
N_IN = 8192
N_CHIPS = 4
HEAD_DIM = 64
RMS_EPS = 1e-6
POOL_HALO = 16
CONV_HALO = 16
LANE = 128
COL_POOL_V, COL_POOL_G = 0, 4
COL_CONV_X, COL_CONV_GB, COL_CONV_GC, COL_CONV_G = 8, 12, 16, 20
COL_SB_Q, COL_SB_K, COL_SB_V, COL_SB_G = 24, 28, 32, 36
COL_MERGE_1024 = 5

ADAM_LR, ADAM_B1, ADAM_B2, ADAM_EPS, ADAM_WD, ADAM_STEP = 0.001, 0.9, 0.999, 1e-08, 0.01, 10

NN = (((1,), (0,)), ((), ()))
NT = (((1,), (1,)), ((), ()))
TN = (((0,), (0,)), ((), ()))


def _sigmoid(x):
    return 1.0 / (1.0 + jnp.exp(-x))


def _silu_and_grad(x):
    s = _sigmoid(x)
    return x * s, s * (1.0 + x * (1.0 - s))


def _dot(a, b, dims):
    return lax.dot_general(a, b, dims, preferred_element_type=F32)


def _matmul(a, b, mode, out_dtype, name, tm=1024, tn=1024, tk=1024, b_lead=(), rider=None):
    b_shape = b.shape[len(b_lead):]
    if mode == "nn":
        (m, k), (k2, n) = a.shape, b_shape
    elif mode == "nt":
        (m, k), (n, k2) = a.shape, b_shape
    else:
        (k, m), (k2, n) = a.shape, b_shape
    assert k == k2 and a.dtype == BF16 and b.dtype == BF16
    tm, tn, tk = min(tm, m), min(tn, n), min(tk, k)
    assert m % tm == 0 and n % tn == 0 and k % tk == 0
    nk = k // tk
    dims = {"nn": NN, "nt": NT, "tn": TN}[mode]

    grid = (m // tm, n // tn, nk)

    def at_step(step):
        return functools.reduce(jnp.logical_and, [pl.program_id(d) == s for d, s in enumerate(step)])

    def body(*refs):
        (a_ref, b_ref, o_ref, *scratch), riding = _split_refs(refs, 2, 1, 1 if nk > 1 else 0, rider)
        _ride(rider, "start", at_step((0, 0, 0)), riding)
        _ride(rider, "middle", at_step(((3 * grid[0]) // 4, 0, 0)), riding)
        compute(a_ref, b_ref, o_ref, scratch)
        _ride(rider, "finish", at_step([g - 1 for g in grid]), riding)

    def compute(a_ref, b_ref, o_ref, scratch):
        p = _dot(a_ref[...], b_ref[...], dims)
        if nk == 1:
            o_ref[...] = p.astype(o_ref.dtype)
        else:
            acc = scratch[0]
            kk = pl.program_id(2)

            @pl.when(kk == 0)
            def _():
                acc[...] = p

            @pl.when(jnp.logical_and(kk > 0, kk < nk - 1))
            def _():
                acc[...] += p

            @pl.when(kk == nk - 1)
            def _():
                o_ref[...] = (acc[...] + p).astype(o_ref.dtype)

    if mode == "tn":
        a_spec = pl.BlockSpec((tk, tm), lambda i, j, kk: (kk, i))
    else:
        a_spec = pl.BlockSpec((tm, tk), lambda i, j, kk: (i, kk))
    squeezed = (None,) * len(b_lead)
    if mode == "nt":
        b_spec = pl.BlockSpec(squeezed + (tn, tk), lambda i, j, kk: (*b_lead, j, kk))
    else:
        b_spec = pl.BlockSpec(squeezed + (tk, tn), lambda i, j, kk: (*b_lead, kk, j))
    extra = _rider_call_args(rider, 2, 1)
    outs = pl.pallas_call(
        body, name=name, grid=grid,
        in_specs=[a_spec, b_spec] + extra["in_specs"],
        out_specs=[pl.BlockSpec((tm, tn), lambda i, j, kk: (i, j))] + extra["out_specs"],
        out_shape=[jax.ShapeDtypeStruct((m, n), out_dtype)] + extra["out_shape"],
        input_output_aliases=extra["aliases"],
        scratch_shapes=([pltpu.VMEM((tm, tn), F32)] if nk > 1 else []) + extra["scratch"],
        compiler_params=pltpu.CompilerParams(dimension_semantics=("arbitrary",) * 3 if rider else
                                             ("parallel", "parallel", "arbitrary")),
    )(a, b, *extra["inputs"])
    return (outs[0], outs[1:]) if rider else outs[0]


def _rms_fwd(x, g, name, ts=512):
    s, d = x.shape

    def body(x_ref, g_ref, h_ref):
        xv = x_ref[...]
        r = lax.rsqrt(jnp.mean(xv * xv, axis=-1, keepdims=True) + RMS_EPS)
        h_ref[...] = (xv * r * g_ref[...]).astype(BF16)

    return pl.pallas_call(
        body, name=name, grid=(s // ts,),
        in_specs=[pl.BlockSpec((ts, d), lambda i: (i, 0)), pl.BlockSpec((1, d), lambda i: (0, 0))],
        out_specs=pl.BlockSpec((ts, d), lambda i: (i, 0)),
        out_shape=jax.ShapeDtypeStruct((s, d), BF16),
    )(x, g)


def _rms_bwd(xin, g, dh, resid, out_dtype, name, ts=512):
    s, d = xin.shape
    has_resid = resid is not None

    def body(*refs):
        if has_resid:
            x_ref, g_ref, dh_ref, res_ref, dx_ref, dg_ref = refs
        else:
            x_ref, g_ref, dh_ref, dx_ref, dg_ref = refs
        xv = x_ref[...]
        dhv = dh_ref[...].astype(F32)
        r = lax.rsqrt(jnp.mean(xv * xv, axis=-1, keepdims=True) + RMS_EPS)
        nrm = xv * r
        dn = dhv * g_ref[...]
        dx = r * (dn - nrm * jnp.mean(dn * nrm, axis=-1, keepdims=True))
        if has_resid:
            dx = dx + res_ref[...]
        dx_ref[...] = dx.astype(dx_ref.dtype)
        part = jnp.sum(dhv * nrm, axis=0, keepdims=True)

        @pl.when(pl.program_id(0) == 0)
        def _():
            dg_ref[...] = part

        @pl.when(pl.program_id(0) > 0)
        def _():
            dg_ref[...] += part

    row = pl.BlockSpec((ts, d), lambda i: (i, 0))
    vec = pl.BlockSpec((1, d), lambda i: (0, 0))
    ins = [xin, g, dh] + ([resid] if has_resid else [])
    return pl.pallas_call(
        body, name=name, grid=(s // ts,),
        in_specs=[row, vec, row] + ([row] if has_resid else []),
        out_specs=[row, vec],
        out_shape=[jax.ShapeDtypeStruct((s, d), out_dtype), jax.ShapeDtypeStruct((1, d), F32)],
        compiler_params=pltpu.CompilerParams(dimension_semantics=("arbitrary",)),
    )(*ins)


def _resid_out(x, out, g, target, name, ts=512):
    s, d = x.shape
    has_loss = target is not None

    def body(*refs):
        if has_loss:
            x_ref, o_ref, g_ref, t_ref, dy_ref, loss_ref = refs
        else:
            x_ref, o_ref, g_ref, y_ref = refs
        ov = o_ref[...]
        r = lax.rsqrt(jnp.mean(ov * ov, axis=-1, keepdims=True) + RMS_EPS)
        yv = x_ref[...] + ov * r * g_ref[...]
        if not has_loss:
            y_ref[...] = yv
            return
        err = yv - t_ref[...]
        dy_ref[...] = err * (1.0 / d)
        part = jnp.sum(jnp.sum(err * err, axis=-1, keepdims=True), axis=0, keepdims=True) * (0.5 / d)
        part = jnp.broadcast_to(part, (1, LANE))

        @pl.when(pl.program_id(0) == 0)
        def _():
            loss_ref[...] = part

        @pl.when(pl.program_id(0) > 0)
        def _():
            loss_ref[...] += part

    row = pl.BlockSpec((ts, d), lambda i: (i, 0))
    vec = pl.BlockSpec((1, d), lambda i: (0, 0))
    if has_loss:
        return pl.pallas_call(
            body, name=name, grid=(s // ts,),
            in_specs=[row, row, vec, row],
            out_specs=[row, pl.BlockSpec((1, LANE), lambda i: (0, 0))],
            out_shape=[jax.ShapeDtypeStruct((s, d), F32), jax.ShapeDtypeStruct((1, LANE), F32)],
            compiler_params=pltpu.CompilerParams(dimension_semantics=("arbitrary",)),
        )(x, out, g, target)
    return pl.pallas_call(
        body, name=name, grid=(s // ts,),
        in_specs=[row, row, vec], out_specs=row,
        out_shape=jax.ShapeDtypeStruct((s, d), F32),
    )(x, out, g)


def _rows_before(ref, start, n, halo):
    if start == 0:
        return jnp.concatenate([jnp.zeros((halo, ref.shape[1]), F32), ref[0:n, :].astype(F32)], axis=0)
    return ref[start - halo:start + n, :].astype(F32)


def _rows_after(ref, start, n, halo):
    if start + n == ref.shape[0]:
        return jnp.concatenate([ref[start:start + n, :].astype(F32), jnp.zeros((halo, ref.shape[1]), F32)], axis=0)
    return ref[start:start + n + halo, :].astype(F32)


def _pick_window(group, s2, s4, s8, s16):
    return jnp.where(group == 0, s2, jnp.where(group == 1, s4, jnp.where(group == 2, s8, s16)))


def _trailing_sums(ext, group):
    s2 = ext + pltpu.roll(ext, 1, 0)
    s4 = s2 + pltpu.roll(s2, 2, 0)
    s8 = s4 + pltpu.roll(s4, 4, 0)
    s16 = s8 + pltpu.roll(s8, 8, 0)
    return _pick_window(group, s2, s4, s8, s16)


def _leading_sums(ext, group):
    n = ext.shape[0]
    s2 = ext + pltpu.roll(ext, n - 1, 0)
    s4 = s2 + pltpu.roll(s2, n - 2, 0)
    s8 = s4 + pltpu.roll(s4, n - 4, 0)
    s16 = s8 + pltpu.roll(s8, n - 8, 0)
    return _pick_window(group, s2, s4, s8, s16)


def _window_count(start, n, group):
    pos = start + lax.broadcasted_iota(jnp.int32, (n, LANE), 0)
    return jnp.minimum(pos + 1, 2 << group).astype(F32)


def _pooled(v_ref, start, n, group):
    ext = _rows_before(v_ref, start, n, POOL_HALO)
    sums = _trailing_sums(ext, group)[POOL_HALO:, :]
    return sums / _window_count(start, n, group) - ext[POOL_HALO:, :]


def _pool_fwd(u, pool_w, pool_scale, name, ts=512):
    s = u.shape[0]

    def body(v_ref, gate_ref, w_ref, sc_ref, y_ref):
        group = pl.program_id(0)
        for c in range(s // ts):
            a = c * ts
            pooled = _pooled(v_ref, a, ts, group)
            mixed = _dot(pooled.astype(BF16), w_ref[...], NN)
            gate = gate_ref[a:a + ts, :].astype(F32)
            y_ref[a:a + ts, :] = (mixed * sc_ref[...] * (gate * _sigmoid(gate))).astype(BF16)

    col = lambda base: pl.BlockSpec((s, LANE), lambda g: (0, base + g))
    return pl.pallas_call(
        body, name=name, grid=(4,),
        in_specs=[col(COL_POOL_V), col(COL_POOL_G),
                  pl.BlockSpec((None, LANE, LANE), lambda g: (g, 0, 0)),
                  pl.BlockSpec((1, LANE), lambda g: (0, g))],
        out_specs=pl.BlockSpec((s, LANE), lambda g: (0, g)),
        out_shape=jax.ShapeDtypeStruct((s, WIDTH), BF16),
    )(u, u, pool_w, pool_scale)


def _pool_bwd(u, dy, pool_w, pool_scale, name, ts=512):
    s = u.shape[0]

    def body(v_ref, gate_ref, dy_ref, w_ref, sc_ref, dv_ref, dgate_ref, dw_ref, dsc_ref):
        group = pl.program_id(0)
        w = w_ref[...]
        scale = sc_ref[...]
        dw = jnp.zeros((LANE, LANE), F32)
        dsc = jnp.zeros((1, LANE), F32)
        for c in range(s // ts):
            a = c * ts
            n_ext = ts + POOL_HALO
            gate_e = _rows_after(gate_ref, a, ts, POOL_HALO)
            dy_e = _rows_after(dy_ref, a, ts, POOL_HALO)
            silu_e, dsilu_e = _silu_and_grad(gate_e)
            dms_e = dy_e * silu_e
            dm_e = (dms_e * scale).astype(BF16)
            dpool_e = _dot(dm_e, w, NT)
            spread = _leading_sums(dpool_e / _window_count(a, n_ext, group), group)
            dv_ref[a:a + ts, :] = (spread[0:ts, :] - dpool_e[0:ts, :]).astype(BF16)
            pooled = _pooled(v_ref, a, ts, group).astype(BF16)
            mixed = _dot(pooled, w, NN)
            dgate_ref[a:a + ts, :] = (dy_e[0:ts, :] * mixed * scale * dsilu_e[0:ts, :]).astype(BF16)
            dsc = dsc + jnp.sum(dms_e[0:ts, :] * mixed, axis=0, keepdims=True)
            dw = dw + _dot(pooled, dm_e[0:ts, :], TN)
        dw_ref[...] = dw
        dsc_ref[...] = dsc

    col = lambda base: pl.BlockSpec((s, LANE), lambda g: (0, base + g))
    out_col = pl.BlockSpec((s, LANE), lambda g: (0, g))
    return pl.pallas_call(
        body, name=name, grid=(4,),
        in_specs=[col(COL_POOL_V), col(COL_POOL_G), out_col,
                  pl.BlockSpec((None, LANE, LANE), lambda g: (g, 0, 0)),
                  pl.BlockSpec((1, LANE), lambda g: (0, g))],
        out_specs=[out_col, out_col,
                   pl.BlockSpec((None, LANE, LANE), lambda g: (g, 0, 0)),
                   pl.BlockSpec((1, LANE), lambda g: (0, g))],
        out_shape=[jax.ShapeDtypeStruct((s, WIDTH), BF16), jax.ShapeDtypeStruct((s, WIDTH), BF16),
                   jax.ShapeDtypeStruct((4, LANE, LANE), F32), jax.ShapeDtypeStruct((1, WIDTH), F32)],
    )(u, u, dy, pool_w, pool_scale)


def _conv_taps(x_ref, gc_ref, start, n):
    z_ext = _rows_before(gc_ref, start, n, CONV_HALO) * _rows_before(x_ref, start, n, CONV_HALO)
    z0 = z_ext[CONV_HALO:, :]
    z1 = pltpu.roll(z_ext, 1, 0)[CONV_HALO:, :]
    z2 = pltpu.roll(z_ext, 2, 0)[CONV_HALO:, :]
    return z0, z1, z2


def _conv_fwd(u, conv_w, conv_b, name, ts=512):
    s = u.shape[0]

    def body(x_ref, gb_ref, gc_ref, g_ref, w_ref, b_ref, y_ref):
        w0, w1, w2 = w_ref[0:1, :], w_ref[1:2, :], w_ref[2:3, :]
        for c in range(s // ts):
            a = c * ts
            z0, z1, z2 = _conv_taps(x_ref, gc_ref, a, ts)
            y = w2 * z0 + w1 * z1 + w0 * z2 + b_ref[...]
            gate = g_ref[a:a + ts, :].astype(F32)
            y_ref[a:a + ts, :] = (gb_ref[a:a + ts, :].astype(F32) * y * (gate * _sigmoid(gate))).astype(BF16)

    col = lambda base: pl.BlockSpec((s, LANE), lambda j: (0, base + j))
    return pl.pallas_call(
        body, name=name, grid=(4,),
        in_specs=[col(COL_CONV_X), col(COL_CONV_GB), col(COL_CONV_GC), col(COL_CONV_G),
                  pl.BlockSpec((3, LANE), lambda j: (0, j)), pl.BlockSpec((1, LANE), lambda j: (0, j))],
        out_specs=pl.BlockSpec((s, LANE), lambda j: (0, j)),
        out_shape=jax.ShapeDtypeStruct((s, WIDTH), BF16),
    )(u, u, u, u, conv_w, conv_b)


def _conv_bwd(u, dy, conv_w, conv_b, name, ts=512):
    s = u.shape[0]

    def body(x_ref, gb_ref, gc_ref, g_ref, dy_ref, w_ref, b_ref,
             dx_ref, dgb_ref, dgc_ref, dg_ref, dw_ref, db_ref):
        w0, w1, w2 = w_ref[0:1, :], w_ref[1:2, :], w_ref[2:3, :]
        acc = [jnp.zeros((1, LANE), F32) for _ in range(4)]
        for c in range(s // ts):
            a = c * ts
            n_ext = ts + CONV_HALO
            gate_e = _rows_after(g_ref, a, ts, CONV_HALO)
            silu_e, dsilu_e = _silu_and_grad(gate_e)
            dy_e = _rows_after(dy_ref, a, ts, CONV_HALO)
            gb_e = _rows_after(gb_ref, a, ts, CONV_HALO)
            dyy_e = dy_e * silu_e * gb_e
            dz = (w2 * dyy_e + w1 * pltpu.roll(dyy_e, n_ext - 1, 0) + w0 * pltpu.roll(dyy_e, n_ext - 2, 0))[0:ts, :]
            z0, z1, z2 = _conv_taps(x_ref, gc_ref, a, ts)
            yb = w2 * z0 + w1 * z1 + w0 * z2 + b_ref[...]
            dyv = dy_e[0:ts, :]
            dyy = dyy_e[0:ts, :]
            dg_ref[a:a + ts, :] = (dyv * gb_e[0:ts, :] * yb * dsilu_e[0:ts, :]).astype(BF16)
            dgb_ref[a:a + ts, :] = (dyv * silu_e[0:ts, :] * yb).astype(BF16)
            dx_ref[a:a + ts, :] = (dz * gc_ref[a:a + ts, :].astype(F32)).astype(BF16)
            dgc_ref[a:a + ts, :] = (dz * x_ref[a:a + ts, :].astype(F32)).astype(BF16)
            for i, term in enumerate((dyy * z2, dyy * z1, dyy * z0, dyy)):
                acc[i] = acc[i] + jnp.sum(term, axis=0, keepdims=True)
        dw_ref[0:1, :] = acc[0]
        dw_ref[1:2, :] = acc[1]
        dw_ref[2:3, :] = acc[2]
        db_ref[...] = acc[3]

    col = lambda base: pl.BlockSpec((s, LANE), lambda j: (0, base + j))
    out_col = pl.BlockSpec((s, LANE), lambda j: (0, j))
    big = jax.ShapeDtypeStruct((s, WIDTH), BF16)
    return pl.pallas_call(
        body, name=name, grid=(4,),
        in_specs=[col(COL_CONV_X), col(COL_CONV_GB), col(COL_CONV_GC), col(COL_CONV_G), out_col,
                  pl.BlockSpec((3, LANE), lambda j: (0, j)), pl.BlockSpec((1, LANE), lambda j: (0, j))],
        out_specs=[out_col, out_col, out_col, out_col,
                   pl.BlockSpec((3, LANE), lambda j: (0, j)), pl.BlockSpec((1, LANE), lambda j: (0, j))],
        out_shape=[big, big, big, big,
                   jax.ShapeDtypeStruct((3, WIDTH), F32), jax.ShapeDtypeStruct((1, WIDTH), F32)],
    )(u, u, u, u, dy, conv_w, conv_b)


LOG2_E = 1.4426950408889634
LN_2 = 0.6931471805599453


def _sb_scores(q_h, k_blk, valid, later_mat, carry):
    z = _dot(q_h, k_blk, NT)
    neg_z = -z
    soft = jnp.log(1.0 + jnp.exp2(jnp.minimum(z, neg_z))) * LOG2_E
    log_keep = jnp.minimum(neg_z, 0.0) - soft
    log_beta = log_keep + z
    if valid is not None:
        log_keep = jnp.where(valid, log_keep, 0.0)
    later = _dot(log_keep.astype(BF16), later_mat, NN) + carry
    return log_keep, log_beta, later


def _masked(valid, x):
    return x if valid is None else jnp.where(valid, x, 0.0)


def _diagonal_masks(tq, tk):
    r = lax.broadcasted_iota(jnp.int32, (tq, tk), 0)
    cidx = lax.broadcasted_iota(jnp.int32, (tq, tk), 1)
    return [cidx + d * tk < r for d in range(tq // tk)]


def _triangle(tk, op):
    r = lax.broadcasted_iota(jnp.int32, (tk, tk), 0)
    cidx = lax.broadcasted_iota(jnp.int32, (tk, tk), 1)
    return op(r, cidx).astype(BF16)


def _split_refs(refs, n_in, n_out, n_scratch, rider):
    r_in = len(rider.inputs) if rider else 0
    r_out = len(rider.out_shape) if rider else 0
    a, b = n_in + r_in, n_in + r_in + n_out + r_out
    own = refs[:n_in] + refs[a:a + n_out] + refs[b:b + n_scratch]
    return own, (refs[n_in:a], refs[a + n_out:b], refs[b + n_scratch:])


def _rider_call_args(rider, n_in, n_out):
    if rider is None:
        return dict(in_specs=[], out_specs=[], out_shape=[], aliases={}, scratch=[], inputs=[])
    return dict(in_specs=[ANY] * len(rider.inputs), out_specs=[ANY] * len(rider.out_shape),
                out_shape=list(rider.out_shape), scratch=list(rider.scratch), inputs=list(rider.inputs),
                aliases={n_in + a: n_out + b for a, b in rider.aliases.items()})


def _ride(rider, phase, when, parts):
    fn = getattr(rider, phase) if rider else None
    if fn is not None:
        pl.when(when)(lambda: fn(*parts))


def _sb_fwd(u, name, t=512, tk=256, pairs=4, rider=None):
    s = u.shape[0]
    assert s // tk <= LANE and 4 % pairs == 0 and t % tk == 0
    scale = HEAD_DIM ** -0.5
    nh = 2 * pairs
    wide = pairs * LANE
    ratio = t // tk
    groups, nq = 4 // pairs, s // t

    def body(*refs):
        own, riding = _split_refs(refs, 4, 3, 4, rider)
        q_ref, k_ref, v_ref, g_ref, o_ref, y_ref, after_ref, kb_ref, vb_ref, acc_ref, carry_ref = own
        grp = pl.program_id(0)
        i = pl.program_id(1)
        _ride(rider, "start", jnp.logical_and(grp == 0, i == 0), riding)
        _ride(rider, "middle", jnp.logical_and(grp == groups - 1, i == (3 * nq) // 4), riding)

        @pl.when(i == 0)
        def _():
            kb_ref[...] = k_ref[...].astype(BF16)
            vb_ref[...] = v_ref[...].astype(BF16)

        lane = lax.broadcasted_iota(jnp.int32, (t, LANE), 1)
        first = lane < HEAD_DIM
        after_ref[...] = jnp.zeros_like(after_ref)
        qv = q_ref[...].astype(F32) * (scale * LOG2_E)
        q_heads = []
        for p in range(pairs):
            qp = qv[:, p * LANE:(p + 1) * LANE]
            q_heads += [jnp.where(first, qp, 0.0).astype(BF16), jnp.where(first, 0.0, qp).astype(BF16)]
        later_mat = _triangle(tk, lambda r, cidx: r > cidx)
        acc_ref[...] = jnp.zeros_like(acc_ref)
        carry_ref[...] = jnp.zeros_like(carry_ref)

        def block(kb, valid, lo=0):
            rows = pl.ds(pl.multiple_of(kb * tk, tk), tk)
            k_blk = kb_ref[rows, :]
            v_blk = vb_ref[rows, :]
            carries = [carry_ref[h, lo:, :] for h in range(nh)]
            afters = [after_ref[lo:, h * LANE:(h + 1) * LANE] for h in range(nh)]
            accs = [acc_ref[h, lo:, :] for h in range(nh)]
            outs = []
            for h in range(nh):
                cols = slice((h // 2) * LANE, (h // 2 + 1) * LANE)
                log_keep, log_beta, later = _sb_scores(q_heads[h][lo:], k_blk[:, cols], valid, later_mat, carries[h])
                a = _masked(valid, jnp.exp2(log_beta + later))
                outs.append((accs[h] + _dot(a.astype(BF16), v_blk[:, cols], NN),
                             carries[h] + jnp.sum(log_keep, axis=1, keepdims=True),
                             jnp.where(lane[lo:] == kb, carries[h], afters[h])))
            for h in range(nh):
                acc_ref[h, lo:, :] = outs[h][0]
                carry_ref[h, lo:, :] = outs[h][1]
                after_ref[lo:, h * LANE:(h + 1) * LANE] = outs[h][2]

        def step(j, _):
            block(ratio * i - 1 - j, None)
            return 0

        masks = _diagonal_masks(t, tk)
        for d in reversed(range(ratio)):
            block(ratio * i + d, masks[d][d * tk:], d * tk)
        lax.fori_loop(0, ratio * i, step, 0)
        for p in range(pairs):
            cols = slice(p * LANE, (p + 1) * LANE)
            o = jnp.where(first, acc_ref[2 * p], acc_ref[2 * p + 1])
            o_ref[:, cols] = o
            gate = g_ref[:, cols].astype(F32)
            y_ref[:, cols] = (o * gate * _sigmoid(gate)).astype(BF16)
        _ride(rider, "finish", jnp.logical_and(grp == groups - 1, i == nq - 1), riding)

    blk = lambda base: pl.BlockSpec((t, wide), lambda g, i: (i, base // pairs + g))
    full = lambda base: pl.BlockSpec((s, wide), lambda g, i: (0, base // pairs + g))
    out_blk = pl.BlockSpec((t, wide), lambda g, i: (i, g))
    extra = _rider_call_args(rider, 4, 3)
    outs = pl.pallas_call(
        body, name=name, grid=(groups, nq),
        in_specs=[blk(COL_SB_Q), full(COL_SB_K), full(COL_SB_V), blk(COL_SB_G)] + extra["in_specs"],
        out_specs=[out_blk, out_blk, pl.BlockSpec((t, nh * LANE), lambda g, i: (i, g))] + extra["out_specs"],
        out_shape=[jax.ShapeDtypeStruct((s, WIDTH), F32), jax.ShapeDtypeStruct((s, WIDTH), BF16),
                   jax.ShapeDtypeStruct((s, 8 * LANE), F32)] + extra["out_shape"],
        input_output_aliases=extra["aliases"],
        scratch_shapes=[pltpu.VMEM((s, wide), BF16), pltpu.VMEM((s, wide), BF16),
                        pltpu.VMEM((nh, t, LANE), F32), pltpu.VMEM((nh, t, 1), F32)] + extra["scratch"],
        compiler_params=pltpu.CompilerParams(dimension_semantics=("arbitrary", "arbitrary")),
    )(u, u, u, u, *extra["inputs"])
    return outs[:3], outs[3:]


def _sb_bwd(u, o, after, dy, name, t=512, tk=256, pairs=2, rider=None):
    s = u.shape[0]
    nq = s // t
    scale = HEAD_DIM ** -0.5
    nh = 2 * pairs
    wide = pairs * LANE
    ratio = t // tk
    groups = 4 // pairs

    def body(*refs):
        own, riding = _split_refs(refs, 7, 4, 6, rider)
        (q_ref, k_ref, v_ref, g_ref, o_ref, after_ref, dy_ref, dq_ref, dk_ref, dv_ref, dg_ref,
         kb_ref, vb_ref, dk_acc, dv_acc, dq_acc, carry_ref) = own
        grp = pl.program_id(0)
        i = pl.program_id(1)
        _ride(rider, "start", jnp.logical_and(grp == 0, i == 0), riding)

        @pl.when(i == 0)
        def _():
            kb_ref[...] = k_ref[...].astype(BF16)
            vb_ref[...] = v_ref[...].astype(BF16)
            dk_acc[...] = jnp.zeros_like(dk_acc)
            dv_acc[...] = jnp.zeros_like(dv_acc)

        lane = lax.broadcasted_iota(jnp.int32, (t, LANE), 1)
        first = lane < HEAD_DIM
        gate = g_ref[...].astype(F32)
        silu, dsilu = _silu_and_grad(gate)
        dyv = dy_ref[...]
        do = dyv * silu
        dg_ref[...] = (dyv * o_ref[...] * dsilu).astype(BF16)
        qv = q_ref[...].astype(F32) * (scale * LOG2_E)
        do_heads, q_heads = [], []
        for p in range(pairs):
            cols = slice(p * LANE, (p + 1) * LANE)
            do_heads += [jnp.where(first, do[:, cols], 0.0).astype(BF16), jnp.where(first, 0.0, do[:, cols]).astype(BF16)]
            q_heads += [jnp.where(first, qv[:, cols], 0.0).astype(BF16), jnp.where(first, 0.0, qv[:, cols]).astype(BF16)]
        later_mat = _triangle(tk, lambda r, cidx: r > cidx)
        before_mat = _triangle(tk, lambda r, cidx: r < cidx)
        dq_acc[...] = jnp.zeros_like(dq_acc)
        carry_ref[...] = jnp.zeros_like(carry_ref)

        def block(kb, valid, lo=0):
            rows = pl.ds(pl.multiple_of(kb * tk, tk), tk)
            k_blk = kb_ref[rows, :]
            v_blk = vb_ref[rows, :]
            carries = [carry_ref[h, lo:, :] for h in range(nh)]
            dq_old = [dq_acc[h, lo:, :] for h in range(nh)]
            dk_old = dk_acc[rows, :]
            dv_old = dv_acc[rows, :]
            outs = []
            for h in range(nh):
                cols = slice((h // 2) * LANE, (h // 2 + 1) * LANE)
                q_h, do_h = q_heads[h][lo:], do_heads[h][lo:]
                after = jnp.sum(jnp.where(lane[lo:] == kb, after_ref[lo:, h * LANE:(h + 1) * LANE], 0.0), axis=1,
                                keepdims=True)
                _, log_beta, later = _sb_scores(q_h, k_blk[:, cols], valid, later_mat, after)
                beta = jnp.exp2(log_beta)
                a = _masked(valid, jnp.exp2(log_beta + later))
                da = _dot(do_h, v_blk[:, cols], NT)
                gterm = a * da
                before = _dot(gterm.astype(BF16), before_mat, NN) + carries[h]
                dz_b = _masked(valid, gterm * (1.0 - beta) - beta * before).astype(BF16)
                outs.append((dq_old[h] + _dot(dz_b, k_blk[:, cols], NN), _dot(dz_b, q_h, TN),
                             _dot(a.astype(BF16), do_h, TN),
                             carries[h] + jnp.sum(gterm, axis=1, keepdims=True)))
            for h in range(nh):
                dq_acc[h, lo:, :] = outs[h][0]
                carry_ref[h, lo:, :] = outs[h][3]
            dk_new = [outs[2 * p][1] + outs[2 * p + 1][1] for p in range(pairs)]
            dv_new = [outs[2 * p][2] + outs[2 * p + 1][2] for p in range(pairs)]
            dk_acc[rows, :] = dk_old + (dk_new[0] if pairs == 1 else jnp.concatenate(dk_new, axis=1))
            dv_acc[rows, :] = dv_old + (dv_new[0] if pairs == 1 else jnp.concatenate(dv_new, axis=1))

        def step(kb, _):
            block(kb, None)
            return 0

        lax.fori_loop(0, ratio * i, step, 0)
        masks = _diagonal_masks(t, tk)
        for d in range(ratio):
            block(ratio * i + d, masks[d][d * tk:], d * tk)
        for p in range(pairs):
            dq_ref[:, p * LANE:(p + 1) * LANE] = (jnp.where(first, dq_acc[2 * p], dq_acc[2 * p + 1]) * scale).astype(BF16)

        @pl.when(i == nq - 1)
        def _():
            dk_ref[...] = (dk_acc[...] * LN_2).astype(BF16)
            dv_ref[...] = dv_acc[...].astype(BF16)

        _ride(rider, "finish", jnp.logical_and(grp == groups - 1, i == nq - 1), riding)

    blk = lambda base: pl.BlockSpec((t, wide), lambda g, i: (i, base // pairs + g))
    full = lambda base: pl.BlockSpec((s, wide), lambda g, i: (0, base // pairs + g))
    out_blk = pl.BlockSpec((t, wide), lambda g, i: (i, g))
    out_full = pl.BlockSpec((s, wide), lambda g, i: (0, g))
    big = jax.ShapeDtypeStruct((s, WIDTH), BF16)
    extra = _rider_call_args(rider, 7, 4)
    outs = pl.pallas_call(
        body, name=name, grid=(groups, nq),
        in_specs=[blk(COL_SB_Q), full(COL_SB_K), full(COL_SB_V), blk(COL_SB_G), out_blk,
                  pl.BlockSpec((t, nh * LANE), lambda g, i: (i, g)), out_blk] + extra["in_specs"],
        out_specs=[out_blk, out_full, out_full, out_blk] + extra["out_specs"],
        out_shape=[big, big, big, big] + extra["out_shape"],
        input_output_aliases=extra["aliases"],
        scratch_shapes=[pltpu.VMEM((s, wide), BF16), pltpu.VMEM((s, wide), BF16),
                        pltpu.VMEM((s, wide), F32), pltpu.VMEM((s, wide), F32),
                        pltpu.VMEM((nh, t, LANE), F32), pltpu.VMEM((nh, t, 1), F32)] + extra["scratch"],
        compiler_params=pltpu.CompilerParams(dimension_semantics=("arbitrary", "arbitrary")),
    )(u, u, u, u, o, after, dy, *extra["inputs"])
    return outs[:4], outs[4:]


def _gate_fwd(u, ys, w_branch, name, ts=256):
    s = u.shape[0]

    def body(m0, m1, m2, y0, y1, y2, w_ref, p0, p1, p2, out_ref):
        tot = None
        for n, (m_ref, y_ref, p_ref) in enumerate(((m0, y0, p0), (m1, y1, p1), (m2, y2, p2))):
            proj = _dot(y_ref[...], w_ref[n], NN)
            p_ref[...] = proj.astype(BF16)
            term = _sigmoid(m_ref[...].astype(F32)) * proj
            tot = term if tot is None else tot + term
        out_ref[...] = tot.astype(BF16)

    mspec = lambda n: pl.BlockSpec((ts, D_MODEL), lambda i: (i, COL_MERGE_1024 + n))
    row = pl.BlockSpec((ts, D_MODEL), lambda i: (i, 0))
    yspec = pl.BlockSpec((ts, WIDTH), lambda i: (i, 0))
    big = jax.ShapeDtypeStruct((s, D_MODEL), BF16)
    outs = pl.pallas_call(
        body, name=name, grid=(s // ts,),
        in_specs=[mspec(0), mspec(1), mspec(2), yspec, yspec, yspec,
                  pl.BlockSpec(w_branch.shape, lambda i: (0, 0, 0))],
        out_specs=[row] * 4, out_shape=[big] * 4,
    )(u, u, u, *ys, w_branch)
    return outs[:3], outs[3]


def _gate_bwd(u, projs, dmerged, w_branch, name, ts=256, rider=None):
    s = u.shape[0]
    steps = s // ts

    def body(*refs):
        own, riding = _split_refs(refs, 8, 9, 0, rider)
        m0, m1, m2, p0, p1, p2, dm_ref, w_ref, dp0, dp1, dp2, dl0, dl1, dl2, dy0, dy1, dy2 = own
        _ride(rider, "start", pl.program_id(0) == 0, riding)
        dm = dm_ref[...].astype(F32)
        for n, (m_ref, p_ref, dp_ref, dl_ref, dy_ref) in enumerate(((m0, p0, dp0, dl0, dy0), (m1, p1, dp1, dl1, dy1),
                                                                    (m2, p2, dp2, dl2, dy2))):
            gate = _sigmoid(m_ref[...].astype(F32))
            dp = (dm * gate).astype(BF16)
            dp_ref[...] = dp
            dl_ref[...] = (dm * p_ref[...].astype(F32) * gate * (1.0 - gate)).astype(BF16)
            dy_ref[...] = _dot(dp, w_ref[n], NT)
        _ride(rider, "finish", pl.program_id(0) == steps - 1, riding)

    mspec = lambda n: pl.BlockSpec((ts, D_MODEL), lambda i: (i, COL_MERGE_1024 + n))
    row = pl.BlockSpec((ts, D_MODEL), lambda i: (i, 0))
    yspec = pl.BlockSpec((ts, WIDTH), lambda i: (i, 0))
    big = jax.ShapeDtypeStruct((s, D_MODEL), BF16)
    extra = _rider_call_args(rider, 8, 9)
    outs = pl.pallas_call(
        body, name=name, grid=(steps,),
        in_specs=[mspec(0), mspec(1), mspec(2), row, row, row, row,
                  pl.BlockSpec(w_branch.shape, lambda i: (0, 0, 0))] + extra["in_specs"],
        out_specs=[row] * 6 + [yspec] * 3 + extra["out_specs"],
        out_shape=[big] * 6 + [jax.ShapeDtypeStruct((s, WIDTH), F32)] * 3 + extra["out_shape"],
        input_output_aliases=extra["aliases"], scratch_shapes=extra["scratch"],
        compiler_params=pltpu.CompilerParams(dimension_semantics=("arbitrary",)),
    )(u, u, u, *projs, dmerged, w_branch, *extra["inputs"])
    return outs[:3], outs[3:6], outs[6:9], outs[9:]


def _as_rows(a):
    return a.reshape(-1, a.shape[-1])


def _row_tile(rows, cols, bytes_per_row_elem=4, cap=1 << 20):
    tr = rows
    while tr * cols * bytes_per_row_elem > cap and tr % 2 == 0 and (tr // 2) % 16 == 0:
        tr //= 2
    return tr


def _cast_bf16(a, name):
    a2 = _as_rows(a)
    rows, cols = a2.shape
    tr = _row_tile(rows, cols)

    def body(a_ref, o_ref):
        o_ref[...] = a_ref[...].astype(BF16)

    spec = pl.BlockSpec((tr, cols), lambda i: (i, 0))
    out = pl.pallas_call(body, name=name, grid=(rows // tr,), in_specs=[spec], out_specs=spec,
                         out_shape=jax.ShapeDtypeStruct((rows, cols), BF16))(a2)
    return out.reshape(a.shape)


def _adamw(w, g, m, v, name):
    shape = w.shape
    w2, g2, m2, v2 = (_as_rows(a) for a in (w, g, m, v))
    rows, cols = w2.shape
    tr = _row_tile(rows, cols)
    c1 = 1.0 - ADAM_B1 ** ADAM_STEP
    c2 = 1.0 - ADAM_B2 ** ADAM_STEP

    def body(w_ref, g_ref, m_ref, v_ref, go_ref, d_ref, nm_ref, nv_ref):
        gv = g_ref[...]
        go_ref[...] = gv
        nm = ADAM_B1 * m_ref[...] + (1.0 - ADAM_B1) * gv
        nv = ADAM_B2 * v_ref[...] + (1.0 - ADAM_B2) * (gv * gv)
        nm_ref[...] = nm
        nv_ref[...] = nv
        d_ref[...] = -ADAM_LR * ((nm / c1) / (jnp.sqrt(nv / c2) + ADAM_EPS) + ADAM_WD * w_ref[...])

    spec = pl.BlockSpec((tr, cols), lambda i: (i, 0))
    sds = jax.ShapeDtypeStruct((rows, cols), F32)
    outs = pl.pallas_call(body, name=name, grid=(rows // tr,), in_specs=[spec] * 4, out_specs=[spec] * 4,
                          out_shape=[sds] * 4)(w2, g2, m2, v2)
    return tuple(o.reshape(shape) for o in outs)


def _sum_slots(a, out_dtype, name):
    n = a.shape[0]
    a3 = a.reshape(n, -1, a.shape[-1])
    _, rows, cols = a3.shape
    tr = _row_tile(rows, cols * n)

    def body(a_ref, o_ref):
        tot = a_ref[0].astype(F32)
        for k in range(1, n):
            tot = tot + a_ref[k].astype(F32)
        o_ref[...] = tot.astype(out_dtype)

    out = pl.pallas_call(
        body, name=name, grid=(rows // tr,),
        in_specs=[pl.BlockSpec((n, tr, cols), lambda i: (0, i, 0))],
        out_specs=pl.BlockSpec((tr, cols), lambda i: (i, 0)),
        out_shape=jax.ShapeDtypeStruct((rows, cols), out_dtype))(a3)
    return out.reshape(a.shape[1:])


def _chip_sum(own, recv, axis, core, name):
    half = recv.shape
    nd = len(half)
    last = nd - 1
    if axis == last:
        tl, nt = half[last], 1
    else:
        tl = min(half[last], 2048)
        nt = half[last] // tl
    block = half[:last] + (tl,)

    def own_index(i, core_ref):
        idx = [0] * nd
        idx[last] = i
        if axis == last:
            idx[last] = core_ref[0]
        else:
            idx[axis] = core_ref[0]
        return tuple(idx)

    def recv_index(i, core_ref):
        idx = [0] * nd
        idx[last] = i
        return tuple(idx)

    def body(core_ref, own_ref, recv_ref, o_ref):
        o_ref[...] = (own_ref[...].astype(F32) + recv_ref[...].astype(F32)).astype(BF16)

    return pl.pallas_call(
        body, name=name,
        grid_spec=pltpu.PrefetchScalarGridSpec(
            num_scalar_prefetch=1, grid=(nt,),
            in_specs=[pl.BlockSpec(block, own_index), pl.BlockSpec(block, recv_index)],
            out_specs=pl.BlockSpec(block, recv_index)),
        out_shape=jax.ShapeDtypeStruct(half, BF16),
    )(core, own, recv)


def _mesh_position():
    return lax.axis_index("x"), lax.axis_index("y"), lax.axis_index("c")


def _other_chips(x, y):
    return [(1 - x, y), (x, 1 - y), (1 - x, 1 - y)]


ALL_FLIPS = [(0, 0, 1), (1, 0, 0), (0, 1, 0), (1, 1, 0), (1, 0, 1), (0, 1, 1), (1, 1, 1)]


def _half(ref, axis, which, size):
    idx = [slice(None)] * len(ref.shape)
    idx[axis] = pl.ds(which * size, size)
    return ref.at[tuple(idx)]


def _sub(ref, picks):
    idx = [slice(None)] * len(ref.shape)
    for axis, start, size in picks:
        idx[axis] = pl.ds(start, size)
    return ref.at[tuple(idx)]


def _remote(src, dst, sems_send, sems_recv, k, to):
    return pltpu.make_async_remote_copy(src_ref=src, dst_ref=dst, send_sem=sems_send.at[k], recv_sem=sems_recv.at[k],
                                        device_id=to, device_id_type=MESH)


def _cast_shard(w, layer, shard_axis, pos, name, tr=512):
    shape = w.shape[1:]
    nd = len(shape)
    assert shard_axis in (nd - 1, nd - 2)
    rows, cols = shape[-2:]
    tr = min(tr, rows)
    nt = rows // tr
    lead = shape[:-2]
    full = list(shape)
    full[shard_axis] *= N_CHIPS
    block = (1,) * len(lead) + (tr, cols)

    def in_index(*args):
        return (layer, *args[:-1], 0)

    def out_index(*args):
        *g, pos_ref = args
        if shard_axis == nd - 1:
            return (*g, pos_ref[1])
        return (*g[:-1], pos_ref[1] * nt + g[-1], 0)

    def body(pos_ref, a_ref, o_ref):
        o_ref[...] = a_ref[...].astype(BF16)

    return pl.pallas_call(
        body, name=name,
        grid_spec=pltpu.PrefetchScalarGridSpec(
            num_scalar_prefetch=1, grid=lead + (nt,),
            in_specs=[pl.BlockSpec((None,) + block, in_index)], out_specs=pl.BlockSpec(block, out_index)),
        out_shape=jax.ShapeDtypeStruct(tuple(full), BF16),
    )(pos, w)


class _Rider:
    def __init__(self, inputs, out_shape, aliases, scratch, start, middle, finish):
        self.inputs, self.out_shape, self.aliases, self.scratch = inputs, out_shape, aliases, scratch
        self.start, self.middle, self.finish = start, middle, finish


def _weight_gather_rider(fulls, layout):
    n = len(fulls)

    def copies(outs, sems):
        send_sems, recv_sems = sems
        x, y, c = _mesh_position()
        chips = _other_chips(x, y)
        sibling = (x, y, 1 - c)
        mine = 2 * x + y

        def place(t, chip, core):
            sh_axis, sh_size, half_axis, half_size, *part = layout[t]
            offset, size = part if part else (0, sh_size)
            return _sub(outs[t], [(sh_axis, chip * sh_size + offset, size), (half_axis, core * half_size, half_size)])

        direct, arrive, forward, arrive_fwd = [], [], [], []
        for t in range(n):
            for k, (px, py) in enumerate(chips):
                theirs = 2 * px + py
                direct.append(_remote(place(t, mine, c), place(t, mine, c), send_sems, recv_sems, 6 * t + k, (px, py, c)))
                arrive.append(_remote(place(t, theirs, c), place(t, theirs, c), send_sems, recv_sems, 6 * t + k, (px, py, c)))
                forward.append(_remote(place(t, theirs, c), place(t, theirs, c), send_sems, recv_sems, 6 * t + 3 + k, sibling))
                arrive_fwd.append(_remote(place(t, theirs, 1 - c), place(t, theirs, 1 - c), send_sems, recv_sems,
                                          6 * t + 3 + k, sibling))
        return direct, arrive, forward, arrive_fwd

    def start(ins, outs, sems):
        for cp in copies(outs, sems)[0]:
            cp.start()

    def middle(ins, outs, sems):
        _, arrive, forward, _ = copies(outs, sems)
        for a, f in zip(arrive, forward):
            a.wait_recv()
            f.start()

    def finish(ins, outs, sems):
        direct, _, forward, arrive_fwd = copies(outs, sems)
        for cp in arrive_fwd:
            cp.wait_recv()
        for cp in direct + forward:
            cp.wait_send()

    rider = _Rider(list(fulls), [jax.ShapeDtypeStruct(a.shape, a.dtype) for a in fulls], {k: k for k in range(n)},
                   [pltpu.SemaphoreType.DMA((6 * n,)), pltpu.SemaphoreType.DMA((6 * n,))], start, middle, finish)
    rider.copies = copies
    return rider


WEIGHT_LAYOUT = [(1, 2048, 0, 512), (2, 256, 1, 256), (0, 256, 1, 512)]


def _conv_w_rider(conv_w):
    def copies(ins, outs, sems):
        x, y, c = _mesh_position()
        mine = 2 * x + y
        local = pltpu.make_async_copy(ins[0], outs[0].at[mine], sems[2].at[0])
        send = [_remote(ins[0], outs[0].at[mine], sems[0], sems[1], k, (px, py, c))
                for k, (px, py) in enumerate(_other_chips(x, y))]
        arrive = [_remote(ins[0], outs[0].at[2 * px + py], sems[0], sems[1], k, (px, py, c))
                  for k, (px, py) in enumerate(_other_chips(x, y))]
        return local, send, arrive

    def start(ins, outs, sems):
        local, send, _ = copies(ins, outs, sems)
        local.start()
        for cp in send:
            cp.start()

    def finish(ins, outs, sems):
        local, send, arrive = copies(ins, outs, sems)
        for cp in arrive:
            cp.wait_recv()
        for cp in send:
            cp.wait_send()
        local.wait()

    return _Rider([conv_w], [jax.ShapeDtypeStruct((N_CHIPS,) + conv_w.shape, F32)], {},
                  [pltpu.SemaphoreType.DMA((3,)), pltpu.SemaphoreType.DMA((3,)), pltpu.SemaphoreType.DMA((1,))],
                  start, None, finish)


def _join_riders(first, second):
    ni, no, ns = len(first.inputs), len(first.out_shape), len(first.scratch)

    def phase(name):
        fns = [getattr(first, name), getattr(second, name)]
        if fns[0] is None and fns[1] is None:
            return None

        def run(ins, outs, sems):
            if fns[0] is not None:
                fns[0](ins[:ni], outs[:no], sems[:ns])
            if fns[1] is not None:
                fns[1](ins[ni:], outs[no:], sems[ns:])
        return run

    aliases = dict(first.aliases)
    aliases.update({ni + a: no + b for a, b in second.aliases.items()})
    return _Rider(first.inputs + second.inputs, first.out_shape + second.out_shape, aliases,
                  first.scratch + second.scratch, phase("start"), phase("middle"), phase("finish"))


def _in_proj_streamed(h, w_in, order, name, rider=None, tm=1024, tn=1024):
    m, k = h.shape
    n = w_in.shape[1]
    tm = min(tm, m)
    nj, ni = n // tn, m // tm
    second = min(1, ni - 1)
    per_chip = WEIGHT_LAYOUT[0][1] // tn
    gather = _weight_gather_rider([w_in] * per_chip, [WEIGHT_LAYOUT[0] + (t * tn, tn) for t in range(per_chip)])

    def body(order_ref, *refs):
        own, riding = _split_refs(refs, 2, 2, 4, rider)
        h_ref, _, u_ref, w_ref, wbuf, tile_sems, send_sems, recv_sems = own
        j, i = pl.program_id(0), pl.program_id(1)
        first = jnp.logical_and(j == 0, i == 0)
        last = jnp.logical_and(j == nj - 1, i == ni - 1)
        direct, arrive, forward, arrive_fwd = gather.copies([w_ref] * per_chip, (send_sems, recv_sems))

        def tile_copy(jj, slot):
            cols = pl.ds(pl.multiple_of(order_ref[jj] * tn, tn), tn)
            return pltpu.make_async_copy(w_ref.at[:, cols], wbuf.at[slot], tile_sems.at[slot])

        @pl.when(first)
        def _():
            for cp in direct:
                cp.start()
            tile_copy(0, 0).start()

        _ride(rider, "start", first, riding)
        for kk in range(len(direct)):
            before = per_chip + kk - 1
            others = N_CHIPS - 1
            ahead = per_chip + (kk // others) * others - 1

            @pl.when(jnp.logical_and(j == ahead, i == 0))
            def _(kk=kk):
                arrive[kk].wait_recv()
                forward[kk].start()

            @pl.when(jnp.logical_and(j == before, i == second))
            def _(kk=kk):
                arrive_fwd[kk].wait_recv()

        @pl.when(i == 0)
        def _():
            tile_copy(j, j % 2).wait()

        @pl.when(jnp.logical_and(i == second, j + 1 < nj))
        def _():
            tile_copy(j + 1, (j + 1) % 2).start()

        _ride(rider, "middle", jnp.logical_and(j == (3 * nj) // 4, i == 0), riding)
        u_ref[...] = _dot(h_ref[...], wbuf[j % 2], NN).astype(BF16)

        @pl.when(last)
        def _():
            for cp in direct + forward:
                cp.wait_send()

        _ride(rider, "finish", last, riding)

    extra = _rider_call_args(rider, 3, 2)
    aliases = {2: 1}
    aliases.update(extra["aliases"])
    outs = pl.pallas_call(
        body, name=name,
        grid_spec=pltpu.PrefetchScalarGridSpec(
            num_scalar_prefetch=1, grid=(nj, ni),
            in_specs=[pl.BlockSpec((tm, k), lambda j, i, order_ref: (i, 0)), ANY] + extra["in_specs"],
            out_specs=[pl.BlockSpec((tm, tn), lambda j, i, order_ref: (i, order_ref[j])), ANY] + extra["out_specs"],
            scratch_shapes=[pltpu.VMEM((2, k, tn), BF16), pltpu.SemaphoreType.DMA((2,))] + gather.scratch
            + extra["scratch"]),
        out_shape=[jax.ShapeDtypeStruct((m, n), BF16), jax.ShapeDtypeStruct(w_in.shape, BF16)] + extra["out_shape"],
        input_output_aliases=aliases,
        compiler_params=pltpu.CompilerParams(dimension_semantics=("arbitrary", "arbitrary")),
    )(order, h, w_in, *extra["inputs"])
    return outs[0], outs[1], outs[2:]


def _gather_weights(fulls, layout, conv_w):
    rider = _weight_gather_rider(fulls, layout)
    n = len(fulls)

    def body(*refs):
        cw, outs, cw_f = refs[n], refs[n + 1:2 * n + 1], refs[2 * n + 1]
        sems, (cw_send, cw_recv, local_sem) = refs[2 * n + 2:2 * n + 4], refs[2 * n + 4:]
        x, y, c = _mesh_position()
        chips = _other_chips(x, y)
        mine = 2 * x + y
        local = pltpu.make_async_copy(cw, cw_f.at[mine], local_sem.at[0])
        local.start()
        rider.start(None, outs, sems)
        small = [_remote(cw, cw_f.at[mine], cw_send, cw_recv, k, (*chip, c)) for k, chip in enumerate(chips)]
        for cp in small:
            cp.start()
        rider.middle(None, outs, sems)
        rider.finish(None, outs, sems)
        for k, (px, py) in enumerate(chips):
            _remote(cw, cw_f.at[2 * px + py], cw_send, cw_recv, k, (px, py, c)).wait_recv()
        for cp in small:
            cp.wait_send()
        local.wait()

    outs = pl.pallas_call(
        body, name="gather_weights",
        in_specs=[ANY] * (n + 1), out_specs=[ANY] * (n + 1),
        out_shape=rider.out_shape + [jax.ShapeDtypeStruct((N_CHIPS,) + conv_w.shape, F32)],
        input_output_aliases=rider.aliases,
        scratch_shapes=rider.scratch + [pltpu.SemaphoreType.DMA((3,)), pltpu.SemaphoreType.DMA((3,)),
                                        pltpu.SemaphoreType.DMA((1,))],
    )(*fulls, conv_w)
    return outs[:n], outs[n]


def _swap_rider(items):
    n = len(items)
    halves = []
    for a, axis in items:
        shp = list(a.shape)
        shp[axis] //= 2
        halves.append(tuple(shp))

    def copies(ins, outs, sems):
        x, y, c = _mesh_position()
        return [_remote(_half(ins[k], items[k][1], 1 - c, halves[k][items[k][1]]), outs[k], sems[0], sems[1], k,
                        (x, y, 1 - c)) for k in range(n)]

    def start(ins, outs, sems):
        for cp in copies(ins, outs, sems):
            cp.start()

    def finish(ins, outs, sems):
        for cp in copies(ins, outs, sems):
            cp.wait()

    return _Rider([a for a, _ in items], [jax.ShapeDtypeStruct(h, a.dtype) for h, (a, _) in zip(halves, items)], {},
                  [pltpu.SemaphoreType.DMA((n,)), pltpu.SemaphoreType.DMA((n,))], start, None, finish)


def _swap_halves(items, name):
    rider = _swap_rider(items)
    n = len(items)

    def body(*refs):
        parts = (refs[:n], refs[n:2 * n], refs[2 * n:])
        rider.start(*parts)
        rider.finish(*parts)

    return pl.pallas_call(
        body, name=name, in_specs=[ANY] * n, out_specs=[ANY] * n, out_shape=rider.out_shape,
        scratch_shapes=rider.scratch,
    )(*rider.inputs)


def _grad_exchange_rider(items):
    n = len(items)
    slices = []
    for a, axis in items:
        shp = list(a.shape)
        shp[axis] //= N_CHIPS
        slices.append(tuple(shp))

    def copies(ins, outs, sems):
        send_sems, recv_sems = sems
        x, y, c = _mesh_position()
        made = []
        for k in range(n):
            axis = items[k][1]
            for r, (px, py) in enumerate(_other_chips(x, y)):
                made.append(_remote(_half(ins[k], axis, 2 * px + py, slices[k][axis]), outs[k].at[r],
                                    send_sems, recv_sems, 3 * k + r, (px, py, c)))
        return made

    def start(ins, outs, sems):
        for cp in copies(ins, outs, sems):
            cp.start()

    def finish(ins, outs, sems):
        for cp in copies(ins, outs, sems):
            cp.wait()

    return _Rider([a for a, _ in items], [jax.ShapeDtypeStruct((N_CHIPS - 1,) + s, BF16) for s in slices], {},
                  [pltpu.SemaphoreType.DMA((3 * n,)), pltpu.SemaphoreType.DMA((3 * n,))], start, None, finish)


def _small_gather_rider(small):
    def copies(ins, outs, sems):
        x, y, c = _mesh_position()
        me = 4 * x + 2 * y + c
        local = pltpu.make_async_copy(ins[0], outs[0].at[me], sems[2].at[0])
        remote = [_remote(ins[0], outs[0].at[me], sems[0], sems[1], r, (x ^ fx, y ^ fy, c ^ fc))
                  for r, (fx, fy, fc) in enumerate(ALL_FLIPS)]
        return local, remote

    def start(ins, outs, sems):
        local, remote = copies(ins, outs, sems)
        local.start()
        for cp in remote:
            cp.start()

    def finish(ins, outs, sems):
        local, remote = copies(ins, outs, sems)
        for cp in remote:
            cp.wait()
        local.wait()

    n = len(ALL_FLIPS)
    return _Rider([small], [jax.ShapeDtypeStruct((2 * N_CHIPS,) + small.shape, F32)], {},
                  [pltpu.SemaphoreType.DMA((n,)), pltpu.SemaphoreType.DMA((n,)), pltpu.SemaphoreType.DMA((1,))],
                  start, None, finish)


def _sum_chips(recv, own, shard_axis, split_axis, pos, dest, layer, name, tr=128):
    sl = recv.shape[1:]
    nd = len(sl)
    tiled = nd == 2 and sl[0] > tr
    nt = sl[0] // tr if tiled else 1
    block = ((tr,) + sl[1:]) if tiled else sl
    shard = list(sl)
    shard[split_axis] *= 2

    def recv_index(i, pos_ref):
        return (0, i) + (0,) * (nd - 1) if tiled else (0,) * (nd + 1)

    def own_index(i, pos_ref):
        idx = [0] * nd
        idx[shard_axis] = pos_ref[1]
        if tiled:
            idx[0] = pos_ref[1] * nt + i if shard_axis == 0 else i
        return tuple(idx)

    def out_index(i, pos_ref):
        idx = [0] * nd
        idx[split_axis] = pos_ref[0]
        if tiled:
            idx[0] = pos_ref[0] * nt + i if split_axis == 0 else i
        return (layer, *idx)

    def body(pos_ref, recv_ref, own_ref, *rest):
        o_ref = rest[-1]
        tot = own_ref[...].astype(F32)
        for k in range(N_CHIPS - 1):
            tot = tot + recv_ref[k].astype(F32)
        o_ref[0] = tot

    in_specs = [pl.BlockSpec((N_CHIPS - 1,) + block, recv_index), pl.BlockSpec(block, own_index)]
    args = [pos, recv, own]
    aliases = {}
    if dest is not None:
        in_specs.append(ANY)
        args.append(dest)
        aliases = {3: 0}
    return pl.pallas_call(
        body, name=name,
        grid_spec=pltpu.PrefetchScalarGridSpec(
            num_scalar_prefetch=1, grid=(nt,), in_specs=in_specs,
            out_specs=pl.BlockSpec((1,) + block, out_index)),
        out_shape=jax.ShapeDtypeStruct((DEPTH,) + tuple(shard), F32),
        input_output_aliases=aliases,
    )(*args)


def _share_halves(bufs, late_small, name):
    n = len(bufs)
    small = _small_gather_rider(late_small)

    def body(*refs):
        outs, (send_sems, recv_sems) = refs[n + 1:2 * n + 1], refs[2 * n + 2:2 * n + 4]
        small_parts = ([refs[n]], [refs[2 * n + 1]], refs[2 * n + 4:])
        x, y, c = _mesh_position()
        small.start(*small_parts)
        copies = []
        for k, (a, axis) in enumerate(bufs):
            size = a.shape[1 + axis] // 2
            mine = _half(outs[k], 1 + axis, c, size)
            copies.append(_remote(mine, mine, send_sems, recv_sems, k, (x, y, 1 - c)))
        for cp in copies:
            cp.start()
        for cp in copies:
            cp.wait()
        small.finish(*small_parts)

    outs = pl.pallas_call(
        body, name=name, in_specs=[ANY] * (n + 1), out_specs=[ANY] * (n + 1),
        out_shape=[jax.ShapeDtypeStruct(a.shape, F32) for a, _ in bufs] + small.out_shape,
        input_output_aliases={k: k for k in range(n)},
        scratch_shapes=[pltpu.SemaphoreType.DMA((n,)), pltpu.SemaphoreType.DMA((n,))] + small.scratch,
    )(*[a for a, _ in bufs], late_small)
    return outs[:n], outs[n]


def _layer_fwd(x, p, l, rider=None, first_layer=None):
    tag = f"l{l}_"
    h = _rms_fwd(x, p["pre_g"], tag + "pre_norm")
    if first_layer is None:
        u = _matmul(h, p["w_in"], "nn", BF16, tag + "in_proj")
    else:
        order, proj_rider = first_layer
        u, w_in, (w_branch, w_out, conv_w_by_chip) = _in_proj_streamed(h, p["w_in"], order, tag + "in_proj", proj_rider)
        conv_w_all = conv_w_by_chip.transpose(1, 2, 0, 3).reshape(DEPTH, 3, WIDTH)
        p = dict(p, w_in=w_in, w_branch=w_branch, w_out=w_out, conv_w=conv_w_all[l], conv_w_all=conv_w_all)
    y_pool = _pool_fwd(u, p["pool_w"], p["pool_scale"], tag + "pool")
    y_conv = _conv_fwd(u, p["conv_w"], p["conv_b"], tag + "conv")
    (o_sb, y_sb, sb_after), carried = _sb_fwd(u, tag + "stickbreak", rider=rider)
    ys = [y_pool, y_conv, y_sb]
    projs, merged = _gate_fwd(u, ys, p["w_branch"], tag + "merge")
    out = _matmul(merged, p["w_out"], "nn", F32, tag + "out_proj")
    saved = dict(x=x, h=h, u=u, ys=ys, o_sb=o_sb, sb_after=sb_after, projs=projs, merged=merged, out=out)
    return out, saved, carried, p


def _layer_bwd(dy, p, saved, l, merge_rider=None, early=None, before_dw=None, late=None):
    tag = f"l{l}_bwd_"
    u = saved["u"]
    d_out, g_post = _rms_bwd(saved["out"], p["post_g"], dy, None, BF16, tag + "post_norm")
    d_merged = _matmul(d_out, p["w_out"], "nt", BF16, tag + "out_proj_dx")
    g_w_out = _matmul(saved["merged"], d_out, "tn", BF16, tag + "out_proj_dw", tk=2048)
    d_projs, d_logits, d_ys, carried_merge = _gate_bwd(u, saved["projs"], d_merged, p["w_branch"], tag + "merge",
                                                       rider=merge_rider)
    g_w_branch = jnp.stack([_matmul(saved["ys"][n], d_projs[n], "tn", BF16, tag + f"branch_dw{n}", tk=2048)
                            for n in range(3)])
    rider = early(g_w_branch, g_w_out, carried_merge) if early else None
    d_pv, d_pg, g_pool_w, g_pool_scale = _pool_bwd(u, d_ys[0], p["pool_w"], p["pool_scale"], tag + "pool")
    d_cx, d_cgb, d_cgc, d_cg, g_conv_w, g_conv_b = _conv_bwd(u, d_ys[1], p["conv_w"], p["conv_b"], tag + "conv")
    (d_q, d_k, d_v, d_sg), carried_attn = _sb_bwd(u, saved["o_sb"], saved["sb_after"], d_ys[2], tag + "stickbreak",
                                                  rider=rider)
    du = jnp.concatenate([d_pv, d_pg, d_cx, d_cgb, d_cgc, d_cg, d_q, d_k, d_v, d_sg] + list(d_logits), axis=1)
    grads = dict(w_branch=g_w_branch, w_out=g_w_out, post_g=g_post, pool_w=g_pool_w, pool_scale=g_pool_scale,
                 conv_w=g_conv_w, conv_b=g_conv_b)
    rider = before_dw(grads) if before_dw else None
    g_w_in = _matmul(saved["h"], du, "tn", BF16, tag + "in_proj_dw", tk=2048, rider=rider)
    g_w_in, carried_dw = g_w_in if rider else (g_w_in, [])
    rider = late(g_w_in) if late else None
    dh = _matmul(du, p["w_in"], "nt", BF16, tag + "in_proj_dx", tk=2048, rider=rider)
    dh, carried_dx = dh if rider else (dh, [])
    dx, g_pre = _rms_bwd(saved["x"], p["pre_g"], dh, dy, F32, tag + "pre_norm")
    grads.update(w_in=g_w_in, pre_g=g_pre)
    return dx, grads, carried_attn, carried_dw, carried_dx


SMALL_ORDER = ["pre_g", "pool_w", "pool_scale", "conv_w", "conv_b", "post_g"]


def _pack_small(per_layer):
    parts, spans, at = [], {}, 0
    for name in SMALL_ORDER:
        a = jnp.stack([per_layer[l][name] for l in range(DEPTH)]).reshape(-1, LANE)
        parts.append(a)
        spans[name] = (at, a.shape[0])
        at += a.shape[0]
    return jnp.concatenate(parts, axis=0), spans


def kernel(x, pre_norm_g, w_in, pool_w, pool_scale, conv_w, conv_b, w_branch, w_out, post_norm_g, loss_target, m_pre_norm_g, m_w_in, m_pool_w, m_pool_scale, m_conv_w, m_conv_b, m_w_branch, m_w_out, m_post_norm_g, v_pre_norm_g, v_w_in, v_pool_w, v_pool_scale, v_conv_w, v_conv_b, v_w_branch, v_w_out, v_post_norm_g):
    mx, my, mc = _mesh_position()
    chip = 2 * mx + my
    core = mc.astype(jnp.int32).reshape(1)
    pos = jnp.stack([mc, chip]).astype(jnp.int32)

    names = ["w_in", "w_branch", "w_out"]
    given = dict(w_in=w_in, w_branch=w_branch, w_out=w_out)
    in_place = [[_cast_shard(given[n], l, WEIGHT_LAYOUT[i][0], pos, f"cast_{n}{l}") for i, n in enumerate(names)]
                for l in range(DEPTH)]
    gathered = in_place[0]
    pool_w_b = _cast_bf16(pool_w, "cast_pool_w")
    per_chip = WEIGHT_LAYOUT[0][1] // 1024
    others = [2 * px + py for px, py in _other_chips(mx, my)]
    order = jnp.stack([per_chip * chip + t_ for t_ in range(per_chip)]
                      + [per_chip * c_ + t_ for t_ in range(per_chip) for c_ in others]).astype(jnp.int32)
    conv_w_f = None

    def layer_params(l, big):
        return dict(pre_g=pre_norm_g[l:l + 1], post_g=post_norm_g[l:l + 1], w_in=big[0], w_branch=big[1],
                    w_out=big[2], pool_w=pool_w_b[l], pool_scale=pool_scale[l:l + 1],
                    conv_w=None if conv_w_f is None else conv_w_f[l], conv_b=conv_b[l:l + 1])

    act = x[0]
    params, saved = [], []
    for l in range(DEPTH):
        rider = _weight_gather_rider(in_place[l + 1], WEIGHT_LAYOUT) if l + 1 < DEPTH else None
        first_layer = None
        if l == 0:
            first_layer = (order, _join_riders(_weight_gather_rider(gathered[1:], WEIGHT_LAYOUT[1:]),
                                               _conv_w_rider(conv_w)))
        out, sv, gathered, layer_p = _layer_fwd(act, layer_params(l, gathered), l, rider, first_layer)
        if l == 0:
            conv_w_f = layer_p["conv_w_all"]
        params.append(layer_p)
        saved.append(sv)
        if l < DEPTH - 1:
            act = _resid_out(act, out, params[l]["post_g"], None, f"l{l}_resid")
    dy, loss_part = _resid_out(act, saved[-1]["out"], params[-1]["post_g"], loss_target[0], "loss_head")

    split_axis = dict(w_in=0, w_branch=1, w_out=1)
    shard_axis = dict(w_in=1, w_branch=2, w_out=0)
    grads = [None] * DEPTH
    chip_sums = [dict() for _ in range(DEPTH)]
    by_chip = [dict() for _ in range(DEPTH)]

    def reduce_in_chip(l, which, g):
        items = [(g[n], split_axis[n]) for n in which]
        from_sibling = _swap_halves(items, f"swap_grad_halves{l}_{which[0]}")
        for n, (a, axis), r in zip(which, items, from_sibling):
            chip_sums[l][n] = _chip_sum(a, r, axis, core, f"chip_sum{l}_{n}")

    def exchange_rider(keys):
        return _grad_exchange_rider([(chip_sums[l][n], shard_axis[n]) for l, n in keys])

    waiting = []
    for l in reversed(range(DEPTH)):
        sent_early, sent_late = list(waiting) + [(l, "w_branch"), (l, "w_out")], [(l, "w_in")]
        waiting_items = [(grads[ll][n], split_axis[n]) for ll, n in waiting]

        def early(g_w_branch, g_w_out, from_sibling, l=l, keys=sent_early, above=tuple(waiting), items=waiting_items):
            for (ll, n), (a, axis), r in zip(above, items, from_sibling):
                chip_sums[ll][n] = _chip_sum(a, r, axis, core, f"chip_sum{ll}_{n}")
            reduce_in_chip(l, ["w_branch", "w_out"], dict(w_branch=g_w_branch, w_out=g_w_out))
            return exchange_rider(keys)

        def late(g_w_in, l=l, keys=sent_late):
            reduce_in_chip(l, ["w_in"], dict(w_in=g_w_in))
            return exchange_rider(keys)

        def before_dw(partial, l=l):
            layers = [dict(partial, pre_g=jnp.zeros_like(pre_norm_g[:1])) if ll == l else grads[ll]
                      for ll in range(DEPTH)]
            return _small_gather_rider(_pack_small(layers)[0])

        if l == DEPTH - 1:
            dy, grads[l], _, _, _ = _layer_bwd(dy, params[l], saved[l], l)
            waiting = [(l, n) for n in names]
        else:
            dy, grads[l], got_early, got_dw, got_late = _layer_bwd(
                dy, params[l], saved[l], l, _swap_rider(waiting_items), early, before_dw if l == 0 else None, late)
            for (ll, n), r in zip(sent_early + sent_late, list(got_early) + list(got_late)):
                by_chip[ll][n] = r
            if l == 0:
                small_all = got_dw[0]
            waiting = []
    assert not waiting
    grad_x = dy[None]
    _, spans = _pack_small(grads)
    late_small = jnp.concatenate([grads[0]["pre_g"].reshape(-1, LANE), jnp.broadcast_to(loss_part, (8, LANE))])
    bufs = []
    for n in names:
        dest = None
        for l in range(DEPTH):
            dest = _sum_chips(by_chip[l][n], chip_sums[l][n], shard_axis[n], split_axis[n], pos, dest, l,
                              f"sum_chips{l}_{n}")
        bufs.append((dest, split_axis[n]))
    (g_w_in, g_w_branch, g_w_out), late_all = _share_halves(bufs, late_small, "share_grad_halves")

    late_sum = _sum_slots(late_all, F32, "sum_late_small")
    n_gain = late_small.shape[0] - 8
    loss = late_sum[n_gain, 0]
    small_sum = _sum_slots(small_all, F32, "sum_small")
    at, _ = spans["pre_g"]
    small_sum = jnp.concatenate([small_sum[:at], late_sum[:n_gain], small_sum[at + n_gain:]])
    small = {}
    for name, like in (("pre_g", pre_norm_g), ("pool_w", pool_w), ("pool_scale", pool_scale), ("conv_b", conv_b),
                       ("post_g", post_norm_g)):
        at, n = spans[name]
        small[name] = small_sum[at:at + n].reshape(like.shape)
    at, n = spans["conv_w"]
    g_conv_w_full = small_sum[at:at + n].reshape(DEPTH, 3, WIDTH)
    g_conv_w = lax.dynamic_slice_in_dim(g_conv_w_full, chip * conv_w.shape[2], conv_w.shape[2], axis=2)

    g = dict(pre_norm_g=small["pre_g"], w_in=g_w_in, pool_w=small["pool_w"], pool_scale=small["pool_scale"],
             conv_w=g_conv_w, conv_b=small["conv_b"], w_branch=g_w_branch, w_out=g_w_out, post_norm_g=small["post_g"])
    w = dict(pre_norm_g=pre_norm_g, w_in=w_in, pool_w=pool_w, pool_scale=pool_scale, conv_w=conv_w, conv_b=conv_b,
             w_branch=w_branch, w_out=w_out, post_norm_g=post_norm_g)
    m = dict(pre_norm_g=m_pre_norm_g, w_in=m_w_in, pool_w=m_pool_w, pool_scale=m_pool_scale, conv_w=m_conv_w,
             conv_b=m_conv_b, w_branch=m_w_branch, w_out=m_w_out, post_norm_g=m_post_norm_g)
    v = dict(pre_norm_g=v_pre_norm_g, w_in=v_w_in, pool_w=v_pool_w, pool_scale=v_pool_scale, conv_w=v_conv_w,
             conv_b=v_conv_b, w_branch=v_w_branch, w_out=v_w_out, post_norm_g=v_post_norm_g)
    order = ["pre_norm_g", "w_in", "pool_w", "pool_scale", "conv_w", "conv_b", "w_branch", "w_out", "post_norm_g"]
    upd = {n: _adamw(w[n], g[n], m[n], v[n], "adamw_" + n) for n in order}
    return (loss, grad_x, *[upd[n][0] for n in order], *[upd[n][1] for n in order], *[upd[n][2] for n in order],
            *[upd[n][3] for n in order])
```
